```python
import math
import jax, jax.numpy as jnp
from jax import lax
import numpy as np

D_MODEL = 1024
BATCH = 4
SEQ = 4096
DEPTH = 1
DEC_BATCH = 128
DEC_SEQ = 1
PAST_LEN = 8192
PAGE_SIZE = 128

D_CONV = D_MODEL
CONV_WIDTH = 3
N_HEADS = 16
N_KV_HEADS = 4
HEAD_DIM = 64
GROUP = N_HEADS // N_KV_HEADS
WINDOW = 128
ATTN_BLOCK = WINDOW
Q_DIM = N_HEADS * HEAD_DIM
KV_DIM = N_KV_HEADS * HEAD_DIM
N_BUCKETS = 32
MAX_DISTANCE = 128
N_EXPERT_GROUPS = 4
EXPERTS_PER_GROUP = 8
N_EXPERTS = N_EXPERT_GROUPS * EXPERTS_PER_GROUP
TOP_K = 2
D_EXPERT = 512
MOE_BLOCK = 128
EPS = 1e-6
IN_OFFSETS = (D_CONV, 2 * D_CONV, 3 * D_CONV, 3 * D_CONV + Q_DIM, 3 * D_CONV + Q_DIM + KV_DIM,
              3 * D_CONV + Q_DIM + 2 * KV_DIM, 3 * D_CONV + Q_DIM + 2 * KV_DIM + D_MODEL)
D_IN = 3 * D_CONV + Q_DIM + 2 * KV_DIM + 2 * D_MODEL

kernel_name = "hybrid_conv_swa_hmoe_decode_step"


def rms_norm(x, g):
    xf = x.astype(jnp.float32)
    y = xf * lax.rsqrt(jnp.mean(xf * xf, axis=-1, keepdims=True) + EPS)
    return (y * g.astype(jnp.float32)).astype(x.dtype)


def t5_bucket(dist):
    n = jnp.maximum(dist, 0)
    max_exact = N_BUCKETS // 2
    nf = jnp.maximum(n, 1).astype(jnp.float32)
    large = max_exact + (jnp.log(nf / max_exact) / math.log(MAX_DISTANCE / max_exact)
                         * (N_BUCKETS - max_exact)).astype(jnp.int32)
    large = jnp.minimum(large, N_BUCKETS - 1)
    return jnp.where(n < max_exact, n, large)


def attend(q, k, v, q_pos, k_pos, rel_bias, sinks):
    lead = q.shape[:-3]
    tq = q.shape[-3]
    tk = k.shape[-3]
    qg = q.reshape(*lead, tq, N_KV_HEADS, GROUP, HEAD_DIM)
    s = jnp.einsum('...qkgd,...skd->...kgqs', qg, k,
                   preferred_element_type=jnp.float32) * (HEAD_DIM ** -0.5)
    dist = q_pos[..., :, None] - k_pos[..., None, :]
    valid = (dist >= 0) & (dist <= WINDOW) & (k_pos[..., None, :] >= 0)
    bias = rel_bias.astype(jnp.float32)[t5_bucket(dist)]
    bias = jnp.moveaxis(bias, -1, -3).reshape(*dist.shape[:-2], N_KV_HEADS, GROUP, tq, tk)
    s = jnp.where(valid[..., None, None, :, :], s + bias, -jnp.inf)
    sink = sinks.astype(jnp.float32).reshape(N_KV_HEADS, GROUP, 1, 1)
    m = jnp.maximum(jnp.max(s, axis=-1, keepdims=True), sink)
    p = jnp.exp(s - m)
    p = p / (jnp.sum(p, axis=-1, keepdims=True) + jnp.exp(sink - m))
    o = jnp.einsum('...kgqs,...skd->...qkgd', p.astype(v.dtype), v)
    return o.reshape(*lead, tq, Q_DIM)


def attn_prompt(q, k, v, rel_bias, sinks):
    b, s = q.shape[:2]
    nb = s // ATTN_BLOCK
    qb = q.reshape(b, nb, ATTN_BLOCK, N_HEADS, HEAD_DIM)

    def band(t):
        t = t.reshape(b, nb, ATTN_BLOCK, N_KV_HEADS, HEAD_DIM)
        prev = jnp.concatenate([jnp.zeros_like(t[:, :1]), t[:, :-1]], axis=1)
        return jnp.concatenate([prev, t], axis=2)

    pos = jnp.arange(s, dtype=jnp.int32).reshape(nb, ATTN_BLOCK)
    k_pos = jnp.concatenate([pos - ATTN_BLOCK, pos], axis=1)
    o = attend(qb, band(k), band(v), pos, k_pos, rel_bias, sinks)
    return o.reshape(b, s, Q_DIM)


def attn_sample(q, k, v, cache_k, cache_v, rel_bias, sinks):
    t = q.shape[1]
    w_buf = cache_k.shape[1]
    k_all = jnp.concatenate([cache_k.astype(k.dtype), k], axis=1)
    v_all = jnp.concatenate([cache_v.astype(v.dtype), v], axis=1)
    q_pos = PAST_LEN + jnp.arange(t, dtype=jnp.int32)
    k_pos = jnp.concatenate([PAST_LEN - w_buf + jnp.arange(w_buf, dtype=jnp.int32), q_pos])
    o = attend(q, k_all, v_all, q_pos, k_pos, rel_bias, sinks)
    return o, k_all[:, -w_buf:], v_all[:, -w_buf:]


def short_conv(u_ext, conv_w):
    t = u_ext.shape[1] - (CONV_WIDTH - 1)
    y = conv_w[0] * u_ext[:, 0:t]
    for j in range(1, CONV_WIDTH):
        y = y + conv_w[j] * u_ext[:, j:j + t]
    return y


def hier_moe(h, w_rg, b_rg, w_re, b_re, w_gate, w_up, w_down):
    lead = h.shape[:-1]
    x = h.reshape(-1, D_MODEL)
    m = x.shape[0]
    g_logits = (x @ w_rg).astype(jnp.float32) + b_rg.astype(jnp.float32)
    g_prob = jax.nn.softmax(g_logits, axis=-1)
    grp = jnp.argmax(g_logits, axis=-1)
    p_grp = jnp.take_along_axis(g_prob, grp[:, None], axis=-1)[:, 0]
    e_logits = ((x @ w_re).astype(jnp.float32) + b_re.astype(jnp.float32)).reshape(
        m, N_EXPERT_GROUPS, EXPERTS_PER_GROUP)
    e_in = jnp.take_along_axis(e_logits, grp[:, None, None], axis=1)[:, 0]
    top_val, top_idx = lax.top_k(e_in, TOP_K)
    w_top = jax.nn.softmax(top_val, axis=-1) * p_grp[:, None]
    expert = grp[:, None] * EXPERTS_PER_GROUP + top_idx
    a = m * TOP_K
    e_flat = expert.reshape(-1)
    t_flat = jnp.repeat(jnp.arange(m, dtype=jnp.int32), TOP_K)
    w_flat = w_top.reshape(-1)
    order = jnp.argsort(e_flat)
    e_s = e_flat[order]
    t_s = t_flat[order]
    counts = jnp.bincount(e_flat, length=N_EXPERTS)
    padded = ((counts + MOE_BLOCK - 1) // MOE_BLOCK) * MOE_BLOCK
    pad_end = jnp.cumsum(padded)
    pad_start = pad_end - padded
    start = jnp.cumsum(counts) - counts
    dest = pad_start[e_s] + (jnp.arange(a, dtype=jnp.int32) - start[e_s])
    n_blocks = -(-a // MOE_BLOCK) + N_EXPERTS
    xs = jnp.zeros((n_blocks * MOE_BLOCK, D_MODEL), x.dtype).at[dest].set(x[t_s])
    block_start = jnp.arange(n_blocks, dtype=jnp.int32) * MOE_BLOCK
    block_expert = jnp.minimum(jnp.searchsorted(pad_end, block_start, side='right'), N_EXPERTS - 1)

    def run_block(args):
        xb, e = args
        return (jax.nn.silu(xb @ w_gate[e]) * (xb @ w_up[e])) @ w_down[e]

    ys = lax.map(run_block, (xs.reshape(n_blocks, MOE_BLOCK, D_MODEL), block_expert))
    ys = ys.reshape(-1, D_MODEL)[dest] * w_flat[order][:, None].astype(x.dtype)
    out = jnp.zeros((m, D_MODEL), x.dtype).at[t_s].add(ys)
    return out.reshape(*lead, D_MODEL)


def decoder_layer(x, conv_hist, attn_fn, norm1_g, w_in, conv_w, w_conv_out, w_attn_out, w_o,
                  norm2_g, w_rg, b_rg, w_re, b_re, w_gate, w_up, w_down):
    b, t = x.shape[:2]
    h = rms_norm(x, norm1_g)
    z = h @ w_in
    cb, cc, ch, q, k, v, ga, gb = jnp.split(z, IN_OFFSETS, axis=-1)
    u_ext = jnp.concatenate([conv_hist.astype(z.dtype), cc * ch], axis=1)
    y_conv = (cb * short_conv(u_ext, conv_w)) @ w_conv_out
    o, k_state, v_state = attn_fn(q.reshape(b, t, N_HEADS, HEAD_DIM),
                                  k.reshape(b, t, N_KV_HEADS, HEAD_DIM),
                                  v.reshape(b, t, N_KV_HEADS, HEAD_DIM))
    y_attn = o @ w_attn_out
    mix = jax.nn.sigmoid(ga) * y_conv + jax.nn.sigmoid(gb) * y_attn
    x = x + mix @ w_o
    x = x + hier_moe(rms_norm(x, norm2_g), w_rg, b_rg, w_re, b_re, w_gate, w_up, w_down)
    return x, u_ext[:, -(CONV_WIDTH - 1):], k_state, v_state


def setup_inputs(seed: int = 0) -> dict:
    key = jax.random.key(seed)
    ks = jax.random.split(key, 24)
    f32 = jnp.float32
    win_buf = min(WINDOW, PAST_LEN)
    nrm = lambda k, shape, scale: jax.random.normal(k, shape, f32) * scale
    return {
        "x_prompt": nrm(ks[0], (BATCH, SEQ, D_MODEL), 1.0),
        "x_sample": nrm(ks[1], (DEC_BATCH, DEC_SEQ, D_MODEL), 1.0),
        "cache_conv": nrm(ks[2], (DEPTH, DEC_BATCH, CONV_WIDTH - 1, D_CONV), 1.0),
        "cache_k": nrm(ks[3], (DEPTH, DEC_BATCH, win_buf, N_KV_HEADS, HEAD_DIM), 1.0),
        "cache_v": nrm(ks[4], (DEPTH, DEC_BATCH, win_buf, N_KV_HEADS, HEAD_DIM), 1.0),
        "norm1_g": 1.0 + nrm(ks[5], (DEPTH, D_MODEL), 0.02),
        "w_in": nrm(ks[6], (DEPTH, D_MODEL, D_IN), D_MODEL ** -0.5),
        "conv_w": nrm(ks[7], (DEPTH, CONV_WIDTH, D_CONV), 0.5),
        "w_conv_out": nrm(ks[8], (DEPTH, D_CONV, D_MODEL), D_CONV ** -0.5),
        "w_attn_out": nrm(ks[9], (DEPTH, Q_DIM, D_MODEL), Q_DIM ** -0.5),
        "w_o": nrm(ks[10], (DEPTH, D_MODEL, D_MODEL), D_MODEL ** -0.5),
        "sinks": nrm(ks[11], (DEPTH, N_HEADS), 0.5),
        "rel_bias": nrm(ks[12], (N_BUCKETS, N_HEADS), 0.5),
        "norm2_g": 1.0 + nrm(ks[13], (DEPTH, D_MODEL), 0.02),
        "w_router_group": nrm(ks[14], (DEPTH, D_MODEL, N_EXPERT_GROUPS), D_MODEL ** -0.5),
        "b_router_group": nrm(ks[15], (DEPTH, N_EXPERT_GROUPS), 0.01),
        "w_router_expert": nrm(ks[16], (DEPTH, D_MODEL, N_EXPERTS), D_MODEL ** -0.5),
        "b_router_expert": nrm(ks[17], (DEPTH, N_EXPERTS), 0.01),
        "w_e_gate": nrm(ks[18], (DEPTH, N_EXPERTS, D_MODEL, D_EXPERT), D_MODEL ** -0.5),
        "w_e_up": nrm(ks[19], (DEPTH, N_EXPERTS, D_MODEL, D_EXPERT), D_MODEL ** -0.5),
        "w_e_down": nrm(ks[20], (DEPTH, N_EXPERTS, D_EXPERT, D_MODEL), D_EXPERT ** -0.5),
        "norm_f_g": 1.0 + nrm(ks[21], (D_MODEL,), 0.02),
    }


def reference(x_prompt, x_sample, cache_conv, cache_k, cache_v, norm1_g, w_in, conv_w,
              w_conv_out, w_attn_out, w_o, sinks, rel_bias, norm2_g, w_router_group,
              b_router_group, w_router_expert, b_router_expert, w_e_gate, w_e_up, w_e_down,
              norm_f_g):
    hp = x_prompt
    hs = x_sample
    win_p = min(WINDOW, x_prompt.shape[1])
    conv_p, kp_l, vp_l, conv_s, ks_l, vs_l = [], [], [], [], [], []
    for l in range(DEPTH):
        weights = (norm1_g[l], w_in[l], conv_w[l], w_conv_out[l], w_attn_out[l], w_o[l],
                   norm2_g[l], w_router_group[l], b_router_group[l], w_router_expert[l],
                   b_router_expert[l], w_e_gate[l], w_e_up[l], w_e_down[l])
        sink_l = sinks[l]

        def prompt_attn(q, k, v, sink_l=sink_l):
            return attn_prompt(q, k, v, rel_bias, sink_l), k[:, -win_p:], v[:, -win_p:]

        def sample_attn(q, k, v, sink_l=sink_l, l=l):
            return attn_sample(q, k, v, cache_k[l], cache_v[l], rel_bias, sink_l)

        zero_hist = jnp.zeros((hp.shape[0], CONV_WIDTH - 1, D_CONV), hp.dtype)
        hp, c_p, k_p, v_p = decoder_layer(hp, zero_hist, prompt_attn, *weights)
        hs, c_s, k_s, v_s = decoder_layer(hs, cache_conv[l], sample_attn, *weights)
        conv_p.append(c_p); kp_l.append(k_p); vp_l.append(v_p)
        conv_s.append(c_s); ks_l.append(k_s); vs_l.append(v_s)
    y_prompt = rms_norm(hp, norm_f_g)
    y_sample = rms_norm(hs, norm_f_g)
    conv_state_prompt = jnp.stack(conv_p)
    k_win_prompt = jnp.stack(kp_l)
    v_win_prompt = jnp.stack(vp_l)
    conv_state_sample = jnp.stack(conv_s)
    k_win_sample = jnp.stack(ks_l)
    v_win_sample = jnp.stack(vs_l)
    return (y_prompt, y_sample, conv_state_prompt, k_win_prompt, v_win_prompt,
            conv_state_sample, k_win_sample, v_win_sample)
```

```python
import functools
import math

import jax
import jax.numpy as jnp
from jax import lax
from jax.experimental import pallas as pl
from jax.experimental.pallas import tpu as pltpu

D_MODEL = 1024
D_CONV = 1024
CONV_WIDTH = 3
N_HEADS = 16
N_KV_HEADS = 4
HEAD_DIM = 64
GROUP = N_HEADS // N_KV_HEADS
WINDOW = 128
Q_DIM = N_HEADS * HEAD_DIM
KV_DIM = N_KV_HEADS * HEAD_DIM
N_BUCKETS = 32
MAX_DISTANCE = 128
N_EXPERT_GROUPS = 4
EXPERTS_PER_GROUP = 8
N_EXPERTS = N_EXPERT_GROUPS * EXPERTS_PER_GROUP
TOP_K = 2
D_EXPERT = 512
EPS = 1e-6
PAST_LEN = 8192

BF16 = jnp.bfloat16
F32 = jnp.float32
NEG_BIG = -1e30

V7X_VMEM_LIMIT_BYTES = 56 * 1024 * 1024
ROUTER_LANES = 128
TM_DENSE = 512
ATTN_BLOCK = 128
MOE_BLOCK = 256
COMBINE_BLOCK = 128
SAMPLE_KEYS = 256
SAMPLE_SEQ_PER_STEP = 8


def _const_spec(shape):
    nd = len(shape)
    return pl.BlockSpec(shape, lambda *_: (0,) * nd, pipeline_mode=pl.Buffered(1))


def _rms_norm_f32(xf, g):
    return xf * lax.rsqrt(jnp.mean(xf * xf, axis=-1, keepdims=True) + EPS) * g


def _in_proj_kernel(*refs, tm, sample, blocks_per_seq, u_tail, kv_tail):
    if sample:
        (x_ref, hist0_ref, hist1_ref, g_ref, wcb_ref, wcc_ref, wch_ref, wq_ref, wkv_ref, wga_ref, wgb_ref,
         cw_ref, yc_ref, q_ref, k_ref, v_ref, sa_ref, sb_ref, ut_ref, kvt_ref) = refs
    else:
        (x_ref, g_ref, wcb_ref, wcc_ref, wch_ref, wq_ref, wkv_ref, wga_ref, wgb_ref,
         cw_ref, yc_ref, q_ref, k_ref, v_ref, sa_ref, sb_ref, ut_ref, kvt_ref, ubuf_ref) = refs

    h = _rms_norm_f32(x_ref[...], g_ref[...]).astype(BF16)

    def proj(w_ref):
        return jnp.dot(h, w_ref[...], preferred_element_type=F32)

    u = proj(wcc_ref) * proj(wch_ref)
    w0 = cw_ref[0:1, :]
    w1 = cw_ref[1:2, :]
    w2 = cw_ref[2:3, :]
    if sample:
        conv = w0 * hist0_ref[...] + w1 * hist1_ref[...] + w2 * u
    else:
        @pl.when(pl.program_id(0) % blocks_per_seq == 0)
        def _():
            ubuf_ref[0:8, :] = jnp.zeros((8, D_CONV), F32)

        ubuf_ref[8:8 + tm, :] = u
        conv = w0 * ubuf_ref[6:6 + tm, :] + w1 * ubuf_ref[7:7 + tm, :] + w2 * u
        ubuf_ref[0:8, :] = u[tm - 8:, :]
    yc_ref[...] = (proj(wcb_ref) * conv).astype(BF16)
    ut_ref[0] = u[tm - u_tail:, :]

    q_ref[...] = (proj(wq_ref) * (HEAD_DIM ** -0.5)).astype(BF16)
    kv = proj(wkv_ref)
    k_ref[...] = kv[:, :KV_DIM].astype(BF16)
    v_ref[...] = kv[:, KV_DIM:].astype(BF16)
    kvt_ref[0] = kv[tm - kv_tail:, :]
    sa_ref[...] = jax.nn.sigmoid(proj(wga_ref)).astype(BF16)
    sb_ref[...] = jax.nn.sigmoid(proj(wgb_ref)).astype(BF16)


def _in_proj(x, g1, w_parts, conv_w, *, tm, blocks_per_seq, u_tail, kv_tail, hist=None):
    m = x.shape[0]
    nblk = m // tm
    sample = hist is not None
    row = lambda width: pl.BlockSpec((tm, width), lambda i: (i, 0))
    in_specs = [row(D_MODEL)]
    args = [x]
    if sample:
        in_specs += [row(D_CONV), row(D_CONV)]
        args += list(hist)
    in_specs += [_const_spec((1, D_MODEL))] + [_const_spec(w.shape) for w in w_parts] + [_const_spec(conv_w.shape)]
    args += [g1] + list(w_parts) + [conv_w]
    out_shape = [
        jax.ShapeDtypeStruct((m, D_CONV), BF16),
        jax.ShapeDtypeStruct((m, Q_DIM), BF16),
        jax.ShapeDtypeStruct((m, KV_DIM), BF16),
        jax.ShapeDtypeStruct((m, KV_DIM), BF16),
        jax.ShapeDtypeStruct((m, D_MODEL), BF16),
        jax.ShapeDtypeStruct((m, D_MODEL), BF16),
        jax.ShapeDtypeStruct((nblk, u_tail, D_CONV), F32),
        jax.ShapeDtypeStruct((nblk, kv_tail, 2 * KV_DIM), F32),
    ]
    out_specs = [row(D_CONV), row(Q_DIM), row(KV_DIM), row(KV_DIM), row(D_MODEL), row(D_MODEL),
                 pl.BlockSpec((1, u_tail, D_CONV), lambda i: (i, 0, 0)),
                 pl.BlockSpec((1, kv_tail, 2 * KV_DIM), lambda i: (i, 0, 0))]
    scratch = [] if sample else [pltpu.VMEM((tm + 8, D_CONV), F32)]
    return pl.pallas_call(
        functools.partial(_in_proj_kernel, tm=tm, sample=sample, blocks_per_seq=blocks_per_seq,
                          u_tail=u_tail, kv_tail=kv_tail),
        grid=(nblk,),
        in_specs=in_specs,
        out_specs=out_specs,
        out_shape=out_shape,
        scratch_shapes=scratch,
        compiler_params=pltpu.CompilerParams(dimension_semantics=("arbitrary",),
                                             vmem_limit_bytes=V7X_VMEM_LIMIT_BYTES),
        name="in_proj_sample" if sample else "in_proj_prompt",
    )(*args)


def _attn_prompt_kernel(sink_ref, q_ref, kc_ref, kp_ref, vc_ref, vp_ref, bias_ref, o_ref):
    first = pl.program_id(1) == 0
    col = lax.broadcasted_iota(jnp.int32, (ATTN_BLOCK, 2 * ATTN_BLOCK), 1)
    no_prev = jnp.logical_and(first, col < ATTN_BLOCK)
    for g in range(N_KV_HEADS):
        ks = slice(g * HEAD_DIM, (g + 1) * HEAD_DIM)
        kcat = jnp.concatenate([kp_ref[:, ks], kc_ref[:, ks]], axis=0)
        vcat = jnp.concatenate([vp_ref[:, ks], vc_ref[:, ks]], axis=0)
        for hh in range(GROUP):
            h = g * GROUP + hh
            hs = slice(h * HEAD_DIM, (h + 1) * HEAD_DIM)
            s = lax.dot_general(q_ref[:, hs], kcat, (((1,), (1,)), ((), ())),
                                preferred_element_type=F32)
            s = jnp.where(no_prev, NEG_BIG, s + bias_ref[h])
            sink = sink_ref[h]
            m = jnp.maximum(jnp.max(s, axis=-1, keepdims=True), sink)
            p = jnp.exp(s - m)
            denom = jnp.sum(p, axis=-1, keepdims=True) + jnp.exp(sink - m)
            o = jnp.dot(p.astype(BF16), vcat, preferred_element_type=F32)
            o_ref[:, hs] = (o / denom).astype(BF16)


def _attn_prompt(q, k, v, bias, sinks, batch, seq):
    nb = seq // ATTN_BLOCK
    cur = lambda b, i: (b * nb + i, 0)
    prev = lambda b, i: (b * nb + jnp.maximum(i - 1, 0), 0)
    return pl.pallas_call(
        _attn_prompt_kernel,
        grid=(batch, nb),
        in_specs=[pl.BlockSpec(memory_space=pltpu.SMEM),
                  pl.BlockSpec((ATTN_BLOCK, Q_DIM), cur),
                  pl.BlockSpec((ATTN_BLOCK, KV_DIM), cur),
                  pl.BlockSpec((ATTN_BLOCK, KV_DIM), prev),
                  pl.BlockSpec((ATTN_BLOCK, KV_DIM), cur),
                  pl.BlockSpec((ATTN_BLOCK, KV_DIM), prev),
                  _const_spec(bias.shape)],
        out_specs=pl.BlockSpec((ATTN_BLOCK, Q_DIM), cur),
        out_shape=jax.ShapeDtypeStruct((batch * seq, Q_DIM), BF16),
        compiler_params=pltpu.CompilerParams(dimension_semantics=("arbitrary", "arbitrary"),
                                             vmem_limit_bytes=V7X_VMEM_LIMIT_BYTES),
        name="attn_prompt",
    )(sinks, q, k, k, v, v, bias)


def _attn_sample_kernel(qbd_ref, ck_ref, cv_ref, kvn_ref, bias_ref, sink_ref, mask_ref, o_ref, *, w_buf):
    pad = jnp.zeros((SAMPLE_KEYS - w_buf, KV_DIM), F32)
    is_new = lax.broadcasted_iota(jnp.int32, (SAMPLE_KEYS, KV_DIM), 0) == w_buf
    for b in range(SAMPLE_SEQ_PER_STEP):
        kall = jnp.where(is_new, kvn_ref[b:b + 1, :KV_DIM],
                         jnp.concatenate([ck_ref[b], pad], axis=0)).astype(BF16)
        vall = jnp.where(is_new, kvn_ref[b:b + 1, KV_DIM:],
                         jnp.concatenate([cv_ref[b], pad], axis=0)).astype(BF16)
        s = lax.dot_general(qbd_ref[b], kall, (((1,), (1,)), ((), ())),
                            preferred_element_type=F32)
        s = s + bias_ref[...]
        sink = sink_ref[...]
        m = jnp.maximum(jnp.max(s, axis=-1, keepdims=True), sink)
        p = jnp.exp(s - m)
        denom = jnp.sum(p, axis=-1, keepdims=True) + jnp.exp(sink - m)
        of = jnp.dot(p.astype(BF16), vall, preferred_element_type=F32) / denom
        of = of * mask_ref[...]
        o_ref[b] = (of[:, 0:HEAD_DIM] + of[:, HEAD_DIM:2 * HEAD_DIM]
                    + of[:, 2 * HEAD_DIM:3 * HEAD_DIM] + of[:, 3 * HEAD_DIM:]).astype(BF16)


def _attn_sample(qbd, ck, cv, kvn, bias, sink_col, head_mask):
    nseq, w_buf = ck.shape[0], ck.shape[1]
    sb = SAMPLE_SEQ_PER_STEP
    return pl.pallas_call(
        functools.partial(_attn_sample_kernel, w_buf=w_buf),
        grid=(nseq // sb,),
        in_specs=[pl.BlockSpec((sb, N_HEADS, KV_DIM), lambda i: (i, 0, 0)),
                  pl.BlockSpec((sb, w_buf, KV_DIM), lambda i: (i, 0, 0)),
                  pl.BlockSpec((sb, w_buf, KV_DIM), lambda i: (i, 0, 0)),
                  pl.BlockSpec((sb, 2 * KV_DIM), lambda i: (i, 0)),
                  _const_spec(bias.shape), _const_spec(sink_col.shape), _const_spec(head_mask.shape)],
        out_specs=pl.BlockSpec((sb, N_HEADS, HEAD_DIM), lambda i: (i, 0, 0)),
        out_shape=jax.ShapeDtypeStruct((nseq, N_HEADS, HEAD_DIM), BF16),
        compiler_params=pltpu.CompilerParams(dimension_semantics=("arbitrary",),
                                             vmem_limit_bytes=V7X_VMEM_LIMIT_BYTES),
        name="attn_sample",
    )(qbd, ck, cv, kvn, bias, sink_col, head_mask)


def _out_proj_rows(yc_ref, o_ref, sa_ref, sb_ref, x_ref, wc_ref, wa_ref, wo_ref, g2_ref, wr_ref, br_ref,
                   x2_ref, h2_ref, lg_ref):
    y_conv = jnp.dot(yc_ref[...], wc_ref[...], preferred_element_type=F32)
    y_attn = jnp.dot(o_ref[...], wa_ref[...], preferred_element_type=F32)
    mix = (sa_ref[...].astype(F32) * y_conv + sb_ref[...].astype(F32) * y_attn).astype(BF16)
    x2 = x_ref[...] + jnp.dot(mix, wo_ref[...], preferred_element_type=F32)
    x2_ref[...] = x2
    h2 = _rms_norm_f32(x2, g2_ref[...])
    h2_ref[...] = h2
    lg_ref[...] = jnp.dot(h2.astype(BF16), wr_ref[...], preferred_element_type=F32) + br_ref[...]


def _out_proj_kernel(*refs, n_first):
    first, second, shared = refs[0:5], refs[5:10], refs[10:]

    @pl.when(pl.program_id(0) < n_first)
    def _():
        _out_proj_rows(*first, *shared)

    @pl.when(pl.program_id(0) >= n_first)
    def _():
        _out_proj_rows(*second, *shared)


def _out_proj(acts_a, acts_b, wc, wa, wo, g2, wr, br, *, tm):
    na = acts_a[4].shape[0] // tm
    nb = acts_b[4].shape[0] // tm
    m_total = (na + nb) * tm
    spec_a = lambda width: pl.BlockSpec((tm, width), lambda i: (jnp.minimum(i, na - 1), 0))
    spec_b = lambda width: pl.BlockSpec((tm, width), lambda i: (jnp.maximum(i - na, 0), 0))
    widths = (D_CONV, Q_DIM, D_MODEL, D_MODEL, D_MODEL)
    in_specs = [spec_a(w) for w in widths] + [spec_b(w) for w in widths]
    in_specs += [_const_spec(wc.shape), _const_spec(wa.shape), _const_spec(wo.shape),
                 _const_spec(g2.shape), _const_spec(wr.shape), _const_spec(br.shape)]
    orow = lambda width: pl.BlockSpec((tm, width), lambda i: (i, 0))
    return pl.pallas_call(
        functools.partial(_out_proj_kernel, n_first=na),
        grid=(na + nb,),
        in_specs=in_specs,
        out_specs=[orow(D_MODEL), orow(D_MODEL), orow(ROUTER_LANES)],
        out_shape=[jax.ShapeDtypeStruct((m_total, D_MODEL), F32),
                   jax.ShapeDtypeStruct((m_total, D_MODEL), F32),
                   jax.ShapeDtypeStruct((m_total, ROUTER_LANES), F32)],
        compiler_params=pltpu.CompilerParams(dimension_semantics=("arbitrary",),
                                             vmem_limit_bytes=V7X_VMEM_LIMIT_BYTES),
        name="out_proj",
    )(*acts_a, *acts_b, wc, wa, wo, g2, wr, br)


def _moe_row_copy(h2_hbm, xs_ref, sem_ref, tok_ref, blk, slot, r):
    tok = tok_ref[blk * MOE_BLOCK + r]
    return pltpu.make_async_copy(h2_hbm.at[pl.ds(tok, 1), :], xs_ref.at[slot, pl.ds(r, 1), :], sem_ref.at[slot])


def _moe_kernel(bexp_ref, nused_ref, tok_ref, h2_hbm, wg_ref, wu_ref, wd_ref, ys_ref,
                xs_ref, wgb_ref, wub_ref, wdb_ref, sem_ref):
    i = pl.program_id(0)
    slot = i % 2
    n_used = nused_ref[0]

    def start(blk, slt):
        for r in range(MOE_BLOCK):
            _moe_row_copy(h2_hbm, xs_ref, sem_ref, tok_ref, blk, slt, r).start()

    @pl.when(jnp.logical_and(i == 0, n_used > 0))
    def _():
        start(0, 0)

    @pl.when(i + 1 < n_used)
    def _():
        start(i + 1, 1 - slot)

    prev_e = bexp_ref[jnp.maximum(i - 1, 0)]

    @pl.when(jnp.logical_or(i == 0, bexp_ref[i] != prev_e))
    def _():
        wgb_ref[...] = wg_ref[0].astype(BF16)
        wub_ref[...] = wu_ref[0].astype(BF16)
        wdb_ref[...] = wd_ref[0].astype(BF16)

    @pl.when(i < n_used)
    def _():
        for r in range(MOE_BLOCK):
            _moe_row_copy(h2_hbm, xs_ref, sem_ref, tok_ref, i, slot, r).wait()
        xb = xs_ref[slot].astype(BF16)
        gate = jnp.dot(xb, wgb_ref[...], preferred_element_type=F32)
        up = jnp.dot(xb, wub_ref[...], preferred_element_type=F32)
        hmid = (jax.nn.silu(gate) * up).astype(BF16)
        ys_ref[...] = jnp.dot(hmid, wdb_ref[...], preferred_element_type=F32)

    @pl.when(i >= n_used)
    def _():
        ys_ref[...] = jnp.zeros_like(ys_ref)


def _moe_experts(block_expert, n_used, tok_of_slot, h2, w_gate, w_up, w_down):
    n_blocks = block_expert.shape[0]
    wspec = lambda shape: pl.BlockSpec((1,) + shape, lambda i, be, nu, tok: (be[i], 0, 0))
    grid_spec = pltpu.PrefetchScalarGridSpec(
        num_scalar_prefetch=3,
        grid=(n_blocks,),
        in_specs=[pl.BlockSpec(memory_space=pl.ANY),
                  wspec((D_MODEL, D_EXPERT)), wspec((D_MODEL, D_EXPERT)), wspec((D_EXPERT, D_MODEL))],
        out_specs=pl.BlockSpec((MOE_BLOCK, D_MODEL), lambda i, be, nu, tok: (i, 0)),
        scratch_shapes=[pltpu.VMEM((2, MOE_BLOCK, D_MODEL), F32),
                        pltpu.VMEM((D_MODEL, D_EXPERT), BF16),
                        pltpu.VMEM((D_MODEL, D_EXPERT), BF16),
                        pltpu.VMEM((D_EXPERT, D_MODEL), BF16),
                        pltpu.SemaphoreType.DMA((2,))],
    )
    return pl.pallas_call(
        _moe_kernel,
        grid_spec=grid_spec,
        out_shape=jax.ShapeDtypeStruct((n_blocks * MOE_BLOCK, D_MODEL), F32),
        compiler_params=pltpu.CompilerParams(dimension_semantics=("arbitrary",),
                                             vmem_limit_bytes=V7X_VMEM_LIMIT_BYTES),
        name="moe_experts",
    )(block_expert, n_used, tok_of_slot, h2, w_gate, w_up, w_down)


def _combine_row_copy(ys_hbm, gbuf_ref, sem_ref, dest_ref, blk, slot, kk, r, row_off):
    src = dest_ref[(row_off + blk * COMBINE_BLOCK + r) * TOP_K + kk]
    return pltpu.make_async_copy(ys_hbm.at[pl.ds(src, 1), :], gbuf_ref.at[slot, kk, pl.ds(r, 1), :],
                                 sem_ref.at[slot])


def _combine_kernel(dest_ref, ys_hbm, x2_ref, w_ref, gf_ref, y_ref, gbuf_ref, sem_ref, *, row_off, nblk):
    i = pl.program_id(0)
    slot = i % 2

    def copies(blk, slt):
        return [_combine_row_copy(ys_hbm, gbuf_ref, sem_ref, dest_ref, blk, slt, kk, r, row_off)
                for kk in range(TOP_K) for r in range(COMBINE_BLOCK)]

    @pl.when(i == 0)
    def _():
        for c in copies(0, 0):
            c.start()

    @pl.when(i + 1 < nblk)
    def _():
        for c in copies(i + 1, 1 - slot):
            c.start()

    for c in copies(i, slot):
        c.wait()
    w = w_ref[...]
    moe = w[:, 0:1] * gbuf_ref[slot, 0] + w[:, 1:2] * gbuf_ref[slot, 1]
    y_ref[...] = _rms_norm_f32(x2_ref[...] + moe, gf_ref[...])


def _moe_combine(dest_flat, ys, x2, w_top, gf, *, row_off, m):
    nblk = m // COMBINE_BLOCK
    off = row_off // COMBINE_BLOCK
    grid_spec = pltpu.PrefetchScalarGridSpec(
        num_scalar_prefetch=1,
        grid=(nblk,),
        in_specs=[pl.BlockSpec(memory_space=pl.ANY),
                  pl.BlockSpec((COMBINE_BLOCK, D_MODEL), lambda i, d: (i + off, 0)),
                  pl.BlockSpec((COMBINE_BLOCK, TOP_K), lambda i, d: (i + off, 0)),
                  pl.BlockSpec((1, D_MODEL), lambda i, d: (0, 0))],
        out_specs=pl.BlockSpec((COMBINE_BLOCK, D_MODEL), lambda i, d: (i, 0)),
        scratch_shapes=[pltpu.VMEM((2, TOP_K, COMBINE_BLOCK, D_MODEL), F32),
                        pltpu.SemaphoreType.DMA((2,))],
    )
    return pl.pallas_call(
        functools.partial(_combine_kernel, row_off=row_off, nblk=nblk),
        grid_spec=grid_spec,
        out_shape=jax.ShapeDtypeStruct((m, D_MODEL), F32),
        compiler_params=pltpu.CompilerParams(dimension_semantics=("arbitrary",),
                                             vmem_limit_bytes=V7X_VMEM_LIMIT_BYTES),
        name="moe_combine_prompt" if row_off == 0 else "moe_combine_sample",
    )(dest_flat, ys, x2, w_top, gf)


def _t5_bucket(dist):
    n = jnp.maximum(dist, 0)
    max_exact = N_BUCKETS // 2
    nf = jnp.maximum(n, 1).astype(F32)
    large = max_exact + (jnp.log(nf / max_exact) / math.log(MAX_DISTANCE / max_exact)
                         * (N_BUCKETS - max_exact)).astype(jnp.int32)
    large = jnp.minimum(large, N_BUCKETS - 1)
    return jnp.where(n < max_exact, n, large)


def _prompt_bias_table(rel_bias):
    qi = jnp.arange(ATTN_BLOCK, dtype=jnp.int32)[:, None]
    kj = jnp.arange(2 * ATTN_BLOCK, dtype=jnp.int32)[None, :] - ATTN_BLOCK
    dist = qi - kj
    valid = (dist >= 0) & (dist <= WINDOW)
    bias = rel_bias.astype(F32)[_t5_bucket(dist)]
    bias = jnp.where(valid[..., None], bias, NEG_BIG)
    return jnp.moveaxis(bias, -1, 0)


def _sample_bias_table(rel_bias, w_buf):
    j = jnp.arange(SAMPLE_KEYS, dtype=jnp.int32)
    dist = jnp.where(j < w_buf, w_buf - j, 0)
    valid = (j <= w_buf) & (dist <= WINDOW)
    bias = rel_bias.astype(F32)[_t5_bucket(dist)]
    return jnp.where(valid[:, None], bias, NEG_BIG).T


def _route(logits):
    m = logits.shape[0]
    g_logits = logits[:, :N_EXPERT_GROUPS]
    g_prob = jax.nn.softmax(g_logits, axis=-1)
    grp = jnp.argmax(g_logits, axis=-1)
    p_grp = jnp.take_along_axis(g_prob, grp[:, None], axis=-1)[:, 0]
    e_logits = logits[:, N_EXPERT_GROUPS:N_EXPERT_GROUPS + N_EXPERTS].reshape(
        m, N_EXPERT_GROUPS, EXPERTS_PER_GROUP)
    e_in = jnp.take_along_axis(e_logits, grp[:, None, None], axis=1)[:, 0]
    top_val, top_idx = lax.top_k(e_in, TOP_K)
    w_top = jax.nn.softmax(top_val, axis=-1) * p_grp[:, None]
    expert = (grp[:, None] * EXPERTS_PER_GROUP + top_idx).astype(jnp.int32)
    a = m * TOP_K
    e_flat = expert.reshape(-1)
    onehot = (e_flat[:, None] == jnp.arange(N_EXPERTS, dtype=jnp.int32)[None, :]).astype(jnp.int32)
    csum = jnp.cumsum(onehot, axis=0)
    rank = jnp.take_along_axis(csum, e_flat[:, None], axis=1)[:, 0] - 1
    counts = csum[-1]
    padded = ((counts + MOE_BLOCK - 1) // MOE_BLOCK) * MOE_BLOCK
    pad_end = jnp.cumsum(padded)
    pad_start = pad_end - padded
    dest = (pad_start[e_flat] + rank).astype(jnp.int32)
    n_blocks = -(-a // MOE_BLOCK) + N_EXPERTS
    tok_of_slot = jnp.zeros((n_blocks * MOE_BLOCK,), jnp.int32).at[dest].set(
        jnp.arange(a, dtype=jnp.int32) // TOP_K)
    block_start = jnp.arange(n_blocks, dtype=jnp.int32) * MOE_BLOCK
    block_expert = jnp.minimum(jnp.searchsorted(pad_end, block_start, side='right'),
                               N_EXPERTS - 1).astype(jnp.int32)
    n_used = (pad_end[-1] // MOE_BLOCK).astype(jnp.int32).reshape(1)
    return w_top, dest, tok_of_slot, block_expert, n_used


def kernel(x_prompt, x_sample, cache_conv, cache_k, cache_v, norm1_g, w_in, conv_w, w_conv_out, w_attn_out, w_o, sinks, rel_bias, norm2_g, w_router_group, b_router_group, w_router_expert, b_router_expert, w_e_gate, w_e_up, w_e_down, norm_f_g):
    assert norm1_g.shape[0] == 1, "single-layer configuration"
    batch, seq, _ = x_prompt.shape
    nseq = x_sample.shape[0]
    w_buf = cache_k.shape[2]
    mp = batch * seq
    m_total = mp + nseq
    assert seq % TM_DENSE == 0 and seq % ATTN_BLOCK == 0 and mp % COMBINE_BLOCK == 0
    assert nseq == COMBINE_BLOCK and nseq % SAMPLE_SEQ_PER_STEP == 0 and w_buf + 1 <= SAMPLE_KEYS

    g1 = norm1_g[0][None, :]
    g2 = norm2_g[0][None, :]
    gf = norm_f_g[None, :]
    wi = w_in[0].astype(BF16)
    o0 = 0
    w_parts = []
    for width in (D_CONV, D_CONV, D_CONV, Q_DIM, 2 * KV_DIM, D_MODEL, D_MODEL):
        w_parts.append(wi[:, o0:o0 + width])
        o0 += width
    cw = conv_w[0]
    wc = w_conv_out[0].astype(BF16)
    wa = w_attn_out[0].astype(BF16)
    wo = w_o[0].astype(BF16)
    pad_cols = ROUTER_LANES - N_EXPERT_GROUPS - N_EXPERTS
    wr = jnp.concatenate([w_router_group[0], w_router_expert[0],
                          jnp.zeros((D_MODEL, pad_cols), F32)], axis=1).astype(BF16)
    br = jnp.concatenate([b_router_group[0], b_router_expert[0], jnp.zeros((pad_cols,), F32)])[None, :]
    sink = sinks[0].astype(F32)

    xp = x_prompt.reshape(mp, D_MODEL)
    bps = seq // TM_DENSE
    yc_p, q_p, k_p, v_p, sa_p, sb_p, ut_p, kvt_p = _in_proj(
        xp, g1, w_parts, cw, tm=TM_DENSE, blocks_per_seq=bps, u_tail=8, kv_tail=WINDOW)
    o_p = _attn_prompt(q_p, k_p, v_p, _prompt_bias_table(rel_bias), sink, batch, seq)

    pad_rows = lambda t: jnp.pad(t, ((0, TM_DENSE - nseq), (0, 0)))
    xs = pad_rows(x_sample.reshape(nseq, D_MODEL))
    hist = (pad_rows(cache_conv[0][:, 0, :]), pad_rows(cache_conv[0][:, 1, :]))
    yc_s, q_s, _, _, sa_s, sb_s, ut_s, kvt_s = _in_proj(
        xs, g1, w_parts, cw, tm=TM_DENSE, blocks_per_seq=1, u_tail=TM_DENSE, kv_tail=TM_DENSE, hist=hist)
    u_s = ut_s[0, :nseq]
    kv_s = kvt_s[0, :nseq]
    head_mask = (jnp.arange(KV_DIM)[None, :] // HEAD_DIM == jnp.arange(N_HEADS)[:, None] // GROUP)
    qbd = (jnp.tile(q_s[:nseq].reshape(nseq, N_HEADS, HEAD_DIM), (1, 1, N_KV_HEADS))
           * head_mask[None].astype(BF16))
    o_s = _attn_sample(qbd, cache_k[0].reshape(nseq, w_buf, KV_DIM), cache_v[0].reshape(nseq, w_buf, KV_DIM),
                       kv_s, _sample_bias_table(rel_bias, w_buf), sink[:, None], head_mask.astype(F32))
    o_s = pad_rows(o_s.reshape(nseq, Q_DIM))

    x2, h2, logits = _out_proj((yc_p, o_p, sa_p, sb_p, xp), (yc_s, o_s, sa_s, sb_s, xs),
                               wc, wa, wo, g2, wr, br, tm=TM_DENSE)

    w_top, dest, tok_of_slot, block_expert, n_used = _route(logits[:m_total])
    ys = _moe_experts(block_expert, n_used, tok_of_slot, h2, w_e_gate[0], w_e_up[0], w_e_down[0])
    y_p = _moe_combine(dest, ys, x2, w_top, gf, row_off=0, m=mp)
    y_s = _moe_combine(dest, ys, x2, w_top, gf, row_off=mp, m=nseq)

    y_prompt = y_p.reshape(batch, seq, D_MODEL)
    y_sample = y_s.reshape(nseq, 1, D_MODEL)
    last = jnp.arange(batch) * bps + (bps - 1)
    conv_state_prompt = ut_p[last][:, 8 - (CONV_WIDTH - 1):, :][None]
    kv_last = kvt_p[last]
    k_win_prompt = kv_last[:, :, :KV_DIM].reshape(batch, WINDOW, N_KV_HEADS, HEAD_DIM)[None]
    v_win_prompt = kv_last[:, :, KV_DIM:].reshape(batch, WINDOW, N_KV_HEADS, HEAD_DIM)[None]
    conv_state_sample = jnp.concatenate([cache_conv[0][:, 1:, :], u_s[:, None, :]], axis=1)[None]
    k_new = kv_s[:, :KV_DIM].reshape(nseq, 1, N_KV_HEADS, HEAD_DIM)
    v_new = kv_s[:, KV_DIM:].reshape(nseq, 1, N_KV_HEADS, HEAD_DIM)
    k_win_sample = jnp.concatenate([cache_k[0], k_new], axis=1)[:, -w_buf:][None]
    v_win_sample = jnp.concatenate([cache_v[0], v_new], axis=1)[:, -w_buf:][None]
    return (y_prompt, y_sample, conv_state_prompt, k_win_prompt, v_win_prompt,
            conv_state_sample, k_win_sample, v_win_sample)
```

```python
import functools
import math

import jax
import jax.numpy as jnp
from jax import lax
from jax.experimental import pallas as pl
from jax.experimental.pallas import tpu as pltpu

D_MODEL = 1024
D_CONV = 1024
CONV_WIDTH = 3
N_HEADS = 16
N_KV_HEADS = 4
HEAD_DIM = 64
GROUP = N_HEADS // N_KV_HEADS
WINDOW = 128
Q_DIM = N_HEADS * HEAD_DIM
KV_DIM = N_KV_HEADS * HEAD_DIM
N_BUCKETS = 32
MAX_DISTANCE = 128
N_EXPERT_GROUPS = 4
EXPERTS_PER_GROUP = 8
N_EXPERTS = N_EXPERT_GROUPS * EXPERTS_PER_GROUP
TOP_K = 2
D_EXPERT = 512
EPS = 1e-6
PAST_LEN = 8192

BF16 = jnp.bfloat16
F32 = jnp.float32
NEG_BIG = -1e30

V7X_VMEM_LIMIT_BYTES = 56 * 1024 * 1024
ROUTER_LANES = 128
TM_DENSE = 512
ATTN_BLOCK = 128
MOE_BLOCK = 256
MOE_BLOCK_LOG2 = 8
ROUTE_CHUNK = 256
RANK_BITS = 16
COMBINE_BLOCK = 128
SAMPLE_KEYS = 256
SAMPLE_SEQ_PER_STEP = 8


def _const_spec(shape):
    nd = len(shape)
    return pl.BlockSpec(shape, lambda *_: (0,) * nd, pipeline_mode=pl.Buffered(1))


def _rms_norm_f32(xf, g):
    return xf * lax.rsqrt(jnp.mean(xf * xf, axis=-1, keepdims=True) + EPS) * g


def _in_proj_kernel(*refs, tm, sample, blocks_per_seq, u_tail, kv_tail):
    if sample:
        (x_ref, hist0_ref, hist1_ref, g_ref, wcb_ref, wcc_ref, wch_ref, wq_ref, wkv_ref, wga_ref, wgb_ref,
         cw_ref, yc_ref, q_ref, k_ref, v_ref, sa_ref, sb_ref, ut_ref, kvt_ref) = refs
    else:
        (x_ref, g_ref, wcb_ref, wcc_ref, wch_ref, wq_ref, wkv_ref, wga_ref, wgb_ref,
         cw_ref, yc_ref, q_ref, k_ref, v_ref, sa_ref, sb_ref, ut_ref, kvt_ref, ubuf_ref) = refs

    h = _rms_norm_f32(x_ref[...], g_ref[...]).astype(BF16)

    def proj(w_ref):
        return jnp.dot(h, w_ref[...], preferred_element_type=F32)

    u = proj(wcc_ref) * proj(wch_ref)
    w0 = cw_ref[0:1, :]
    w1 = cw_ref[1:2, :]
    w2 = cw_ref[2:3, :]
    if sample:
        conv = w0 * hist0_ref[...] + w1 * hist1_ref[...] + w2 * u
    else:
        @pl.when(pl.program_id(0) % blocks_per_seq == 0)
        def _():
            ubuf_ref[0:8, :] = jnp.zeros((8, D_CONV), F32)

        ubuf_ref[8:8 + tm, :] = u
        conv = w0 * ubuf_ref[6:6 + tm, :] + w1 * ubuf_ref[7:7 + tm, :] + w2 * u
        ubuf_ref[0:8, :] = u[tm - 8:, :]
    yc_ref[...] = (proj(wcb_ref) * conv).astype(BF16)
    ut_ref[0] = u[tm - u_tail:, :]

    q_ref[...] = (proj(wq_ref) * (HEAD_DIM ** -0.5)).astype(BF16)
    kv = proj(wkv_ref)
    k_ref[...] = kv[:, :KV_DIM].astype(BF16)
    v_ref[...] = kv[:, KV_DIM:].astype(BF16)
    kvt_ref[0] = kv[tm - kv_tail:, :]
    sa_ref[...] = jax.nn.sigmoid(proj(wga_ref)).astype(BF16)
    sb_ref[...] = jax.nn.sigmoid(proj(wgb_ref)).astype(BF16)


def _in_proj(x, g1, w_parts, conv_w, *, tm, blocks_per_seq, u_tail, kv_tail, hist=None):
    m = x.shape[0]
    nblk = m // tm
    sample = hist is not None
    row = lambda width: pl.BlockSpec((tm, width), lambda i: (i, 0))
    in_specs = [row(D_MODEL)]
    args = [x]
    if sample:
        in_specs += [row(D_CONV), row(D_CONV)]
        args += list(hist)
    in_specs += [_const_spec((1, D_MODEL))] + [_const_spec(w.shape) for w in w_parts] + [_const_spec(conv_w.shape)]
    args += [g1] + list(w_parts) + [conv_w]
    out_shape = [
        jax.ShapeDtypeStruct((m, D_CONV), BF16),
        jax.ShapeDtypeStruct((m, Q_DIM), BF16),
        jax.ShapeDtypeStruct((m, KV_DIM), BF16),
        jax.ShapeDtypeStruct((m, KV_DIM), BF16),
        jax.ShapeDtypeStruct((m, D_MODEL), BF16),
        jax.ShapeDtypeStruct((m, D_MODEL), BF16),
        jax.ShapeDtypeStruct((nblk, u_tail, D_CONV), F32),
        jax.ShapeDtypeStruct((nblk, kv_tail, 2 * KV_DIM), F32),
    ]
    out_specs = [row(D_CONV), row(Q_DIM), row(KV_DIM), row(KV_DIM), row(D_MODEL), row(D_MODEL),
                 pl.BlockSpec((1, u_tail, D_CONV), lambda i: (i, 0, 0)),
                 pl.BlockSpec((1, kv_tail, 2 * KV_DIM), lambda i: (i, 0, 0))]
    scratch = [] if sample else [pltpu.VMEM((tm + 8, D_CONV), F32)]
    return pl.pallas_call(
        functools.partial(_in_proj_kernel, tm=tm, sample=sample, blocks_per_seq=blocks_per_seq,
                          u_tail=u_tail, kv_tail=kv_tail),
        grid=(nblk,),
        in_specs=in_specs,
        out_specs=out_specs,
        out_shape=out_shape,
        scratch_shapes=scratch,
        compiler_params=pltpu.CompilerParams(dimension_semantics=("arbitrary",),
                                             vmem_limit_bytes=V7X_VMEM_LIMIT_BYTES),
        name="in_proj_sample" if sample else "in_proj_prompt",
    )(*args)


def _attn_prompt_kernel(sink_ref, q_ref, kc_ref, kp_ref, vc_ref, vp_ref, bias_ref, o_ref):
    first = pl.program_id(1) == 0
    col = lax.broadcasted_iota(jnp.int32, (ATTN_BLOCK, 2 * ATTN_BLOCK), 1)
    no_prev = jnp.logical_and(first, col < ATTN_BLOCK)
    for g in range(N_KV_HEADS):
        ks = slice(g * HEAD_DIM, (g + 1) * HEAD_DIM)
        kcat = jnp.concatenate([kp_ref[:, ks], kc_ref[:, ks]], axis=0)
        vcat = jnp.concatenate([vp_ref[:, ks], vc_ref[:, ks]], axis=0)
        for hh in range(GROUP):
            h = g * GROUP + hh
            hs = slice(h * HEAD_DIM, (h + 1) * HEAD_DIM)
            s = lax.dot_general(q_ref[:, hs], kcat, (((1,), (1,)), ((), ())),
                                preferred_element_type=F32)
            s = jnp.where(no_prev, NEG_BIG, s + bias_ref[h])
            sink = sink_ref[h]
            m = jnp.maximum(jnp.max(s, axis=-1, keepdims=True), sink)
            p = jnp.exp(s - m)
            denom = jnp.sum(p, axis=-1, keepdims=True) + jnp.exp(sink - m)
            o = jnp.dot(p.astype(BF16), vcat, preferred_element_type=F32)
            o_ref[:, hs] = (o / denom).astype(BF16)


def _attn_prompt(q, k, v, bias, sinks, batch, seq):
    nb = seq // ATTN_BLOCK
    cur = lambda b, i: (b * nb + i, 0)
    prev = lambda b, i: (b * nb + jnp.maximum(i - 1, 0), 0)
    return pl.pallas_call(
        _attn_prompt_kernel,
        grid=(batch, nb),
        in_specs=[pl.BlockSpec(memory_space=pltpu.SMEM),
                  pl.BlockSpec((ATTN_BLOCK, Q_DIM), cur),
                  pl.BlockSpec((ATTN_BLOCK, KV_DIM), cur),
                  pl.BlockSpec((ATTN_BLOCK, KV_DIM), prev),
                  pl.BlockSpec((ATTN_BLOCK, KV_DIM), cur),
                  pl.BlockSpec((ATTN_BLOCK, KV_DIM), prev),
                  _const_spec(bias.shape)],
        out_specs=pl.BlockSpec((ATTN_BLOCK, Q_DIM), cur),
        out_shape=jax.ShapeDtypeStruct((batch * seq, Q_DIM), BF16),
        compiler_params=pltpu.CompilerParams(dimension_semantics=("arbitrary", "arbitrary"),
                                             vmem_limit_bytes=V7X_VMEM_LIMIT_BYTES),
        name="attn_prompt",
    )(sinks, q, k, k, v, v, bias)


def _attn_sample_kernel(qbd_ref, ck_ref, cv_ref, kvn_ref, bias_ref, sink_ref, mask_ref, o_ref, *, w_buf):
    pad = jnp.zeros((SAMPLE_KEYS - w_buf, KV_DIM), F32)
    is_new = lax.broadcasted_iota(jnp.int32, (SAMPLE_KEYS, KV_DIM), 0) == w_buf
    for b in range(SAMPLE_SEQ_PER_STEP):
        kall = jnp.where(is_new, kvn_ref[b:b + 1, :KV_DIM],
                         jnp.concatenate([ck_ref[b], pad], axis=0)).astype(BF16)
        vall = jnp.where(is_new, kvn_ref[b:b + 1, KV_DIM:],
                         jnp.concatenate([cv_ref[b], pad], axis=0)).astype(BF16)
        s = lax.dot_general(qbd_ref[b], kall, (((1,), (1,)), ((), ())),
                            preferred_element_type=F32)
        s = s + bias_ref[...]
        sink = sink_ref[...]
        m = jnp.maximum(jnp.max(s, axis=-1, keepdims=True), sink)
        p = jnp.exp(s - m)
        denom = jnp.sum(p, axis=-1, keepdims=True) + jnp.exp(sink - m)
        of = jnp.dot(p.astype(BF16), vall, preferred_element_type=F32) / denom
        of = of * mask_ref[...]
        o_ref[b] = (of[:, 0:HEAD_DIM] + of[:, HEAD_DIM:2 * HEAD_DIM]
                    + of[:, 2 * HEAD_DIM:3 * HEAD_DIM] + of[:, 3 * HEAD_DIM:]).astype(BF16)


def _attn_sample(qbd, ck, cv, kvn, bias, sink_col, head_mask):
    nseq, w_buf = ck.shape[0], ck.shape[1]
    sb = SAMPLE_SEQ_PER_STEP
    return pl.pallas_call(
        functools.partial(_attn_sample_kernel, w_buf=w_buf),
        grid=(nseq // sb,),
        in_specs=[pl.BlockSpec((sb, N_HEADS, KV_DIM), lambda i: (i, 0, 0)),
                  pl.BlockSpec((sb, w_buf, KV_DIM), lambda i: (i, 0, 0)),
                  pl.BlockSpec((sb, w_buf, KV_DIM), lambda i: (i, 0, 0)),
                  pl.BlockSpec((sb, 2 * KV_DIM), lambda i: (i, 0)),
                  _const_spec(bias.shape), _const_spec(sink_col.shape), _const_spec(head_mask.shape)],
        out_specs=pl.BlockSpec((sb, N_HEADS, HEAD_DIM), lambda i: (i, 0, 0)),
        out_shape=jax.ShapeDtypeStruct((nseq, N_HEADS, HEAD_DIM), BF16),
        compiler_params=pltpu.CompilerParams(dimension_semantics=("arbitrary",),
                                             vmem_limit_bytes=V7X_VMEM_LIMIT_BYTES),
        name="attn_sample",
    )(qbd, ck, cv, kvn, bias, sink_col, head_mask)


def _route_rows(logits, base, valid_rows):
    tm = logits.shape[0]
    lane = lax.broadcasted_iota(jnp.int32, logits.shape, 1)
    lane_f = lane.astype(F32)
    no_lane = float(ROUTER_LANES)

    def top1(mask):
        best = jnp.max(jnp.where(mask, logits, -jnp.inf), axis=-1, keepdims=True)
        idx = jnp.min(jnp.where(jnp.logical_and(mask, logits == best), lane_f, no_lane), axis=-1, keepdims=True)
        return best, idx

    gmask = lane < N_EXPERT_GROUPS
    gmax, grp = top1(gmask)
    gsum = jnp.sum(jnp.where(gmask, jnp.exp(logits - gmax), 0.0), axis=-1, keepdims=True)
    p_grp = 1.0 / gsum
    lo = N_EXPERT_GROUPS + EXPERTS_PER_GROUP * grp
    emask = jnp.logical_and(lane_f >= lo, lane_f < lo + EXPERTS_PER_GROUP)
    v1, i1 = top1(emask)
    v2, i2 = top1(jnp.logical_and(emask, lane_f != i1))
    e21 = jnp.exp(v2 - v1)
    w1 = p_grp / (1.0 + e21)
    w2 = p_grp * e21 / (1.0 + e21)

    oh1 = lane_f == i1
    oh2 = lane_f == i2
    if valid_rows < tm:
        valid = lax.broadcasted_iota(jnp.int32, logits.shape, 0) < valid_rows
        oh1 = jnp.logical_and(oh1, valid)
        oh2 = jnp.logical_and(oh2, valid)
    oh = oh1.astype(F32) + oh2.astype(F32)
    before = (lax.broadcasted_iota(jnp.int32, (tm, tm), 0) > lax.broadcasted_iota(jnp.int32, (tm, tm), 1))
    seen = jnp.dot(before.astype(BF16), oh.astype(BF16), preferred_element_type=F32) + base
    r1 = jnp.sum(jnp.where(oh1, seen, 0.0), axis=-1, keepdims=True)
    r2 = jnp.sum(jnp.where(oh2, seen, 0.0), axis=-1, keepdims=True)
    key1 = (i1.astype(jnp.int32) - N_EXPERT_GROUPS) * (1 << RANK_BITS) + r1.astype(jnp.int32)
    key2 = (i2.astype(jnp.int32) - N_EXPERT_GROUPS) * (1 << RANK_BITS) + r2.astype(jnp.int32)
    w1b = lax.bitcast_convert_type(w1, jnp.int32)
    w2b = lax.bitcast_convert_type(w2, jnp.int32)
    words = jnp.where(lane == 0, key1, jnp.where(lane == 1, key2, jnp.where(lane == 2, w1b,
                      jnp.where(lane == 3, w2b, 0))))
    return words, base + jnp.sum(oh, axis=0, keepdims=True)


def _out_proj_rows(yc_ref, o_ref, sa_ref, sb_ref, x_ref, wc_ref, wa_ref, wo_ref, g2_ref, wr_ref, br_ref,
                   x2_ref, h2_ref, route_ref, cnt_ref, *, valid_rows):
    y_conv = jnp.dot(yc_ref[...], wc_ref[...], preferred_element_type=F32)
    y_attn = jnp.dot(o_ref[...], wa_ref[...], preferred_element_type=F32)
    mix = (sa_ref[...].astype(F32) * y_conv + sb_ref[...].astype(F32) * y_attn).astype(BF16)
    x2 = x_ref[...] + jnp.dot(mix, wo_ref[...], preferred_element_type=F32)
    x2_ref[...] = x2
    h2 = _rms_norm_f32(x2, g2_ref[...])
    h2_ref[...] = h2
    logits = jnp.dot(h2.astype(BF16), wr_ref[...], preferred_element_type=F32) + br_ref[...]
    words, cnt = _route_rows(logits, cnt_ref[...], valid_rows)
    route_ref[...] = words
    cnt_ref[...] = cnt


def _out_proj_kernel(*refs, n_first, valid_rows_second):
    first, second, shared = refs[0:5], refs[5:10], refs[10:]
    cnt_ref = shared[-1]
    tm = first[4].shape[0]

    @pl.when(pl.program_id(0) == 0)
    def _():
        cnt_ref[...] = jnp.zeros_like(cnt_ref)

    @pl.when(pl.program_id(0) < n_first)
    def _():
        _out_proj_rows(*first, *shared, valid_rows=tm)

    @pl.when(pl.program_id(0) >= n_first)
    def _():
        _out_proj_rows(*second, *shared, valid_rows=valid_rows_second)


def _out_proj(acts_a, acts_b, wc, wa, wo, g2, wr, br, *, tm, valid_rows_b):
    na = acts_a[4].shape[0] // tm
    nb = acts_b[4].shape[0] // tm
    assert nb == 1
    m_total = (na + nb) * tm
    spec_a = lambda width: pl.BlockSpec((tm, width), lambda i: (jnp.minimum(i, na - 1), 0))
    spec_b = lambda width: pl.BlockSpec((tm, width), lambda i: (jnp.maximum(i - na, 0), 0))
    widths = (D_CONV, Q_DIM, D_MODEL, D_MODEL, D_MODEL)
    in_specs = [spec_a(w) for w in widths] + [spec_b(w) for w in widths]
    in_specs += [_const_spec(wc.shape), _const_spec(wa.shape), _const_spec(wo.shape),
                 _const_spec(g2.shape), _const_spec(wr.shape), _const_spec(br.shape)]
    orow = lambda width: pl.BlockSpec((tm, width), lambda i: (i, 0))
    return pl.pallas_call(
        functools.partial(_out_proj_kernel, n_first=na, valid_rows_second=valid_rows_b),
        grid=(na + nb,),
        in_specs=in_specs,
        out_specs=[orow(D_MODEL), orow(D_MODEL), orow(ROUTER_LANES),
                   pl.BlockSpec((1, ROUTER_LANES), lambda i: (0, 0))],
        out_shape=[jax.ShapeDtypeStruct((m_total, D_MODEL), F32),
                   jax.ShapeDtypeStruct((m_total, D_MODEL), F32),
                   jax.ShapeDtypeStruct((m_total, ROUTER_LANES), jnp.int32),
                   jax.ShapeDtypeStruct((1, ROUTER_LANES), F32)],
        compiler_params=pltpu.CompilerParams(dimension_semantics=("arbitrary",),
                                             vmem_limit_bytes=V7X_VMEM_LIMIT_BYTES),
        name="out_proj",
    )(*acts_a, *acts_b, wc, wa, wo, g2, wr, br)


def _route_index_kernel(keys_ref, counts_ref, dest_ref, tok_ref, bexp_ref, nused_ref, pstart_ref, *, n_blocks):
    @pl.when(pl.program_id(0) == 0)
    def _():
        def per_expert(e, blk0):
            cnt = counts_ref[e]
            nblk = lax.shift_right_logical(cnt + (MOE_BLOCK - 1), MOE_BLOCK_LOG2)
            pstart_ref[e] = blk0 * MOE_BLOCK

            def mark(b, c):
                bexp_ref[b] = e
                return c
            lax.fori_loop(blk0, blk0 + nblk, mark, 0)

            def clear(s, c):
                tok_ref[s] = 0
                return c
            lax.fori_loop(blk0 * MOE_BLOCK + cnt, (blk0 + nblk) * MOE_BLOCK, clear, 0)
            return blk0 + nblk

        n_used = lax.fori_loop(0, N_EXPERTS, per_expert, 0)
        nused_ref[0] = n_used

        def unused_block(b, c):
            bexp_ref[b] = N_EXPERTS - 1
            return c
        lax.fori_loop(n_used, n_blocks, unused_block, 0)

        def unused_slot(s, c):
            tok_ref[s] = 0
            return c
        lax.fori_loop(n_used * MOE_BLOCK, n_blocks * MOE_BLOCK, unused_slot, 0)

    base = pl.program_id(0) * ROUTE_CHUNK

    def place(jj, c):
        j = base + jj
        key = keys_ref[j]
        d = pstart_ref[lax.shift_right_logical(key, RANK_BITS)] + (key & ((1 << RANK_BITS) - 1))
        dest_ref[j] = d
        tok_ref[d] = lax.shift_right_logical(j, 1)
        return c
    lax.fori_loop(0, ROUTE_CHUNK, place, 0, unroll=8)


def _route_index(keys, counts, n_blocks):
    n_assign = keys.shape[0]
    assert n_assign % ROUTE_CHUNK == 0
    smem = pl.BlockSpec(memory_space=pltpu.SMEM)
    dest, tok, bexp, nused, _ = pl.pallas_call(
        functools.partial(_route_index_kernel, n_blocks=n_blocks),
        grid=(n_assign // ROUTE_CHUNK,),
        in_specs=[smem, smem],
        out_specs=[smem, smem, smem, smem, smem],
        out_shape=[jax.ShapeDtypeStruct((n_assign,), jnp.int32),
                   jax.ShapeDtypeStruct((n_blocks * MOE_BLOCK,), jnp.int32),
                   jax.ShapeDtypeStruct((n_blocks,), jnp.int32),
                   jax.ShapeDtypeStruct((1,), jnp.int32),
                   jax.ShapeDtypeStruct((N_EXPERTS,), jnp.int32)],
        compiler_params=pltpu.CompilerParams(dimension_semantics=("arbitrary",)),
        name="route_index",
    )(keys, counts)
    return dest, tok, bexp, nused


def _moe_row_copy(h2_hbm, xs_ref, sem_ref, tok_ref, blk, slot, r):
    tok = tok_ref[blk * MOE_BLOCK + r]
    return pltpu.make_async_copy(h2_hbm.at[pl.ds(tok, 1), :], xs_ref.at[slot, pl.ds(r, 1), :], sem_ref.at[slot])


def _moe_kernel(bexp_ref, nused_ref, tok_ref, h2_hbm, wg_ref, wu_ref, wd_ref, ys_ref,
                xs_ref, wgb_ref, wub_ref, wdb_ref, sem_ref):
    i = pl.program_id(0)
    slot = i % 2
    n_used = nused_ref[0]

    def start(blk, slt):
        for r in range(MOE_BLOCK):
            _moe_row_copy(h2_hbm, xs_ref, sem_ref, tok_ref, blk, slt, r).start(priority=r % 2)

    @pl.when(jnp.logical_and(i == 0, n_used > 0))
    def _():
        start(0, 0)

    @pl.when(i + 1 < n_used)
    def _():
        start(i + 1, 1 - slot)

    prev_e = bexp_ref[jnp.maximum(i - 1, 0)]

    @pl.when(jnp.logical_or(i == 0, bexp_ref[i] != prev_e))
    def _():
        wgb_ref[...] = wg_ref[0].astype(BF16)
        wub_ref[...] = wu_ref[0].astype(BF16)
        wdb_ref[...] = wd_ref[0].astype(BF16)

    @pl.when(i < n_used)
    def _():
        for r in range(MOE_BLOCK):
            _moe_row_copy(h2_hbm, xs_ref, sem_ref, tok_ref, i, slot, r).wait()
        xb = xs_ref[slot].astype(BF16)
        gate = jnp.dot(xb, wgb_ref[...], preferred_element_type=F32)
        up = jnp.dot(xb, wub_ref[...], preferred_element_type=F32)
        hmid = (jax.nn.silu(gate) * up).astype(BF16)
        ys_ref[...] = jnp.dot(hmid, wdb_ref[...], preferred_element_type=F32)

    @pl.when(i >= n_used)
    def _():
        ys_ref[...] = jnp.zeros_like(ys_ref)


def _moe_experts(block_expert, n_used, tok_of_slot, h2, w_gate, w_up, w_down):
    n_blocks = block_expert.shape[0]
    wspec = lambda shape: pl.BlockSpec((1,) + shape, lambda i, be, nu, tok: (be[i], 0, 0))
    grid_spec = pltpu.PrefetchScalarGridSpec(
        num_scalar_prefetch=3,
        grid=(n_blocks,),
        in_specs=[pl.BlockSpec(memory_space=pl.ANY),
                  wspec((D_MODEL, D_EXPERT)), wspec((D_MODEL, D_EXPERT)), wspec((D_EXPERT, D_MODEL))],
        out_specs=pl.BlockSpec((MOE_BLOCK, D_MODEL), lambda i, be, nu, tok: (i, 0)),
        scratch_shapes=[pltpu.VMEM((2, MOE_BLOCK, D_MODEL), F32),
                        pltpu.VMEM((D_MODEL, D_EXPERT), BF16),
                        pltpu.VMEM((D_MODEL, D_EXPERT), BF16),
                        pltpu.VMEM((D_EXPERT, D_MODEL), BF16),
                        pltpu.SemaphoreType.DMA((2,))],
    )
    return pl.pallas_call(
        _moe_kernel,
        grid_spec=grid_spec,
        out_shape=jax.ShapeDtypeStruct((n_blocks * MOE_BLOCK, D_MODEL), F32),
        compiler_params=pltpu.CompilerParams(dimension_semantics=("arbitrary",),
                                             vmem_limit_bytes=V7X_VMEM_LIMIT_BYTES),
        name="moe_experts",
    )(block_expert, n_used, tok_of_slot, h2, w_gate, w_up, w_down)


def _combine_row_copy(ys_hbm, gbuf_ref, sem_ref, dest_ref, blk, slot, kk, r, row_off):
    src = dest_ref[(row_off + blk * COMBINE_BLOCK + r) * TOP_K + kk]
    return pltpu.make_async_copy(ys_hbm.at[pl.ds(src, 1), :], gbuf_ref.at[slot, kk, pl.ds(r, 1), :],
                                 sem_ref.at[slot])


def _combine_kernel(dest_ref, ys_hbm, x2_ref, w_ref, gf_ref, y_ref, gbuf_ref, sem_ref, *, row_off, nblk):
    i = pl.program_id(0)
    slot = i % 2

    def copies(blk, slt):
        return [_combine_row_copy(ys_hbm, gbuf_ref, sem_ref, dest_ref, blk, slt, kk, r, row_off)
                for kk in range(TOP_K) for r in range(COMBINE_BLOCK)]

    @pl.when(i == 0)
    def _():
        for n, c in enumerate(copies(0, 0)):
            c.start(priority=n % 2)

    @pl.when(i + 1 < nblk)
    def _():
        for n, c in enumerate(copies(i + 1, 1 - slot)):
            c.start(priority=n % 2)

    for c in copies(i, slot):
        c.wait()
    w = w_ref[...]
    moe = w[:, 0:1] * gbuf_ref[slot, 0] + w[:, 1:2] * gbuf_ref[slot, 1]
    y_ref[...] = _rms_norm_f32(x2_ref[...] + moe, gf_ref[...])


def _moe_combine(dest_flat, ys, x2, w_top, gf, *, row_off, m):
    nblk = m // COMBINE_BLOCK
    off = row_off // COMBINE_BLOCK
    grid_spec = pltpu.PrefetchScalarGridSpec(
        num_scalar_prefetch=1,
        grid=(nblk,),
        in_specs=[pl.BlockSpec(memory_space=pl.ANY),
                  pl.BlockSpec((COMBINE_BLOCK, D_MODEL), lambda i, d: (i + off, 0)),
                  pl.BlockSpec((COMBINE_BLOCK, TOP_K), lambda i, d: (i + off, 0)),
                  pl.BlockSpec((1, D_MODEL), lambda i, d: (0, 0))],
        out_specs=pl.BlockSpec((COMBINE_BLOCK, D_MODEL), lambda i, d: (i, 0)),
        scratch_shapes=[pltpu.VMEM((2, TOP_K, COMBINE_BLOCK, D_MODEL), F32),
                        pltpu.SemaphoreType.DMA((2,))],
    )
    return pl.pallas_call(
        functools.partial(_combine_kernel, row_off=row_off, nblk=nblk),
        grid_spec=grid_spec,
        out_shape=jax.ShapeDtypeStruct((m, D_MODEL), F32),
        compiler_params=pltpu.CompilerParams(dimension_semantics=("arbitrary",),
                                             vmem_limit_bytes=V7X_VMEM_LIMIT_BYTES),
        name="moe_combine_prompt" if row_off == 0 else "moe_combine_sample",
    )(dest_flat, ys, x2, w_top, gf)


def _t5_bucket(dist):
    n = jnp.maximum(dist, 0)
    max_exact = N_BUCKETS // 2
    nf = jnp.maximum(n, 1).astype(F32)
    large = max_exact + (jnp.log(nf / max_exact) / math.log(MAX_DISTANCE / max_exact)
                         * (N_BUCKETS - max_exact)).astype(jnp.int32)
    large = jnp.minimum(large, N_BUCKETS - 1)
    return jnp.where(n < max_exact, n, large)


def _prompt_bias_table(rel_bias):
    qi = jnp.arange(ATTN_BLOCK, dtype=jnp.int32)[:, None]
    kj = jnp.arange(2 * ATTN_BLOCK, dtype=jnp.int32)[None, :] - ATTN_BLOCK
    dist = qi - kj
    valid = (dist >= 0) & (dist <= WINDOW)
    bias = rel_bias.astype(F32)[_t5_bucket(dist)]
    bias = jnp.where(valid[..., None], bias, NEG_BIG)
    return jnp.moveaxis(bias, -1, 0)


def _sample_bias_table(rel_bias, w_buf):
    j = jnp.arange(SAMPLE_KEYS, dtype=jnp.int32)
    dist = jnp.where(j < w_buf, w_buf - j, 0)
    valid = (j <= w_buf) & (dist <= WINDOW)
    bias = rel_bias.astype(F32)[_t5_bucket(dist)]
    return jnp.where(valid[:, None], bias, NEG_BIG).T


def kernel(x_prompt, x_sample, cache_conv, cache_k, cache_v, norm1_g, w_in, conv_w, w_conv_out, w_attn_out, w_o, sinks, rel_bias, norm2_g, w_router_group, b_router_group, w_router_expert, b_router_expert, w_e_gate, w_e_up, w_e_down, norm_f_g):
    assert norm1_g.shape[0] == 1, "single-layer configuration"
    batch, seq, _ = x_prompt.shape
    nseq = x_sample.shape[0]
    w_buf = cache_k.shape[2]
    mp = batch * seq
    m_total = mp + nseq
    assert seq % TM_DENSE == 0 and seq % ATTN_BLOCK == 0 and mp % COMBINE_BLOCK == 0
    assert nseq == COMBINE_BLOCK and nseq % SAMPLE_SEQ_PER_STEP == 0 and w_buf + 1 <= SAMPLE_KEYS
    assert TOP_K == 2 and MOE_BLOCK == 1 << MOE_BLOCK_LOG2 and m_total * TOP_K < 1 << RANK_BITS

    g1 = norm1_g[0][None, :]
    g2 = norm2_g[0][None, :]
    gf = norm_f_g[None, :]
    wi = w_in[0].astype(BF16)
    o0 = 0
    w_parts = []
    for width in (D_CONV, D_CONV, D_CONV, Q_DIM, 2 * KV_DIM, D_MODEL, D_MODEL):
        w_parts.append(wi[:, o0:o0 + width])
        o0 += width
    cw = conv_w[0]
    wc = w_conv_out[0].astype(BF16)
    wa = w_attn_out[0].astype(BF16)
    wo = w_o[0].astype(BF16)
    pad_cols = ROUTER_LANES - N_EXPERT_GROUPS - N_EXPERTS
    wr = jnp.concatenate([w_router_group[0], w_router_expert[0],
                          jnp.zeros((D_MODEL, pad_cols), F32)], axis=1).astype(BF16)
    br = jnp.concatenate([b_router_group[0], b_router_expert[0], jnp.zeros((pad_cols,), F32)])[None, :]
    sink = sinks[0].astype(F32)

    xp = x_prompt.reshape(mp, D_MODEL)
    bps = seq // TM_DENSE
    yc_p, q_p, k_p, v_p, sa_p, sb_p, ut_p, kvt_p = _in_proj(
        xp, g1, w_parts, cw, tm=TM_DENSE, blocks_per_seq=bps, u_tail=8, kv_tail=WINDOW)
    o_p = _attn_prompt(q_p, k_p, v_p, _prompt_bias_table(rel_bias), sink, batch, seq)

    pad_rows = lambda t: jnp.pad(t, ((0, TM_DENSE - nseq), (0, 0)))
    xs = pad_rows(x_sample.reshape(nseq, D_MODEL))
    hist = (pad_rows(cache_conv[0][:, 0, :]), pad_rows(cache_conv[0][:, 1, :]))
    yc_s, q_s, _, _, sa_s, sb_s, ut_s, kvt_s = _in_proj(
        xs, g1, w_parts, cw, tm=TM_DENSE, blocks_per_seq=1, u_tail=TM_DENSE, kv_tail=TM_DENSE, hist=hist)
    u_s = ut_s[0, :nseq]
    kv_s = kvt_s[0, :nseq]
    head_mask = (jnp.arange(KV_DIM)[None, :] // HEAD_DIM == jnp.arange(N_HEADS)[:, None] // GROUP)
    qbd = (jnp.tile(q_s[:nseq].reshape(nseq, N_HEADS, HEAD_DIM), (1, 1, N_KV_HEADS))
           * head_mask[None].astype(BF16))
    o_s = _attn_sample(qbd, cache_k[0].reshape(nseq, w_buf, KV_DIM), cache_v[0].reshape(nseq, w_buf, KV_DIM),
                       kv_s, _sample_bias_table(rel_bias, w_buf), sink[:, None], head_mask.astype(F32))
    o_s = pad_rows(o_s.reshape(nseq, Q_DIM))

    x2, h2, route, cnt = _out_proj((yc_p, o_p, sa_p, sb_p, xp), (yc_s, o_s, sa_s, sb_s, xs),
                                   wc, wa, wo, g2, wr, br, tm=TM_DENSE, valid_rows_b=nseq)

    keys = route[:m_total, 0:TOP_K].reshape(-1)
    w_top = lax.bitcast_convert_type(route[:m_total, TOP_K:2 * TOP_K], F32)
    counts = cnt[0, N_EXPERT_GROUPS:N_EXPERT_GROUPS + N_EXPERTS].astype(jnp.int32)
    n_blocks = -(-(m_total * TOP_K) // MOE_BLOCK) + N_EXPERTS
    dest, tok_of_slot, block_expert, n_used = _route_index(keys, counts, n_blocks)
    ys = _moe_experts(block_expert, n_used, tok_of_slot, h2, w_e_gate[0], w_e_up[0], w_e_down[0])
    y_p = _moe_combine(dest, ys, x2, w_top, gf, row_off=0, m=mp)
    y_s = _moe_combine(dest, ys, x2, w_top, gf, row_off=mp, m=nseq)

    y_prompt = y_p.reshape(batch, seq, D_MODEL)
    y_sample = y_s.reshape(nseq, 1, D_MODEL)
    last = jnp.arange(batch) * bps + (bps - 1)
    conv_state_prompt = ut_p[last][:, 8 - (CONV_WIDTH - 1):, :][None]
    kv_last = kvt_p[last]
    k_win_prompt = kv_last[:, :, :KV_DIM].reshape(batch, WINDOW, N_KV_HEADS, HEAD_DIM)[None]
    v_win_prompt = kv_last[:, :, KV_DIM:].reshape(batch, WINDOW, N_KV_HEADS, HEAD_DIM)[None]
    conv_state_sample = jnp.concatenate([cache_conv[0][:, 1:, :], u_s[:, None, :]], axis=1)[None]
    k_new = kv_s[:, :KV_DIM].reshape(nseq, 1, N_KV_HEADS, HEAD_DIM)
    v_new = kv_s[:, KV_DIM:].reshape(nseq, 1, N_KV_HEADS, HEAD_DIM)
    k_win_sample = jnp.concatenate([cache_k[0], k_new], axis=1)[:, -w_buf:][None]
    v_win_sample = jnp.concatenate([cache_v[0], v_new], axis=1)[:, -w_buf:][None]
    return (y_prompt, y_sample, conv_state_prompt, k_win_prompt, v_win_prompt,
            conv_state_sample, k_win_sample, v_win_sample)
```

```python
import functools
import math

import jax
import jax.numpy as jnp
from jax import lax
from jax.experimental import pallas as pl
from jax.experimental.pallas import tpu as pltpu

D_MODEL = 1024
D_CONV = 1024
CONV_WIDTH = 3
N_HEADS = 16
N_KV_HEADS = 4
HEAD_DIM = 64
GROUP = N_HEADS // N_KV_HEADS
WINDOW = 128
Q_DIM = N_HEADS * HEAD_DIM
KV_DIM = N_KV_HEADS * HEAD_DIM
N_BUCKETS = 32
MAX_DISTANCE = 128
N_EXPERT_GROUPS = 4
EXPERTS_PER_GROUP = 8
N_EXPERTS = N_EXPERT_GROUPS * EXPERTS_PER_GROUP
TOP_K = 2
D_EXPERT = 512
EPS = 1e-6
PAST_LEN = 8192

BF16 = jnp.bfloat16
F32 = jnp.float32
NEG_BIG = -1e30

V7X_VMEM_LIMIT_BYTES = 56 * 1024 * 1024
ROUTER_LANES = 128
TM_DENSE = 512
ATTN_BLOCK = 128
MOE_BLOCK = 256
MOE_BLOCK_LOG2 = 8
ASSIGN_BITS = 16
COMBINE_BLOCK = 256
SAMPLE_KEYS = 256
SAMPLE_SEQ_PER_STEP = 8


def _const_spec(shape):
    nd = len(shape)
    return pl.BlockSpec(shape, lambda *_: (0,) * nd, pipeline_mode=pl.Buffered(1))


def _rms_norm_f32(xf, g):
    return xf * lax.rsqrt(jnp.mean(xf * xf, axis=-1, keepdims=True) + EPS) * g


def _in_proj_kernel(*refs, tm, sample, blocks_per_seq, u_tail, kv_tail):
    if sample:
        (x_ref, hist0_ref, hist1_ref, g_ref, wcb_ref, wcc_ref, wch_ref, wq_ref, wkv_ref, wga_ref, wgb_ref,
         cw_ref, yc_ref, q_ref, k_ref, v_ref, sa_ref, sb_ref, ut_ref, kvt_ref) = refs
    else:
        (x_ref, g_ref, wcb_ref, wcc_ref, wch_ref, wq_ref, wkv_ref, wga_ref, wgb_ref,
         cw_ref, yc_ref, q_ref, k_ref, v_ref, sa_ref, sb_ref, ut_ref, kvt_ref, ubuf_ref) = refs

    h = _rms_norm_f32(x_ref[...], g_ref[...]).astype(BF16)

    def proj(w_ref):
        return jnp.dot(h, w_ref[...], preferred_element_type=F32)

    u = proj(wcc_ref) * proj(wch_ref)
    w0 = cw_ref[0:1, :]
    w1 = cw_ref[1:2, :]
    w2 = cw_ref[2:3, :]
    if sample:
        conv = w0 * hist0_ref[...] + w1 * hist1_ref[...] + w2 * u
    else:
        @pl.when(pl.program_id(0) % blocks_per_seq == 0)
        def _():
            ubuf_ref[0:8, :] = jnp.zeros((8, D_CONV), F32)

        ubuf_ref[8:8 + tm, :] = u
        conv = w0 * ubuf_ref[6:6 + tm, :] + w1 * ubuf_ref[7:7 + tm, :] + w2 * u
        ubuf_ref[0:8, :] = u[tm - 8:, :]
    yc_ref[...] = (proj(wcb_ref) * conv).astype(BF16)
    ut_ref[0] = u[tm - u_tail:, :]

    q_ref[...] = (proj(wq_ref) * (HEAD_DIM ** -0.5)).astype(BF16)
    kv = proj(wkv_ref)
    k_ref[...] = kv[:, :KV_DIM].astype(BF16)
    v_ref[...] = kv[:, KV_DIM:].astype(BF16)
    kvt_ref[0] = kv[tm - kv_tail:, :]
    sa_ref[...] = jax.nn.sigmoid(proj(wga_ref)).astype(BF16)
    sb_ref[...] = jax.nn.sigmoid(proj(wgb_ref)).astype(BF16)


def _in_proj(x, g1, w_parts, conv_w, *, tm, blocks_per_seq, u_tail, kv_tail, hist=None):
    m = x.shape[0]
    nblk = m // tm
    sample = hist is not None
    row = lambda width: pl.BlockSpec((tm, width), lambda i: (i, 0))
    in_specs = [row(D_MODEL)]
    args = [x]
    if sample:
        in_specs += [row(D_CONV), row(D_CONV)]
        args += list(hist)
    in_specs += [_const_spec((1, D_MODEL))] + [_const_spec(w.shape) for w in w_parts] + [_const_spec(conv_w.shape)]
    args += [g1] + list(w_parts) + [conv_w]
    out_shape = [
        jax.ShapeDtypeStruct((m, D_CONV), BF16),
        jax.ShapeDtypeStruct((m, Q_DIM), BF16),
        jax.ShapeDtypeStruct((m, KV_DIM), BF16),
        jax.ShapeDtypeStruct((m, KV_DIM), BF16),
        jax.ShapeDtypeStruct((m, D_MODEL), BF16),
        jax.ShapeDtypeStruct((m, D_MODEL), BF16),
        jax.ShapeDtypeStruct((nblk, u_tail, D_CONV), F32),
        jax.ShapeDtypeStruct((nblk, kv_tail, 2 * KV_DIM), F32),
    ]
    out_specs = [row(D_CONV), row(Q_DIM), row(KV_DIM), row(KV_DIM), row(D_MODEL), row(D_MODEL),
                 pl.BlockSpec((1, u_tail, D_CONV), lambda i: (i, 0, 0)),
                 pl.BlockSpec((1, kv_tail, 2 * KV_DIM), lambda i: (i, 0, 0))]
    scratch = [] if sample else [pltpu.VMEM((tm + 8, D_CONV), F32)]
    return pl.pallas_call(
        functools.partial(_in_proj_kernel, tm=tm, sample=sample, blocks_per_seq=blocks_per_seq,
                          u_tail=u_tail, kv_tail=kv_tail),
        grid=(nblk,),
        in_specs=in_specs,
        out_specs=out_specs,
        out_shape=out_shape,
        scratch_shapes=scratch,
        compiler_params=pltpu.CompilerParams(dimension_semantics=("arbitrary",),
                                             vmem_limit_bytes=V7X_VMEM_LIMIT_BYTES),
        name="in_proj_sample" if sample else "in_proj_prompt",
    )(*args)


def _attn_prompt_kernel(sink_ref, q_ref, kc_ref, kp_ref, vc_ref, vp_ref, bias_ref, o_ref):
    first = pl.program_id(1) == 0
    col = lax.broadcasted_iota(jnp.int32, (ATTN_BLOCK, 2 * ATTN_BLOCK), 1)
    no_prev = jnp.logical_and(first, col < ATTN_BLOCK)
    for g in range(N_KV_HEADS):
        ks = slice(g * HEAD_DIM, (g + 1) * HEAD_DIM)
        kcat = jnp.concatenate([kp_ref[:, ks], kc_ref[:, ks]], axis=0)
        vcat = jnp.concatenate([vp_ref[:, ks], vc_ref[:, ks]], axis=0)
        for hh in range(GROUP):
            h = g * GROUP + hh
            hs = slice(h * HEAD_DIM, (h + 1) * HEAD_DIM)
            s = lax.dot_general(q_ref[:, hs], kcat, (((1,), (1,)), ((), ())),
                                preferred_element_type=F32)
            s = jnp.where(no_prev, NEG_BIG, s + bias_ref[h])
            sink = sink_ref[h]
            m = jnp.maximum(jnp.max(s, axis=-1, keepdims=True), sink)
            p = jnp.exp(s - m)
            denom = jnp.sum(p, axis=-1, keepdims=True) + jnp.exp(sink - m)
            o = jnp.dot(p.astype(BF16), vcat, preferred_element_type=F32)
            o_ref[:, hs] = (o / denom).astype(BF16)


def _attn_prompt(q, k, v, bias, sinks, batch, seq):
    nb = seq // ATTN_BLOCK
    cur = lambda b, i: (b * nb + i, 0)
    prev = lambda b, i: (b * nb + jnp.maximum(i - 1, 0), 0)
    return pl.pallas_call(
        _attn_prompt_kernel,
        grid=(batch, nb),
        in_specs=[pl.BlockSpec(memory_space=pltpu.SMEM),
                  pl.BlockSpec((ATTN_BLOCK, Q_DIM), cur),
                  pl.BlockSpec((ATTN_BLOCK, KV_DIM), cur),
                  pl.BlockSpec((ATTN_BLOCK, KV_DIM), prev),
                  pl.BlockSpec((ATTN_BLOCK, KV_DIM), cur),
                  pl.BlockSpec((ATTN_BLOCK, KV_DIM), prev),
                  _const_spec(bias.shape)],
        out_specs=pl.BlockSpec((ATTN_BLOCK, Q_DIM), cur),
        out_shape=jax.ShapeDtypeStruct((batch * seq, Q_DIM), BF16),
        compiler_params=pltpu.CompilerParams(dimension_semantics=("arbitrary", "arbitrary"),
                                             vmem_limit_bytes=V7X_VMEM_LIMIT_BYTES),
        name="attn_prompt",
    )(sinks, q, k, k, v, v, bias)


def _attn_sample_kernel(qbd_ref, ck_ref, cv_ref, kvn_ref, bias_ref, sink_ref, mask_ref, o_ref, *, w_buf):
    pad = jnp.zeros((SAMPLE_KEYS - w_buf, KV_DIM), F32)
    is_new = lax.broadcasted_iota(jnp.int32, (SAMPLE_KEYS, KV_DIM), 0) == w_buf
    for b in range(SAMPLE_SEQ_PER_STEP):
        kall = jnp.where(is_new, kvn_ref[b:b + 1, :KV_DIM],
                         jnp.concatenate([ck_ref[b], pad], axis=0)).astype(BF16)
        vall = jnp.where(is_new, kvn_ref[b:b + 1, KV_DIM:],
                         jnp.concatenate([cv_ref[b], pad], axis=0)).astype(BF16)
        s = lax.dot_general(qbd_ref[b], kall, (((1,), (1,)), ((), ())),
                            preferred_element_type=F32)
        s = s + bias_ref[...]
        sink = sink_ref[...]
        m = jnp.maximum(jnp.max(s, axis=-1, keepdims=True), sink)
        p = jnp.exp(s - m)
        denom = jnp.sum(p, axis=-1, keepdims=True) + jnp.exp(sink - m)
        of = jnp.dot(p.astype(BF16), vall, preferred_element_type=F32) / denom
        of = of * mask_ref[...]
        o_ref[b] = (of[:, 0:HEAD_DIM] + of[:, HEAD_DIM:2 * HEAD_DIM]
                    + of[:, 2 * HEAD_DIM:3 * HEAD_DIM] + of[:, 3 * HEAD_DIM:]).astype(BF16)


def _attn_sample(qbd, ck, cv, kvn, bias, sink_col, head_mask):
    nseq, w_buf = ck.shape[0], ck.shape[1]
    sb = SAMPLE_SEQ_PER_STEP
    return pl.pallas_call(
        functools.partial(_attn_sample_kernel, w_buf=w_buf),
        grid=(nseq // sb,),
        in_specs=[pl.BlockSpec((sb, N_HEADS, KV_DIM), lambda i: (i, 0, 0)),
                  pl.BlockSpec((sb, w_buf, KV_DIM), lambda i: (i, 0, 0)),
                  pl.BlockSpec((sb, w_buf, KV_DIM), lambda i: (i, 0, 0)),
                  pl.BlockSpec((sb, 2 * KV_DIM), lambda i: (i, 0)),
                  _const_spec(bias.shape), _const_spec(sink_col.shape), _const_spec(head_mask.shape)],
        out_specs=pl.BlockSpec((sb, N_HEADS, HEAD_DIM), lambda i: (i, 0, 0)),
        out_shape=jax.ShapeDtypeStruct((nseq, N_HEADS, HEAD_DIM), BF16),
        compiler_params=pltpu.CompilerParams(dimension_semantics=("arbitrary",),
                                             vmem_limit_bytes=V7X_VMEM_LIMIT_BYTES),
        name="attn_sample",
    )(qbd, ck, cv, kvn, bias, sink_col, head_mask)


def _route_rows(logits, row0, valid_rows):
    tm = logits.shape[0]
    lane = lax.broadcasted_iota(jnp.int32, logits.shape, 1)
    lane_f = lane.astype(F32)
    no_lane = float(ROUTER_LANES)

    def top1(mask):
        best = jnp.max(jnp.where(mask, logits, -jnp.inf), axis=-1, keepdims=True)
        idx = jnp.min(jnp.where(jnp.logical_and(mask, logits == best), lane_f, no_lane), axis=-1, keepdims=True)
        return best, idx

    gmask = lane < N_EXPERT_GROUPS
    gmax, grp = top1(gmask)
    gsum = jnp.sum(jnp.where(gmask, jnp.exp(logits - gmax), 0.0), axis=-1, keepdims=True)
    p_grp = 1.0 / gsum
    lo = N_EXPERT_GROUPS + EXPERTS_PER_GROUP * grp
    emask = jnp.logical_and(lane_f >= lo, lane_f < lo + EXPERTS_PER_GROUP)
    v1, i1 = top1(emask)
    v2, i2 = top1(jnp.logical_and(emask, lane_f != i1))
    e21 = jnp.exp(v2 - v1)
    w1 = p_grp / (1.0 + e21)
    w2 = p_grp * e21 / (1.0 + e21)

    oh1 = lane_f == i1
    oh2 = lane_f == i2
    if valid_rows < tm:
        valid = lax.broadcasted_iota(jnp.int32, logits.shape, 0) < valid_rows
        oh1 = jnp.logical_and(oh1, valid)
        oh2 = jnp.logical_and(oh2, valid)
    oh = oh1.astype(F32) + oh2.astype(F32)
    assign0 = (row0 + lax.broadcasted_iota(jnp.int32, (tm, 1), 0)) * TOP_K
    key1 = (i1.astype(jnp.int32) - N_EXPERT_GROUPS) * (1 << ASSIGN_BITS) + assign0
    key2 = (i2.astype(jnp.int32) - N_EXPERT_GROUPS) * (1 << ASSIGN_BITS) + assign0 + 1
    w1b = lax.bitcast_convert_type(w1, jnp.int32)
    w2b = lax.bitcast_convert_type(w2, jnp.int32)
    words = jnp.where(lane == 0, key1, jnp.where(lane == 1, key2, jnp.where(lane == 2, w1b,
                      jnp.where(lane == 3, w2b, 0))))
    return words, jnp.sum(oh, axis=0, keepdims=True)


def _out_proj_rows(yc_ref, o_ref, sa_ref, sb_ref, x_ref, wc_ref, wa_ref, wo_ref, g2_ref, wr_ref, br_ref,
                   x2_ref, h2_ref, route_ref, cnt_ref, *, valid_rows):
    y_conv = jnp.dot(yc_ref[...], wc_ref[...], preferred_element_type=F32)
    y_attn = jnp.dot(o_ref[...], wa_ref[...], preferred_element_type=F32)
    mix = (sa_ref[...].astype(F32) * y_conv + sb_ref[...].astype(F32) * y_attn).astype(BF16)
    x2 = x_ref[...] + jnp.dot(mix, wo_ref[...], preferred_element_type=F32)
    x2_ref[...] = x2
    h2 = _rms_norm_f32(x2, g2_ref[...])
    h2_ref[...] = h2
    logits = jnp.dot(h2.astype(BF16), wr_ref[...], preferred_element_type=F32) + br_ref[...]
    words, cnt = _route_rows(logits, pl.program_id(0) * x_ref.shape[0], valid_rows)
    route_ref[...] = words
    cnt_ref[...] += cnt


def _out_proj_kernel(*refs, n_first, valid_rows_second):
    first, second, shared = refs[0:5], refs[5:10], refs[10:]
    cnt_ref = shared[-1]
    tm = first[4].shape[0]

    @pl.when(pl.program_id(0) == 0)
    def _():
        cnt_ref[...] = jnp.zeros_like(cnt_ref)

    @pl.when(pl.program_id(0) < n_first)
    def _():
        _out_proj_rows(*first, *shared, valid_rows=tm)

    @pl.when(pl.program_id(0) >= n_first)
    def _():
        _out_proj_rows(*second, *shared, valid_rows=valid_rows_second)


def _out_proj(acts_a, acts_b, wc, wa, wo, g2, wr, br, *, tm, valid_rows_b):
    na = acts_a[4].shape[0] // tm
    nb = acts_b[4].shape[0] // tm
    assert nb == 1
    m_total = (na + nb) * tm
    spec_a = lambda width: pl.BlockSpec((tm, width), lambda i: (jnp.minimum(i, na - 1), 0))
    spec_b = lambda width: pl.BlockSpec((tm, width), lambda i: (jnp.maximum(i - na, 0), 0))
    widths = (D_CONV, Q_DIM, D_MODEL, D_MODEL, D_MODEL)
    in_specs = [spec_a(w) for w in widths] + [spec_b(w) for w in widths]
    in_specs += [_const_spec(wc.shape), _const_spec(wa.shape), _const_spec(wo.shape),
                 _const_spec(g2.shape), _const_spec(wr.shape), _const_spec(br.shape)]
    orow = lambda width: pl.BlockSpec((tm, width), lambda i: (i, 0))
    return pl.pallas_call(
        functools.partial(_out_proj_kernel, n_first=na, valid_rows_second=valid_rows_b),
        grid=(na + nb,),
        in_specs=in_specs,
        out_specs=[orow(D_MODEL), orow(D_MODEL), orow(ROUTER_LANES),
                   pl.BlockSpec((1, ROUTER_LANES), lambda i: (0, 0))],
        out_shape=[jax.ShapeDtypeStruct((m_total, D_MODEL), F32),
                   jax.ShapeDtypeStruct((m_total, D_MODEL), F32),
                   jax.ShapeDtypeStruct((m_total, ROUTER_LANES), jnp.int32),
                   jax.ShapeDtypeStruct((1, ROUTER_LANES), F32)],
        compiler_params=pltpu.CompilerParams(dimension_semantics=("arbitrary",),
                                             vmem_limit_bytes=V7X_VMEM_LIMIT_BYTES),
        name="out_proj",
    )(*acts_a, *acts_b, wc, wa, wo, g2, wr, br)


def _block_table_kernel(counts_ref, bexp_ref, bpos_ref, bcnt_ref, nused_ref, *, n_blocks):
    def per_expert(e, carry):
        blk0, pos0 = carry
        cnt = counts_ref[e]
        nblk = lax.shift_right_logical(cnt + (MOE_BLOCK - 1), MOE_BLOCK_LOG2)

        def mark(b, c):
            off = (b - blk0) * MOE_BLOCK
            bexp_ref[b] = e
            bpos_ref[b] = pos0 + off
            bcnt_ref[b] = jnp.minimum(cnt - off, MOE_BLOCK)
            return c
        lax.fori_loop(blk0, blk0 + nblk, mark, 0)
        return blk0 + nblk, pos0 + cnt

    n_used, _ = lax.fori_loop(0, N_EXPERTS, per_expert, (0, 0))
    nused_ref[0] = n_used

    def unused(b, c):
        bexp_ref[b] = N_EXPERTS - 1
        bpos_ref[b] = 0
        bcnt_ref[b] = 0
        return c
    lax.fori_loop(n_used, n_blocks, unused, 0)


def _block_tables(counts, n_blocks):
    smem = pl.BlockSpec(memory_space=pltpu.SMEM)
    blk = jax.ShapeDtypeStruct((n_blocks,), jnp.int32)
    return pl.pallas_call(
        functools.partial(_block_table_kernel, n_blocks=n_blocks),
        in_specs=[smem],
        out_specs=[smem, smem, smem, smem],
        out_shape=[blk, blk, blk, jax.ShapeDtypeStruct((1,), jnp.int32)],
        name="block_tables",
    )(counts)


def _moe_kernel(order_ref, bexp_ref, bpos_ref, bcnt_ref, nused_ref, h2_hbm, wg_ref, wu_ref, wd_ref, contrib_hbm,
                xs_ref, ys_ref, wgb_ref, wub_ref, wdb_ref, gsem_ref, ssem_ref, *, n_assign, n_blocks):
    i = pl.program_id(0)
    slot = i % 2
    n_used = nused_ref[0]
    active = i < n_used
    last_active = i == n_used - 1

    def gather(pos0, slt, r):
        tok = lax.shift_right_logical(order_ref[pos0 + r], 1)
        return pltpu.make_async_copy(h2_hbm.at[pl.ds(tok, 1), :], xs_ref.at[slt, pl.ds(r, 1), :], gsem_ref.at[slt])

    def gather_wait(slt, r):
        pltpu.make_async_copy(h2_hbm.at[pl.ds(0, 1), :], xs_ref.at[slt, pl.ds(r, 1), :], gsem_ref.at[slt]).wait()

    def scatter(pos0, cnt, trash0, slt, r):
        dst = jnp.where(r < cnt, order_ref[pos0 + r], trash0 + r)
        return pltpu.make_async_copy(ys_ref.at[slt, pl.ds(r, 1), :], contrib_hbm.at[pl.ds(dst, 1), :],
                                     ssem_ref.at[slt])

    def scatter_wait(slt, r):
        pltpu.make_async_copy(ys_ref.at[slt, pl.ds(r, 1), :], contrib_hbm.at[pl.ds(0, 1), :],
                              ssem_ref.at[slt]).wait()

    @pl.when(i == 0)
    def _():
        ys_ref[...] = jnp.zeros_like(ys_ref)
        for s in range(2):
            fill = pltpu.make_async_copy(ys_ref.at[s], contrib_hbm.at[pl.ds(n_assign + s * MOE_BLOCK, MOE_BLOCK), :],
                                         ssem_ref.at[s])
            fill.start()
            fill.wait()

    @pl.when(jnp.logical_and(i == 0, active))
    def _():
        pos0 = bpos_ref[0]
        for r in range(MOE_BLOCK):
            gather(pos0, 0, r).start()

    @pl.when(active)
    def _():
        for r in range(MOE_BLOCK):
            gather_wait(slot, r)

    @pl.when(jnp.logical_and(active, i >= 2))
    def _():
        for r in range(MOE_BLOCK):
            scatter_wait(slot, r)

    @pl.when(jnp.logical_or(i == 0, bexp_ref[i] != bexp_ref[jnp.maximum(i - 1, 0)]))
    def _():
        wgb_ref[...] = wg_ref[0].astype(BF16)
        wub_ref[...] = wu_ref[0].astype(BF16)
        wdb_ref[...] = wd_ref[0].astype(BF16)

    @pl.when(active)
    def _():
        pos_next = bpos_ref[jnp.minimum(i + 1, n_blocks - 1)]
        pos0 = bpos_ref[i]
        cnt = bcnt_ref[i]
        trash0 = n_assign + slot * MOE_BLOCK
        for r in range(MOE_BLOCK):
            gather(pos_next, 1 - slot, r).start()
        xb = xs_ref[slot].astype(BF16)
        gate = jnp.dot(xb, wgb_ref[...], preferred_element_type=F32)
        up = jnp.dot(xb, wub_ref[...], preferred_element_type=F32)
        hmid = (jax.nn.silu(gate) * up).astype(BF16)
        ys_ref[slot] = jnp.dot(hmid, wdb_ref[...], preferred_element_type=F32)
        for r in range(MOE_BLOCK):
            scatter(pos0, cnt, trash0, slot, r).start()

    @pl.when(last_active)
    def _():
        for r in range(MOE_BLOCK):
            gather_wait(1 - slot, r)
        for r in range(MOE_BLOCK):
            scatter_wait(slot, r)

    @pl.when(jnp.logical_and(last_active, i >= 1))
    def _():
        for r in range(MOE_BLOCK):
            scatter_wait(1 - slot, r)


def _moe_experts(order, bexp, bpos, bcnt, n_used, h2, w_gate, w_up, w_down, *, n_assign):
    n_blocks = bexp.shape[0]
    wspec = lambda shape: pl.BlockSpec((1,) + shape, lambda i, order, be, *_: (be[i], 0, 0))
    grid_spec = pltpu.PrefetchScalarGridSpec(
        num_scalar_prefetch=5,
        grid=(n_blocks,),
        in_specs=[pl.BlockSpec(memory_space=pl.ANY),
                  wspec((D_MODEL, D_EXPERT)), wspec((D_MODEL, D_EXPERT)), wspec((D_EXPERT, D_MODEL))],
        out_specs=pl.BlockSpec(memory_space=pl.ANY),
        scratch_shapes=[pltpu.VMEM((2, MOE_BLOCK, D_MODEL), F32),
                        pltpu.VMEM((2, MOE_BLOCK, D_MODEL), F32),
                        pltpu.VMEM((D_MODEL, D_EXPERT), BF16),
                        pltpu.VMEM((D_MODEL, D_EXPERT), BF16),
                        pltpu.VMEM((D_EXPERT, D_MODEL), BF16),
                        pltpu.SemaphoreType.DMA((2,)),
                        pltpu.SemaphoreType.DMA((2,))],
    )
    return pl.pallas_call(
        functools.partial(_moe_kernel, n_assign=n_assign, n_blocks=n_blocks),
        grid_spec=grid_spec,
        out_shape=jax.ShapeDtypeStruct((n_assign + 2 * MOE_BLOCK, D_MODEL), F32),
        compiler_params=pltpu.CompilerParams(dimension_semantics=("arbitrary",),
                                             vmem_limit_bytes=V7X_VMEM_LIMIT_BYTES),
        name="moe_experts",
    )(order, bexp, bpos, bcnt, n_used, h2, w_gate, w_up, w_down)


def _combine_kernel(c_ref, x2_ref, w_ref, gf_ref, y_ref):
    w = w_ref[...]
    moe = w[:, 0:1] * c_ref[:, :D_MODEL] + w[:, 1:2] * c_ref[:, D_MODEL:]
    y_ref[...] = _rms_norm_f32(x2_ref[...] + moe, gf_ref[...])


def _moe_combine(contrib2, x2, w_top, gf, *, row_off, m, tc):
    off = row_off // tc
    return pl.pallas_call(
        _combine_kernel,
        grid=(m // tc,),
        in_specs=[pl.BlockSpec((tc, TOP_K * D_MODEL), lambda i: (i + off, 0)),
                  pl.BlockSpec((tc, D_MODEL), lambda i: (i + off, 0)),
                  pl.BlockSpec((tc, TOP_K), lambda i: (i + off, 0)),
                  _const_spec((1, D_MODEL))],
        out_specs=pl.BlockSpec((tc, D_MODEL), lambda i: (i, 0)),
        out_shape=jax.ShapeDtypeStruct((m, D_MODEL), F32),
        compiler_params=pltpu.CompilerParams(dimension_semantics=("arbitrary",),
                                             vmem_limit_bytes=V7X_VMEM_LIMIT_BYTES),
        name="moe_combine_prompt" if row_off == 0 else "moe_combine_sample",
    )(contrib2, x2, w_top, gf)


def _t5_bucket(dist):
    n = jnp.maximum(dist, 0)
    max_exact = N_BUCKETS // 2
    nf = jnp.maximum(n, 1).astype(F32)
    large = max_exact + (jnp.log(nf / max_exact) / math.log(MAX_DISTANCE / max_exact)
                         * (N_BUCKETS - max_exact)).astype(jnp.int32)
    large = jnp.minimum(large, N_BUCKETS - 1)
    return jnp.where(n < max_exact, n, large)


def _prompt_bias_table(rel_bias):
    qi = jnp.arange(ATTN_BLOCK, dtype=jnp.int32)[:, None]
    kj = jnp.arange(2 * ATTN_BLOCK, dtype=jnp.int32)[None, :] - ATTN_BLOCK
    dist = qi - kj
    valid = (dist >= 0) & (dist <= WINDOW)
    bias = rel_bias.astype(F32)[_t5_bucket(dist)]
    bias = jnp.where(valid[..., None], bias, NEG_BIG)
    return jnp.moveaxis(bias, -1, 0)


def _sample_bias_table(rel_bias, w_buf):
    j = jnp.arange(SAMPLE_KEYS, dtype=jnp.int32)
    dist = jnp.where(j < w_buf, w_buf - j, 0)
    valid = (j <= w_buf) & (dist <= WINDOW)
    bias = rel_bias.astype(F32)[_t5_bucket(dist)]
    return jnp.where(valid[:, None], bias, NEG_BIG).T


def kernel(x_prompt, x_sample, cache_conv, cache_k, cache_v, norm1_g, w_in, conv_w, w_conv_out, w_attn_out, w_o, sinks, rel_bias, norm2_g, w_router_group, b_router_group, w_router_expert, b_router_expert, w_e_gate, w_e_up, w_e_down, norm_f_g):
    assert norm1_g.shape[0] == 1, "single-layer configuration"
    batch, seq, _ = x_prompt.shape
    nseq = x_sample.shape[0]
    w_buf = cache_k.shape[2]
    mp = batch * seq
    m_total = mp + nseq
    assert seq % TM_DENSE == 0 and seq % ATTN_BLOCK == 0 and mp % COMBINE_BLOCK == 0
    assert nseq % SAMPLE_SEQ_PER_STEP == 0 and mp % nseq == 0 and w_buf + 1 <= SAMPLE_KEYS
    assert TOP_K == 2 and MOE_BLOCK == 1 << MOE_BLOCK_LOG2 and m_total * TOP_K < 1 << ASSIGN_BITS

    g1 = norm1_g[0][None, :]
    g2 = norm2_g[0][None, :]
    gf = norm_f_g[None, :]
    wi = w_in[0].astype(BF16)
    o0 = 0
    w_parts = []
    for width in (D_CONV, D_CONV, D_CONV, Q_DIM, 2 * KV_DIM, D_MODEL, D_MODEL):
        w_parts.append(wi[:, o0:o0 + width])
        o0 += width
    cw = conv_w[0]
    wc = w_conv_out[0].astype(BF16)
    wa = w_attn_out[0].astype(BF16)
    wo = w_o[0].astype(BF16)
    pad_cols = ROUTER_LANES - N_EXPERT_GROUPS - N_EXPERTS
    wr = jnp.concatenate([w_router_group[0], w_router_expert[0],
                          jnp.zeros((D_MODEL, pad_cols), F32)], axis=1).astype(BF16)
    br = jnp.concatenate([b_router_group[0], b_router_expert[0], jnp.zeros((pad_cols,), F32)])[None, :]
    sink = sinks[0].astype(F32)

    xp = x_prompt.reshape(mp, D_MODEL)
    bps = seq // TM_DENSE
    yc_p, q_p, k_p, v_p, sa_p, sb_p, ut_p, kvt_p = _in_proj(
        xp, g1, w_parts, cw, tm=TM_DENSE, blocks_per_seq=bps, u_tail=8, kv_tail=WINDOW)
    o_p = _attn_prompt(q_p, k_p, v_p, _prompt_bias_table(rel_bias), sink, batch, seq)

    pad_rows = lambda t: jnp.pad(t, ((0, TM_DENSE - nseq), (0, 0)))
    xs = pad_rows(x_sample.reshape(nseq, D_MODEL))
    hist = (pad_rows(cache_conv[0][:, 0, :]), pad_rows(cache_conv[0][:, 1, :]))
    yc_s, q_s, _, _, sa_s, sb_s, ut_s, kvt_s = _in_proj(
        xs, g1, w_parts, cw, tm=TM_DENSE, blocks_per_seq=1, u_tail=TM_DENSE, kv_tail=TM_DENSE, hist=hist)
    u_s = ut_s[0, :nseq]
    kv_s = kvt_s[0, :nseq]
    head_mask = (jnp.arange(KV_DIM)[None, :] // HEAD_DIM == jnp.arange(N_HEADS)[:, None] // GROUP)
    qbd = (jnp.tile(q_s[:nseq].reshape(nseq, N_HEADS, HEAD_DIM), (1, 1, N_KV_HEADS))
           * head_mask[None].astype(BF16))
    o_s = _attn_sample(qbd, cache_k[0].reshape(nseq, w_buf, KV_DIM), cache_v[0].reshape(nseq, w_buf, KV_DIM),
                       kv_s, _sample_bias_table(rel_bias, w_buf), sink[:, None], head_mask.astype(F32))
    o_s = pad_rows(o_s.reshape(nseq, Q_DIM))

    x2, h2, route, cnt = _out_proj((yc_p, o_p, sa_p, sb_p, xp), (yc_s, o_s, sa_s, sb_s, xs),
                                   wc, wa, wo, g2, wr, br, tm=TM_DENSE, valid_rows_b=nseq)

    n_assign = m_total * TOP_K
    keys = route[:m_total, 0:TOP_K].reshape(-1)
    w_top = lax.bitcast_convert_type(route[:m_total, TOP_K:2 * TOP_K], F32)
    counts = cnt[0, N_EXPERT_GROUPS:N_EXPERT_GROUPS + N_EXPERTS].astype(jnp.int32)
    order = jnp.pad(jnp.sort(keys) & ((1 << ASSIGN_BITS) - 1), (0, MOE_BLOCK))
    n_blocks = -(-n_assign // MOE_BLOCK) + N_EXPERTS
    bexp, bpos, bcnt, n_used = _block_tables(counts, n_blocks)
    contrib = _moe_experts(order, bexp, bpos, bcnt, n_used, h2, w_e_gate[0], w_e_up[0], w_e_down[0],
                           n_assign=n_assign)
    contrib2 = contrib.reshape(-1, TOP_K * D_MODEL)
    y_p = _moe_combine(contrib2, x2, w_top, gf, row_off=0, m=mp, tc=COMBINE_BLOCK)
    y_s = _moe_combine(contrib2, x2, w_top, gf, row_off=mp, m=nseq, tc=nseq)

    y_prompt = y_p.reshape(batch, seq, D_MODEL)
    y_sample = y_s.reshape(nseq, 1, D_MODEL)
    conv_state_prompt = ut_p.reshape(batch, bps, 8, D_CONV)[:, -1, 8 - (CONV_WIDTH - 1):, :][None]
    kv_last = kvt_p.reshape(batch, bps, WINDOW, 2 * KV_DIM)[:, -1]
    k_win_prompt = kv_last[:, :, :KV_DIM].reshape(batch, WINDOW, N_KV_HEADS, HEAD_DIM)[None]
    v_win_prompt = kv_last[:, :, KV_DIM:].reshape(batch, WINDOW, N_KV_HEADS, HEAD_DIM)[None]
    conv_state_sample = jnp.concatenate([cache_conv[0][:, 1:, :], u_s[:, None, :]], axis=1)[None]
    k_new = kv_s[:, :KV_DIM].reshape(nseq, 1, N_KV_HEADS, HEAD_DIM)
    v_new = kv_s[:, KV_DIM:].reshape(nseq, 1, N_KV_HEADS, HEAD_DIM)
    k_win_sample = jnp.concatenate([cache_k[0], k_new], axis=1)[:, -w_buf:][None]
    v_win_sample = jnp.concatenate([cache_v[0], v_new], axis=1)[:, -w_buf:][None]
    return (y_prompt, y_sample, conv_state_prompt, k_win_prompt, v_win_prompt,
            conv_state_sample, k_win_sample, v_win_sample)
```

```python
import functools
import math

import jax
import jax.numpy as jnp
from jax import lax
from jax.experimental import pallas as pl
from jax.experimental.pallas import tpu as pltpu

D_MODEL = 1024
D_CONV = 1024
CONV_WIDTH = 3
N_HEADS = 16
N_KV_HEADS = 4
HEAD_DIM = 64
GROUP = N_HEADS // N_KV_HEADS
WINDOW = 128
Q_DIM = N_HEADS * HEAD_DIM
KV_DIM = N_KV_HEADS * HEAD_DIM
N_BUCKETS = 32
MAX_DISTANCE = 128
N_EXPERT_GROUPS = 4
EXPERTS_PER_GROUP = 8
N_EXPERTS = N_EXPERT_GROUPS * EXPERTS_PER_GROUP
TOP_K = 2
D_EXPERT = 512
EPS = 1e-6
PAST_LEN = 8192

BF16 = jnp.bfloat16
F32 = jnp.float32
NEG_BIG = -1e30

V7X_VMEM_LIMIT_BYTES = 56 * 1024 * 1024
TILE_ROWS = 8
LANES = 128
ROUTER_LANES = 128
TM_DENSE = 512
ATTN_BLOCK = 128
MOE_BLOCK = 256
MOE_BLOCK_LOG2 = 8
ASSIGN_BITS = 16
COMBINE_BLOCK = 256
SAMPLE_KEYS = 256
SAMPLE_SEQ_PER_STEP = 8


def _const_spec(shape):
    nd = len(shape)
    return pl.BlockSpec(shape, lambda *_: (0,) * nd, pipeline_mode=pl.Buffered(1))


def _rms_norm_f32(xf, g):
    return xf * lax.rsqrt(jnp.mean(xf * xf, axis=-1, keepdims=True) + EPS) * g


def _in_proj_kernel(*refs, tm, sample, blocks_per_seq, u_tail, kv_tail):
    if sample:
        (x_ref, hist0_ref, hist1_ref, g_ref, wcb_ref, wcc_ref, wch_ref, wq_ref, wkv_ref, wga_ref, wgb_ref,
         cw_ref, yc_ref, q_ref, k_ref, v_ref, sa_ref, sb_ref, ut_ref, kvt_ref) = refs
    else:
        (x_ref, g_ref, wcb_ref, wcc_ref, wch_ref, wq_ref, wkv_ref, wga_ref, wgb_ref,
         cw_ref, yc_ref, q_ref, k_ref, v_ref, sa_ref, sb_ref, ut_ref, kvt_ref, ubuf_ref) = refs

    h = _rms_norm_f32(x_ref[...], g_ref[...]).astype(BF16)

    def proj(w_ref):
        return jnp.dot(h, w_ref[...], preferred_element_type=F32)

    u = proj(wcc_ref) * proj(wch_ref)
    w0 = cw_ref[0:1, :]
    w1 = cw_ref[1:2, :]
    w2 = cw_ref[2:3, :]
    if sample:
        conv = w0 * hist0_ref[...] + w1 * hist1_ref[...] + w2 * u
    else:
        @pl.when(pl.program_id(0) % blocks_per_seq == 0)
        def _():
            ubuf_ref[0:8, :] = jnp.zeros((8, D_CONV), F32)

        ubuf_ref[8:8 + tm, :] = u
        conv = w0 * ubuf_ref[6:6 + tm, :] + w1 * ubuf_ref[7:7 + tm, :] + w2 * u
        ubuf_ref[0:8, :] = u[tm - 8:, :]
    yc_ref[...] = (proj(wcb_ref) * conv).astype(BF16)
    ut_ref[0] = u[tm - u_tail:, :]

    q_ref[...] = (proj(wq_ref) * (HEAD_DIM ** -0.5)).astype(BF16)
    kv = proj(wkv_ref)
    k_ref[...] = kv[:, :KV_DIM].astype(BF16)
    v_ref[...] = kv[:, KV_DIM:].astype(BF16)
    kvt_ref[0] = kv[tm - kv_tail:, :]
    sa_ref[...] = jax.nn.sigmoid(proj(wga_ref)).astype(sa_ref.dtype)
    sb_ref[...] = jax.nn.sigmoid(proj(wgb_ref)).astype(sb_ref.dtype)


def _in_proj(x, g1, w_parts, conv_w, *, tm, blocks_per_seq, u_tail, kv_tail, hist=None, gate_dtype=BF16):
    m = x.shape[0]
    nblk = m // tm
    sample = hist is not None
    row = lambda width: pl.BlockSpec((tm, width), lambda i: (i, 0))
    in_specs = [row(D_MODEL)]
    args = [x]
    if sample:
        in_specs += [row(D_CONV), row(D_CONV)]
        args += list(hist)
    in_specs += [_const_spec((1, D_MODEL))] + [_const_spec(w.shape) for w in w_parts] + [_const_spec(conv_w.shape)]
    args += [g1] + list(w_parts) + [conv_w]
    out_shape = [
        jax.ShapeDtypeStruct((m, D_CONV), BF16),
        jax.ShapeDtypeStruct((m, Q_DIM), BF16),
        jax.ShapeDtypeStruct((m, KV_DIM), BF16),
        jax.ShapeDtypeStruct((m, KV_DIM), BF16),
        jax.ShapeDtypeStruct((m, D_MODEL), gate_dtype),
        jax.ShapeDtypeStruct((m, D_MODEL), gate_dtype),
        jax.ShapeDtypeStruct((nblk, u_tail, D_CONV), F32),
        jax.ShapeDtypeStruct((nblk, kv_tail, 2 * KV_DIM), F32),
    ]
    out_specs = [row(D_CONV), row(Q_DIM), row(KV_DIM), row(KV_DIM), row(D_MODEL), row(D_MODEL),
                 pl.BlockSpec((1, u_tail, D_CONV), lambda i: (i, 0, 0)),
                 pl.BlockSpec((1, kv_tail, 2 * KV_DIM), lambda i: (i, 0, 0))]
    scratch = [] if sample else [pltpu.VMEM((tm + 8, D_CONV), F32)]
    return pl.pallas_call(
        functools.partial(_in_proj_kernel, tm=tm, sample=sample, blocks_per_seq=blocks_per_seq,
                          u_tail=u_tail, kv_tail=kv_tail),
        grid=(nblk,),
        in_specs=in_specs,
        out_specs=out_specs,
        out_shape=out_shape,
        scratch_shapes=scratch,
        compiler_params=pltpu.CompilerParams(dimension_semantics=("arbitrary",),
                                             vmem_limit_bytes=V7X_VMEM_LIMIT_BYTES),
        name="in_proj_sample" if sample else "in_proj_prompt",
    )(*args)


def _attn_prompt_kernel(sink_ref, q_ref, kc_ref, kp_ref, vc_ref, vp_ref, bias_ref, o_ref):
    first = pl.program_id(1) == 0
    col = lax.broadcasted_iota(jnp.int32, (ATTN_BLOCK, 2 * ATTN_BLOCK), 1)
    no_prev = jnp.logical_and(first, col < ATTN_BLOCK)
    for g in range(N_KV_HEADS):
        ks = slice(g * HEAD_DIM, (g + 1) * HEAD_DIM)
        kcat = jnp.concatenate([kp_ref[:, ks], kc_ref[:, ks]], axis=0)
        vcat = jnp.concatenate([vp_ref[:, ks], vc_ref[:, ks]], axis=0)
        for hh in range(GROUP):
            h = g * GROUP + hh
            hs = slice(h * HEAD_DIM, (h + 1) * HEAD_DIM)
            s = lax.dot_general(q_ref[:, hs], kcat, (((1,), (1,)), ((), ())),
                                preferred_element_type=F32)
            s = jnp.where(no_prev, NEG_BIG, s + bias_ref[h])
            sink = sink_ref[h]
            m = jnp.maximum(jnp.max(s, axis=-1, keepdims=True), sink)
            p = jnp.exp(s - m)
            denom = jnp.sum(p, axis=-1, keepdims=True) + jnp.exp(sink - m)
            o = jnp.dot(p.astype(BF16), vcat, preferred_element_type=F32)
            o_ref[:, hs] = (o / denom).astype(BF16)


def _attn_prompt(q, k, v, bias, sinks, batch, seq):
    nb = seq // ATTN_BLOCK
    cur = lambda b, i: (b * nb + i, 0)
    prev = lambda b, i: (b * nb + jnp.maximum(i - 1, 0), 0)
    return pl.pallas_call(
        _attn_prompt_kernel,
        grid=(batch, nb),
        in_specs=[pl.BlockSpec(memory_space=pltpu.SMEM),
                  pl.BlockSpec((ATTN_BLOCK, Q_DIM), cur),
                  pl.BlockSpec((ATTN_BLOCK, KV_DIM), cur),
                  pl.BlockSpec((ATTN_BLOCK, KV_DIM), prev),
                  pl.BlockSpec((ATTN_BLOCK, KV_DIM), cur),
                  pl.BlockSpec((ATTN_BLOCK, KV_DIM), prev),
                  _const_spec(bias.shape)],
        out_specs=pl.BlockSpec((ATTN_BLOCK, Q_DIM), cur),
        out_shape=jax.ShapeDtypeStruct((batch * seq, Q_DIM), BF16),
        compiler_params=pltpu.CompilerParams(dimension_semantics=("arbitrary", "arbitrary"),
                                             vmem_limit_bytes=V7X_VMEM_LIMIT_BYTES),
        name="attn_prompt",
    )(sinks, q, k, k, v, v, bias)


def _attn_sample_kernel(qbd_ref, ck_ref, cv_ref, kvn_ref, bias_ref, sink_ref, mask_ref, o_ref, *, w_buf):
    pad = jnp.zeros((SAMPLE_KEYS - w_buf, KV_DIM), F32)
    is_new = lax.broadcasted_iota(jnp.int32, (SAMPLE_KEYS, KV_DIM), 0) == w_buf
    for b in range(SAMPLE_SEQ_PER_STEP):
        kall = jnp.where(is_new, kvn_ref[b:b + 1, :KV_DIM],
                         jnp.concatenate([ck_ref[b], pad], axis=0)).astype(BF16)
        vall = jnp.where(is_new, kvn_ref[b:b + 1, KV_DIM:],
                         jnp.concatenate([cv_ref[b], pad], axis=0)).astype(BF16)
        s = lax.dot_general(qbd_ref[b], kall, (((1,), (1,)), ((), ())),
                            preferred_element_type=F32)
        s = s + bias_ref[...]
        sink = sink_ref[...]
        m = jnp.maximum(jnp.max(s, axis=-1, keepdims=True), sink)
        p = jnp.exp(s - m)
        p = p / (jnp.sum(p, axis=-1, keepdims=True) + jnp.exp(sink - m))
        of = jnp.dot(p.astype(BF16), vall, preferred_element_type=F32)
        of = of * mask_ref[...]
        o_ref[b] = (of[:, 0:HEAD_DIM] + of[:, HEAD_DIM:2 * HEAD_DIM]
                    + of[:, 2 * HEAD_DIM:3 * HEAD_DIM] + of[:, 3 * HEAD_DIM:]).astype(BF16)


def _attn_sample(qbd, ck, cv, kvn, bias, sink_col, head_mask):
    nseq, w_buf = ck.shape[0], ck.shape[1]
    sb = SAMPLE_SEQ_PER_STEP
    return pl.pallas_call(
        functools.partial(_attn_sample_kernel, w_buf=w_buf),
        grid=(nseq // sb,),
        in_specs=[pl.BlockSpec((sb, N_HEADS, KV_DIM), lambda i: (i, 0, 0)),
                  pl.BlockSpec((sb, w_buf, KV_DIM), lambda i: (i, 0, 0)),
                  pl.BlockSpec((sb, w_buf, KV_DIM), lambda i: (i, 0, 0)),
                  pl.BlockSpec((sb, 2 * KV_DIM), lambda i: (i, 0)),
                  _const_spec(bias.shape), _const_spec(sink_col.shape), _const_spec(head_mask.shape)],
        out_specs=pl.BlockSpec((sb, N_HEADS, HEAD_DIM), lambda i: (i, 0, 0)),
        out_shape=jax.ShapeDtypeStruct((nseq, N_HEADS, HEAD_DIM), BF16),
        compiler_params=pltpu.CompilerParams(dimension_semantics=("arbitrary",),
                                             vmem_limit_bytes=V7X_VMEM_LIMIT_BYTES),
        name="attn_sample",
    )(qbd, ck, cv, kvn, bias, sink_col, head_mask)


def _route_rows(logits, row0, valid_rows, half):
    tm = logits.shape[0]
    lane = lax.broadcasted_iota(jnp.int32, logits.shape, 1)
    lane_f = lane.astype(F32)
    no_lane = float(ROUTER_LANES)

    def top1(mask):
        best = jnp.max(jnp.where(mask, logits, -jnp.inf), axis=-1, keepdims=True)
        idx = jnp.min(jnp.where(jnp.logical_and(mask, logits == best), lane_f, no_lane), axis=-1, keepdims=True)
        return best, idx

    gmask = lane < N_EXPERT_GROUPS
    gmax, grp = top1(gmask)
    gsum = jnp.sum(jnp.where(gmask, jnp.exp(logits - gmax), 0.0), axis=-1, keepdims=True)
    p_grp = 1.0 / gsum
    lo = N_EXPERT_GROUPS + EXPERTS_PER_GROUP * grp
    emask = jnp.logical_and(lane_f >= lo, lane_f < lo + EXPERTS_PER_GROUP)
    v1, i1 = top1(emask)
    v2, i2 = top1(jnp.logical_and(emask, lane_f != i1))
    e21 = jnp.exp(v2 - v1)
    w1 = p_grp / (1.0 + e21)
    w2 = p_grp * e21 / (1.0 + e21)

    oh1 = lane_f == i1
    oh2 = lane_f == i2
    if valid_rows < tm:
        valid = lax.broadcasted_iota(jnp.int32, logits.shape, 0) < valid_rows
        oh1 = jnp.logical_and(oh1, valid)
        oh2 = jnp.logical_and(oh2, valid)
    oh = oh1.astype(F32) + oh2.astype(F32)
    token = row0 + lax.broadcasted_iota(jnp.int32, (tm, 1), 0)
    key1 = (i1.astype(jnp.int32) - N_EXPERT_GROUPS) * (1 << ASSIGN_BITS) + token
    key2 = (i2.astype(jnp.int32) - N_EXPERT_GROUPS) * (1 << ASSIGN_BITS) + token + half
    w1b = lax.bitcast_convert_type(w1, jnp.int32)
    w2b = lax.bitcast_convert_type(w2, jnp.int32)
    words = jnp.where(lane == 0, key1, jnp.where(lane == 1, key2, jnp.where(lane == 2, w1b,
                      jnp.where(lane == 3, w2b, 0))))
    return words, jnp.sum(oh, axis=0, keepdims=True)


def _store_token_tiles(ref, x):
    n = x.shape[0]
    for c in range(D_MODEL // LANES):
        ref[pl.ds(c, n, stride=TILE_ROWS), :] = x[:, c * LANES:(c + 1) * LANES]


def _load_token_tiles(ref, n):
    return jnp.concatenate([ref[pl.ds(c, n, stride=TILE_ROWS), :] for c in range(D_MODEL // LANES)], axis=1)


def _out_proj_rows(yc_ref, o_ref, sa_ref, sb_ref, x_ref, wc_ref, wa_ref, wo_ref, g2_ref, wr_ref, br_ref,
                   x2_ref, h2t_ref, route_ref, cnt_ref, *, valid_rows, half):
    y_conv = jnp.dot(yc_ref[...], wc_ref[...], preferred_element_type=F32)
    y_attn = jnp.dot(o_ref[...], wa_ref[...], preferred_element_type=F32)
    mix = (sa_ref[...].astype(F32) * y_conv + sb_ref[...].astype(F32) * y_attn).astype(BF16)
    x2 = x_ref[...] + jnp.dot(mix, wo_ref[...], preferred_element_type=F32)
    x2_ref[...] = x2
    h2 = _rms_norm_f32(x2, g2_ref[...])
    _store_token_tiles(h2t_ref, h2)
    logits = jnp.dot(h2.astype(BF16), wr_ref[...], preferred_element_type=F32) + br_ref[...]
    words, cnt = _route_rows(logits, pl.program_id(0) * x_ref.shape[0], valid_rows, half)
    route_ref[...] = words
    cnt_ref[...] += cnt


def _out_proj_kernel(*refs, n_first, valid_rows_second, half):
    first, second, shared = refs[0:5], refs[5:10], refs[10:]
    cnt_ref = shared[-1]
    tm = first[4].shape[0]

    @pl.when(pl.program_id(0) == 0)
    def _():
        cnt_ref[...] = jnp.zeros_like(cnt_ref)

    @pl.when(pl.program_id(0) < n_first)
    def _():
        _out_proj_rows(*first, *shared, valid_rows=tm, half=half)

    @pl.when(pl.program_id(0) >= n_first)
    def _():
        _out_proj_rows(*second, *shared, valid_rows=valid_rows_second, half=half)


def _out_proj(acts_a, acts_b, wc, wa, wo, g2, wr, br, *, tm, valid_rows_b, half):
    na = acts_a[4].shape[0] // tm
    nb = acts_b[4].shape[0] // tm
    assert nb == 1
    m_total = (na + nb) * tm
    spec_a = lambda width: pl.BlockSpec((tm, width), lambda i: (jnp.minimum(i, na - 1), 0))
    spec_b = lambda width: pl.BlockSpec((tm, width), lambda i: (jnp.maximum(i - na, 0), 0))
    widths = (D_CONV, Q_DIM, D_MODEL, D_MODEL, D_MODEL)
    in_specs = [spec_a(w) for w in widths] + [spec_b(w) for w in widths]
    in_specs += [_const_spec(wc.shape), _const_spec(wa.shape), _const_spec(wo.shape),
                 _const_spec(g2.shape), _const_spec(wr.shape), _const_spec(br.shape)]
    orow = lambda width: pl.BlockSpec((tm, width), lambda i: (i, 0))
    return pl.pallas_call(
        functools.partial(_out_proj_kernel, n_first=na, valid_rows_second=valid_rows_b, half=half),
        grid=(na + nb,),
        in_specs=in_specs,
        out_specs=[orow(D_MODEL), pl.BlockSpec((tm * TILE_ROWS, LANES), lambda i: (i, 0)), orow(ROUTER_LANES),
                   pl.BlockSpec((1, ROUTER_LANES), lambda i: (0, 0))],
        out_shape=[jax.ShapeDtypeStruct((m_total, D_MODEL), F32),
                   jax.ShapeDtypeStruct((m_total * TILE_ROWS, LANES), F32),
                   jax.ShapeDtypeStruct((m_total, ROUTER_LANES), jnp.int32),
                   jax.ShapeDtypeStruct((1, ROUTER_LANES), F32)],
        compiler_params=pltpu.CompilerParams(dimension_semantics=("arbitrary",),
                                             vmem_limit_bytes=V7X_VMEM_LIMIT_BYTES),
        name="out_proj",
    )(*acts_a, *acts_b, wc, wa, wo, g2, wr, br)


def _block_table_kernel(counts_ref, bexp_ref, bpos_ref, bcnt_ref, nused_ref, *, n_blocks):
    def per_expert(e, carry):
        blk0, pos0 = carry
        cnt = counts_ref[e]
        nblk = lax.shift_right_logical(cnt + (MOE_BLOCK - 1), MOE_BLOCK_LOG2)

        def mark(b, c):
            off = (b - blk0) * MOE_BLOCK
            bexp_ref[b] = e
            bpos_ref[b] = pos0 + off
            bcnt_ref[b] = jnp.minimum(cnt - off, MOE_BLOCK)
            return c
        lax.fori_loop(blk0, blk0 + nblk, mark, 0)
        return blk0 + nblk, pos0 + cnt

    n_used, _ = lax.fori_loop(0, N_EXPERTS, per_expert, (0, 0))
    nused_ref[0] = n_used

    def unused(b, c):
        bexp_ref[b] = N_EXPERTS - 1
        bpos_ref[b] = 0
        bcnt_ref[b] = 0
        return c
    lax.fori_loop(n_used, n_blocks, unused, 0)


def _block_tables(counts, n_blocks):
    smem = pl.BlockSpec(memory_space=pltpu.SMEM)
    blk = jax.ShapeDtypeStruct((n_blocks,), jnp.int32)
    return pl.pallas_call(
        functools.partial(_block_table_kernel, n_blocks=n_blocks),
        in_specs=[smem],
        out_specs=[smem, smem, smem, smem],
        out_shape=[blk, blk, blk, jax.ShapeDtypeStruct((1,), jnp.int32)],
        name="block_tables",
    )(counts)


def _moe_kernel(tok_ref, order_ref, bexp_ref, bpos_ref, bcnt_ref, nused_ref, h2t_hbm, wg_ref, wu_ref, wd_ref,
                contrib_hbm, xs_ref, ys_ref, wgb_ref, wub_ref, wdb_ref, gsem_ref, ssem_ref,
                *, m_total, half, n_blocks):
    i = pl.program_id(0)
    slot = i % 2
    n_used = nused_ref[0]
    active = i < n_used
    last_active = i == n_used - 1
    tile = lambda t: pl.ds(pl.multiple_of(t * TILE_ROWS, TILE_ROWS), TILE_ROWS)

    def gather(pos0, slt, r):
        return pltpu.make_async_copy(h2t_hbm.at[tile(tok_ref[pos0 + r]), :], xs_ref.at[slt, tile(r), :],
                                     gsem_ref.at[slt])

    def gather_wait(slt, r):
        pltpu.make_async_copy(h2t_hbm.at[tile(0), :], xs_ref.at[slt, tile(r), :], gsem_ref.at[slt]).wait()

    def scatter(pos0, cnt, trash0, slt, r):
        dst = jnp.where(r < cnt, order_ref[pos0 + r], trash0 + r)
        return pltpu.make_async_copy(ys_ref.at[slt, tile(r), :], contrib_hbm.at[tile(dst), :], ssem_ref.at[slt])

    def scatter_wait(slt, r):
        pltpu.make_async_copy(ys_ref.at[slt, tile(r), :], contrib_hbm.at[tile(0), :], ssem_ref.at[slt]).wait()

    @pl.when(i == 0)
    def _():
        ys_ref[...] = jnp.zeros_like(ys_ref)
        gap = half - m_total
        fills = [(2 * half + s * MOE_BLOCK, MOE_BLOCK) for s in range(2)]
        fills += [(k * half + m_total, gap) for k in range(TOP_K)] if gap else []
        for start, n in fills:
            fill = pltpu.make_async_copy(ys_ref.at[0, pl.ds(0, n * TILE_ROWS), :],
                                         contrib_hbm.at[pl.ds(start * TILE_ROWS, n * TILE_ROWS), :], ssem_ref.at[0])
            fill.start()
            fill.wait()

    @pl.when(jnp.logical_and(i == 0, active))
    def _():
        pos0 = bpos_ref[0]
        for r in range(MOE_BLOCK):
            gather(pos0, 0, r).start()

    @pl.when(active)
    def _():
        for r in range(MOE_BLOCK):
            gather_wait(slot, r)

    @pl.when(jnp.logical_and(active, i >= 2))
    def _():
        for r in range(MOE_BLOCK):
            scatter_wait(slot, r)

    @pl.when(jnp.logical_or(i == 0, bexp_ref[i] != bexp_ref[jnp.maximum(i - 1, 0)]))
    def _():
        wgb_ref[...] = wg_ref[0].astype(BF16)
        wub_ref[...] = wu_ref[0].astype(BF16)
        wdb_ref[...] = wd_ref[0].astype(BF16)

    @pl.when(active)
    def _():
        pos_next = bpos_ref[jnp.minimum(i + 1, n_blocks - 1)]
        pos0 = bpos_ref[i]
        cnt = bcnt_ref[i]
        trash0 = 2 * half + slot * MOE_BLOCK
        for r in range(MOE_BLOCK):
            gather(pos_next, 1 - slot, r).start()
        xb = _load_token_tiles(xs_ref.at[slot], MOE_BLOCK).astype(BF16)
        gate = jnp.dot(xb, wgb_ref[...], preferred_element_type=F32)
        up = jnp.dot(xb, wub_ref[...], preferred_element_type=F32)
        hmid = (jax.nn.silu(gate) * up).astype(BF16)
        _store_token_tiles(ys_ref.at[slot], jnp.dot(hmid, wdb_ref[...], preferred_element_type=F32))
        for r in range(MOE_BLOCK):
            scatter(pos0, cnt, trash0, slot, r).start()

    @pl.when(last_active)
    def _():
        for r in range(MOE_BLOCK):
            gather_wait(1 - slot, r)
        for r in range(MOE_BLOCK):
            scatter_wait(slot, r)

    @pl.when(jnp.logical_and(last_active, i >= 1))
    def _():
        for r in range(MOE_BLOCK):
            scatter_wait(1 - slot, r)


def _moe_experts(tok, order, bexp, bpos, bcnt, n_used, h2t, w_gate, w_up, w_down, *, m_total, half):
    n_blocks = bexp.shape[0]
    assert 0 <= half - m_total <= MOE_BLOCK
    wspec = lambda shape: pl.BlockSpec((1,) + shape, lambda i, tok, order, be, *_: (be[i], 0, 0))
    grid_spec = pltpu.PrefetchScalarGridSpec(
        num_scalar_prefetch=6,
        grid=(n_blocks,),
        in_specs=[pl.BlockSpec(memory_space=pl.ANY),
                  wspec((D_MODEL, D_EXPERT)), wspec((D_MODEL, D_EXPERT)), wspec((D_EXPERT, D_MODEL))],
        out_specs=pl.BlockSpec(memory_space=pl.ANY),
        scratch_shapes=[pltpu.VMEM((2, MOE_BLOCK * TILE_ROWS, LANES), F32),
                        pltpu.VMEM((2, MOE_BLOCK * TILE_ROWS, LANES), F32),
                        pltpu.VMEM((D_MODEL, D_EXPERT), BF16),
                        pltpu.VMEM((D_MODEL, D_EXPERT), BF16),
                        pltpu.VMEM((D_EXPERT, D_MODEL), BF16),
                        pltpu.SemaphoreType.DMA((2,)),
                        pltpu.SemaphoreType.DMA((2,))],
    )
    return pl.pallas_call(
        functools.partial(_moe_kernel, m_total=m_total, half=half, n_blocks=n_blocks),
        grid_spec=grid_spec,
        out_shape=jax.ShapeDtypeStruct(((2 * half + 2 * MOE_BLOCK) * TILE_ROWS, LANES), F32),
        compiler_params=pltpu.CompilerParams(dimension_semantics=("arbitrary",),
                                             vmem_limit_bytes=V7X_VMEM_LIMIT_BYTES),
        name="moe_experts",
    )(tok, order, bexp, bpos, bcnt, n_used, h2t, w_gate, w_up, w_down)


def _combine_kernel(c0_ref, c1_ref, x2_ref, w_ref, gf_ref, y_ref):
    tc = x2_ref.shape[0]
    w = w_ref[...]
    moe = w[:, 0:1] * _load_token_tiles(c0_ref, tc) + w[:, 1:2] * _load_token_tiles(c1_ref, tc)
    y_ref[...] = _rms_norm_f32(x2_ref[...] + moe, gf_ref[...])


def _moe_combine(contrib, x2, w_top, gf, *, row_off, m, tc, half):
    off = row_off // tc
    assert row_off % tc == 0 and half % tc == 0
    ctile = lambda k: pl.BlockSpec((tc * TILE_ROWS, LANES), lambda i: (i + off + k * (half // tc), 0))
    return pl.pallas_call(
        _combine_kernel,
        grid=(m // tc,),
        in_specs=[ctile(0), ctile(1),
                  pl.BlockSpec((tc, D_MODEL), lambda i: (i + off, 0)),
                  pl.BlockSpec((tc, TOP_K), lambda i: (i + off, 0)),
                  _const_spec((1, D_MODEL))],
        out_specs=pl.BlockSpec((tc, D_MODEL), lambda i: (i, 0)),
        out_shape=jax.ShapeDtypeStruct((m, D_MODEL), F32),
        compiler_params=pltpu.CompilerParams(dimension_semantics=("arbitrary",),
                                             vmem_limit_bytes=V7X_VMEM_LIMIT_BYTES),
        name="moe_combine_prompt" if row_off == 0 else "moe_combine_sample",
    )(contrib, contrib, x2, w_top, gf)


def _t5_bucket(dist):
    n = jnp.maximum(dist, 0)
    max_exact = N_BUCKETS // 2
    nf = jnp.maximum(n, 1).astype(F32)
    large = max_exact + (jnp.log(nf / max_exact) / math.log(MAX_DISTANCE / max_exact)
                         * (N_BUCKETS - max_exact)).astype(jnp.int32)
    large = jnp.minimum(large, N_BUCKETS - 1)
    return jnp.where(n < max_exact, n, large)


def _bucket_bias(rel_bias, dist):
    onehot = (_t5_bucket(dist)[..., None] == jnp.arange(N_BUCKETS, dtype=jnp.int32)).astype(F32)
    return jnp.dot(onehot, rel_bias.astype(F32), precision=lax.Precision.HIGHEST)


def _prompt_bias_table(rel_bias):
    qi = jnp.arange(ATTN_BLOCK, dtype=jnp.int32)[:, None]
    kj = jnp.arange(2 * ATTN_BLOCK, dtype=jnp.int32)[None, :] - ATTN_BLOCK
    dist = qi - kj
    valid = (dist >= 0) & (dist <= WINDOW)
    bias = jnp.where(valid[..., None], _bucket_bias(rel_bias, dist), NEG_BIG)
    return jnp.moveaxis(bias, -1, 0)


def _sample_bias_table(rel_bias, w_buf):
    j = jnp.arange(SAMPLE_KEYS, dtype=jnp.int32)
    dist = jnp.where(j < w_buf, w_buf - j, 0)
    valid = (j <= w_buf) & (dist <= WINDOW)
    return jnp.where(valid[:, None], _bucket_bias(rel_bias, dist), NEG_BIG).T


def kernel(x_prompt, x_sample, cache_conv, cache_k, cache_v, norm1_g, w_in, conv_w, w_conv_out, w_attn_out, w_o, sinks, rel_bias, norm2_g, w_router_group, b_router_group, w_router_expert, b_router_expert, w_e_gate, w_e_up, w_e_down, norm_f_g):
    assert norm1_g.shape[0] == 1, "single-layer configuration"
    batch, seq, _ = x_prompt.shape
    nseq = x_sample.shape[0]
    w_buf = cache_k.shape[2]
    mp = batch * seq
    m_total = mp + nseq
    assert seq % TM_DENSE == 0 and seq % ATTN_BLOCK == 0 and mp % COMBINE_BLOCK == 0
    assert nseq % SAMPLE_SEQ_PER_STEP == 0 and mp % nseq == 0 and w_buf + 1 <= SAMPLE_KEYS
    assert TOP_K == 2 and MOE_BLOCK == 1 << MOE_BLOCK_LOG2 and m_total * TOP_K < 1 << ASSIGN_BITS

    g1 = norm1_g[0][None, :]
    g2 = norm2_g[0][None, :]
    gf = norm_f_g[None, :]
    wi = w_in[0].astype(BF16)
    o0 = 0
    w_parts = []
    for width in (D_CONV, D_CONV, D_CONV, Q_DIM, 2 * KV_DIM, D_MODEL, D_MODEL):
        w_parts.append(wi[:, o0:o0 + width])
        o0 += width
    cw = conv_w[0]
    wc = w_conv_out[0].astype(BF16)
    wa = w_attn_out[0].astype(BF16)
    wo = w_o[0].astype(BF16)
    pad_cols = ROUTER_LANES - N_EXPERT_GROUPS - N_EXPERTS
    wr = jnp.concatenate([w_router_group[0], w_router_expert[0],
                          jnp.zeros((D_MODEL, pad_cols), F32)], axis=1).astype(BF16)
    br = jnp.concatenate([b_router_group[0], b_router_expert[0], jnp.zeros((pad_cols,), F32)])[None, :]
    sink = sinks[0].astype(F32)

    xp = x_prompt.reshape(mp, D_MODEL)
    bps = seq // TM_DENSE
    yc_p, q_p, k_p, v_p, sa_p, sb_p, ut_p, kvt_p = _in_proj(
        xp, g1, w_parts, cw, tm=TM_DENSE, blocks_per_seq=bps, u_tail=8, kv_tail=WINDOW)
    o_p = _attn_prompt(q_p, k_p, v_p, _prompt_bias_table(rel_bias), sink, batch, seq)

    pad_rows = lambda t: jnp.pad(t, ((0, TM_DENSE - nseq), (0, 0)))
    xs = pad_rows(x_sample.reshape(nseq, D_MODEL))
    hist = (pad_rows(cache_conv[0][:, 0, :]), pad_rows(cache_conv[0][:, 1, :]))
    yc_s, q_s, _, _, sa_s, sb_s, ut_s, kvt_s = _in_proj(
        xs, g1, w_parts, cw, tm=TM_DENSE, blocks_per_seq=1, u_tail=TM_DENSE, kv_tail=TM_DENSE, hist=hist,
        gate_dtype=F32)
    u_s = ut_s[0, :nseq]
    kv_s = kvt_s[0, :nseq]
    head_mask = (jnp.arange(KV_DIM)[None, :] // HEAD_DIM == jnp.arange(N_HEADS)[:, None] // GROUP)
    qbd = (jnp.tile(q_s[:nseq].reshape(nseq, N_HEADS, HEAD_DIM), (1, 1, N_KV_HEADS))
           * head_mask[None].astype(BF16))
    o_s = _attn_sample(qbd, cache_k[0].reshape(nseq, w_buf, KV_DIM), cache_v[0].reshape(nseq, w_buf, KV_DIM),
                       kv_s, _sample_bias_table(rel_bias, w_buf), sink[:, None], head_mask.astype(F32))
    o_s = pad_rows(o_s.reshape(nseq, Q_DIM))

    half = -(-m_total // COMBINE_BLOCK) * COMBINE_BLOCK
    assert half % nseq == 0 and TOP_K * half < 1 << ASSIGN_BITS
    x2, h2t, route, cnt = _out_proj((yc_p, o_p, sa_p, sb_p, xp), (yc_s, o_s, sa_s, sb_s, xs),
                                    wc, wa, wo, g2, wr, br, tm=TM_DENSE, valid_rows_b=nseq, half=half)

    n_assign = m_total * TOP_K
    keys = route[:m_total, 0:TOP_K].reshape(-1)
    w_top = lax.bitcast_convert_type(route[:m_total, TOP_K:2 * TOP_K], F32)
    counts = cnt[0, N_EXPERT_GROUPS:N_EXPERT_GROUPS + N_EXPERTS].astype(jnp.int32)
    order = jnp.pad(jnp.sort(keys) & ((1 << ASSIGN_BITS) - 1), (0, MOE_BLOCK))
    tok = jnp.where(order >= half, order - half, order)
    n_blocks = -(-n_assign // MOE_BLOCK) + N_EXPERTS
    bexp, bpos, bcnt, n_used = _block_tables(counts, n_blocks)
    contrib = _moe_experts(tok, order, bexp, bpos, bcnt, n_used, h2t, w_e_gate[0], w_e_up[0], w_e_down[0],
                           m_total=m_total, half=half)
    y_p = _moe_combine(contrib, x2, w_top, gf, row_off=0, m=mp, tc=COMBINE_BLOCK, half=half)
    y_s = _moe_combine(contrib, x2, w_top, gf, row_off=mp, m=nseq, tc=nseq, half=half)

    y_prompt = y_p.reshape(batch, seq, D_MODEL)
    y_sample = y_s.reshape(nseq, 1, D_MODEL)
    conv_state_prompt = ut_p.reshape(batch, bps, 8, D_CONV)[:, -1, 8 - (CONV_WIDTH - 1):, :][None]
    kv_last = kvt_p.reshape(batch, bps, WINDOW, 2 * KV_DIM)[:, -1]
    k_win_prompt = kv_last[:, :, :KV_DIM].reshape(batch, WINDOW, N_KV_HEADS, HEAD_DIM)[None]
    v_win_prompt = kv_last[:, :, KV_DIM:].reshape(batch, WINDOW, N_KV_HEADS, HEAD_DIM)[None]
    conv_state_sample = jnp.concatenate([cache_conv[0][:, 1:, :], u_s[:, None, :]], axis=1)[None]
    k_new = kv_s[:, :KV_DIM].reshape(nseq, 1, N_KV_HEADS, HEAD_DIM)
    v_new = kv_s[:, KV_DIM:].reshape(nseq, 1, N_KV_HEADS, HEAD_DIM)
    k_win_sample = jnp.concatenate([cache_k[0], k_new], axis=1)[:, -w_buf:][None]
    v_win_sample = jnp.concatenate([cache_v[0], v_new], axis=1)[:, -w_buf:][None]
    return (y_prompt, y_sample, conv_state_prompt, k_win_prompt, v_win_prompt,
            conv_state_sample, k_win_sample, v_win_sample)
```

```python
import functools
import math

import jax
import jax.numpy as jnp
from jax import lax
from jax.experimental import pallas as pl
from jax.experimental.pallas import tpu as pltpu

D_MODEL = 1024
D_CONV = 1024
CONV_WIDTH = 3
N_HEADS = 16
N_KV_HEADS = 4
HEAD_DIM = 64
GROUP = N_HEADS // N_KV_HEADS
WINDOW = 128
Q_DIM = N_HEADS * HEAD_DIM
KV_DIM = N_KV_HEADS * HEAD_DIM
N_BUCKETS = 32
MAX_DISTANCE = 128
N_EXPERT_GROUPS = 4
EXPERTS_PER_GROUP = 8
N_EXPERTS = N_EXPERT_GROUPS * EXPERTS_PER_GROUP
TOP_K = 2
D_EXPERT = 512
EPS = 1e-6
PAST_LEN = 8192

BF16 = jnp.bfloat16
F32 = jnp.float32
NEG_BIG = -1e30

V7X_VMEM_LIMIT_BYTES = 56 * 1024 * 1024
TILE_ROWS = 8
LANES = 128
ROUTER_LANES = 128
TM_DENSE = 512
ATTN_BLOCK = 128
MOE_BLOCK = 256
MOE_BLOCK_LOG2 = 8
ASSIGN_BITS = 16
COMBINE_BLOCK = 256
SAMPLE_KEYS = 256
SAMPLE_SEQ_PER_STEP = 8


def _const_spec(shape):
    nd = len(shape)
    return pl.BlockSpec(shape, lambda *_: (0,) * nd, pipeline_mode=pl.Buffered(1))


def _rms_norm_f32(xf, g):
    return xf * lax.rsqrt(jnp.mean(xf * xf, axis=-1, keepdims=True) + EPS) * g


def _in_proj_kernel(*refs, tm, sample, blocks_per_seq, u_tail, kv_tail):
    if sample:
        (x_ref, hist0_ref, hist1_ref, g_ref, wcb_ref, wcc_ref, wch_ref, wq_ref, wkv_ref, wga_ref, wgb_ref,
         cw_ref, yc_ref, q_ref, k_ref, v_ref, sa_ref, sb_ref, ut_ref, kvt_ref) = refs
    else:
        (x_ref, g_ref, wcb_ref, wcc_ref, wch_ref, wq_ref, wkv_ref, wga_ref, wgb_ref,
         cw_ref, yc_ref, q_ref, k_ref, v_ref, sa_ref, sb_ref, ut_ref, kvt_ref, ubuf_ref) = refs

    h = _rms_norm_f32(x_ref[...], g_ref[...]).astype(BF16)

    def proj(w_ref):
        return jnp.dot(h, w_ref[...], preferred_element_type=F32)

    u = proj(wcc_ref) * proj(wch_ref)
    w0 = cw_ref[0:1, :]
    w1 = cw_ref[1:2, :]
    w2 = cw_ref[2:3, :]
    if sample:
        conv = w0 * hist0_ref[...] + w1 * hist1_ref[...] + w2 * u
    else:
        @pl.when(pl.program_id(0) % blocks_per_seq == 0)
        def _():
            ubuf_ref[0:8, :] = jnp.zeros((8, D_CONV), F32)

        ubuf_ref[8:8 + tm, :] = u
        conv = w0 * ubuf_ref[6:6 + tm, :] + w1 * ubuf_ref[7:7 + tm, :] + w2 * u
        ubuf_ref[0:8, :] = u[tm - 8:, :]
    yc_ref[...] = (proj(wcb_ref) * conv).astype(BF16)
    ut_ref[0] = u[tm - u_tail:, :]

    q_ref[...] = (proj(wq_ref) * (HEAD_DIM ** -0.5)).astype(BF16)
    kv = proj(wkv_ref)
    k_ref[...] = kv[:, :KV_DIM].astype(BF16)
    v_ref[...] = kv[:, KV_DIM:].astype(BF16)
    kvt_ref[0] = kv[tm - kv_tail:, :]
    sa_ref[...] = jax.nn.sigmoid(proj(wga_ref)).astype(sa_ref.dtype)
    sb_ref[...] = jax.nn.sigmoid(proj(wgb_ref)).astype(sb_ref.dtype)


def _in_proj(x, g1, w_parts, conv_w, *, tm, blocks_per_seq, u_tail, kv_tail, hist=None, gate_dtype=BF16):
    m = x.shape[0]
    nblk = m // tm
    sample = hist is not None
    row = lambda width: pl.BlockSpec((tm, width), lambda i: (i, 0))
    in_specs = [row(D_MODEL)]
    args = [x]
    if sample:
        in_specs += [row(D_CONV), row(D_CONV)]
        args += list(hist)
    in_specs += [_const_spec((1, D_MODEL))] + [_const_spec(w.shape) for w in w_parts] + [_const_spec(conv_w.shape)]
    args += [g1] + list(w_parts) + [conv_w]
    out_shape = [
        jax.ShapeDtypeStruct((m, D_CONV), BF16),
        jax.ShapeDtypeStruct((m, Q_DIM), BF16),
        jax.ShapeDtypeStruct((m, KV_DIM), BF16),
        jax.ShapeDtypeStruct((m, KV_DIM), BF16),
        jax.ShapeDtypeStruct((m, D_MODEL), gate_dtype),
        jax.ShapeDtypeStruct((m, D_MODEL), gate_dtype),
        jax.ShapeDtypeStruct((nblk, u_tail, D_CONV), F32),
        jax.ShapeDtypeStruct((nblk, kv_tail, 2 * KV_DIM), F32),
    ]
    out_specs = [row(D_CONV), row(Q_DIM), row(KV_DIM), row(KV_DIM), row(D_MODEL), row(D_MODEL),
                 pl.BlockSpec((1, u_tail, D_CONV), lambda i: (i, 0, 0)),
                 pl.BlockSpec((1, kv_tail, 2 * KV_DIM), lambda i: (i, 0, 0))]
    scratch = [] if sample else [pltpu.VMEM((tm + 8, D_CONV), F32)]
    return pl.pallas_call(
        functools.partial(_in_proj_kernel, tm=tm, sample=sample, blocks_per_seq=blocks_per_seq,
                          u_tail=u_tail, kv_tail=kv_tail),
        grid=(nblk,),
        in_specs=in_specs,
        out_specs=out_specs,
        out_shape=out_shape,
        scratch_shapes=scratch,
        compiler_params=pltpu.CompilerParams(dimension_semantics=("arbitrary",),
                                             vmem_limit_bytes=V7X_VMEM_LIMIT_BYTES),
        name="in_proj_sample" if sample else "in_proj_prompt",
    )(*args)


def _attn_prompt_kernel(sink_ref, q_ref, kc_ref, kp_ref, vc_ref, vp_ref, bias_ref, o_ref):
    first = pl.program_id(1) == 0
    col = lax.broadcasted_iota(jnp.int32, (ATTN_BLOCK, 2 * ATTN_BLOCK), 1)
    no_prev = jnp.logical_and(first, col < ATTN_BLOCK)
    for g in range(N_KV_HEADS):
        ks = slice(g * HEAD_DIM, (g + 1) * HEAD_DIM)
        kcat = jnp.concatenate([kp_ref[:, ks], kc_ref[:, ks]], axis=0)
        vcat = jnp.concatenate([vp_ref[:, ks], vc_ref[:, ks]], axis=0)
        for hh in range(GROUP):
            h = g * GROUP + hh
            hs = slice(h * HEAD_DIM, (h + 1) * HEAD_DIM)
            s = lax.dot_general(q_ref[:, hs], kcat, (((1,), (1,)), ((), ())),
                                preferred_element_type=F32)
            s = jnp.where(no_prev, NEG_BIG, s + bias_ref[h])
            sink = sink_ref[h]
            m = jnp.maximum(jnp.max(s, axis=-1, keepdims=True), sink)
            p = jnp.exp(s - m)
            denom = jnp.sum(p, axis=-1, keepdims=True) + jnp.exp(sink - m)
            o = jnp.dot(p.astype(BF16), vcat, preferred_element_type=F32)
            o_ref[:, hs] = (o / denom).astype(BF16)


def _attn_prompt(q, k, v, bias, sinks, batch, seq):
    nb = seq // ATTN_BLOCK
    cur = lambda b, i: (b * nb + i, 0)
    prev = lambda b, i: (b * nb + jnp.maximum(i - 1, 0), 0)
    return pl.pallas_call(
        _attn_prompt_kernel,
        grid=(batch, nb),
        in_specs=[pl.BlockSpec(memory_space=pltpu.SMEM),
                  pl.BlockSpec((ATTN_BLOCK, Q_DIM), cur),
                  pl.BlockSpec((ATTN_BLOCK, KV_DIM), cur),
                  pl.BlockSpec((ATTN_BLOCK, KV_DIM), prev),
                  pl.BlockSpec((ATTN_BLOCK, KV_DIM), cur),
                  pl.BlockSpec((ATTN_BLOCK, KV_DIM), prev),
                  _const_spec(bias.shape)],
        out_specs=pl.BlockSpec((ATTN_BLOCK, Q_DIM), cur),
        out_shape=jax.ShapeDtypeStruct((batch * seq, Q_DIM), BF16),
        compiler_params=pltpu.CompilerParams(dimension_semantics=("arbitrary", "arbitrary"),
                                             vmem_limit_bytes=V7X_VMEM_LIMIT_BYTES),
        name="attn_prompt",
    )(sinks, q, k, k, v, v, bias)


def _attn_sample_kernel(qbd_ref, ck_ref, cv_ref, kvn_ref, bias_ref, sink_ref, mask_ref, o_ref, *, w_buf):
    pad = jnp.zeros((SAMPLE_KEYS - w_buf, KV_DIM), F32)
    is_new = lax.broadcasted_iota(jnp.int32, (SAMPLE_KEYS, KV_DIM), 0) == w_buf
    for b in range(SAMPLE_SEQ_PER_STEP):
        kall = jnp.where(is_new, kvn_ref[b:b + 1, :KV_DIM],
                         jnp.concatenate([ck_ref[b], pad], axis=0)).astype(BF16)
        vall = jnp.where(is_new, kvn_ref[b:b + 1, KV_DIM:],
                         jnp.concatenate([cv_ref[b], pad], axis=0)).astype(BF16)
        s = lax.dot_general(qbd_ref[b], kall, (((1,), (1,)), ((), ())),
                            preferred_element_type=F32)
        s = s + bias_ref[...]
        sink = sink_ref[...]
        m = jnp.maximum(jnp.max(s, axis=-1, keepdims=True), sink)
        p = jnp.exp(s - m)
        p = p / (jnp.sum(p, axis=-1, keepdims=True) + jnp.exp(sink - m))
        of = jnp.dot(p.astype(BF16), vall, preferred_element_type=F32)
        of = of * mask_ref[...]
        o_ref[b] = (of[:, 0:HEAD_DIM] + of[:, HEAD_DIM:2 * HEAD_DIM]
                    + of[:, 2 * HEAD_DIM:3 * HEAD_DIM] + of[:, 3 * HEAD_DIM:]).astype(BF16)


def _attn_sample(qbd, ck, cv, kvn, bias, sink_col, head_mask):
    nseq, w_buf = ck.shape[0], ck.shape[1]
    sb = SAMPLE_SEQ_PER_STEP
    return pl.pallas_call(
        functools.partial(_attn_sample_kernel, w_buf=w_buf),
        grid=(nseq // sb,),
        in_specs=[pl.BlockSpec((sb, N_HEADS, KV_DIM), lambda i: (i, 0, 0)),
                  pl.BlockSpec((sb, w_buf, KV_DIM), lambda i: (i, 0, 0)),
                  pl.BlockSpec((sb, w_buf, KV_DIM), lambda i: (i, 0, 0)),
                  pl.BlockSpec((sb, 2 * KV_DIM), lambda i: (i, 0)),
                  _const_spec(bias.shape), _const_spec(sink_col.shape), _const_spec(head_mask.shape)],
        out_specs=pl.BlockSpec((sb, N_HEADS, HEAD_DIM), lambda i: (i, 0, 0)),
        out_shape=jax.ShapeDtypeStruct((nseq, N_HEADS, HEAD_DIM), BF16),
        compiler_params=pltpu.CompilerParams(dimension_semantics=("arbitrary",),
                                             vmem_limit_bytes=V7X_VMEM_LIMIT_BYTES),
        name="attn_sample",
    )(qbd, ck, cv, kvn, bias, sink_col, head_mask)


def _route_rows(logits, row0, valid_rows, half):
    tm = logits.shape[0]
    lane = lax.broadcasted_iota(jnp.int32, logits.shape, 1)
    lane_f = lane.astype(F32)
    no_lane = float(ROUTER_LANES)

    def top1(mask):
        best = jnp.max(jnp.where(mask, logits, -jnp.inf), axis=-1, keepdims=True)
        idx = jnp.min(jnp.where(jnp.logical_and(mask, logits == best), lane_f, no_lane), axis=-1, keepdims=True)
        return best, idx

    gmask = lane < N_EXPERT_GROUPS
    gmax, grp = top1(gmask)
    gsum = jnp.sum(jnp.where(gmask, jnp.exp(logits - gmax), 0.0), axis=-1, keepdims=True)
    p_grp = 1.0 / gsum
    lo = N_EXPERT_GROUPS + EXPERTS_PER_GROUP * grp
    emask = jnp.logical_and(lane_f >= lo, lane_f < lo + EXPERTS_PER_GROUP)
    v1, i1 = top1(emask)
    v2, i2 = top1(jnp.logical_and(emask, lane_f != i1))
    e21 = jnp.exp(v2 - v1)
    w1 = p_grp / (1.0 + e21)
    w2 = p_grp * e21 / (1.0 + e21)

    oh1 = lane_f == i1
    oh2 = lane_f == i2
    if valid_rows < tm:
        valid = lax.broadcasted_iota(jnp.int32, logits.shape, 0) < valid_rows
        oh1 = jnp.logical_and(oh1, valid)
        oh2 = jnp.logical_and(oh2, valid)
    oh = oh1.astype(F32) + oh2.astype(F32)
    token = row0 + lax.broadcasted_iota(jnp.int32, (tm, 1), 0)
    key1 = (i1.astype(jnp.int32) - N_EXPERT_GROUPS) * (1 << ASSIGN_BITS) + token
    key2 = (i2.astype(jnp.int32) - N_EXPERT_GROUPS) * (1 << ASSIGN_BITS) + token + half
    w1b = lax.bitcast_convert_type(w1, jnp.int32)
    w2b = lax.bitcast_convert_type(w2, jnp.int32)
    words = jnp.where(lane == 0, key1, jnp.where(lane == 1, key2, jnp.where(lane == 2, w1b,
                      jnp.where(lane == 3, w2b, 0))))
    return words, jnp.sum(oh, axis=0, keepdims=True)


def _store_token_tiles(ref, x):
    n = x.shape[0]
    for c in range(D_MODEL // LANES):
        ref[pl.ds(c, n, stride=TILE_ROWS), :] = x[:, c * LANES:(c + 1) * LANES]


def _load_token_tiles(ref, n):
    return jnp.concatenate([ref[pl.ds(c, n, stride=TILE_ROWS), :] for c in range(D_MODEL // LANES)], axis=1)


def _out_proj_rows(yc_ref, o_ref, sa_ref, sb_ref, x_ref, wc_ref, wa_ref, wo_ref, g2_ref, wr_ref, br_ref,
                   x2_ref, h2t_ref, route_ref, cnt_ref, *, valid_rows, half):
    y_conv = jnp.dot(yc_ref[...], wc_ref[...], preferred_element_type=F32)
    y_attn = jnp.dot(o_ref[...], wa_ref[...], preferred_element_type=F32)
    mix = (sa_ref[...].astype(F32) * y_conv + sb_ref[...].astype(F32) * y_attn).astype(BF16)
    x2 = x_ref[...] + jnp.dot(mix, wo_ref[...], preferred_element_type=F32)
    x2_ref[...] = x2
    h2 = _rms_norm_f32(x2, g2_ref[...])
    _store_token_tiles(h2t_ref, h2)
    logits = jnp.dot(h2.astype(BF16), wr_ref[...], preferred_element_type=F32) + br_ref[...]
    words, cnt = _route_rows(logits, pl.program_id(0) * x_ref.shape[0], valid_rows, half)
    route_ref[...] = words
    cnt_ref[...] += cnt


def _out_proj_kernel(*refs, n_first, valid_rows_second, half):
    first, second, shared = refs[0:5], refs[5:10], refs[10:]
    cnt_ref = shared[-1]
    tm = first[4].shape[0]

    @pl.when(pl.program_id(0) == 0)
    def _():
        cnt_ref[...] = jnp.zeros_like(cnt_ref)

    @pl.when(pl.program_id(0) < n_first)
    def _():
        _out_proj_rows(*first, *shared, valid_rows=tm, half=half)

    @pl.when(pl.program_id(0) >= n_first)
    def _():
        _out_proj_rows(*second, *shared, valid_rows=valid_rows_second, half=half)


def _out_proj(acts_a, acts_b, wc, wa, wo, g2, wr, br, *, tm, valid_rows_b, half):
    na = acts_a[4].shape[0] // tm
    nb = acts_b[4].shape[0] // tm
    assert nb == 1
    m_total = (na + nb) * tm
    spec_a = lambda width: pl.BlockSpec((tm, width), lambda i: (jnp.minimum(i, na - 1), 0))
    spec_b = lambda width: pl.BlockSpec((tm, width), lambda i: (jnp.maximum(i - na, 0), 0))
    widths = (D_CONV, Q_DIM, D_MODEL, D_MODEL, D_MODEL)
    in_specs = [spec_a(w) for w in widths] + [spec_b(w) for w in widths]
    in_specs += [_const_spec(wc.shape), _const_spec(wa.shape), _const_spec(wo.shape),
                 _const_spec(g2.shape), _const_spec(wr.shape), _const_spec(br.shape)]
    orow = lambda width: pl.BlockSpec((tm, width), lambda i: (i, 0))
    return pl.pallas_call(
        functools.partial(_out_proj_kernel, n_first=na, valid_rows_second=valid_rows_b, half=half),
        grid=(na + nb,),
        in_specs=in_specs,
        out_specs=[orow(D_MODEL), pl.BlockSpec((tm * TILE_ROWS, LANES), lambda i: (i, 0)), orow(ROUTER_LANES),
                   pl.BlockSpec((1, ROUTER_LANES), lambda i: (0, 0))],
        out_shape=[jax.ShapeDtypeStruct((m_total, D_MODEL), F32),
                   jax.ShapeDtypeStruct((m_total * TILE_ROWS, LANES), F32),
                   jax.ShapeDtypeStruct((m_total, ROUTER_LANES), jnp.int32),
                   jax.ShapeDtypeStruct((1, ROUTER_LANES), F32)],
        compiler_params=pltpu.CompilerParams(dimension_semantics=("arbitrary",),
                                             vmem_limit_bytes=V7X_VMEM_LIMIT_BYTES),
        name="out_proj",
    )(*acts_a, *acts_b, wc, wa, wo, g2, wr, br)


def _block_table_kernel(counts_ref, bexp_ref, bpos_ref, bcnt_ref, nused_ref, *, n_blocks):
    def per_expert(e, carry):
        blk0, pos0 = carry
        cnt = counts_ref[e]
        nblk = lax.shift_right_logical(cnt + (MOE_BLOCK - 1), MOE_BLOCK_LOG2)

        def mark(b, c):
            off = (b - blk0) * MOE_BLOCK
            bexp_ref[b] = e
            bpos_ref[b] = pos0 + off
            bcnt_ref[b] = jnp.minimum(cnt - off, MOE_BLOCK)
            return c
        lax.fori_loop(blk0, blk0 + nblk, mark, 0)
        return blk0 + nblk, pos0 + cnt

    n_used, _ = lax.fori_loop(0, N_EXPERTS, per_expert, (0, 0))
    nused_ref[0] = n_used

    def unused(b, c):
        bexp_ref[b] = N_EXPERTS - 1
        bpos_ref[b] = 0
        bcnt_ref[b] = 0
        return c
    lax.fori_loop(n_used, n_blocks, unused, 0)


def _block_tables(counts, n_blocks):
    smem = pl.BlockSpec(memory_space=pltpu.SMEM)
    blk = jax.ShapeDtypeStruct((n_blocks,), jnp.int32)
    return pl.pallas_call(
        functools.partial(_block_table_kernel, n_blocks=n_blocks),
        in_specs=[smem],
        out_specs=[smem, smem, smem, smem],
        out_shape=[blk, blk, blk, jax.ShapeDtypeStruct((1,), jnp.int32)],
        name="block_tables",
    )(counts)


def _moe_kernel(tok_ref, order_ref, bexp_ref, bpos_ref, bcnt_ref, nused_ref, h2t_hbm, wg_ref, wu_ref, wd_ref,
                contrib_hbm, xs_ref, ys_ref, wgb_ref, wub_ref, wdb_ref, gsem_ref, ssem_ref,
                *, m_total, half, n_blocks):
    i = pl.program_id(0)
    slot = i % 2
    n_used = nused_ref[0]
    active = i < n_used
    last_active = i == n_used - 1
    tile = lambda t: pl.ds(pl.multiple_of(t * TILE_ROWS, TILE_ROWS), TILE_ROWS)

    def gather(pos0, slt, r):
        return pltpu.make_async_copy(h2t_hbm.at[tile(tok_ref[pos0 + r]), :], xs_ref.at[slt, tile(r), :],
                                     gsem_ref.at[slt])

    def gather_wait(slt, r):
        pltpu.make_async_copy(h2t_hbm.at[tile(0), :], xs_ref.at[slt, tile(r), :], gsem_ref.at[slt]).wait()

    def scatter(pos0, cnt, trash0, slt, r):
        dst = jnp.where(r < cnt, order_ref[pos0 + r], trash0 + r)
        return pltpu.make_async_copy(ys_ref.at[slt, tile(r), :], contrib_hbm.at[tile(dst), :], ssem_ref.at[slt])

    def scatter_wait(slt, r):
        pltpu.make_async_copy(ys_ref.at[slt, tile(r), :], contrib_hbm.at[tile(0), :], ssem_ref.at[slt]).wait()

    @pl.when(i == 0)
    def _():
        ys_ref[...] = jnp.zeros_like(ys_ref)
        gap = half - m_total
        fills = [(2 * half + s * MOE_BLOCK, MOE_BLOCK) for s in range(2)]
        fills += [(k * half + m_total, gap) for k in range(TOP_K)] if gap else []
        for start, n in fills:
            fill = pltpu.make_async_copy(ys_ref.at[0, pl.ds(0, n * TILE_ROWS), :],
                                         contrib_hbm.at[pl.ds(start * TILE_ROWS, n * TILE_ROWS), :], ssem_ref.at[0])
            fill.start()
            fill.wait()

    @pl.when(jnp.logical_and(i == 0, active))
    def _():
        pos0 = bpos_ref[0]
        for r in range(MOE_BLOCK):
            gather(pos0, 0, r).start()

    @pl.when(active)
    def _():
        for r in range(MOE_BLOCK):
            gather_wait(slot, r)

    @pl.when(jnp.logical_and(active, i >= 2))
    def _():
        for r in range(MOE_BLOCK):
            scatter_wait(slot, r)

    @pl.when(jnp.logical_or(i == 0, bexp_ref[i] != bexp_ref[jnp.maximum(i - 1, 0)]))
    def _():
        wgb_ref[...] = wg_ref[0].astype(BF16)
        wub_ref[...] = wu_ref[0].astype(BF16)
        wdb_ref[...] = wd_ref[0].astype(BF16)

    def run_block(slt):
        pos_next = bpos_ref[jnp.minimum(i + 1, n_blocks - 1)]
        pos0 = bpos_ref[i]
        cnt = bcnt_ref[i]
        trash0 = 2 * half + slt * MOE_BLOCK
        for r in range(MOE_BLOCK):
            gather(pos_next, 1 - slt, r).start()
        xb = _load_token_tiles(xs_ref.at[slt], MOE_BLOCK).astype(BF16)
        gate = jnp.dot(xb, wgb_ref[...], preferred_element_type=F32)
        up = jnp.dot(xb, wub_ref[...], preferred_element_type=F32)
        hmid = (jax.nn.silu(gate) * up).astype(BF16)
        _store_token_tiles(ys_ref.at[slt], jnp.dot(hmid, wdb_ref[...], preferred_element_type=F32))
        for r in range(MOE_BLOCK):
            scatter(pos0, cnt, trash0, slt, r).start()

    for slt in range(2):
        pl.when(jnp.logical_and(active, slot == slt))(functools.partial(run_block, slt))

    @pl.when(last_active)
    def _():
        for r in range(MOE_BLOCK):
            gather_wait(1 - slot, r)
        for r in range(MOE_BLOCK):
            scatter_wait(slot, r)

    @pl.when(jnp.logical_and(last_active, i >= 1))
    def _():
        for r in range(MOE_BLOCK):
            scatter_wait(1 - slot, r)


def _moe_experts(tok, order, bexp, bpos, bcnt, n_used, h2t, w_gate, w_up, w_down, *, m_total, half):
    n_blocks = bexp.shape[0]
    assert 0 <= half - m_total <= MOE_BLOCK
    wspec = lambda shape: pl.BlockSpec((1,) + shape, lambda i, tok, order, be, *_: (be[i], 0, 0))
    grid_spec = pltpu.PrefetchScalarGridSpec(
        num_scalar_prefetch=6,
        grid=(n_blocks,),
        in_specs=[pl.BlockSpec(memory_space=pl.ANY),
                  wspec((D_MODEL, D_EXPERT)), wspec((D_MODEL, D_EXPERT)), wspec((D_EXPERT, D_MODEL))],
        out_specs=pl.BlockSpec(memory_space=pl.ANY),
        scratch_shapes=[pltpu.VMEM((2, MOE_BLOCK * TILE_ROWS, LANES), F32),
                        pltpu.VMEM((2, MOE_BLOCK * TILE_ROWS, LANES), F32),
                        pltpu.VMEM((D_MODEL, D_EXPERT), BF16),
                        pltpu.VMEM((D_MODEL, D_EXPERT), BF16),
                        pltpu.VMEM((D_EXPERT, D_MODEL), BF16),
                        pltpu.SemaphoreType.DMA((2,)),
                        pltpu.SemaphoreType.DMA((2,))],
    )
    return pl.pallas_call(
        functools.partial(_moe_kernel, m_total=m_total, half=half, n_blocks=n_blocks),
        grid_spec=grid_spec,
        out_shape=jax.ShapeDtypeStruct(((2 * half + 2 * MOE_BLOCK) * TILE_ROWS, LANES), F32),
        compiler_params=pltpu.CompilerParams(dimension_semantics=("arbitrary",),
                                             vmem_limit_bytes=V7X_VMEM_LIMIT_BYTES),
        name="moe_experts",
    )(tok, order, bexp, bpos, bcnt, n_used, h2t, w_gate, w_up, w_down)


def _combine_kernel(c0_ref, c1_ref, x2_ref, w_ref, gf_ref, y_ref):
    tc = x2_ref.shape[0]
    w = w_ref[...]
    moe = w[:, 0:1] * _load_token_tiles(c0_ref, tc) + w[:, 1:2] * _load_token_tiles(c1_ref, tc)
    y_ref[...] = _rms_norm_f32(x2_ref[...] + moe, gf_ref[...])


def _moe_combine(contrib, x2, w_top, gf, *, row_off, m, tc, half):
    off = row_off // tc
    assert row_off % tc == 0 and half % tc == 0
    ctile = lambda k: pl.BlockSpec((tc * TILE_ROWS, LANES), lambda i: (i + off + k * (half // tc), 0))
    return pl.pallas_call(
        _combine_kernel,
        grid=(m // tc,),
        in_specs=[ctile(0), ctile(1),
                  pl.BlockSpec((tc, D_MODEL), lambda i: (i + off, 0)),
                  pl.BlockSpec((tc, TOP_K), lambda i: (i + off, 0)),
                  _const_spec((1, D_MODEL))],
        out_specs=pl.BlockSpec((tc, D_MODEL), lambda i: (i, 0)),
        out_shape=jax.ShapeDtypeStruct((m, D_MODEL), F32),
        compiler_params=pltpu.CompilerParams(dimension_semantics=("arbitrary",),
                                             vmem_limit_bytes=V7X_VMEM_LIMIT_BYTES),
        name="moe_combine_prompt" if row_off == 0 else "moe_combine_sample",
    )(contrib, contrib, x2, w_top, gf)


def _t5_bucket(dist):
    n = jnp.maximum(dist, 0)
    max_exact = N_BUCKETS // 2
    nf = jnp.maximum(n, 1).astype(F32)
    large = max_exact + (jnp.log(nf / max_exact) / math.log(MAX_DISTANCE / max_exact)
                         * (N_BUCKETS - max_exact)).astype(jnp.int32)
    large = jnp.minimum(large, N_BUCKETS - 1)
    return jnp.where(n < max_exact, n, large)


def _bucket_bias(rel_bias, dist):
    onehot = (_t5_bucket(dist)[..., None] == jnp.arange(N_BUCKETS, dtype=jnp.int32)).astype(F32)
    return jnp.dot(onehot, rel_bias.astype(F32), precision=lax.Precision.HIGHEST)


def _prompt_bias_table(rel_bias):
    qi = jnp.arange(ATTN_BLOCK, dtype=jnp.int32)[:, None]
    kj = jnp.arange(2 * ATTN_BLOCK, dtype=jnp.int32)[None, :] - ATTN_BLOCK
    dist = qi - kj
    valid = (dist >= 0) & (dist <= WINDOW)
    bias = jnp.where(valid[..., None], _bucket_bias(rel_bias, dist), NEG_BIG)
    return jnp.moveaxis(bias, -1, 0)


def _sample_bias_table(rel_bias, w_buf):
    j = jnp.arange(SAMPLE_KEYS, dtype=jnp.int32)
    dist = jnp.where(j < w_buf, w_buf - j, 0)
    valid = (j <= w_buf) & (dist <= WINDOW)
    return jnp.where(valid[:, None], _bucket_bias(rel_bias, dist), NEG_BIG).T


def kernel(x_prompt, x_sample, cache_conv, cache_k, cache_v, norm1_g, w_in, conv_w, w_conv_out, w_attn_out, w_o, sinks, rel_bias, norm2_g, w_router_group, b_router_group, w_router_expert, b_router_expert, w_e_gate, w_e_up, w_e_down, norm_f_g):
    assert norm1_g.shape[0] == 1, "single-layer configuration"
    batch, seq, _ = x_prompt.shape
    nseq = x_sample.shape[0]
    w_buf = cache_k.shape[2]
    mp = batch * seq
    m_total = mp + nseq
    assert seq % TM_DENSE == 0 and seq % ATTN_BLOCK == 0 and mp % COMBINE_BLOCK == 0
    assert nseq % SAMPLE_SEQ_PER_STEP == 0 and mp % nseq == 0 and w_buf + 1 <= SAMPLE_KEYS
    assert TOP_K == 2 and MOE_BLOCK == 1 << MOE_BLOCK_LOG2 and m_total * TOP_K < 1 << ASSIGN_BITS

    g1 = norm1_g[0][None, :]
    g2 = norm2_g[0][None, :]
    gf = norm_f_g[None, :]
    wi = w_in[0].astype(BF16)
    o0 = 0
    w_parts = []
    for width in (D_CONV, D_CONV, D_CONV, Q_DIM, 2 * KV_DIM, D_MODEL, D_MODEL):
        w_parts.append(wi[:, o0:o0 + width])
        o0 += width
    cw = conv_w[0]
    wc = w_conv_out[0].astype(BF16)
    wa = w_attn_out[0].astype(BF16)
    wo = w_o[0].astype(BF16)
    pad_cols = ROUTER_LANES - N_EXPERT_GROUPS - N_EXPERTS
    wr = jnp.concatenate([w_router_group[0], w_router_expert[0],
                          jnp.zeros((D_MODEL, pad_cols), F32)], axis=1).astype(BF16)
    br = jnp.concatenate([b_router_group[0], b_router_expert[0], jnp.zeros((pad_cols,), F32)])[None, :]
    sink = sinks[0].astype(F32)

    xp = x_prompt.reshape(mp, D_MODEL)
    bps = seq // TM_DENSE
    yc_p, q_p, k_p, v_p, sa_p, sb_p, ut_p, kvt_p = _in_proj(
        xp, g1, w_parts, cw, tm=TM_DENSE, blocks_per_seq=bps, u_tail=8, kv_tail=WINDOW)
    o_p = _attn_prompt(q_p, k_p, v_p, _prompt_bias_table(rel_bias), sink, batch, seq)

    pad_rows = lambda t: jnp.pad(t, ((0, TM_DENSE - nseq), (0, 0)))
    xs = pad_rows(x_sample.reshape(nseq, D_MODEL))
    hist = (pad_rows(cache_conv[0][:, 0, :]), pad_rows(cache_conv[0][:, 1, :]))
    yc_s, q_s, _, _, sa_s, sb_s, ut_s, kvt_s = _in_proj(
        xs, g1, w_parts, cw, tm=TM_DENSE, blocks_per_seq=1, u_tail=TM_DENSE, kv_tail=TM_DENSE, hist=hist,
        gate_dtype=F32)
    u_s = ut_s[0, :nseq]
    kv_s = kvt_s[0, :nseq]
    head_mask = (jnp.arange(KV_DIM)[None, :] // HEAD_DIM == jnp.arange(N_HEADS)[:, None] // GROUP)
    qbd = (jnp.tile(q_s[:nseq].reshape(nseq, N_HEADS, HEAD_DIM), (1, 1, N_KV_HEADS))
           * head_mask[None].astype(BF16))
    o_s = _attn_sample(qbd, cache_k[0].reshape(nseq, w_buf, KV_DIM), cache_v[0].reshape(nseq, w_buf, KV_DIM),
                       kv_s, _sample_bias_table(rel_bias, w_buf), sink[:, None], head_mask.astype(F32))
    o_s = pad_rows(o_s.reshape(nseq, Q_DIM))

    half = -(-m_total // COMBINE_BLOCK) * COMBINE_BLOCK
    assert half % nseq == 0 and TOP_K * half < 1 << ASSIGN_BITS
    x2, h2t, route, cnt = _out_proj((yc_p, o_p, sa_p, sb_p, xp), (yc_s, o_s, sa_s, sb_s, xs),
                                    wc, wa, wo, g2, wr, br, tm=TM_DENSE, valid_rows_b=nseq, half=half)

    n_assign = m_total * TOP_K
    keys = route[:m_total, 0:TOP_K].reshape(-1)
    w_top = lax.bitcast_convert_type(route[:m_total, TOP_K:2 * TOP_K], F32)
    counts = cnt[0, N_EXPERT_GROUPS:N_EXPERT_GROUPS + N_EXPERTS].astype(jnp.int32)
    order = jnp.pad(jnp.sort(keys) & ((1 << ASSIGN_BITS) - 1), (0, MOE_BLOCK))
    tok = jnp.where(order >= half, order - half, order)
    n_blocks = -(-n_assign // MOE_BLOCK) + N_EXPERTS
    bexp, bpos, bcnt, n_used = _block_tables(counts, n_blocks)
    contrib = _moe_experts(tok, order, bexp, bpos, bcnt, n_used, h2t, w_e_gate[0], w_e_up[0], w_e_down[0],
                           m_total=m_total, half=half)
    y_p = _moe_combine(contrib, x2, w_top, gf, row_off=0, m=mp, tc=COMBINE_BLOCK, half=half)
    y_s = _moe_combine(contrib, x2, w_top, gf, row_off=mp, m=nseq, tc=nseq, half=half)

    y_prompt = y_p.reshape(batch, seq, D_MODEL)
    y_sample = y_s.reshape(nseq, 1, D_MODEL)
    conv_state_prompt = ut_p.reshape(batch, bps, 8, D_CONV)[:, -1, 8 - (CONV_WIDTH - 1):, :][None]
    kv_last = kvt_p.reshape(batch, bps, WINDOW, 2 * KV_DIM)[:, -1]
    k_win_prompt = kv_last[:, :, :KV_DIM].reshape(batch, WINDOW, N_KV_HEADS, HEAD_DIM)[None]
    v_win_prompt = kv_last[:, :, KV_DIM:].reshape(batch, WINDOW, N_KV_HEADS, HEAD_DIM)[None]
    conv_state_sample = jnp.concatenate([cache_conv[0][:, 1:, :], u_s[:, None, :]], axis=1)[None]
    k_new = kv_s[:, :KV_DIM].reshape(nseq, 1, N_KV_HEADS, HEAD_DIM)
    v_new = kv_s[:, KV_DIM:].reshape(nseq, 1, N_KV_HEADS, HEAD_DIM)
    k_win_sample = jnp.concatenate([cache_k[0], k_new], axis=1)[:, -w_buf:][None]
    v_win_sample = jnp.concatenate([cache_v[0], v_new], axis=1)[:, -w_buf:][None]
    return (y_prompt, y_sample, conv_state_prompt, k_win_prompt, v_win_prompt,
            conv_state_sample, k_win_sample, v_win_sample)
```

```python
import functools
import math

import jax
import jax.numpy as jnp
from jax import lax
from jax.experimental import pallas as pl
from jax.experimental.pallas import tpu as pltpu

D_MODEL = 1024
D_CONV = 1024
CONV_WIDTH = 3
N_HEADS = 16
N_KV_HEADS = 4
HEAD_DIM = 64
GROUP = N_HEADS // N_KV_HEADS
WINDOW = 128
Q_DIM = N_HEADS * HEAD_DIM
KV_DIM = N_KV_HEADS * HEAD_DIM
N_BUCKETS = 32
MAX_DISTANCE = 128
N_EXPERT_GROUPS = 4
EXPERTS_PER_GROUP = 8
N_EXPERTS = N_EXPERT_GROUPS * EXPERTS_PER_GROUP
TOP_K = 2
D_EXPERT = 512
EPS = 1e-6
PAST_LEN = 8192

BF16 = jnp.bfloat16
F32 = jnp.float32
NEG_BIG = -1e30

V7X_VMEM_LIMIT_BYTES = 56 * 1024 * 1024
TILE_ROWS = 8
LANES = 128
ROUTER_LANES = 128
TM_DENSE = 512
ATTN_BLOCK = 128
MOE_BLOCK = 256
MOE_BLOCK_LOG2 = 8
ASSIGN_BITS = 16
COMBINE_BLOCK = 256
SAMPLE_KEYS = 256
SAMPLE_SEQ_PER_STEP = 8


def _const_spec(shape):
    nd = len(shape)
    return pl.BlockSpec(shape, lambda *_: (0,) * nd, pipeline_mode=pl.Buffered(1))


def _rms_norm_f32(xf, g):
    return xf * lax.rsqrt(jnp.mean(xf * xf, axis=-1, keepdims=True) + EPS) * g


def _in_proj_kernel(*refs, tm, sample, blocks_per_seq, u_tail, kv_tail):
    if sample:
        (x_ref, hist0_ref, hist1_ref, g_ref, wcb_ref, wcc_ref, wch_ref, wq_ref, wkv_ref, wga_ref, wgb_ref,
         cw_ref, yc_ref, q_ref, k_ref, v_ref, sa_ref, sb_ref, ut_ref, kvt_ref) = refs
    else:
        (x_ref, g_ref, wcb_ref, wcc_ref, wch_ref, wq_ref, wkv_ref, wga_ref, wgb_ref,
         cw_ref, yc_ref, q_ref, k_ref, v_ref, sa_ref, sb_ref, ut_ref, kvt_ref, ubuf_ref) = refs

    h = _rms_norm_f32(x_ref[...], g_ref[...]).astype(BF16)

    def proj(w_ref):
        return jnp.dot(h, w_ref[...], preferred_element_type=F32)

    u = proj(wcc_ref) * proj(wch_ref)
    w0 = cw_ref[0:1, :]
    w1 = cw_ref[1:2, :]
    w2 = cw_ref[2:3, :]
    if sample:
        conv = w0 * hist0_ref[...] + w1 * hist1_ref[...] + w2 * u
    else:
        @pl.when(pl.program_id(0) % blocks_per_seq == 0)
        def _():
            ubuf_ref[0:8, :] = jnp.zeros((8, D_CONV), F32)

        ubuf_ref[8:8 + tm, :] = u
        conv = w0 * ubuf_ref[6:6 + tm, :] + w1 * ubuf_ref[7:7 + tm, :] + w2 * u
        ubuf_ref[0:8, :] = u[tm - 8:, :]
    yc_ref[...] = (proj(wcb_ref) * conv).astype(BF16)
    ut_ref[0] = u[tm - u_tail:, :]

    q_ref[...] = (proj(wq_ref) * (HEAD_DIM ** -0.5)).astype(BF16)
    kv = proj(wkv_ref)
    k_ref[...] = kv[:, :KV_DIM].astype(BF16)
    v_ref[...] = kv[:, KV_DIM:].astype(BF16)
    kvt_ref[0] = kv[tm - kv_tail:, :]
    sa_ref[...] = jax.nn.sigmoid(proj(wga_ref)).astype(sa_ref.dtype)
    sb_ref[...] = jax.nn.sigmoid(proj(wgb_ref)).astype(sb_ref.dtype)


def _in_proj(x, g1, w_parts, conv_w, *, tm, blocks_per_seq, u_tail, kv_tail, hist=None, gate_dtype=BF16):
    m = x.shape[0]
    nblk = m // tm
    sample = hist is not None
    row = lambda width: pl.BlockSpec((tm, width), lambda i: (i, 0))
    in_specs = [row(D_MODEL)]
    args = [x]
    if sample:
        in_specs += [row(D_CONV), row(D_CONV)]
        args += list(hist)
    in_specs += [_const_spec((1, D_MODEL))] + [_const_spec(w.shape) for w in w_parts] + [_const_spec(conv_w.shape)]
    args += [g1] + list(w_parts) + [conv_w]
    out_shape = [
        jax.ShapeDtypeStruct((m, D_CONV), BF16),
        jax.ShapeDtypeStruct((m, Q_DIM), BF16),
        jax.ShapeDtypeStruct((m, KV_DIM), BF16),
        jax.ShapeDtypeStruct((m, KV_DIM), BF16),
        jax.ShapeDtypeStruct((m, D_MODEL), gate_dtype),
        jax.ShapeDtypeStruct((m, D_MODEL), gate_dtype),
        jax.ShapeDtypeStruct((nblk, u_tail, D_CONV), F32),
        jax.ShapeDtypeStruct((nblk, kv_tail, 2 * KV_DIM), F32),
    ]
    out_specs = [row(D_CONV), row(Q_DIM), row(KV_DIM), row(KV_DIM), row(D_MODEL), row(D_MODEL),
                 pl.BlockSpec((1, u_tail, D_CONV), lambda i: (i, 0, 0)),
                 pl.BlockSpec((1, kv_tail, 2 * KV_DIM), lambda i: (i, 0, 0))]
    scratch = [] if sample else [pltpu.VMEM((tm + 8, D_CONV), F32)]
    return pl.pallas_call(
        functools.partial(_in_proj_kernel, tm=tm, sample=sample, blocks_per_seq=blocks_per_seq,
                          u_tail=u_tail, kv_tail=kv_tail),
        grid=(nblk,),
        in_specs=in_specs,
        out_specs=out_specs,
        out_shape=out_shape,
        scratch_shapes=scratch,
        compiler_params=pltpu.CompilerParams(dimension_semantics=("arbitrary",),
                                             vmem_limit_bytes=V7X_VMEM_LIMIT_BYTES),
        name="in_proj_sample" if sample else "in_proj_prompt",
    )(*args)


def _attn_prompt_kernel(sink_ref, q_ref, kc_ref, kp_ref, vc_ref, vp_ref, bias_ref, o_ref):
    first = pl.program_id(1) == 0
    col = lax.broadcasted_iota(jnp.int32, (ATTN_BLOCK, 2 * ATTN_BLOCK), 1)
    no_prev = jnp.logical_and(first, col < ATTN_BLOCK)
    for g in range(N_KV_HEADS):
        ks = slice(g * HEAD_DIM, (g + 1) * HEAD_DIM)
        kcat = jnp.concatenate([kp_ref[:, ks], kc_ref[:, ks]], axis=0)
        vcat = jnp.concatenate([vp_ref[:, ks], vc_ref[:, ks]], axis=0)
        for hh in range(GROUP):
            h = g * GROUP + hh
            hs = slice(h * HEAD_DIM, (h + 1) * HEAD_DIM)
            s = lax.dot_general(q_ref[:, hs], kcat, (((1,), (1,)), ((), ())),
                                preferred_element_type=F32)
            s = jnp.where(no_prev, NEG_BIG, s + bias_ref[h])
            sink = sink_ref[h]
            m = jnp.maximum(jnp.max(s, axis=-1, keepdims=True), sink)
            p = jnp.exp(s - m)
            denom = jnp.sum(p, axis=-1, keepdims=True) + jnp.exp(sink - m)
            o = jnp.dot(p.astype(BF16), vcat, preferred_element_type=F32)
            o_ref[:, hs] = (o / denom).astype(BF16)


def _attn_prompt(q, k, v, bias, sinks, batch, seq):
    nb = seq // ATTN_BLOCK
    cur = lambda b, i: (b * nb + i, 0)
    prev = lambda b, i: (b * nb + jnp.maximum(i - 1, 0), 0)
    return pl.pallas_call(
        _attn_prompt_kernel,
        grid=(batch, nb),
        in_specs=[pl.BlockSpec(memory_space=pltpu.SMEM),
                  pl.BlockSpec((ATTN_BLOCK, Q_DIM), cur),
                  pl.BlockSpec((ATTN_BLOCK, KV_DIM), cur),
                  pl.BlockSpec((ATTN_BLOCK, KV_DIM), prev),
                  pl.BlockSpec((ATTN_BLOCK, KV_DIM), cur),
                  pl.BlockSpec((ATTN_BLOCK, KV_DIM), prev),
                  _const_spec(bias.shape)],
        out_specs=pl.BlockSpec((ATTN_BLOCK, Q_DIM), cur),
        out_shape=jax.ShapeDtypeStruct((batch * seq, Q_DIM), BF16),
        compiler_params=pltpu.CompilerParams(dimension_semantics=("arbitrary", "arbitrary"),
                                             vmem_limit_bytes=V7X_VMEM_LIMIT_BYTES),
        name="attn_prompt",
    )(sinks, q, k, k, v, v, bias)


def _attn_sample_kernel(qbd_ref, ck_ref, cv_ref, kvn_ref, bias_ref, sink_ref, mask_ref, o_ref, *, w_buf):
    pad = jnp.zeros((SAMPLE_KEYS - w_buf, KV_DIM), F32)
    is_new = lax.broadcasted_iota(jnp.int32, (SAMPLE_KEYS, KV_DIM), 0) == w_buf
    for b in range(SAMPLE_SEQ_PER_STEP):
        kall = jnp.where(is_new, kvn_ref[b:b + 1, :KV_DIM],
                         jnp.concatenate([ck_ref[b], pad], axis=0)).astype(BF16)
        vall = jnp.where(is_new, kvn_ref[b:b + 1, KV_DIM:],
                         jnp.concatenate([cv_ref[b], pad], axis=0)).astype(BF16)
        s = lax.dot_general(qbd_ref[b], kall, (((1,), (1,)), ((), ())),
                            preferred_element_type=F32)
        s = s + bias_ref[...]
        sink = sink_ref[...]
        m = jnp.maximum(jnp.max(s, axis=-1, keepdims=True), sink)
        p = jnp.exp(s - m)
        p = p / (jnp.sum(p, axis=-1, keepdims=True) + jnp.exp(sink - m))
        of = jnp.dot(p.astype(BF16), vall, preferred_element_type=F32)
        of = of * mask_ref[...]
        o_ref[b] = (of[:, 0:HEAD_DIM] + of[:, HEAD_DIM:2 * HEAD_DIM]
                    + of[:, 2 * HEAD_DIM:3 * HEAD_DIM] + of[:, 3 * HEAD_DIM:]).astype(BF16)


def _attn_sample(qbd, ck, cv, kvn, bias, sink_col, head_mask):
    nseq, w_buf = ck.shape[0], ck.shape[1]
    sb = SAMPLE_SEQ_PER_STEP
    return pl.pallas_call(
        functools.partial(_attn_sample_kernel, w_buf=w_buf),
        grid=(nseq // sb,),
        in_specs=[pl.BlockSpec((sb, N_HEADS, KV_DIM), lambda i: (i, 0, 0)),
                  pl.BlockSpec((sb, w_buf, KV_DIM), lambda i: (i, 0, 0)),
                  pl.BlockSpec((sb, w_buf, KV_DIM), lambda i: (i, 0, 0)),
                  pl.BlockSpec((sb, 2 * KV_DIM), lambda i: (i, 0)),
                  _const_spec(bias.shape), _const_spec(sink_col.shape), _const_spec(head_mask.shape)],
        out_specs=pl.BlockSpec((sb, N_HEADS, HEAD_DIM), lambda i: (i, 0, 0)),
        out_shape=jax.ShapeDtypeStruct((nseq, N_HEADS, HEAD_DIM), BF16),
        compiler_params=pltpu.CompilerParams(dimension_semantics=("arbitrary",),
                                             vmem_limit_bytes=V7X_VMEM_LIMIT_BYTES),
        name="attn_sample",
    )(qbd, ck, cv, kvn, bias, sink_col, head_mask)


def _route_rows(logits, row0, valid_rows, half):
    tm = logits.shape[0]
    lane = lax.broadcasted_iota(jnp.int32, logits.shape, 1)
    lane_f = lane.astype(F32)
    no_lane = float(ROUTER_LANES)

    def top1(mask):
        best = jnp.max(jnp.where(mask, logits, -jnp.inf), axis=-1, keepdims=True)
        idx = jnp.min(jnp.where(jnp.logical_and(mask, logits == best), lane_f, no_lane), axis=-1, keepdims=True)
        return best, idx

    gmask = lane < N_EXPERT_GROUPS
    gmax, grp = top1(gmask)
    gsum = jnp.sum(jnp.where(gmask, jnp.exp(logits - gmax), 0.0), axis=-1, keepdims=True)
    p_grp = 1.0 / gsum
    lo = N_EXPERT_GROUPS + EXPERTS_PER_GROUP * grp
    emask = jnp.logical_and(lane_f >= lo, lane_f < lo + EXPERTS_PER_GROUP)
    v1, i1 = top1(emask)
    v2, i2 = top1(jnp.logical_and(emask, lane_f != i1))
    e21 = jnp.exp(v2 - v1)
    w1 = p_grp / (1.0 + e21)
    w2 = p_grp * e21 / (1.0 + e21)

    oh1 = lane_f == i1
    oh2 = lane_f == i2
    if valid_rows < tm:
        valid = lax.broadcasted_iota(jnp.int32, logits.shape, 0) < valid_rows
        oh1 = jnp.logical_and(oh1, valid)
        oh2 = jnp.logical_and(oh2, valid)
    oh = oh1.astype(F32) + oh2.astype(F32)
    token = row0 + lax.broadcasted_iota(jnp.int32, (tm, 1), 0)
    key1 = (i1.astype(jnp.int32) - N_EXPERT_GROUPS) * (1 << ASSIGN_BITS) + token
    key2 = (i2.astype(jnp.int32) - N_EXPERT_GROUPS) * (1 << ASSIGN_BITS) + token + half
    w1b = lax.bitcast_convert_type(w1, jnp.int32)
    w2b = lax.bitcast_convert_type(w2, jnp.int32)
    words = jnp.where(lane == 0, key1, jnp.where(lane == 1, key2, jnp.where(lane == 2, w1b,
                      jnp.where(lane == 3, w2b, 0))))
    return words, jnp.sum(oh, axis=0, keepdims=True)


def _store_token_tiles(ref, x):
    n = x.shape[0]
    for c in range(D_MODEL // LANES):
        ref[pl.ds(c, n, stride=TILE_ROWS), :] = x[:, c * LANES:(c + 1) * LANES]


def _load_token_tiles(ref, n):
    return jnp.concatenate([ref[pl.ds(c, n, stride=TILE_ROWS), :] for c in range(D_MODEL // LANES)], axis=1)


def _out_proj_rows(yc_ref, o_ref, sa_ref, sb_ref, x_ref, wc_ref, wa_ref, wo_ref, g2_ref, wr_ref, br_ref,
                   x2_ref, h2t_ref, route_ref, cnt_ref, *, valid_rows, half):
    y_conv = jnp.dot(yc_ref[...], wc_ref[...], preferred_element_type=F32)
    y_attn = jnp.dot(o_ref[...], wa_ref[...], preferred_element_type=F32)
    mix = (sa_ref[...].astype(F32) * y_conv + sb_ref[...].astype(F32) * y_attn).astype(BF16)
    x2 = x_ref[...] + jnp.dot(mix, wo_ref[...], preferred_element_type=F32)
    x2_ref[...] = x2
    h2 = _rms_norm_f32(x2, g2_ref[...])
    _store_token_tiles(h2t_ref, h2)
    logits = jnp.dot(h2.astype(BF16), wr_ref[...], preferred_element_type=F32) + br_ref[...]
    words, cnt = _route_rows(logits, pl.program_id(0) * x_ref.shape[0], valid_rows, half)
    route_ref[...] = words
    cnt_ref[...] += cnt


def _out_proj_kernel(*refs, n_first, valid_rows_second, half):
    first, second, shared = refs[0:5], refs[5:10], refs[10:]
    cnt_ref = shared[-1]
    tm = first[4].shape[0]

    @pl.when(pl.program_id(0) == 0)
    def _():
        cnt_ref[...] = jnp.zeros_like(cnt_ref)

    @pl.when(pl.program_id(0) < n_first)
    def _():
        _out_proj_rows(*first, *shared, valid_rows=tm, half=half)

    @pl.when(pl.program_id(0) >= n_first)
    def _():
        _out_proj_rows(*second, *shared, valid_rows=valid_rows_second, half=half)


def _out_proj(acts_a, acts_b, wc, wa, wo, g2, wr, br, *, tm, valid_rows_b, half):
    na = acts_a[4].shape[0] // tm
    nb = acts_b[4].shape[0] // tm
    assert nb == 1
    m_total = (na + nb) * tm
    spec_a = lambda width: pl.BlockSpec((tm, width), lambda i: (jnp.minimum(i, na - 1), 0))
    spec_b = lambda width: pl.BlockSpec((tm, width), lambda i: (jnp.maximum(i - na, 0), 0))
    widths = (D_CONV, Q_DIM, D_MODEL, D_MODEL, D_MODEL)
    in_specs = [spec_a(w) for w in widths] + [spec_b(w) for w in widths]
    in_specs += [_const_spec(wc.shape), _const_spec(wa.shape), _const_spec(wo.shape),
                 _const_spec(g2.shape), _const_spec(wr.shape), _const_spec(br.shape)]
    orow = lambda width: pl.BlockSpec((tm, width), lambda i: (i, 0))
    return pl.pallas_call(
        functools.partial(_out_proj_kernel, n_first=na, valid_rows_second=valid_rows_b, half=half),
        grid=(na + nb,),
        in_specs=in_specs,
        out_specs=[orow(D_MODEL), pl.BlockSpec((tm * TILE_ROWS, LANES), lambda i: (i, 0)), orow(ROUTER_LANES),
                   pl.BlockSpec((1, ROUTER_LANES), lambda i: (0, 0))],
        out_shape=[jax.ShapeDtypeStruct((m_total, D_MODEL), F32),
                   jax.ShapeDtypeStruct((m_total * TILE_ROWS, LANES), F32),
                   jax.ShapeDtypeStruct((m_total, ROUTER_LANES), jnp.int32),
                   jax.ShapeDtypeStruct((1, ROUTER_LANES), F32)],
        compiler_params=pltpu.CompilerParams(dimension_semantics=("arbitrary",),
                                             vmem_limit_bytes=V7X_VMEM_LIMIT_BYTES),
        name="out_proj",
    )(*acts_a, *acts_b, wc, wa, wo, g2, wr, br)


def _block_table_kernel(counts_ref, bexp_ref, bpos_ref, bcnt_ref, nused_ref, *, n_blocks):
    def per_expert(e, carry):
        blk0, pos0 = carry
        cnt = counts_ref[e]
        nblk = lax.shift_right_logical(cnt + (MOE_BLOCK - 1), MOE_BLOCK_LOG2)

        def mark(b, c):
            off = (b - blk0) * MOE_BLOCK
            bexp_ref[b] = e
            bpos_ref[b] = pos0 + off
            bcnt_ref[b] = jnp.minimum(cnt - off, MOE_BLOCK)
            return c
        lax.fori_loop(blk0, blk0 + nblk, mark, 0)
        return blk0 + nblk, pos0 + cnt

    n_used, _ = lax.fori_loop(0, N_EXPERTS, per_expert, (0, 0))
    nused_ref[0] = n_used

    def unused(b, c):
        bexp_ref[b] = N_EXPERTS - 1
        bpos_ref[b] = 0
        bcnt_ref[b] = 0
        return c
    lax.fori_loop(n_used, n_blocks, unused, 0)


def _block_tables(counts, n_blocks):
    smem = pl.BlockSpec(memory_space=pltpu.SMEM)
    blk = jax.ShapeDtypeStruct((n_blocks,), jnp.int32)
    return pl.pallas_call(
        functools.partial(_block_table_kernel, n_blocks=n_blocks),
        in_specs=[smem],
        out_specs=[smem, smem, smem, smem],
        out_shape=[blk, blk, blk, jax.ShapeDtypeStruct((1,), jnp.int32)],
        name="block_tables",
    )(counts)


def _moe_kernel(tok_ref, order_ref, bexp_ref, bpos_ref, bcnt_ref, nused_ref, h2t_hbm, wg_ref, wu_ref, wd_ref,
                contrib_hbm, xs_ref, ys_ref, wgb_ref, wub_ref, wdb_ref, gsem_ref, ssem_ref,
                *, m_total, half, n_blocks):
    i = pl.program_id(0)
    slot = i % 2
    n_used = nused_ref[0]
    active = i < n_used
    last_active = i == n_used - 1
    tile = lambda t: pl.ds(pl.multiple_of(t * TILE_ROWS, TILE_ROWS), TILE_ROWS)

    def gather(pos0, slt, r):
        return pltpu.make_async_copy(h2t_hbm.at[tile(tok_ref[pos0 + r]), :], xs_ref.at[slt, tile(r), :],
                                     gsem_ref.at[slt])

    def gather_wait(slt, r):
        pltpu.make_async_copy(h2t_hbm.at[tile(0), :], xs_ref.at[slt, tile(r), :], gsem_ref.at[slt]).wait()

    def scatter(pos0, cnt, trash0, slt, r):
        dst = jnp.where(r < cnt, order_ref[pos0 + r], trash0 + r)
        return pltpu.make_async_copy(ys_ref.at[slt, tile(r), :], contrib_hbm.at[tile(dst), :], ssem_ref.at[slt])

    def scatter_wait(slt, r):
        pltpu.make_async_copy(ys_ref.at[slt, tile(r), :], contrib_hbm.at[tile(0), :], ssem_ref.at[slt]).wait()

    @pl.when(i == 0)
    def _():
        ys_ref[...] = jnp.zeros_like(ys_ref)
        gap = half - m_total
        fills = [(2 * half + s * MOE_BLOCK, MOE_BLOCK) for s in range(2)]
        fills += [(k * half + m_total, gap) for k in range(TOP_K)] if gap else []
        for start, n in fills:
            fill = pltpu.make_async_copy(ys_ref.at[0, pl.ds(0, n * TILE_ROWS), :],
                                         contrib_hbm.at[pl.ds(start * TILE_ROWS, n * TILE_ROWS), :], ssem_ref.at[0])
            fill.start()
            fill.wait()

    @pl.when(jnp.logical_and(i == 0, active))
    def _():
        pos0 = bpos_ref[0]
        for r in range(MOE_BLOCK):
            gather(pos0, 0, r).start()

    @pl.when(active)
    def _():
        for r in range(MOE_BLOCK):
            gather_wait(slot, r)

    @pl.when(jnp.logical_and(active, i >= 2))
    def _():
        for r in range(MOE_BLOCK):
            scatter_wait(slot, r)

    @pl.when(jnp.logical_or(i == 0, bexp_ref[i] != bexp_ref[jnp.maximum(i - 1, 0)]))
    def _():
        wgb_ref[...] = wg_ref[0].astype(BF16)
        wub_ref[...] = wu_ref[0].astype(BF16)
        wdb_ref[...] = wd_ref[0].astype(BF16)

    def run_block(slt):
        pos_next = bpos_ref[jnp.minimum(i + 1, n_blocks - 1)]
        pos0 = bpos_ref[i]
        cnt = bcnt_ref[i]
        trash0 = 2 * half + slt * MOE_BLOCK
        for r in range(MOE_BLOCK):
            gather(pos_next, 1 - slt, r).start(priority=1)
        xb = _load_token_tiles(xs_ref.at[slt], MOE_BLOCK).astype(BF16)
        gate = jnp.dot(xb, wgb_ref[...], preferred_element_type=F32)
        up = jnp.dot(xb, wub_ref[...], preferred_element_type=F32)
        hmid = (jax.nn.silu(gate) * up).astype(BF16)
        _store_token_tiles(ys_ref.at[slt], jnp.dot(hmid, wdb_ref[...], preferred_element_type=F32))
        for r in range(MOE_BLOCK):
            scatter(pos0, cnt, trash0, slt, r).start(priority=r % 2)

    for slt in range(2):
        pl.when(jnp.logical_and(active, slot == slt))(functools.partial(run_block, slt))

    @pl.when(last_active)
    def _():
        for r in range(MOE_BLOCK):
            gather_wait(1 - slot, r)
        for r in range(MOE_BLOCK):
            scatter_wait(slot, r)

    @pl.when(jnp.logical_and(last_active, i >= 1))
    def _():
        for r in range(MOE_BLOCK):
            scatter_wait(1 - slot, r)


def _moe_experts(tok, order, bexp, bpos, bcnt, n_used, h2t, w_gate, w_up, w_down, *, m_total, half):
    n_blocks = bexp.shape[0]
    assert 0 <= half - m_total <= MOE_BLOCK
    wspec = lambda shape: pl.BlockSpec((1,) + shape, lambda i, tok, order, be, *_: (be[i], 0, 0))
    grid_spec = pltpu.PrefetchScalarGridSpec(
        num_scalar_prefetch=6,
        grid=(n_blocks,),
        in_specs=[pl.BlockSpec(memory_space=pl.ANY),
                  wspec((D_MODEL, D_EXPERT)), wspec((D_MODEL, D_EXPERT)), wspec((D_EXPERT, D_MODEL))],
        out_specs=pl.BlockSpec(memory_space=pl.ANY),
        scratch_shapes=[pltpu.VMEM((2, MOE_BLOCK * TILE_ROWS, LANES), F32),
                        pltpu.VMEM((2, MOE_BLOCK * TILE_ROWS, LANES), F32),
                        pltpu.VMEM((D_MODEL, D_EXPERT), BF16),
                        pltpu.VMEM((D_MODEL, D_EXPERT), BF16),
                        pltpu.VMEM((D_EXPERT, D_MODEL), BF16),
                        pltpu.SemaphoreType.DMA((2,)),
                        pltpu.SemaphoreType.DMA((2,))],
    )
    return pl.pallas_call(
        functools.partial(_moe_kernel, m_total=m_total, half=half, n_blocks=n_blocks),
        grid_spec=grid_spec,
        out_shape=jax.ShapeDtypeStruct(((2 * half + 2 * MOE_BLOCK) * TILE_ROWS, LANES), F32),
        compiler_params=pltpu.CompilerParams(dimension_semantics=("arbitrary",),
                                             vmem_limit_bytes=V7X_VMEM_LIMIT_BYTES),
        name="moe_experts",
    )(tok, order, bexp, bpos, bcnt, n_used, h2t, w_gate, w_up, w_down)


def _combine_kernel(c0_ref, c1_ref, x2_ref, w_ref, gf_ref, y_ref):
    tc = x2_ref.shape[0]
    w = w_ref[...]
    moe = w[:, 0:1] * _load_token_tiles(c0_ref, tc) + w[:, 1:2] * _load_token_tiles(c1_ref, tc)
    y_ref[...] = _rms_norm_f32(x2_ref[...] + moe, gf_ref[...])


def _moe_combine(contrib, x2, w_top, gf, *, row_off, m, tc, half):
    off = row_off // tc
    assert row_off % tc == 0 and half % tc == 0
    ctile = lambda k: pl.BlockSpec((tc * TILE_ROWS, LANES), lambda i: (i + off + k * (half // tc), 0))
    return pl.pallas_call(
        _combine_kernel,
        grid=(m // tc,),
        in_specs=[ctile(0), ctile(1),
                  pl.BlockSpec((tc, D_MODEL), lambda i: (i + off, 0)),
                  pl.BlockSpec((tc, TOP_K), lambda i: (i + off, 0)),
                  _const_spec((1, D_MODEL))],
        out_specs=pl.BlockSpec((tc, D_MODEL), lambda i: (i, 0)),
        out_shape=jax.ShapeDtypeStruct((m, D_MODEL), F32),
        compiler_params=pltpu.CompilerParams(dimension_semantics=("arbitrary",),
                                             vmem_limit_bytes=V7X_VMEM_LIMIT_BYTES),
        name="moe_combine_prompt" if row_off == 0 else "moe_combine_sample",
    )(contrib, contrib, x2, w_top, gf)


def _t5_bucket(dist):
    n = jnp.maximum(dist, 0)
    max_exact = N_BUCKETS // 2
    nf = jnp.maximum(n, 1).astype(F32)
    large = max_exact + (jnp.log(nf / max_exact) / math.log(MAX_DISTANCE / max_exact)
                         * (N_BUCKETS - max_exact)).astype(jnp.int32)
    large = jnp.minimum(large, N_BUCKETS - 1)
    return jnp.where(n < max_exact, n, large)


def _bucket_bias(rel_bias, dist):
    onehot = (_t5_bucket(dist)[..., None] == jnp.arange(N_BUCKETS, dtype=jnp.int32)).astype(F32)
    return jnp.dot(onehot, rel_bias.astype(F32), precision=lax.Precision.HIGHEST)


def _prompt_bias_table(rel_bias):
    qi = jnp.arange(ATTN_BLOCK, dtype=jnp.int32)[:, None]
    kj = jnp.arange(2 * ATTN_BLOCK, dtype=jnp.int32)[None, :] - ATTN_BLOCK
    dist = qi - kj
    valid = (dist >= 0) & (dist <= WINDOW)
    bias = jnp.where(valid[..., None], _bucket_bias(rel_bias, dist), NEG_BIG)
    return jnp.moveaxis(bias, -1, 0)


def _sample_bias_table(rel_bias, w_buf):
    j = jnp.arange(SAMPLE_KEYS, dtype=jnp.int32)
    dist = jnp.where(j < w_buf, w_buf - j, 0)
    valid = (j <= w_buf) & (dist <= WINDOW)
    return jnp.where(valid[:, None], _bucket_bias(rel_bias, dist), NEG_BIG).T


def kernel(x_prompt, x_sample, cache_conv, cache_k, cache_v, norm1_g, w_in, conv_w, w_conv_out, w_attn_out, w_o, sinks, rel_bias, norm2_g, w_router_group, b_router_group, w_router_expert, b_router_expert, w_e_gate, w_e_up, w_e_down, norm_f_g):
    assert norm1_g.shape[0] == 1, "single-layer configuration"
    batch, seq, _ = x_prompt.shape
    nseq = x_sample.shape[0]
    w_buf = cache_k.shape[2]
    mp = batch * seq
    m_total = mp + nseq
    assert seq % TM_DENSE == 0 and seq % ATTN_BLOCK == 0 and mp % COMBINE_BLOCK == 0
    assert nseq % SAMPLE_SEQ_PER_STEP == 0 and mp % nseq == 0 and w_buf + 1 <= SAMPLE_KEYS
    assert TOP_K == 2 and MOE_BLOCK == 1 << MOE_BLOCK_LOG2 and m_total * TOP_K < 1 << ASSIGN_BITS

    g1 = norm1_g[0][None, :]
    g2 = norm2_g[0][None, :]
    gf = norm_f_g[None, :]
    wi = w_in[0].astype(BF16)
    o0 = 0
    w_parts = []
    for width in (D_CONV, D_CONV, D_CONV, Q_DIM, 2 * KV_DIM, D_MODEL, D_MODEL):
        w_parts.append(wi[:, o0:o0 + width])
        o0 += width
    cw = conv_w[0]
    wc = w_conv_out[0].astype(BF16)
    wa = w_attn_out[0].astype(BF16)
    wo = w_o[0].astype(BF16)
    pad_cols = ROUTER_LANES - N_EXPERT_GROUPS - N_EXPERTS
    wr = jnp.concatenate([w_router_group[0], w_router_expert[0],
                          jnp.zeros((D_MODEL, pad_cols), F32)], axis=1).astype(BF16)
    br = jnp.concatenate([b_router_group[0], b_router_expert[0], jnp.zeros((pad_cols,), F32)])[None, :]
    sink = sinks[0].astype(F32)

    xp = x_prompt.reshape(mp, D_MODEL)
    bps = seq // TM_DENSE
    yc_p, q_p, k_p, v_p, sa_p, sb_p, ut_p, kvt_p = _in_proj(
        xp, g1, w_parts, cw, tm=TM_DENSE, blocks_per_seq=bps, u_tail=8, kv_tail=WINDOW)
    o_p = _attn_prompt(q_p, k_p, v_p, _prompt_bias_table(rel_bias), sink, batch, seq)

    pad_rows = lambda t: jnp.pad(t, ((0, TM_DENSE - nseq), (0, 0)))
    xs = pad_rows(x_sample.reshape(nseq, D_MODEL))
    hist = (pad_rows(cache_conv[0][:, 0, :]), pad_rows(cache_conv[0][:, 1, :]))
    yc_s, q_s, _, _, sa_s, sb_s, ut_s, kvt_s = _in_proj(
        xs, g1, w_parts, cw, tm=TM_DENSE, blocks_per_seq=1, u_tail=TM_DENSE, kv_tail=TM_DENSE, hist=hist,
        gate_dtype=F32)
    u_s = ut_s[0, :nseq]
    kv_s = kvt_s[0, :nseq]
    head_mask = (jnp.arange(KV_DIM)[None, :] // HEAD_DIM == jnp.arange(N_HEADS)[:, None] // GROUP)
    qbd = (jnp.tile(q_s[:nseq].reshape(nseq, N_HEADS, HEAD_DIM), (1, 1, N_KV_HEADS))
           * head_mask[None].astype(BF16))
    o_s = _attn_sample(qbd, cache_k[0].reshape(nseq, w_buf, KV_DIM), cache_v[0].reshape(nseq, w_buf, KV_DIM),
                       kv_s, _sample_bias_table(rel_bias, w_buf), sink[:, None], head_mask.astype(F32))
    o_s = pad_rows(o_s.reshape(nseq, Q_DIM))

    half = -(-m_total // COMBINE_BLOCK) * COMBINE_BLOCK
    assert half % nseq == 0 and TOP_K * half < 1 << ASSIGN_BITS
    x2, h2t, route, cnt = _out_proj((yc_p, o_p, sa_p, sb_p, xp), (yc_s, o_s, sa_s, sb_s, xs),
                                    wc, wa, wo, g2, wr, br, tm=TM_DENSE, valid_rows_b=nseq, half=half)

    n_assign = m_total * TOP_K
    keys = route[:m_total, 0:TOP_K].reshape(-1)
    w_top = lax.bitcast_convert_type(route[:m_total, TOP_K:2 * TOP_K], F32)
    counts = cnt[0, N_EXPERT_GROUPS:N_EXPERT_GROUPS + N_EXPERTS].astype(jnp.int32)
    order = jnp.pad(jnp.sort(keys) & ((1 << ASSIGN_BITS) - 1), (0, MOE_BLOCK))
    tok = jnp.where(order >= half, order - half, order)
    n_blocks = -(-n_assign // MOE_BLOCK) + N_EXPERTS
    bexp, bpos, bcnt, n_used = _block_tables(counts, n_blocks)
    contrib = _moe_experts(tok, order, bexp, bpos, bcnt, n_used, h2t, w_e_gate[0], w_e_up[0], w_e_down[0],
                           m_total=m_total, half=half)
    y_p = _moe_combine(contrib, x2, w_top, gf, row_off=0, m=mp, tc=COMBINE_BLOCK, half=half)
    y_s = _moe_combine(contrib, x2, w_top, gf, row_off=mp, m=nseq, tc=nseq, half=half)

    y_prompt = y_p.reshape(batch, seq, D_MODEL)
    y_sample = y_s.reshape(nseq, 1, D_MODEL)
    conv_state_prompt = ut_p.reshape(batch, bps, 8, D_CONV)[:, -1, 8 - (CONV_WIDTH - 1):, :][None]
    kv_last = kvt_p.reshape(batch, bps, WINDOW, 2 * KV_DIM)[:, -1]
    k_win_prompt = kv_last[:, :, :KV_DIM].reshape(batch, WINDOW, N_KV_HEADS, HEAD_DIM)[None]
    v_win_prompt = kv_last[:, :, KV_DIM:].reshape(batch, WINDOW, N_KV_HEADS, HEAD_DIM)[None]
    conv_state_sample = jnp.concatenate([cache_conv[0][:, 1:, :], u_s[:, None, :]], axis=1)[None]
    k_new = kv_s[:, :KV_DIM].reshape(nseq, 1, N_KV_HEADS, HEAD_DIM)
    v_new = kv_s[:, KV_DIM:].reshape(nseq, 1, N_KV_HEADS, HEAD_DIM)
    k_win_sample = jnp.concatenate([cache_k[0], k_new], axis=1)[:, -w_buf:][None]
    v_win_sample = jnp.concatenate([cache_v[0], v_new], axis=1)[:, -w_buf:][None]
    return (y_prompt, y_sample, conv_state_prompt, k_win_prompt, v_win_prompt,
            conv_state_sample, k_win_sample, v_win_sample)
```

```python
import functools
import math

import jax
import jax.numpy as jnp
from jax import lax
from jax.experimental import pallas as pl
from jax.experimental.pallas import tpu as pltpu

D_MODEL = 1024
D_CONV = 1024
CONV_WIDTH = 3
N_HEADS = 16
N_KV_HEADS = 4
HEAD_DIM = 64
GROUP = N_HEADS // N_KV_HEADS
WINDOW = 128
Q_DIM = N_HEADS * HEAD_DIM
KV_DIM = N_KV_HEADS * HEAD_DIM
N_BUCKETS = 32
MAX_DISTANCE = 128
N_EXPERT_GROUPS = 4
EXPERTS_PER_GROUP = 8
N_EXPERTS = N_EXPERT_GROUPS * EXPERTS_PER_GROUP
TOP_K = 2
D_EXPERT = 512
EPS = 1e-6
PAST_LEN = 8192

BF16 = jnp.bfloat16
F32 = jnp.float32
NEG_BIG = -1e30

V7X_VMEM_LIMIT_BYTES = 56 * 1024 * 1024
MOE_VMEM_LIMIT_BYTES = 62 * 1024 * 1024
TILE_ROWS = 8
LANES = 128
ROUTER_LANES = 128
TM_DENSE = 512
ATTN_BLOCK = 128
MOE_BLOCK = 256
MOE_BLOCK_LOG2 = 8
ASSIGN_BITS = 16
COMBINE_BLOCK = 256
SAMPLE_KEYS = 256
SAMPLE_SEQ_PER_STEP = 8


def _const_spec(shape):
    nd = len(shape)
    return pl.BlockSpec(shape, lambda *_: (0,) * nd, pipeline_mode=pl.Buffered(1))


def _rms_norm_f32(xf, g):
    return xf * lax.rsqrt(jnp.mean(xf * xf, axis=-1, keepdims=True) + EPS) * g


def _in_proj_kernel(*refs, tm, sample, blocks_per_seq, u_tail, kv_tail):
    if sample:
        (x_ref, hist0_ref, hist1_ref, g_ref, wcb_ref, wcc_ref, wch_ref, wq_ref, wkv_ref, wga_ref, wgb_ref,
         cw_ref, yc_ref, q_ref, k_ref, v_ref, sa_ref, sb_ref, ut_ref, kvt_ref) = refs
    else:
        (x_ref, g_ref, wcb_ref, wcc_ref, wch_ref, wq_ref, wkv_ref, wga_ref, wgb_ref,
         cw_ref, yc_ref, q_ref, k_ref, v_ref, sa_ref, sb_ref, ut_ref, kvt_ref, ubuf_ref) = refs

    h = _rms_norm_f32(x_ref[...], g_ref[...]).astype(BF16)

    def proj(w_ref):
        return jnp.dot(h, w_ref[...], preferred_element_type=F32)

    u = proj(wcc_ref) * proj(wch_ref)
    w0 = cw_ref[0:1, :]
    w1 = cw_ref[1:2, :]
    w2 = cw_ref[2:3, :]
    if sample:
        conv = w0 * hist0_ref[...] + w1 * hist1_ref[...] + w2 * u
    else:
        @pl.when(pl.program_id(0) % blocks_per_seq == 0)
        def _():
            ubuf_ref[0:8, :] = jnp.zeros((8, D_CONV), F32)

        ubuf_ref[8:8 + tm, :] = u
        conv = w0 * ubuf_ref[6:6 + tm, :] + w1 * ubuf_ref[7:7 + tm, :] + w2 * u
        ubuf_ref[0:8, :] = u[tm - 8:, :]
    yc_ref[...] = (proj(wcb_ref) * conv).astype(BF16)
    ut_ref[0] = u[tm - u_tail:, :]

    q_ref[...] = (proj(wq_ref) * (HEAD_DIM ** -0.5)).astype(BF16)
    kv = proj(wkv_ref)
    k_ref[...] = kv[:, :KV_DIM].astype(BF16)
    v_ref[...] = kv[:, KV_DIM:].astype(BF16)
    kvt_ref[0] = kv[tm - kv_tail:, :]
    sa_ref[...] = jax.nn.sigmoid(proj(wga_ref)).astype(sa_ref.dtype)
    sb_ref[...] = jax.nn.sigmoid(proj(wgb_ref)).astype(sb_ref.dtype)


def _in_proj(x, g1, w_parts, conv_w, *, tm, blocks_per_seq, u_tail, kv_tail, hist=None, gate_dtype=BF16):
    m = x.shape[0]
    nblk = m // tm
    sample = hist is not None
    row = lambda width: pl.BlockSpec((tm, width), lambda i: (i, 0))
    in_specs = [row(D_MODEL)]
    args = [x]
    if sample:
        in_specs += [row(D_CONV), row(D_CONV)]
        args += list(hist)
    in_specs += [_const_spec((1, D_MODEL))] + [_const_spec(w.shape) for w in w_parts] + [_const_spec(conv_w.shape)]
    args += [g1] + list(w_parts) + [conv_w]
    out_shape = [
        jax.ShapeDtypeStruct((m, D_CONV), BF16),
        jax.ShapeDtypeStruct((m, Q_DIM), BF16),
        jax.ShapeDtypeStruct((m, KV_DIM), BF16),
        jax.ShapeDtypeStruct((m, KV_DIM), BF16),
        jax.ShapeDtypeStruct((m, D_MODEL), gate_dtype),
        jax.ShapeDtypeStruct((m, D_MODEL), gate_dtype),
        jax.ShapeDtypeStruct((nblk, u_tail, D_CONV), F32),
        jax.ShapeDtypeStruct((nblk, kv_tail, 2 * KV_DIM), F32),
    ]
    out_specs = [row(D_CONV), row(Q_DIM), row(KV_DIM), row(KV_DIM), row(D_MODEL), row(D_MODEL),
                 pl.BlockSpec((1, u_tail, D_CONV), lambda i: (i, 0, 0)),
                 pl.BlockSpec((1, kv_tail, 2 * KV_DIM), lambda i: (i, 0, 0))]
    scratch = [] if sample else [pltpu.VMEM((tm + 8, D_CONV), F32)]
    return pl.pallas_call(
        functools.partial(_in_proj_kernel, tm=tm, sample=sample, blocks_per_seq=blocks_per_seq,
                          u_tail=u_tail, kv_tail=kv_tail),
        grid=(nblk,),
        in_specs=in_specs,
        out_specs=out_specs,
        out_shape=out_shape,
        scratch_shapes=scratch,
        compiler_params=pltpu.CompilerParams(dimension_semantics=("arbitrary",),
                                             vmem_limit_bytes=V7X_VMEM_LIMIT_BYTES),
        name="in_proj_sample" if sample else "in_proj_prompt",
    )(*args)


def _attn_prompt_kernel(sink_ref, q_ref, kc_ref, kp_ref, vc_ref, vp_ref, bias_ref, o_ref):
    first = pl.program_id(1) == 0
    col = lax.broadcasted_iota(jnp.int32, (ATTN_BLOCK, 2 * ATTN_BLOCK), 1)
    no_prev = jnp.logical_and(first, col < ATTN_BLOCK)
    for g in range(N_KV_HEADS):
        ks = slice(g * HEAD_DIM, (g + 1) * HEAD_DIM)
        kcat = jnp.concatenate([kp_ref[:, ks], kc_ref[:, ks]], axis=0)
        vcat = jnp.concatenate([vp_ref[:, ks], vc_ref[:, ks]], axis=0)
        for hh in range(GROUP):
            h = g * GROUP + hh
            hs = slice(h * HEAD_DIM, (h + 1) * HEAD_DIM)
            s = lax.dot_general(q_ref[:, hs], kcat, (((1,), (1,)), ((), ())),
                                preferred_element_type=F32)
            s = jnp.where(no_prev, NEG_BIG, s + bias_ref[h])
            sink = sink_ref[h]
            m = jnp.maximum(jnp.max(s, axis=-1, keepdims=True), sink)
            p = jnp.exp(s - m)
            denom = jnp.sum(p, axis=-1, keepdims=True) + jnp.exp(sink - m)
            o = jnp.dot(p.astype(BF16), vcat, preferred_element_type=F32)
            o_ref[:, hs] = (o / denom).astype(BF16)


def _attn_prompt(q, k, v, bias, sinks, batch, seq):
    nb = seq // ATTN_BLOCK
    cur = lambda b, i: (b * nb + i, 0)
    prev = lambda b, i: (b * nb + jnp.maximum(i - 1, 0), 0)
    return pl.pallas_call(
        _attn_prompt_kernel,
        grid=(batch, nb),
        in_specs=[pl.BlockSpec(memory_space=pltpu.SMEM),
                  pl.BlockSpec((ATTN_BLOCK, Q_DIM), cur),
                  pl.BlockSpec((ATTN_BLOCK, KV_DIM), cur),
                  pl.BlockSpec((ATTN_BLOCK, KV_DIM), prev),
                  pl.BlockSpec((ATTN_BLOCK, KV_DIM), cur),
                  pl.BlockSpec((ATTN_BLOCK, KV_DIM), prev),
                  _const_spec(bias.shape)],
        out_specs=pl.BlockSpec((ATTN_BLOCK, Q_DIM), cur),
        out_shape=jax.ShapeDtypeStruct((batch * seq, Q_DIM), BF16),
        compiler_params=pltpu.CompilerParams(dimension_semantics=("arbitrary", "arbitrary"),
                                             vmem_limit_bytes=V7X_VMEM_LIMIT_BYTES),
        name="attn_prompt",
    )(sinks, q, k, k, v, v, bias)


def _attn_sample_kernel(qbd_ref, ck_ref, cv_ref, kvn_ref, bias_ref, sink_ref, mask_ref, o_ref, *, w_buf):
    pad = jnp.zeros((SAMPLE_KEYS - w_buf, KV_DIM), F32)
    is_new = lax.broadcasted_iota(jnp.int32, (SAMPLE_KEYS, KV_DIM), 0) == w_buf
    for b in range(SAMPLE_SEQ_PER_STEP):
        kall = jnp.where(is_new, kvn_ref[b:b + 1, :KV_DIM],
                         jnp.concatenate([ck_ref[b], pad], axis=0)).astype(BF16)
        vall = jnp.where(is_new, kvn_ref[b:b + 1, KV_DIM:],
                         jnp.concatenate([cv_ref[b], pad], axis=0)).astype(BF16)
        s = lax.dot_general(qbd_ref[b], kall, (((1,), (1,)), ((), ())),
                            preferred_element_type=F32)
        s = s + bias_ref[...]
        sink = sink_ref[...]
        m = jnp.maximum(jnp.max(s, axis=-1, keepdims=True), sink)
        p = jnp.exp(s - m)
        p = p / (jnp.sum(p, axis=-1, keepdims=True) + jnp.exp(sink - m))
        of = jnp.dot(p.astype(BF16), vall, preferred_element_type=F32)
        of = of * mask_ref[...]
        o_ref[b] = (of[:, 0:HEAD_DIM] + of[:, HEAD_DIM:2 * HEAD_DIM]
                    + of[:, 2 * HEAD_DIM:3 * HEAD_DIM] + of[:, 3 * HEAD_DIM:]).astype(BF16)


def _attn_sample(qbd, ck, cv, kvn, bias, sink_col, head_mask):
    nseq, w_buf = ck.shape[0], ck.shape[1]
    sb = SAMPLE_SEQ_PER_STEP
    return pl.pallas_call(
        functools.partial(_attn_sample_kernel, w_buf=w_buf),
        grid=(nseq // sb,),
        in_specs=[pl.BlockSpec((sb, N_HEADS, KV_DIM), lambda i: (i, 0, 0)),
                  pl.BlockSpec((sb, w_buf, KV_DIM), lambda i: (i, 0, 0)),
                  pl.BlockSpec((sb, w_buf, KV_DIM), lambda i: (i, 0, 0)),
                  pl.BlockSpec((sb, 2 * KV_DIM), lambda i: (i, 0)),
                  _const_spec(bias.shape), _const_spec(sink_col.shape), _const_spec(head_mask.shape)],
        out_specs=pl.BlockSpec((sb, N_HEADS, HEAD_DIM), lambda i: (i, 0, 0)),
        out_shape=jax.ShapeDtypeStruct((nseq, N_HEADS, HEAD_DIM), BF16),
        compiler_params=pltpu.CompilerParams(dimension_semantics=("arbitrary",),
                                             vmem_limit_bytes=V7X_VMEM_LIMIT_BYTES),
        name="attn_sample",
    )(qbd, ck, cv, kvn, bias, sink_col, head_mask)


def _route_rows(logits, row0, valid_rows, half):
    tm = logits.shape[0]
    lane = lax.broadcasted_iota(jnp.int32, logits.shape, 1)
    lane_f = lane.astype(F32)
    no_lane = float(ROUTER_LANES)

    def top1(mask):
        best = jnp.max(jnp.where(mask, logits, -jnp.inf), axis=-1, keepdims=True)
        idx = jnp.min(jnp.where(jnp.logical_and(mask, logits == best), lane_f, no_lane), axis=-1, keepdims=True)
        return best, idx

    gmask = lane < N_EXPERT_GROUPS
    gmax, grp = top1(gmask)
    gsum = jnp.sum(jnp.where(gmask, jnp.exp(logits - gmax), 0.0), axis=-1, keepdims=True)
    p_grp = 1.0 / gsum
    lo = N_EXPERT_GROUPS + EXPERTS_PER_GROUP * grp
    emask = jnp.logical_and(lane_f >= lo, lane_f < lo + EXPERTS_PER_GROUP)
    v1, i1 = top1(emask)
    v2, i2 = top1(jnp.logical_and(emask, lane_f != i1))
    e21 = jnp.exp(v2 - v1)
    w1 = p_grp / (1.0 + e21)
    w2 = p_grp * e21 / (1.0 + e21)

    oh1 = lane_f == i1
    oh2 = lane_f == i2
    if valid_rows < tm:
        valid = lax.broadcasted_iota(jnp.int32, logits.shape, 0) < valid_rows
        oh1 = jnp.logical_and(oh1, valid)
        oh2 = jnp.logical_and(oh2, valid)
    oh = oh1.astype(F32) + oh2.astype(F32)
    token = row0 + lax.broadcasted_iota(jnp.int32, (tm, 1), 0)
    key1 = (i1.astype(jnp.int32) - N_EXPERT_GROUPS) * (1 << ASSIGN_BITS) + token
    key2 = (i2.astype(jnp.int32) - N_EXPERT_GROUPS) * (1 << ASSIGN_BITS) + token + half
    w1b = lax.bitcast_convert_type(w1, jnp.int32)
    w2b = lax.bitcast_convert_type(w2, jnp.int32)
    words = jnp.where(lane == 0, key1, jnp.where(lane == 1, key2, jnp.where(lane == 2, w1b,
                      jnp.where(lane == 3, w2b, 0))))
    return words, jnp.sum(oh, axis=0, keepdims=True)


def _store_token_tiles(ref, x):
    n = x.shape[0]
    for c in range(D_MODEL // LANES):
        ref[pl.ds(c, n, stride=TILE_ROWS), :] = x[:, c * LANES:(c + 1) * LANES]


def _load_token_tiles(ref, n):
    return jnp.concatenate([ref[pl.ds(c, n, stride=TILE_ROWS), :] for c in range(D_MODEL // LANES)], axis=1)


def _pack_bf16_pairs(x):
    hw = x.shape[1] // 2
    bits = lambda v: lax.bitcast_convert_type(v.astype(BF16).astype(F32), jnp.uint32)
    return (bits(x[:, hw:]) & jnp.uint32(0xFFFF0000)) | (bits(x[:, :hw]) >> 16)


def _unpack_bf16_pairs(w):
    lo = lax.bitcast_convert_type(w << 16, F32)
    hi = lax.bitcast_convert_type(w & jnp.uint32(0xFFFF0000), F32)
    return jnp.concatenate([lo, hi], axis=1).astype(BF16)


def _out_proj_rows(yc_ref, o_ref, sa_ref, sb_ref, x_ref, wc_ref, wa_ref, wo_ref, g2_ref, wr_ref, br_ref,
                   x2_ref, h2p_ref, route_ref, cnt_ref, *, valid_rows, half):
    y_conv = jnp.dot(yc_ref[...], wc_ref[...], preferred_element_type=F32)
    y_attn = jnp.dot(o_ref[...], wa_ref[...], preferred_element_type=F32)
    mix = (sa_ref[...].astype(F32) * y_conv + sb_ref[...].astype(F32) * y_attn).astype(BF16)
    x2 = x_ref[...] + jnp.dot(mix, wo_ref[...], preferred_element_type=F32)
    x2_ref[...] = x2
    h2 = _rms_norm_f32(x2, g2_ref[...])
    h2p_ref[...] = _pack_bf16_pairs(h2)
    logits = jnp.dot(h2.astype(BF16), wr_ref[...], preferred_element_type=F32) + br_ref[...]
    words, cnt = _route_rows(logits, pl.program_id(0) * x_ref.shape[0], valid_rows, half)
    route_ref[...] = words
    cnt_ref[...] += cnt


def _out_proj_kernel(*refs, n_first, valid_rows_second, half):
    first, second, shared = refs[0:5], refs[5:10], refs[10:]
    cnt_ref = shared[-1]
    tm = first[4].shape[0]

    @pl.when(pl.program_id(0) == 0)
    def _():
        cnt_ref[...] = jnp.zeros_like(cnt_ref)

    @pl.when(pl.program_id(0) < n_first)
    def _():
        _out_proj_rows(*first, *shared, valid_rows=tm, half=half)

    @pl.when(pl.program_id(0) >= n_first)
    def _():
        _out_proj_rows(*second, *shared, valid_rows=valid_rows_second, half=half)


def _out_proj(acts_a, acts_b, wc, wa, wo, g2, wr, br, *, tm, valid_rows_b, half):
    na = acts_a[4].shape[0] // tm
    nb = acts_b[4].shape[0] // tm
    assert nb == 1
    m_total = (na + nb) * tm
    spec_a = lambda width: pl.BlockSpec((tm, width), lambda i: (jnp.minimum(i, na - 1), 0))
    spec_b = lambda width: pl.BlockSpec((tm, width), lambda i: (jnp.maximum(i - na, 0), 0))
    widths = (D_CONV, Q_DIM, D_MODEL, D_MODEL, D_MODEL)
    in_specs = [spec_a(w) for w in widths] + [spec_b(w) for w in widths]
    in_specs += [_const_spec(wc.shape), _const_spec(wa.shape), _const_spec(wo.shape),
                 _const_spec(g2.shape), _const_spec(wr.shape), _const_spec(br.shape)]
    orow = lambda width: pl.BlockSpec((tm, width), lambda i: (i, 0))
    return pl.pallas_call(
        functools.partial(_out_proj_kernel, n_first=na, valid_rows_second=valid_rows_b, half=half),
        grid=(na + nb,),
        in_specs=in_specs,
        out_specs=[orow(D_MODEL), orow(D_MODEL // 2), orow(ROUTER_LANES),
                   pl.BlockSpec((1, ROUTER_LANES), lambda i: (0, 0))],
        out_shape=[jax.ShapeDtypeStruct((m_total, D_MODEL), F32),
                   jax.ShapeDtypeStruct((m_total, D_MODEL // 2), jnp.uint32),
                   jax.ShapeDtypeStruct((m_total, ROUTER_LANES), jnp.int32),
                   jax.ShapeDtypeStruct((1, ROUTER_LANES), F32)],
        compiler_params=pltpu.CompilerParams(dimension_semantics=("arbitrary",),
                                             vmem_limit_bytes=V7X_VMEM_LIMIT_BYTES),
        name="out_proj",
    )(*acts_a, *acts_b, wc, wa, wo, g2, wr, br)


def _block_table_kernel(counts_ref, bexp_ref, bpos_ref, bcnt_ref, nused_ref, *, n_blocks):
    def per_expert(e, carry):
        blk0, pos0 = carry
        cnt = counts_ref[e]
        nblk = lax.shift_right_logical(cnt + (MOE_BLOCK - 1), MOE_BLOCK_LOG2)

        def mark(b, c):
            off = (b - blk0) * MOE_BLOCK
            bexp_ref[b] = e
            bpos_ref[b] = pos0 + off
            bcnt_ref[b] = jnp.minimum(cnt - off, MOE_BLOCK)
            return c
        lax.fori_loop(blk0, blk0 + nblk, mark, 0)
        return blk0 + nblk, pos0 + cnt

    n_used, _ = lax.fori_loop(0, N_EXPERTS, per_expert, (0, 0))
    nused_ref[0] = n_used

    def unused(b, c):
        bexp_ref[b] = N_EXPERTS - 1
        bpos_ref[b] = 0
        bcnt_ref[b] = 0
        return c
    lax.fori_loop(n_used, n_blocks, unused, 0)


def _block_tables(counts, n_blocks):
    smem = pl.BlockSpec(memory_space=pltpu.SMEM)
    blk = jax.ShapeDtypeStruct((n_blocks,), jnp.int32)
    return pl.pallas_call(
        functools.partial(_block_table_kernel, n_blocks=n_blocks),
        in_specs=[smem],
        out_specs=[smem, smem, smem, smem],
        out_shape=[blk, blk, blk, jax.ShapeDtypeStruct((1,), jnp.int32)],
        name="block_tables",
    )(counts)


def _moe_kernel(tok_ref, order_ref, bexp_ref, bpos_ref, bcnt_ref, nused_ref, h2p_ref, wg_ref, wu_ref, wd_ref,
                contrib_hbm, xs_ref, ys_ref, wgb_ref, wub_ref, wdb_ref, ssem_ref,
                *, m_total, half, n_blocks):
    i = pl.program_id(0)
    slot = i % 2
    n_used = nused_ref[0]
    active = i < n_used
    last_active = i == n_used - 1
    tile = lambda t: pl.ds(pl.multiple_of(t * TILE_ROWS, TILE_ROWS), TILE_ROWS)

    def gather(pos0, slt):
        for r in range(MOE_BLOCK):
            xs_ref[slt, pl.ds(r, 1), :] = h2p_ref[pl.ds(tok_ref[pos0 + r], 1), :]

    def scatter(pos0, cnt, trash0, slt, r):
        dst = jnp.where(r < cnt, order_ref[pos0 + r], trash0 + r)
        return pltpu.make_async_copy(ys_ref.at[slt, tile(r), :], contrib_hbm.at[tile(dst), :], ssem_ref.at[slt])

    def scatter_wait(slt, r):
        pltpu.make_async_copy(ys_ref.at[slt, tile(r), :], contrib_hbm.at[tile(0), :], ssem_ref.at[slt]).wait()

    @pl.when(i == 0)
    def _():
        ys_ref[...] = jnp.zeros_like(ys_ref)
        gap = half - m_total
        fills = [(2 * half + s * MOE_BLOCK, MOE_BLOCK) for s in range(2)]
        fills += [(k * half + m_total, gap) for k in range(TOP_K)] if gap else []
        for start, n in fills:
            fill = pltpu.make_async_copy(ys_ref.at[0, pl.ds(0, n * TILE_ROWS), :],
                                         contrib_hbm.at[pl.ds(start * TILE_ROWS, n * TILE_ROWS), :], ssem_ref.at[0])
            fill.start()
            fill.wait()

    @pl.when(i == 0)
    def _():
        gather(bpos_ref[0], 0)

    @pl.when(jnp.logical_and(active, i >= 2))
    def _():
        for r in range(MOE_BLOCK):
            scatter_wait(slot, r)

    @pl.when(jnp.logical_or(i == 0, bexp_ref[i] != bexp_ref[jnp.maximum(i - 1, 0)]))
    def _():
        wgb_ref[...] = wg_ref[0].astype(BF16)
        wub_ref[...] = wu_ref[0].astype(BF16)
        wdb_ref[...] = wd_ref[0].astype(BF16)

    def run_block(slt):
        pos_next = bpos_ref[jnp.minimum(i + 1, n_blocks - 1)]
        pos0 = bpos_ref[i]
        cnt = bcnt_ref[i]
        trash0 = 2 * half + slt * MOE_BLOCK
        gather(pos_next, 1 - slt)
        xb = _unpack_bf16_pairs(xs_ref[slt])
        gate = jnp.dot(xb, wgb_ref[...], preferred_element_type=F32)
        up = jnp.dot(xb, wub_ref[...], preferred_element_type=F32)
        hmid = (jax.nn.silu(gate) * up).astype(BF16)
        _store_token_tiles(ys_ref.at[slt], jnp.dot(hmid, wdb_ref[...], preferred_element_type=F32))
        for r in range(MOE_BLOCK):
            scatter(pos0, cnt, trash0, slt, r).start(priority=r % 2)

    for slt in range(2):
        pl.when(jnp.logical_and(active, slot == slt))(functools.partial(run_block, slt))

    @pl.when(last_active)
    def _():
        for r in range(MOE_BLOCK):
            scatter_wait(slot, r)

    @pl.when(jnp.logical_and(last_active, i >= 1))
    def _():
        for r in range(MOE_BLOCK):
            scatter_wait(1 - slot, r)


def _moe_experts(tok, order, bexp, bpos, bcnt, n_used, h2p, w_gate, w_up, w_down, *, m_total, half):
    n_blocks = bexp.shape[0]
    assert 0 <= half - m_total <= MOE_BLOCK
    wspec = lambda shape: pl.BlockSpec((1,) + shape, lambda i, tok, order, be, *_: (be[i], 0, 0))
    grid_spec = pltpu.PrefetchScalarGridSpec(
        num_scalar_prefetch=6,
        grid=(n_blocks,),
        in_specs=[_const_spec(h2p.shape),
                  wspec((D_MODEL, D_EXPERT)), wspec((D_MODEL, D_EXPERT)), wspec((D_EXPERT, D_MODEL))],
        out_specs=pl.BlockSpec(memory_space=pl.ANY),
        scratch_shapes=[pltpu.VMEM((2, MOE_BLOCK, D_MODEL // 2), jnp.uint32),
                        pltpu.VMEM((2, MOE_BLOCK * TILE_ROWS, LANES), F32),
                        pltpu.VMEM((D_MODEL, D_EXPERT), BF16),
                        pltpu.VMEM((D_MODEL, D_EXPERT), BF16),
                        pltpu.VMEM((D_EXPERT, D_MODEL), BF16),
                        pltpu.SemaphoreType.DMA((2,))],
    )
    return pl.pallas_call(
        functools.partial(_moe_kernel, m_total=m_total, half=half, n_blocks=n_blocks),
        grid_spec=grid_spec,
        out_shape=jax.ShapeDtypeStruct(((2 * half + 2 * MOE_BLOCK) * TILE_ROWS, LANES), F32),
        compiler_params=pltpu.CompilerParams(dimension_semantics=("arbitrary",),
                                             vmem_limit_bytes=MOE_VMEM_LIMIT_BYTES),
        name="moe_experts",
    )(tok, order, bexp, bpos, bcnt, n_used, h2p, w_gate, w_up, w_down)


def _combine_kernel(c0_ref, c1_ref, x2_ref, w_ref, gf_ref, y_ref):
    tc = x2_ref.shape[0]
    w = w_ref[...]
    moe = w[:, 0:1] * _load_token_tiles(c0_ref, tc) + w[:, 1:2] * _load_token_tiles(c1_ref, tc)
    y_ref[...] = _rms_norm_f32(x2_ref[...] + moe, gf_ref[...])


def _moe_combine(contrib, x2, w_top, gf, *, row_off, m, tc, half):
    off = row_off // tc
    assert row_off % tc == 0 and half % tc == 0
    ctile = lambda k: pl.BlockSpec((tc * TILE_ROWS, LANES), lambda i: (i + off + k * (half // tc), 0))
    return pl.pallas_call(
        _combine_kernel,
        grid=(m // tc,),
        in_specs=[ctile(0), ctile(1),
                  pl.BlockSpec((tc, D_MODEL), lambda i: (i + off, 0)),
                  pl.BlockSpec((tc, TOP_K), lambda i: (i + off, 0)),
                  _const_spec((1, D_MODEL))],
        out_specs=pl.BlockSpec((tc, D_MODEL), lambda i: (i, 0)),
        out_shape=jax.ShapeDtypeStruct((m, D_MODEL), F32),
        compiler_params=pltpu.CompilerParams(dimension_semantics=("arbitrary",),
                                             vmem_limit_bytes=V7X_VMEM_LIMIT_BYTES),
        name="moe_combine_prompt" if row_off == 0 else "moe_combine_sample",
    )(contrib, contrib, x2, w_top, gf)


def _t5_bucket(dist):
    n = jnp.maximum(dist, 0)
    max_exact = N_BUCKETS // 2
    nf = jnp.maximum(n, 1).astype(F32)
    large = max_exact + (jnp.log(nf / max_exact) / math.log(MAX_DISTANCE / max_exact)
                         * (N_BUCKETS - max_exact)).astype(jnp.int32)
    large = jnp.minimum(large, N_BUCKETS - 1)
    return jnp.where(n < max_exact, n, large)


def _bucket_bias(rel_bias, dist):
    onehot = (_t5_bucket(dist)[..., None] == jnp.arange(N_BUCKETS, dtype=jnp.int32)).astype(F32)
    return jnp.dot(onehot, rel_bias.astype(F32), precision=lax.Precision.HIGHEST)


def _prompt_bias_table(rel_bias):
    qi = jnp.arange(ATTN_BLOCK, dtype=jnp.int32)[:, None]
    kj = jnp.arange(2 * ATTN_BLOCK, dtype=jnp.int32)[None, :] - ATTN_BLOCK
    dist = qi - kj
    valid = (dist >= 0) & (dist <= WINDOW)
    bias = jnp.where(valid[..., None], _bucket_bias(rel_bias, dist), NEG_BIG)
    return jnp.moveaxis(bias, -1, 0)


def _sample_bias_table(rel_bias, w_buf):
    j = jnp.arange(SAMPLE_KEYS, dtype=jnp.int32)
    dist = jnp.where(j < w_buf, w_buf - j, 0)
    valid = (j <= w_buf) & (dist <= WINDOW)
    return jnp.where(valid[:, None], _bucket_bias(rel_bias, dist), NEG_BIG).T


def kernel(x_prompt, x_sample, cache_conv, cache_k, cache_v, norm1_g, w_in, conv_w, w_conv_out, w_attn_out, w_o, sinks, rel_bias, norm2_g, w_router_group, b_router_group, w_router_expert, b_router_expert, w_e_gate, w_e_up, w_e_down, norm_f_g):
    assert norm1_g.shape[0] == 1, "single-layer configuration"
    batch, seq, _ = x_prompt.shape
    nseq = x_sample.shape[0]
    w_buf = cache_k.shape[2]
    mp = batch * seq
    m_total = mp + nseq
    assert seq % TM_DENSE == 0 and seq % ATTN_BLOCK == 0 and mp % COMBINE_BLOCK == 0
    assert nseq % SAMPLE_SEQ_PER_STEP == 0 and mp % nseq == 0 and w_buf + 1 <= SAMPLE_KEYS
    assert TOP_K == 2 and MOE_BLOCK == 1 << MOE_BLOCK_LOG2 and m_total * TOP_K < 1 << ASSIGN_BITS

    g1 = norm1_g[0][None, :]
    g2 = norm2_g[0][None, :]
    gf = norm_f_g[None, :]
    wi = w_in[0].astype(BF16)
    o0 = 0
    w_parts = []
    for width in (D_CONV, D_CONV, D_CONV, Q_DIM, 2 * KV_DIM, D_MODEL, D_MODEL):
        w_parts.append(wi[:, o0:o0 + width])
        o0 += width
    cw = conv_w[0]
    wc = w_conv_out[0].astype(BF16)
    wa = w_attn_out[0].astype(BF16)
    wo = w_o[0].astype(BF16)
    pad_cols = ROUTER_LANES - N_EXPERT_GROUPS - N_EXPERTS
    wr = jnp.concatenate([w_router_group[0], w_router_expert[0],
                          jnp.zeros((D_MODEL, pad_cols), F32)], axis=1).astype(BF16)
    br = jnp.concatenate([b_router_group[0], b_router_expert[0], jnp.zeros((pad_cols,), F32)])[None, :]
    sink = sinks[0].astype(F32)

    xp = x_prompt.reshape(mp, D_MODEL)
    bps = seq // TM_DENSE
    yc_p, q_p, k_p, v_p, sa_p, sb_p, ut_p, kvt_p = _in_proj(
        xp, g1, w_parts, cw, tm=TM_DENSE, blocks_per_seq=bps, u_tail=8, kv_tail=WINDOW)
    o_p = _attn_prompt(q_p, k_p, v_p, _prompt_bias_table(rel_bias), sink, batch, seq)

    pad_rows = lambda t: jnp.pad(t, ((0, TM_DENSE - nseq), (0, 0)))
    xs = pad_rows(x_sample.reshape(nseq, D_MODEL))
    hist = (pad_rows(cache_conv[0][:, 0, :]), pad_rows(cache_conv[0][:, 1, :]))
    yc_s, q_s, _, _, sa_s, sb_s, ut_s, kvt_s = _in_proj(
        xs, g1, w_parts, cw, tm=TM_DENSE, blocks_per_seq=1, u_tail=TM_DENSE, kv_tail=TM_DENSE, hist=hist,
        gate_dtype=F32)
    u_s = ut_s[0, :nseq]
    kv_s = kvt_s[0, :nseq]
    head_mask = (jnp.arange(KV_DIM)[None, :] // HEAD_DIM == jnp.arange(N_HEADS)[:, None] // GROUP)
    qbd = (jnp.tile(q_s[:nseq].reshape(nseq, N_HEADS, HEAD_DIM), (1, 1, N_KV_HEADS))
           * head_mask[None].astype(BF16))
    o_s = _attn_sample(qbd, cache_k[0].reshape(nseq, w_buf, KV_DIM), cache_v[0].reshape(nseq, w_buf, KV_DIM),
                       kv_s, _sample_bias_table(rel_bias, w_buf), sink[:, None], head_mask.astype(F32))
    o_s = pad_rows(o_s.reshape(nseq, Q_DIM))

    half = -(-m_total // COMBINE_BLOCK) * COMBINE_BLOCK
    assert half % nseq == 0 and TOP_K * half < 1 << ASSIGN_BITS
    x2, h2p, route, cnt = _out_proj((yc_p, o_p, sa_p, sb_p, xp), (yc_s, o_s, sa_s, sb_s, xs),
                                    wc, wa, wo, g2, wr, br, tm=TM_DENSE, valid_rows_b=nseq, half=half)

    n_assign = m_total * TOP_K
    keys = route[:m_total, 0:TOP_K].reshape(-1)
    w_top = lax.bitcast_convert_type(route[:m_total, TOP_K:2 * TOP_K], F32)
    counts = cnt[0, N_EXPERT_GROUPS:N_EXPERT_GROUPS + N_EXPERTS].astype(jnp.int32)
    order = jnp.pad(jnp.sort(keys) & ((1 << ASSIGN_BITS) - 1), (0, MOE_BLOCK))
    tok = jnp.where(order >= half, order - half, order)
    n_blocks = -(-n_assign // MOE_BLOCK) + N_EXPERTS
    bexp, bpos, bcnt, n_used = _block_tables(counts, n_blocks)
    contrib = _moe_experts(tok, order, bexp, bpos, bcnt, n_used, h2p, w_e_gate[0], w_e_up[0], w_e_down[0],
                           m_total=m_total, half=half)
    y_p = _moe_combine(contrib, x2, w_top, gf, row_off=0, m=mp, tc=COMBINE_BLOCK, half=half)
    y_s = _moe_combine(contrib, x2, w_top, gf, row_off=mp, m=nseq, tc=nseq, half=half)

    y_prompt = y_p.reshape(batch, seq, D_MODEL)
    y_sample = y_s.reshape(nseq, 1, D_MODEL)
    conv_state_prompt = ut_p.reshape(batch, bps, 8, D_CONV)[:, -1, 8 - (CONV_WIDTH - 1):, :][None]
    kv_last = kvt_p.reshape(batch, bps, WINDOW, 2 * KV_DIM)[:, -1]
    k_win_prompt = kv_last[:, :, :KV_DIM].reshape(batch, WINDOW, N_KV_HEADS, HEAD_DIM)[None]
    v_win_prompt = kv_last[:, :, KV_DIM:].reshape(batch, WINDOW, N_KV_HEADS, HEAD_DIM)[None]
    conv_state_sample = jnp.concatenate([cache_conv[0][:, 1:, :], u_s[:, None, :]], axis=1)[None]
    k_new = kv_s[:, :KV_DIM].reshape(nseq, 1, N_KV_HEADS, HEAD_DIM)
    v_new = kv_s[:, KV_DIM:].reshape(nseq, 1, N_KV_HEADS, HEAD_DIM)
    k_win_sample = jnp.concatenate([cache_k[0], k_new], axis=1)[:, -w_buf:][None]
    v_win_sample = jnp.concatenate([cache_v[0], v_new], axis=1)[:, -w_buf:][None]
    return (y_prompt, y_sample, conv_state_prompt, k_win_prompt, v_win_prompt,
            conv_state_sample, k_win_sample, v_win_sample)
```

```python
import functools
import math

import jax
import jax.numpy as jnp
from jax import lax
from jax.experimental import pallas as pl
from jax.experimental.pallas import tpu as pltpu

D_MODEL = 1024
D_CONV = 1024
CONV_WIDTH = 3
N_HEADS = 16
N_KV_HEADS = 4
HEAD_DIM = 64
GROUP = N_HEADS // N_KV_HEADS
WINDOW = 128
Q_DIM = N_HEADS * HEAD_DIM
KV_DIM = N_KV_HEADS * HEAD_DIM
N_BUCKETS = 32
MAX_DISTANCE = 128
N_EXPERT_GROUPS = 4
EXPERTS_PER_GROUP = 8
N_EXPERTS = N_EXPERT_GROUPS * EXPERTS_PER_GROUP
TOP_K = 2
D_EXPERT = 512
EPS = 1e-6
PAST_LEN = 8192

BF16 = jnp.bfloat16
F32 = jnp.float32
NEG_BIG = -1e30

V7X_VMEM_LIMIT_BYTES = 56 * 1024 * 1024
MOE_VMEM_LIMIT_BYTES = 62 * 1024 * 1024
TILE_ROWS = 8
LANES = 128
ROUTER_LANES = 128
TM_DENSE = 512
ATTN_BLOCK = 128
MOE_BLOCK = 256
MOE_BLOCK_LOG2 = 8
ASSIGN_BITS = 16
COMBINE_BLOCK = 256
SAMPLE_SEQ_PER_STEP = 8


def _const_spec(shape):
    nd = len(shape)
    return pl.BlockSpec(shape, lambda *_: (0,) * nd, pipeline_mode=pl.Buffered(1))


def _rms_norm_f32(xf, g):
    return xf * lax.rsqrt(jnp.mean(xf * xf, axis=-1, keepdims=True) + EPS) * g


def _in_proj_kernel(*refs, tm, sample, blocks_per_seq, u_tail, kv_tail):
    if sample:
        (x_ref, hist0_ref, hist1_ref, g_ref, wcb_ref, wcc_ref, wch_ref, wq_ref, wkv_ref, wga_ref, wgb_ref,
         cw_ref, yc_ref, q_ref, k_ref, v_ref, sa_ref, sb_ref, ut_ref, kvt_ref) = refs
    else:
        (x_ref, g_ref, wcb_ref, wcc_ref, wch_ref, wq_ref, wkv_ref, wga_ref, wgb_ref,
         cw_ref, yc_ref, q_ref, k_ref, v_ref, sa_ref, sb_ref, ut_ref, kvt_ref, ubuf_ref) = refs

    h = _rms_norm_f32(x_ref[...], g_ref[...]).astype(BF16)

    def proj(w_ref):
        return jnp.dot(h, w_ref[...], preferred_element_type=F32)

    u = proj(wcc_ref) * proj(wch_ref)
    w0 = cw_ref[0:1, :]
    w1 = cw_ref[1:2, :]
    w2 = cw_ref[2:3, :]
    if sample:
        conv = w0 * hist0_ref[...] + w1 * hist1_ref[...] + w2 * u
    else:
        @pl.when(pl.program_id(0) % blocks_per_seq == 0)
        def _():
            ubuf_ref[0:8, :] = jnp.zeros((8, D_CONV), F32)

        ubuf_ref[8:8 + tm, :] = u
        conv = w0 * ubuf_ref[6:6 + tm, :] + w1 * ubuf_ref[7:7 + tm, :] + w2 * u
        ubuf_ref[0:8, :] = u[tm - 8:, :]
    yc_ref[...] = (proj(wcb_ref) * conv).astype(BF16)
    ut_ref[0] = u[tm - u_tail:, :]

    q_ref[...] = (proj(wq_ref) * (HEAD_DIM ** -0.5)).astype(BF16)
    kv = proj(wkv_ref)
    k_ref[...] = kv[:, :KV_DIM].astype(BF16)
    v_ref[...] = kv[:, KV_DIM:].astype(BF16)
    kvt_ref[0] = kv[tm - kv_tail:, :]
    sa_ref[...] = jax.nn.sigmoid(proj(wga_ref)).astype(sa_ref.dtype)
    sb_ref[...] = jax.nn.sigmoid(proj(wgb_ref)).astype(sb_ref.dtype)


def _in_proj(x, g1, w_parts, conv_w, *, tm, blocks_per_seq, u_tail, kv_tail, hist=None, gate_dtype=BF16):
    m = x.shape[0]
    nblk = m // tm
    sample = hist is not None
    row = lambda width: pl.BlockSpec((tm, width), lambda i: (i, 0))
    in_specs = [row(D_MODEL)]
    args = [x]
    if sample:
        in_specs += [row(D_CONV), row(D_CONV)]
        args += list(hist)
    in_specs += [_const_spec((1, D_MODEL))] + [_const_spec(w.shape) for w in w_parts] + [_const_spec(conv_w.shape)]
    args += [g1] + list(w_parts) + [conv_w]
    out_shape = [
        jax.ShapeDtypeStruct((m, D_CONV), BF16),
        jax.ShapeDtypeStruct((m, Q_DIM), BF16),
        jax.ShapeDtypeStruct((m, KV_DIM), BF16),
        jax.ShapeDtypeStruct((m, KV_DIM), BF16),
        jax.ShapeDtypeStruct((m, D_MODEL), gate_dtype),
        jax.ShapeDtypeStruct((m, D_MODEL), gate_dtype),
        jax.ShapeDtypeStruct((nblk, u_tail, D_CONV), F32),
        jax.ShapeDtypeStruct((nblk, kv_tail, 2 * KV_DIM), F32),
    ]
    out_specs = [row(D_CONV), row(Q_DIM), row(KV_DIM), row(KV_DIM), row(D_MODEL), row(D_MODEL),
                 pl.BlockSpec((1, u_tail, D_CONV), lambda i: (i, 0, 0)),
                 pl.BlockSpec((1, kv_tail, 2 * KV_DIM), lambda i: (i, 0, 0))]
    scratch = [] if sample else [pltpu.VMEM((tm + 8, D_CONV), F32)]
    return pl.pallas_call(
        functools.partial(_in_proj_kernel, tm=tm, sample=sample, blocks_per_seq=blocks_per_seq,
                          u_tail=u_tail, kv_tail=kv_tail),
        grid=(nblk,),
        in_specs=in_specs,
        out_specs=out_specs,
        out_shape=out_shape,
        scratch_shapes=scratch,
        compiler_params=pltpu.CompilerParams(dimension_semantics=("arbitrary",),
                                             vmem_limit_bytes=V7X_VMEM_LIMIT_BYTES),
        name="in_proj_sample" if sample else "in_proj_prompt",
    )(*args)


def _attn_prompt_kernel(sink_ref, q_ref, kc_ref, kp_ref, vc_ref, vp_ref, bias_ref, o_ref):
    first = pl.program_id(1) == 0
    col = lax.broadcasted_iota(jnp.int32, (ATTN_BLOCK, 2 * ATTN_BLOCK), 1)
    no_prev = jnp.logical_and(first, col < ATTN_BLOCK)
    for g in range(N_KV_HEADS):
        ks = slice(g * HEAD_DIM, (g + 1) * HEAD_DIM)
        kcat = jnp.concatenate([kp_ref[:, ks], kc_ref[:, ks]], axis=0)
        vcat = jnp.concatenate([vp_ref[:, ks], vc_ref[:, ks]], axis=0)
        for hh in range(GROUP):
            h = g * GROUP + hh
            hs = slice(h * HEAD_DIM, (h + 1) * HEAD_DIM)
            s = lax.dot_general(q_ref[:, hs], kcat, (((1,), (1,)), ((), ())),
                                preferred_element_type=F32)
            s = jnp.where(no_prev, NEG_BIG, s + bias_ref[h])
            sink = sink_ref[h]
            m = jnp.maximum(jnp.max(s, axis=-1, keepdims=True), sink)
            p = jnp.exp(s - m)
            denom = jnp.sum(p, axis=-1, keepdims=True) + jnp.exp(sink - m)
            o = jnp.dot(p.astype(BF16), vcat, preferred_element_type=F32)
            o_ref[:, hs] = (o / denom).astype(BF16)


def _attn_prompt(q, k, v, bias, sinks, batch, seq):
    nb = seq // ATTN_BLOCK
    cur = lambda b, i: (b * nb + i, 0)
    prev = lambda b, i: (b * nb + jnp.maximum(i - 1, 0), 0)
    return pl.pallas_call(
        _attn_prompt_kernel,
        grid=(batch, nb),
        in_specs=[pl.BlockSpec(memory_space=pltpu.SMEM),
                  pl.BlockSpec((ATTN_BLOCK, Q_DIM), cur),
                  pl.BlockSpec((ATTN_BLOCK, KV_DIM), cur),
                  pl.BlockSpec((ATTN_BLOCK, KV_DIM), prev),
                  pl.BlockSpec((ATTN_BLOCK, KV_DIM), cur),
                  pl.BlockSpec((ATTN_BLOCK, KV_DIM), prev),
                  _const_spec(bias.shape)],
        out_specs=pl.BlockSpec((ATTN_BLOCK, Q_DIM), cur),
        out_shape=jax.ShapeDtypeStruct((batch * seq, Q_DIM), BF16),
        compiler_params=pltpu.CompilerParams(dimension_semantics=("arbitrary", "arbitrary"),
                                             vmem_limit_bytes=V7X_VMEM_LIMIT_BYTES),
        name="attn_prompt",
    )(sinks, q, k, k, v, v, bias)


def _attn_sample_kernel(qbd_ref, ck_ref, cv_ref, kvn_ref, bias_ref, sink_ref, mask_ref, o_ref, *, w_buf):
    bf16_round = lambda t: t.astype(BF16).astype(F32)
    for b in range(SAMPLE_SEQ_PER_STEP):
        qbd = qbd_ref[b]
        s = lax.dot_general(qbd, ck_ref[b].astype(BF16), (((1,), (1,)), ((), ())),
                            preferred_element_type=F32) + bias_ref[:, :w_buf]
        s_new = (jnp.sum(qbd.astype(F32) * bf16_round(kvn_ref[b:b + 1, :KV_DIM]), axis=-1, keepdims=True)
                 + bias_ref[:, w_buf:w_buf + 1])
        sink = sink_ref[...]
        m = jnp.maximum(jnp.maximum(jnp.max(s, axis=-1, keepdims=True), s_new), sink)
        p = jnp.exp(s - m)
        p_new = jnp.exp(s_new - m)
        denom = jnp.sum(p, axis=-1, keepdims=True) + p_new + jnp.exp(sink - m)
        of = (jnp.dot((p / denom).astype(BF16), cv_ref[b].astype(BF16), preferred_element_type=F32)
              + bf16_round(p_new / denom) * bf16_round(kvn_ref[b:b + 1, KV_DIM:]))
        of = of * mask_ref[...]
        o_ref[b] = (of[:, 0:HEAD_DIM] + of[:, HEAD_DIM:2 * HEAD_DIM]
                    + of[:, 2 * HEAD_DIM:3 * HEAD_DIM] + of[:, 3 * HEAD_DIM:]).astype(BF16)


def _attn_sample(qbd, ck, cv, kvn, bias, sink_col, head_mask):
    nseq, w_buf = ck.shape[0], ck.shape[1]
    sb = SAMPLE_SEQ_PER_STEP
    return pl.pallas_call(
        functools.partial(_attn_sample_kernel, w_buf=w_buf),
        grid=(nseq // sb,),
        in_specs=[pl.BlockSpec((sb, N_HEADS, KV_DIM), lambda i: (i, 0, 0)),
                  pl.BlockSpec((sb, w_buf, KV_DIM), lambda i: (i, 0, 0)),
                  pl.BlockSpec((sb, w_buf, KV_DIM), lambda i: (i, 0, 0)),
                  pl.BlockSpec((sb, 2 * KV_DIM), lambda i: (i, 0)),
                  _const_spec(bias.shape), _const_spec(sink_col.shape), _const_spec(head_mask.shape)],
        out_specs=pl.BlockSpec((sb, N_HEADS, HEAD_DIM), lambda i: (i, 0, 0)),
        out_shape=jax.ShapeDtypeStruct((nseq, N_HEADS, HEAD_DIM), BF16),
        compiler_params=pltpu.CompilerParams(dimension_semantics=("arbitrary",),
                                             vmem_limit_bytes=V7X_VMEM_LIMIT_BYTES),
        name="attn_sample",
    )(qbd, ck, cv, kvn, bias, sink_col, head_mask)


def _route_rows(logits, row0, valid_rows, half):
    tm = logits.shape[0]
    lane = lax.broadcasted_iota(jnp.int32, logits.shape, 1)
    lane_f = lane.astype(F32)
    no_lane = float(ROUTER_LANES)

    def top1(mask):
        best = jnp.max(jnp.where(mask, logits, -jnp.inf), axis=-1, keepdims=True)
        idx = jnp.min(jnp.where(jnp.logical_and(mask, logits == best), lane_f, no_lane), axis=-1, keepdims=True)
        return best, idx

    gmask = lane < N_EXPERT_GROUPS
    gmax, grp = top1(gmask)
    gsum = jnp.sum(jnp.where(gmask, jnp.exp(logits - gmax), 0.0), axis=-1, keepdims=True)
    p_grp = 1.0 / gsum
    lo = N_EXPERT_GROUPS + EXPERTS_PER_GROUP * grp
    emask = jnp.logical_and(lane_f >= lo, lane_f < lo + EXPERTS_PER_GROUP)
    v1, i1 = top1(emask)
    v2, i2 = top1(jnp.logical_and(emask, lane_f != i1))
    e21 = jnp.exp(v2 - v1)
    w1 = p_grp / (1.0 + e21)
    w2 = p_grp * e21 / (1.0 + e21)

    oh1 = lane_f == i1
    oh2 = lane_f == i2
    if valid_rows < tm:
        valid = lax.broadcasted_iota(jnp.int32, logits.shape, 0) < valid_rows
        oh1 = jnp.logical_and(oh1, valid)
        oh2 = jnp.logical_and(oh2, valid)
    oh = oh1.astype(F32) + oh2.astype(F32)
    token = row0 + lax.broadcasted_iota(jnp.int32, (tm, 1), 0)
    key1 = (i1.astype(jnp.int32) - N_EXPERT_GROUPS) * (1 << ASSIGN_BITS) + token
    key2 = (i2.astype(jnp.int32) - N_EXPERT_GROUPS) * (1 << ASSIGN_BITS) + token + half
    w1b = lax.bitcast_convert_type(w1, jnp.int32)
    w2b = lax.bitcast_convert_type(w2, jnp.int32)
    words = jnp.where(lane == 0, key1, jnp.where(lane == 1, key2, jnp.where(lane == 2, w1b,
                      jnp.where(lane == 3, w2b, 0))))
    return words, jnp.sum(oh, axis=0, keepdims=True)


def _store_token_tiles(ref, x):
    n = x.shape[0]
    for c in range(D_MODEL // LANES):
        ref[pl.ds(c, n, stride=TILE_ROWS), :] = x[:, c * LANES:(c + 1) * LANES]


def _load_token_tiles(ref, n):
    return jnp.concatenate([ref[pl.ds(c, n, stride=TILE_ROWS), :] for c in range(D_MODEL // LANES)], axis=1)


def _pack_bf16_pairs(x):
    hw = x.shape[1] // 2
    bits = lambda v: lax.bitcast_convert_type(v.astype(BF16).astype(F32), jnp.uint32)
    return (bits(x[:, hw:]) & jnp.uint32(0xFFFF0000)) | (bits(x[:, :hw]) >> 16)


def _unpack_bf16_pairs(w):
    lo = lax.bitcast_convert_type(w << 16, F32)
    hi = lax.bitcast_convert_type(w & jnp.uint32(0xFFFF0000), F32)
    return jnp.concatenate([lo, hi], axis=1).astype(BF16)


def _out_proj_rows(yc_ref, o_ref, sa_ref, sb_ref, x_ref, wc_ref, wa_ref, wo_ref, g2_ref, wr_ref, br_ref,
                   x2_ref, h2p_ref, route_ref, cnt_ref, *, valid_rows, half):
    y_conv = jnp.dot(yc_ref[...], wc_ref[...], preferred_element_type=F32)
    y_attn = jnp.dot(o_ref[...], wa_ref[...], preferred_element_type=F32)
    mix = (sa_ref[...].astype(F32) * y_conv + sb_ref[...].astype(F32) * y_attn).astype(BF16)
    x2 = x_ref[...] + jnp.dot(mix, wo_ref[...], preferred_element_type=F32)
    x2_ref[...] = x2
    h2 = _rms_norm_f32(x2, g2_ref[...])
    h2p_ref[...] = _pack_bf16_pairs(h2)
    logits = jnp.dot(h2.astype(BF16), wr_ref[...], preferred_element_type=F32) + br_ref[...]
    words, cnt = _route_rows(logits, pl.program_id(0) * x_ref.shape[0], valid_rows, half)
    route_ref[...] = words
    cnt_ref[...] += cnt


def _out_proj_kernel(*refs, n_first, valid_rows_second, half):
    first, second, shared = refs[0:5], refs[5:10], refs[10:]
    cnt_ref = shared[-1]
    tm = first[4].shape[0]

    @pl.when(pl.program_id(0) == 0)
    def _():
        cnt_ref[...] = jnp.zeros_like(cnt_ref)

    @pl.when(pl.program_id(0) < n_first)
    def _():
        _out_proj_rows(*first, *shared, valid_rows=tm, half=half)

    @pl.when(pl.program_id(0) >= n_first)
    def _():
        _out_proj_rows(*second, *shared, valid_rows=valid_rows_second, half=half)


def _out_proj(acts_a, acts_b, wc, wa, wo, g2, wr, br, *, tm, valid_rows_b, half):
    na = acts_a[4].shape[0] // tm
    nb = acts_b[4].shape[0] // tm
    assert nb == 1
    m_total = (na + nb) * tm
    spec_a = lambda width: pl.BlockSpec((tm, width), lambda i: (jnp.minimum(i, na - 1), 0))
    spec_b = lambda width: pl.BlockSpec((tm, width), lambda i: (jnp.maximum(i - na, 0), 0))
    widths = (D_CONV, Q_DIM, D_MODEL, D_MODEL, D_MODEL)
    in_specs = [spec_a(w) for w in widths] + [spec_b(w) for w in widths]
    in_specs += [_const_spec(wc.shape), _const_spec(wa.shape), _const_spec(wo.shape),
                 _const_spec(g2.shape), _const_spec(wr.shape), _const_spec(br.shape)]
    orow = lambda width: pl.BlockSpec((tm, width), lambda i: (i, 0))
    return pl.pallas_call(
        functools.partial(_out_proj_kernel, n_first=na, valid_rows_second=valid_rows_b, half=half),
        grid=(na + nb,),
        in_specs=in_specs,
        out_specs=[orow(D_MODEL), orow(D_MODEL // 2), orow(ROUTER_LANES),
                   pl.BlockSpec((1, ROUTER_LANES), lambda i: (0, 0))],
        out_shape=[jax.ShapeDtypeStruct((m_total, D_MODEL), F32),
                   jax.ShapeDtypeStruct((m_total, D_MODEL // 2), jnp.uint32),
                   jax.ShapeDtypeStruct((m_total, ROUTER_LANES), jnp.int32),
                   jax.ShapeDtypeStruct((1, ROUTER_LANES), F32)],
        compiler_params=pltpu.CompilerParams(dimension_semantics=("arbitrary",),
                                             vmem_limit_bytes=V7X_VMEM_LIMIT_BYTES),
        name="out_proj",
    )(*acts_a, *acts_b, wc, wa, wo, g2, wr, br)


def _block_table_kernel(counts_ref, bexp_ref, bpos_ref, bcnt_ref, bslot_ref, bnext_ref, nused_ref, first_ref,
                        *, n_blocks):
    def per_expert(e, carry):
        blk0, pos0, ordinal = carry
        cnt = counts_ref[e]
        nblk = lax.shift_right_logical(cnt + (MOE_BLOCK - 1), MOE_BLOCK_LOG2)
        first_ref[e] = jnp.where(nblk > 0, blk0, -1)

        def mark(b, c):
            off = (b - blk0) * MOE_BLOCK
            bexp_ref[b] = e
            bpos_ref[b] = pos0 + off
            bcnt_ref[b] = jnp.minimum(cnt - off, MOE_BLOCK)
            bslot_ref[b] = jnp.where(b == blk0, ordinal & 1, -1)
            bnext_ref[b] = -1
            return c
        lax.fori_loop(blk0, blk0 + nblk, mark, 0)
        return blk0 + nblk, pos0 + cnt, ordinal + jnp.where(nblk > 0, 1, 0)

    n_used, _, _ = lax.fori_loop(0, N_EXPERTS, per_expert, (0, 0, 0))
    nused_ref[0] = n_used

    def unused(b, c):
        bexp_ref[b] = N_EXPERTS - 1
        bpos_ref[b] = 0
        bcnt_ref[b] = 0
        bslot_ref[b] = -1
        bnext_ref[b] = -1
        return c
    lax.fori_loop(n_used, n_blocks, unused, 0)

    def link(k, nxt):
        e = N_EXPERTS - 1 - k
        fb = first_ref[e]

        @pl.when(fb >= 0)
        def _():
            bnext_ref[fb] = nxt
        return jnp.where(fb >= 0, e, nxt)
    lax.fori_loop(0, N_EXPERTS, link, -1)


def _block_tables(counts, n_blocks):
    smem = pl.BlockSpec(memory_space=pltpu.SMEM)
    blk = jax.ShapeDtypeStruct((n_blocks,), jnp.int32)
    return pl.pallas_call(
        functools.partial(_block_table_kernel, n_blocks=n_blocks),
        in_specs=[smem],
        out_specs=[smem] * 6,
        out_shape=[blk] * 5 + [jax.ShapeDtypeStruct((1,), jnp.int32)],
        scratch_shapes=[pltpu.SMEM((N_EXPERTS,), jnp.int32)],
        name="block_tables",
    )(counts)


def _moe_kernel(tok_ref, order_ref, bexp_ref, bpos_ref, bcnt_ref, bslot_ref, bnext_ref, nused_ref,
                h2p_ref, wg_hbm, wu_hbm, wd_hbm, contrib_hbm,
                xs_ref, ys_ref, wgf_ref, wuf_ref, wdf_ref, wgb_ref, wub_ref, wdb_ref, ssem_ref, wsem_ref,
                *, m_total, half, n_blocks):
    i = pl.program_id(0)
    slot = i % 2
    n_used = nused_ref[0]
    active = i < n_used
    last_active = i == n_used - 1
    tile = lambda t: pl.ds(pl.multiple_of(t * TILE_ROWS, TILE_ROWS), TILE_ROWS)

    def gather(pos0, slt):
        for r in range(MOE_BLOCK):
            xs_ref[slt, pl.ds(r, 1), :] = h2p_ref[pl.ds(tok_ref[pos0 + r], 1), :]

    def scatter(pos0, cnt, trash0, slt, r):
        dst = jnp.where(r < cnt, order_ref[pos0 + r], trash0 + r)
        return pltpu.make_async_copy(ys_ref.at[slt, tile(r), :], contrib_hbm.at[tile(dst), :], ssem_ref.at[slt])

    def scatter_wait(slt, r):
        pltpu.make_async_copy(ys_ref.at[slt, tile(r), :], contrib_hbm.at[tile(0), :], ssem_ref.at[slt]).wait()

    @pl.when(i == 0)
    def _():
        ys_ref[...] = jnp.zeros_like(ys_ref)
        gap = half - m_total
        fills = [(2 * half + s * MOE_BLOCK, MOE_BLOCK) for s in range(2)]
        fills += [(k * half + m_total, gap) for k in range(TOP_K)] if gap else []
        for start, n in fills:
            fill = pltpu.make_async_copy(ys_ref.at[0, pl.ds(0, n * TILE_ROWS), :],
                                         contrib_hbm.at[pl.ds(start * TILE_ROWS, n * TILE_ROWS), :], ssem_ref.at[0])
            fill.start()
            fill.wait()

    def weight_copies(e, s):
        return [pltpu.make_async_copy(src.at[e], dst.at[s], wsem_ref.at[s])
                for src, dst in ((wg_hbm, wgf_ref), (wu_hbm, wuf_ref), (wd_hbm, wdf_ref))]

    @pl.when(i == 0)
    def _():
        for c in weight_copies(bexp_ref[0], 0):
            c.start()
        gather(bpos_ref[0], 0)

    @pl.when(jnp.logical_and(active, i >= 2))
    def _():
        for r in range(MOE_BLOCK):
            scatter_wait(slot, r)

    wslot = bslot_ref[i]

    @pl.when(wslot >= 0)
    def _():
        for c in weight_copies(0, wslot):
            c.wait()

    @pl.when(jnp.logical_and(wslot >= 0, bnext_ref[i] >= 0))
    def _():
        for c in weight_copies(bnext_ref[i], 1 - wslot):
            c.start()

    @pl.when(wslot >= 0)
    def _():
        wgb_ref[...] = wgf_ref[wslot].astype(BF16)
        wub_ref[...] = wuf_ref[wslot].astype(BF16)
        wdb_ref[...] = wdf_ref[wslot].astype(BF16)

    def run_block(slt):
        pos_next = bpos_ref[jnp.minimum(i + 1, n_blocks - 1)]
        pos0 = bpos_ref[i]
        cnt = bcnt_ref[i]
        trash0 = 2 * half + slt * MOE_BLOCK
        gather(pos_next, 1 - slt)
        xb = _unpack_bf16_pairs(xs_ref[slt])
        gate = jnp.dot(xb, wgb_ref[...], preferred_element_type=F32)
        up = jnp.dot(xb, wub_ref[...], preferred_element_type=F32)
        hmid = (jax.nn.silu(gate) * up).astype(BF16)
        _store_token_tiles(ys_ref.at[slt], jnp.dot(hmid, wdb_ref[...], preferred_element_type=F32))
        for r in range(MOE_BLOCK):
            scatter(pos0, cnt, trash0, slt, r).start(priority=r % 2)

    for slt in range(2):
        pl.when(jnp.logical_and(active, slot == slt))(functools.partial(run_block, slt))

    @pl.when(last_active)
    def _():
        for r in range(MOE_BLOCK):
            scatter_wait(slot, r)

    @pl.when(jnp.logical_and(last_active, i >= 1))
    def _():
        for r in range(MOE_BLOCK):
            scatter_wait(1 - slot, r)


def _moe_experts(tok, order, tables, h2p, w_gate, w_up, w_down, *, m_total, half):
    n_blocks = tables[0].shape[0]
    assert 0 <= half - m_total <= MOE_BLOCK
    hbm = pl.BlockSpec(memory_space=pl.ANY)
    grid_spec = pltpu.PrefetchScalarGridSpec(
        num_scalar_prefetch=2 + len(tables),
        grid=(n_blocks,),
        in_specs=[_const_spec(h2p.shape), hbm, hbm, hbm],
        out_specs=hbm,
        scratch_shapes=[pltpu.VMEM((2, MOE_BLOCK, D_MODEL // 2), jnp.uint32),
                        pltpu.VMEM((2, MOE_BLOCK * TILE_ROWS, LANES), F32),
                        pltpu.VMEM((2, D_MODEL, D_EXPERT), F32),
                        pltpu.VMEM((2, D_MODEL, D_EXPERT), F32),
                        pltpu.VMEM((2, D_EXPERT, D_MODEL), F32),
                        pltpu.VMEM((D_MODEL, D_EXPERT), BF16),
                        pltpu.VMEM((D_MODEL, D_EXPERT), BF16),
                        pltpu.VMEM((D_EXPERT, D_MODEL), BF16),
                        pltpu.SemaphoreType.DMA((2,)),
                        pltpu.SemaphoreType.DMA((2,))],
    )
    return pl.pallas_call(
        functools.partial(_moe_kernel, m_total=m_total, half=half, n_blocks=n_blocks),
        grid_spec=grid_spec,
        out_shape=jax.ShapeDtypeStruct(((2 * half + 2 * MOE_BLOCK) * TILE_ROWS, LANES), F32),
        compiler_params=pltpu.CompilerParams(dimension_semantics=("arbitrary",),
                                             vmem_limit_bytes=MOE_VMEM_LIMIT_BYTES),
        name="moe_experts",
    )(tok, order, *tables, h2p, w_gate, w_up, w_down)


def _combine_kernel(c0_ref, c1_ref, x2_ref, w_ref, gf_ref, y_ref):
    tc = x2_ref.shape[0]
    w = w_ref[...]
    moe = w[:, 0:1] * _load_token_tiles(c0_ref, tc) + w[:, 1:2] * _load_token_tiles(c1_ref, tc)
    y_ref[...] = _rms_norm_f32(x2_ref[...] + moe, gf_ref[...])


def _moe_combine(contrib, x2, w_top, gf, *, row_off, m, tc, half):
    off = row_off // tc
    assert row_off % tc == 0 and half % tc == 0
    ctile = lambda k: pl.BlockSpec((tc * TILE_ROWS, LANES), lambda i: (i + off + k * (half // tc), 0))
    return pl.pallas_call(
        _combine_kernel,
        grid=(m // tc,),
        in_specs=[ctile(0), ctile(1),
                  pl.BlockSpec((tc, D_MODEL), lambda i: (i + off, 0)),
                  pl.BlockSpec((tc, TOP_K), lambda i: (i + off, 0)),
                  _const_spec((1, D_MODEL))],
        out_specs=pl.BlockSpec((tc, D_MODEL), lambda i: (i, 0)),
        out_shape=jax.ShapeDtypeStruct((m, D_MODEL), F32),
        compiler_params=pltpu.CompilerParams(dimension_semantics=("arbitrary",),
                                             vmem_limit_bytes=V7X_VMEM_LIMIT_BYTES),
        name="moe_combine_prompt" if row_off == 0 else "moe_combine_sample",
    )(contrib, contrib, x2, w_top, gf)


def _t5_bucket(dist):
    n = jnp.maximum(dist, 0)
    max_exact = N_BUCKETS // 2
    nf = jnp.maximum(n, 1).astype(F32)
    large = max_exact + (jnp.log(nf / max_exact) / math.log(MAX_DISTANCE / max_exact)
                         * (N_BUCKETS - max_exact)).astype(jnp.int32)
    large = jnp.minimum(large, N_BUCKETS - 1)
    return jnp.where(n < max_exact, n, large)


def _bucket_bias(rel_bias, dist):
    onehot = (_t5_bucket(dist)[..., None] == jnp.arange(N_BUCKETS, dtype=jnp.int32)).astype(F32)
    return jnp.dot(onehot, rel_bias.astype(F32), precision=lax.Precision.HIGHEST)


def _prompt_bias_table(rel_bias):
    qi = jnp.arange(ATTN_BLOCK, dtype=jnp.int32)[:, None]
    kj = jnp.arange(2 * ATTN_BLOCK, dtype=jnp.int32)[None, :] - ATTN_BLOCK
    dist = qi - kj
    valid = (dist >= 0) & (dist <= WINDOW)
    bias = jnp.where(valid[..., None], _bucket_bias(rel_bias, dist), NEG_BIG)
    return jnp.moveaxis(bias, -1, 0)


def _sample_bias_table(rel_bias, w_buf):
    j = jnp.arange(w_buf + 1, dtype=jnp.int32)
    dist = w_buf - j
    return jnp.where((dist <= WINDOW)[:, None], _bucket_bias(rel_bias, dist), NEG_BIG).T


def kernel(x_prompt, x_sample, cache_conv, cache_k, cache_v, norm1_g, w_in, conv_w, w_conv_out, w_attn_out, w_o, sinks, rel_bias, norm2_g, w_router_group, b_router_group, w_router_expert, b_router_expert, w_e_gate, w_e_up, w_e_down, norm_f_g):
    assert norm1_g.shape[0] == 1, "single-layer configuration"
    batch, seq, _ = x_prompt.shape
    nseq = x_sample.shape[0]
    w_buf = cache_k.shape[2]
    mp = batch * seq
    m_total = mp + nseq
    assert seq % TM_DENSE == 0 and seq % ATTN_BLOCK == 0 and mp % COMBINE_BLOCK == 0
    assert nseq % SAMPLE_SEQ_PER_STEP == 0 and mp % nseq == 0
    assert TOP_K == 2 and MOE_BLOCK == 1 << MOE_BLOCK_LOG2 and m_total * TOP_K < 1 << ASSIGN_BITS

    g1 = norm1_g[0][None, :]
    g2 = norm2_g[0][None, :]
    gf = norm_f_g[None, :]
    wi = w_in[0].astype(BF16)
    o0 = 0
    w_parts = []
    for width in (D_CONV, D_CONV, D_CONV, Q_DIM, 2 * KV_DIM, D_MODEL, D_MODEL):
        w_parts.append(wi[:, o0:o0 + width])
        o0 += width
    cw = conv_w[0]
    wc = w_conv_out[0].astype(BF16)
    wa = w_attn_out[0].astype(BF16)
    wo = w_o[0].astype(BF16)
    pad_cols = ROUTER_LANES - N_EXPERT_GROUPS - N_EXPERTS
    wr = jnp.concatenate([w_router_group[0], w_router_expert[0],
                          jnp.zeros((D_MODEL, pad_cols), F32)], axis=1).astype(BF16)
    br = jnp.concatenate([b_router_group[0], b_router_expert[0], jnp.zeros((pad_cols,), F32)])[None, :]
    sink = sinks[0].astype(F32)

    xp = x_prompt.reshape(mp, D_MODEL)
    bps = seq // TM_DENSE
    yc_p, q_p, k_p, v_p, sa_p, sb_p, ut_p, kvt_p = _in_proj(
        xp, g1, w_parts, cw, tm=TM_DENSE, blocks_per_seq=bps, u_tail=8, kv_tail=WINDOW)
    o_p = _attn_prompt(q_p, k_p, v_p, _prompt_bias_table(rel_bias), sink, batch, seq)

    pad_rows = lambda t: jnp.pad(t, ((0, TM_DENSE - nseq), (0, 0)))
    xs = pad_rows(x_sample.reshape(nseq, D_MODEL))
    hist = (pad_rows(cache_conv[0][:, 0, :]), pad_rows(cache_conv[0][:, 1, :]))
    yc_s, q_s, _, _, sa_s, sb_s, ut_s, kvt_s = _in_proj(
        xs, g1, w_parts, cw, tm=TM_DENSE, blocks_per_seq=1, u_tail=TM_DENSE, kv_tail=TM_DENSE, hist=hist,
        gate_dtype=F32)
    u_s = ut_s[0, :nseq]
    kv_s = kvt_s[0, :nseq]
    head_mask = (jnp.arange(KV_DIM)[None, :] // HEAD_DIM == jnp.arange(N_HEADS)[:, None] // GROUP)
    qbd = (jnp.tile(q_s[:nseq].reshape(nseq, N_HEADS, HEAD_DIM), (1, 1, N_KV_HEADS))
           * head_mask[None].astype(BF16))
    o_s = _attn_sample(qbd, cache_k[0].reshape(nseq, w_buf, KV_DIM), cache_v[0].reshape(nseq, w_buf, KV_DIM),
                       kv_s, _sample_bias_table(rel_bias, w_buf), sink[:, None], head_mask.astype(F32))
    o_s = pad_rows(o_s.reshape(nseq, Q_DIM))

    half = -(-m_total // COMBINE_BLOCK) * COMBINE_BLOCK
    assert half % nseq == 0 and TOP_K * half < 1 << ASSIGN_BITS
    x2, h2p, route, cnt = _out_proj((yc_p, o_p, sa_p, sb_p, xp), (yc_s, o_s, sa_s, sb_s, xs),
                                    wc, wa, wo, g2, wr, br, tm=TM_DENSE, valid_rows_b=nseq, half=half)

    n_assign = m_total * TOP_K
    keys = route[:m_total, 0:TOP_K].reshape(-1)
    w_top = lax.bitcast_convert_type(route[:m_total, TOP_K:2 * TOP_K], F32)
    counts = cnt[0, N_EXPERT_GROUPS:N_EXPERT_GROUPS + N_EXPERTS].astype(jnp.int32)
    order = jnp.pad(jnp.sort(keys) & ((1 << ASSIGN_BITS) - 1), (0, MOE_BLOCK))
    tok = jnp.where(order >= half, order - half, order)
    n_blocks = -(-n_assign // MOE_BLOCK) + N_EXPERTS
    tables = _block_tables(counts, n_blocks)
    contrib = _moe_experts(tok, order, tables, h2p, w_e_gate[0], w_e_up[0], w_e_down[0],
                           m_total=m_total, half=half)
    y_p = _moe_combine(contrib, x2, w_top, gf, row_off=0, m=mp, tc=COMBINE_BLOCK, half=half)
    y_s = _moe_combine(contrib, x2, w_top, gf, row_off=mp, m=nseq, tc=nseq, half=half)

    y_prompt = y_p.reshape(batch, seq, D_MODEL)
    y_sample = y_s.reshape(nseq, 1, D_MODEL)
    conv_state_prompt = ut_p.reshape(batch, bps, 8, D_CONV)[:, -1, 8 - (CONV_WIDTH - 1):, :][None]
    kv_last = kvt_p.reshape(batch, bps, WINDOW, 2 * KV_DIM)[:, -1]
    k_win_prompt = kv_last[:, :, :KV_DIM].reshape(batch, WINDOW, N_KV_HEADS, HEAD_DIM)[None]
    v_win_prompt = kv_last[:, :, KV_DIM:].reshape(batch, WINDOW, N_KV_HEADS, HEAD_DIM)[None]
    conv_state_sample = jnp.concatenate([cache_conv[0][:, 1:, :], u_s[:, None, :]], axis=1)[None]
    k_new = kv_s[:, :KV_DIM].reshape(nseq, 1, N_KV_HEADS, HEAD_DIM)
    v_new = kv_s[:, KV_DIM:].reshape(nseq, 1, N_KV_HEADS, HEAD_DIM)
    k_win_sample = jnp.concatenate([cache_k[0], k_new], axis=1)[:, -w_buf:][None]
    v_win_sample = jnp.concatenate([cache_v[0], v_new], axis=1)[:, -w_buf:][None]
    return (y_prompt, y_sample, conv_state_prompt, k_win_prompt, v_win_prompt,
            conv_state_sample, k_win_sample, v_win_sample)
```

```python
import functools
import math

import jax
import jax.numpy as jnp
from jax import lax
from jax.experimental import pallas as pl
from jax.experimental.pallas import tpu as pltpu

D_MODEL = 1024
D_CONV = 1024
CONV_WIDTH = 3
N_HEADS = 16
N_KV_HEADS = 4
HEAD_DIM = 64
GROUP = N_HEADS // N_KV_HEADS
WINDOW = 128
Q_DIM = N_HEADS * HEAD_DIM
KV_DIM = N_KV_HEADS * HEAD_DIM
N_BUCKETS = 32
MAX_DISTANCE = 128
N_EXPERT_GROUPS = 4
EXPERTS_PER_GROUP = 8
N_EXPERTS = N_EXPERT_GROUPS * EXPERTS_PER_GROUP
TOP_K = 2
D_EXPERT = 512
EPS = 1e-6
PAST_LEN = 8192

BF16 = jnp.bfloat16
F32 = jnp.float32
NEG_BIG = -1e30

V7X_VMEM_LIMIT_BYTES = 56 * 1024 * 1024
MOE_VMEM_LIMIT_BYTES = 62 * 1024 * 1024
TILE_ROWS = 8
LANES = 128
ROUTER_LANES = 128
TM_DENSE = 512
ATTN_BLOCK = 128
MOE_BLOCK = 256
MOE_BLOCK_LOG2 = 8
ASSIGN_BITS = 16
COMBINE_BLOCK = 256
SAMPLE_SEQ_PER_STEP = 8
HEADS_PER_STORE = LANES // HEAD_DIM


def _const_spec(shape):
    nd = len(shape)
    return pl.BlockSpec(shape, lambda *_: (0,) * nd, pipeline_mode=pl.Buffered(1))


def _rms_norm_f32(xf, g):
    return xf * lax.rsqrt(jnp.mean(xf * xf, axis=-1, keepdims=True) + EPS) * g


def _in_proj_kernel(*refs, tm, sample, blocks_per_seq, u_tail, kv_tail):
    if sample:
        (x_ref, hist0_ref, hist1_ref, g_ref, wcb_ref, wcc_ref, wch_ref, wq_ref, wkv_ref, wga_ref, wgb_ref,
         cw_ref, yc_ref, q_ref, k_ref, v_ref, sa_ref, sb_ref, ut_ref, kvt_ref) = refs
    else:
        (x_ref, g_ref, wcb_ref, wcc_ref, wch_ref, wq_ref, wkv_ref, wga_ref, wgb_ref,
         cw_ref, yc_ref, q_ref, k_ref, v_ref, sa_ref, sb_ref, ut_ref, kvt_ref, ubuf_ref) = refs

    h = _rms_norm_f32(x_ref[...], g_ref[...]).astype(BF16)

    def proj(w_ref):
        return jnp.dot(h, w_ref[...], preferred_element_type=F32)

    u = proj(wcc_ref) * proj(wch_ref)
    w0 = cw_ref[0:1, :]
    w1 = cw_ref[1:2, :]
    w2 = cw_ref[2:3, :]
    if sample:
        conv = w0 * hist0_ref[...] + w1 * hist1_ref[...] + w2 * u
    else:
        @pl.when(pl.program_id(0) % blocks_per_seq == 0)
        def _():
            ubuf_ref[0:8, :] = jnp.zeros((8, D_CONV), F32)

        ubuf_ref[8:8 + tm, :] = u
        conv = w0 * ubuf_ref[6:6 + tm, :] + w1 * ubuf_ref[7:7 + tm, :] + w2 * u
        ubuf_ref[0:8, :] = u[tm - 8:, :]
    yc_ref[...] = (proj(wcb_ref) * conv).astype(BF16)
    ut_ref[0] = u[tm - u_tail:, :]

    q_ref[...] = (proj(wq_ref) * (HEAD_DIM ** -0.5)).astype(BF16)
    kv = proj(wkv_ref)
    k_ref[...] = kv[:, :KV_DIM].astype(BF16)
    v_ref[...] = kv[:, KV_DIM:].astype(BF16)
    kvt_ref[0] = kv[tm - kv_tail:, :]
    sa_ref[...] = jax.nn.sigmoid(proj(wga_ref)).astype(sa_ref.dtype)
    sb_ref[...] = jax.nn.sigmoid(proj(wgb_ref)).astype(sb_ref.dtype)


def _in_proj(x, g1, w_parts, conv_w, *, tm, blocks_per_seq, u_tail, kv_tail, hist=None, gate_dtype=BF16):
    m = x.shape[0]
    nblk = m // tm
    sample = hist is not None
    row = lambda width: pl.BlockSpec((tm, width), lambda i: (i, 0))
    in_specs = [row(D_MODEL)]
    args = [x]
    if sample:
        in_specs += [row(D_CONV), row(D_CONV)]
        args += list(hist)
    in_specs += [_const_spec((1, D_MODEL))] + [_const_spec(w.shape) for w in w_parts] + [_const_spec(conv_w.shape)]
    args += [g1] + list(w_parts) + [conv_w]
    out_shape = [
        jax.ShapeDtypeStruct((m, D_CONV), BF16),
        jax.ShapeDtypeStruct((m, Q_DIM), BF16),
        jax.ShapeDtypeStruct((m, KV_DIM), BF16),
        jax.ShapeDtypeStruct((m, KV_DIM), BF16),
        jax.ShapeDtypeStruct((m, D_MODEL), gate_dtype),
        jax.ShapeDtypeStruct((m, D_MODEL), gate_dtype),
        jax.ShapeDtypeStruct((nblk, u_tail, D_CONV), F32),
        jax.ShapeDtypeStruct((nblk, kv_tail, 2 * KV_DIM), F32),
    ]
    out_specs = [row(D_CONV), row(Q_DIM), row(KV_DIM), row(KV_DIM), row(D_MODEL), row(D_MODEL),
                 pl.BlockSpec((1, u_tail, D_CONV), lambda i: (i, 0, 0)),
                 pl.BlockSpec((1, kv_tail, 2 * KV_DIM), lambda i: (i, 0, 0))]
    scratch = [] if sample else [pltpu.VMEM((tm + 8, D_CONV), F32)]
    return pl.pallas_call(
        functools.partial(_in_proj_kernel, tm=tm, sample=sample, blocks_per_seq=blocks_per_seq,
                          u_tail=u_tail, kv_tail=kv_tail),
        grid=(nblk,),
        in_specs=in_specs,
        out_specs=out_specs,
        out_shape=out_shape,
        scratch_shapes=scratch,
        compiler_params=pltpu.CompilerParams(dimension_semantics=("arbitrary",),
                                             vmem_limit_bytes=V7X_VMEM_LIMIT_BYTES),
        name="in_proj_sample" if sample else "in_proj_prompt",
    )(*args)


def _attn_prompt_kernel(sink_ref, q_ref, kc_ref, kp_ref, vc_ref, vp_ref, bias_ref, o_ref):
    first = pl.program_id(1) == 0
    col = lax.broadcasted_iota(jnp.int32, (ATTN_BLOCK, 2 * ATTN_BLOCK), 1)
    no_prev = jnp.logical_and(first, col < ATTN_BLOCK)
    for g in range(N_KV_HEADS):
        ks = slice(g * HEAD_DIM, (g + 1) * HEAD_DIM)
        kcat = jnp.concatenate([kp_ref[:, ks], kc_ref[:, ks]], axis=0)
        vcat = jnp.concatenate([vp_ref[:, ks], vc_ref[:, ks]], axis=0)
        for h0 in range(g * GROUP, (g + 1) * GROUP, HEADS_PER_STORE):
            outs = []
            for h in range(h0, h0 + HEADS_PER_STORE):
                hs = slice(h * HEAD_DIM, (h + 1) * HEAD_DIM)
                s = lax.dot_general(q_ref[:, hs], kcat, (((1,), (1,)), ((), ())),
                                    preferred_element_type=F32)
                s = jnp.where(no_prev, NEG_BIG, s + bias_ref[h])
                sink = sink_ref[h]
                m = jnp.maximum(jnp.max(s, axis=-1, keepdims=True), sink)
                p = jnp.exp(s - m)
                denom = jnp.sum(p, axis=-1, keepdims=True) + jnp.exp(sink - m)
                o = jnp.dot(p.astype(BF16), vcat, preferred_element_type=F32)
                outs.append((o / denom).astype(BF16))
            o_ref[:, h0 * HEAD_DIM:(h0 + HEADS_PER_STORE) * HEAD_DIM] = jnp.concatenate(outs, axis=1)


def _attn_prompt(q, k, v, bias, sinks, batch, seq):
    nb = seq // ATTN_BLOCK
    cur = lambda b, i: (b * nb + i, 0)
    prev = lambda b, i: (b * nb + jnp.maximum(i - 1, 0), 0)
    return pl.pallas_call(
        _attn_prompt_kernel,
        grid=(batch, nb),
        in_specs=[pl.BlockSpec(memory_space=pltpu.SMEM),
                  pl.BlockSpec((ATTN_BLOCK, Q_DIM), cur),
                  pl.BlockSpec((ATTN_BLOCK, KV_DIM), cur),
                  pl.BlockSpec((ATTN_BLOCK, KV_DIM), prev),
                  pl.BlockSpec((ATTN_BLOCK, KV_DIM), cur),
                  pl.BlockSpec((ATTN_BLOCK, KV_DIM), prev),
                  _const_spec(bias.shape)],
        out_specs=pl.BlockSpec((ATTN_BLOCK, Q_DIM), cur),
        out_shape=jax.ShapeDtypeStruct((batch * seq, Q_DIM), BF16),
        compiler_params=pltpu.CompilerParams(dimension_semantics=("arbitrary", "arbitrary"),
                                             vmem_limit_bytes=V7X_VMEM_LIMIT_BYTES),
        name="attn_prompt",
    )(sinks, q, k, k, v, v, bias)


def _attn_sample_kernel(qbd_ref, ck_ref, cv_ref, kvn_ref, bias_ref, sink_ref, mask_ref, o_ref, *, w_buf):
    bf16_round = lambda t: t.astype(BF16).astype(F32)
    outs = []
    for b in range(SAMPLE_SEQ_PER_STEP):
        qbd = qbd_ref[b]
        s = lax.dot_general(qbd, ck_ref[b].astype(BF16), (((1,), (1,)), ((), ())),
                            preferred_element_type=F32) + bias_ref[:, :w_buf]
        s_new = (jnp.sum(qbd.astype(F32) * bf16_round(kvn_ref[b:b + 1, :KV_DIM]), axis=-1, keepdims=True)
                 + bias_ref[:, w_buf:w_buf + 1])
        sink = sink_ref[...]
        m = jnp.maximum(jnp.maximum(jnp.max(s, axis=-1, keepdims=True), s_new), sink)
        p = jnp.exp(s - m)
        p_new = jnp.exp(s_new - m)
        denom = jnp.sum(p, axis=-1, keepdims=True) + p_new + jnp.exp(sink - m)
        of = (jnp.dot((p / denom).astype(BF16), cv_ref[b].astype(BF16), preferred_element_type=F32)
              + bf16_round(p_new / denom) * bf16_round(kvn_ref[b:b + 1, KV_DIM:]))
        of = of * mask_ref[...]
        outs.append((of[:, 0:HEAD_DIM] + of[:, HEAD_DIM:2 * HEAD_DIM]
                     + of[:, 2 * HEAD_DIM:3 * HEAD_DIM] + of[:, 3 * HEAD_DIM:]).astype(BF16))
    for b, ob in enumerate(outs):
        o_ref[b] = ob


def _attn_sample(qbd, ck, cv, kvn, bias, sink_col, head_mask):
    nseq, w_buf = ck.shape[0], ck.shape[1]
    sb = SAMPLE_SEQ_PER_STEP
    return pl.pallas_call(
        functools.partial(_attn_sample_kernel, w_buf=w_buf),
        grid=(nseq // sb,),
        in_specs=[pl.BlockSpec((sb, N_HEADS, KV_DIM), lambda i: (i, 0, 0)),
                  pl.BlockSpec((sb, w_buf, KV_DIM), lambda i: (i, 0, 0)),
                  pl.BlockSpec((sb, w_buf, KV_DIM), lambda i: (i, 0, 0)),
                  pl.BlockSpec((sb, 2 * KV_DIM), lambda i: (i, 0)),
                  _const_spec(bias.shape), _const_spec(sink_col.shape), _const_spec(head_mask.shape)],
        out_specs=pl.BlockSpec((sb, N_HEADS, HEAD_DIM), lambda i: (i, 0, 0)),
        out_shape=jax.ShapeDtypeStruct((nseq, N_HEADS, HEAD_DIM), BF16),
        compiler_params=pltpu.CompilerParams(dimension_semantics=("arbitrary",),
                                             vmem_limit_bytes=V7X_VMEM_LIMIT_BYTES),
        name="attn_sample",
    )(qbd, ck, cv, kvn, bias, sink_col, head_mask)


def _route_rows(logits, row0, valid_rows, half):
    tm = logits.shape[0]
    lane = lax.broadcasted_iota(jnp.int32, logits.shape, 1)
    lane_f = lane.astype(F32)
    no_lane = float(ROUTER_LANES)

    def top1(mask):
        best = jnp.max(jnp.where(mask, logits, -jnp.inf), axis=-1, keepdims=True)
        idx = jnp.min(jnp.where(jnp.logical_and(mask, logits == best), lane_f, no_lane), axis=-1, keepdims=True)
        return best, idx

    gmask = lane < N_EXPERT_GROUPS
    gmax, grp = top1(gmask)
    gsum = jnp.sum(jnp.where(gmask, jnp.exp(logits - gmax), 0.0), axis=-1, keepdims=True)
    p_grp = 1.0 / gsum
    lo = N_EXPERT_GROUPS + EXPERTS_PER_GROUP * grp
    emask = jnp.logical_and(lane_f >= lo, lane_f < lo + EXPERTS_PER_GROUP)
    v1, i1 = top1(emask)
    v2, i2 = top1(jnp.logical_and(emask, lane_f != i1))
    e21 = jnp.exp(v2 - v1)
    w1 = p_grp / (1.0 + e21)
    w2 = p_grp * e21 / (1.0 + e21)

    oh1 = lane_f == i1
    oh2 = lane_f == i2
    if valid_rows < tm:
        valid = lax.broadcasted_iota(jnp.int32, logits.shape, 0) < valid_rows
        oh1 = jnp.logical_and(oh1, valid)
        oh2 = jnp.logical_and(oh2, valid)
    oh = oh1.astype(F32) + oh2.astype(F32)
    token = row0 + lax.broadcasted_iota(jnp.int32, (tm, 1), 0)
    key1 = (i1.astype(jnp.int32) - N_EXPERT_GROUPS) * (1 << ASSIGN_BITS) + token
    key2 = (i2.astype(jnp.int32) - N_EXPERT_GROUPS) * (1 << ASSIGN_BITS) + token + half
    w1b = lax.bitcast_convert_type(w1, jnp.int32)
    w2b = lax.bitcast_convert_type(w2, jnp.int32)
    words = jnp.where(lane == 0, key1, jnp.where(lane == 1, key2, jnp.where(lane == 2, w1b,
                      jnp.where(lane == 3, w2b, 0))))
    return words, jnp.sum(oh, axis=0, keepdims=True)


def _store_token_tiles(ref, x):
    n = x.shape[0]
    for c in range(D_MODEL // LANES):
        ref[pl.ds(c, n, stride=TILE_ROWS), :] = x[:, c * LANES:(c + 1) * LANES]


def _load_token_tiles(ref, n):
    return jnp.concatenate([ref[pl.ds(c, n, stride=TILE_ROWS), :] for c in range(D_MODEL // LANES)], axis=1)


def _pack_bf16_pairs(x):
    hw = x.shape[1] // 2
    bits = lambda v: lax.bitcast_convert_type(v.astype(BF16).astype(F32), jnp.uint32)
    return (bits(x[:, hw:]) & jnp.uint32(0xFFFF0000)) | (bits(x[:, :hw]) >> 16)


def _unpack_bf16_pairs(w):
    lo = lax.bitcast_convert_type(w << 16, F32)
    hi = lax.bitcast_convert_type(w & jnp.uint32(0xFFFF0000), F32)
    return jnp.concatenate([lo, hi], axis=1).astype(BF16)


def _out_proj_rows(yc_ref, o_ref, sa_ref, sb_ref, x_ref, wc_ref, wa_ref, wo_ref, g2_ref, wr_ref, br_ref,
                   x2_ref, h2p_ref, route_ref, cnt_ref, *, valid_rows, half):
    y_conv = jnp.dot(yc_ref[...], wc_ref[...], preferred_element_type=F32)
    y_attn = jnp.dot(o_ref[...], wa_ref[...], preferred_element_type=F32)
    mix = (sa_ref[...].astype(F32) * y_conv + sb_ref[...].astype(F32) * y_attn).astype(BF16)
    x2 = x_ref[...] + jnp.dot(mix, wo_ref[...], preferred_element_type=F32)
    x2_ref[...] = x2
    h2 = _rms_norm_f32(x2, g2_ref[...])
    h2p_ref[...] = _pack_bf16_pairs(h2)
    logits = jnp.dot(h2.astype(BF16), wr_ref[...], preferred_element_type=F32) + br_ref[...]
    words, cnt = _route_rows(logits, pl.program_id(0) * x_ref.shape[0], valid_rows, half)
    route_ref[...] = words
    cnt_ref[...] += cnt


def _out_proj_kernel(*refs, n_first, valid_rows_second, half):
    first, second, shared = refs[0:5], refs[5:10], refs[10:]
    cnt_ref = shared[-1]
    tm = first[4].shape[0]

    @pl.when(pl.program_id(0) == 0)
    def _():
        cnt_ref[...] = jnp.zeros_like(cnt_ref)

    @pl.when(pl.program_id(0) < n_first)
    def _():
        _out_proj_rows(*first, *shared, valid_rows=tm, half=half)

    @pl.when(pl.program_id(0) >= n_first)
    def _():
        _out_proj_rows(*second, *shared, valid_rows=valid_rows_second, half=half)


def _out_proj(acts_a, acts_b, wc, wa, wo, g2, wr, br, *, tm, valid_rows_b, half):
    na = acts_a[4].shape[0] // tm
    nb = acts_b[4].shape[0] // tm
    assert nb == 1
    m_total = (na + nb) * tm
    spec_a = lambda width: pl.BlockSpec((tm, width), lambda i: (jnp.minimum(i, na - 1), 0))
    spec_b = lambda width: pl.BlockSpec((tm, width), lambda i: (jnp.maximum(i - na, 0), 0))
    widths = (D_CONV, Q_DIM, D_MODEL, D_MODEL, D_MODEL)
    in_specs = [spec_a(w) for w in widths] + [spec_b(w) for w in widths]
    in_specs += [_const_spec(wc.shape), _const_spec(wa.shape), _const_spec(wo.shape),
                 _const_spec(g2.shape), _const_spec(wr.shape), _const_spec(br.shape)]
    orow = lambda width: pl.BlockSpec((tm, width), lambda i: (i, 0))
    return pl.pallas_call(
        functools.partial(_out_proj_kernel, n_first=na, valid_rows_second=valid_rows_b, half=half),
        grid=(na + nb,),
        in_specs=in_specs,
        out_specs=[orow(D_MODEL), orow(D_MODEL // 2), orow(ROUTER_LANES),
                   pl.BlockSpec((1, ROUTER_LANES), lambda i: (0, 0))],
        out_shape=[jax.ShapeDtypeStruct((m_total, D_MODEL), F32),
                   jax.ShapeDtypeStruct((m_total, D_MODEL // 2), jnp.uint32),
                   jax.ShapeDtypeStruct((m_total, ROUTER_LANES), jnp.int32),
                   jax.ShapeDtypeStruct((1, ROUTER_LANES), F32)],
        compiler_params=pltpu.CompilerParams(dimension_semantics=("arbitrary",),
                                             vmem_limit_bytes=V7X_VMEM_LIMIT_BYTES),
        name="out_proj",
    )(*acts_a, *acts_b, wc, wa, wo, g2, wr, br)


def _block_table_kernel(counts_ref, bexp_ref, bpos_ref, bcnt_ref, bslot_ref, bnext_ref, nused_ref, first_ref,
                        *, n_blocks):
    def per_expert(e, carry):
        blk0, pos0, ordinal = carry
        cnt = counts_ref[e]
        nblk = lax.shift_right_logical(cnt + (MOE_BLOCK - 1), MOE_BLOCK_LOG2)
        first_ref[e] = jnp.where(nblk > 0, blk0, -1)

        def mark(b, c):
            off = (b - blk0) * MOE_BLOCK
            bexp_ref[b] = e
            bpos_ref[b] = pos0 + off
            bcnt_ref[b] = jnp.minimum(cnt - off, MOE_BLOCK)
            bslot_ref[b] = jnp.where(b == blk0, ordinal & 1, -1)
            bnext_ref[b] = -1
            return c
        lax.fori_loop(blk0, blk0 + nblk, mark, 0)
        return blk0 + nblk, pos0 + cnt, ordinal + jnp.where(nblk > 0, 1, 0)

    n_used, _, _ = lax.fori_loop(0, N_EXPERTS, per_expert, (0, 0, 0))
    nused_ref[0] = n_used

    def unused(b, c):
        bexp_ref[b] = N_EXPERTS - 1
        bpos_ref[b] = 0
        bcnt_ref[b] = 0
        bslot_ref[b] = -1
        bnext_ref[b] = -1
        return c
    lax.fori_loop(n_used, n_blocks, unused, 0)

    def link(k, nxt):
        e = N_EXPERTS - 1 - k
        fb = first_ref[e]

        @pl.when(fb >= 0)
        def _():
            bnext_ref[fb] = nxt
        return jnp.where(fb >= 0, e, nxt)
    lax.fori_loop(0, N_EXPERTS, link, -1)


def _block_tables(counts, n_blocks):
    smem = pl.BlockSpec(memory_space=pltpu.SMEM)
    blk = jax.ShapeDtypeStruct((n_blocks,), jnp.int32)
    return pl.pallas_call(
        functools.partial(_block_table_kernel, n_blocks=n_blocks),
        in_specs=[smem],
        out_specs=[smem] * 6,
        out_shape=[blk] * 5 + [jax.ShapeDtypeStruct((1,), jnp.int32)],
        scratch_shapes=[pltpu.SMEM((N_EXPERTS,), jnp.int32)],
        name="block_tables",
    )(counts)


def _moe_kernel(tok_ref, order_ref, bexp_ref, bpos_ref, bcnt_ref, bslot_ref, bnext_ref, nused_ref,
                h2p_ref, wg_hbm, wu_hbm, wd_hbm, contrib_hbm,
                xs_ref, ys_ref, wgf_ref, wuf_ref, wdf_ref, wgb_ref, wub_ref, wdb_ref, ssem_ref, wsem_ref,
                *, m_total, half, n_blocks):
    i = pl.program_id(0)
    slot = i % 2
    n_used = nused_ref[0]
    active = i < n_used
    last_active = i == n_used - 1
    tile = lambda t: pl.ds(pl.multiple_of(t * TILE_ROWS, TILE_ROWS), TILE_ROWS)

    def gather(pos0, slt):
        for r in range(MOE_BLOCK):
            xs_ref[slt, pl.ds(r, 1), :] = h2p_ref[pl.ds(tok_ref[pos0 + r], 1), :]

    def scatter(pos0, cnt, trash0, slt, r):
        dst = jnp.where(r < cnt, order_ref[pos0 + r], trash0 + r)
        return pltpu.make_async_copy(ys_ref.at[slt, tile(r), :], contrib_hbm.at[tile(dst), :], ssem_ref.at[slt])

    def scatter_wait(slt, r):
        pltpu.make_async_copy(ys_ref.at[slt, tile(r), :], contrib_hbm.at[tile(0), :], ssem_ref.at[slt]).wait()

    @pl.when(i == 0)
    def _():
        ys_ref[...] = jnp.zeros_like(ys_ref)
        gap = half - m_total
        fills = [(2 * half + s * MOE_BLOCK, MOE_BLOCK) for s in range(2)]
        fills += [(k * half + m_total, gap) for k in range(TOP_K)] if gap else []
        for start, n in fills:
            fill = pltpu.make_async_copy(ys_ref.at[0, pl.ds(0, n * TILE_ROWS), :],
                                         contrib_hbm.at[pl.ds(start * TILE_ROWS, n * TILE_ROWS), :], ssem_ref.at[0])
            fill.start()
            fill.wait()

    def weight_copies(e, s):
        return [pltpu.make_async_copy(src.at[e], dst.at[s], wsem_ref.at[s])
                for src, dst in ((wg_hbm, wgf_ref), (wu_hbm, wuf_ref), (wd_hbm, wdf_ref))]

    @pl.when(i == 0)
    def _():
        for c in weight_copies(bexp_ref[0], 0):
            c.start()
        gather(bpos_ref[0], 0)

    @pl.when(jnp.logical_and(active, i >= 2))
    def _():
        for r in range(MOE_BLOCK):
            scatter_wait(slot, r)

    wslot = bslot_ref[i]

    @pl.when(wslot >= 0)
    def _():
        for c in weight_copies(0, wslot):
            c.wait()

    @pl.when(jnp.logical_and(wslot >= 0, bnext_ref[i] >= 0))
    def _():
        for c in weight_copies(bnext_ref[i], 1 - wslot):
            c.start()

    @pl.when(wslot >= 0)
    def _():
        wgb_ref[...] = wgf_ref[wslot].astype(BF16)
        wub_ref[...] = wuf_ref[wslot].astype(BF16)
        wdb_ref[...] = wdf_ref[wslot].astype(BF16)

    def run_block(slt):
        pos_next = bpos_ref[jnp.minimum(i + 1, n_blocks - 1)]
        pos0 = bpos_ref[i]
        cnt = bcnt_ref[i]
        trash0 = 2 * half + slt * MOE_BLOCK
        gather(pos_next, 1 - slt)
        xb = _unpack_bf16_pairs(xs_ref[slt])
        gate = jnp.dot(xb, wgb_ref[...], preferred_element_type=F32)
        up = jnp.dot(xb, wub_ref[...], preferred_element_type=F32)
        hmid = (jax.nn.silu(gate) * up).astype(BF16)
        _store_token_tiles(ys_ref.at[slt], jnp.dot(hmid, wdb_ref[...], preferred_element_type=F32))
        for r in range(MOE_BLOCK):
            scatter(pos0, cnt, trash0, slt, r).start(priority=r % 2)

    for slt in range(2):
        pl.when(jnp.logical_and(active, slot == slt))(functools.partial(run_block, slt))

    @pl.when(last_active)
    def _():
        for r in range(MOE_BLOCK):
            scatter_wait(slot, r)

    @pl.when(jnp.logical_and(last_active, i >= 1))
    def _():
        for r in range(MOE_BLOCK):
            scatter_wait(1 - slot, r)


def _moe_experts(tok, order, tables, h2p, w_gate, w_up, w_down, *, m_total, half):
    n_blocks = tables[0].shape[0]
    assert 0 <= half - m_total <= MOE_BLOCK
    hbm = pl.BlockSpec(memory_space=pl.ANY)
    grid_spec = pltpu.PrefetchScalarGridSpec(
        num_scalar_prefetch=2 + len(tables),
        grid=(n_blocks,),
        in_specs=[_const_spec(h2p.shape), hbm, hbm, hbm],
        out_specs=hbm,
        scratch_shapes=[pltpu.VMEM((2, MOE_BLOCK, D_MODEL // 2), jnp.uint32),
                        pltpu.VMEM((2, MOE_BLOCK * TILE_ROWS, LANES), F32),
                        pltpu.VMEM((2, D_MODEL, D_EXPERT), F32),
                        pltpu.VMEM((2, D_MODEL, D_EXPERT), F32),
                        pltpu.VMEM((2, D_EXPERT, D_MODEL), F32),
                        pltpu.VMEM((D_MODEL, D_EXPERT), BF16),
                        pltpu.VMEM((D_MODEL, D_EXPERT), BF16),
                        pltpu.VMEM((D_EXPERT, D_MODEL), BF16),
                        pltpu.SemaphoreType.DMA((2,)),
                        pltpu.SemaphoreType.DMA((2,))],
    )
    return pl.pallas_call(
        functools.partial(_moe_kernel, m_total=m_total, half=half, n_blocks=n_blocks),
        grid_spec=grid_spec,
        out_shape=jax.ShapeDtypeStruct(((2 * half + 2 * MOE_BLOCK) * TILE_ROWS, LANES), F32),
        compiler_params=pltpu.CompilerParams(dimension_semantics=("arbitrary",),
                                             vmem_limit_bytes=MOE_VMEM_LIMIT_BYTES),
        name="moe_experts",
    )(tok, order, *tables, h2p, w_gate, w_up, w_down)


def _combine_kernel(c0_ref, c1_ref, x2_ref, w_ref, gf_ref, y_ref):
    tc = x2_ref.shape[0]
    w = w_ref[...]
    moe = w[:, 0:1] * _load_token_tiles(c0_ref, tc) + w[:, 1:2] * _load_token_tiles(c1_ref, tc)
    y_ref[...] = _rms_norm_f32(x2_ref[...] + moe, gf_ref[...])


def _moe_combine(contrib, x2, w_top, gf, *, row_off, m, tc, half):
    off = row_off // tc
    assert row_off % tc == 0 and half % tc == 0
    ctile = lambda k: pl.BlockSpec((tc * TILE_ROWS, LANES), lambda i: (i + off + k * (half // tc), 0))
    return pl.pallas_call(
        _combine_kernel,
        grid=(m // tc,),
        in_specs=[ctile(0), ctile(1),
                  pl.BlockSpec((tc, D_MODEL), lambda i: (i + off, 0)),
                  pl.BlockSpec((tc, TOP_K), lambda i: (i + off, 0)),
                  _const_spec((1, D_MODEL))],
        out_specs=pl.BlockSpec((tc, D_MODEL), lambda i: (i, 0)),
        out_shape=jax.ShapeDtypeStruct((m, D_MODEL), F32),
        compiler_params=pltpu.CompilerParams(dimension_semantics=("arbitrary",),
                                             vmem_limit_bytes=V7X_VMEM_LIMIT_BYTES),
        name="moe_combine_prompt" if row_off == 0 else "moe_combine_sample",
    )(contrib, contrib, x2, w_top, gf)


def _t5_bucket(dist):
    n = jnp.maximum(dist, 0)
    max_exact = N_BUCKETS // 2
    nf = jnp.maximum(n, 1).astype(F32)
    large = max_exact + (jnp.log(nf / max_exact) / math.log(MAX_DISTANCE / max_exact)
                         * (N_BUCKETS - max_exact)).astype(jnp.int32)
    large = jnp.minimum(large, N_BUCKETS - 1)
    return jnp.where(n < max_exact, n, large)


def _bucket_bias(rel_bias, dist):
    onehot = (_t5_bucket(dist)[..., None] == jnp.arange(N_BUCKETS, dtype=jnp.int32)).astype(F32)
    return jnp.dot(onehot, rel_bias.astype(F32), precision=lax.Precision.HIGHEST)


def _prompt_bias_table(rel_bias):
    qi = jnp.arange(ATTN_BLOCK, dtype=jnp.int32)[:, None]
    kj = jnp.arange(2 * ATTN_BLOCK, dtype=jnp.int32)[None, :] - ATTN_BLOCK
    dist = qi - kj
    valid = (dist >= 0) & (dist <= WINDOW)
    bias = jnp.where(valid[..., None], _bucket_bias(rel_bias, dist), NEG_BIG)
    return jnp.moveaxis(bias, -1, 0)


def _sample_bias_table(rel_bias, w_buf):
    j = jnp.arange(w_buf + 1, dtype=jnp.int32)
    dist = w_buf - j
    return jnp.where((dist <= WINDOW)[:, None], _bucket_bias(rel_bias, dist), NEG_BIG).T


def kernel(x_prompt, x_sample, cache_conv, cache_k, cache_v, norm1_g, w_in, conv_w, w_conv_out, w_attn_out, w_o, sinks, rel_bias, norm2_g, w_router_group, b_router_group, w_router_expert, b_router_expert, w_e_gate, w_e_up, w_e_down, norm_f_g):
    assert norm1_g.shape[0] == 1, "single-layer configuration"
    batch, seq, _ = x_prompt.shape
    nseq = x_sample.shape[0]
    w_buf = cache_k.shape[2]
    mp = batch * seq
    m_total = mp + nseq
    assert seq % TM_DENSE == 0 and seq % ATTN_BLOCK == 0 and mp % COMBINE_BLOCK == 0
    assert nseq % SAMPLE_SEQ_PER_STEP == 0 and mp % nseq == 0
    assert TOP_K == 2 and MOE_BLOCK == 1 << MOE_BLOCK_LOG2 and m_total * TOP_K < 1 << ASSIGN_BITS

    g1 = norm1_g[0][None, :]
    g2 = norm2_g[0][None, :]
    gf = norm_f_g[None, :]
    wi = w_in[0].astype(BF16)
    o0 = 0
    w_parts = []
    for width in (D_CONV, D_CONV, D_CONV, Q_DIM, 2 * KV_DIM, D_MODEL, D_MODEL):
        w_parts.append(wi[:, o0:o0 + width])
        o0 += width
    cw = conv_w[0]
    wc = w_conv_out[0].astype(BF16)
    wa = w_attn_out[0].astype(BF16)
    wo = w_o[0].astype(BF16)
    pad_cols = ROUTER_LANES - N_EXPERT_GROUPS - N_EXPERTS
    wr = jnp.concatenate([w_router_group[0], w_router_expert[0],
                          jnp.zeros((D_MODEL, pad_cols), F32)], axis=1).astype(BF16)
    br = jnp.concatenate([b_router_group[0], b_router_expert[0], jnp.zeros((pad_cols,), F32)])[None, :]
    sink = sinks[0].astype(F32)

    xp = x_prompt.reshape(mp, D_MODEL)
    bps = seq // TM_DENSE
    yc_p, q_p, k_p, v_p, sa_p, sb_p, ut_p, kvt_p = _in_proj(
        xp, g1, w_parts, cw, tm=TM_DENSE, blocks_per_seq=bps, u_tail=8, kv_tail=WINDOW)
    o_p = _attn_prompt(q_p, k_p, v_p, _prompt_bias_table(rel_bias), sink, batch, seq)

    pad_rows = lambda t: jnp.pad(t, ((0, TM_DENSE - nseq), (0, 0)))
    xs = pad_rows(x_sample.reshape(nseq, D_MODEL))
    hist = (pad_rows(cache_conv[0][:, 0, :]), pad_rows(cache_conv[0][:, 1, :]))
    yc_s, q_s, _, _, sa_s, sb_s, ut_s, kvt_s = _in_proj(
        xs, g1, w_parts, cw, tm=TM_DENSE, blocks_per_seq=1, u_tail=TM_DENSE, kv_tail=TM_DENSE, hist=hist,
        gate_dtype=F32)
    u_s = ut_s[0, :nseq]
    kv_s = kvt_s[0, :nseq]
    head_mask = (jnp.arange(KV_DIM)[None, :] // HEAD_DIM == jnp.arange(N_HEADS)[:, None] // GROUP)
    qbd = (jnp.tile(q_s[:nseq].reshape(nseq, N_HEADS, HEAD_DIM), (1, 1, N_KV_HEADS))
           * head_mask[None].astype(BF16))
    o_s = _attn_sample(qbd, cache_k[0].reshape(nseq, w_buf, KV_DIM), cache_v[0].reshape(nseq, w_buf, KV_DIM),
                       kv_s, _sample_bias_table(rel_bias, w_buf), sink[:, None], head_mask.astype(F32))
    o_s = pad_rows(o_s.reshape(nseq, Q_DIM))

    half = -(-m_total // COMBINE_BLOCK) * COMBINE_BLOCK
    assert half % nseq == 0 and TOP_K * half < 1 << ASSIGN_BITS
    x2, h2p, route, cnt = _out_proj((yc_p, o_p, sa_p, sb_p, xp), (yc_s, o_s, sa_s, sb_s, xs),
                                    wc, wa, wo, g2, wr, br, tm=TM_DENSE, valid_rows_b=nseq, half=half)

    n_assign = m_total * TOP_K
    keys = route[:m_total, 0:TOP_K].reshape(-1)
    w_top = lax.bitcast_convert_type(route[:m_total, TOP_K:2 * TOP_K], F32)
    counts = cnt[0, N_EXPERT_GROUPS:N_EXPERT_GROUPS + N_EXPERTS].astype(jnp.int32)
    order = jnp.pad(jnp.sort(keys) & ((1 << ASSIGN_BITS) - 1), (0, MOE_BLOCK))
    tok = jnp.where(order >= half, order - half, order)
    n_blocks = -(-n_assign // MOE_BLOCK) + N_EXPERTS
    tables = _block_tables(counts, n_blocks)
    contrib = _moe_experts(tok, order, tables, h2p, w_e_gate[0], w_e_up[0], w_e_down[0],
                           m_total=m_total, half=half)
    y_p = _moe_combine(contrib, x2, w_top, gf, row_off=0, m=mp, tc=COMBINE_BLOCK, half=half)
    y_s = _moe_combine(contrib, x2, w_top, gf, row_off=mp, m=nseq, tc=nseq, half=half)

    y_prompt = y_p.reshape(batch, seq, D_MODEL)
    y_sample = y_s.reshape(nseq, 1, D_MODEL)
    conv_state_prompt = ut_p.reshape(batch, bps, 8, D_CONV)[:, -1, 8 - (CONV_WIDTH - 1):, :][None]
    kv_last = kvt_p.reshape(batch, bps, WINDOW, 2 * KV_DIM)[:, -1]
    k_win_prompt = kv_last[:, :, :KV_DIM].reshape(batch, WINDOW, N_KV_HEADS, HEAD_DIM)[None]
    v_win_prompt = kv_last[:, :, KV_DIM:].reshape(batch, WINDOW, N_KV_HEADS, HEAD_DIM)[None]
    conv_state_sample = jnp.concatenate([cache_conv[0][:, 1:, :], u_s[:, None, :]], axis=1)[None]
    k_new = kv_s[:, :KV_DIM].reshape(nseq, 1, N_KV_HEADS, HEAD_DIM)
    v_new = kv_s[:, KV_DIM:].reshape(nseq, 1, N_KV_HEADS, HEAD_DIM)
    k_win_sample = jnp.concatenate([cache_k[0], k_new], axis=1)[:, -w_buf:][None]
    v_win_sample = jnp.concatenate([cache_v[0], v_new], axis=1)[:, -w_buf:][None]
    return (y_prompt, y_sample, conv_state_prompt, k_win_prompt, v_win_prompt,
            conv_state_sample, k_win_sample, v_win_sample)
```

```python
import functools
import math

import jax
import jax.numpy as jnp
from jax import lax
from jax.experimental import pallas as pl
from jax.experimental.pallas import tpu as pltpu

D_MODEL = 1024
D_CONV = 1024
CONV_WIDTH = 3
N_HEADS = 16
N_KV_HEADS = 4
HEAD_DIM = 64
GROUP = N_HEADS // N_KV_HEADS
WINDOW = 128
Q_DIM = N_HEADS * HEAD_DIM
KV_DIM = N_KV_HEADS * HEAD_DIM
N_BUCKETS = 32
MAX_DISTANCE = 128
N_EXPERT_GROUPS = 4
EXPERTS_PER_GROUP = 8
N_EXPERTS = N_EXPERT_GROUPS * EXPERTS_PER_GROUP
TOP_K = 2
D_EXPERT = 512
EPS = 1e-6
PAST_LEN = 8192

BF16 = jnp.bfloat16
F32 = jnp.float32
NEG_BIG = -1e30

V7X_VMEM_LIMIT_BYTES = 56 * 1024 * 1024
MOE_VMEM_LIMIT_BYTES = 62 * 1024 * 1024
TILE_ROWS = 8
LANES = 128
ROUTER_LANES = 128
TM_DENSE = 512
ATTN_BLOCK = 128
MOE_BLOCK = 256
MOE_BLOCK_LOG2 = 8
ASSIGN_BITS = 16
COMBINE_BLOCK = 256
SAMPLE_SEQ_PER_STEP = 8
HEADS_PER_STORE = LANES // HEAD_DIM


def _const_spec(shape):
    nd = len(shape)
    return pl.BlockSpec(shape, lambda *_: (0,) * nd, pipeline_mode=pl.Buffered(1))


def _rms_norm_f32(xf, g):
    return xf * lax.rsqrt(jnp.mean(xf * xf, axis=-1, keepdims=True) + EPS) * g


def _in_proj_kernel(*refs, tm, sample, blocks_per_seq, u_tail, kv_tail):
    if sample:
        (x_ref, hist0_ref, hist1_ref, g_ref, w_ref,
         cw_ref, yc_ref, q_ref, k_ref, v_ref, sa_ref, sb_ref, ut_ref, kvt_ref) = refs
    else:
        (x_ref, g_ref, w_ref,
         cw_ref, yc_ref, q_ref, k_ref, v_ref, sa_ref, sb_ref, ut_ref, kvt_ref, ubuf_ref) = refs

    h = _rms_norm_f32(x_ref[...], g_ref[...]).astype(BF16)

    widths = (D_CONV, D_CONV, D_CONV, Q_DIM, 2 * KV_DIM, D_MODEL, D_MODEL)
    starts = [sum(widths[:n]) for n in range(len(widths))]
    wcb_ref, wcc_ref, wch_ref, wq_ref, wkv_ref, wga_ref, wgb_ref = [
        w_ref.at[:, pl.ds(a, n)] for a, n in zip(starts, widths)]

    def proj(part_ref):
        return jnp.dot(h, part_ref[...], preferred_element_type=F32)

    u = proj(wcc_ref) * proj(wch_ref)
    w0 = cw_ref[0:1, :]
    w1 = cw_ref[1:2, :]
    w2 = cw_ref[2:3, :]
    if sample:
        conv = w0 * hist0_ref[...] + w1 * hist1_ref[...] + w2 * u
    else:
        @pl.when(pl.program_id(0) % blocks_per_seq == 0)
        def _():
            ubuf_ref[0:8, :] = jnp.zeros((8, D_CONV), F32)

        ubuf_ref[8:8 + tm, :] = u
        conv = w0 * ubuf_ref[6:6 + tm, :] + w1 * ubuf_ref[7:7 + tm, :] + w2 * u
        ubuf_ref[0:8, :] = u[tm - 8:, :]
    yc_ref[...] = (proj(wcb_ref) * conv).astype(BF16)
    ut_ref[0] = u[tm - u_tail:, :]

    q_ref[...] = (proj(wq_ref) * (HEAD_DIM ** -0.5)).astype(BF16)
    kv = proj(wkv_ref)
    k_ref[...] = kv[:, :KV_DIM].astype(BF16)
    v_ref[...] = kv[:, KV_DIM:].astype(BF16)
    kvt_ref[0] = kv[tm - kv_tail:, :]
    sa_ref[...] = jax.nn.sigmoid(proj(wga_ref)).astype(sa_ref.dtype)
    sb_ref[...] = jax.nn.sigmoid(proj(wgb_ref)).astype(sb_ref.dtype)


def _in_proj(x, g1, w_in, conv_w, *, tm, blocks_per_seq, u_tail, kv_tail, hist=None, gate_dtype=BF16):
    m = x.shape[0]
    nblk = m // tm
    sample = hist is not None
    row = lambda width: pl.BlockSpec((tm, width), lambda i: (i, 0))
    in_specs = [row(D_MODEL)]
    args = [x]
    if sample:
        in_specs += [row(D_CONV), row(D_CONV)]
        args += list(hist)
    in_specs += [_const_spec((1, D_MODEL)), _const_spec(w_in.shape), _const_spec(conv_w.shape)]
    args += [g1, w_in, conv_w]
    out_shape = [
        jax.ShapeDtypeStruct((m, D_CONV), BF16),
        jax.ShapeDtypeStruct((m, Q_DIM), BF16),
        jax.ShapeDtypeStruct((m, KV_DIM), BF16),
        jax.ShapeDtypeStruct((m, KV_DIM), BF16),
        jax.ShapeDtypeStruct((m, D_MODEL), gate_dtype),
        jax.ShapeDtypeStruct((m, D_MODEL), gate_dtype),
        jax.ShapeDtypeStruct((nblk, u_tail, D_CONV), F32),
        jax.ShapeDtypeStruct((nblk, kv_tail, 2 * KV_DIM), F32),
    ]
    out_specs = [row(D_CONV), row(Q_DIM), row(KV_DIM), row(KV_DIM), row(D_MODEL), row(D_MODEL),
                 pl.BlockSpec((1, u_tail, D_CONV), lambda i: (i, 0, 0)),
                 pl.BlockSpec((1, kv_tail, 2 * KV_DIM), lambda i: (i, 0, 0))]
    scratch = [] if sample else [pltpu.VMEM((tm + 8, D_CONV), F32)]
    return pl.pallas_call(
        functools.partial(_in_proj_kernel, tm=tm, sample=sample, blocks_per_seq=blocks_per_seq,
                          u_tail=u_tail, kv_tail=kv_tail),
        grid=(nblk,),
        in_specs=in_specs,
        out_specs=out_specs,
        out_shape=out_shape,
        scratch_shapes=scratch,
        compiler_params=pltpu.CompilerParams(dimension_semantics=("arbitrary",),
                                             vmem_limit_bytes=V7X_VMEM_LIMIT_BYTES),
        name="in_proj_sample" if sample else "in_proj_prompt",
    )(*args)


def _attn_prompt_kernel(sink_ref, q_ref, kc_ref, kp_ref, vc_ref, vp_ref, bias_ref, o_ref):
    first = pl.program_id(1) == 0
    col = lax.broadcasted_iota(jnp.int32, (ATTN_BLOCK, 2 * ATTN_BLOCK), 1)
    no_prev = jnp.logical_and(first, col < ATTN_BLOCK)
    for g in range(N_KV_HEADS):
        ks = slice(g * HEAD_DIM, (g + 1) * HEAD_DIM)
        kcat = jnp.concatenate([kp_ref[:, ks], kc_ref[:, ks]], axis=0)
        vcat = jnp.concatenate([vp_ref[:, ks], vc_ref[:, ks]], axis=0)
        for h0 in range(g * GROUP, (g + 1) * GROUP, HEADS_PER_STORE):
            outs = []
            for h in range(h0, h0 + HEADS_PER_STORE):
                hs = slice(h * HEAD_DIM, (h + 1) * HEAD_DIM)
                s = lax.dot_general(q_ref[:, hs], kcat, (((1,), (1,)), ((), ())),
                                    preferred_element_type=F32)
                s = jnp.where(no_prev, NEG_BIG, s + bias_ref[h])
                sink = sink_ref[h]
                m = jnp.maximum(jnp.max(s, axis=-1, keepdims=True), sink)
                p = jnp.exp(s - m)
                denom = jnp.sum(p, axis=-1, keepdims=True) + jnp.exp(sink - m)
                o = jnp.dot(p.astype(BF16), vcat, preferred_element_type=F32)
                outs.append((o / denom).astype(BF16))
            o_ref[:, h0 * HEAD_DIM:(h0 + HEADS_PER_STORE) * HEAD_DIM] = jnp.concatenate(outs, axis=1)


def _attn_prompt(q, k, v, bias, sinks, batch, seq):
    nb = seq // ATTN_BLOCK
    cur = lambda b, i: (b * nb + i, 0)
    prev = lambda b, i: (b * nb + jnp.maximum(i - 1, 0), 0)
    return pl.pallas_call(
        _attn_prompt_kernel,
        grid=(batch, nb),
        in_specs=[pl.BlockSpec(memory_space=pltpu.SMEM),
                  pl.BlockSpec((ATTN_BLOCK, Q_DIM), cur),
                  pl.BlockSpec((ATTN_BLOCK, KV_DIM), cur),
                  pl.BlockSpec((ATTN_BLOCK, KV_DIM), prev),
                  pl.BlockSpec((ATTN_BLOCK, KV_DIM), cur),
                  pl.BlockSpec((ATTN_BLOCK, KV_DIM), prev),
                  _const_spec(bias.shape)],
        out_specs=pl.BlockSpec((ATTN_BLOCK, Q_DIM), cur),
        out_shape=jax.ShapeDtypeStruct((batch * seq, Q_DIM), BF16),
        compiler_params=pltpu.CompilerParams(dimension_semantics=("arbitrary", "arbitrary"),
                                             vmem_limit_bytes=V7X_VMEM_LIMIT_BYTES),
        name="attn_prompt",
    )(sinks, q, k, k, v, v, bias)


def _attn_sample_kernel(qbd_ref, ck_ref, cv_ref, kvn_ref, bias_ref, sink_ref, mask_ref, o_ref, *, w_buf):
    bf16_round = lambda t: t.astype(BF16).astype(F32)
    seqs = range(SAMPLE_SEQ_PER_STEP)
    sink = sink_ref[...]
    s = [lax.dot_general(qbd_ref[b], ck_ref[b].astype(BF16), (((1,), (1,)), ((), ())),
                         preferred_element_type=F32) + bias_ref[:, :w_buf] for b in seqs]
    s_new = [jnp.sum(qbd_ref[b].astype(F32) * bf16_round(kvn_ref[b:b + 1, :KV_DIM]), axis=-1, keepdims=True)
             + bias_ref[:, w_buf:w_buf + 1] for b in seqs]
    m = [jnp.maximum(jnp.maximum(jnp.max(s[b], axis=-1, keepdims=True), s_new[b]), sink) for b in seqs]
    p = [jnp.exp(s[b] - m[b]) for b in seqs]
    p_new = [jnp.exp(s_new[b] - m[b]) for b in seqs]
    denom = [jnp.sum(p[b], axis=-1, keepdims=True) + p_new[b] + jnp.exp(sink - m[b]) for b in seqs]
    of = [jnp.dot((p[b] / denom[b]).astype(BF16), cv_ref[b].astype(BF16), preferred_element_type=F32)
          + bf16_round(p_new[b] / denom[b]) * bf16_round(kvn_ref[b:b + 1, KV_DIM:]) for b in seqs]
    for b in seqs:
        ob = of[b] * mask_ref[...]
        o_ref[b] = (ob[:, 0:HEAD_DIM] + ob[:, HEAD_DIM:2 * HEAD_DIM]
                    + ob[:, 2 * HEAD_DIM:3 * HEAD_DIM] + ob[:, 3 * HEAD_DIM:]).astype(BF16)


def _attn_sample(qbd, ck, cv, kvn, bias, sink_col, head_mask):
    nseq, w_buf = ck.shape[0], ck.shape[1]
    sb = SAMPLE_SEQ_PER_STEP
    return pl.pallas_call(
        functools.partial(_attn_sample_kernel, w_buf=w_buf),
        grid=(nseq // sb,),
        in_specs=[pl.BlockSpec((sb, N_HEADS, KV_DIM), lambda i: (i, 0, 0)),
                  pl.BlockSpec((sb, w_buf, KV_DIM), lambda i: (i, 0, 0)),
                  pl.BlockSpec((sb, w_buf, KV_DIM), lambda i: (i, 0, 0)),
                  pl.BlockSpec((sb, 2 * KV_DIM), lambda i: (i, 0)),
                  _const_spec(bias.shape), _const_spec(sink_col.shape), _const_spec(head_mask.shape)],
        out_specs=pl.BlockSpec((sb, N_HEADS, HEAD_DIM), lambda i: (i, 0, 0)),
        out_shape=jax.ShapeDtypeStruct((nseq, N_HEADS, HEAD_DIM), BF16),
        compiler_params=pltpu.CompilerParams(dimension_semantics=("arbitrary",),
                                             vmem_limit_bytes=V7X_VMEM_LIMIT_BYTES),
        name="attn_sample",
    )(qbd, ck, cv, kvn, bias, sink_col, head_mask)


def _route_rows(logits, row0, valid_rows, half):
    tm = logits.shape[0]
    lane = lax.broadcasted_iota(jnp.int32, logits.shape, 1)
    lane_f = lane.astype(F32)
    no_lane = float(ROUTER_LANES)

    def top1(mask):
        best = jnp.max(jnp.where(mask, logits, -jnp.inf), axis=-1, keepdims=True)
        idx = jnp.min(jnp.where(jnp.logical_and(mask, logits == best), lane_f, no_lane), axis=-1, keepdims=True)
        return best, idx

    gmask = lane < N_EXPERT_GROUPS
    gmax, grp = top1(gmask)
    gsum = jnp.sum(jnp.where(gmask, jnp.exp(logits - gmax), 0.0), axis=-1, keepdims=True)
    p_grp = 1.0 / gsum
    lo = N_EXPERT_GROUPS + EXPERTS_PER_GROUP * grp
    emask = jnp.logical_and(lane_f >= lo, lane_f < lo + EXPERTS_PER_GROUP)
    v1, i1 = top1(emask)
    v2, i2 = top1(jnp.logical_and(emask, lane_f != i1))
    e21 = jnp.exp(v2 - v1)
    w1 = p_grp / (1.0 + e21)
    w2 = p_grp * e21 / (1.0 + e21)

    oh1 = lane_f == i1
    oh2 = lane_f == i2
    if valid_rows < tm:
        valid = lax.broadcasted_iota(jnp.int32, logits.shape, 0) < valid_rows
        oh1 = jnp.logical_and(oh1, valid)
        oh2 = jnp.logical_and(oh2, valid)
    oh = oh1.astype(F32) + oh2.astype(F32)
    token = row0 + lax.broadcasted_iota(jnp.int32, (tm, 1), 0)
    key1 = (i1.astype(jnp.int32) - N_EXPERT_GROUPS) * (1 << ASSIGN_BITS) + token
    key2 = (i2.astype(jnp.int32) - N_EXPERT_GROUPS) * (1 << ASSIGN_BITS) + token + half
    w1b = lax.bitcast_convert_type(w1, jnp.int32)
    w2b = lax.bitcast_convert_type(w2, jnp.int32)
    words = jnp.where(lane == 0, key1, jnp.where(lane == 1, key2, jnp.where(lane == 2, w1b,
                      jnp.where(lane == 3, w2b, 0))))
    return words, jnp.sum(oh, axis=0, keepdims=True)


def _store_token_tiles(ref, x):
    n = x.shape[0]
    for c in range(D_MODEL // LANES):
        ref[pl.ds(c, n, stride=TILE_ROWS), :] = x[:, c * LANES:(c + 1) * LANES]


def _load_token_tiles(ref, n):
    return jnp.concatenate([ref[pl.ds(c, n, stride=TILE_ROWS), :] for c in range(D_MODEL // LANES)], axis=1)


def _pack_bf16_pairs(x):
    hw = x.shape[1] // 2
    bits = lambda v: lax.bitcast_convert_type(v.astype(BF16).astype(F32), jnp.uint32)
    return (bits(x[:, hw:]) & jnp.uint32(0xFFFF0000)) | (bits(x[:, :hw]) >> 16)


def _unpack_bf16_pairs(w):
    lo = lax.bitcast_convert_type(w << 16, F32)
    hi = lax.bitcast_convert_type(w & jnp.uint32(0xFFFF0000), F32)
    return jnp.concatenate([lo, hi], axis=1).astype(BF16)


def _out_proj_rows(yc_ref, o_ref, sa_ref, sb_ref, x_ref, wc_ref, wa_ref, wo_ref, g2_ref, wr_ref, br_ref,
                   x2_ref, h2p_ref, route_ref, cnt_ref, *, valid_rows, half):
    y_conv = jnp.dot(yc_ref[...], wc_ref[...], preferred_element_type=F32)
    y_attn = jnp.dot(o_ref[...], wa_ref[...], preferred_element_type=F32)
    mix = (sa_ref[...].astype(F32) * y_conv + sb_ref[...].astype(F32) * y_attn).astype(BF16)
    x2 = x_ref[...] + jnp.dot(mix, wo_ref[...], preferred_element_type=F32)
    x2_ref[...] = x2
    h2 = _rms_norm_f32(x2, g2_ref[...])
    h2p_ref[...] = _pack_bf16_pairs(h2)
    logits = jnp.dot(h2.astype(BF16), wr_ref[...], preferred_element_type=F32) + br_ref[...]
    words, cnt = _route_rows(logits, pl.program_id(0) * x_ref.shape[0], valid_rows, half)
    route_ref[...] = words
    cnt_ref[...] += cnt


def _out_proj_kernel(*refs, n_first, valid_rows_second, half):
    first, second, shared = refs[0:5], refs[5:10], refs[10:]
    cnt_ref = shared[-1]
    tm = first[4].shape[0]

    @pl.when(pl.program_id(0) == 0)
    def _():
        cnt_ref[...] = jnp.zeros_like(cnt_ref)

    @pl.when(pl.program_id(0) < n_first)
    def _():
        _out_proj_rows(*first, *shared, valid_rows=tm, half=half)

    @pl.when(pl.program_id(0) >= n_first)
    def _():
        _out_proj_rows(*second, *shared, valid_rows=valid_rows_second, half=half)


def _out_proj(acts_a, acts_b, wc, wa, wo, g2, wr, br, *, tm, valid_rows_b, half):
    na = acts_a[4].shape[0] // tm
    nb = acts_b[4].shape[0] // tm
    assert nb == 1
    m_total = (na + nb) * tm
    spec_a = lambda width: pl.BlockSpec((tm, width), lambda i: (jnp.minimum(i, na - 1), 0))
    spec_b = lambda width: pl.BlockSpec((tm, width), lambda i: (jnp.maximum(i - na, 0), 0))
    widths = (D_CONV, Q_DIM, D_MODEL, D_MODEL, D_MODEL)
    in_specs = [spec_a(w) for w in widths] + [spec_b(w) for w in widths]
    in_specs += [_const_spec(wc.shape), _const_spec(wa.shape), _const_spec(wo.shape),
                 _const_spec(g2.shape), _const_spec(wr.shape), _const_spec(br.shape)]
    orow = lambda width: pl.BlockSpec((tm, width), lambda i: (i, 0))
    return pl.pallas_call(
        functools.partial(_out_proj_kernel, n_first=na, valid_rows_second=valid_rows_b, half=half),
        grid=(na + nb,),
        in_specs=in_specs,
        out_specs=[orow(D_MODEL), orow(D_MODEL // 2), orow(ROUTER_LANES),
                   pl.BlockSpec((1, ROUTER_LANES), lambda i: (0, 0))],
        out_shape=[jax.ShapeDtypeStruct((m_total, D_MODEL), F32),
                   jax.ShapeDtypeStruct((m_total, D_MODEL // 2), jnp.uint32),
                   jax.ShapeDtypeStruct((m_total, ROUTER_LANES), jnp.int32),
                   jax.ShapeDtypeStruct((1, ROUTER_LANES), F32)],
        compiler_params=pltpu.CompilerParams(dimension_semantics=("arbitrary",),
                                             vmem_limit_bytes=V7X_VMEM_LIMIT_BYTES),
        name="out_proj",
    )(*acts_a, *acts_b, wc, wa, wo, g2, wr, br)


def _block_table_kernel(counts_ref, bexp_ref, bpos_ref, bcnt_ref, bslot_ref, bnext_ref, nused_ref, first_ref,
                        *, n_blocks):
    def per_expert(e, carry):
        blk0, pos0, ordinal = carry
        cnt = counts_ref[e]
        nblk = lax.shift_right_logical(cnt + (MOE_BLOCK - 1), MOE_BLOCK_LOG2)
        first_ref[e] = jnp.where(nblk > 0, blk0, -1)

        def mark(b, c):
            off = (b - blk0) * MOE_BLOCK
            bexp_ref[b] = e
            bpos_ref[b] = pos0 + off
            bcnt_ref[b] = jnp.minimum(cnt - off, MOE_BLOCK)
            bslot_ref[b] = jnp.where(b == blk0, ordinal & 1, -1)
            bnext_ref[b] = -1
            return c
        lax.fori_loop(blk0, blk0 + nblk, mark, 0)
        return blk0 + nblk, pos0 + cnt, ordinal + jnp.where(nblk > 0, 1, 0)

    n_used, _, _ = lax.fori_loop(0, N_EXPERTS, per_expert, (0, 0, 0))
    nused_ref[0] = n_used

    def unused(b, c):
        bexp_ref[b] = N_EXPERTS - 1
        bpos_ref[b] = 0
        bcnt_ref[b] = 0
        bslot_ref[b] = -1
        bnext_ref[b] = -1
        return c
    lax.fori_loop(n_used, n_blocks, unused, 0)

    def link(k, nxt):
        e = N_EXPERTS - 1 - k
        fb = first_ref[e]

        @pl.when(fb >= 0)
        def _():
            bnext_ref[fb] = nxt
        return jnp.where(fb >= 0, e, nxt)
    lax.fori_loop(0, N_EXPERTS, link, -1)


def _block_tables(counts, n_blocks):
    smem = pl.BlockSpec(memory_space=pltpu.SMEM)
    blk = jax.ShapeDtypeStruct((n_blocks,), jnp.int32)
    return pl.pallas_call(
        functools.partial(_block_table_kernel, n_blocks=n_blocks),
        in_specs=[smem],
        out_specs=[smem] * 6,
        out_shape=[blk] * 5 + [jax.ShapeDtypeStruct((1,), jnp.int32)],
        scratch_shapes=[pltpu.SMEM((N_EXPERTS,), jnp.int32)],
        name="block_tables",
    )(counts)


def _moe_kernel(tok_ref, order_ref, bexp_ref, bpos_ref, bcnt_ref, bslot_ref, bnext_ref, nused_ref,
                h2p_ref, wg_hbm, wu_hbm, wd_hbm, contrib_hbm,
                xs_ref, ys_ref, wgf_ref, wuf_ref, wdf_ref, wgb_ref, wub_ref, wdb_ref, ssem_ref, wsem_ref,
                *, m_total, half, n_blocks):
    i = pl.program_id(0)
    slot = i % 2
    n_used = nused_ref[0]
    active = i < n_used
    last_active = i == n_used - 1
    tile = lambda t: pl.ds(pl.multiple_of(t * TILE_ROWS, TILE_ROWS), TILE_ROWS)

    def gather(pos0, slt):
        for r in range(MOE_BLOCK):
            xs_ref[slt, pl.ds(r, 1), :] = h2p_ref[pl.ds(tok_ref[pos0 + r], 1), :]

    def scatter(pos0, cnt, trash0, slt, r):
        dst = jnp.where(r < cnt, order_ref[pos0 + r], trash0 + r)
        return pltpu.make_async_copy(ys_ref.at[slt, tile(r), :], contrib_hbm.at[tile(dst), :], ssem_ref.at[slt])

    def scatter_wait(slt, r):
        pltpu.make_async_copy(ys_ref.at[slt, tile(r), :], contrib_hbm.at[tile(0), :], ssem_ref.at[slt]).wait()

    @pl.when(i == 0)
    def _():
        ys_ref[...] = jnp.zeros_like(ys_ref)
        gap = half - m_total
        fills = [(2 * half + s * MOE_BLOCK, MOE_BLOCK) for s in range(2)]
        fills += [(k * half + m_total, gap) for k in range(TOP_K)] if gap else []
        for start, n in fills:
            fill = pltpu.make_async_copy(ys_ref.at[0, pl.ds(0, n * TILE_ROWS), :],
                                         contrib_hbm.at[pl.ds(start * TILE_ROWS, n * TILE_ROWS), :], ssem_ref.at[0])
            fill.start()
            fill.wait()

    def weight_copies(e, s):
        return [pltpu.make_async_copy(src.at[e], dst.at[s], wsem_ref.at[s])
                for src, dst in ((wg_hbm, wgf_ref), (wu_hbm, wuf_ref), (wd_hbm, wdf_ref))]

    @pl.when(i == 0)
    def _():
        for c in weight_copies(bexp_ref[0], 0):
            c.start()
        gather(bpos_ref[0], 0)

    @pl.when(jnp.logical_and(active, i >= 2))
    def _():
        for r in range(MOE_BLOCK):
            scatter_wait(slot, r)

    wslot = bslot_ref[i]

    @pl.when(wslot >= 0)
    def _():
        for c in weight_copies(0, wslot):
            c.wait()

    @pl.when(jnp.logical_and(wslot >= 0, bnext_ref[i] >= 0))
    def _():
        for c in weight_copies(bnext_ref[i], 1 - wslot):
            c.start()

    @pl.when(wslot >= 0)
    def _():
        wgb_ref[...] = wgf_ref[wslot].astype(BF16)
        wub_ref[...] = wuf_ref[wslot].astype(BF16)
        wdb_ref[...] = wdf_ref[wslot].astype(BF16)

    def run_block(slt):
        pos_next = bpos_ref[jnp.minimum(i + 1, n_blocks - 1)]
        pos0 = bpos_ref[i]
        cnt = bcnt_ref[i]
        trash0 = 2 * half + slt * MOE_BLOCK
        gather(pos_next, 1 - slt)
        xb = _unpack_bf16_pairs(xs_ref[slt])
        gate = jnp.dot(xb, wgb_ref[...], preferred_element_type=F32)
        up = jnp.dot(xb, wub_ref[...], preferred_element_type=F32)
        hmid = (jax.nn.silu(gate) * up).astype(BF16)
        _store_token_tiles(ys_ref.at[slt], jnp.dot(hmid, wdb_ref[...], preferred_element_type=F32))
        for r in range(MOE_BLOCK):
            scatter(pos0, cnt, trash0, slt, r).start(priority=r % 2)

    for slt in range(2):
        pl.when(jnp.logical_and(active, slot == slt))(functools.partial(run_block, slt))

    @pl.when(last_active)
    def _():
        for r in range(MOE_BLOCK):
            scatter_wait(slot, r)

    @pl.when(jnp.logical_and(last_active, i >= 1))
    def _():
        for r in range(MOE_BLOCK):
            scatter_wait(1 - slot, r)


def _moe_experts(tok, order, tables, h2p, w_gate, w_up, w_down, *, m_total, half):
    n_blocks = tables[0].shape[0]
    assert 0 <= half - m_total <= MOE_BLOCK
    hbm = pl.BlockSpec(memory_space=pl.ANY)
    grid_spec = pltpu.PrefetchScalarGridSpec(
        num_scalar_prefetch=2 + len(tables),
        grid=(n_blocks,),
        in_specs=[_const_spec(h2p.shape), hbm, hbm, hbm],
        out_specs=hbm,
        scratch_shapes=[pltpu.VMEM((2, MOE_BLOCK, D_MODEL // 2), jnp.uint32),
                        pltpu.VMEM((2, MOE_BLOCK * TILE_ROWS, LANES), F32),
                        pltpu.VMEM((2, D_MODEL, D_EXPERT), F32),
                        pltpu.VMEM((2, D_MODEL, D_EXPERT), F32),
                        pltpu.VMEM((2, D_EXPERT, D_MODEL), F32),
                        pltpu.VMEM((D_MODEL, D_EXPERT), BF16),
                        pltpu.VMEM((D_MODEL, D_EXPERT), BF16),
                        pltpu.VMEM((D_EXPERT, D_MODEL), BF16),
                        pltpu.SemaphoreType.DMA((2,)),
                        pltpu.SemaphoreType.DMA((2,))],
    )
    return pl.pallas_call(
        functools.partial(_moe_kernel, m_total=m_total, half=half, n_blocks=n_blocks),
        grid_spec=grid_spec,
        out_shape=jax.ShapeDtypeStruct(((2 * half + 2 * MOE_BLOCK) * TILE_ROWS, LANES), F32),
        compiler_params=pltpu.CompilerParams(dimension_semantics=("arbitrary",),
                                             vmem_limit_bytes=MOE_VMEM_LIMIT_BYTES),
        name="moe_experts",
    )(tok, order, *tables, h2p, w_gate, w_up, w_down)


def _combine_kernel(c0_ref, c1_ref, x2_ref, route_ref, gf_ref, y_ref):
    tc = x2_ref.shape[0]
    w = lax.bitcast_convert_type(route_ref[:, TOP_K:2 * TOP_K], F32)
    moe = w[:, 0:1] * _load_token_tiles(c0_ref, tc) + w[:, 1:2] * _load_token_tiles(c1_ref, tc)
    y_ref[...] = _rms_norm_f32(x2_ref[...] + moe, gf_ref[...])


def _moe_combine(contrib, x2, route, gf, *, row_off, m, tc, half):
    off = row_off // tc
    assert row_off % tc == 0 and half % tc == 0
    ctile = lambda k: pl.BlockSpec((tc * TILE_ROWS, LANES), lambda i: (i + off + k * (half // tc), 0))
    return pl.pallas_call(
        _combine_kernel,
        grid=(m // tc,),
        in_specs=[ctile(0), ctile(1),
                  pl.BlockSpec((tc, D_MODEL), lambda i: (i + off, 0)),
                  pl.BlockSpec((tc, ROUTER_LANES), lambda i: (i + off, 0)),
                  _const_spec((1, D_MODEL))],
        out_specs=pl.BlockSpec((tc, D_MODEL), lambda i: (i, 0)),
        out_shape=jax.ShapeDtypeStruct((m, D_MODEL), F32),
        compiler_params=pltpu.CompilerParams(dimension_semantics=("arbitrary",),
                                             vmem_limit_bytes=V7X_VMEM_LIMIT_BYTES),
        name="moe_combine_prompt" if row_off == 0 else "moe_combine_sample",
    )(contrib, contrib, x2, route, gf)


def _t5_bucket(dist):
    n = jnp.maximum(dist, 0)
    max_exact = N_BUCKETS // 2
    nf = jnp.maximum(n, 1).astype(F32)
    large = max_exact + (jnp.log(nf / max_exact) / math.log(MAX_DISTANCE / max_exact)
                         * (N_BUCKETS - max_exact)).astype(jnp.int32)
    large = jnp.minimum(large, N_BUCKETS - 1)
    return jnp.where(n < max_exact, n, large)


def _bucket_bias(rel_bias, dist):
    onehot = (_t5_bucket(dist)[..., None] == jnp.arange(N_BUCKETS, dtype=jnp.int32)).astype(F32)
    return jnp.dot(onehot, rel_bias.astype(F32), precision=lax.Precision.HIGHEST)


def _prompt_bias_table(rel_bias):
    qi = jnp.arange(ATTN_BLOCK, dtype=jnp.int32)[:, None]
    kj = jnp.arange(2 * ATTN_BLOCK, dtype=jnp.int32)[None, :] - ATTN_BLOCK
    dist = qi - kj
    valid = (dist >= 0) & (dist <= WINDOW)
    bias = jnp.where(valid[..., None], _bucket_bias(rel_bias, dist), NEG_BIG)
    return jnp.moveaxis(bias, -1, 0)


def _sample_bias_table(rel_bias, w_buf):
    j = jnp.arange(w_buf + 1, dtype=jnp.int32)
    dist = w_buf - j
    return jnp.where((dist <= WINDOW)[:, None], _bucket_bias(rel_bias, dist), NEG_BIG).T


def kernel(x_prompt, x_sample, cache_conv, cache_k, cache_v, norm1_g, w_in, conv_w, w_conv_out, w_attn_out, w_o, sinks, rel_bias, norm2_g, w_router_group, b_router_group, w_router_expert, b_router_expert, w_e_gate, w_e_up, w_e_down, norm_f_g):
    assert norm1_g.shape[0] == 1, "single-layer configuration"
    batch, seq, _ = x_prompt.shape
    nseq = x_sample.shape[0]
    w_buf = cache_k.shape[2]
    mp = batch * seq
    m_total = mp + nseq
    assert seq % TM_DENSE == 0 and seq % ATTN_BLOCK == 0 and mp % COMBINE_BLOCK == 0
    assert nseq % SAMPLE_SEQ_PER_STEP == 0 and mp % nseq == 0
    assert TOP_K == 2 and MOE_BLOCK == 1 << MOE_BLOCK_LOG2 and m_total * TOP_K < 1 << ASSIGN_BITS

    g1 = norm1_g[0][None, :]
    g2 = norm2_g[0][None, :]
    gf = norm_f_g[None, :]
    wi = w_in[0].astype(BF16)
    cw = conv_w[0]
    wc = w_conv_out[0].astype(BF16)
    wa = w_attn_out[0].astype(BF16)
    wo = w_o[0].astype(BF16)
    pad_cols = ROUTER_LANES - N_EXPERT_GROUPS - N_EXPERTS
    wr = jnp.concatenate([w_router_group[0], w_router_expert[0],
                          jnp.zeros((D_MODEL, pad_cols), F32)], axis=1).astype(BF16)
    br = jnp.concatenate([b_router_group[0], b_router_expert[0], jnp.zeros((pad_cols,), F32)])[None, :]
    sink = sinks[0].astype(F32)

    xp = x_prompt.reshape(mp, D_MODEL)
    bps = seq // TM_DENSE
    yc_p, q_p, k_p, v_p, sa_p, sb_p, ut_p, kvt_p = _in_proj(
        xp, g1, wi, cw, tm=TM_DENSE, blocks_per_seq=bps, u_tail=8, kv_tail=WINDOW)
    o_p = _attn_prompt(q_p, k_p, v_p, _prompt_bias_table(rel_bias), sink, batch, seq)

    pad_rows = lambda t: jnp.pad(t, ((0, TM_DENSE - nseq), (0, 0)))
    xs = pad_rows(x_sample.reshape(nseq, D_MODEL))
    hist = (pad_rows(cache_conv[0][:, 0, :]), pad_rows(cache_conv[0][:, 1, :]))
    yc_s, q_s, _, _, sa_s, sb_s, ut_s, kvt_s = _in_proj(
        xs, g1, wi, cw, tm=TM_DENSE, blocks_per_seq=1, u_tail=TM_DENSE, kv_tail=TM_DENSE, hist=hist,
        gate_dtype=F32)
    u_s = ut_s[0, :nseq]
    kv_s = kvt_s[0, :nseq]
    head_mask = (jnp.arange(KV_DIM)[None, :] // HEAD_DIM == jnp.arange(N_HEADS)[:, None] // GROUP)
    qbd = (jnp.tile(q_s[:nseq].reshape(nseq, N_HEADS, HEAD_DIM), (1, 1, N_KV_HEADS))
           * head_mask[None].astype(BF16))
    o_s = _attn_sample(qbd, cache_k[0].reshape(nseq, w_buf, KV_DIM), cache_v[0].reshape(nseq, w_buf, KV_DIM),
                       kv_s, _sample_bias_table(rel_bias, w_buf), sink[:, None], head_mask.astype(F32))
    o_s = pad_rows(o_s.reshape(nseq, Q_DIM))

    half = -(-m_total // COMBINE_BLOCK) * COMBINE_BLOCK
    assert half % nseq == 0 and TOP_K * half < 1 << ASSIGN_BITS
    x2, h2p, route, cnt = _out_proj((yc_p, o_p, sa_p, sb_p, xp), (yc_s, o_s, sa_s, sb_s, xs),
                                    wc, wa, wo, g2, wr, br, tm=TM_DENSE, valid_rows_b=nseq, half=half)

    n_assign = m_total * TOP_K
    keys = route[:m_total, 0:TOP_K].reshape(-1)
    counts = cnt[0, N_EXPERT_GROUPS:N_EXPERT_GROUPS + N_EXPERTS].astype(jnp.int32)
    order = jnp.pad(jnp.sort(keys) & ((1 << ASSIGN_BITS) - 1), (0, MOE_BLOCK))
    tok = jnp.where(order >= half, order - half, order)
    n_blocks = -(-n_assign // MOE_BLOCK) + N_EXPERTS
    tables = _block_tables(counts, n_blocks)
    contrib = _moe_experts(tok, order, tables, h2p, w_e_gate[0], w_e_up[0], w_e_down[0],
                           m_total=m_total, half=half)
    y_p = _moe_combine(contrib, x2, route, gf, row_off=0, m=mp, tc=COMBINE_BLOCK, half=half)
    y_s = _moe_combine(contrib, x2, route, gf, row_off=mp, m=nseq, tc=nseq, half=half)

    y_prompt = y_p.reshape(batch, seq, D_MODEL)
    y_sample = y_s.reshape(nseq, 1, D_MODEL)
    conv_state_prompt = ut_p.reshape(batch, bps, 8, D_CONV)[:, -1, 8 - (CONV_WIDTH - 1):, :][None]
    kv_last = kvt_p.reshape(batch, bps, WINDOW, 2 * KV_DIM)[:, -1]
    k_win_prompt = kv_last[:, :, :KV_DIM].reshape(batch, WINDOW, N_KV_HEADS, HEAD_DIM)[None]
    v_win_prompt = kv_last[:, :, KV_DIM:].reshape(batch, WINDOW, N_KV_HEADS, HEAD_DIM)[None]
    conv_state_sample = jnp.concatenate([cache_conv[0][:, 1:, :], u_s[:, None, :]], axis=1)[None]
    k_new = kv_s[:, :KV_DIM].reshape(nseq, 1, N_KV_HEADS, HEAD_DIM)
    v_new = kv_s[:, KV_DIM:].reshape(nseq, 1, N_KV_HEADS, HEAD_DIM)
    k_win_sample = jnp.concatenate([cache_k[0], k_new], axis=1)[:, -w_buf:][None]
    v_win_sample = jnp.concatenate([cache_v[0], v_new], axis=1)[:, -w_buf:][None]
    return (y_prompt, y_sample, conv_state_prompt, k_win_prompt, v_win_prompt,
            conv_state_sample, k_win_sample, v_win_sample)
```

```python
import functools
import math

import jax
import jax.numpy as jnp
from jax import lax
from jax.experimental import pallas as pl
from jax.experimental.pallas import tpu as pltpu

D_MODEL = 1024
D_CONV = 1024
CONV_WIDTH = 3
N_HEADS = 16
N_KV_HEADS = 4
HEAD_DIM = 64
GROUP = N_HEADS // N_KV_HEADS
WINDOW = 128
Q_DIM = N_HEADS * HEAD_DIM
KV_DIM = N_KV_HEADS * HEAD_DIM
N_BUCKETS = 32
MAX_DISTANCE = 128
N_EXPERT_GROUPS = 4
EXPERTS_PER_GROUP = 8
N_EXPERTS = N_EXPERT_GROUPS * EXPERTS_PER_GROUP
TOP_K = 2
D_EXPERT = 512
EPS = 1e-6
PAST_LEN = 8192

BF16 = jnp.bfloat16
F32 = jnp.float32
NEG_BIG = -1e30

V7X_VMEM_LIMIT_BYTES = 56 * 1024 * 1024
MOE_VMEM_LIMIT_BYTES = 62 * 1024 * 1024
TILE_ROWS = 8
LANES = 128
ROUTER_LANES = 128
TM_DENSE = 512
ATTN_BLOCK = 128
MOE_BLOCK = 256
MOE_BLOCK_LOG2 = 8
ASSIGN_BITS = 16
COMBINE_BLOCK = 256
SAMPLE_SEQ_PER_STEP = 8
HEADS_PER_STORE = LANES // HEAD_DIM


def _const_spec(shape):
    nd = len(shape)
    return pl.BlockSpec(shape, lambda *_: (0,) * nd, pipeline_mode=pl.Buffered(1))


def _rms_norm_f32(xf, g):
    return xf * lax.rsqrt(jnp.mean(xf * xf, axis=-1, keepdims=True) + EPS) * g


def _in_proj_kernel(*refs, tm, sample, blocks_per_seq, u_tail, kv_tail):
    if sample:
        (x_ref, hist0_ref, hist1_ref, g_ref, w_ref,
         cw_ref, yc_ref, q_ref, k_ref, v_ref, sa_ref, sb_ref, ut_ref, kvt_ref) = refs
    else:
        (x_ref, g_ref, w_ref,
         cw_ref, yc_ref, q_ref, k_ref, v_ref, sa_ref, sb_ref, ut_ref, kvt_ref, ubuf_ref) = refs

    h = _rms_norm_f32(x_ref[...], g_ref[...]).astype(BF16)

    widths = (D_CONV, D_CONV, D_CONV, Q_DIM, 2 * KV_DIM, D_MODEL, D_MODEL)
    starts = [sum(widths[:n]) for n in range(len(widths))]
    wcb_ref, wcc_ref, wch_ref, wq_ref, wkv_ref, wga_ref, wgb_ref = [
        w_ref.at[:, pl.ds(a, n)] for a, n in zip(starts, widths)]

    def proj(part_ref):
        return jnp.dot(h, part_ref[...], preferred_element_type=F32)

    u = proj(wcc_ref) * proj(wch_ref)
    w0 = cw_ref[0:1, :]
    w1 = cw_ref[1:2, :]
    w2 = cw_ref[2:3, :]
    if sample:
        conv = w0 * hist0_ref[...] + w1 * hist1_ref[...] + w2 * u
    else:
        @pl.when(pl.program_id(0) % blocks_per_seq == 0)
        def _():
            ubuf_ref[0:8, :] = jnp.zeros((8, D_CONV), F32)

        ubuf_ref[8:8 + tm, :] = u
        conv = w0 * ubuf_ref[6:6 + tm, :] + w1 * ubuf_ref[7:7 + tm, :] + w2 * u
        ubuf_ref[0:8, :] = u[tm - 8:, :]
    yc_ref[...] = (proj(wcb_ref) * conv).astype(BF16)
    ut_ref[0] = u[tm - u_tail:, :]

    q_ref[...] = (proj(wq_ref) * (HEAD_DIM ** -0.5)).astype(BF16)
    kv = proj(wkv_ref)
    k_ref[...] = kv[:, :KV_DIM].astype(BF16)
    v_ref[...] = kv[:, KV_DIM:].astype(BF16)
    kvt_ref[0] = kv[tm - kv_tail:, :]
    sa_ref[...] = jax.nn.sigmoid(proj(wga_ref)).astype(sa_ref.dtype)
    sb_ref[...] = jax.nn.sigmoid(proj(wgb_ref)).astype(sb_ref.dtype)


def _in_proj(x, g1, w_in, conv_w, *, tm, blocks_per_seq, u_tail, kv_tail, hist=None, gate_dtype=BF16):
    m = x.shape[0]
    nblk = m // tm
    sample = hist is not None
    row = lambda width: pl.BlockSpec((tm, width), lambda i: (i, 0))
    in_specs = [row(D_MODEL)]
    args = [x]
    if sample:
        in_specs += [row(D_CONV), row(D_CONV)]
        args += list(hist)
    in_specs += [_const_spec((1, D_MODEL)), _const_spec(w_in.shape), _const_spec(conv_w.shape)]
    args += [g1, w_in, conv_w]
    out_shape = [
        jax.ShapeDtypeStruct((m, D_CONV), BF16),
        jax.ShapeDtypeStruct((m, Q_DIM), BF16),
        jax.ShapeDtypeStruct((m, KV_DIM), BF16),
        jax.ShapeDtypeStruct((m, KV_DIM), BF16),
        jax.ShapeDtypeStruct((m, D_MODEL), gate_dtype),
        jax.ShapeDtypeStruct((m, D_MODEL), gate_dtype),
        jax.ShapeDtypeStruct((nblk, u_tail, D_CONV), F32),
        jax.ShapeDtypeStruct((nblk, kv_tail, 2 * KV_DIM), F32),
    ]
    out_specs = [row(D_CONV), row(Q_DIM), row(KV_DIM), row(KV_DIM), row(D_MODEL), row(D_MODEL),
                 pl.BlockSpec((1, u_tail, D_CONV), lambda i: (i, 0, 0)),
                 pl.BlockSpec((1, kv_tail, 2 * KV_DIM), lambda i: (i, 0, 0))]
    scratch = [] if sample else [pltpu.VMEM((tm + 8, D_CONV), F32)]
    return pl.pallas_call(
        functools.partial(_in_proj_kernel, tm=tm, sample=sample, blocks_per_seq=blocks_per_seq,
                          u_tail=u_tail, kv_tail=kv_tail),
        grid=(nblk,),
        in_specs=in_specs,
        out_specs=out_specs,
        out_shape=out_shape,
        scratch_shapes=scratch,
        compiler_params=pltpu.CompilerParams(dimension_semantics=("arbitrary",),
                                             vmem_limit_bytes=V7X_VMEM_LIMIT_BYTES),
        name="in_proj_sample" if sample else "in_proj_prompt",
    )(*args)


def _attn_prompt_kernel(sink_ref, q_ref, kc_ref, kp_ref, vc_ref, vp_ref, bias_ref, o_ref):
    first = pl.program_id(1) == 0
    col = lax.broadcasted_iota(jnp.int32, (ATTN_BLOCK, 2 * ATTN_BLOCK), 1)
    no_prev = jnp.logical_and(first, col < ATTN_BLOCK)
    for g in range(N_KV_HEADS):
        ks = slice(g * HEAD_DIM, (g + 1) * HEAD_DIM)
        kcat = jnp.concatenate([kp_ref[:, ks], kc_ref[:, ks]], axis=0)
        vcat = jnp.concatenate([vp_ref[:, ks], vc_ref[:, ks]], axis=0)
        for h0 in range(g * GROUP, (g + 1) * GROUP, HEADS_PER_STORE):
            outs = []
            for h in range(h0, h0 + HEADS_PER_STORE):
                hs = slice(h * HEAD_DIM, (h + 1) * HEAD_DIM)
                s = lax.dot_general(q_ref[:, hs], kcat, (((1,), (1,)), ((), ())),
                                    preferred_element_type=F32)
                s = jnp.where(no_prev, NEG_BIG, s + bias_ref[h])
                sink = sink_ref[h]
                m = jnp.maximum(jnp.max(s, axis=-1, keepdims=True), sink)
                p = jnp.exp(s - m)
                denom = jnp.sum(p, axis=-1, keepdims=True) + jnp.exp(sink - m)
                o = jnp.dot(p.astype(BF16), vcat, preferred_element_type=F32)
                outs.append((o / denom).astype(BF16))
            o_ref[:, h0 * HEAD_DIM:(h0 + HEADS_PER_STORE) * HEAD_DIM] = jnp.concatenate(outs, axis=1)


def _attn_prompt(q, k, v, bias, sinks, batch, seq):
    nb = seq // ATTN_BLOCK
    cur = lambda b, i: (b * nb + i, 0)
    prev = lambda b, i: (b * nb + jnp.maximum(i - 1, 0), 0)
    return pl.pallas_call(
        _attn_prompt_kernel,
        grid=(batch, nb),
        in_specs=[pl.BlockSpec(memory_space=pltpu.SMEM),
                  pl.BlockSpec((ATTN_BLOCK, Q_DIM), cur),
                  pl.BlockSpec((ATTN_BLOCK, KV_DIM), cur),
                  pl.BlockSpec((ATTN_BLOCK, KV_DIM), prev),
                  pl.BlockSpec((ATTN_BLOCK, KV_DIM), cur),
                  pl.BlockSpec((ATTN_BLOCK, KV_DIM), prev),
                  _const_spec(bias.shape)],
        out_specs=pl.BlockSpec((ATTN_BLOCK, Q_DIM), cur),
        out_shape=jax.ShapeDtypeStruct((batch * seq, Q_DIM), BF16),
        compiler_params=pltpu.CompilerParams(dimension_semantics=("arbitrary", "arbitrary"),
                                             vmem_limit_bytes=V7X_VMEM_LIMIT_BYTES),
        name="attn_prompt",
    )(sinks, q, k, k, v, v, bias)


def _attn_sample_kernel(qbd_ref, ck_ref, cv_ref, kvn_ref, bias_ref, sink_ref, mask_ref, o_ref, *, w_buf):
    bf16_round = lambda t: t.astype(BF16).astype(F32)
    seqs = range(SAMPLE_SEQ_PER_STEP)
    sink = sink_ref[...]
    s = [lax.dot_general(qbd_ref[b], ck_ref[b].astype(BF16), (((1,), (1,)), ((), ())),
                         preferred_element_type=F32) + bias_ref[:, :w_buf] for b in seqs]
    s_new = [jnp.sum(qbd_ref[b].astype(F32) * bf16_round(kvn_ref[b:b + 1, :KV_DIM]), axis=-1, keepdims=True)
             + bias_ref[:, w_buf:w_buf + 1] for b in seqs]
    m = [jnp.maximum(jnp.maximum(jnp.max(s[b], axis=-1, keepdims=True), s_new[b]), sink) for b in seqs]
    p = [jnp.exp(s[b] - m[b]) for b in seqs]
    p_new = [jnp.exp(s_new[b] - m[b]) for b in seqs]
    denom = [jnp.sum(p[b], axis=-1, keepdims=True) + p_new[b] + jnp.exp(sink - m[b]) for b in seqs]
    of = [jnp.dot((p[b] / denom[b]).astype(BF16), cv_ref[b].astype(BF16), preferred_element_type=F32)
          + bf16_round(p_new[b] / denom[b]) * bf16_round(kvn_ref[b:b + 1, KV_DIM:]) for b in seqs]
    for b in seqs:
        ob = of[b] * mask_ref[...]
        o_ref[b] = (ob[:, 0:HEAD_DIM] + ob[:, HEAD_DIM:2 * HEAD_DIM]
                    + ob[:, 2 * HEAD_DIM:3 * HEAD_DIM] + ob[:, 3 * HEAD_DIM:]).astype(BF16)


def _attn_sample(qbd, ck, cv, kvn, bias, sink_col, head_mask):
    nseq, w_buf = ck.shape[0], ck.shape[1]
    sb = SAMPLE_SEQ_PER_STEP
    return pl.pallas_call(
        functools.partial(_attn_sample_kernel, w_buf=w_buf),
        grid=(nseq // sb,),
        in_specs=[pl.BlockSpec((sb, N_HEADS, KV_DIM), lambda i: (i, 0, 0)),
                  pl.BlockSpec((sb, w_buf, KV_DIM), lambda i: (i, 0, 0)),
                  pl.BlockSpec((sb, w_buf, KV_DIM), lambda i: (i, 0, 0)),
                  pl.BlockSpec((sb, 2 * KV_DIM), lambda i: (i, 0)),
                  _const_spec(bias.shape), _const_spec(sink_col.shape), _const_spec(head_mask.shape)],
        out_specs=pl.BlockSpec((sb, N_HEADS, HEAD_DIM), lambda i: (i, 0, 0)),
        out_shape=jax.ShapeDtypeStruct((nseq, N_HEADS, HEAD_DIM), BF16),
        compiler_params=pltpu.CompilerParams(dimension_semantics=("arbitrary",),
                                             vmem_limit_bytes=V7X_VMEM_LIMIT_BYTES),
        name="attn_sample",
    )(qbd, ck, cv, kvn, bias, sink_col, head_mask)


def _route_rows(logits, row0, valid_rows, half):
    tm = logits.shape[0]
    lane = lax.broadcasted_iota(jnp.int32, logits.shape, 1)
    lane_f = lane.astype(F32)
    no_lane = float(ROUTER_LANES)

    def top1(mask):
        best = jnp.max(jnp.where(mask, logits, -jnp.inf), axis=-1, keepdims=True)
        idx = jnp.min(jnp.where(jnp.logical_and(mask, logits == best), lane_f, no_lane), axis=-1, keepdims=True)
        return best, idx

    gmask = lane < N_EXPERT_GROUPS
    gmax, grp = top1(gmask)
    gsum = jnp.sum(jnp.where(gmask, jnp.exp(logits - gmax), 0.0), axis=-1, keepdims=True)
    p_grp = 1.0 / gsum
    lo = N_EXPERT_GROUPS + EXPERTS_PER_GROUP * grp
    emask = jnp.logical_and(lane_f >= lo, lane_f < lo + EXPERTS_PER_GROUP)
    v1, i1 = top1(emask)
    v2, i2 = top1(jnp.logical_and(emask, lane_f != i1))
    e21 = jnp.exp(v2 - v1)
    w1 = p_grp / (1.0 + e21)
    w2 = p_grp * e21 / (1.0 + e21)

    oh1 = lane_f == i1
    oh2 = lane_f == i2
    if valid_rows < tm:
        valid = lax.broadcasted_iota(jnp.int32, logits.shape, 0) < valid_rows
        oh1 = jnp.logical_and(oh1, valid)
        oh2 = jnp.logical_and(oh2, valid)
    oh = oh1.astype(F32) + oh2.astype(F32)
    token = row0 + lax.broadcasted_iota(jnp.int32, (tm, 1), 0)
    key1 = (i1.astype(jnp.int32) - N_EXPERT_GROUPS) * (1 << ASSIGN_BITS) + token
    key2 = (i2.astype(jnp.int32) - N_EXPERT_GROUPS) * (1 << ASSIGN_BITS) + token + half
    w1b = lax.bitcast_convert_type(w1, jnp.int32)
    w2b = lax.bitcast_convert_type(w2, jnp.int32)
    words = jnp.where(lane == 0, key1, jnp.where(lane == 1, key2, jnp.where(lane == 2, w1b,
                      jnp.where(lane == 3, w2b, 0))))
    return words, jnp.sum(oh, axis=0, keepdims=True)


def _store_token_tiles(ref, x):
    n = x.shape[0]
    for c in range(D_MODEL // LANES):
        ref[pl.ds(c, n, stride=TILE_ROWS), :] = x[:, c * LANES:(c + 1) * LANES]


def _load_token_tiles(ref, n):
    return jnp.concatenate([ref[pl.ds(c, n, stride=TILE_ROWS), :] for c in range(D_MODEL // LANES)], axis=1)


def _pack_bf16_pairs(x):
    hw = x.shape[1] // 2
    bits = lambda v: lax.bitcast_convert_type(v.astype(BF16).astype(F32), jnp.uint32)
    return (bits(x[:, hw:]) & jnp.uint32(0xFFFF0000)) | (bits(x[:, :hw]) >> 16)


def _unpack_bf16_pairs(w):
    lo = lax.bitcast_convert_type(w << 16, F32)
    hi = lax.bitcast_convert_type(w & jnp.uint32(0xFFFF0000), F32)
    return jnp.concatenate([lo, hi], axis=1).astype(BF16)


def _out_proj_rows(yc_ref, o_ref, sa_ref, sb_ref, x_ref, wc_ref, wa_ref, wo_ref, g2_ref, wr_ref, br_ref,
                   x2_ref, h2p_ref, route_ref, cnt_ref, *, valid_rows, half):
    y_conv = jnp.dot(yc_ref[...], wc_ref[...], preferred_element_type=F32)
    y_attn = jnp.dot(o_ref[...], wa_ref[...], preferred_element_type=F32)
    mix = (sa_ref[...].astype(F32) * y_conv + sb_ref[...].astype(F32) * y_attn).astype(BF16)
    x2 = x_ref[...] + jnp.dot(mix, wo_ref[...], preferred_element_type=F32)
    x2_ref[...] = x2
    h2 = _rms_norm_f32(x2, g2_ref[...])
    h2p_ref[...] = _pack_bf16_pairs(h2)
    logits = jnp.dot(h2.astype(BF16), wr_ref[...], preferred_element_type=F32) + br_ref[...]
    words, cnt = _route_rows(logits, pl.program_id(0) * x_ref.shape[0], valid_rows, half)
    route_ref[...] = words
    cnt_ref[...] += cnt


def _out_proj_kernel(*refs, n_first, valid_rows_second, half):
    first, second, shared = refs[0:5], refs[5:10], refs[10:]
    cnt_ref = shared[-1]
    tm = first[4].shape[0]

    @pl.when(pl.program_id(0) == 0)
    def _():
        cnt_ref[...] = jnp.zeros_like(cnt_ref)

    @pl.when(pl.program_id(0) < n_first)
    def _():
        _out_proj_rows(*first, *shared, valid_rows=tm, half=half)

    @pl.when(pl.program_id(0) >= n_first)
    def _():
        _out_proj_rows(*second, *shared, valid_rows=valid_rows_second, half=half)


def _out_proj(acts_a, acts_b, wc, wa, wo, g2, wr, br, *, tm, valid_rows_b, half):
    na = acts_a[4].shape[0] // tm
    nb = acts_b[4].shape[0] // tm
    assert nb == 1
    m_total = (na + nb) * tm
    spec_a = lambda width: pl.BlockSpec((tm, width), lambda i: (jnp.minimum(i, na - 1), 0))
    spec_b = lambda width: pl.BlockSpec((tm, width), lambda i: (jnp.maximum(i - na, 0), 0))
    widths = (D_CONV, Q_DIM, D_MODEL, D_MODEL, D_MODEL)
    in_specs = [spec_a(w) for w in widths] + [spec_b(w) for w in widths]
    in_specs += [_const_spec(wc.shape), _const_spec(wa.shape), _const_spec(wo.shape),
                 _const_spec(g2.shape), _const_spec(wr.shape), _const_spec(br.shape)]
    orow = lambda width: pl.BlockSpec((tm, width), lambda i: (i, 0))
    return pl.pallas_call(
        functools.partial(_out_proj_kernel, n_first=na, valid_rows_second=valid_rows_b, half=half),
        grid=(na + nb,),
        in_specs=in_specs,
        out_specs=[orow(D_MODEL), orow(D_MODEL // 2), orow(ROUTER_LANES),
                   pl.BlockSpec((1, ROUTER_LANES), lambda i: (0, 0))],
        out_shape=[jax.ShapeDtypeStruct((m_total, D_MODEL), F32),
                   jax.ShapeDtypeStruct((m_total, D_MODEL // 2), jnp.uint32),
                   jax.ShapeDtypeStruct((m_total, ROUTER_LANES), jnp.int32),
                   jax.ShapeDtypeStruct((1, ROUTER_LANES), F32)],
        compiler_params=pltpu.CompilerParams(dimension_semantics=("arbitrary",),
                                             vmem_limit_bytes=V7X_VMEM_LIMIT_BYTES),
        name="out_proj",
    )(*acts_a, *acts_b, wc, wa, wo, g2, wr, br)


def _block_table_kernel(counts_ref, bexp_ref, bpos_ref, bcnt_ref, bslot_ref, bnext_ref, nused_ref, first_ref,
                        *, n_blocks):
    def per_expert(e, carry):
        blk0, pos0, ordinal = carry
        cnt = counts_ref[e]
        nblk = lax.shift_right_logical(cnt + (MOE_BLOCK - 1), MOE_BLOCK_LOG2)
        first_ref[e] = jnp.where(nblk > 0, blk0, -1)

        def mark(b, c):
            off = (b - blk0) * MOE_BLOCK
            bexp_ref[b] = e
            bpos_ref[b] = pos0 + off
            bcnt_ref[b] = jnp.minimum(cnt - off, MOE_BLOCK)
            bslot_ref[b] = jnp.where(b == blk0, ordinal & 1, -1)
            bnext_ref[b] = -1
            return c
        lax.fori_loop(blk0, blk0 + nblk, mark, 0)
        return blk0 + nblk, pos0 + cnt, ordinal + jnp.where(nblk > 0, 1, 0)

    n_used, _, _ = lax.fori_loop(0, N_EXPERTS, per_expert, (0, 0, 0))
    nused_ref[0] = n_used

    def unused(b, c):
        bexp_ref[b] = N_EXPERTS - 1
        bpos_ref[b] = 0
        bcnt_ref[b] = 0
        bslot_ref[b] = -1
        bnext_ref[b] = -1
        return c
    lax.fori_loop(n_used, n_blocks, unused, 0)

    def link(k, nxt):
        e = N_EXPERTS - 1 - k
        fb = first_ref[e]

        @pl.when(fb >= 0)
        def _():
            bnext_ref[fb] = nxt
        return jnp.where(fb >= 0, e, nxt)
    lax.fori_loop(0, N_EXPERTS, link, -1)


def _block_tables(counts, n_blocks):
    smem = pl.BlockSpec(memory_space=pltpu.SMEM)
    blk = jax.ShapeDtypeStruct((n_blocks,), jnp.int32)
    return pl.pallas_call(
        functools.partial(_block_table_kernel, n_blocks=n_blocks),
        in_specs=[smem],
        out_specs=[smem] * 6,
        out_shape=[blk] * 5 + [jax.ShapeDtypeStruct((1,), jnp.int32)],
        scratch_shapes=[pltpu.SMEM((N_EXPERTS,), jnp.int32)],
        name="block_tables",
    )(counts)


def _moe_kernel(tok_ref, order_ref, bexp_ref, bpos_ref, bcnt_ref, bslot_ref, bnext_ref, nused_ref,
                h2p_ref, wg_hbm, wu_hbm, wd_hbm, contrib_hbm,
                xs_ref, ys_ref, wgf_ref, wuf_ref, wdf_ref, wgb_ref, wub_ref, wdb_ref, ssem_ref, wsem_ref,
                *, m_total, half, n_blocks):
    i = pl.program_id(0)
    slot = i % 2
    n_used = nused_ref[0]
    active = i < n_used
    last_active = i == n_used - 1
    tile = lambda t: pl.ds(pl.multiple_of(t * TILE_ROWS, TILE_ROWS), TILE_ROWS)

    def gather(pos0, slt):
        for r in range(MOE_BLOCK):
            xs_ref[slt, pl.ds(r, 1), :] = h2p_ref[pl.ds(tok_ref[pos0 + r], 1), :]

    def scatter(pos0, cnt, trash0, slt, r):
        dst = jnp.where(r < cnt, order_ref[pos0 + r], trash0 + r)
        return pltpu.make_async_copy(ys_ref.at[slt, tile(r), :], contrib_hbm.at[tile(dst), :], ssem_ref.at[slt])

    def scatter_wait(slt, r):
        pltpu.make_async_copy(ys_ref.at[slt, tile(r), :], contrib_hbm.at[tile(0), :], ssem_ref.at[slt]).wait()

    @pl.when(i == 0)
    def _():
        ys_ref[...] = jnp.zeros_like(ys_ref)
        gap = half - m_total
        fills = [(2 * half + s * MOE_BLOCK, MOE_BLOCK) for s in range(2)]
        fills += [(k * half + m_total, gap) for k in range(TOP_K)] if gap else []
        for start, n in fills:
            fill = pltpu.make_async_copy(ys_ref.at[0, pl.ds(0, n * TILE_ROWS), :],
                                         contrib_hbm.at[pl.ds(start * TILE_ROWS, n * TILE_ROWS), :], ssem_ref.at[0])
            fill.start()
            fill.wait()

    def weight_copies(e, s):
        return [pltpu.make_async_copy(src.at[e], dst.at[s], wsem_ref.at[s])
                for src, dst in ((wg_hbm, wgf_ref), (wu_hbm, wuf_ref), (wd_hbm, wdf_ref))]

    @pl.when(i == 0)
    def _():
        for c in weight_copies(bexp_ref[0], 0):
            c.start()
        gather(bpos_ref[0], 0)

    @pl.when(jnp.logical_and(active, i >= 2))
    def _():
        for r in range(MOE_BLOCK):
            scatter_wait(slot, r)

    wslot = bslot_ref[i]

    @pl.when(wslot >= 0)
    def _():
        for c in weight_copies(0, wslot):
            c.wait()

    @pl.when(jnp.logical_and(wslot >= 0, bnext_ref[i] >= 0))
    def _():
        for c in weight_copies(bnext_ref[i], 1 - wslot):
            c.start()

    @pl.when(wslot >= 0)
    def _():
        wgb_ref[...] = wgf_ref[wslot].astype(BF16)
        wub_ref[...] = wuf_ref[wslot].astype(BF16)
        wdb_ref[...] = wdf_ref[wslot].astype(BF16)

    def run_block(slt):
        pos_next = bpos_ref[jnp.minimum(i + 1, n_blocks - 1)]
        pos0 = bpos_ref[i]
        cnt = bcnt_ref[i]
        trash0 = 2 * half + slt * MOE_BLOCK
        gather(pos_next, 1 - slt)
        xb = _unpack_bf16_pairs(xs_ref[slt])
        gate = jnp.dot(xb, wgb_ref[...], preferred_element_type=F32)
        up = jnp.dot(xb, wub_ref[...], preferred_element_type=F32)
        hmid = (jax.nn.silu(gate) * up).astype(BF16)
        _store_token_tiles(ys_ref.at[slt], jnp.dot(hmid, wdb_ref[...], preferred_element_type=F32))
        for r in range(MOE_BLOCK):
            scatter(pos0, cnt, trash0, slt, r).start(priority=r % 2)

    for slt in range(2):
        pl.when(jnp.logical_and(active, slot == slt))(functools.partial(run_block, slt))

    @pl.when(last_active)
    def _():
        for r in range(MOE_BLOCK):
            scatter_wait(slot, r)

    @pl.when(jnp.logical_and(last_active, i >= 1))
    def _():
        for r in range(MOE_BLOCK):
            scatter_wait(1 - slot, r)


def _moe_experts(tok, order, tables, h2p, w_gate, w_up, w_down, *, m_total, half):
    n_blocks = tables[0].shape[0]
    assert 0 <= half - m_total <= MOE_BLOCK
    hbm = pl.BlockSpec(memory_space=pl.ANY)
    grid_spec = pltpu.PrefetchScalarGridSpec(
        num_scalar_prefetch=2 + len(tables),
        grid=(n_blocks,),
        in_specs=[_const_spec(h2p.shape), hbm, hbm, hbm],
        out_specs=hbm,
        scratch_shapes=[pltpu.VMEM((2, MOE_BLOCK, D_MODEL // 2), jnp.uint32),
                        pltpu.VMEM((2, MOE_BLOCK * TILE_ROWS, LANES), F32),
                        pltpu.VMEM((2, D_MODEL, D_EXPERT), F32),
                        pltpu.VMEM((2, D_MODEL, D_EXPERT), F32),
                        pltpu.VMEM((2, D_EXPERT, D_MODEL), F32),
                        pltpu.VMEM((D_MODEL, D_EXPERT), BF16),
                        pltpu.VMEM((D_MODEL, D_EXPERT), BF16),
                        pltpu.VMEM((D_EXPERT, D_MODEL), BF16),
                        pltpu.SemaphoreType.DMA((2,)),
                        pltpu.SemaphoreType.DMA((2,))],
    )
    return pl.pallas_call(
        functools.partial(_moe_kernel, m_total=m_total, half=half, n_blocks=n_blocks),
        grid_spec=grid_spec,
        out_shape=jax.ShapeDtypeStruct(((2 * half + 2 * MOE_BLOCK) * TILE_ROWS, LANES), F32),
        compiler_params=pltpu.CompilerParams(dimension_semantics=("arbitrary",),
                                             vmem_limit_bytes=MOE_VMEM_LIMIT_BYTES),
        name="moe_experts",
    )(tok, order, *tables, h2p, w_gate, w_up, w_down)


def _combine_kernel(c0_ref, c1_ref, x2_ref, route_ref, gf_ref, y_ref):
    tc = x2_ref.shape[0]
    w = lax.bitcast_convert_type(route_ref[:, TOP_K:2 * TOP_K], F32)
    moe = w[:, 0:1] * _load_token_tiles(c0_ref, tc) + w[:, 1:2] * _load_token_tiles(c1_ref, tc)
    y_ref[...] = _rms_norm_f32(x2_ref[...] + moe, gf_ref[...])


def _moe_combine(contrib, x2, route, gf, *, row_off, m, tc, half):
    off = row_off // tc
    assert row_off % tc == 0 and half % tc == 0
    ctile = lambda k: pl.BlockSpec((tc * TILE_ROWS, LANES), lambda i: (i + off + k * (half // tc), 0))
    return pl.pallas_call(
        _combine_kernel,
        grid=(m // tc,),
        in_specs=[ctile(0), ctile(1),
                  pl.BlockSpec((tc, D_MODEL), lambda i: (i + off, 0)),
                  pl.BlockSpec((tc, ROUTER_LANES), lambda i: (i + off, 0)),
                  _const_spec((1, D_MODEL))],
        out_specs=pl.BlockSpec((tc, D_MODEL), lambda i: (i, 0)),
        out_shape=jax.ShapeDtypeStruct((m, D_MODEL), F32),
        compiler_params=pltpu.CompilerParams(dimension_semantics=("arbitrary",),
                                             vmem_limit_bytes=V7X_VMEM_LIMIT_BYTES),
        name="moe_combine_prompt" if row_off == 0 else "moe_combine_sample",
    )(contrib, contrib, x2, route, gf)


def _t5_bucket(dist):
    n = jnp.maximum(dist, 0)
    max_exact = N_BUCKETS // 2
    nf = jnp.maximum(n, 1).astype(F32)
    large = max_exact + (jnp.log(nf / max_exact) / math.log(MAX_DISTANCE / max_exact)
                         * (N_BUCKETS - max_exact)).astype(jnp.int32)
    large = jnp.minimum(large, N_BUCKETS - 1)
    return jnp.where(n < max_exact, n, large)


def _bucket_bias(rel_bias, dist, valid):
    buckets = _t5_bucket(dist).reshape(1, -1)
    onehot = (buckets == jnp.arange(N_BUCKETS, dtype=jnp.int32)[:, None]).astype(F32)
    bias = jnp.dot(rel_bias.astype(F32).T, onehot, precision=lax.Precision.HIGHEST)
    return jnp.where(valid.reshape(1, -1), bias, NEG_BIG).reshape((rel_bias.shape[1],) + dist.shape)


def _prompt_bias_table(rel_bias):
    qi = jnp.arange(ATTN_BLOCK, dtype=jnp.int32)[:, None]
    kj = jnp.arange(2 * ATTN_BLOCK, dtype=jnp.int32)[None, :] - ATTN_BLOCK
    dist = qi - kj
    return _bucket_bias(rel_bias, dist, (dist >= 0) & (dist <= WINDOW))


def _sample_bias_table(rel_bias, w_buf):
    dist = w_buf - jnp.arange(w_buf + 1, dtype=jnp.int32)
    return _bucket_bias(rel_bias, dist, dist <= WINDOW)


def kernel(x_prompt, x_sample, cache_conv, cache_k, cache_v, norm1_g, w_in, conv_w, w_conv_out, w_attn_out, w_o, sinks, rel_bias, norm2_g, w_router_group, b_router_group, w_router_expert, b_router_expert, w_e_gate, w_e_up, w_e_down, norm_f_g):
    assert norm1_g.shape[0] == 1, "single-layer configuration"
    batch, seq, _ = x_prompt.shape
    nseq = x_sample.shape[0]
    w_buf = cache_k.shape[2]
    mp = batch * seq
    m_total = mp + nseq
    assert seq % TM_DENSE == 0 and seq % ATTN_BLOCK == 0 and mp % COMBINE_BLOCK == 0
    assert nseq % SAMPLE_SEQ_PER_STEP == 0 and mp % nseq == 0
    assert TOP_K == 2 and MOE_BLOCK == 1 << MOE_BLOCK_LOG2 and m_total * TOP_K < 1 << ASSIGN_BITS

    g1 = norm1_g[0][None, :]
    g2 = norm2_g[0][None, :]
    gf = norm_f_g[None, :]
    wi = w_in[0].astype(BF16)
    cw = conv_w[0]
    wc = w_conv_out[0].astype(BF16)
    wa = w_attn_out[0].astype(BF16)
    wo = w_o[0].astype(BF16)
    pad_cols = ROUTER_LANES - N_EXPERT_GROUPS - N_EXPERTS
    wr = jnp.concatenate([w_router_group[0], w_router_expert[0],
                          jnp.zeros((D_MODEL, pad_cols), F32)], axis=1).astype(BF16)
    br = jnp.concatenate([b_router_group[0], b_router_expert[0], jnp.zeros((pad_cols,), F32)])[None, :]
    sink = sinks[0].astype(F32)

    xp = x_prompt.reshape(mp, D_MODEL)
    bps = seq // TM_DENSE
    yc_p, q_p, k_p, v_p, sa_p, sb_p, ut_p, kvt_p = _in_proj(
        xp, g1, wi, cw, tm=TM_DENSE, blocks_per_seq=bps, u_tail=8, kv_tail=WINDOW)
    o_p = _attn_prompt(q_p, k_p, v_p, _prompt_bias_table(rel_bias), sink, batch, seq)

    pad_rows = lambda t: jnp.pad(t, ((0, TM_DENSE - nseq), (0, 0)))
    xs = pad_rows(x_sample.reshape(nseq, D_MODEL))
    hist = (pad_rows(cache_conv[0][:, 0, :]), pad_rows(cache_conv[0][:, 1, :]))
    yc_s, q_s, _, _, sa_s, sb_s, ut_s, kvt_s = _in_proj(
        xs, g1, wi, cw, tm=TM_DENSE, blocks_per_seq=1, u_tail=TM_DENSE, kv_tail=TM_DENSE, hist=hist,
        gate_dtype=F32)
    u_s = ut_s[0, :nseq]
    kv_s = kvt_s[0, :nseq]
    head_mask = (jnp.arange(KV_DIM)[None, :] // HEAD_DIM == jnp.arange(N_HEADS)[:, None] // GROUP)
    qbd = (jnp.tile(q_s[:nseq].reshape(nseq, N_HEADS, HEAD_DIM), (1, 1, N_KV_HEADS))
           * head_mask[None].astype(BF16))
    o_s = _attn_sample(qbd, cache_k[0].reshape(nseq, w_buf, KV_DIM), cache_v[0].reshape(nseq, w_buf, KV_DIM),
                       kv_s, _sample_bias_table(rel_bias, w_buf), sink[:, None], head_mask.astype(F32))
    o_s = pad_rows(o_s.reshape(nseq, Q_DIM))

    half = -(-m_total // COMBINE_BLOCK) * COMBINE_BLOCK
    assert half % nseq == 0 and TOP_K * half < 1 << ASSIGN_BITS
    x2, h2p, route, cnt = _out_proj((yc_p, o_p, sa_p, sb_p, xp), (yc_s, o_s, sa_s, sb_s, xs),
                                    wc, wa, wo, g2, wr, br, tm=TM_DENSE, valid_rows_b=nseq, half=half)

    n_assign = m_total * TOP_K
    keys = route[:m_total, 0:TOP_K].reshape(-1)
    counts = cnt[0, N_EXPERT_GROUPS:N_EXPERT_GROUPS + N_EXPERTS].astype(jnp.int32)
    order = jnp.pad(jnp.sort(keys) & ((1 << ASSIGN_BITS) - 1), (0, MOE_BLOCK))
    tok = jnp.where(order >= half, order - half, order)
    n_blocks = -(-n_assign // MOE_BLOCK) + N_EXPERTS
    tables = _block_tables(counts, n_blocks)
    contrib = _moe_experts(tok, order, tables, h2p, w_e_gate[0], w_e_up[0], w_e_down[0],
                           m_total=m_total, half=half)
    y_p = _moe_combine(contrib, x2, route, gf, row_off=0, m=mp, tc=COMBINE_BLOCK, half=half)
    y_s = _moe_combine(contrib, x2, route, gf, row_off=mp, m=nseq, tc=nseq, half=half)

    y_prompt = y_p.reshape(batch, seq, D_MODEL)
    y_sample = y_s.reshape(nseq, 1, D_MODEL)
    conv_state_prompt = ut_p.reshape(batch, bps, 8, D_CONV)[:, -1, 8 - (CONV_WIDTH - 1):, :][None]
    kv_last = kvt_p.reshape(batch, bps, WINDOW, 2 * KV_DIM)[:, -1]
    k_win_prompt = kv_last[:, :, :KV_DIM].reshape(batch, WINDOW, N_KV_HEADS, HEAD_DIM)[None]
    v_win_prompt = kv_last[:, :, KV_DIM:].reshape(batch, WINDOW, N_KV_HEADS, HEAD_DIM)[None]
    conv_state_sample = jnp.concatenate([cache_conv[0][:, 1:, :], u_s[:, None, :]], axis=1)[None]
    k_new = kv_s[:, :KV_DIM].reshape(nseq, 1, N_KV_HEADS, HEAD_DIM)
    v_new = kv_s[:, KV_DIM:].reshape(nseq, 1, N_KV_HEADS, HEAD_DIM)
    k_win_sample = jnp.concatenate([cache_k[0], k_new], axis=1)[:, -w_buf:][None]
    v_win_sample = jnp.concatenate([cache_v[0], v_new], axis=1)[:, -w_buf:][None]
    return (y_prompt, y_sample, conv_state_prompt, k_win_prompt, v_win_prompt,
            conv_state_sample, k_win_sample, v_win_sample)
```

```python
import functools
import math

import jax
import jax.numpy as jnp
from jax import lax
from jax.experimental import pallas as pl
from jax.experimental.pallas import tpu as pltpu

D_MODEL = 1024
D_CONV = 1024
CONV_WIDTH = 3
N_HEADS = 16
N_KV_HEADS = 4
HEAD_DIM = 64
GROUP = N_HEADS // N_KV_HEADS
WINDOW = 128
Q_DIM = N_HEADS * HEAD_DIM
KV_DIM = N_KV_HEADS * HEAD_DIM
N_BUCKETS = 32
MAX_DISTANCE = 128
N_EXPERT_GROUPS = 4
EXPERTS_PER_GROUP = 8
N_EXPERTS = N_EXPERT_GROUPS * EXPERTS_PER_GROUP
TOP_K = 2
D_EXPERT = 512
EPS = 1e-6
PAST_LEN = 8192

BF16 = jnp.bfloat16
F32 = jnp.float32
NEG_BIG = -1e30

V7X_VMEM_LIMIT_BYTES = 56 * 1024 * 1024
MOE_VMEM_LIMIT_BYTES = 62 * 1024 * 1024
TILE_ROWS = 8
LANES = 128
ROUTER_LANES = 128
TM_DENSE = 512
ATTN_BLOCK = 128
MOE_BLOCK = 256
MOE_BLOCK_LOG2 = 8
ASSIGN_BITS = 16
COMBINE_BLOCK = 256
SAMPLE_SEQ_PER_STEP = 8
HEADS_PER_STORE = LANES // HEAD_DIM


def _const_spec(shape):
    nd = len(shape)
    return pl.BlockSpec(shape, lambda *_: (0,) * nd, pipeline_mode=pl.Buffered(1))


def _rms_norm_f32(xf, g):
    return xf * lax.rsqrt(jnp.mean(xf * xf, axis=-1, keepdims=True) + EPS) * g


def _in_proj_kernel(*refs, tm, sample, blocks_per_seq, u_tail, kv_tail):
    if sample:
        (x_ref, hist0_ref, hist1_ref, g_ref, w_ref,
         cw_ref, yc_ref, q_ref, k_ref, v_ref, sa_ref, sb_ref, ut_ref, kvt_ref) = refs
    else:
        (x_ref, g_ref, w_ref,
         cw_ref, yc_ref, q_ref, k_ref, v_ref, sa_ref, sb_ref, ut_ref, kvt_ref, ubuf_ref) = refs

    h = _rms_norm_f32(x_ref[...], g_ref[...]).astype(BF16)

    widths = (D_CONV, D_CONV, D_CONV, Q_DIM, 2 * KV_DIM, D_MODEL, D_MODEL)
    starts = [sum(widths[:n]) for n in range(len(widths))]
    wcb_ref, wcc_ref, wch_ref, wq_ref, wkv_ref, wga_ref, wgb_ref = [
        w_ref.at[:, pl.ds(a, n)] for a, n in zip(starts, widths)]

    def proj(part_ref):
        return jnp.dot(h, part_ref[...], preferred_element_type=F32)

    u = proj(wcc_ref) * proj(wch_ref)
    w0 = cw_ref[0:1, :]
    w1 = cw_ref[1:2, :]
    w2 = cw_ref[2:3, :]
    if sample:
        conv = w0 * hist0_ref[...] + w1 * hist1_ref[...] + w2 * u
    else:
        @pl.when(pl.program_id(0) % blocks_per_seq == 0)
        def _():
            ubuf_ref[0:8, :] = jnp.zeros((8, D_CONV), F32)

        ubuf_ref[8:8 + tm, :] = u
        conv = w0 * ubuf_ref[6:6 + tm, :] + w1 * ubuf_ref[7:7 + tm, :] + w2 * u
        ubuf_ref[0:8, :] = u[tm - 8:, :]
    yc_ref[...] = (proj(wcb_ref) * conv).astype(BF16)
    ut_ref[0] = u[tm - u_tail:, :]

    q_ref[...] = (proj(wq_ref) * (HEAD_DIM ** -0.5)).astype(BF16)
    kv = proj(wkv_ref)
    k_ref[...] = kv[:, :KV_DIM].astype(BF16)
    v_ref[...] = kv[:, KV_DIM:].astype(BF16)
    kvt_ref[0] = kv[tm - kv_tail:, :]
    sa_ref[...] = jax.nn.sigmoid(proj(wga_ref)).astype(sa_ref.dtype)
    sb_ref[...] = jax.nn.sigmoid(proj(wgb_ref)).astype(sb_ref.dtype)


def _in_proj(x, g1, w_in, conv_w, *, tm, blocks_per_seq, u_tail, kv_tail, hist=None, gate_dtype=BF16):
    m = x.shape[0]
    nblk = m // tm
    sample = hist is not None
    row = lambda width: pl.BlockSpec((tm, width), lambda i: (i, 0))
    in_specs = [row(D_MODEL)]
    args = [x]
    if sample:
        in_specs += [row(D_CONV), row(D_CONV)]
        args += list(hist)
    in_specs += [_const_spec((1, D_MODEL)), _const_spec(w_in.shape), _const_spec(conv_w.shape)]
    args += [g1, w_in, conv_w]
    out_shape = [
        jax.ShapeDtypeStruct((m, D_CONV), BF16),
        jax.ShapeDtypeStruct((m, Q_DIM), BF16),
        jax.ShapeDtypeStruct((m, KV_DIM), BF16),
        jax.ShapeDtypeStruct((m, KV_DIM), BF16),
        jax.ShapeDtypeStruct((m, D_MODEL), gate_dtype),
        jax.ShapeDtypeStruct((m, D_MODEL), gate_dtype),
        jax.ShapeDtypeStruct((nblk, u_tail, D_CONV), F32),
        jax.ShapeDtypeStruct((nblk, kv_tail, 2 * KV_DIM), F32),
    ]
    out_specs = [row(D_CONV), row(Q_DIM), row(KV_DIM), row(KV_DIM), row(D_MODEL), row(D_MODEL),
                 pl.BlockSpec((1, u_tail, D_CONV), lambda i: (i, 0, 0)),
                 pl.BlockSpec((1, kv_tail, 2 * KV_DIM), lambda i: (i, 0, 0))]
    scratch = [] if sample else [pltpu.VMEM((tm + 8, D_CONV), F32)]
    return pl.pallas_call(
        functools.partial(_in_proj_kernel, tm=tm, sample=sample, blocks_per_seq=blocks_per_seq,
                          u_tail=u_tail, kv_tail=kv_tail),
        grid=(nblk,),
        in_specs=in_specs,
        out_specs=out_specs,
        out_shape=out_shape,
        scratch_shapes=scratch,
        compiler_params=pltpu.CompilerParams(dimension_semantics=("arbitrary",),
                                             vmem_limit_bytes=V7X_VMEM_LIMIT_BYTES),
        name="in_proj_sample" if sample else "in_proj_prompt",
    )(*args)


def _attn_prompt_kernel(sink_ref, q_ref, kc_ref, kp_ref, vc_ref, vp_ref, bias_ref, o_ref):
    first = pl.program_id(1) == 0
    col = lax.broadcasted_iota(jnp.int32, (ATTN_BLOCK, 2 * ATTN_BLOCK), 1)
    no_prev = jnp.logical_and(first, col < ATTN_BLOCK)
    for g in range(N_KV_HEADS):
        ks = slice(g * HEAD_DIM, (g + 1) * HEAD_DIM)
        kcat = jnp.concatenate([kp_ref[:, ks], kc_ref[:, ks]], axis=0)
        vcat = jnp.concatenate([vp_ref[:, ks], vc_ref[:, ks]], axis=0)
        for h0 in range(g * GROUP, (g + 1) * GROUP, HEADS_PER_STORE):
            outs = []
            for h in range(h0, h0 + HEADS_PER_STORE):
                hs = slice(h * HEAD_DIM, (h + 1) * HEAD_DIM)
                s = lax.dot_general(q_ref[:, hs], kcat, (((1,), (1,)), ((), ())),
                                    preferred_element_type=F32)
                s = jnp.where(no_prev, NEG_BIG, s + bias_ref[h])
                sink = sink_ref[h]
                m = jnp.maximum(jnp.max(s, axis=-1, keepdims=True), sink)
                p = jnp.exp(s - m)
                denom = jnp.sum(p, axis=-1, keepdims=True) + jnp.exp(sink - m)
                o = jnp.dot(p.astype(BF16), vcat, preferred_element_type=F32)
                outs.append((o / denom).astype(BF16))
            o_ref[:, h0 * HEAD_DIM:(h0 + HEADS_PER_STORE) * HEAD_DIM] = jnp.concatenate(outs, axis=1)


def _attn_prompt(q, k, v, bias, sinks, batch, seq):
    nb = seq // ATTN_BLOCK
    cur = lambda b, i: (b * nb + i, 0)
    prev = lambda b, i: (b * nb + jnp.maximum(i - 1, 0), 0)
    return pl.pallas_call(
        _attn_prompt_kernel,
        grid=(batch, nb),
        in_specs=[pl.BlockSpec(memory_space=pltpu.SMEM),
                  pl.BlockSpec((ATTN_BLOCK, Q_DIM), cur),
                  pl.BlockSpec((ATTN_BLOCK, KV_DIM), cur),
                  pl.BlockSpec((ATTN_BLOCK, KV_DIM), prev),
                  pl.BlockSpec((ATTN_BLOCK, KV_DIM), cur),
                  pl.BlockSpec((ATTN_BLOCK, KV_DIM), prev),
                  _const_spec(bias.shape)],
        out_specs=pl.BlockSpec((ATTN_BLOCK, Q_DIM), cur),
        out_shape=jax.ShapeDtypeStruct((batch * seq, Q_DIM), BF16),
        compiler_params=pltpu.CompilerParams(dimension_semantics=("arbitrary", "arbitrary"),
                                             vmem_limit_bytes=V7X_VMEM_LIMIT_BYTES),
        name="attn_prompt",
    )(sinks, q, k, k, v, v, bias)


def _attn_sample_kernel(qbd_ref, ckt_ref, cvt_ref, kvn_ref, knt_ref, vnt_ref, bias_ref, sink_ref, mask_ref,
                        o_ref, kwin_ref, vwin_ref, *, w_buf):
    bf16_round = lambda t: t.astype(BF16).astype(F32)
    seqs = range(SAMPLE_SEQ_PER_STEP)
    sink = sink_ref[...]
    newest = lax.broadcasted_iota(jnp.int32, (KV_DIM, w_buf), 1) == w_buf - 1
    for b in seqs:
        kwin_ref[b] = jnp.where(newest, knt_ref[b], pltpu.roll(ckt_ref[b], w_buf - 1, axis=1))
        vwin_ref[b] = jnp.where(newest, vnt_ref[b], pltpu.roll(cvt_ref[b], w_buf - 1, axis=1))
    s = [jnp.dot(qbd_ref[b], ckt_ref[b].astype(BF16), preferred_element_type=F32) + bias_ref[:, :w_buf]
         for b in seqs]
    s_new = [jnp.sum(qbd_ref[b].astype(F32) * bf16_round(kvn_ref[b:b + 1, :KV_DIM]), axis=-1, keepdims=True)
             + bias_ref[:, w_buf:w_buf + 1] for b in seqs]
    m = [jnp.maximum(jnp.maximum(jnp.max(s[b], axis=-1, keepdims=True), s_new[b]), sink) for b in seqs]
    p = [jnp.exp(s[b] - m[b]) for b in seqs]
    p_new = [jnp.exp(s_new[b] - m[b]) for b in seqs]
    denom = [jnp.sum(p[b], axis=-1, keepdims=True) + p_new[b] + jnp.exp(sink - m[b]) for b in seqs]
    of = [lax.dot_general((p[b] / denom[b]).astype(BF16), cvt_ref[b].astype(BF16), (((1,), (1,)), ((), ())),
                          preferred_element_type=F32)
          + bf16_round(p_new[b] / denom[b]) * bf16_round(kvn_ref[b:b + 1, KV_DIM:]) for b in seqs]
    for b in seqs:
        ob = of[b] * mask_ref[...]
        o_ref[b] = (ob[:, 0:HEAD_DIM] + ob[:, HEAD_DIM:2 * HEAD_DIM]
                    + ob[:, 2 * HEAD_DIM:3 * HEAD_DIM] + ob[:, 3 * HEAD_DIM:]).astype(BF16)


def _attn_sample(qbd, ckt, cvt, kvn, bias, sink_col, head_mask):
    nseq, w_buf = ckt.shape[0], ckt.shape[2]
    sb = SAMPLE_SEQ_PER_STEP
    seq3 = lambda d1, d2: pl.BlockSpec((sb, d1, d2), lambda i: (i, 0, 0))
    win = jax.ShapeDtypeStruct((nseq, KV_DIM, w_buf), F32)
    return pl.pallas_call(
        functools.partial(_attn_sample_kernel, w_buf=w_buf),
        grid=(nseq // sb,),
        in_specs=[seq3(N_HEADS, KV_DIM), seq3(KV_DIM, w_buf), seq3(KV_DIM, w_buf),
                  pl.BlockSpec((sb, 2 * KV_DIM), lambda i: (i, 0)), seq3(KV_DIM, 1), seq3(KV_DIM, 1),
                  _const_spec(bias.shape), _const_spec(sink_col.shape), _const_spec(head_mask.shape)],
        out_specs=[seq3(N_HEADS, HEAD_DIM), seq3(KV_DIM, w_buf), seq3(KV_DIM, w_buf)],
        out_shape=[jax.ShapeDtypeStruct((nseq, N_HEADS, HEAD_DIM), BF16), win, win],
        compiler_params=pltpu.CompilerParams(dimension_semantics=("arbitrary",),
                                             vmem_limit_bytes=V7X_VMEM_LIMIT_BYTES),
        name="attn_sample",
    )(qbd, ckt, cvt, kvn, kvn[:, :KV_DIM, None], kvn[:, KV_DIM:, None], bias, sink_col, head_mask)


def _route_rows(logits, row0, valid_rows, half):
    tm = logits.shape[0]
    lane = lax.broadcasted_iota(jnp.int32, logits.shape, 1)
    lane_f = lane.astype(F32)
    no_lane = float(ROUTER_LANES)

    def top1(mask):
        best = jnp.max(jnp.where(mask, logits, -jnp.inf), axis=-1, keepdims=True)
        idx = jnp.min(jnp.where(jnp.logical_and(mask, logits == best), lane_f, no_lane), axis=-1, keepdims=True)
        return best, idx

    gmask = lane < N_EXPERT_GROUPS
    gmax, grp = top1(gmask)
    gsum = jnp.sum(jnp.where(gmask, jnp.exp(logits - gmax), 0.0), axis=-1, keepdims=True)
    p_grp = 1.0 / gsum
    lo = N_EXPERT_GROUPS + EXPERTS_PER_GROUP * grp
    emask = jnp.logical_and(lane_f >= lo, lane_f < lo + EXPERTS_PER_GROUP)
    v1, i1 = top1(emask)
    v2, i2 = top1(jnp.logical_and(emask, lane_f != i1))
    e21 = jnp.exp(v2 - v1)
    w1 = p_grp / (1.0 + e21)
    w2 = p_grp * e21 / (1.0 + e21)

    oh1 = lane_f == i1
    oh2 = lane_f == i2
    if valid_rows < tm:
        valid = lax.broadcasted_iota(jnp.int32, logits.shape, 0) < valid_rows
        oh1 = jnp.logical_and(oh1, valid)
        oh2 = jnp.logical_and(oh2, valid)
    oh = oh1.astype(F32) + oh2.astype(F32)
    token = row0 + lax.broadcasted_iota(jnp.int32, (tm, 1), 0)
    key1 = (i1.astype(jnp.int32) - N_EXPERT_GROUPS) * (1 << ASSIGN_BITS) + token
    key2 = (i2.astype(jnp.int32) - N_EXPERT_GROUPS) * (1 << ASSIGN_BITS) + token + half
    w1b = lax.bitcast_convert_type(w1, jnp.int32)
    w2b = lax.bitcast_convert_type(w2, jnp.int32)
    words = jnp.where(lane == 0, key1, jnp.where(lane == 1, key2, jnp.where(lane == 2, w1b,
                      jnp.where(lane == 3, w2b, 0))))
    return words, jnp.sum(oh, axis=0, keepdims=True)


def _store_token_tiles(ref, x):
    n = x.shape[0]
    for c in range(D_MODEL // LANES):
        ref[pl.ds(c, n, stride=TILE_ROWS), :] = x[:, c * LANES:(c + 1) * LANES]


def _load_token_tiles(ref, n):
    return jnp.concatenate([ref[pl.ds(c, n, stride=TILE_ROWS), :] for c in range(D_MODEL // LANES)], axis=1)


def _pack_bf16_pairs(x):
    hw = x.shape[1] // 2
    bits = lambda v: lax.bitcast_convert_type(v.astype(BF16).astype(F32), jnp.uint32)
    return (bits(x[:, hw:]) & jnp.uint32(0xFFFF0000)) | (bits(x[:, :hw]) >> 16)


def _unpack_bf16_pairs(w):
    lo = lax.bitcast_convert_type(w << 16, F32)
    hi = lax.bitcast_convert_type(w & jnp.uint32(0xFFFF0000), F32)
    return jnp.concatenate([lo, hi], axis=1).astype(BF16)


def _out_proj_rows(yc_ref, o_ref, sa_ref, sb_ref, x_ref, wc_ref, wa_ref, wo_ref, g2_ref, wr_ref, br_ref,
                   x2_ref, h2p_ref, route_ref, cnt_ref, *, valid_rows, half):
    y_conv = jnp.dot(yc_ref[...], wc_ref[...], preferred_element_type=F32)
    y_attn = jnp.dot(o_ref[...], wa_ref[...], preferred_element_type=F32)
    mix = (sa_ref[...].astype(F32) * y_conv + sb_ref[...].astype(F32) * y_attn).astype(BF16)
    x2 = x_ref[...] + jnp.dot(mix, wo_ref[...], preferred_element_type=F32)
    x2_ref[...] = x2
    h2 = _rms_norm_f32(x2, g2_ref[...])
    h2p_ref[...] = _pack_bf16_pairs(h2)
    logits = jnp.dot(h2.astype(BF16), wr_ref[...], preferred_element_type=F32) + br_ref[...]
    words, cnt = _route_rows(logits, pl.program_id(0) * x_ref.shape[0], valid_rows, half)
    route_ref[...] = words
    cnt_ref[...] += cnt


def _out_proj_kernel(*refs, n_first, valid_rows_second, half):
    first, second, shared = refs[0:5], refs[5:10], refs[10:]
    cnt_ref = shared[-1]
    tm = first[4].shape[0]

    @pl.when(pl.program_id(0) == 0)
    def _():
        cnt_ref[...] = jnp.zeros_like(cnt_ref)

    @pl.when(pl.program_id(0) < n_first)
    def _():
        _out_proj_rows(*first, *shared, valid_rows=tm, half=half)

    @pl.when(pl.program_id(0) >= n_first)
    def _():
        _out_proj_rows(*second, *shared, valid_rows=valid_rows_second, half=half)


def _out_proj(acts_a, acts_b, wc, wa, wo, g2, wr, br, *, tm, valid_rows_b, half):
    na = acts_a[4].shape[0] // tm
    nb = acts_b[4].shape[0] // tm
    assert nb == 1
    m_total = (na + nb) * tm
    spec_a = lambda width: pl.BlockSpec((tm, width), lambda i: (jnp.minimum(i, na - 1), 0))
    spec_b = lambda width: pl.BlockSpec((tm, width), lambda i: (jnp.maximum(i - na, 0), 0))
    widths = (D_CONV, Q_DIM, D_MODEL, D_MODEL, D_MODEL)
    in_specs = [spec_a(w) for w in widths] + [spec_b(w) for w in widths]
    in_specs += [_const_spec(wc.shape), _const_spec(wa.shape), _const_spec(wo.shape),
                 _const_spec(g2.shape), _const_spec(wr.shape), _const_spec(br.shape)]
    orow = lambda width: pl.BlockSpec((tm, width), lambda i: (i, 0))
    return pl.pallas_call(
        functools.partial(_out_proj_kernel, n_first=na, valid_rows_second=valid_rows_b, half=half),
        grid=(na + nb,),
        in_specs=in_specs,
        out_specs=[orow(D_MODEL), orow(D_MODEL // 2), orow(ROUTER_LANES),
                   pl.BlockSpec((1, ROUTER_LANES), lambda i: (0, 0))],
        out_shape=[jax.ShapeDtypeStruct((m_total, D_MODEL), F32),
                   jax.ShapeDtypeStruct((m_total, D_MODEL // 2), jnp.uint32),
                   jax.ShapeDtypeStruct((m_total, ROUTER_LANES), jnp.int32),
                   jax.ShapeDtypeStruct((1, ROUTER_LANES), F32)],
        compiler_params=pltpu.CompilerParams(dimension_semantics=("arbitrary",),
                                             vmem_limit_bytes=V7X_VMEM_LIMIT_BYTES),
        name="out_proj",
    )(*acts_a, *acts_b, wc, wa, wo, g2, wr, br)


def _block_table_kernel(counts_ref, bexp_ref, bpos_ref, bcnt_ref, bslot_ref, bnext_ref, nused_ref, first_ref,
                        *, n_blocks):
    def per_expert(e, carry):
        blk0, pos0, ordinal = carry
        cnt = counts_ref[e]
        nblk = lax.shift_right_logical(cnt + (MOE_BLOCK - 1), MOE_BLOCK_LOG2)
        first_ref[e] = jnp.where(nblk > 0, blk0, -1)

        def mark(b, c):
            off = (b - blk0) * MOE_BLOCK
            bexp_ref[b] = e
            bpos_ref[b] = pos0 + off
            bcnt_ref[b] = jnp.minimum(cnt - off, MOE_BLOCK)
            bslot_ref[b] = jnp.where(b == blk0, ordinal & 1, -1)
            bnext_ref[b] = -1
            return c
        lax.fori_loop(blk0, blk0 + nblk, mark, 0)
        return blk0 + nblk, pos0 + cnt, ordinal + jnp.where(nblk > 0, 1, 0)

    n_used, _, _ = lax.fori_loop(0, N_EXPERTS, per_expert, (0, 0, 0))
    nused_ref[0] = n_used

    def unused(b, c):
        bexp_ref[b] = N_EXPERTS - 1
        bpos_ref[b] = 0
        bcnt_ref[b] = 0
        bslot_ref[b] = -1
        bnext_ref[b] = -1
        return c
    lax.fori_loop(n_used, n_blocks, unused, 0)

    def link(k, nxt):
        e = N_EXPERTS - 1 - k
        fb = first_ref[e]

        @pl.when(fb >= 0)
        def _():
            bnext_ref[fb] = nxt
        return jnp.where(fb >= 0, e, nxt)
    lax.fori_loop(0, N_EXPERTS, link, -1)


def _block_tables(counts, n_blocks):
    smem = pl.BlockSpec(memory_space=pltpu.SMEM)
    blk = jax.ShapeDtypeStruct((n_blocks,), jnp.int32)
    return pl.pallas_call(
        functools.partial(_block_table_kernel, n_blocks=n_blocks),
        in_specs=[smem],
        out_specs=[smem] * 6,
        out_shape=[blk] * 5 + [jax.ShapeDtypeStruct((1,), jnp.int32)],
        scratch_shapes=[pltpu.SMEM((N_EXPERTS,), jnp.int32)],
        name="block_tables",
    )(counts)


def _moe_kernel(tok_ref, order_ref, bexp_ref, bpos_ref, bcnt_ref, bslot_ref, bnext_ref, nused_ref,
                h2p_ref, wg_hbm, wu_hbm, wd_hbm, contrib_hbm,
                xs_ref, ys_ref, wgf_ref, wuf_ref, wdf_ref, wgb_ref, wub_ref, wdb_ref, ssem_ref, wsem_ref,
                *, m_total, half, n_blocks):
    i = pl.program_id(0)
    slot = i % 2
    n_used = nused_ref[0]
    active = i < n_used
    last_active = i == n_used - 1
    tile = lambda t: pl.ds(pl.multiple_of(t * TILE_ROWS, TILE_ROWS), TILE_ROWS)

    def gather(pos0, slt):
        for r in range(MOE_BLOCK):
            xs_ref[slt, pl.ds(r, 1), :] = h2p_ref[pl.ds(tok_ref[pos0 + r], 1), :]

    def scatter(pos0, cnt, trash0, slt, r):
        dst = jnp.where(r < cnt, order_ref[pos0 + r], trash0 + r)
        return pltpu.make_async_copy(ys_ref.at[slt, tile(r), :], contrib_hbm.at[tile(dst), :], ssem_ref.at[slt])

    def scatter_wait(slt, r):
        pltpu.make_async_copy(ys_ref.at[slt, tile(r), :], contrib_hbm.at[tile(0), :], ssem_ref.at[slt]).wait()

    @pl.when(i == 0)
    def _():
        ys_ref[...] = jnp.zeros_like(ys_ref)
        gap = half - m_total
        fills = [(2 * half + s * MOE_BLOCK, MOE_BLOCK) for s in range(2)]
        fills += [(k * half + m_total, gap) for k in range(TOP_K)] if gap else []
        for start, n in fills:
            fill = pltpu.make_async_copy(ys_ref.at[0, pl.ds(0, n * TILE_ROWS), :],
                                         contrib_hbm.at[pl.ds(start * TILE_ROWS, n * TILE_ROWS), :], ssem_ref.at[0])
            fill.start()
            fill.wait()

    def weight_copies(e, s):
        return [pltpu.make_async_copy(src.at[e], dst.at[s], wsem_ref.at[s])
                for src, dst in ((wg_hbm, wgf_ref), (wu_hbm, wuf_ref), (wd_hbm, wdf_ref))]

    @pl.when(i == 0)
    def _():
        for c in weight_copies(bexp_ref[0], 0):
            c.start()
        gather(bpos_ref[0], 0)

    @pl.when(jnp.logical_and(active, i >= 2))
    def _():
        for r in range(MOE_BLOCK):
            scatter_wait(slot, r)

    wslot = bslot_ref[i]

    @pl.when(wslot >= 0)
    def _():
        for c in weight_copies(0, wslot):
            c.wait()

    @pl.when(jnp.logical_and(wslot >= 0, bnext_ref[i] >= 0))
    def _():
        for c in weight_copies(bnext_ref[i], 1 - wslot):
            c.start()

    @pl.when(wslot >= 0)
    def _():
        wgb_ref[...] = wgf_ref[wslot].astype(BF16)
        wub_ref[...] = wuf_ref[wslot].astype(BF16)
        wdb_ref[...] = wdf_ref[wslot].astype(BF16)

    def run_block(slt):
        pos_next = bpos_ref[jnp.minimum(i + 1, n_blocks - 1)]
        pos0 = bpos_ref[i]
        cnt = bcnt_ref[i]
        trash0 = 2 * half + slt * MOE_BLOCK
        gather(pos_next, 1 - slt)
        xb = _unpack_bf16_pairs(xs_ref[slt])
        gate = jnp.dot(xb, wgb_ref[...], preferred_element_type=F32)
        up = jnp.dot(xb, wub_ref[...], preferred_element_type=F32)
        hmid = (jax.nn.silu(gate) * up).astype(BF16)
        _store_token_tiles(ys_ref.at[slt], jnp.dot(hmid, wdb_ref[...], preferred_element_type=F32))
        for r in range(MOE_BLOCK):
            scatter(pos0, cnt, trash0, slt, r).start(priority=r % 2)

    for slt in range(2):
        pl.when(jnp.logical_and(active, slot == slt))(functools.partial(run_block, slt))

    @pl.when(last_active)
    def _():
        for r in range(MOE_BLOCK):
            scatter_wait(slot, r)

    @pl.when(jnp.logical_and(last_active, i >= 1))
    def _():
        for r in range(MOE_BLOCK):
            scatter_wait(1 - slot, r)


def _moe_experts(tok, order, tables, h2p, w_gate, w_up, w_down, *, m_total, half):
    n_blocks = tables[0].shape[0]
    assert 0 <= half - m_total <= MOE_BLOCK
    hbm = pl.BlockSpec(memory_space=pl.ANY)
    grid_spec = pltpu.PrefetchScalarGridSpec(
        num_scalar_prefetch=2 + len(tables),
        grid=(n_blocks,),
        in_specs=[_const_spec(h2p.shape), hbm, hbm, hbm],
        out_specs=hbm,
        scratch_shapes=[pltpu.VMEM((2, MOE_BLOCK, D_MODEL // 2), jnp.uint32),
                        pltpu.VMEM((2, MOE_BLOCK * TILE_ROWS, LANES), F32),
                        pltpu.VMEM((2, D_MODEL, D_EXPERT), F32),
                        pltpu.VMEM((2, D_MODEL, D_EXPERT), F32),
                        pltpu.VMEM((2, D_EXPERT, D_MODEL), F32),
                        pltpu.VMEM((D_MODEL, D_EXPERT), BF16),
                        pltpu.VMEM((D_MODEL, D_EXPERT), BF16),
                        pltpu.VMEM((D_EXPERT, D_MODEL), BF16),
                        pltpu.SemaphoreType.DMA((2,)),
                        pltpu.SemaphoreType.DMA((2,))],
    )
    return pl.pallas_call(
        functools.partial(_moe_kernel, m_total=m_total, half=half, n_blocks=n_blocks),
        grid_spec=grid_spec,
        out_shape=jax.ShapeDtypeStruct(((2 * half + 2 * MOE_BLOCK) * TILE_ROWS, LANES), F32),
        compiler_params=pltpu.CompilerParams(dimension_semantics=("arbitrary",),
                                             vmem_limit_bytes=MOE_VMEM_LIMIT_BYTES),
        name="moe_experts",
    )(tok, order, *tables, h2p, w_gate, w_up, w_down)


def _combine_kernel(c0_ref, c1_ref, x2_ref, route_ref, gf_ref, y_ref):
    tc = x2_ref.shape[0]
    w = lax.bitcast_convert_type(route_ref[:, TOP_K:2 * TOP_K], F32)
    moe = w[:, 0:1] * _load_token_tiles(c0_ref, tc) + w[:, 1:2] * _load_token_tiles(c1_ref, tc)
    y_ref[...] = _rms_norm_f32(x2_ref[...] + moe, gf_ref[...])


def _moe_combine(contrib, x2, route, gf, *, row_off, m, tc, half):
    off = row_off // tc
    assert row_off % tc == 0 and half % tc == 0
    ctile = lambda k: pl.BlockSpec((tc * TILE_ROWS, LANES), lambda i: (i + off + k * (half // tc), 0))
    return pl.pallas_call(
        _combine_kernel,
        grid=(m // tc,),
        in_specs=[ctile(0), ctile(1),
                  pl.BlockSpec((tc, D_MODEL), lambda i: (i + off, 0)),
                  pl.BlockSpec((tc, ROUTER_LANES), lambda i: (i + off, 0)),
                  _const_spec((1, D_MODEL))],
        out_specs=pl.BlockSpec((tc, D_MODEL), lambda i: (i, 0)),
        out_shape=jax.ShapeDtypeStruct((m, D_MODEL), F32),
        compiler_params=pltpu.CompilerParams(dimension_semantics=("arbitrary",),
                                             vmem_limit_bytes=V7X_VMEM_LIMIT_BYTES),
        name="moe_combine_prompt" if row_off == 0 else "moe_combine_sample",
    )(contrib, contrib, x2, route, gf)


def _t5_bucket(dist):
    n = jnp.maximum(dist, 0)
    max_exact = N_BUCKETS // 2
    nf = jnp.maximum(n, 1).astype(F32)
    large = max_exact + (jnp.log(nf / max_exact) / math.log(MAX_DISTANCE / max_exact)
                         * (N_BUCKETS - max_exact)).astype(jnp.int32)
    large = jnp.minimum(large, N_BUCKETS - 1)
    return jnp.where(n < max_exact, n, large)


def _bucket_bias(rel_bias, dist, valid):
    buckets = _t5_bucket(dist).reshape(1, -1)
    onehot = (buckets == jnp.arange(N_BUCKETS, dtype=jnp.int32)[:, None]).astype(F32)
    bias = jnp.dot(rel_bias.astype(F32).T, onehot, precision=lax.Precision.HIGHEST)
    return jnp.where(valid.reshape(1, -1), bias, NEG_BIG).reshape((rel_bias.shape[1],) + dist.shape)


def _prompt_bias_table(rel_bias):
    qi = jnp.arange(ATTN_BLOCK, dtype=jnp.int32)[:, None]
    kj = jnp.arange(2 * ATTN_BLOCK, dtype=jnp.int32)[None, :] - ATTN_BLOCK
    dist = qi - kj
    return _bucket_bias(rel_bias, dist, (dist >= 0) & (dist <= WINDOW))


def _sample_bias_table(rel_bias, w_buf):
    dist = w_buf - jnp.arange(w_buf + 1, dtype=jnp.int32)
    return _bucket_bias(rel_bias, dist, dist <= WINDOW)


def kernel(x_prompt, x_sample, cache_conv, cache_k, cache_v, norm1_g, w_in, conv_w, w_conv_out, w_attn_out, w_o, sinks, rel_bias, norm2_g, w_router_group, b_router_group, w_router_expert, b_router_expert, w_e_gate, w_e_up, w_e_down, norm_f_g):
    assert norm1_g.shape[0] == 1, "single-layer configuration"
    batch, seq, _ = x_prompt.shape
    nseq = x_sample.shape[0]
    w_buf = cache_k.shape[2]
    mp = batch * seq
    m_total = mp + nseq
    assert seq % TM_DENSE == 0 and seq % ATTN_BLOCK == 0 and mp % COMBINE_BLOCK == 0
    assert nseq % SAMPLE_SEQ_PER_STEP == 0 and mp % nseq == 0
    assert TOP_K == 2 and MOE_BLOCK == 1 << MOE_BLOCK_LOG2 and m_total * TOP_K < 1 << ASSIGN_BITS

    g1 = norm1_g[0][None, :]
    g2 = norm2_g[0][None, :]
    gf = norm_f_g[None, :]
    wi = w_in[0].astype(BF16)
    cw = conv_w[0]
    wc = w_conv_out[0].astype(BF16)
    wa = w_attn_out[0].astype(BF16)
    wo = w_o[0].astype(BF16)
    pad_cols = ROUTER_LANES - N_EXPERT_GROUPS - N_EXPERTS
    wr = jnp.concatenate([w_router_group[0], w_router_expert[0],
                          jnp.zeros((D_MODEL, pad_cols), F32)], axis=1).astype(BF16)
    br = jnp.concatenate([b_router_group[0], b_router_expert[0], jnp.zeros((pad_cols,), F32)])[None, :]
    sink = sinks[0].astype(F32)

    xp = x_prompt.reshape(mp, D_MODEL)
    bps = seq // TM_DENSE
    yc_p, q_p, k_p, v_p, sa_p, sb_p, ut_p, kvt_p = _in_proj(
        xp, g1, wi, cw, tm=TM_DENSE, blocks_per_seq=bps, u_tail=8, kv_tail=WINDOW)
    o_p = _attn_prompt(q_p, k_p, v_p, _prompt_bias_table(rel_bias), sink, batch, seq)

    pad_rows = lambda t: jnp.pad(t, ((0, TM_DENSE - nseq), (0, 0)))
    xs = pad_rows(x_sample.reshape(nseq, D_MODEL))
    hist = (pad_rows(cache_conv[0][:, 0, :]), pad_rows(cache_conv[0][:, 1, :]))
    yc_s, q_s, _, _, sa_s, sb_s, ut_s, kvt_s = _in_proj(
        xs, g1, wi, cw, tm=TM_DENSE, blocks_per_seq=1, u_tail=TM_DENSE, kv_tail=TM_DENSE, hist=hist,
        gate_dtype=F32)
    u_s = ut_s[0, :nseq]
    kv_s = kvt_s[0, :nseq]
    head_mask = (jnp.arange(KV_DIM)[None, :] // HEAD_DIM == jnp.arange(N_HEADS)[:, None] // GROUP)
    qbd = (jnp.tile(q_s[:nseq].reshape(nseq, N_HEADS, HEAD_DIM), (1, 1, N_KV_HEADS))
           * head_mask[None].astype(BF16))
    to_keys_minor = lambda c: jnp.transpose(c.reshape(nseq, w_buf, KV_DIM), (0, 2, 1))
    from_keys_minor = lambda c: jnp.transpose(c, (0, 2, 1)).reshape(1, nseq, w_buf, N_KV_HEADS, HEAD_DIM)
    o_s, kwin_s, vwin_s = _attn_sample(qbd, to_keys_minor(cache_k[0]), to_keys_minor(cache_v[0]), kv_s,
                                       _sample_bias_table(rel_bias, w_buf), sink[:, None],
                                       head_mask.astype(F32))
    o_s = pad_rows(o_s.reshape(nseq, Q_DIM))

    half = -(-m_total // COMBINE_BLOCK) * COMBINE_BLOCK
    assert half % nseq == 0 and TOP_K * half < 1 << ASSIGN_BITS
    x2, h2p, route, cnt = _out_proj((yc_p, o_p, sa_p, sb_p, xp), (yc_s, o_s, sa_s, sb_s, xs),
                                    wc, wa, wo, g2, wr, br, tm=TM_DENSE, valid_rows_b=nseq, half=half)

    n_assign = m_total * TOP_K
    keys = route[:m_total, 0:TOP_K].reshape(-1)
    counts = cnt[0, N_EXPERT_GROUPS:N_EXPERT_GROUPS + N_EXPERTS].astype(jnp.int32)
    order = jnp.pad(jnp.sort(keys) & ((1 << ASSIGN_BITS) - 1), (0, MOE_BLOCK))
    tok = jnp.where(order >= half, order - half, order)
    n_blocks = -(-n_assign // MOE_BLOCK) + N_EXPERTS
    tables = _block_tables(counts, n_blocks)
    contrib = _moe_experts(tok, order, tables, h2p, w_e_gate[0], w_e_up[0], w_e_down[0],
                           m_total=m_total, half=half)
    y_p = _moe_combine(contrib, x2, route, gf, row_off=0, m=mp, tc=COMBINE_BLOCK, half=half)
    y_s = _moe_combine(contrib, x2, route, gf, row_off=mp, m=nseq, tc=nseq, half=half)

    y_prompt = y_p.reshape(batch, seq, D_MODEL)
    y_sample = y_s.reshape(nseq, 1, D_MODEL)
    conv_state_prompt = ut_p.reshape(batch, bps, 8, D_CONV)[:, -1, 8 - (CONV_WIDTH - 1):, :][None]
    kv_last = kvt_p.reshape(batch, bps, WINDOW, 2 * KV_DIM)[:, -1]
    k_win_prompt = kv_last[:, :, :KV_DIM].reshape(batch, WINDOW, N_KV_HEADS, HEAD_DIM)[None]
    v_win_prompt = kv_last[:, :, KV_DIM:].reshape(batch, WINDOW, N_KV_HEADS, HEAD_DIM)[None]
    conv_state_sample = jnp.concatenate([cache_conv[0][:, 1:, :], u_s[:, None, :]], axis=1)[None]
    k_win_sample = from_keys_minor(kwin_s)
    v_win_sample = from_keys_minor(vwin_s)
    return (y_prompt, y_sample, conv_state_prompt, k_win_prompt, v_win_prompt,
            conv_state_sample, k_win_sample, v_win_sample)
```

```python
import functools
import math

import jax
import jax.numpy as jnp
from jax import lax
from jax.experimental import pallas as pl
from jax.experimental.pallas import tpu as pltpu

D_MODEL = 1024
D_CONV = 1024
CONV_WIDTH = 3
N_HEADS = 16
N_KV_HEADS = 4
HEAD_DIM = 64
GROUP = N_HEADS // N_KV_HEADS
WINDOW = 128
Q_DIM = N_HEADS * HEAD_DIM
KV_DIM = N_KV_HEADS * HEAD_DIM
N_BUCKETS = 32
MAX_DISTANCE = 128
N_EXPERT_GROUPS = 4
EXPERTS_PER_GROUP = 8
N_EXPERTS = N_EXPERT_GROUPS * EXPERTS_PER_GROUP
TOP_K = 2
D_EXPERT = 512
EPS = 1e-6
PAST_LEN = 8192

BF16 = jnp.bfloat16
F32 = jnp.float32
NEG_BIG = -1e30

V7X_VMEM_LIMIT_BYTES = 56 * 1024 * 1024
MOE_VMEM_LIMIT_BYTES = 62 * 1024 * 1024
TILE_ROWS = 8
LANES = 128
ROUTER_LANES = 128
TM_DENSE = 512
ATTN_BLOCK = 128
MOE_BLOCK = 256
MOE_BLOCK_LOG2 = 8
ASSIGN_BITS = 16
COMBINE_BLOCK = 256
SAMPLE_SEQ_PER_STEP = 8
HEADS_PER_STORE = LANES // HEAD_DIM


def _const_spec(shape):
    nd = len(shape)
    return pl.BlockSpec(shape, lambda *_: (0,) * nd, pipeline_mode=pl.Buffered(1))


def _rms_norm_f32(xf, g):
    return xf * lax.rsqrt(jnp.mean(xf * xf, axis=-1, keepdims=True) + EPS) * g


def _in_proj_kernel(*refs, tm, sample, blocks_per_seq, u_tail, kv_tail):
    if sample:
        (x_ref, hist0_ref, hist1_ref, g_ref, w_ref,
         cw_ref, yc_ref, q_ref, k_ref, v_ref, sa_ref, sb_ref, ut_ref, kvt_ref) = refs
    else:
        (x_ref, g_ref, w_ref,
         cw_ref, yc_ref, q_ref, k_ref, v_ref, sa_ref, sb_ref, ut_ref, kvt_ref, ubuf_ref) = refs

    h = _rms_norm_f32(x_ref[...], g_ref[...]).astype(BF16)

    widths = (D_CONV, D_CONV, D_CONV, Q_DIM, 2 * KV_DIM, D_MODEL, D_MODEL)
    starts = [sum(widths[:n]) for n in range(len(widths))]
    wcb_ref, wcc_ref, wch_ref, wq_ref, wkv_ref, wga_ref, wgb_ref = [
        w_ref.at[:, pl.ds(a, n)] for a, n in zip(starts, widths)]

    def proj(part_ref):
        return jnp.dot(h, part_ref[...], preferred_element_type=F32)

    u = proj(wcc_ref) * proj(wch_ref)
    w0 = cw_ref[0:1, :]
    w1 = cw_ref[1:2, :]
    w2 = cw_ref[2:3, :]
    if sample:
        conv = w0 * hist0_ref[...] + w1 * hist1_ref[...] + w2 * u
    else:
        @pl.when(pl.program_id(0) % blocks_per_seq == 0)
        def _():
            ubuf_ref[0:8, :] = jnp.zeros((8, D_CONV), F32)

        ubuf_ref[8:8 + tm, :] = u
        conv = w0 * ubuf_ref[6:6 + tm, :] + w1 * ubuf_ref[7:7 + tm, :] + w2 * u
        ubuf_ref[0:8, :] = u[tm - 8:, :]
    yc_ref[...] = (proj(wcb_ref) * conv).astype(BF16)
    ut_ref[0] = u[tm - u_tail:, :]

    q_ref[...] = (proj(wq_ref) * (HEAD_DIM ** -0.5)).astype(BF16)
    kv = proj(wkv_ref)
    k_ref[...] = kv[:, :KV_DIM].astype(BF16)
    v_ref[...] = kv[:, KV_DIM:].astype(BF16)
    kvt_ref[0] = kv[tm - kv_tail:, :]
    sa_ref[...] = jax.nn.sigmoid(proj(wga_ref)).astype(sa_ref.dtype)
    sb_ref[...] = jax.nn.sigmoid(proj(wgb_ref)).astype(sb_ref.dtype)


def _in_proj(x, g1, w_in, conv_w, *, tm, blocks_per_seq, u_tail, kv_tail, hist=None, gate_dtype=BF16):
    m = x.shape[0]
    nblk = m // tm
    sample = hist is not None
    row = lambda width: pl.BlockSpec((tm, width), lambda i: (i, 0))
    in_specs = [row(D_MODEL)]
    args = [x]
    if sample:
        in_specs += [row(D_CONV), row(D_CONV)]
        args += list(hist)
    in_specs += [_const_spec((1, D_MODEL)), _const_spec(w_in.shape), _const_spec(conv_w.shape)]
    args += [g1, w_in, conv_w]
    out_shape = [
        jax.ShapeDtypeStruct((m, D_CONV), BF16),
        jax.ShapeDtypeStruct((m, Q_DIM), BF16),
        jax.ShapeDtypeStruct((m, KV_DIM), BF16),
        jax.ShapeDtypeStruct((m, KV_DIM), BF16),
        jax.ShapeDtypeStruct((m, D_MODEL), gate_dtype),
        jax.ShapeDtypeStruct((m, D_MODEL), gate_dtype),
        jax.ShapeDtypeStruct((nblk, u_tail, D_CONV), F32),
        jax.ShapeDtypeStruct((nblk, kv_tail, 2 * KV_DIM), F32),
    ]
    out_specs = [row(D_CONV), row(Q_DIM), row(KV_DIM), row(KV_DIM), row(D_MODEL), row(D_MODEL),
                 pl.BlockSpec((1, u_tail, D_CONV), lambda i: (i, 0, 0)),
                 pl.BlockSpec((1, kv_tail, 2 * KV_DIM), lambda i: (i, 0, 0))]
    scratch = [] if sample else [pltpu.VMEM((tm + 8, D_CONV), F32)]
    return pl.pallas_call(
        functools.partial(_in_proj_kernel, tm=tm, sample=sample, blocks_per_seq=blocks_per_seq,
                          u_tail=u_tail, kv_tail=kv_tail),
        grid=(nblk,),
        in_specs=in_specs,
        out_specs=out_specs,
        out_shape=out_shape,
        scratch_shapes=scratch,
        compiler_params=pltpu.CompilerParams(dimension_semantics=("arbitrary",),
                                             vmem_limit_bytes=V7X_VMEM_LIMIT_BYTES),
        name="in_proj_sample" if sample else "in_proj_prompt",
    )(*args)


def _attn_prompt_kernel(sink_ref, q_ref, kc_ref, kp_ref, vc_ref, vp_ref, bias_ref, o_ref):
    first = pl.program_id(1) == 0
    col = lax.broadcasted_iota(jnp.int32, (ATTN_BLOCK, 2 * ATTN_BLOCK), 1)
    no_prev = jnp.logical_and(first, col < ATTN_BLOCK)
    for g in range(N_KV_HEADS):
        ks = slice(g * HEAD_DIM, (g + 1) * HEAD_DIM)
        kcat = jnp.concatenate([kp_ref[:, ks], kc_ref[:, ks]], axis=0)
        vcat = jnp.concatenate([vp_ref[:, ks], vc_ref[:, ks]], axis=0)
        for h0 in range(g * GROUP, (g + 1) * GROUP, HEADS_PER_STORE):
            outs = []
            for h in range(h0, h0 + HEADS_PER_STORE):
                hs = slice(h * HEAD_DIM, (h + 1) * HEAD_DIM)
                s = lax.dot_general(q_ref[:, hs], kcat, (((1,), (1,)), ((), ())),
                                    preferred_element_type=F32)
                s = jnp.where(no_prev, NEG_BIG, s + bias_ref[h])
                sink = sink_ref[h]
                m = jnp.maximum(jnp.max(s, axis=-1, keepdims=True), sink)
                p = jnp.exp(s - m)
                denom = jnp.sum(p, axis=-1, keepdims=True) + jnp.exp(sink - m)
                o = jnp.dot(p.astype(BF16), vcat, preferred_element_type=F32)
                outs.append((o / denom).astype(BF16))
            o_ref[:, h0 * HEAD_DIM:(h0 + HEADS_PER_STORE) * HEAD_DIM] = jnp.concatenate(outs, axis=1)


def _attn_prompt(q, k, v, bias, sinks, batch, seq):
    nb = seq // ATTN_BLOCK
    cur = lambda b, i: (b * nb + i, 0)
    prev = lambda b, i: (b * nb + jnp.maximum(i - 1, 0), 0)
    return pl.pallas_call(
        _attn_prompt_kernel,
        grid=(batch, nb),
        in_specs=[pl.BlockSpec(memory_space=pltpu.SMEM),
                  pl.BlockSpec((ATTN_BLOCK, Q_DIM), cur),
                  pl.BlockSpec((ATTN_BLOCK, KV_DIM), cur),
                  pl.BlockSpec((ATTN_BLOCK, KV_DIM), prev),
                  pl.BlockSpec((ATTN_BLOCK, KV_DIM), cur),
                  pl.BlockSpec((ATTN_BLOCK, KV_DIM), prev),
                  _const_spec(bias.shape)],
        out_specs=pl.BlockSpec((ATTN_BLOCK, Q_DIM), cur),
        out_shape=jax.ShapeDtypeStruct((batch * seq, Q_DIM), BF16),
        compiler_params=pltpu.CompilerParams(dimension_semantics=("arbitrary", "arbitrary"),
                                             vmem_limit_bytes=V7X_VMEM_LIMIT_BYTES),
        name="attn_prompt",
    )(sinks, q, k, k, v, v, bias)


def _attn_sample_kernel(qbd_ref, ckt_ref, cvt_ref, kvn_ref, knt_ref, vnt_ref, bias_ref, sink_ref, mask_ref,
                        o_ref, kwin_ref, vwin_ref, *, w_buf):
    bf16_round = lambda t: t.astype(BF16).astype(F32)
    seqs = range(SAMPLE_SEQ_PER_STEP)
    sink = sink_ref[...]
    newest = lax.broadcasted_iota(jnp.int32, (KV_DIM, w_buf), 1) == w_buf - 1
    for b in seqs:
        kwin_ref[b] = jnp.where(newest, knt_ref[b], pltpu.roll(ckt_ref[b], w_buf - 1, axis=1))
        vwin_ref[b] = jnp.where(newest, vnt_ref[b], pltpu.roll(cvt_ref[b], w_buf - 1, axis=1))
    s = [jnp.dot(qbd_ref[b], ckt_ref[b].astype(BF16), preferred_element_type=F32) + bias_ref[:, :w_buf]
         for b in seqs]
    s_new = [jnp.sum(qbd_ref[b].astype(F32) * bf16_round(kvn_ref[b:b + 1, :KV_DIM]), axis=-1, keepdims=True)
             + bias_ref[:, w_buf:w_buf + 1] for b in seqs]
    m = [jnp.maximum(jnp.maximum(jnp.max(s[b], axis=-1, keepdims=True), s_new[b]), sink) for b in seqs]
    p = [jnp.exp(s[b] - m[b]) for b in seqs]
    p_new = [jnp.exp(s_new[b] - m[b]) for b in seqs]
    denom = [jnp.sum(p[b], axis=-1, keepdims=True) + p_new[b] + jnp.exp(sink - m[b]) for b in seqs]
    of = [lax.dot_general((p[b] / denom[b]).astype(BF16), cvt_ref[b].astype(BF16), (((1,), (1,)), ((), ())),
                          preferred_element_type=F32)
          + bf16_round(p_new[b] / denom[b]) * bf16_round(kvn_ref[b:b + 1, KV_DIM:]) for b in seqs]
    for b in seqs:
        ob = of[b] * mask_ref[...]
        o_ref[b] = (ob[:, 0:HEAD_DIM] + ob[:, HEAD_DIM:2 * HEAD_DIM]
                    + ob[:, 2 * HEAD_DIM:3 * HEAD_DIM] + ob[:, 3 * HEAD_DIM:]).astype(BF16)


def _attn_sample(qbd, ckt, cvt, kvn, bias, sink_col, head_mask):
    nseq, w_buf = ckt.shape[0], ckt.shape[2]
    sb = SAMPLE_SEQ_PER_STEP
    seq3 = lambda d1, d2: pl.BlockSpec((sb, d1, d2), lambda i: (i, 0, 0))
    win = jax.ShapeDtypeStruct((nseq, KV_DIM, w_buf), F32)
    return pl.pallas_call(
        functools.partial(_attn_sample_kernel, w_buf=w_buf),
        grid=(nseq // sb,),
        in_specs=[seq3(N_HEADS, KV_DIM), seq3(KV_DIM, w_buf), seq3(KV_DIM, w_buf),
                  pl.BlockSpec((sb, 2 * KV_DIM), lambda i: (i, 0)), seq3(KV_DIM, 1), seq3(KV_DIM, 1),
                  _const_spec(bias.shape), _const_spec(sink_col.shape), _const_spec(head_mask.shape)],
        out_specs=[seq3(N_HEADS, HEAD_DIM), seq3(KV_DIM, w_buf), seq3(KV_DIM, w_buf)],
        out_shape=[jax.ShapeDtypeStruct((nseq, N_HEADS, HEAD_DIM), BF16), win, win],
        compiler_params=pltpu.CompilerParams(dimension_semantics=("arbitrary",),
                                             vmem_limit_bytes=V7X_VMEM_LIMIT_BYTES),
        name="attn_sample",
    )(qbd, ckt, cvt, kvn, kvn[:, :KV_DIM, None], kvn[:, KV_DIM:, None], bias, sink_col, head_mask)


def _route_rows(logits, row0, valid_rows, half):
    tm = logits.shape[0]
    lane = lax.broadcasted_iota(jnp.int32, logits.shape, 1)
    lane_f = lane.astype(F32)
    no_lane = float(ROUTER_LANES)

    def top1(mask):
        best = jnp.max(jnp.where(mask, logits, -jnp.inf), axis=-1, keepdims=True)
        idx = jnp.min(jnp.where(jnp.logical_and(mask, logits == best), lane_f, no_lane), axis=-1, keepdims=True)
        return best, idx

    gmask = lane < N_EXPERT_GROUPS
    gmax, grp = top1(gmask)
    gsum = jnp.sum(jnp.where(gmask, jnp.exp(logits - gmax), 0.0), axis=-1, keepdims=True)
    p_grp = 1.0 / gsum
    lo = N_EXPERT_GROUPS + EXPERTS_PER_GROUP * grp
    emask = jnp.logical_and(lane_f >= lo, lane_f < lo + EXPERTS_PER_GROUP)
    v1, i1 = top1(emask)
    v2, i2 = top1(jnp.logical_and(emask, lane_f != i1))
    e21 = jnp.exp(v2 - v1)
    w1 = p_grp / (1.0 + e21)
    w2 = p_grp * e21 / (1.0 + e21)

    oh1 = lane_f == i1
    oh2 = lane_f == i2
    if valid_rows < tm:
        valid = lax.broadcasted_iota(jnp.int32, logits.shape, 0) < valid_rows
        oh1 = jnp.logical_and(oh1, valid)
        oh2 = jnp.logical_and(oh2, valid)
    oh = oh1.astype(F32) + oh2.astype(F32)
    token = row0 + lax.broadcasted_iota(jnp.int32, (tm, 1), 0)
    key1 = (i1.astype(jnp.int32) - N_EXPERT_GROUPS) * (1 << ASSIGN_BITS) + token
    key2 = (i2.astype(jnp.int32) - N_EXPERT_GROUPS) * (1 << ASSIGN_BITS) + token + half
    w1b = lax.bitcast_convert_type(w1, jnp.int32)
    w2b = lax.bitcast_convert_type(w2, jnp.int32)
    words = jnp.where(lane == 0, key1, jnp.where(lane == 1, key2, jnp.where(lane == 2, w1b,
                      jnp.where(lane == 3, w2b, 0))))
    return words, jnp.sum(oh, axis=0, keepdims=True)


def _store_token_tiles(ref, x):
    n = x.shape[0]
    for c in range(D_MODEL // LANES):
        ref[pl.ds(c, n, stride=TILE_ROWS), :] = x[:, c * LANES:(c + 1) * LANES]


def _load_token_tiles(ref, n):
    return jnp.concatenate([ref[pl.ds(c, n, stride=TILE_ROWS), :] for c in range(D_MODEL // LANES)], axis=1)


def _pack_bf16_pairs(x):
    hw = x.shape[1] // 2
    bits = lambda v: lax.bitcast_convert_type(v.astype(BF16).astype(F32), jnp.uint32)
    return (bits(x[:, hw:]) & jnp.uint32(0xFFFF0000)) | (bits(x[:, :hw]) >> 16)


def _unpack_bf16_pairs(w):
    lo = lax.bitcast_convert_type(w << 16, F32)
    hi = lax.bitcast_convert_type(w & jnp.uint32(0xFFFF0000), F32)
    return jnp.concatenate([lo, hi], axis=1).astype(BF16)


def _out_proj_rows(yc_ref, o_ref, sa_ref, sb_ref, x_ref, wc_ref, wa_ref, wo_ref, g2_ref, wr_ref, br_ref,
                   x2_ref, h2p_ref, route_ref, keys_ref, cnt_ref, *, valid_rows, half):
    y_conv = jnp.dot(yc_ref[...], wc_ref[...], preferred_element_type=F32)
    y_attn = jnp.dot(o_ref[...], wa_ref[...], preferred_element_type=F32)
    mix = (sa_ref[...].astype(F32) * y_conv + sb_ref[...].astype(F32) * y_attn).astype(BF16)
    x2 = x_ref[...] + jnp.dot(mix, wo_ref[...], preferred_element_type=F32)
    x2_ref[...] = x2
    h2 = _rms_norm_f32(x2, g2_ref[...])
    h2p_ref[...] = _pack_bf16_pairs(h2)
    logits = jnp.dot(h2.astype(BF16), wr_ref[...], preferred_element_type=F32) + br_ref[...]
    words, cnt = _route_rows(logits, pl.program_id(0) * x_ref.shape[0], valid_rows, half)
    route_ref[...] = words
    keys_ref[...] = words.T[:TILE_ROWS, :]
    cnt_ref[...] += cnt


def _out_proj_kernel(*refs, n_first, valid_rows_second, half):
    first, second, shared = refs[0:5], refs[5:10], refs[10:]
    cnt_ref = shared[-1]
    tm = first[4].shape[0]

    @pl.when(pl.program_id(0) == 0)
    def _():
        cnt_ref[...] = jnp.zeros_like(cnt_ref)

    @pl.when(pl.program_id(0) < n_first)
    def _():
        _out_proj_rows(*first, *shared, valid_rows=tm, half=half)

    @pl.when(pl.program_id(0) >= n_first)
    def _():
        _out_proj_rows(*second, *shared, valid_rows=valid_rows_second, half=half)


def _out_proj(acts_a, acts_b, wc, wa, wo, g2, wr, br, *, tm, valid_rows_b, half):
    na = acts_a[4].shape[0] // tm
    nb = acts_b[4].shape[0] // tm
    assert nb == 1
    m_total = (na + nb) * tm
    spec_a = lambda width: pl.BlockSpec((tm, width), lambda i: (jnp.minimum(i, na - 1), 0))
    spec_b = lambda width: pl.BlockSpec((tm, width), lambda i: (jnp.maximum(i - na, 0), 0))
    widths = (D_CONV, Q_DIM, D_MODEL, D_MODEL, D_MODEL)
    in_specs = [spec_a(w) for w in widths] + [spec_b(w) for w in widths]
    in_specs += [_const_spec(wc.shape), _const_spec(wa.shape), _const_spec(wo.shape),
                 _const_spec(g2.shape), _const_spec(wr.shape), _const_spec(br.shape)]
    orow = lambda width: pl.BlockSpec((tm, width), lambda i: (i, 0))
    return pl.pallas_call(
        functools.partial(_out_proj_kernel, n_first=na, valid_rows_second=valid_rows_b, half=half),
        grid=(na + nb,),
        in_specs=in_specs,
        out_specs=[orow(D_MODEL), orow(D_MODEL // 2), orow(ROUTER_LANES),
                   pl.BlockSpec((TILE_ROWS, tm), lambda i: (0, i)),
                   pl.BlockSpec((1, ROUTER_LANES), lambda i: (0, 0))],
        out_shape=[jax.ShapeDtypeStruct((m_total, D_MODEL), F32),
                   jax.ShapeDtypeStruct((m_total, D_MODEL // 2), jnp.uint32),
                   jax.ShapeDtypeStruct((m_total, ROUTER_LANES), jnp.int32),
                   jax.ShapeDtypeStruct((TILE_ROWS, m_total), jnp.int32),
                   jax.ShapeDtypeStruct((1, ROUTER_LANES), F32)],
        compiler_params=pltpu.CompilerParams(dimension_semantics=("arbitrary",),
                                             vmem_limit_bytes=V7X_VMEM_LIMIT_BYTES),
        name="out_proj",
    )(*acts_a, *acts_b, wc, wa, wo, g2, wr, br)


def _block_table_kernel(counts_ref, bexp_ref, bpos_ref, bcnt_ref, bslot_ref, bnext_ref, nused_ref, first_ref,
                        *, n_blocks):
    def per_expert(e, carry):
        blk0, pos0, ordinal = carry
        cnt = counts_ref[e]
        nblk = lax.shift_right_logical(cnt + (MOE_BLOCK - 1), MOE_BLOCK_LOG2)
        first_ref[e] = jnp.where(nblk > 0, blk0, -1)

        def mark(b, c):
            off = (b - blk0) * MOE_BLOCK
            bexp_ref[b] = e
            bpos_ref[b] = pos0 + off
            bcnt_ref[b] = jnp.minimum(cnt - off, MOE_BLOCK)
            bslot_ref[b] = jnp.where(b == blk0, ordinal & 1, -1)
            bnext_ref[b] = -1
            return c
        lax.fori_loop(blk0, blk0 + nblk, mark, 0)
        return blk0 + nblk, pos0 + cnt, ordinal + jnp.where(nblk > 0, 1, 0)

    n_used, _, _ = lax.fori_loop(0, N_EXPERTS, per_expert, (0, 0, 0))
    nused_ref[0] = n_used

    def unused(b, c):
        bexp_ref[b] = N_EXPERTS - 1
        bpos_ref[b] = 0
        bcnt_ref[b] = 0
        bslot_ref[b] = -1
        bnext_ref[b] = -1
        return c
    lax.fori_loop(n_used, n_blocks, unused, 0)

    def link(k, nxt):
        e = N_EXPERTS - 1 - k
        fb = first_ref[e]

        @pl.when(fb >= 0)
        def _():
            bnext_ref[fb] = nxt
        return jnp.where(fb >= 0, e, nxt)
    lax.fori_loop(0, N_EXPERTS, link, -1)


def _block_tables(counts, n_blocks):
    smem = pl.BlockSpec(memory_space=pltpu.SMEM)
    blk = jax.ShapeDtypeStruct((n_blocks,), jnp.int32)
    return pl.pallas_call(
        functools.partial(_block_table_kernel, n_blocks=n_blocks),
        in_specs=[smem],
        out_specs=[smem] * 6,
        out_shape=[blk] * 5 + [jax.ShapeDtypeStruct((1,), jnp.int32)],
        scratch_shapes=[pltpu.SMEM((N_EXPERTS,), jnp.int32)],
        name="block_tables",
    )(counts)


def _moe_kernel(tok_ref, order_ref, bexp_ref, bpos_ref, bcnt_ref, bslot_ref, bnext_ref, nused_ref,
                h2p_ref, wg_hbm, wu_hbm, wd_hbm, contrib_hbm,
                xs_ref, ys_ref, wgf_ref, wuf_ref, wdf_ref, wgb_ref, wub_ref, wdb_ref, ssem_ref, wsem_ref,
                *, m_total, half, n_blocks):
    step = pl.program_id(0)
    n_used = nused_ref[0]
    tile = lambda t: pl.ds(pl.multiple_of(t * TILE_ROWS, TILE_ROWS), TILE_ROWS)

    def gather(pos0, slt):
        for r in range(MOE_BLOCK):
            xs_ref[slt, pl.ds(r, 1), :] = h2p_ref[pl.ds(tok_ref[pos0 + r], 1), :]

    def scatter(pos0, cnt, trash0, slt, r):
        dst = jnp.where(r < cnt, order_ref[pos0 + r], trash0 + r)
        return pltpu.make_async_copy(ys_ref.at[slt, tile(r), :], contrib_hbm.at[tile(dst), :], ssem_ref.at[slt])

    def scatter_wait(slt, r):
        pltpu.make_async_copy(ys_ref.at[slt, tile(r), :], contrib_hbm.at[tile(0), :], ssem_ref.at[slt]).wait()

    @pl.when(step == 0)
    def _():
        ys_ref[...] = jnp.zeros_like(ys_ref)
        gap = half - m_total
        fills = [(2 * half + s * MOE_BLOCK, MOE_BLOCK) for s in range(2)]
        fills += [(k * half + m_total, gap) for k in range(TOP_K)] if gap else []
        for start, n in fills:
            fill = pltpu.make_async_copy(ys_ref.at[0, pl.ds(0, n * TILE_ROWS), :],
                                         contrib_hbm.at[pl.ds(start * TILE_ROWS, n * TILE_ROWS), :], ssem_ref.at[0])
            fill.start()
            fill.wait()

    def weight_copies(e, s):
        return [pltpu.make_async_copy(src.at[e], dst.at[s], wsem_ref.at[s])
                for src, dst in ((wg_hbm, wgf_ref), (wu_hbm, wuf_ref), (wd_hbm, wdf_ref))]

    @pl.when(step == 0)
    def _():
        for c in weight_copies(bexp_ref[0], 0):
            c.start()
        gather(bpos_ref[0], 0)

    def wait_scatters(slt):
        for r in range(MOE_BLOCK):
            scatter_wait(slt, r)

    def expert_weights(i):
        wslot = bslot_ref[i]

        @pl.when(wslot >= 0)
        def _():
            for c in weight_copies(0, wslot):
                c.wait()

        @pl.when(jnp.logical_and(wslot >= 0, bnext_ref[i] >= 0))
        def _():
            for c in weight_copies(bnext_ref[i], 1 - wslot):
                c.start()

        @pl.when(wslot >= 0)
        def _():
            wgb_ref[...] = wgf_ref[wslot].astype(BF16)
            wub_ref[...] = wuf_ref[wslot].astype(BF16)
            wdb_ref[...] = wdf_ref[wslot].astype(BF16)

    def run_block(i, slt):
        pos_next = bpos_ref[jnp.minimum(i + 1, n_blocks - 1)]
        pos0 = bpos_ref[i]
        cnt = bcnt_ref[i]
        trash0 = 2 * half + slt * MOE_BLOCK
        gather(pos_next, 1 - slt)
        xb = _unpack_bf16_pairs(xs_ref[slt])
        gate = jnp.dot(xb, wgb_ref[...], preferred_element_type=F32)
        up = jnp.dot(xb, wub_ref[...], preferred_element_type=F32)
        hmid = (jax.nn.silu(gate) * up).astype(BF16)
        _store_token_tiles(ys_ref.at[slt], jnp.dot(hmid, wdb_ref[...], preferred_element_type=F32))
        for r in range(MOE_BLOCK):
            scatter(pos0, cnt, trash0, slt, r).start(priority=r % 2)

    for slt in range(2):
        i = 2 * step + slt
        active = i < n_used
        last_active = i == n_used - 1
        pl.when(jnp.logical_and(active, i >= 2))(functools.partial(wait_scatters, slt))
        expert_weights(i)
        pl.when(active)(functools.partial(run_block, i, slt))
        pl.when(last_active)(functools.partial(wait_scatters, slt))
        pl.when(jnp.logical_and(last_active, i >= 1))(functools.partial(wait_scatters, 1 - slt))


def _moe_experts(tok, order, tables, h2p, w_gate, w_up, w_down, *, m_total, half):
    n_blocks = tables[0].shape[0]
    assert 0 <= half - m_total <= MOE_BLOCK and n_blocks % 2 == 0
    hbm = pl.BlockSpec(memory_space=pl.ANY)
    grid_spec = pltpu.PrefetchScalarGridSpec(
        num_scalar_prefetch=2 + len(tables),
        grid=(n_blocks // 2,),
        in_specs=[_const_spec(h2p.shape), hbm, hbm, hbm],
        out_specs=hbm,
        scratch_shapes=[pltpu.VMEM((2, MOE_BLOCK, D_MODEL // 2), jnp.uint32),
                        pltpu.VMEM((2, MOE_BLOCK * TILE_ROWS, LANES), F32),
                        pltpu.VMEM((2, D_MODEL, D_EXPERT), F32),
                        pltpu.VMEM((2, D_MODEL, D_EXPERT), F32),
                        pltpu.VMEM((2, D_EXPERT, D_MODEL), F32),
                        pltpu.VMEM((D_MODEL, D_EXPERT), BF16),
                        pltpu.VMEM((D_MODEL, D_EXPERT), BF16),
                        pltpu.VMEM((D_EXPERT, D_MODEL), BF16),
                        pltpu.SemaphoreType.DMA((2,)),
                        pltpu.SemaphoreType.DMA((2,))],
    )
    return pl.pallas_call(
        functools.partial(_moe_kernel, m_total=m_total, half=half, n_blocks=n_blocks),
        grid_spec=grid_spec,
        out_shape=jax.ShapeDtypeStruct(((2 * half + 2 * MOE_BLOCK) * TILE_ROWS, LANES), F32),
        compiler_params=pltpu.CompilerParams(dimension_semantics=("arbitrary",),
                                             vmem_limit_bytes=MOE_VMEM_LIMIT_BYTES),
        name="moe_experts",
    )(tok, order, *tables, h2p, w_gate, w_up, w_down)


def _combine_kernel(c0_ref, c1_ref, x2_ref, route_ref, gf_ref, y_ref):
    tc = x2_ref.shape[0]
    w = lax.bitcast_convert_type(route_ref[:, TOP_K:2 * TOP_K], F32)
    moe = w[:, 0:1] * _load_token_tiles(c0_ref, tc) + w[:, 1:2] * _load_token_tiles(c1_ref, tc)
    y_ref[...] = _rms_norm_f32(x2_ref[...] + moe, gf_ref[...])


def _moe_combine(contrib, x2, route, gf, *, row_off, m, tc, half):
    off = row_off // tc
    assert row_off % tc == 0 and half % tc == 0
    ctile = lambda k: pl.BlockSpec((tc * TILE_ROWS, LANES), lambda i: (i + off + k * (half // tc), 0))
    return pl.pallas_call(
        _combine_kernel,
        grid=(m // tc,),
        in_specs=[ctile(0), ctile(1),
                  pl.BlockSpec((tc, D_MODEL), lambda i: (i + off, 0)),
                  pl.BlockSpec((tc, ROUTER_LANES), lambda i: (i + off, 0)),
                  _const_spec((1, D_MODEL))],
        out_specs=pl.BlockSpec((tc, D_MODEL), lambda i: (i, 0)),
        out_shape=jax.ShapeDtypeStruct((m, D_MODEL), F32),
        compiler_params=pltpu.CompilerParams(dimension_semantics=("arbitrary",),
                                             vmem_limit_bytes=V7X_VMEM_LIMIT_BYTES),
        name="moe_combine_prompt" if row_off == 0 else "moe_combine_sample",
    )(contrib, contrib, x2, route, gf)


def _t5_bucket(dist):
    n = jnp.maximum(dist, 0)
    max_exact = N_BUCKETS // 2
    nf = jnp.maximum(n, 1).astype(F32)
    large = max_exact + (jnp.log(nf / max_exact) / math.log(MAX_DISTANCE / max_exact)
                         * (N_BUCKETS - max_exact)).astype(jnp.int32)
    large = jnp.minimum(large, N_BUCKETS - 1)
    return jnp.where(n < max_exact, n, large)


def _bucket_bias(rel_bias, dist, valid):
    buckets = _t5_bucket(dist).reshape(1, -1)
    onehot = (buckets == jnp.arange(N_BUCKETS, dtype=jnp.int32)[:, None]).astype(F32)
    bias = jnp.dot(rel_bias.astype(F32).T, onehot, precision=lax.Precision.HIGHEST)
    return jnp.where(valid.reshape(1, -1), bias, NEG_BIG).reshape((rel_bias.shape[1],) + dist.shape)


def _prompt_bias_table(rel_bias):
    qi = jnp.arange(ATTN_BLOCK, dtype=jnp.int32)[:, None]
    kj = jnp.arange(2 * ATTN_BLOCK, dtype=jnp.int32)[None, :] - ATTN_BLOCK
    dist = qi - kj
    return _bucket_bias(rel_bias, dist, (dist >= 0) & (dist <= WINDOW))


def _sample_bias_table(rel_bias, w_buf):
    dist = w_buf - jnp.arange(w_buf + 1, dtype=jnp.int32)
    return _bucket_bias(rel_bias, dist, dist <= WINDOW)


def kernel(x_prompt, x_sample, cache_conv, cache_k, cache_v, norm1_g, w_in, conv_w, w_conv_out, w_attn_out, w_o, sinks, rel_bias, norm2_g, w_router_group, b_router_group, w_router_expert, b_router_expert, w_e_gate, w_e_up, w_e_down, norm_f_g):
    assert norm1_g.shape[0] == 1, "single-layer configuration"
    batch, seq, _ = x_prompt.shape
    nseq = x_sample.shape[0]
    w_buf = cache_k.shape[2]
    mp = batch * seq
    m_total = mp + nseq
    assert seq % TM_DENSE == 0 and seq % ATTN_BLOCK == 0 and mp % COMBINE_BLOCK == 0
    assert nseq % SAMPLE_SEQ_PER_STEP == 0 and mp % nseq == 0
    assert TOP_K == 2 and MOE_BLOCK == 1 << MOE_BLOCK_LOG2 and m_total * TOP_K < 1 << ASSIGN_BITS

    g1 = norm1_g[0][None, :]
    g2 = norm2_g[0][None, :]
    gf = norm_f_g[None, :]
    wi = w_in[0].astype(BF16)
    cw = conv_w[0]
    wc = w_conv_out[0].astype(BF16)
    wa = w_attn_out[0].astype(BF16)
    wo = w_o[0].astype(BF16)
    pad_cols = ROUTER_LANES - N_EXPERT_GROUPS - N_EXPERTS
    wr = jnp.concatenate([w_router_group[0], w_router_expert[0],
                          jnp.zeros((D_MODEL, pad_cols), F32)], axis=1).astype(BF16)
    br = jnp.concatenate([b_router_group[0], b_router_expert[0], jnp.zeros((pad_cols,), F32)])[None, :]
    sink = sinks[0].astype(F32)

    xp = x_prompt.reshape(mp, D_MODEL)
    bps = seq // TM_DENSE
    yc_p, q_p, k_p, v_p, sa_p, sb_p, ut_p, kvt_p = _in_proj(
        xp, g1, wi, cw, tm=TM_DENSE, blocks_per_seq=bps, u_tail=8, kv_tail=WINDOW)
    o_p = _attn_prompt(q_p, k_p, v_p, _prompt_bias_table(rel_bias), sink, batch, seq)

    pad_rows = lambda t: jnp.pad(t, ((0, TM_DENSE - nseq), (0, 0)))
    xs = pad_rows(x_sample.reshape(nseq, D_MODEL))
    hist = (pad_rows(cache_conv[0][:, 0, :]), pad_rows(cache_conv[0][:, 1, :]))
    yc_s, q_s, _, _, sa_s, sb_s, ut_s, kvt_s = _in_proj(
        xs, g1, wi, cw, tm=TM_DENSE, blocks_per_seq=1, u_tail=TM_DENSE, kv_tail=TM_DENSE, hist=hist,
        gate_dtype=F32)
    u_s = ut_s[0, :nseq]
    kv_s = kvt_s[0, :nseq]
    head_mask = (jnp.arange(KV_DIM)[None, :] // HEAD_DIM == jnp.arange(N_HEADS)[:, None] // GROUP)
    qbd = (jnp.tile(q_s[:nseq].reshape(nseq, N_HEADS, HEAD_DIM), (1, 1, N_KV_HEADS))
           * head_mask[None].astype(BF16))
    to_keys_minor = lambda c: jnp.transpose(c.reshape(nseq, w_buf, KV_DIM), (0, 2, 1))
    from_keys_minor = lambda c: jnp.transpose(c, (0, 2, 1)).reshape(1, nseq, w_buf, N_KV_HEADS, HEAD_DIM)
    o_s, kwin_s, vwin_s = _attn_sample(qbd, to_keys_minor(cache_k[0]), to_keys_minor(cache_v[0]), kv_s,
                                       _sample_bias_table(rel_bias, w_buf), sink[:, None],
                                       head_mask.astype(F32))
    o_s = pad_rows(o_s.reshape(nseq, Q_DIM))

    half = -(-m_total // COMBINE_BLOCK) * COMBINE_BLOCK
    assert half % nseq == 0 and TOP_K * half < 1 << ASSIGN_BITS
    x2, h2p, route, route_t, cnt = _out_proj((yc_p, o_p, sa_p, sb_p, xp), (yc_s, o_s, sa_s, sb_s, xs),
                                             wc, wa, wo, g2, wr, br, tm=TM_DENSE, valid_rows_b=nseq, half=half)

    n_assign = m_total * TOP_K
    keys = route_t[0:TOP_K, :m_total].reshape(-1)
    counts = cnt[0, N_EXPERT_GROUPS:N_EXPERT_GROUPS + N_EXPERTS].astype(jnp.int32)
    order = jnp.pad(jnp.sort(keys) & ((1 << ASSIGN_BITS) - 1), (0, MOE_BLOCK))
    tok = jnp.where(order >= half, order - half, order)
    n_blocks = -(-n_assign // MOE_BLOCK) + N_EXPERTS
    n_blocks += n_blocks % 2
    tables = _block_tables(counts, n_blocks)
    contrib = _moe_experts(tok, order, tables, h2p, w_e_gate[0], w_e_up[0], w_e_down[0],
                           m_total=m_total, half=half)
    y_p = _moe_combine(contrib, x2, route, gf, row_off=0, m=mp, tc=COMBINE_BLOCK, half=half)
    y_s = _moe_combine(contrib, x2, route, gf, row_off=mp, m=nseq, tc=nseq, half=half)

    y_prompt = y_p.reshape(batch, seq, D_MODEL)
    y_sample = y_s.reshape(nseq, 1, D_MODEL)
    conv_state_prompt = ut_p.reshape(batch, bps, 8, D_CONV)[:, -1, 8 - (CONV_WIDTH - 1):, :][None]
    kv_last = kvt_p.reshape(batch, bps, WINDOW, 2 * KV_DIM)[:, -1]
    k_win_prompt = kv_last[:, :, :KV_DIM].reshape(batch, WINDOW, N_KV_HEADS, HEAD_DIM)[None]
    v_win_prompt = kv_last[:, :, KV_DIM:].reshape(batch, WINDOW, N_KV_HEADS, HEAD_DIM)[None]
    conv_state_sample = jnp.concatenate([cache_conv[0][:, 1:, :], u_s[:, None, :]], axis=1)[None]
    k_win_sample = from_keys_minor(kwin_s)
    v_win_sample = from_keys_minor(vwin_s)
    return (y_prompt, y_sample, conv_state_prompt, k_win_prompt, v_win_prompt,
            conv_state_sample, k_win_sample, v_win_sample)
```

```python
import functools
import math

import jax
import jax.numpy as jnp
from jax import lax
from jax.experimental import pallas as pl
from jax.experimental.pallas import tpu as pltpu

D_MODEL = 1024
D_CONV = 1024
CONV_WIDTH = 3
N_HEADS = 16
N_KV_HEADS = 4
HEAD_DIM = 64
GROUP = N_HEADS // N_KV_HEADS
WINDOW = 128
Q_DIM = N_HEADS * HEAD_DIM
KV_DIM = N_KV_HEADS * HEAD_DIM
N_BUCKETS = 32
MAX_DISTANCE = 128
N_EXPERT_GROUPS = 4
EXPERTS_PER_GROUP = 8
N_EXPERTS = N_EXPERT_GROUPS * EXPERTS_PER_GROUP
TOP_K = 2
D_EXPERT = 512
EPS = 1e-6
PAST_LEN = 8192

BF16 = jnp.bfloat16
F32 = jnp.float32
NEG_BIG = -1e30

V7X_VMEM_LIMIT_BYTES = 56 * 1024 * 1024
MOE_VMEM_LIMIT_BYTES = 62 * 1024 * 1024
TILE_ROWS = 8
LANES = 128
ROUTER_LANES = 128
TM_DENSE = 512
TM_IN_PROJ = 1024
ATTN_BLOCK = 128
MOE_BLOCK = 256
MOE_BLOCK_LOG2 = 8
ASSIGN_BITS = 16
COMBINE_BLOCK = 256
SAMPLE_SEQ_PER_STEP = 16
HEADS_PER_STORE = LANES // HEAD_DIM


def _const_spec(shape):
    nd = len(shape)
    return pl.BlockSpec(shape, lambda *_: (0,) * nd, pipeline_mode=pl.Buffered(1))


def _rms_norm_f32(xf, g):
    return xf * lax.rsqrt(jnp.mean(xf * xf, axis=-1, keepdims=True) + EPS) * g


def _in_proj_kernel(*refs, tm, sample, blocks_per_seq, u_tail, kv_tail):
    if sample:
        (x_ref, hist0_ref, hist1_ref, g_ref, w_ref,
         cw_ref, yc_ref, q_ref, k_ref, v_ref, sa_ref, sb_ref, ut_ref, kvt_ref) = refs
    else:
        (x_ref, g_ref, w_ref,
         cw_ref, yc_ref, q_ref, k_ref, v_ref, sa_ref, sb_ref, ut_ref, kvt_ref, ubuf_ref) = refs

    h = _rms_norm_f32(x_ref[...], g_ref[...]).astype(BF16)

    widths = (D_CONV, D_CONV, D_CONV, Q_DIM, 2 * KV_DIM, D_MODEL, D_MODEL)
    starts = [sum(widths[:n]) for n in range(len(widths))]
    wcb_ref, wcc_ref, wch_ref, wq_ref, wkv_ref, wga_ref, wgb_ref = [
        w_ref.at[:, pl.ds(a, n)] for a, n in zip(starts, widths)]

    def proj(part_ref):
        return jnp.dot(h, part_ref[...], preferred_element_type=F32)

    u = proj(wcc_ref) * proj(wch_ref)
    w0 = cw_ref[0:1, :]
    w1 = cw_ref[1:2, :]
    w2 = cw_ref[2:3, :]
    if sample:
        conv = w0 * hist0_ref[...] + w1 * hist1_ref[...] + w2 * u
    else:
        @pl.when(pl.program_id(0) % blocks_per_seq == 0)
        def _():
            ubuf_ref[0:8, :] = jnp.zeros((8, D_CONV), F32)

        ubuf_ref[8:8 + tm, :] = u
        conv = w0 * ubuf_ref[6:6 + tm, :] + w1 * ubuf_ref[7:7 + tm, :] + w2 * u
        ubuf_ref[0:8, :] = u[tm - 8:, :]
    yc_ref[...] = (proj(wcb_ref) * conv).astype(BF16)
    ut_ref[0] = u[tm - u_tail:, :]

    q_ref[...] = (proj(wq_ref) * (HEAD_DIM ** -0.5)).astype(BF16)
    kv = proj(wkv_ref)
    k_ref[...] = kv[:, :KV_DIM].astype(BF16)
    v_ref[...] = kv[:, KV_DIM:].astype(BF16)
    kvt_ref[0] = kv[tm - kv_tail:, :]
    sa_ref[...] = jax.nn.sigmoid(proj(wga_ref)).astype(sa_ref.dtype)
    sb_ref[...] = jax.nn.sigmoid(proj(wgb_ref)).astype(sb_ref.dtype)


def _in_proj(x, g1, w_in, conv_w, *, tm, blocks_per_seq, u_tail, kv_tail, hist=None, gate_dtype=BF16):
    m = x.shape[0]
    nblk = m // tm
    sample = hist is not None
    row = lambda width: pl.BlockSpec((tm, width), lambda i: (i, 0))
    in_specs = [row(D_MODEL)]
    args = [x]
    if sample:
        in_specs += [row(D_CONV), row(D_CONV)]
        args += list(hist)
    in_specs += [_const_spec((1, D_MODEL)), _const_spec(w_in.shape), _const_spec(conv_w.shape)]
    args += [g1, w_in, conv_w]
    out_shape = [
        jax.ShapeDtypeStruct((m, D_CONV), BF16),
        jax.ShapeDtypeStruct((m, Q_DIM), BF16),
        jax.ShapeDtypeStruct((m, KV_DIM), BF16),
        jax.ShapeDtypeStruct((m, KV_DIM), BF16),
        jax.ShapeDtypeStruct((m, D_MODEL), gate_dtype),
        jax.ShapeDtypeStruct((m, D_MODEL), gate_dtype),
        jax.ShapeDtypeStruct((nblk, u_tail, D_CONV), F32),
        jax.ShapeDtypeStruct((nblk, kv_tail, 2 * KV_DIM), F32),
    ]
    out_specs = [row(D_CONV), row(Q_DIM), row(KV_DIM), row(KV_DIM), row(D_MODEL), row(D_MODEL),
                 pl.BlockSpec((1, u_tail, D_CONV), lambda i: (i, 0, 0)),
                 pl.BlockSpec((1, kv_tail, 2 * KV_DIM), lambda i: (i, 0, 0))]
    scratch = [] if sample else [pltpu.VMEM((tm + 8, D_CONV), F32)]
    return pl.pallas_call(
        functools.partial(_in_proj_kernel, tm=tm, sample=sample, blocks_per_seq=blocks_per_seq,
                          u_tail=u_tail, kv_tail=kv_tail),
        grid=(nblk,),
        in_specs=in_specs,
        out_specs=out_specs,
        out_shape=out_shape,
        scratch_shapes=scratch,
        compiler_params=pltpu.CompilerParams(dimension_semantics=("arbitrary",),
                                             vmem_limit_bytes=MOE_VMEM_LIMIT_BYTES),
        name="in_proj_sample" if sample else "in_proj_prompt",
    )(*args)


def _attn_prompt_kernel(sink_ref, q_ref, kc_ref, kp_ref, vc_ref, vp_ref, bias_ref, o_ref):
    first = pl.program_id(1) == 0
    col = lax.broadcasted_iota(jnp.int32, (ATTN_BLOCK, 2 * ATTN_BLOCK), 1)
    no_prev = jnp.logical_and(first, col < ATTN_BLOCK)
    for g in range(N_KV_HEADS):
        ks = slice(g * HEAD_DIM, (g + 1) * HEAD_DIM)
        kcat = jnp.concatenate([kp_ref[:, ks], kc_ref[:, ks]], axis=0)
        vcat = jnp.concatenate([vp_ref[:, ks], vc_ref[:, ks]], axis=0)
        for h0 in range(g * GROUP, (g + 1) * GROUP, HEADS_PER_STORE):
            outs = []
            for h in range(h0, h0 + HEADS_PER_STORE):
                hs = slice(h * HEAD_DIM, (h + 1) * HEAD_DIM)
                s = lax.dot_general(q_ref[:, hs], kcat, (((1,), (1,)), ((), ())),
                                    preferred_element_type=F32)
                s = jnp.where(no_prev, NEG_BIG, s + bias_ref[h])
                sink = sink_ref[h]
                m = jnp.maximum(jnp.max(s, axis=-1, keepdims=True), sink)
                p = jnp.exp(s - m)
                denom = jnp.sum(p, axis=-1, keepdims=True) + jnp.exp(sink - m)
                o = jnp.dot(p.astype(BF16), vcat, preferred_element_type=F32)
                outs.append((o / denom).astype(BF16))
            o_ref[:, h0 * HEAD_DIM:(h0 + HEADS_PER_STORE) * HEAD_DIM] = jnp.concatenate(outs, axis=1)


def _attn_prompt(q, k, v, bias, sinks, batch, seq):
    nb = seq // ATTN_BLOCK
    cur = lambda b, i: (b * nb + i, 0)
    prev = lambda b, i: (b * nb + jnp.maximum(i - 1, 0), 0)
    return pl.pallas_call(
        _attn_prompt_kernel,
        grid=(batch, nb),
        in_specs=[pl.BlockSpec(memory_space=pltpu.SMEM),
                  pl.BlockSpec((ATTN_BLOCK, Q_DIM), cur),
                  pl.BlockSpec((ATTN_BLOCK, KV_DIM), cur),
                  pl.BlockSpec((ATTN_BLOCK, KV_DIM), prev),
                  pl.BlockSpec((ATTN_BLOCK, KV_DIM), cur),
                  pl.BlockSpec((ATTN_BLOCK, KV_DIM), prev),
                  _const_spec(bias.shape)],
        out_specs=pl.BlockSpec((ATTN_BLOCK, Q_DIM), cur),
        out_shape=jax.ShapeDtypeStruct((batch * seq, Q_DIM), BF16),
        compiler_params=pltpu.CompilerParams(dimension_semantics=("arbitrary", "arbitrary"),
                                             vmem_limit_bytes=V7X_VMEM_LIMIT_BYTES),
        name="attn_prompt",
    )(sinks, q, k, k, v, v, bias)


def _attn_sample_kernel(qbd_ref, ckt_ref, cvt_ref, kvn_ref, knt_ref, vnt_ref, bias_ref, sink_ref, mask_ref,
                        o_ref, kwin_ref, vwin_ref, *, w_buf):
    bf16_round = lambda t: t.astype(BF16).astype(F32)
    seqs = range(SAMPLE_SEQ_PER_STEP)
    sink = sink_ref[...]
    newest = lax.broadcasted_iota(jnp.int32, (KV_DIM, w_buf), 1) == w_buf - 1
    for b in seqs:
        kwin_ref[b] = jnp.where(newest, knt_ref[b], pltpu.roll(ckt_ref[b], w_buf - 1, axis=1))
        vwin_ref[b] = jnp.where(newest, vnt_ref[b], pltpu.roll(cvt_ref[b], w_buf - 1, axis=1))
    s = [jnp.dot(qbd_ref[b], ckt_ref[b].astype(BF16), preferred_element_type=F32) + bias_ref[:, :w_buf]
         for b in seqs]
    s_new = [jnp.sum(qbd_ref[b].astype(F32) * bf16_round(kvn_ref[b:b + 1, :KV_DIM]), axis=-1, keepdims=True)
             + bias_ref[:, w_buf:w_buf + 1] for b in seqs]
    m = [jnp.maximum(jnp.maximum(jnp.max(s[b], axis=-1, keepdims=True), s_new[b]), sink) for b in seqs]
    p = [jnp.exp(s[b] - m[b]) for b in seqs]
    p_new = [jnp.exp(s_new[b] - m[b]) for b in seqs]
    denom = [jnp.sum(p[b], axis=-1, keepdims=True) + p_new[b] + jnp.exp(sink - m[b]) for b in seqs]
    of = [lax.dot_general((p[b] / denom[b]).astype(BF16), cvt_ref[b].astype(BF16), (((1,), (1,)), ((), ())),
                          preferred_element_type=F32)
          + bf16_round(p_new[b] / denom[b]) * bf16_round(kvn_ref[b:b + 1, KV_DIM:]) for b in seqs]
    for b in seqs:
        ob = of[b] * mask_ref[...]
        o_ref[b] = (ob[:, 0:HEAD_DIM] + ob[:, HEAD_DIM:2 * HEAD_DIM]
                    + ob[:, 2 * HEAD_DIM:3 * HEAD_DIM] + ob[:, 3 * HEAD_DIM:]).astype(BF16)


def _attn_sample(qbd, ckt, cvt, kvn, bias, sink_col, head_mask):
    nseq, w_buf = ckt.shape[0], ckt.shape[2]
    sb = SAMPLE_SEQ_PER_STEP
    seq3 = lambda d1, d2: pl.BlockSpec((sb, d1, d2), lambda i: (i, 0, 0))
    win = jax.ShapeDtypeStruct((nseq, KV_DIM, w_buf), F32)
    return pl.pallas_call(
        functools.partial(_attn_sample_kernel, w_buf=w_buf),
        grid=(nseq // sb,),
        in_specs=[seq3(N_HEADS, KV_DIM), seq3(KV_DIM, w_buf), seq3(KV_DIM, w_buf),
                  pl.BlockSpec((sb, 2 * KV_DIM), lambda i: (i, 0)), seq3(KV_DIM, 1), seq3(KV_DIM, 1),
                  _const_spec(bias.shape), _const_spec(sink_col.shape), _const_spec(head_mask.shape)],
        out_specs=[seq3(N_HEADS, HEAD_DIM), seq3(KV_DIM, w_buf), seq3(KV_DIM, w_buf)],
        out_shape=[jax.ShapeDtypeStruct((nseq, N_HEADS, HEAD_DIM), BF16), win, win],
        compiler_params=pltpu.CompilerParams(dimension_semantics=("arbitrary",),
                                             vmem_limit_bytes=V7X_VMEM_LIMIT_BYTES),
        name="attn_sample",
    )(qbd, ckt, cvt, kvn, kvn[:, :KV_DIM, None], kvn[:, KV_DIM:, None], bias, sink_col, head_mask)


def _route_rows(logits, row0, valid_rows, half):
    tm = logits.shape[0]
    lane = lax.broadcasted_iota(jnp.int32, logits.shape, 1)
    lane_f = lane.astype(F32)
    no_lane = float(ROUTER_LANES)

    def top1(mask):
        best = jnp.max(jnp.where(mask, logits, -jnp.inf), axis=-1, keepdims=True)
        idx = jnp.min(jnp.where(jnp.logical_and(mask, logits == best), lane_f, no_lane), axis=-1, keepdims=True)
        return best, idx

    gmask = lane < N_EXPERT_GROUPS
    gmax, grp = top1(gmask)
    gsum = jnp.sum(jnp.where(gmask, jnp.exp(logits - gmax), 0.0), axis=-1, keepdims=True)
    p_grp = 1.0 / gsum
    lo = N_EXPERT_GROUPS + EXPERTS_PER_GROUP * grp
    emask = jnp.logical_and(lane_f >= lo, lane_f < lo + EXPERTS_PER_GROUP)
    v1, i1 = top1(emask)
    v2, i2 = top1(jnp.logical_and(emask, lane_f != i1))
    e21 = jnp.exp(v2 - v1)
    w1 = p_grp / (1.0 + e21)
    w2 = p_grp * e21 / (1.0 + e21)

    oh1 = lane_f == i1
    oh2 = lane_f == i2
    if valid_rows < tm:
        valid = lax.broadcasted_iota(jnp.int32, logits.shape, 0) < valid_rows
        oh1 = jnp.logical_and(oh1, valid)
        oh2 = jnp.logical_and(oh2, valid)
    oh = oh1.astype(F32) + oh2.astype(F32)
    token = row0 + lax.broadcasted_iota(jnp.int32, (tm, 1), 0)
    key1 = (i1.astype(jnp.int32) - N_EXPERT_GROUPS) * (1 << ASSIGN_BITS) + token
    key2 = (i2.astype(jnp.int32) - N_EXPERT_GROUPS) * (1 << ASSIGN_BITS) + token + half
    w1b = lax.bitcast_convert_type(w1, jnp.int32)
    w2b = lax.bitcast_convert_type(w2, jnp.int32)
    words = jnp.where(lane == 0, key1, jnp.where(lane == 1, key2, jnp.where(lane == 2, w1b,
                      jnp.where(lane == 3, w2b, 0))))
    return words, jnp.sum(oh, axis=0, keepdims=True)


def _store_token_tiles(ref, x):
    n = x.shape[0]
    for c in range(D_MODEL // LANES):
        ref[pl.ds(c, n, stride=TILE_ROWS), :] = x[:, c * LANES:(c + 1) * LANES]


def _load_token_tiles(ref, n):
    return jnp.concatenate([ref[pl.ds(c, n, stride=TILE_ROWS), :] for c in range(D_MODEL // LANES)], axis=1)


def _pack_bf16_pairs(x):
    hw = x.shape[1] // 2
    bits = lambda v: lax.bitcast_convert_type(v.astype(BF16).astype(F32), jnp.uint32)
    return (bits(x[:, hw:]) & jnp.uint32(0xFFFF0000)) | (bits(x[:, :hw]) >> 16)


def _unpack_bf16_pairs(w):
    lo = lax.bitcast_convert_type(w << 16, F32)
    hi = lax.bitcast_convert_type(w & jnp.uint32(0xFFFF0000), F32)
    return jnp.concatenate([lo, hi], axis=1).astype(BF16)


def _out_proj_rows(yc_ref, o_ref, sa_ref, sb_ref, x_ref, wc_ref, wa_ref, wo_ref, g2_ref, wr_ref, br_ref,
                   x2_ref, h2p_ref, route_ref, keys_ref, cnt_ref, *, valid_rows, half):
    y_conv = jnp.dot(yc_ref[...], wc_ref[...], preferred_element_type=F32)
    y_attn = jnp.dot(o_ref[...], wa_ref[...], preferred_element_type=F32)
    mix = (sa_ref[...].astype(F32) * y_conv + sb_ref[...].astype(F32) * y_attn).astype(BF16)
    x2 = x_ref[...] + jnp.dot(mix, wo_ref[...], preferred_element_type=F32)
    x2_ref[...] = x2
    h2 = _rms_norm_f32(x2, g2_ref[...])
    h2p_ref[...] = _pack_bf16_pairs(h2)
    logits = jnp.dot(h2.astype(BF16), wr_ref[...], preferred_element_type=F32) + br_ref[...]
    words, cnt = _route_rows(logits, pl.program_id(0) * x_ref.shape[0], valid_rows, half)
    route_ref[...] = words
    keys_ref[...] = words.T[:TILE_ROWS, :]
    cnt_ref[...] += cnt


def _out_proj_kernel(*refs, n_first, valid_rows_second, half):
    first, second, shared = refs[0:5], refs[5:10], refs[10:]
    cnt_ref = shared[-1]
    tm = first[4].shape[0]

    @pl.when(pl.program_id(0) == 0)
    def _():
        cnt_ref[...] = jnp.zeros_like(cnt_ref)

    @pl.when(pl.program_id(0) < n_first)
    def _():
        _out_proj_rows(*first, *shared, valid_rows=tm, half=half)

    @pl.when(pl.program_id(0) >= n_first)
    def _():
        _out_proj_rows(*second, *shared, valid_rows=valid_rows_second, half=half)


def _out_proj(acts_a, acts_b, wc, wa, wo, g2, wr, br, *, tm, valid_rows_b, half):
    na = acts_a[4].shape[0] // tm
    nb = acts_b[4].shape[0] // tm
    assert nb == 1
    m_total = (na + nb) * tm
    spec_a = lambda width: pl.BlockSpec((tm, width), lambda i: (jnp.minimum(i, na - 1), 0))
    spec_b = lambda width: pl.BlockSpec((tm, width), lambda i: (jnp.maximum(i - na, 0), 0))
    widths = (D_CONV, Q_DIM, D_MODEL, D_MODEL, D_MODEL)
    in_specs = [spec_a(w) for w in widths] + [spec_b(w) for w in widths]
    in_specs += [_const_spec(wc.shape), _const_spec(wa.shape), _const_spec(wo.shape),
                 _const_spec(g2.shape), _const_spec(wr.shape), _const_spec(br.shape)]
    orow = lambda width: pl.BlockSpec((tm, width), lambda i: (i, 0))
    return pl.pallas_call(
        functools.partial(_out_proj_kernel, n_first=na, valid_rows_second=valid_rows_b, half=half),
        grid=(na + nb,),
        in_specs=in_specs,
        out_specs=[orow(D_MODEL), orow(D_MODEL // 2), orow(ROUTER_LANES),
                   pl.BlockSpec((TILE_ROWS, tm), lambda i: (0, i)),
                   pl.BlockSpec((1, ROUTER_LANES), lambda i: (0, 0))],
        out_shape=[jax.ShapeDtypeStruct((m_total, D_MODEL), F32),
                   jax.ShapeDtypeStruct((m_total, D_MODEL // 2), jnp.uint32),
                   jax.ShapeDtypeStruct((m_total, ROUTER_LANES), jnp.int32),
                   jax.ShapeDtypeStruct((TILE_ROWS, m_total), jnp.int32),
                   jax.ShapeDtypeStruct((1, ROUTER_LANES), F32)],
        compiler_params=pltpu.CompilerParams(dimension_semantics=("arbitrary",),
                                             vmem_limit_bytes=V7X_VMEM_LIMIT_BYTES),
        name="out_proj",
    )(*acts_a, *acts_b, wc, wa, wo, g2, wr, br)


def _block_table_kernel(counts_ref, bexp_ref, bpos_ref, bcnt_ref, bslot_ref, bnext_ref, nused_ref, first_ref,
                        *, n_blocks):
    def per_expert(e, carry):
        blk0, pos0, ordinal = carry
        cnt = counts_ref[e]
        nblk = lax.shift_right_logical(cnt + (MOE_BLOCK - 1), MOE_BLOCK_LOG2)
        first_ref[e] = jnp.where(nblk > 0, blk0, -1)

        def mark(b, c):
            off = (b - blk0) * MOE_BLOCK
            bexp_ref[b] = e
            bpos_ref[b] = pos0 + off
            bcnt_ref[b] = jnp.minimum(cnt - off, MOE_BLOCK)
            bslot_ref[b] = jnp.where(b == blk0, ordinal & 1, -1)
            bnext_ref[b] = -1
            return c
        lax.fori_loop(blk0, blk0 + nblk, mark, 0)
        return blk0 + nblk, pos0 + cnt, ordinal + jnp.where(nblk > 0, 1, 0)

    n_used, _, _ = lax.fori_loop(0, N_EXPERTS, per_expert, (0, 0, 0))
    nused_ref[0] = n_used

    def unused(b, c):
        bexp_ref[b] = N_EXPERTS - 1
        bpos_ref[b] = 0
        bcnt_ref[b] = 0
        bslot_ref[b] = -1
        bnext_ref[b] = -1
        return c
    lax.fori_loop(n_used, n_blocks, unused, 0)

    def link(k, nxt):
        e = N_EXPERTS - 1 - k
        fb = first_ref[e]

        @pl.when(fb >= 0)
        def _():
            bnext_ref[fb] = nxt
        return jnp.where(fb >= 0, e, nxt)
    lax.fori_loop(0, N_EXPERTS, link, -1)


def _block_tables(counts, n_blocks):
    smem = pl.BlockSpec(memory_space=pltpu.SMEM)
    blk = jax.ShapeDtypeStruct((n_blocks,), jnp.int32)
    return pl.pallas_call(
        functools.partial(_block_table_kernel, n_blocks=n_blocks),
        in_specs=[smem],
        out_specs=[smem] * 6,
        out_shape=[blk] * 5 + [jax.ShapeDtypeStruct((1,), jnp.int32)],
        scratch_shapes=[pltpu.SMEM((N_EXPERTS,), jnp.int32)],
        name="block_tables",
    )(counts)


def _moe_kernel(tok_ref, order_ref, bexp_ref, bpos_ref, bcnt_ref, bslot_ref, bnext_ref, nused_ref,
                h2p_ref, wg_hbm, wu_hbm, wd_hbm, contrib_hbm,
                xs_ref, ys_ref, wgf_ref, wuf_ref, wdf_ref, wgb_ref, wub_ref, wdb_ref, ssem_ref, wsem_ref,
                *, m_total, half, n_blocks):
    step = pl.program_id(0)
    n_used = nused_ref[0]
    tile = lambda t: pl.ds(pl.multiple_of(t * TILE_ROWS, TILE_ROWS), TILE_ROWS)

    def gather(pos0, slt):
        for r in range(MOE_BLOCK):
            xs_ref[slt, pl.ds(r, 1), :] = h2p_ref[pl.ds(tok_ref[pos0 + r], 1), :]

    def scatter(pos0, cnt, trash0, slt, r):
        dst = jnp.where(r < cnt, order_ref[pos0 + r], trash0 + r)
        return pltpu.make_async_copy(ys_ref.at[slt, tile(r), :], contrib_hbm.at[tile(dst), :], ssem_ref.at[slt])

    def scatter_wait(slt, r):
        pltpu.make_async_copy(ys_ref.at[slt, tile(r), :], contrib_hbm.at[tile(0), :], ssem_ref.at[slt]).wait()

    @pl.when(step == 0)
    def _():
        ys_ref[...] = jnp.zeros_like(ys_ref)
        gap = half - m_total
        fills = [(2 * half + s * MOE_BLOCK, MOE_BLOCK) for s in range(2)]
        fills += [(k * half + m_total, gap) for k in range(TOP_K)] if gap else []
        for start, n in fills:
            fill = pltpu.make_async_copy(ys_ref.at[0, pl.ds(0, n * TILE_ROWS), :],
                                         contrib_hbm.at[pl.ds(start * TILE_ROWS, n * TILE_ROWS), :], ssem_ref.at[0])
            fill.start()
            fill.wait()

    def weight_copies(e, s):
        return [pltpu.make_async_copy(src.at[e], dst.at[s], wsem_ref.at[s])
                for src, dst in ((wg_hbm, wgf_ref), (wu_hbm, wuf_ref), (wd_hbm, wdf_ref))]

    @pl.when(step == 0)
    def _():
        for c in weight_copies(bexp_ref[0], 0):
            c.start()
        gather(bpos_ref[0], 0)

    def wait_scatters(slt):
        for r in range(MOE_BLOCK):
            scatter_wait(slt, r)

    def expert_weights(i):
        wslot = bslot_ref[i]

        @pl.when(wslot >= 0)
        def _():
            for c in weight_copies(0, wslot):
                c.wait()

        @pl.when(jnp.logical_and(wslot >= 0, bnext_ref[i] >= 0))
        def _():
            for c in weight_copies(bnext_ref[i], 1 - wslot):
                c.start()

        @pl.when(wslot >= 0)
        def _():
            wgb_ref[...] = wgf_ref[wslot].astype(BF16)
            wub_ref[...] = wuf_ref[wslot].astype(BF16)
            wdb_ref[...] = wdf_ref[wslot].astype(BF16)

    def run_block(i, slt):
        pos_next = bpos_ref[jnp.minimum(i + 1, n_blocks - 1)]
        pos0 = bpos_ref[i]
        cnt = bcnt_ref[i]
        trash0 = 2 * half + slt * MOE_BLOCK
        gather(pos_next, 1 - slt)
        xb = _unpack_bf16_pairs(xs_ref[slt])
        gate = jnp.dot(xb, wgb_ref[...], preferred_element_type=F32)
        up = jnp.dot(xb, wub_ref[...], preferred_element_type=F32)
        hmid = (jax.nn.silu(gate) * up).astype(BF16)
        _store_token_tiles(ys_ref.at[slt], jnp.dot(hmid, wdb_ref[...], preferred_element_type=F32))
        for r in range(MOE_BLOCK):
            scatter(pos0, cnt, trash0, slt, r).start(priority=r % 2)

    for slt in range(2):
        i = 2 * step + slt
        active = i < n_used
        last_active = i == n_used - 1
        pl.when(jnp.logical_and(active, i >= 2))(functools.partial(wait_scatters, slt))
        expert_weights(i)
        pl.when(active)(functools.partial(run_block, i, slt))
        pl.when(last_active)(functools.partial(wait_scatters, slt))
        pl.when(jnp.logical_and(last_active, i >= 1))(functools.partial(wait_scatters, 1 - slt))


def _moe_experts(tok, order, tables, h2p, w_gate, w_up, w_down, *, m_total, half):
    n_blocks = tables[0].shape[0]
    assert 0 <= half - m_total <= MOE_BLOCK and n_blocks % 2 == 0
    hbm = pl.BlockSpec(memory_space=pl.ANY)
    grid_spec = pltpu.PrefetchScalarGridSpec(
        num_scalar_prefetch=2 + len(tables),
        grid=(n_blocks // 2,),
        in_specs=[_const_spec(h2p.shape), hbm, hbm, hbm],
        out_specs=hbm,
        scratch_shapes=[pltpu.VMEM((2, MOE_BLOCK, D_MODEL // 2), jnp.uint32),
                        pltpu.VMEM((2, MOE_BLOCK * TILE_ROWS, LANES), F32),
                        pltpu.VMEM((2, D_MODEL, D_EXPERT), F32),
                        pltpu.VMEM((2, D_MODEL, D_EXPERT), F32),
                        pltpu.VMEM((2, D_EXPERT, D_MODEL), F32),
                        pltpu.VMEM((D_MODEL, D_EXPERT), BF16),
                        pltpu.VMEM((D_MODEL, D_EXPERT), BF16),
                        pltpu.VMEM((D_EXPERT, D_MODEL), BF16),
                        pltpu.SemaphoreType.DMA((2,)),
                        pltpu.SemaphoreType.DMA((2,))],
    )
    return pl.pallas_call(
        functools.partial(_moe_kernel, m_total=m_total, half=half, n_blocks=n_blocks),
        grid_spec=grid_spec,
        out_shape=jax.ShapeDtypeStruct(((2 * half + 2 * MOE_BLOCK) * TILE_ROWS, LANES), F32),
        compiler_params=pltpu.CompilerParams(dimension_semantics=("arbitrary",),
                                             vmem_limit_bytes=MOE_VMEM_LIMIT_BYTES),
        name="moe_experts",
    )(tok, order, *tables, h2p, w_gate, w_up, w_down)


def _combine_kernel(c0_ref, c1_ref, x2_ref, route_ref, gf_ref, y_ref):
    tc = x2_ref.shape[0]
    w = lax.bitcast_convert_type(route_ref[:, TOP_K:2 * TOP_K], F32)
    moe = w[:, 0:1] * _load_token_tiles(c0_ref, tc) + w[:, 1:2] * _load_token_tiles(c1_ref, tc)
    y_ref[...] = _rms_norm_f32(x2_ref[...] + moe, gf_ref[...])


def _moe_combine(contrib, x2, route, gf, *, row_off, m, tc, half):
    off = row_off // tc
    assert row_off % tc == 0 and half % tc == 0
    ctile = lambda k: pl.BlockSpec((tc * TILE_ROWS, LANES), lambda i: (i + off + k * (half // tc), 0))
    return pl.pallas_call(
        _combine_kernel,
        grid=(m // tc,),
        in_specs=[ctile(0), ctile(1),
                  pl.BlockSpec((tc, D_MODEL), lambda i: (i + off, 0)),
                  pl.BlockSpec((tc, ROUTER_LANES), lambda i: (i + off, 0)),
                  _const_spec((1, D_MODEL))],
        out_specs=pl.BlockSpec((tc, D_MODEL), lambda i: (i, 0)),
        out_shape=jax.ShapeDtypeStruct((m, D_MODEL), F32),
        compiler_params=pltpu.CompilerParams(dimension_semantics=("arbitrary",),
                                             vmem_limit_bytes=V7X_VMEM_LIMIT_BYTES),
        name="moe_combine_prompt" if row_off == 0 else "moe_combine_sample",
    )(contrib, contrib, x2, route, gf)


def _t5_bucket(dist):
    n = jnp.maximum(dist, 0)
    max_exact = N_BUCKETS // 2
    nf = jnp.maximum(n, 1).astype(F32)
    large = max_exact + (jnp.log(nf / max_exact) / math.log(MAX_DISTANCE / max_exact)
                         * (N_BUCKETS - max_exact)).astype(jnp.int32)
    large = jnp.minimum(large, N_BUCKETS - 1)
    return jnp.where(n < max_exact, n, large)


def _bucket_bias(rel_bias, dist, valid):
    buckets = _t5_bucket(dist).reshape(1, -1)
    onehot = (buckets == jnp.arange(N_BUCKETS, dtype=jnp.int32)[:, None]).astype(F32)
    bias = jnp.dot(rel_bias.astype(F32).T, onehot, precision=lax.Precision.HIGHEST)
    return jnp.where(valid.reshape(1, -1), bias, NEG_BIG).reshape((rel_bias.shape[1],) + dist.shape)


def _prompt_bias_table(rel_bias):
    qi = jnp.arange(ATTN_BLOCK, dtype=jnp.int32)[:, None]
    kj = jnp.arange(2 * ATTN_BLOCK, dtype=jnp.int32)[None, :] - ATTN_BLOCK
    dist = qi - kj
    return _bucket_bias(rel_bias, dist, (dist >= 0) & (dist <= WINDOW))


def _sample_bias_table(rel_bias, w_buf):
    dist = w_buf - jnp.arange(w_buf + 1, dtype=jnp.int32)
    return _bucket_bias(rel_bias, dist, dist <= WINDOW)


def kernel(x_prompt, x_sample, cache_conv, cache_k, cache_v, norm1_g, w_in, conv_w, w_conv_out, w_attn_out, w_o, sinks, rel_bias, norm2_g, w_router_group, b_router_group, w_router_expert, b_router_expert, w_e_gate, w_e_up, w_e_down, norm_f_g):
    assert norm1_g.shape[0] == 1, "single-layer configuration"
    batch, seq, _ = x_prompt.shape
    nseq = x_sample.shape[0]
    w_buf = cache_k.shape[2]
    mp = batch * seq
    m_total = mp + nseq
    assert seq % TM_DENSE == 0 and seq % TM_IN_PROJ == 0 and seq % ATTN_BLOCK == 0 and mp % COMBINE_BLOCK == 0
    assert nseq % SAMPLE_SEQ_PER_STEP == 0 and mp % nseq == 0
    assert TOP_K == 2 and MOE_BLOCK == 1 << MOE_BLOCK_LOG2 and m_total * TOP_K < 1 << ASSIGN_BITS

    g1 = norm1_g[0][None, :]
    g2 = norm2_g[0][None, :]
    gf = norm_f_g[None, :]
    wi = w_in[0].astype(BF16)
    cw = conv_w[0]
    wc = w_conv_out[0].astype(BF16)
    wa = w_attn_out[0].astype(BF16)
    wo = w_o[0].astype(BF16)
    pad_cols = ROUTER_LANES - N_EXPERT_GROUPS - N_EXPERTS
    wr = jnp.concatenate([w_router_group[0], w_router_expert[0],
                          jnp.zeros((D_MODEL, pad_cols), F32)], axis=1).astype(BF16)
    br = jnp.concatenate([b_router_group[0], b_router_expert[0], jnp.zeros((pad_cols,), F32)])[None, :]
    sink = sinks[0].astype(F32)

    xp = x_prompt.reshape(mp, D_MODEL)
    bps = seq // TM_IN_PROJ
    yc_p, q_p, k_p, v_p, sa_p, sb_p, ut_p, kvt_p = _in_proj(
        xp, g1, wi, cw, tm=TM_IN_PROJ, blocks_per_seq=bps, u_tail=8, kv_tail=WINDOW)
    o_p = _attn_prompt(q_p, k_p, v_p, _prompt_bias_table(rel_bias), sink, batch, seq)

    pad_rows = lambda t: jnp.pad(t, ((0, TM_DENSE - nseq), (0, 0)))
    xs = pad_rows(x_sample.reshape(nseq, D_MODEL))
    hist = (pad_rows(cache_conv[0][:, 0, :]), pad_rows(cache_conv[0][:, 1, :]))
    yc_s, q_s, _, _, sa_s, sb_s, ut_s, kvt_s = _in_proj(
        xs, g1, wi, cw, tm=TM_DENSE, blocks_per_seq=1, u_tail=TM_DENSE, kv_tail=TM_DENSE, hist=hist,
        gate_dtype=F32)
    u_s = ut_s[0, :nseq]
    kv_s = kvt_s[0, :nseq]
    head_mask = (jnp.arange(KV_DIM)[None, :] // HEAD_DIM == jnp.arange(N_HEADS)[:, None] // GROUP)
    qbd = (jnp.tile(q_s[:nseq].reshape(nseq, N_HEADS, HEAD_DIM), (1, 1, N_KV_HEADS))
           * head_mask[None].astype(BF16))
    to_keys_minor = lambda c: jnp.transpose(c.reshape(nseq, w_buf, KV_DIM), (0, 2, 1))
    from_keys_minor = lambda c: jnp.transpose(c, (0, 2, 1)).reshape(1, nseq, w_buf, N_KV_HEADS, HEAD_DIM)
    o_s, kwin_s, vwin_s = _attn_sample(qbd, to_keys_minor(cache_k[0]), to_keys_minor(cache_v[0]), kv_s,
                                       _sample_bias_table(rel_bias, w_buf), sink[:, None],
                                       head_mask.astype(F32))
    o_s = pad_rows(o_s.reshape(nseq, Q_DIM))

    half = -(-m_total // COMBINE_BLOCK) * COMBINE_BLOCK
    assert half % nseq == 0 and TOP_K * half < 1 << ASSIGN_BITS
    x2, h2p, route, route_t, cnt = _out_proj((yc_p, o_p, sa_p, sb_p, xp), (yc_s, o_s, sa_s, sb_s, xs),
                                             wc, wa, wo, g2, wr, br, tm=TM_DENSE, valid_rows_b=nseq, half=half)

    n_assign = m_total * TOP_K
    keys = route_t[0:TOP_K, :m_total].reshape(-1)
    counts = cnt[0, N_EXPERT_GROUPS:N_EXPERT_GROUPS + N_EXPERTS].astype(jnp.int32)
    order = jnp.pad(jnp.sort(keys) & ((1 << ASSIGN_BITS) - 1), (0, MOE_BLOCK))
    tok = jnp.where(order >= half, order - half, order)
    n_blocks = -(-n_assign // MOE_BLOCK) + N_EXPERTS
    n_blocks += n_blocks % 2
    tables = _block_tables(counts, n_blocks)
    contrib = _moe_experts(tok, order, tables, h2p, w_e_gate[0], w_e_up[0], w_e_down[0],
                           m_total=m_total, half=half)
    y_p = _moe_combine(contrib, x2, route, gf, row_off=0, m=mp, tc=COMBINE_BLOCK, half=half)
    y_s = _moe_combine(contrib, x2, route, gf, row_off=mp, m=nseq, tc=nseq, half=half)

    y_prompt = y_p.reshape(batch, seq, D_MODEL)
    y_sample = y_s.reshape(nseq, 1, D_MODEL)
    conv_state_prompt = ut_p.reshape(batch, bps, 8, D_CONV)[:, -1, 8 - (CONV_WIDTH - 1):, :][None]
    kv_last = kvt_p.reshape(batch, bps, WINDOW, 2 * KV_DIM)[:, -1]
    k_win_prompt = kv_last[:, :, :KV_DIM].reshape(batch, WINDOW, N_KV_HEADS, HEAD_DIM)[None]
    v_win_prompt = kv_last[:, :, KV_DIM:].reshape(batch, WINDOW, N_KV_HEADS, HEAD_DIM)[None]
    conv_state_sample = jnp.concatenate([cache_conv[0][:, 1:, :], u_s[:, None, :]], axis=1)[None]
    k_win_sample = from_keys_minor(kwin_s)
    v_win_sample = from_keys_minor(vwin_s)
    return (y_prompt, y_sample, conv_state_prompt, k_win_prompt, v_win_prompt,
            conv_state_sample, k_win_sample, v_win_sample)
```

```python
import functools
import math

import jax
import jax.numpy as jnp
from jax import lax
from jax.experimental import pallas as pl
from jax.experimental.pallas import tpu as pltpu

D_MODEL = 1024
D_CONV = 1024
CONV_WIDTH = 3
N_HEADS = 16
N_KV_HEADS = 4
HEAD_DIM = 64
GROUP = N_HEADS // N_KV_HEADS
WINDOW = 128
Q_DIM = N_HEADS * HEAD_DIM
KV_DIM = N_KV_HEADS * HEAD_DIM
N_BUCKETS = 32
MAX_DISTANCE = 128
N_EXPERT_GROUPS = 4
EXPERTS_PER_GROUP = 8
N_EXPERTS = N_EXPERT_GROUPS * EXPERTS_PER_GROUP
TOP_K = 2
D_EXPERT = 512
EPS = 1e-6
PAST_LEN = 8192

BF16 = jnp.bfloat16
F32 = jnp.float32
NEG_BIG = -1e30

V7X_VMEM_LIMIT_BYTES = 56 * 1024 * 1024
MOE_VMEM_LIMIT_BYTES = 62 * 1024 * 1024
TILE_ROWS = 8
LANES = 128
ROUTER_LANES = 128
TM_DENSE = 512
TM_IN_PROJ = 1024
OUT_PROJ_PARTS = 2
ATTN_BLOCK = 128
MOE_BLOCK = 256
MOE_BLOCK_LOG2 = 8
ASSIGN_BITS = 16
COMBINE_BLOCK = 256
SAMPLE_SEQ_PER_STEP = 16
HEADS_PER_STORE = LANES // HEAD_DIM


def _const_spec(shape):
    nd = len(shape)
    return pl.BlockSpec(shape, lambda *_: (0,) * nd, pipeline_mode=pl.Buffered(1))


def _rms_norm_f32(xf, g):
    return xf * lax.rsqrt(jnp.mean(xf * xf, axis=-1, keepdims=True) + EPS) * g


def _in_proj_kernel(*refs, tm, sample, blocks_per_seq, u_tail, kv_tail):
    if sample:
        (x_ref, hist0_ref, hist1_ref, g_ref, w_ref,
         cw_ref, yc_ref, q_ref, k_ref, v_ref, sa_ref, sb_ref, ut_ref, kvt_ref) = refs
    else:
        (x_ref, g_ref, w_ref,
         cw_ref, yc_ref, q_ref, k_ref, v_ref, sa_ref, sb_ref, ut_ref, kvt_ref, ubuf_ref) = refs

    h = _rms_norm_f32(x_ref[...], g_ref[...]).astype(BF16)

    widths = (D_CONV, D_CONV, D_CONV, Q_DIM, 2 * KV_DIM, D_MODEL, D_MODEL)
    starts = [sum(widths[:n]) for n in range(len(widths))]
    wcb_ref, wcc_ref, wch_ref, wq_ref, wkv_ref, wga_ref, wgb_ref = [
        w_ref.at[:, pl.ds(a, n)] for a, n in zip(starts, widths)]

    def proj(part_ref):
        return jnp.dot(h, part_ref[...], preferred_element_type=F32)

    u = proj(wcc_ref) * proj(wch_ref)
    w0 = cw_ref[0:1, :]
    w1 = cw_ref[1:2, :]
    w2 = cw_ref[2:3, :]
    if sample:
        conv = w0 * hist0_ref[...] + w1 * hist1_ref[...] + w2 * u
    else:
        @pl.when(pl.program_id(0) % blocks_per_seq == 0)
        def _():
            ubuf_ref[0:8, :] = jnp.zeros((8, D_CONV), F32)

        ubuf_ref[8:8 + tm, :] = u
        conv = w0 * ubuf_ref[6:6 + tm, :] + w1 * ubuf_ref[7:7 + tm, :] + w2 * u
        ubuf_ref[0:8, :] = u[tm - 8:, :]
    yc_ref[...] = (proj(wcb_ref) * conv).astype(BF16)
    ut_ref[0] = u[tm - u_tail:, :]

    q_ref[...] = (proj(wq_ref) * (HEAD_DIM ** -0.5)).astype(BF16)
    kv = proj(wkv_ref)
    k_ref[...] = kv[:, :KV_DIM].astype(BF16)
    v_ref[...] = kv[:, KV_DIM:].astype(BF16)
    kvt_ref[0] = kv[tm - kv_tail:, :]
    sa_ref[...] = jax.nn.sigmoid(proj(wga_ref)).astype(sa_ref.dtype)
    sb_ref[...] = jax.nn.sigmoid(proj(wgb_ref)).astype(sb_ref.dtype)


def _in_proj(x, g1, w_in, conv_w, *, tm, blocks_per_seq, u_tail, kv_tail, hist=None, gate_dtype=BF16):
    m = x.shape[0]
    nblk = m // tm
    sample = hist is not None
    row = lambda width: pl.BlockSpec((tm, width), lambda i: (i, 0))
    in_specs = [row(D_MODEL)]
    args = [x]
    if sample:
        in_specs += [row(D_CONV), row(D_CONV)]
        args += list(hist)
    in_specs += [_const_spec((1, D_MODEL)), _const_spec(w_in.shape), _const_spec(conv_w.shape)]
    args += [g1, w_in, conv_w]
    out_shape = [
        jax.ShapeDtypeStruct((m, D_CONV), BF16),
        jax.ShapeDtypeStruct((m, Q_DIM), BF16),
        jax.ShapeDtypeStruct((m, KV_DIM), BF16),
        jax.ShapeDtypeStruct((m, KV_DIM), BF16),
        jax.ShapeDtypeStruct((m, D_MODEL), gate_dtype),
        jax.ShapeDtypeStruct((m, D_MODEL), gate_dtype),
        jax.ShapeDtypeStruct((nblk, u_tail, D_CONV), F32),
        jax.ShapeDtypeStruct((nblk, kv_tail, 2 * KV_DIM), F32),
    ]
    out_specs = [row(D_CONV), row(Q_DIM), row(KV_DIM), row(KV_DIM), row(D_MODEL), row(D_MODEL),
                 pl.BlockSpec((1, u_tail, D_CONV), lambda i: (i, 0, 0)),
                 pl.BlockSpec((1, kv_tail, 2 * KV_DIM), lambda i: (i, 0, 0))]
    scratch = [] if sample else [pltpu.VMEM((tm + 8, D_CONV), F32)]
    return pl.pallas_call(
        functools.partial(_in_proj_kernel, tm=tm, sample=sample, blocks_per_seq=blocks_per_seq,
                          u_tail=u_tail, kv_tail=kv_tail),
        grid=(nblk,),
        in_specs=in_specs,
        out_specs=out_specs,
        out_shape=out_shape,
        scratch_shapes=scratch,
        compiler_params=pltpu.CompilerParams(dimension_semantics=("arbitrary",),
                                             vmem_limit_bytes=MOE_VMEM_LIMIT_BYTES),
        name="in_proj_sample" if sample else "in_proj_prompt",
    )(*args)


def _attn_prompt_kernel(sink_ref, q_ref, kc_ref, kp_ref, vc_ref, vp_ref, bias_ref, o_ref):
    first = pl.program_id(1) == 0
    col = lax.broadcasted_iota(jnp.int32, (ATTN_BLOCK, 2 * ATTN_BLOCK), 1)
    no_prev = jnp.logical_and(first, col < ATTN_BLOCK)
    for g in range(N_KV_HEADS):
        ks = slice(g * HEAD_DIM, (g + 1) * HEAD_DIM)
        kcat = jnp.concatenate([kp_ref[:, ks], kc_ref[:, ks]], axis=0)
        vcat = jnp.concatenate([vp_ref[:, ks], vc_ref[:, ks]], axis=0)
        for h0 in range(g * GROUP, (g + 1) * GROUP, HEADS_PER_STORE):
            outs = []
            for h in range(h0, h0 + HEADS_PER_STORE):
                hs = slice(h * HEAD_DIM, (h + 1) * HEAD_DIM)
                s = lax.dot_general(q_ref[:, hs], kcat, (((1,), (1,)), ((), ())),
                                    preferred_element_type=F32)
                s = jnp.where(no_prev, NEG_BIG, s + bias_ref[h])
                sink = sink_ref[h]
                m = jnp.maximum(jnp.max(s, axis=-1, keepdims=True), sink)
                p = jnp.exp(s - m)
                denom = jnp.sum(p, axis=-1, keepdims=True) + jnp.exp(sink - m)
                o = jnp.dot(p.astype(BF16), vcat, preferred_element_type=F32)
                outs.append((o / denom).astype(BF16))
            o_ref[:, h0 * HEAD_DIM:(h0 + HEADS_PER_STORE) * HEAD_DIM] = jnp.concatenate(outs, axis=1)


def _attn_prompt(q, k, v, bias, sinks, batch, seq):
    nb = seq // ATTN_BLOCK
    cur = lambda b, i: (b * nb + i, 0)
    prev = lambda b, i: (b * nb + jnp.maximum(i - 1, 0), 0)
    return pl.pallas_call(
        _attn_prompt_kernel,
        grid=(batch, nb),
        in_specs=[pl.BlockSpec(memory_space=pltpu.SMEM),
                  pl.BlockSpec((ATTN_BLOCK, Q_DIM), cur),
                  pl.BlockSpec((ATTN_BLOCK, KV_DIM), cur),
                  pl.BlockSpec((ATTN_BLOCK, KV_DIM), prev),
                  pl.BlockSpec((ATTN_BLOCK, KV_DIM), cur),
                  pl.BlockSpec((ATTN_BLOCK, KV_DIM), prev),
                  _const_spec(bias.shape)],
        out_specs=pl.BlockSpec((ATTN_BLOCK, Q_DIM), cur),
        out_shape=jax.ShapeDtypeStruct((batch * seq, Q_DIM), BF16),
        compiler_params=pltpu.CompilerParams(dimension_semantics=("arbitrary", "arbitrary"),
                                             vmem_limit_bytes=V7X_VMEM_LIMIT_BYTES),
        name="attn_prompt",
    )(sinks, q, k, k, v, v, bias)


def _attn_sample_kernel(qbd_ref, ckt_ref, cvt_ref, kvn_ref, knt_ref, vnt_ref, bias_ref, sink_ref, mask_ref,
                        o_ref, kwin_ref, vwin_ref, *, w_buf):
    bf16_round = lambda t: t.astype(BF16).astype(F32)
    seqs = range(SAMPLE_SEQ_PER_STEP)
    sink = sink_ref[...]
    newest = lax.broadcasted_iota(jnp.int32, (KV_DIM, w_buf), 1) == w_buf - 1
    for b in seqs:
        kwin_ref[b] = jnp.where(newest, knt_ref[b], pltpu.roll(ckt_ref[b], w_buf - 1, axis=1))
        vwin_ref[b] = jnp.where(newest, vnt_ref[b], pltpu.roll(cvt_ref[b], w_buf - 1, axis=1))
    s = [jnp.dot(qbd_ref[b], ckt_ref[b].astype(BF16), preferred_element_type=F32) + bias_ref[:, :w_buf]
         for b in seqs]
    s_new = [jnp.sum(qbd_ref[b].astype(F32) * bf16_round(kvn_ref[b:b + 1, :KV_DIM]), axis=-1, keepdims=True)
             + bias_ref[:, w_buf:w_buf + 1] for b in seqs]
    m = [jnp.maximum(jnp.maximum(jnp.max(s[b], axis=-1, keepdims=True), s_new[b]), sink) for b in seqs]
    p = [jnp.exp(s[b] - m[b]) for b in seqs]
    p_new = [jnp.exp(s_new[b] - m[b]) for b in seqs]
    denom = [jnp.sum(p[b], axis=-1, keepdims=True) + p_new[b] + jnp.exp(sink - m[b]) for b in seqs]
    of = [lax.dot_general((p[b] / denom[b]).astype(BF16), cvt_ref[b].astype(BF16), (((1,), (1,)), ((), ())),
                          preferred_element_type=F32)
          + bf16_round(p_new[b] / denom[b]) * bf16_round(kvn_ref[b:b + 1, KV_DIM:]) for b in seqs]
    for b in seqs:
        ob = of[b] * mask_ref[...]
        o_ref[b] = (ob[:, 0:HEAD_DIM] + ob[:, HEAD_DIM:2 * HEAD_DIM]
                    + ob[:, 2 * HEAD_DIM:3 * HEAD_DIM] + ob[:, 3 * HEAD_DIM:]).astype(BF16)


def _attn_sample(qbd, ckt, cvt, kvn, bias, sink_col, head_mask):
    nseq, w_buf = ckt.shape[0], ckt.shape[2]
    sb = SAMPLE_SEQ_PER_STEP
    seq3 = lambda d1, d2: pl.BlockSpec((sb, d1, d2), lambda i: (i, 0, 0))
    win = jax.ShapeDtypeStruct((nseq, KV_DIM, w_buf), F32)
    return pl.pallas_call(
        functools.partial(_attn_sample_kernel, w_buf=w_buf),
        grid=(nseq // sb,),
        in_specs=[seq3(N_HEADS, KV_DIM), seq3(KV_DIM, w_buf), seq3(KV_DIM, w_buf),
                  pl.BlockSpec((sb, 2 * KV_DIM), lambda i: (i, 0)), seq3(KV_DIM, 1), seq3(KV_DIM, 1),
                  _const_spec(bias.shape), _const_spec(sink_col.shape), _const_spec(head_mask.shape)],
        out_specs=[seq3(N_HEADS, HEAD_DIM), seq3(KV_DIM, w_buf), seq3(KV_DIM, w_buf)],
        out_shape=[jax.ShapeDtypeStruct((nseq, N_HEADS, HEAD_DIM), BF16), win, win],
        compiler_params=pltpu.CompilerParams(dimension_semantics=("arbitrary",),
                                             vmem_limit_bytes=V7X_VMEM_LIMIT_BYTES),
        name="attn_sample",
    )(qbd, ckt, cvt, kvn, kvn[:, :KV_DIM, None], kvn[:, KV_DIM:, None], bias, sink_col, head_mask)


def _route_rows(logits, row0, valid_rows, half):
    tm = logits.shape[0]
    lane = lax.broadcasted_iota(jnp.int32, logits.shape, 1)
    lane_f = lane.astype(F32)
    no_lane = float(ROUTER_LANES)

    def top1(mask):
        best = jnp.max(jnp.where(mask, logits, -jnp.inf), axis=-1, keepdims=True)
        idx = jnp.min(jnp.where(jnp.logical_and(mask, logits == best), lane_f, no_lane), axis=-1, keepdims=True)
        return best, idx

    gmask = lane < N_EXPERT_GROUPS
    gmax, grp = top1(gmask)
    gsum = jnp.sum(jnp.where(gmask, jnp.exp(logits - gmax), 0.0), axis=-1, keepdims=True)
    p_grp = 1.0 / gsum
    lo = N_EXPERT_GROUPS + EXPERTS_PER_GROUP * grp
    emask = jnp.logical_and(lane_f >= lo, lane_f < lo + EXPERTS_PER_GROUP)
    v1, i1 = top1(emask)
    v2, i2 = top1(jnp.logical_and(emask, lane_f != i1))
    e21 = jnp.exp(v2 - v1)
    w1 = p_grp / (1.0 + e21)
    w2 = p_grp * e21 / (1.0 + e21)

    oh1 = lane_f == i1
    oh2 = lane_f == i2
    if valid_rows < tm:
        valid = lax.broadcasted_iota(jnp.int32, logits.shape, 0) < valid_rows
        oh1 = jnp.logical_and(oh1, valid)
        oh2 = jnp.logical_and(oh2, valid)
    oh = oh1.astype(F32) + oh2.astype(F32)
    token = row0 + lax.broadcasted_iota(jnp.int32, (tm, 1), 0)
    key1 = (i1.astype(jnp.int32) - N_EXPERT_GROUPS) * (1 << ASSIGN_BITS) + token
    key2 = (i2.astype(jnp.int32) - N_EXPERT_GROUPS) * (1 << ASSIGN_BITS) + token + half
    w1b = lax.bitcast_convert_type(w1, jnp.int32)
    w2b = lax.bitcast_convert_type(w2, jnp.int32)
    words = jnp.where(lane == 0, key1, jnp.where(lane == 1, key2, jnp.where(lane == 2, w1b,
                      jnp.where(lane == 3, w2b, 0))))
    return words, jnp.sum(oh, axis=0, keepdims=True)


def _store_token_tiles(ref, x):
    n = x.shape[0]
    for c in range(D_MODEL // LANES):
        ref[pl.ds(c, n, stride=TILE_ROWS), :] = x[:, c * LANES:(c + 1) * LANES]


def _load_token_tiles(ref, n):
    return jnp.concatenate([ref[pl.ds(c, n, stride=TILE_ROWS), :] for c in range(D_MODEL // LANES)], axis=1)


def _pack_bf16_pairs(x):
    hw = x.shape[1] // 2
    bits = lambda v: lax.bitcast_convert_type(v.astype(BF16).astype(F32), jnp.uint32)
    return (bits(x[:, hw:]) & jnp.uint32(0xFFFF0000)) | (bits(x[:, :hw]) >> 16)


def _unpack_bf16_pairs(w):
    lo = lax.bitcast_convert_type(w << 16, F32)
    hi = lax.bitcast_convert_type(w & jnp.uint32(0xFFFF0000), F32)
    return jnp.concatenate([lo, hi], axis=1).astype(BF16)


def _out_proj_rows(yc_ref, o_ref, sa_ref, sb_ref, x_ref, wc_ref, wa_ref, wo_ref, g2_ref, wr_ref, br_ref,
                   x2_ref, h2p_ref, route_ref, keys_ref, cnt_ref, *, valid_rows, half):
    tm = x_ref.shape[0]
    parts = [(r0, tm // OUT_PROJ_PARTS) for r0 in range(0, tm, tm // OUT_PROJ_PARTS)]
    for r0, n in parts:
        rows = pl.ds(r0, n)
        y_conv = jnp.dot(yc_ref[rows, :], wc_ref[...], preferred_element_type=F32)
        y_attn = jnp.dot(o_ref[rows, :], wa_ref[...], preferred_element_type=F32)
        mix = (sa_ref[rows, :].astype(F32) * y_conv + sb_ref[rows, :].astype(F32) * y_attn).astype(BF16)
        x2_ref[rows, :] = x_ref[rows, :] + jnp.dot(mix, wo_ref[...], preferred_element_type=F32)
    for r0, n in parts:
        rows = pl.ds(r0, n)
        h2 = _rms_norm_f32(x2_ref[rows, :], g2_ref[...])
        h2p_ref[rows, :] = _pack_bf16_pairs(h2)
        logits = jnp.dot(h2.astype(BF16), wr_ref[...], preferred_element_type=F32) + br_ref[...]
        words, cnt = _route_rows(logits, pl.program_id(0) * tm + r0, min(max(valid_rows - r0, 0), n), half)
        route_ref[rows, :] = words
        keys_ref[:, rows] = words.T[:TILE_ROWS, :]
        cnt_ref[...] += cnt


def _out_proj_kernel(*refs, n_first, valid_rows_second, half):
    first, second, shared = refs[0:5], refs[5:10], refs[10:]
    cnt_ref = shared[-1]
    tm = first[4].shape[0]

    @pl.when(pl.program_id(0) == 0)
    def _():
        cnt_ref[...] = jnp.zeros_like(cnt_ref)

    @pl.when(pl.program_id(0) < n_first)
    def _():
        _out_proj_rows(*first, *shared, valid_rows=tm, half=half)

    @pl.when(pl.program_id(0) >= n_first)
    def _():
        _out_proj_rows(*second, *shared, valid_rows=valid_rows_second, half=half)


def _out_proj(acts_a, acts_b, wc, wa, wo, g2, wr, br, *, tm, valid_rows_b, half):
    na = acts_a[4].shape[0] // tm
    nb = acts_b[4].shape[0] // tm
    assert nb == 1
    m_total = (na + nb) * tm
    spec_a = lambda width: pl.BlockSpec((tm, width), lambda i: (jnp.minimum(i, na - 1), 0))
    spec_b = lambda width: pl.BlockSpec((tm, width), lambda i: (jnp.maximum(i - na, 0), 0))
    widths = (D_CONV, Q_DIM, D_MODEL, D_MODEL, D_MODEL)
    in_specs = [spec_a(w) for w in widths] + [spec_b(w) for w in widths]
    in_specs += [_const_spec(wc.shape), _const_spec(wa.shape), _const_spec(wo.shape),
                 _const_spec(g2.shape), _const_spec(wr.shape), _const_spec(br.shape)]
    orow = lambda width: pl.BlockSpec((tm, width), lambda i: (i, 0))
    return pl.pallas_call(
        functools.partial(_out_proj_kernel, n_first=na, valid_rows_second=valid_rows_b, half=half),
        grid=(na + nb,),
        in_specs=in_specs,
        out_specs=[orow(D_MODEL), orow(D_MODEL // 2), orow(ROUTER_LANES),
                   pl.BlockSpec((TILE_ROWS, tm), lambda i: (0, i)),
                   pl.BlockSpec((1, ROUTER_LANES), lambda i: (0, 0))],
        out_shape=[jax.ShapeDtypeStruct((m_total, D_MODEL), F32),
                   jax.ShapeDtypeStruct((m_total, D_MODEL // 2), jnp.uint32),
                   jax.ShapeDtypeStruct((m_total, ROUTER_LANES), jnp.int32),
                   jax.ShapeDtypeStruct((TILE_ROWS, m_total), jnp.int32),
                   jax.ShapeDtypeStruct((1, ROUTER_LANES), F32)],
        compiler_params=pltpu.CompilerParams(dimension_semantics=("arbitrary",),
                                             vmem_limit_bytes=V7X_VMEM_LIMIT_BYTES),
        name="out_proj",
    )(*acts_a, *acts_b, wc, wa, wo, g2, wr, br)


def _block_table_kernel(counts_ref, bexp_ref, bpos_ref, bcnt_ref, bslot_ref, bnext_ref, nused_ref, first_ref,
                        *, n_blocks):
    def per_expert(e, carry):
        blk0, pos0, ordinal = carry
        cnt = counts_ref[e]
        nblk = lax.shift_right_logical(cnt + (MOE_BLOCK - 1), MOE_BLOCK_LOG2)
        first_ref[e] = jnp.where(nblk > 0, blk0, -1)

        def mark(b, c):
            off = (b - blk0) * MOE_BLOCK
            bexp_ref[b] = e
            bpos_ref[b] = pos0 + off
            bcnt_ref[b] = jnp.minimum(cnt - off, MOE_BLOCK)
            bslot_ref[b] = jnp.where(b == blk0, ordinal & 1, -1)
            bnext_ref[b] = -1
            return c
        lax.fori_loop(blk0, blk0 + nblk, mark, 0)
        return blk0 + nblk, pos0 + cnt, ordinal + jnp.where(nblk > 0, 1, 0)

    n_used, _, _ = lax.fori_loop(0, N_EXPERTS, per_expert, (0, 0, 0))
    nused_ref[0] = n_used

    def unused(b, c):
        bexp_ref[b] = N_EXPERTS - 1
        bpos_ref[b] = 0
        bcnt_ref[b] = 0
        bslot_ref[b] = -1
        bnext_ref[b] = -1
        return c
    lax.fori_loop(n_used, n_blocks, unused, 0)

    def link(k, nxt):
        e = N_EXPERTS - 1 - k
        fb = first_ref[e]

        @pl.when(fb >= 0)
        def _():
            bnext_ref[fb] = nxt
        return jnp.where(fb >= 0, e, nxt)
    lax.fori_loop(0, N_EXPERTS, link, -1)


def _block_tables(counts, n_blocks):
    smem = pl.BlockSpec(memory_space=pltpu.SMEM)
    blk = jax.ShapeDtypeStruct((n_blocks,), jnp.int32)
    return pl.pallas_call(
        functools.partial(_block_table_kernel, n_blocks=n_blocks),
        in_specs=[smem],
        out_specs=[smem] * 6,
        out_shape=[blk] * 5 + [jax.ShapeDtypeStruct((1,), jnp.int32)],
        scratch_shapes=[pltpu.SMEM((N_EXPERTS,), jnp.int32)],
        name="block_tables",
    )(counts)


def _moe_kernel(tok_ref, order_ref, bexp_ref, bpos_ref, bcnt_ref, bslot_ref, bnext_ref, nused_ref,
                h2p_ref, wg_hbm, wu_hbm, wd_hbm, contrib_hbm,
                xs_ref, ys_ref, wgf_ref, wuf_ref, wdf_ref, wgb_ref, wub_ref, wdb_ref, ssem_ref, wsem_ref,
                *, m_total, half, n_blocks):
    step = pl.program_id(0)
    n_used = nused_ref[0]
    tile = lambda t: pl.ds(pl.multiple_of(t * TILE_ROWS, TILE_ROWS), TILE_ROWS)

    def gather(pos0, slt):
        for r in range(MOE_BLOCK):
            xs_ref[slt, pl.ds(r, 1), :] = h2p_ref[pl.ds(tok_ref[pos0 + r], 1), :]

    def scatter(pos0, cnt, trash0, slt, r):
        dst = jnp.where(r < cnt, order_ref[pos0 + r], trash0 + r)
        return pltpu.make_async_copy(ys_ref.at[slt, tile(r), :], contrib_hbm.at[tile(dst), :], ssem_ref.at[slt])

    def scatter_wait(slt, r):
        pltpu.make_async_copy(ys_ref.at[slt, tile(r), :], contrib_hbm.at[tile(0), :], ssem_ref.at[slt]).wait()

    @pl.when(step == 0)
    def _():
        ys_ref[...] = jnp.zeros_like(ys_ref)
        gap = half - m_total
        fills = [(2 * half + s * MOE_BLOCK, MOE_BLOCK) for s in range(2)]
        fills += [(k * half + m_total, gap) for k in range(TOP_K)] if gap else []
        for start, n in fills:
            fill = pltpu.make_async_copy(ys_ref.at[0, pl.ds(0, n * TILE_ROWS), :],
                                         contrib_hbm.at[pl.ds(start * TILE_ROWS, n * TILE_ROWS), :], ssem_ref.at[0])
            fill.start()
            fill.wait()

    def weight_copies(e, s):
        return [pltpu.make_async_copy(src.at[e], dst.at[s], wsem_ref.at[s])
                for src, dst in ((wg_hbm, wgf_ref), (wu_hbm, wuf_ref), (wd_hbm, wdf_ref))]

    @pl.when(step == 0)
    def _():
        for c in weight_copies(bexp_ref[0], 0):
            c.start()
        gather(bpos_ref[0], 0)

    def wait_scatters(slt):
        for r in range(MOE_BLOCK):
            scatter_wait(slt, r)

    def expert_weights(i):
        wslot = bslot_ref[i]

        @pl.when(wslot >= 0)
        def _():
            for c in weight_copies(0, wslot):
                c.wait()

        @pl.when(jnp.logical_and(wslot >= 0, bnext_ref[i] >= 0))
        def _():
            for c in weight_copies(bnext_ref[i], 1 - wslot):
                c.start()

        @pl.when(wslot >= 0)
        def _():
            wgb_ref[...] = wgf_ref[wslot].astype(BF16)
            wub_ref[...] = wuf_ref[wslot].astype(BF16)
            wdb_ref[...] = wdf_ref[wslot].astype(BF16)

    def run_block(i, slt):
        pos_next = bpos_ref[jnp.minimum(i + 1, n_blocks - 1)]
        pos0 = bpos_ref[i]
        cnt = bcnt_ref[i]
        trash0 = 2 * half + slt * MOE_BLOCK
        gather(pos_next, 1 - slt)
        xb = _unpack_bf16_pairs(xs_ref[slt])
        gate = jnp.dot(xb, wgb_ref[...], preferred_element_type=F32)
        up = jnp.dot(xb, wub_ref[...], preferred_element_type=F32)
        hmid = (jax.nn.silu(gate) * up).astype(BF16)
        _store_token_tiles(ys_ref.at[slt], jnp.dot(hmid, wdb_ref[...], preferred_element_type=F32))
        for r in range(MOE_BLOCK):
            scatter(pos0, cnt, trash0, slt, r).start(priority=r % 2)

    for slt in range(2):
        i = 2 * step + slt
        active = i < n_used
        last_active = i == n_used - 1
        pl.when(jnp.logical_and(active, i >= 2))(functools.partial(wait_scatters, slt))
        expert_weights(i)
        pl.when(active)(functools.partial(run_block, i, slt))
        pl.when(last_active)(functools.partial(wait_scatters, slt))
        pl.when(jnp.logical_and(last_active, i >= 1))(functools.partial(wait_scatters, 1 - slt))


def _moe_experts(tok, order, tables, h2p, w_gate, w_up, w_down, *, m_total, half):
    n_blocks = tables[0].shape[0]
    assert 0 <= half - m_total <= MOE_BLOCK and n_blocks % 2 == 0
    hbm = pl.BlockSpec(memory_space=pl.ANY)
    grid_spec = pltpu.PrefetchScalarGridSpec(
        num_scalar_prefetch=2 + len(tables),
        grid=(n_blocks // 2,),
        in_specs=[_const_spec(h2p.shape), hbm, hbm, hbm],
        out_specs=hbm,
        scratch_shapes=[pltpu.VMEM((2, MOE_BLOCK, D_MODEL // 2), jnp.uint32),
                        pltpu.VMEM((2, MOE_BLOCK * TILE_ROWS, LANES), F32),
                        pltpu.VMEM((2, D_MODEL, D_EXPERT), F32),
                        pltpu.VMEM((2, D_MODEL, D_EXPERT), F32),
                        pltpu.VMEM((2, D_EXPERT, D_MODEL), F32),
                        pltpu.VMEM((D_MODEL, D_EXPERT), BF16),
                        pltpu.VMEM((D_MODEL, D_EXPERT), BF16),
                        pltpu.VMEM((D_EXPERT, D_MODEL), BF16),
                        pltpu.SemaphoreType.DMA((2,)),
                        pltpu.SemaphoreType.DMA((2,))],
    )
    return pl.pallas_call(
        functools.partial(_moe_kernel, m_total=m_total, half=half, n_blocks=n_blocks),
        grid_spec=grid_spec,
        out_shape=jax.ShapeDtypeStruct(((2 * half + 2 * MOE_BLOCK) * TILE_ROWS, LANES), F32),
        compiler_params=pltpu.CompilerParams(dimension_semantics=("arbitrary",),
                                             vmem_limit_bytes=MOE_VMEM_LIMIT_BYTES),
        name="moe_experts",
    )(tok, order, *tables, h2p, w_gate, w_up, w_down)


def _combine_kernel(c0_ref, c1_ref, x2_ref, route_ref, gf_ref, y_ref):
    tc = x2_ref.shape[0]
    w = lax.bitcast_convert_type(route_ref[:, TOP_K:2 * TOP_K], F32)
    moe = w[:, 0:1] * _load_token_tiles(c0_ref, tc) + w[:, 1:2] * _load_token_tiles(c1_ref, tc)
    y_ref[...] = _rms_norm_f32(x2_ref[...] + moe, gf_ref[...])


def _moe_combine(contrib, x2, route, gf, *, row_off, m, tc, half):
    off = row_off // tc
    assert row_off % tc == 0 and half % tc == 0
    ctile = lambda k: pl.BlockSpec((tc * TILE_ROWS, LANES), lambda i: (i + off + k * (half // tc), 0))
    return pl.pallas_call(
        _combine_kernel,
        grid=(m // tc,),
        in_specs=[ctile(0), ctile(1),
                  pl.BlockSpec((tc, D_MODEL), lambda i: (i + off, 0)),
                  pl.BlockSpec((tc, ROUTER_LANES), lambda i: (i + off, 0)),
                  _const_spec((1, D_MODEL))],
        out_specs=pl.BlockSpec((tc, D_MODEL), lambda i: (i, 0)),
        out_shape=jax.ShapeDtypeStruct((m, D_MODEL), F32),
        compiler_params=pltpu.CompilerParams(dimension_semantics=("arbitrary",),
                                             vmem_limit_bytes=V7X_VMEM_LIMIT_BYTES),
        name="moe_combine_prompt" if row_off == 0 else "moe_combine_sample",
    )(contrib, contrib, x2, route, gf)


def _t5_bucket(dist):
    n = jnp.maximum(dist, 0)
    max_exact = N_BUCKETS // 2
    nf = jnp.maximum(n, 1).astype(F32)
    large = max_exact + (jnp.log(nf / max_exact) / math.log(MAX_DISTANCE / max_exact)
                         * (N_BUCKETS - max_exact)).astype(jnp.int32)
    large = jnp.minimum(large, N_BUCKETS - 1)
    return jnp.where(n < max_exact, n, large)


def _bucket_bias(rel_bias, dist, valid):
    buckets = _t5_bucket(dist).reshape(1, -1)
    onehot = (buckets == jnp.arange(N_BUCKETS, dtype=jnp.int32)[:, None]).astype(F32)
    bias = jnp.dot(rel_bias.astype(F32).T, onehot, precision=lax.Precision.HIGHEST)
    return jnp.where(valid.reshape(1, -1), bias, NEG_BIG).reshape((rel_bias.shape[1],) + dist.shape)


def _prompt_bias_table(rel_bias):
    qi = jnp.arange(ATTN_BLOCK, dtype=jnp.int32)[:, None]
    kj = jnp.arange(2 * ATTN_BLOCK, dtype=jnp.int32)[None, :] - ATTN_BLOCK
    dist = qi - kj
    return _bucket_bias(rel_bias, dist, (dist >= 0) & (dist <= WINDOW))


def _sample_bias_table(rel_bias, w_buf):
    dist = w_buf - jnp.arange(w_buf + 1, dtype=jnp.int32)
    return _bucket_bias(rel_bias, dist, dist <= WINDOW)


def kernel(x_prompt, x_sample, cache_conv, cache_k, cache_v, norm1_g, w_in, conv_w, w_conv_out, w_attn_out, w_o, sinks, rel_bias, norm2_g, w_router_group, b_router_group, w_router_expert, b_router_expert, w_e_gate, w_e_up, w_e_down, norm_f_g):
    assert norm1_g.shape[0] == 1, "single-layer configuration"
    batch, seq, _ = x_prompt.shape
    nseq = x_sample.shape[0]
    w_buf = cache_k.shape[2]
    mp = batch * seq
    m_total = mp + nseq
    assert seq % TM_DENSE == 0 and seq % TM_IN_PROJ == 0 and seq % ATTN_BLOCK == 0 and mp % COMBINE_BLOCK == 0
    assert nseq % SAMPLE_SEQ_PER_STEP == 0 and mp % nseq == 0
    assert TOP_K == 2 and MOE_BLOCK == 1 << MOE_BLOCK_LOG2 and m_total * TOP_K < 1 << ASSIGN_BITS

    g1 = norm1_g[0][None, :]
    g2 = norm2_g[0][None, :]
    gf = norm_f_g[None, :]
    wi = w_in[0].astype(BF16)
    cw = conv_w[0]
    wc = w_conv_out[0].astype(BF16)
    wa = w_attn_out[0].astype(BF16)
    wo = w_o[0].astype(BF16)
    pad_cols = ROUTER_LANES - N_EXPERT_GROUPS - N_EXPERTS
    wr = jnp.concatenate([w_router_group[0], w_router_expert[0],
                          jnp.zeros((D_MODEL, pad_cols), F32)], axis=1).astype(BF16)
    br = jnp.concatenate([b_router_group[0], b_router_expert[0], jnp.zeros((pad_cols,), F32)])[None, :]
    sink = sinks[0].astype(F32)

    xp = x_prompt.reshape(mp, D_MODEL)
    bps = seq // TM_IN_PROJ
    yc_p, q_p, k_p, v_p, sa_p, sb_p, ut_p, kvt_p = _in_proj(
        xp, g1, wi, cw, tm=TM_IN_PROJ, blocks_per_seq=bps, u_tail=8, kv_tail=WINDOW)
    o_p = _attn_prompt(q_p, k_p, v_p, _prompt_bias_table(rel_bias), sink, batch, seq)

    pad_rows = lambda t: jnp.pad(t, ((0, TM_DENSE - nseq), (0, 0)))
    xs = pad_rows(x_sample.reshape(nseq, D_MODEL))
    hist = (pad_rows(cache_conv[0][:, 0, :]), pad_rows(cache_conv[0][:, 1, :]))
    yc_s, q_s, _, _, sa_s, sb_s, ut_s, kvt_s = _in_proj(
        xs, g1, wi, cw, tm=TM_DENSE, blocks_per_seq=1, u_tail=TM_DENSE, kv_tail=TM_DENSE, hist=hist,
        gate_dtype=F32)
    u_s = ut_s[0, :nseq]
    kv_s = kvt_s[0, :nseq]
    head_mask = (jnp.arange(KV_DIM)[None, :] // HEAD_DIM == jnp.arange(N_HEADS)[:, None] // GROUP)
    qbd = (jnp.tile(q_s[:nseq].reshape(nseq, N_HEADS, HEAD_DIM), (1, 1, N_KV_HEADS))
           * head_mask[None].astype(BF16))
    to_keys_minor = lambda c: jnp.transpose(c.reshape(nseq, w_buf, KV_DIM), (0, 2, 1))
    from_keys_minor = lambda c: jnp.transpose(c, (0, 2, 1)).reshape(1, nseq, w_buf, N_KV_HEADS, HEAD_DIM)
    o_s, kwin_s, vwin_s = _attn_sample(qbd, to_keys_minor(cache_k[0]), to_keys_minor(cache_v[0]), kv_s,
                                       _sample_bias_table(rel_bias, w_buf), sink[:, None],
                                       head_mask.astype(F32))
    o_s = pad_rows(o_s.reshape(nseq, Q_DIM))

    half = -(-m_total // COMBINE_BLOCK) * COMBINE_BLOCK
    assert half % nseq == 0 and TOP_K * half < 1 << ASSIGN_BITS
    x2, h2p, route, route_t, cnt = _out_proj((yc_p, o_p, sa_p, sb_p, xp), (yc_s, o_s, sa_s, sb_s, xs),
                                             wc, wa, wo, g2, wr, br, tm=TM_DENSE, valid_rows_b=nseq, half=half)

    n_assign = m_total * TOP_K
    keys = route_t[0:TOP_K, :m_total].reshape(-1)
    counts = cnt[0, N_EXPERT_GROUPS:N_EXPERT_GROUPS + N_EXPERTS].astype(jnp.int32)
    order = jnp.pad(jnp.sort(keys) & ((1 << ASSIGN_BITS) - 1), (0, MOE_BLOCK))
    tok = jnp.where(order >= half, order - half, order)
    n_blocks = -(-n_assign // MOE_BLOCK) + N_EXPERTS
    n_blocks += n_blocks % 2
    tables = _block_tables(counts, n_blocks)
    contrib = _moe_experts(tok, order, tables, h2p, w_e_gate[0], w_e_up[0], w_e_down[0],
                           m_total=m_total, half=half)
    y_p = _moe_combine(contrib, x2, route, gf, row_off=0, m=mp, tc=COMBINE_BLOCK, half=half)
    y_s = _moe_combine(contrib, x2, route, gf, row_off=mp, m=nseq, tc=nseq, half=half)

    y_prompt = y_p.reshape(batch, seq, D_MODEL)
    y_sample = y_s.reshape(nseq, 1, D_MODEL)
    conv_state_prompt = ut_p.reshape(batch, bps, 8, D_CONV)[:, -1, 8 - (CONV_WIDTH - 1):, :][None]
    kv_last = kvt_p.reshape(batch, bps, WINDOW, 2 * KV_DIM)[:, -1]
    k_win_prompt = kv_last[:, :, :KV_DIM].reshape(batch, WINDOW, N_KV_HEADS, HEAD_DIM)[None]
    v_win_prompt = kv_last[:, :, KV_DIM:].reshape(batch, WINDOW, N_KV_HEADS, HEAD_DIM)[None]
    conv_state_sample = jnp.concatenate([cache_conv[0][:, 1:, :], u_s[:, None, :]], axis=1)[None]
    k_win_sample = from_keys_minor(kwin_s)
    v_win_sample = from_keys_minor(vwin_s)
    return (y_prompt, y_sample, conv_state_prompt, k_win_prompt, v_win_prompt,
            conv_state_sample, k_win_sample, v_win_sample)
```

```python
import functools
import math

import jax
import jax.numpy as jnp
from jax import lax
from jax.experimental import pallas as pl
from jax.experimental.pallas import tpu as pltpu

D_MODEL = 1024
D_CONV = 1024
CONV_WIDTH = 3
N_HEADS = 16
N_KV_HEADS = 4
HEAD_DIM = 64
GROUP = N_HEADS // N_KV_HEADS
WINDOW = 128
Q_DIM = N_HEADS * HEAD_DIM
KV_DIM = N_KV_HEADS * HEAD_DIM
N_BUCKETS = 32
MAX_DISTANCE = 128
N_EXPERT_GROUPS = 4
EXPERTS_PER_GROUP = 8
N_EXPERTS = N_EXPERT_GROUPS * EXPERTS_PER_GROUP
TOP_K = 2
D_EXPERT = 512
EPS = 1e-6
PAST_LEN = 8192

BF16 = jnp.bfloat16
F32 = jnp.float32
NEG_BIG = -1e30

V7X_VMEM_LIMIT_BYTES = 56 * 1024 * 1024
MOE_VMEM_LIMIT_BYTES = 62 * 1024 * 1024
TILE_ROWS = 8
LANES = 128
ROUTER_LANES = 128
TM_DENSE = 512
TM_IN_PROJ = 1024
OUT_PROJ_PARTS = 2
ATTN_BLOCK = 128
MOE_BLOCK = 256
MOE_BLOCK_LOG2 = 8
ASSIGN_BITS = 16
COMBINE_BLOCK = 256
SAMPLE_SEQ_PER_STEP = 16
HEADS_PER_STORE = LANES // HEAD_DIM


def _const_spec(shape):
    nd = len(shape)
    return pl.BlockSpec(shape, lambda *_: (0,) * nd, pipeline_mode=pl.Buffered(1))


def _rms_norm_f32(xf, g):
    return xf * lax.rsqrt(jnp.mean(xf * xf, axis=-1, keepdims=True) + EPS) * g


def _in_proj_kernel(*refs, tm, sample, blocks_per_seq, u_tail, kv_tail):
    if sample:
        (x_ref, hist0_ref, hist1_ref, g_ref, w_ref,
         cw_ref, yc_ref, q_ref, k_ref, v_ref, sa_ref, sb_ref, ut_ref, kvt_ref) = refs
    else:
        (x_ref, g_ref, w_ref,
         cw_ref, yc_ref, q_ref, k_ref, v_ref, sa_ref, sb_ref, ut_ref, kvt_ref, ubuf_ref) = refs

    h = _rms_norm_f32(x_ref[...], g_ref[...]).astype(BF16)

    widths = (D_CONV, D_CONV, D_CONV, Q_DIM, 2 * KV_DIM, D_MODEL, D_MODEL)
    starts = [sum(widths[:n]) for n in range(len(widths))]
    wcb_ref, wcc_ref, wch_ref, wq_ref, wkv_ref, wga_ref, wgb_ref = [
        w_ref.at[:, pl.ds(a, n)] for a, n in zip(starts, widths)]

    def proj(part_ref):
        return jnp.dot(h, part_ref[...], preferred_element_type=F32)

    u = proj(wcc_ref) * proj(wch_ref)
    w0 = cw_ref[0:1, :]
    w1 = cw_ref[1:2, :]
    w2 = cw_ref[2:3, :]
    if sample:
        conv = w0 * hist0_ref[...] + w1 * hist1_ref[...] + w2 * u
    else:
        @pl.when(pl.program_id(0) % blocks_per_seq == 0)
        def _():
            ubuf_ref[0:8, :] = jnp.zeros((8, D_CONV), F32)

        ubuf_ref[8:8 + tm, :] = u
        conv = w0 * ubuf_ref[6:6 + tm, :] + w1 * ubuf_ref[7:7 + tm, :] + w2 * u
        ubuf_ref[0:8, :] = u[tm - 8:, :]
    yc_ref[...] = (proj(wcb_ref) * conv).astype(BF16)
    ut_ref[0] = u[tm - u_tail:, :]

    q_ref[...] = (proj(wq_ref) * (HEAD_DIM ** -0.5)).astype(BF16)
    kv = proj(wkv_ref)
    k_ref[...] = kv[:, :KV_DIM].astype(BF16)
    v_ref[...] = kv[:, KV_DIM:].astype(BF16)
    kvt_ref[0] = kv[tm - kv_tail:, :]
    sa_ref[...] = jax.nn.sigmoid(proj(wga_ref)).astype(sa_ref.dtype)
    sb_ref[...] = jax.nn.sigmoid(proj(wgb_ref)).astype(sb_ref.dtype)


def _in_proj(x, g1, w_in, conv_w, *, tm, blocks_per_seq, u_tail, kv_tail, hist=None, gate_dtype=BF16):
    m = x.shape[0]
    nblk = m // tm
    sample = hist is not None
    row = lambda width: pl.BlockSpec((tm, width), lambda i: (i, 0))
    in_specs = [row(D_MODEL)]
    args = [x]
    if sample:
        in_specs += [row(D_CONV), row(D_CONV)]
        args += list(hist)
    in_specs += [_const_spec((1, D_MODEL)), _const_spec(w_in.shape), _const_spec(conv_w.shape)]
    args += [g1, w_in, conv_w]
    out_shape = [
        jax.ShapeDtypeStruct((m, D_CONV), BF16),
        jax.ShapeDtypeStruct((m, Q_DIM), BF16),
        jax.ShapeDtypeStruct((m, KV_DIM), BF16),
        jax.ShapeDtypeStruct((m, KV_DIM), BF16),
        jax.ShapeDtypeStruct((m, D_MODEL), gate_dtype),
        jax.ShapeDtypeStruct((m, D_MODEL), gate_dtype),
        jax.ShapeDtypeStruct((nblk, u_tail, D_CONV), F32),
        jax.ShapeDtypeStruct((nblk, kv_tail, 2 * KV_DIM), F32),
    ]
    out_specs = [row(D_CONV), row(Q_DIM), row(KV_DIM), row(KV_DIM), row(D_MODEL), row(D_MODEL),
                 pl.BlockSpec((1, u_tail, D_CONV), lambda i: (i, 0, 0)),
                 pl.BlockSpec((1, kv_tail, 2 * KV_DIM), lambda i: (i, 0, 0))]
    scratch = [] if sample else [pltpu.VMEM((tm + 8, D_CONV), F32)]
    return pl.pallas_call(
        functools.partial(_in_proj_kernel, tm=tm, sample=sample, blocks_per_seq=blocks_per_seq,
                          u_tail=u_tail, kv_tail=kv_tail),
        grid=(nblk,),
        in_specs=in_specs,
        out_specs=out_specs,
        out_shape=out_shape,
        scratch_shapes=scratch,
        compiler_params=pltpu.CompilerParams(dimension_semantics=("arbitrary",),
                                             vmem_limit_bytes=MOE_VMEM_LIMIT_BYTES),
        name="in_proj_sample" if sample else "in_proj_prompt",
    )(*args)


def _attn_prompt_kernel(sink_ref, q_ref, kc_ref, kp_ref, vc_ref, vp_ref, bias_ref, o_ref):
    first = pl.program_id(1) == 0
    col = lax.broadcasted_iota(jnp.int32, (ATTN_BLOCK, 2 * ATTN_BLOCK), 1)
    no_prev = jnp.logical_and(first, col < ATTN_BLOCK)
    for g in range(N_KV_HEADS):
        ks = slice(g * HEAD_DIM, (g + 1) * HEAD_DIM)
        kcat = jnp.concatenate([kp_ref[:, ks], kc_ref[:, ks]], axis=0)
        vcat = jnp.concatenate([vp_ref[:, ks], vc_ref[:, ks]], axis=0)
        for h0 in range(g * GROUP, (g + 1) * GROUP, HEADS_PER_STORE):
            outs = []
            for h in range(h0, h0 + HEADS_PER_STORE):
                hs = slice(h * HEAD_DIM, (h + 1) * HEAD_DIM)
                s = lax.dot_general(q_ref[:, hs], kcat, (((1,), (1,)), ((), ())),
                                    preferred_element_type=F32)
                s = jnp.where(no_prev, NEG_BIG, s + bias_ref[h])
                sink = sink_ref[h]
                m = jnp.maximum(jnp.max(s, axis=-1, keepdims=True), sink)
                p = jnp.exp(s - m)
                denom = jnp.sum(p, axis=-1, keepdims=True) + jnp.exp(sink - m)
                o = jnp.dot(p.astype(BF16), vcat, preferred_element_type=F32)
                outs.append((o / denom).astype(BF16))
            o_ref[:, h0 * HEAD_DIM:(h0 + HEADS_PER_STORE) * HEAD_DIM] = jnp.concatenate(outs, axis=1)


def _attn_prompt(q, k, v, bias, sinks, batch, seq):
    nb = seq // ATTN_BLOCK
    cur = lambda b, i: (b * nb + i, 0)
    prev = lambda b, i: (b * nb + jnp.maximum(i - 1, 0), 0)
    return pl.pallas_call(
        _attn_prompt_kernel,
        grid=(batch, nb),
        in_specs=[pl.BlockSpec(memory_space=pltpu.SMEM),
                  pl.BlockSpec((ATTN_BLOCK, Q_DIM), cur),
                  pl.BlockSpec((ATTN_BLOCK, KV_DIM), cur),
                  pl.BlockSpec((ATTN_BLOCK, KV_DIM), prev),
                  pl.BlockSpec((ATTN_BLOCK, KV_DIM), cur),
                  pl.BlockSpec((ATTN_BLOCK, KV_DIM), prev),
                  _const_spec(bias.shape)],
        out_specs=pl.BlockSpec((ATTN_BLOCK, Q_DIM), cur),
        out_shape=jax.ShapeDtypeStruct((batch * seq, Q_DIM), BF16),
        compiler_params=pltpu.CompilerParams(dimension_semantics=("arbitrary", "arbitrary"),
                                             vmem_limit_bytes=V7X_VMEM_LIMIT_BYTES),
        name="attn_prompt",
    )(sinks, q, k, k, v, v, bias)


def _attn_sample_kernel(qbd_ref, ckt_ref, cvt_ref, kvn_ref, bias_ref, sink_ref, mask_ref,
                        o_ref, kwin_ref, vwin_ref, *, w_buf):
    bf16_round = lambda t: t.astype(BF16).astype(F32)
    seqs = range(SAMPLE_SEQ_PER_STEP)
    sink = sink_ref[...]
    newest = lax.broadcasted_iota(jnp.int32, (KV_DIM, w_buf), 1) == w_buf - 1
    kvn_t = kvn_ref[...].T
    for b in seqs:
        kwin_ref[b] = jnp.where(newest, kvn_t[:KV_DIM, b:b + 1], pltpu.roll(ckt_ref[b], w_buf - 1, axis=1))
        vwin_ref[b] = jnp.where(newest, kvn_t[KV_DIM:, b:b + 1], pltpu.roll(cvt_ref[b], w_buf - 1, axis=1))
    s = [jnp.dot(qbd_ref[b], ckt_ref[b].astype(BF16), preferred_element_type=F32) + bias_ref[:, :w_buf]
         for b in seqs]
    s_new = [jnp.sum(qbd_ref[b].astype(F32) * bf16_round(kvn_ref[b:b + 1, :KV_DIM]), axis=-1, keepdims=True)
             + bias_ref[:, w_buf:w_buf + 1] for b in seqs]
    m = [jnp.maximum(jnp.maximum(jnp.max(s[b], axis=-1, keepdims=True), s_new[b]), sink) for b in seqs]
    p = [jnp.exp(s[b] - m[b]) for b in seqs]
    p_new = [jnp.exp(s_new[b] - m[b]) for b in seqs]
    denom = [jnp.sum(p[b], axis=-1, keepdims=True) + p_new[b] + jnp.exp(sink - m[b]) for b in seqs]
    of = [lax.dot_general((p[b] / denom[b]).astype(BF16), cvt_ref[b].astype(BF16), (((1,), (1,)), ((), ())),
                          preferred_element_type=F32)
          + bf16_round(p_new[b] / denom[b]) * bf16_round(kvn_ref[b:b + 1, KV_DIM:]) for b in seqs]
    for b in seqs:
        ob = of[b] * mask_ref[...]
        o_ref[b] = (ob[:, 0:HEAD_DIM] + ob[:, HEAD_DIM:2 * HEAD_DIM]
                    + ob[:, 2 * HEAD_DIM:3 * HEAD_DIM] + ob[:, 3 * HEAD_DIM:]).astype(BF16)


def _attn_sample(qbd, ckt, cvt, kvn, bias, sink_col, head_mask):
    nseq, w_buf = ckt.shape[0], ckt.shape[2]
    sb = SAMPLE_SEQ_PER_STEP
    seq3 = lambda d1, d2: pl.BlockSpec((sb, d1, d2), lambda i: (i, 0, 0))
    win = jax.ShapeDtypeStruct((nseq, KV_DIM, w_buf), F32)
    return pl.pallas_call(
        functools.partial(_attn_sample_kernel, w_buf=w_buf),
        grid=(nseq // sb,),
        in_specs=[seq3(N_HEADS, KV_DIM), seq3(KV_DIM, w_buf), seq3(KV_DIM, w_buf),
                  pl.BlockSpec((sb, 2 * KV_DIM), lambda i: (i, 0)),
                  _const_spec(bias.shape), _const_spec(sink_col.shape), _const_spec(head_mask.shape)],
        out_specs=[seq3(N_HEADS, HEAD_DIM), seq3(KV_DIM, w_buf), seq3(KV_DIM, w_buf)],
        out_shape=[jax.ShapeDtypeStruct((nseq, N_HEADS, HEAD_DIM), BF16), win, win],
        compiler_params=pltpu.CompilerParams(dimension_semantics=("arbitrary",),
                                             vmem_limit_bytes=V7X_VMEM_LIMIT_BYTES),
        name="attn_sample",
    )(qbd, ckt, cvt, kvn, bias, sink_col, head_mask)


def _route_rows(logits, row0, valid_rows, half):
    tm = logits.shape[0]
    lane = lax.broadcasted_iota(jnp.int32, logits.shape, 1)
    lane_f = lane.astype(F32)
    no_lane = float(ROUTER_LANES)

    def top1(mask):
        best = jnp.max(jnp.where(mask, logits, -jnp.inf), axis=-1, keepdims=True)
        idx = jnp.min(jnp.where(jnp.logical_and(mask, logits == best), lane_f, no_lane), axis=-1, keepdims=True)
        return best, idx

    gmask = lane < N_EXPERT_GROUPS
    gmax, grp = top1(gmask)
    gsum = jnp.sum(jnp.where(gmask, jnp.exp(logits - gmax), 0.0), axis=-1, keepdims=True)
    p_grp = 1.0 / gsum
    lo = N_EXPERT_GROUPS + EXPERTS_PER_GROUP * grp
    emask = jnp.logical_and(lane_f >= lo, lane_f < lo + EXPERTS_PER_GROUP)
    v1, i1 = top1(emask)
    v2, i2 = top1(jnp.logical_and(emask, lane_f != i1))
    e21 = jnp.exp(v2 - v1)
    w1 = p_grp / (1.0 + e21)
    w2 = p_grp * e21 / (1.0 + e21)

    oh1 = lane_f == i1
    oh2 = lane_f == i2
    if valid_rows < tm:
        valid = lax.broadcasted_iota(jnp.int32, logits.shape, 0) < valid_rows
        oh1 = jnp.logical_and(oh1, valid)
        oh2 = jnp.logical_and(oh2, valid)
    oh = oh1.astype(F32) + oh2.astype(F32)
    token = row0 + lax.broadcasted_iota(jnp.int32, (tm, 1), 0)
    key1 = (i1.astype(jnp.int32) - N_EXPERT_GROUPS) * (1 << ASSIGN_BITS) + token
    key2 = (i2.astype(jnp.int32) - N_EXPERT_GROUPS) * (1 << ASSIGN_BITS) + token + half
    w1b = lax.bitcast_convert_type(w1, jnp.int32)
    w2b = lax.bitcast_convert_type(w2, jnp.int32)
    words = jnp.where(lane == 0, key1, jnp.where(lane == 1, key2, jnp.where(lane == 2, w1b,
                      jnp.where(lane == 3, w2b, 0))))
    return words, jnp.sum(oh, axis=0, keepdims=True)


def _store_token_tiles(ref, x):
    n = x.shape[0]
    for c in range(D_MODEL // LANES):
        ref[pl.ds(c, n, stride=TILE_ROWS), :] = x[:, c * LANES:(c + 1) * LANES]


def _load_token_tiles(ref, n):
    return jnp.concatenate([ref[pl.ds(c, n, stride=TILE_ROWS), :] for c in range(D_MODEL // LANES)], axis=1)


def _pack_bf16_pairs(x):
    hw = x.shape[1] // 2
    bits = lambda v: lax.bitcast_convert_type(v.astype(BF16).astype(F32), jnp.uint32)
    return (bits(x[:, hw:]) & jnp.uint32(0xFFFF0000)) | (bits(x[:, :hw]) >> 16)


def _unpack_bf16_pairs(w):
    lo = lax.bitcast_convert_type(w << 16, F32)
    hi = lax.bitcast_convert_type(w & jnp.uint32(0xFFFF0000), F32)
    return jnp.concatenate([lo, hi], axis=1).astype(BF16)


def _out_proj_rows(yc_ref, o_ref, sa_ref, sb_ref, x_ref, wc_ref, wa_ref, wo_ref, g2_ref, wr_ref, br_ref,
                   x2_ref, h2p_ref, route_ref, keys_ref, cnt_ref, *, valid_rows, half):
    tm = x_ref.shape[0]
    parts = [(r0, tm // OUT_PROJ_PARTS) for r0 in range(0, tm, tm // OUT_PROJ_PARTS)]
    for r0, n in parts:
        rows = pl.ds(r0, n)
        y_conv = jnp.dot(yc_ref[rows, :], wc_ref[...], preferred_element_type=F32)
        y_attn = jnp.dot(o_ref[rows, :], wa_ref[...], preferred_element_type=F32)
        mix = (sa_ref[rows, :].astype(F32) * y_conv + sb_ref[rows, :].astype(F32) * y_attn).astype(BF16)
        x2_ref[rows, :] = x_ref[rows, :] + jnp.dot(mix, wo_ref[...], preferred_element_type=F32)
    for r0, n in parts:
        rows = pl.ds(r0, n)
        h2 = _rms_norm_f32(x2_ref[rows, :], g2_ref[...])
        h2p_ref[rows, :] = _pack_bf16_pairs(h2)
        logits = jnp.dot(h2.astype(BF16), wr_ref[...], preferred_element_type=F32) + br_ref[...]
        words, cnt = _route_rows(logits, pl.program_id(0) * tm + r0, min(max(valid_rows - r0, 0), n), half)
        route_ref[rows, :] = words
        keys_ref[:, rows] = words.T[:TILE_ROWS, :]
        cnt_ref[...] += cnt


def _out_proj_kernel(*refs, n_first, valid_rows_second, half):
    first, second, shared = refs[0:5], refs[5:10], refs[10:]
    cnt_ref = shared[-1]
    tm = first[4].shape[0]

    @pl.when(pl.program_id(0) == 0)
    def _():
        cnt_ref[...] = jnp.zeros_like(cnt_ref)

    @pl.when(pl.program_id(0) < n_first)
    def _():
        _out_proj_rows(*first, *shared, valid_rows=tm, half=half)

    @pl.when(pl.program_id(0) >= n_first)
    def _():
        _out_proj_rows(*second, *shared, valid_rows=valid_rows_second, half=half)


def _out_proj(acts_a, acts_b, wc, wa, wo, g2, wr, br, *, tm, valid_rows_b, half):
    na = acts_a[4].shape[0] // tm
    nb = acts_b[4].shape[0] // tm
    assert nb == 1
    m_total = (na + nb) * tm
    spec_a = lambda width: pl.BlockSpec((tm, width), lambda i: (jnp.minimum(i, na - 1), 0))
    spec_b = lambda width: pl.BlockSpec((tm, width), lambda i: (jnp.maximum(i - na, 0), 0))
    widths = (D_CONV, Q_DIM, D_MODEL, D_MODEL, D_MODEL)
    in_specs = [spec_a(w) for w in widths] + [spec_b(w) for w in widths]
    in_specs += [_const_spec(wc.shape), _const_spec(wa.shape), _const_spec(wo.shape),
                 _const_spec(g2.shape), _const_spec(wr.shape), _const_spec(br.shape)]
    orow = lambda width: pl.BlockSpec((tm, width), lambda i: (i, 0))
    return pl.pallas_call(
        functools.partial(_out_proj_kernel, n_first=na, valid_rows_second=valid_rows_b, half=half),
        grid=(na + nb,),
        in_specs=in_specs,
        out_specs=[orow(D_MODEL), orow(D_MODEL // 2), orow(ROUTER_LANES),
                   pl.BlockSpec((TILE_ROWS, tm), lambda i: (0, i)),
                   pl.BlockSpec((1, ROUTER_LANES), lambda i: (0, 0))],
        out_shape=[jax.ShapeDtypeStruct((m_total, D_MODEL), F32),
                   jax.ShapeDtypeStruct((m_total, D_MODEL // 2), jnp.uint32),
                   jax.ShapeDtypeStruct((m_total, ROUTER_LANES), jnp.int32),
                   jax.ShapeDtypeStruct((TILE_ROWS, m_total), jnp.int32),
                   jax.ShapeDtypeStruct((1, ROUTER_LANES), F32)],
        compiler_params=pltpu.CompilerParams(dimension_semantics=("arbitrary",),
                                             vmem_limit_bytes=V7X_VMEM_LIMIT_BYTES),
        name="out_proj",
    )(*acts_a, *acts_b, wc, wa, wo, g2, wr, br)


def _block_table_kernel(counts_ref, bexp_ref, bpos_ref, bcnt_ref, bslot_ref, bnext_ref, nused_ref, first_ref,
                        *, n_blocks):
    def per_expert(e, carry):
        blk0, pos0, ordinal = carry
        cnt = counts_ref[e]
        nblk = lax.shift_right_logical(cnt + (MOE_BLOCK - 1), MOE_BLOCK_LOG2)
        first_ref[e] = jnp.where(nblk > 0, blk0, -1)

        def mark(b, c):
            off = (b - blk0) * MOE_BLOCK
            bexp_ref[b] = e
            bpos_ref[b] = pos0 + off
            bcnt_ref[b] = jnp.minimum(cnt - off, MOE_BLOCK)
            bslot_ref[b] = jnp.where(b == blk0, ordinal & 1, -1)
            bnext_ref[b] = -1
            return c
        lax.fori_loop(blk0, blk0 + nblk, mark, 0)
        return blk0 + nblk, pos0 + cnt, ordinal + jnp.where(nblk > 0, 1, 0)

    n_used, _, _ = lax.fori_loop(0, N_EXPERTS, per_expert, (0, 0, 0))
    nused_ref[0] = n_used

    def unused(b, c):
        bexp_ref[b] = N_EXPERTS - 1
        bpos_ref[b] = 0
        bcnt_ref[b] = 0
        bslot_ref[b] = -1
        bnext_ref[b] = -1
        return c
    lax.fori_loop(n_used, n_blocks, unused, 0)

    def link(k, nxt):
        e = N_EXPERTS - 1 - k
        fb = first_ref[e]

        @pl.when(fb >= 0)
        def _():
            bnext_ref[fb] = nxt
        return jnp.where(fb >= 0, e, nxt)
    lax.fori_loop(0, N_EXPERTS, link, -1)


def _block_tables(counts, n_blocks):
    smem = pl.BlockSpec(memory_space=pltpu.SMEM)
    blk = jax.ShapeDtypeStruct((n_blocks,), jnp.int32)
    return pl.pallas_call(
        functools.partial(_block_table_kernel, n_blocks=n_blocks),
        in_specs=[smem],
        out_specs=[smem] * 6,
        out_shape=[blk] * 5 + [jax.ShapeDtypeStruct((1,), jnp.int32)],
        scratch_shapes=[pltpu.SMEM((N_EXPERTS,), jnp.int32)],
        name="block_tables",
    )(counts)


def _moe_kernel(tok_ref, order_ref, bexp_ref, bpos_ref, bcnt_ref, bslot_ref, bnext_ref, nused_ref,
                h2p_ref, wg_hbm, wu_hbm, wd_hbm, contrib_hbm,
                xs_ref, ys_ref, wgf_ref, wuf_ref, wdf_ref, wgb_ref, wub_ref, wdb_ref, ssem_ref, wsem_ref,
                *, m_total, half, n_blocks):
    step = pl.program_id(0)
    n_used = nused_ref[0]
    tile = lambda t: pl.ds(pl.multiple_of(t * TILE_ROWS, TILE_ROWS), TILE_ROWS)

    def gather(pos0, slt):
        for r in range(MOE_BLOCK):
            xs_ref[slt, pl.ds(r, 1), :] = h2p_ref[pl.ds(tok_ref[pos0 + r], 1), :]

    def scatter(pos0, cnt, trash0, slt, r):
        dst = jnp.where(r < cnt, order_ref[pos0 + r], trash0 + r)
        return pltpu.make_async_copy(ys_ref.at[slt, tile(r), :], contrib_hbm.at[tile(dst), :], ssem_ref.at[slt])

    def scatter_wait(slt, r):
        pltpu.make_async_copy(ys_ref.at[slt, tile(r), :], contrib_hbm.at[tile(0), :], ssem_ref.at[slt]).wait()

    @pl.when(step == 0)
    def _():
        ys_ref[...] = jnp.zeros_like(ys_ref)
        gap = half - m_total
        fills = [(2 * half + s * MOE_BLOCK, MOE_BLOCK) for s in range(2)]
        fills += [(k * half + m_total, gap) for k in range(TOP_K)] if gap else []
        for start, n in fills:
            fill = pltpu.make_async_copy(ys_ref.at[0, pl.ds(0, n * TILE_ROWS), :],
                                         contrib_hbm.at[pl.ds(start * TILE_ROWS, n * TILE_ROWS), :], ssem_ref.at[0])
            fill.start()
            fill.wait()

    def weight_copies(e, s):
        return [pltpu.make_async_copy(src.at[e], dst.at[s], wsem_ref.at[s])
                for src, dst in ((wg_hbm, wgf_ref), (wu_hbm, wuf_ref), (wd_hbm, wdf_ref))]

    @pl.when(step == 0)
    def _():
        for c in weight_copies(bexp_ref[0], 0):
            c.start()
        gather(bpos_ref[0], 0)

    def wait_scatters(slt):
        for r in range(MOE_BLOCK):
            scatter_wait(slt, r)

    def expert_weights(i):
        wslot = bslot_ref[i]

        @pl.when(wslot >= 0)
        def _():
            for c in weight_copies(0, wslot):
                c.wait()

        @pl.when(jnp.logical_and(wslot >= 0, bnext_ref[i] >= 0))
        def _():
            for c in weight_copies(bnext_ref[i], 1 - wslot):
                c.start()

        @pl.when(wslot >= 0)
        def _():
            wgb_ref[...] = wgf_ref[wslot].astype(BF16)
            wub_ref[...] = wuf_ref[wslot].astype(BF16)
            wdb_ref[...] = wdf_ref[wslot].astype(BF16)

    def run_block(i, slt):
        pos_next = bpos_ref[jnp.minimum(i + 1, n_blocks - 1)]
        pos0 = bpos_ref[i]
        cnt = bcnt_ref[i]
        trash0 = 2 * half + slt * MOE_BLOCK
        gather(pos_next, 1 - slt)
        xb = _unpack_bf16_pairs(xs_ref[slt])
        gate = jnp.dot(xb, wgb_ref[...], preferred_element_type=F32)
        up = jnp.dot(xb, wub_ref[...], preferred_element_type=F32)
        hmid = (jax.nn.silu(gate) * up).astype(BF16)
        _store_token_tiles(ys_ref.at[slt], jnp.dot(hmid, wdb_ref[...], preferred_element_type=F32))
        for r in range(MOE_BLOCK):
            scatter(pos0, cnt, trash0, slt, r).start(priority=r % 2)

    for slt in range(2):
        i = 2 * step + slt
        active = i < n_used
        last_active = i == n_used - 1
        pl.when(jnp.logical_and(active, i >= 2))(functools.partial(wait_scatters, slt))
        expert_weights(i)
        pl.when(active)(functools.partial(run_block, i, slt))
        pl.when(last_active)(functools.partial(wait_scatters, slt))
        pl.when(jnp.logical_and(last_active, i >= 1))(functools.partial(wait_scatters, 1 - slt))


def _moe_experts(tok, order, tables, h2p, w_gate, w_up, w_down, *, m_total, half):
    n_blocks = tables[0].shape[0]
    assert 0 <= half - m_total <= MOE_BLOCK and n_blocks % 2 == 0
    hbm = pl.BlockSpec(memory_space=pl.ANY)
    grid_spec = pltpu.PrefetchScalarGridSpec(
        num_scalar_prefetch=2 + len(tables),
        grid=(n_blocks // 2,),
        in_specs=[_const_spec(h2p.shape), hbm, hbm, hbm],
        out_specs=hbm,
        scratch_shapes=[pltpu.VMEM((2, MOE_BLOCK, D_MODEL // 2), jnp.uint32),
                        pltpu.VMEM((2, MOE_BLOCK * TILE_ROWS, LANES), F32),
                        pltpu.VMEM((2, D_MODEL, D_EXPERT), F32),
                        pltpu.VMEM((2, D_MODEL, D_EXPERT), F32),
                        pltpu.VMEM((2, D_EXPERT, D_MODEL), F32),
                        pltpu.VMEM((D_MODEL, D_EXPERT), BF16),
                        pltpu.VMEM((D_MODEL, D_EXPERT), BF16),
                        pltpu.VMEM((D_EXPERT, D_MODEL), BF16),
                        pltpu.SemaphoreType.DMA((2,)),
                        pltpu.SemaphoreType.DMA((2,))],
    )
    return pl.pallas_call(
        functools.partial(_moe_kernel, m_total=m_total, half=half, n_blocks=n_blocks),
        grid_spec=grid_spec,
        out_shape=jax.ShapeDtypeStruct(((2 * half + 2 * MOE_BLOCK) * TILE_ROWS, LANES), F32),
        compiler_params=pltpu.CompilerParams(dimension_semantics=("arbitrary",),
                                             vmem_limit_bytes=MOE_VMEM_LIMIT_BYTES),
        name="moe_experts",
    )(tok, order, *tables, h2p, w_gate, w_up, w_down)


def _combine_kernel(c0_ref, c1_ref, x2_ref, route_ref, gf_ref, y_ref):
    tc = x2_ref.shape[0]
    w = lax.bitcast_convert_type(route_ref[:, TOP_K:2 * TOP_K], F32)
    moe = w[:, 0:1] * _load_token_tiles(c0_ref, tc) + w[:, 1:2] * _load_token_tiles(c1_ref, tc)
    y_ref[...] = _rms_norm_f32(x2_ref[...] + moe, gf_ref[...])


def _moe_combine(contrib, x2, route, gf, *, row_off, m, tc, half):
    off = row_off // tc
    assert row_off % tc == 0 and half % tc == 0
    ctile = lambda k: pl.BlockSpec((tc * TILE_ROWS, LANES), lambda i: (i + off + k * (half // tc), 0))
    return pl.pallas_call(
        _combine_kernel,
        grid=(m // tc,),
        in_specs=[ctile(0), ctile(1),
                  pl.BlockSpec((tc, D_MODEL), lambda i: (i + off, 0)),
                  pl.BlockSpec((tc, ROUTER_LANES), lambda i: (i + off, 0)),
                  _const_spec((1, D_MODEL))],
        out_specs=pl.BlockSpec((tc, D_MODEL), lambda i: (i, 0)),
        out_shape=jax.ShapeDtypeStruct((m, D_MODEL), F32),
        compiler_params=pltpu.CompilerParams(dimension_semantics=("arbitrary",),
                                             vmem_limit_bytes=V7X_VMEM_LIMIT_BYTES),
        name="moe_combine_prompt" if row_off == 0 else "moe_combine_sample",
    )(contrib, contrib, x2, route, gf)


def _t5_bucket(dist):
    n = jnp.maximum(dist, 0)
    max_exact = N_BUCKETS // 2
    nf = jnp.maximum(n, 1).astype(F32)
    large = max_exact + (jnp.log(nf / max_exact) / math.log(MAX_DISTANCE / max_exact)
                         * (N_BUCKETS - max_exact)).astype(jnp.int32)
    large = jnp.minimum(large, N_BUCKETS - 1)
    return jnp.where(n < max_exact, n, large)


def _bucket_bias(rel_bias, dist, valid):
    buckets = _t5_bucket(dist).reshape(1, -1)
    onehot = (buckets == jnp.arange(N_BUCKETS, dtype=jnp.int32)[:, None]).astype(F32)
    bias = jnp.dot(rel_bias.astype(F32).T, onehot, precision=lax.Precision.HIGHEST)
    return jnp.where(valid.reshape(1, -1), bias, NEG_BIG).reshape((rel_bias.shape[1],) + dist.shape)


def _prompt_bias_table(rel_bias):
    qi = jnp.arange(ATTN_BLOCK, dtype=jnp.int32)[:, None]
    kj = jnp.arange(2 * ATTN_BLOCK, dtype=jnp.int32)[None, :] - ATTN_BLOCK
    dist = qi - kj
    return _bucket_bias(rel_bias, dist, (dist >= 0) & (dist <= WINDOW))


def _sample_bias_table(rel_bias, w_buf):
    dist = w_buf - jnp.arange(w_buf + 1, dtype=jnp.int32)
    return _bucket_bias(rel_bias, dist, dist <= WINDOW)


def kernel(x_prompt, x_sample, cache_conv, cache_k, cache_v, norm1_g, w_in, conv_w, w_conv_out, w_attn_out, w_o, sinks, rel_bias, norm2_g, w_router_group, b_router_group, w_router_expert, b_router_expert, w_e_gate, w_e_up, w_e_down, norm_f_g):
    assert norm1_g.shape[0] == 1, "single-layer configuration"
    batch, seq, _ = x_prompt.shape
    nseq = x_sample.shape[0]
    w_buf = cache_k.shape[2]
    mp = batch * seq
    m_total = mp + nseq
    assert seq % TM_DENSE == 0 and seq % TM_IN_PROJ == 0 and seq % ATTN_BLOCK == 0 and mp % COMBINE_BLOCK == 0
    assert nseq % SAMPLE_SEQ_PER_STEP == 0 and mp % nseq == 0
    assert TOP_K == 2 and MOE_BLOCK == 1 << MOE_BLOCK_LOG2 and m_total * TOP_K < 1 << ASSIGN_BITS

    g1 = norm1_g[0][None, :]
    g2 = norm2_g[0][None, :]
    gf = norm_f_g[None, :]
    wi = w_in[0].astype(BF16)
    cw = conv_w[0]
    wc = w_conv_out[0].astype(BF16)
    wa = w_attn_out[0].astype(BF16)
    wo = w_o[0].astype(BF16)
    pad_cols = ROUTER_LANES - N_EXPERT_GROUPS - N_EXPERTS
    wr = jnp.concatenate([w_router_group[0], w_router_expert[0],
                          jnp.zeros((D_MODEL, pad_cols), F32)], axis=1).astype(BF16)
    br = jnp.concatenate([b_router_group[0], b_router_expert[0], jnp.zeros((pad_cols,), F32)])[None, :]
    sink = sinks[0].astype(F32)

    xp = x_prompt.reshape(mp, D_MODEL)
    bps = seq // TM_IN_PROJ
    yc_p, q_p, k_p, v_p, sa_p, sb_p, ut_p, kvt_p = _in_proj(
        xp, g1, wi, cw, tm=TM_IN_PROJ, blocks_per_seq=bps, u_tail=8, kv_tail=WINDOW)
    o_p = _attn_prompt(q_p, k_p, v_p, _prompt_bias_table(rel_bias), sink, batch, seq)

    pad_rows = lambda t: jnp.pad(t, ((0, TM_DENSE - nseq), (0, 0)))
    xs = pad_rows(x_sample.reshape(nseq, D_MODEL))
    hist = (pad_rows(cache_conv[0][:, 0, :]), pad_rows(cache_conv[0][:, 1, :]))
    yc_s, q_s, _, _, sa_s, sb_s, ut_s, kvt_s = _in_proj(
        xs, g1, wi, cw, tm=TM_DENSE, blocks_per_seq=1, u_tail=TM_DENSE, kv_tail=TM_DENSE, hist=hist,
        gate_dtype=F32)
    u_s = ut_s[0, :nseq]
    kv_s = kvt_s[0, :nseq]
    head_mask = (jnp.arange(KV_DIM)[None, :] // HEAD_DIM == jnp.arange(N_HEADS)[:, None] // GROUP)
    qbd = (jnp.tile(q_s[:nseq].reshape(nseq, N_HEADS, HEAD_DIM), (1, 1, N_KV_HEADS))
           * head_mask[None].astype(BF16))
    to_keys_minor = lambda c: jnp.transpose(c.reshape(nseq, w_buf, KV_DIM), (0, 2, 1))
    from_keys_minor = lambda c: jnp.transpose(c, (0, 2, 1)).reshape(1, nseq, w_buf, N_KV_HEADS, HEAD_DIM)
    o_s, kwin_s, vwin_s = _attn_sample(qbd, to_keys_minor(cache_k[0]), to_keys_minor(cache_v[0]), kv_s,
                                       _sample_bias_table(rel_bias, w_buf), sink[:, None],
                                       head_mask.astype(F32))
    o_s = pad_rows(o_s.reshape(nseq, Q_DIM))

    half = -(-m_total // COMBINE_BLOCK) * COMBINE_BLOCK
    assert half % nseq == 0 and TOP_K * half < 1 << ASSIGN_BITS
    x2, h2p, route, route_t, cnt = _out_proj((yc_p, o_p, sa_p, sb_p, xp), (yc_s, o_s, sa_s, sb_s, xs),
                                             wc, wa, wo, g2, wr, br, tm=TM_DENSE, valid_rows_b=nseq, half=half)

    n_assign = m_total * TOP_K
    keys = route_t[0:TOP_K, :m_total].reshape(-1)
    counts = cnt[0, N_EXPERT_GROUPS:N_EXPERT_GROUPS + N_EXPERTS].astype(jnp.int32)
    order = jnp.pad(jnp.sort(keys) & ((1 << ASSIGN_BITS) - 1), (0, MOE_BLOCK))
    tok = jnp.where(order >= half, order - half, order)
    n_blocks = -(-n_assign // MOE_BLOCK) + N_EXPERTS
    n_blocks += n_blocks % 2
    tables = _block_tables(counts, n_blocks)
    contrib = _moe_experts(tok, order, tables, h2p, w_e_gate[0], w_e_up[0], w_e_down[0],
                           m_total=m_total, half=half)
    y_p = _moe_combine(contrib, x2, route, gf, row_off=0, m=mp, tc=COMBINE_BLOCK, half=half)
    y_s = _moe_combine(contrib, x2, route, gf, row_off=mp, m=nseq, tc=nseq, half=half)

    y_prompt = y_p.reshape(batch, seq, D_MODEL)
    y_sample = y_s.reshape(nseq, 1, D_MODEL)
    conv_state_prompt = ut_p.reshape(batch, bps, 8, D_CONV)[:, -1, 8 - (CONV_WIDTH - 1):, :][None]
    kv_last = kvt_p.reshape(batch, bps, WINDOW, 2 * KV_DIM)[:, -1]
    k_win_prompt = kv_last[:, :, :KV_DIM].reshape(batch, WINDOW, N_KV_HEADS, HEAD_DIM)[None]
    v_win_prompt = kv_last[:, :, KV_DIM:].reshape(batch, WINDOW, N_KV_HEADS, HEAD_DIM)[None]
    conv_state_sample = jnp.concatenate([cache_conv[0][:, 1:, :], u_s[:, None, :]], axis=1)[None]
    k_win_sample = from_keys_minor(kwin_s)
    v_win_sample = from_keys_minor(vwin_s)
    return (y_prompt, y_sample, conv_state_prompt, k_win_prompt, v_win_prompt,
            conv_state_sample, k_win_sample, v_win_sample)
```

```python
import functools
import math

import jax
import jax.numpy as jnp
from jax import lax
from jax.experimental import pallas as pl
from jax.experimental.pallas import tpu as pltpu

D_MODEL = 1024
D_CONV = 1024
CONV_WIDTH = 3
N_HEADS = 16
N_KV_HEADS = 4
HEAD_DIM = 64
GROUP = N_HEADS // N_KV_HEADS
WINDOW = 128
Q_DIM = N_HEADS * HEAD_DIM
KV_DIM = N_KV_HEADS * HEAD_DIM
N_BUCKETS = 32
MAX_DISTANCE = 128
N_EXPERT_GROUPS = 4
EXPERTS_PER_GROUP = 8
N_EXPERTS = N_EXPERT_GROUPS * EXPERTS_PER_GROUP
TOP_K = 2
D_EXPERT = 512
EPS = 1e-6
PAST_LEN = 8192

BF16 = jnp.bfloat16
F32 = jnp.float32
NEG_BIG = -1e30

V7X_VMEM_LIMIT_BYTES = 56 * 1024 * 1024
MOE_VMEM_LIMIT_BYTES = 62 * 1024 * 1024
TILE_ROWS = 8
LANES = 128
ROUTER_LANES = 128
TM_DENSE = 512
TM_IN_PROJ = 1024
OUT_PROJ_PARTS = 2
ATTN_BLOCK = 128
MOE_BLOCK = 256
MOE_BLOCK_LOG2 = 8
ASSIGN_BITS = 16
COMBINE_BLOCK = 256
SAMPLE_SEQ_PER_STEP = 16
HEADS_PER_STORE = LANES // HEAD_DIM


def _const_spec(shape):
    nd = len(shape)
    return pl.BlockSpec(shape, lambda *_: (0,) * nd, pipeline_mode=pl.Buffered(1))


def _rms_norm_f32(xf, g):
    return xf * lax.rsqrt(jnp.mean(xf * xf, axis=-1, keepdims=True) + EPS) * g


def _in_proj_kernel(*refs, tm, sample, blocks_per_seq, u_tail, kv_tail):
    if sample:
        (x_ref, hist0_ref, hist1_ref, g_ref, w_ref,
         cw_ref, yc_ref, q_ref, k_ref, v_ref, sa_ref, sb_ref, ut_ref, kvt_ref) = refs
    else:
        (x_ref, g_ref, w_ref,
         cw_ref, yc_ref, q_ref, k_ref, v_ref, sa_ref, sb_ref, ut_ref, kvt_ref, ubuf_ref) = refs

    h = _rms_norm_f32(x_ref[...], g_ref[...]).astype(BF16)

    widths = (D_CONV, D_CONV, D_CONV, Q_DIM, 2 * KV_DIM, D_MODEL, D_MODEL)
    starts = [sum(widths[:n]) for n in range(len(widths))]
    wcb_ref, wcc_ref, wch_ref, wq_ref, wkv_ref, wga_ref, wgb_ref = [
        w_ref.at[:, pl.ds(a, n)] for a, n in zip(starts, widths)]

    def proj(part_ref):
        return jnp.dot(h, part_ref[...], preferred_element_type=F32)

    u = proj(wcc_ref) * proj(wch_ref)
    w0 = cw_ref[0:1, :]
    w1 = cw_ref[1:2, :]
    w2 = cw_ref[2:3, :]
    if sample:
        conv = w0 * hist0_ref[...] + w1 * hist1_ref[...] + w2 * u
    else:
        @pl.when(pl.program_id(0) % blocks_per_seq == 0)
        def _():
            ubuf_ref[0:8, :] = jnp.zeros((8, D_CONV), F32)

        ubuf_ref[8:8 + tm, :] = u
        conv = w0 * ubuf_ref[6:6 + tm, :] + w1 * ubuf_ref[7:7 + tm, :] + w2 * u
        ubuf_ref[0:8, :] = u[tm - 8:, :]
    yc_ref[...] = (proj(wcb_ref) * conv).astype(BF16)
    ut_ref[0] = u[tm - u_tail:, :]

    q_ref[...] = (proj(wq_ref) * (HEAD_DIM ** -0.5)).astype(BF16)
    kv = proj(wkv_ref)
    k_ref[...] = kv[:, :KV_DIM].astype(BF16)
    v_ref[...] = kv[:, KV_DIM:].astype(BF16)
    kvt_ref[0] = kv[tm - kv_tail:, :]
    sa_ref[...] = jax.nn.sigmoid(proj(wga_ref)).astype(sa_ref.dtype)
    sb_ref[...] = jax.nn.sigmoid(proj(wgb_ref)).astype(sb_ref.dtype)


def _in_proj(x, g1, w_in, conv_w, *, tm, blocks_per_seq, u_tail, kv_tail, hist=None, gate_dtype=BF16):
    m = x.shape[0]
    nblk = m // tm
    sample = hist is not None
    row = lambda width: pl.BlockSpec((tm, width), lambda i: (i, 0))
    in_specs = [row(D_MODEL)]
    args = [x]
    if sample:
        in_specs += [row(D_CONV), row(D_CONV)]
        args += list(hist)
    in_specs += [_const_spec((1, D_MODEL)), _const_spec(w_in.shape), _const_spec(conv_w.shape)]
    args += [g1, w_in, conv_w]
    out_shape = [
        jax.ShapeDtypeStruct((m, D_CONV), BF16),
        jax.ShapeDtypeStruct((m, Q_DIM), BF16),
        jax.ShapeDtypeStruct((m, KV_DIM), BF16),
        jax.ShapeDtypeStruct((m, KV_DIM), BF16),
        jax.ShapeDtypeStruct((m, D_MODEL), gate_dtype),
        jax.ShapeDtypeStruct((m, D_MODEL), gate_dtype),
        jax.ShapeDtypeStruct((nblk, u_tail, D_CONV), F32),
        jax.ShapeDtypeStruct((nblk, kv_tail, 2 * KV_DIM), F32),
    ]
    out_specs = [row(D_CONV), row(Q_DIM), row(KV_DIM), row(KV_DIM), row(D_MODEL), row(D_MODEL),
                 pl.BlockSpec((1, u_tail, D_CONV), lambda i: (i, 0, 0)),
                 pl.BlockSpec((1, kv_tail, 2 * KV_DIM), lambda i: (i, 0, 0))]
    scratch = [] if sample else [pltpu.VMEM((tm + 8, D_CONV), F32)]
    return pl.pallas_call(
        functools.partial(_in_proj_kernel, tm=tm, sample=sample, blocks_per_seq=blocks_per_seq,
                          u_tail=u_tail, kv_tail=kv_tail),
        grid=(nblk,),
        in_specs=in_specs,
        out_specs=out_specs,
        out_shape=out_shape,
        scratch_shapes=scratch,
        compiler_params=pltpu.CompilerParams(dimension_semantics=("arbitrary",),
                                             vmem_limit_bytes=MOE_VMEM_LIMIT_BYTES),
        name="in_proj_sample" if sample else "in_proj_prompt",
    )(*args)


def _attn_prompt_kernel(sink_ref, q_ref, kc_ref, kp_ref, vc_ref, vp_ref, bias_ref, o_ref):
    first = pl.program_id(1) == 0
    col = lax.broadcasted_iota(jnp.int32, (ATTN_BLOCK, 2 * ATTN_BLOCK), 1)
    no_prev = jnp.logical_and(first, col < ATTN_BLOCK)
    for g in range(N_KV_HEADS):
        ks = slice(g * HEAD_DIM, (g + 1) * HEAD_DIM)
        kcat = jnp.concatenate([kp_ref[:, ks], kc_ref[:, ks]], axis=0)
        vcat = jnp.concatenate([vp_ref[:, ks], vc_ref[:, ks]], axis=0)
        for h0 in range(g * GROUP, (g + 1) * GROUP, HEADS_PER_STORE):
            outs = []
            for h in range(h0, h0 + HEADS_PER_STORE):
                hs = slice(h * HEAD_DIM, (h + 1) * HEAD_DIM)
                s = lax.dot_general(q_ref[:, hs], kcat, (((1,), (1,)), ((), ())),
                                    preferred_element_type=F32)
                s = jnp.where(no_prev, NEG_BIG, s + bias_ref[h])
                sink = sink_ref[h]
                m = jnp.maximum(jnp.max(s, axis=-1, keepdims=True), sink)
                p = jnp.exp(s - m)
                denom = jnp.sum(p, axis=-1, keepdims=True) + jnp.exp(sink - m)
                o = jnp.dot(p.astype(BF16), vcat, preferred_element_type=F32)
                outs.append((o / denom).astype(BF16))
            o_ref[:, h0 * HEAD_DIM:(h0 + HEADS_PER_STORE) * HEAD_DIM] = jnp.concatenate(outs, axis=1)


def _attn_prompt(q, k, v, bias, sinks, batch, seq):
    nb = seq // ATTN_BLOCK
    cur = lambda b, i: (b * nb + i, 0)
    prev = lambda b, i: (b * nb + jnp.maximum(i - 1, 0), 0)
    return pl.pallas_call(
        _attn_prompt_kernel,
        grid=(batch, nb),
        in_specs=[pl.BlockSpec(memory_space=pltpu.SMEM),
                  pl.BlockSpec((ATTN_BLOCK, Q_DIM), cur),
                  pl.BlockSpec((ATTN_BLOCK, KV_DIM), cur),
                  pl.BlockSpec((ATTN_BLOCK, KV_DIM), prev),
                  pl.BlockSpec((ATTN_BLOCK, KV_DIM), cur),
                  pl.BlockSpec((ATTN_BLOCK, KV_DIM), prev),
                  _const_spec(bias.shape)],
        out_specs=pl.BlockSpec((ATTN_BLOCK, Q_DIM), cur),
        out_shape=jax.ShapeDtypeStruct((batch * seq, Q_DIM), BF16),
        compiler_params=pltpu.CompilerParams(dimension_semantics=("arbitrary", "arbitrary"),
                                             vmem_limit_bytes=V7X_VMEM_LIMIT_BYTES),
        name="attn_prompt",
    )(sinks, q, k, k, v, v, bias)


def _attn_sample_kernel(qbd_ref, ckt_ref, cvt_ref, kvn_ref, bias_ref, sink_ref, mask_ref,
                        o_ref, kwin_ref, vwin_ref, *, w_buf):
    bf16_round = lambda t: t.astype(BF16).astype(F32)
    seqs = range(SAMPLE_SEQ_PER_STEP)
    sink = sink_ref[...]
    newest = lax.broadcasted_iota(jnp.int32, (KV_DIM, w_buf), 1) == w_buf - 1
    kvn_t = kvn_ref[...].T
    for b in seqs:
        kwin_ref[b] = jnp.where(newest, kvn_t[:KV_DIM, b:b + 1], pltpu.roll(ckt_ref[b], w_buf - 1, axis=1))
        vwin_ref[b] = jnp.where(newest, kvn_t[KV_DIM:, b:b + 1], pltpu.roll(cvt_ref[b], w_buf - 1, axis=1))
    s = [jnp.dot(qbd_ref[b], ckt_ref[b].astype(BF16), preferred_element_type=F32) + bias_ref[:, :w_buf]
         for b in seqs]
    s_new = [jnp.sum(qbd_ref[b].astype(F32) * bf16_round(kvn_ref[b:b + 1, :KV_DIM]), axis=-1, keepdims=True)
             + bias_ref[:, w_buf:w_buf + 1] for b in seqs]
    m = [jnp.maximum(jnp.maximum(jnp.max(s[b], axis=-1, keepdims=True), s_new[b]), sink) for b in seqs]
    p = [jnp.exp(s[b] - m[b]) for b in seqs]
    p_new = [jnp.exp(s_new[b] - m[b]) for b in seqs]
    denom = [jnp.sum(p[b], axis=-1, keepdims=True) + p_new[b] + jnp.exp(sink - m[b]) for b in seqs]
    of = [lax.dot_general((p[b] / denom[b]).astype(BF16), cvt_ref[b].astype(BF16), (((1,), (1,)), ((), ())),
                          preferred_element_type=F32)
          + bf16_round(p_new[b] / denom[b]) * bf16_round(kvn_ref[b:b + 1, KV_DIM:]) for b in seqs]
    for b in seqs:
        ob = of[b] * mask_ref[...]
        o_ref[b] = (ob[:, 0:HEAD_DIM] + ob[:, HEAD_DIM:2 * HEAD_DIM]
                    + ob[:, 2 * HEAD_DIM:3 * HEAD_DIM] + ob[:, 3 * HEAD_DIM:]).astype(BF16)


def _attn_sample(qbd, ckt, cvt, kvn, bias, sink_col, head_mask):
    nseq, w_buf = ckt.shape[0], ckt.shape[2]
    sb = SAMPLE_SEQ_PER_STEP
    seq3 = lambda d1, d2: pl.BlockSpec((sb, d1, d2), lambda i: (i, 0, 0))
    win = jax.ShapeDtypeStruct((nseq, KV_DIM, w_buf), F32)
    return pl.pallas_call(
        functools.partial(_attn_sample_kernel, w_buf=w_buf),
        grid=(nseq // sb,),
        in_specs=[seq3(N_HEADS, KV_DIM), seq3(KV_DIM, w_buf), seq3(KV_DIM, w_buf),
                  pl.BlockSpec((sb, 2 * KV_DIM), lambda i: (i, 0)),
                  _const_spec(bias.shape), _const_spec(sink_col.shape), _const_spec(head_mask.shape)],
        out_specs=[seq3(N_HEADS, HEAD_DIM), seq3(KV_DIM, w_buf), seq3(KV_DIM, w_buf)],
        out_shape=[jax.ShapeDtypeStruct((nseq, N_HEADS, HEAD_DIM), BF16), win, win],
        compiler_params=pltpu.CompilerParams(dimension_semantics=("arbitrary",),
                                             vmem_limit_bytes=V7X_VMEM_LIMIT_BYTES),
        name="attn_sample",
    )(qbd, ckt, cvt, kvn, bias, sink_col, head_mask)


def _route_rows(logits, row0, valid_rows, half):
    tm = logits.shape[0]
    lane = lax.broadcasted_iota(jnp.int32, logits.shape, 1)
    lane_f = lane.astype(F32)
    no_lane = float(ROUTER_LANES)

    def top1(mask):
        best = jnp.max(jnp.where(mask, logits, -jnp.inf), axis=-1, keepdims=True)
        idx = jnp.min(jnp.where(jnp.logical_and(mask, logits == best), lane_f, no_lane), axis=-1, keepdims=True)
        return best, idx

    gmask = lane < N_EXPERT_GROUPS
    gmax, grp = top1(gmask)
    gsum = jnp.sum(jnp.where(gmask, jnp.exp(logits - gmax), 0.0), axis=-1, keepdims=True)
    p_grp = 1.0 / gsum
    lo = N_EXPERT_GROUPS + EXPERTS_PER_GROUP * grp
    emask = jnp.logical_and(lane_f >= lo, lane_f < lo + EXPERTS_PER_GROUP)
    v1, i1 = top1(emask)
    v2, i2 = top1(jnp.logical_and(emask, lane_f != i1))
    e21 = jnp.exp(v2 - v1)
    w1 = p_grp / (1.0 + e21)
    w2 = p_grp * e21 / (1.0 + e21)

    oh1 = lane_f == i1
    oh2 = lane_f == i2
    if valid_rows < tm:
        valid = lax.broadcasted_iota(jnp.int32, logits.shape, 0) < valid_rows
        oh1 = jnp.logical_and(oh1, valid)
        oh2 = jnp.logical_and(oh2, valid)
    oh = oh1.astype(F32) + oh2.astype(F32)
    token = row0 + lax.broadcasted_iota(jnp.int32, (tm, 1), 0)
    key1 = (i1.astype(jnp.int32) - N_EXPERT_GROUPS) * (1 << ASSIGN_BITS) + token
    key2 = (i2.astype(jnp.int32) - N_EXPERT_GROUPS) * (1 << ASSIGN_BITS) + token + half
    w1b = lax.bitcast_convert_type(w1, jnp.int32)
    w2b = lax.bitcast_convert_type(w2, jnp.int32)
    words = jnp.where(lane == 0, key1, jnp.where(lane == 1, key2, jnp.where(lane == 2, w1b,
                      jnp.where(lane == 3, w2b, 0))))
    return words, jnp.sum(oh, axis=0, keepdims=True)


def _store_token_tiles(ref, x):
    n = x.shape[0]
    for c in range(D_MODEL // LANES):
        ref[pl.ds(c, n, stride=TILE_ROWS), :] = x[:, c * LANES:(c + 1) * LANES]


def _load_token_tiles(ref, n):
    return jnp.concatenate([ref[pl.ds(c, n, stride=TILE_ROWS), :] for c in range(D_MODEL // LANES)], axis=1)


def _pack_bf16_pairs(x):
    hw = x.shape[1] // 2
    bits = lambda v: lax.bitcast_convert_type(v.astype(BF16).astype(F32), jnp.uint32)
    return (bits(x[:, hw:]) & jnp.uint32(0xFFFF0000)) | (bits(x[:, :hw]) >> 16)


def _unpack_bf16_pairs(w):
    lo = lax.bitcast_convert_type(w << 16, F32)
    hi = lax.bitcast_convert_type(w & jnp.uint32(0xFFFF0000), F32)
    return jnp.concatenate([lo, hi], axis=1).astype(BF16)


def _out_proj_rows(yc_ref, o_ref, sa_ref, sb_ref, x_ref, wc_ref, wa_ref, wo_ref, g2_ref, wr_ref, br_ref,
                   x2_ref, h2p_ref, route_ref, keys_ref, cnt_ref, *, valid_rows, half):
    tm = x_ref.shape[0]
    parts = [(r0, tm // OUT_PROJ_PARTS) for r0 in range(0, tm, tm // OUT_PROJ_PARTS)]
    for r0, n in parts:
        rows = pl.ds(r0, n)
        y_conv = jnp.dot(yc_ref[rows, :], wc_ref[...], preferred_element_type=F32)
        y_attn = jnp.dot(o_ref[rows, :], wa_ref[...], preferred_element_type=F32)
        mix = (sa_ref[rows, :].astype(F32) * y_conv + sb_ref[rows, :].astype(F32) * y_attn).astype(BF16)
        x2_ref[rows, :] = x_ref[rows, :] + jnp.dot(mix, wo_ref[...], preferred_element_type=F32)
    for r0, n in parts:
        rows = pl.ds(r0, n)
        h2 = _rms_norm_f32(x2_ref[rows, :], g2_ref[...])
        h2p_ref[rows, :] = _pack_bf16_pairs(h2)
        logits = jnp.dot(h2.astype(BF16), wr_ref[...], preferred_element_type=F32) + br_ref[...]
        words, cnt = _route_rows(logits, pl.program_id(0) * tm + r0, min(max(valid_rows - r0, 0), n), half)
        route_ref[rows, :] = words
        keys_ref[:, rows] = words.T[:TILE_ROWS, :]
        cnt_ref[...] += cnt


def _out_proj_kernel(*refs, n_first, valid_rows_second, half):
    first, second, shared = refs[0:5], refs[5:10], refs[10:]
    cnt_ref = shared[-1]
    tm = first[4].shape[0]

    @pl.when(pl.program_id(0) == 0)
    def _():
        cnt_ref[...] = jnp.zeros_like(cnt_ref)

    @pl.when(pl.program_id(0) < n_first)
    def _():
        _out_proj_rows(*first, *shared, valid_rows=tm, half=half)

    @pl.when(pl.program_id(0) >= n_first)
    def _():
        _out_proj_rows(*second, *shared, valid_rows=valid_rows_second, half=half)


def _out_proj(acts_a, acts_b, wc, wa, wo, g2, wr, br, *, tm, valid_rows_b, half):
    na = acts_a[4].shape[0] // tm
    nb = acts_b[4].shape[0] // tm
    assert nb == 1
    m_total = (na + nb) * tm
    spec_a = lambda width: pl.BlockSpec((tm, width), lambda i: (jnp.minimum(i, na - 1), 0))
    spec_b = lambda width: pl.BlockSpec((tm, width), lambda i: (jnp.maximum(i - na, 0), 0))
    widths = (D_CONV, Q_DIM, D_MODEL, D_MODEL, D_MODEL)
    in_specs = [spec_a(w) for w in widths] + [spec_b(w) for w in widths]
    in_specs += [_const_spec(wc.shape), _const_spec(wa.shape), _const_spec(wo.shape),
                 _const_spec(g2.shape), _const_spec(wr.shape), _const_spec(br.shape)]
    orow = lambda width: pl.BlockSpec((tm, width), lambda i: (i, 0))
    return pl.pallas_call(
        functools.partial(_out_proj_kernel, n_first=na, valid_rows_second=valid_rows_b, half=half),
        grid=(na + nb,),
        in_specs=in_specs,
        out_specs=[orow(D_MODEL), orow(D_MODEL // 2), orow(ROUTER_LANES),
                   pl.BlockSpec((TILE_ROWS, tm), lambda i: (0, i)),
                   pl.BlockSpec((1, ROUTER_LANES), lambda i: (0, 0))],
        out_shape=[jax.ShapeDtypeStruct((m_total, D_MODEL), F32),
                   jax.ShapeDtypeStruct((m_total, D_MODEL // 2), jnp.uint32),
                   jax.ShapeDtypeStruct((m_total, ROUTER_LANES), jnp.int32),
                   jax.ShapeDtypeStruct((TILE_ROWS, m_total), jnp.int32),
                   jax.ShapeDtypeStruct((1, ROUTER_LANES), F32)],
        compiler_params=pltpu.CompilerParams(dimension_semantics=("arbitrary",),
                                             vmem_limit_bytes=V7X_VMEM_LIMIT_BYTES),
        name="out_proj",
    )(*acts_a, *acts_b, wc, wa, wo, g2, wr, br)


def _block_table_kernel(counts_ref, bexp_ref, bpos_ref, bcnt_ref, bslot_ref, bnext_ref, nused_ref, first_ref,
                        *, n_blocks):
    def per_expert(e, carry):
        blk0, pos0, ordinal = carry
        cnt = counts_ref[e]
        nblk = lax.shift_right_logical(cnt + (MOE_BLOCK - 1), MOE_BLOCK_LOG2)
        first_ref[e] = jnp.where(nblk > 0, blk0, -1)

        def mark(b, c):
            off = (b - blk0) * MOE_BLOCK
            bexp_ref[b] = e
            bpos_ref[b] = pos0 + off
            bcnt_ref[b] = jnp.minimum(cnt - off, MOE_BLOCK)
            bslot_ref[b] = jnp.where(b == blk0, ordinal & 1, -1)
            bnext_ref[b] = -1
            return c
        lax.fori_loop(blk0, blk0 + nblk, mark, 0)
        return blk0 + nblk, pos0 + cnt, ordinal + jnp.where(nblk > 0, 1, 0)

    n_used, _, _ = lax.fori_loop(0, N_EXPERTS, per_expert, (0, 0, 0))
    nused_ref[0] = n_used

    def unused(b, c):
        bexp_ref[b] = N_EXPERTS - 1
        bpos_ref[b] = 0
        bcnt_ref[b] = 0
        bslot_ref[b] = -1
        bnext_ref[b] = -1
        return c
    lax.fori_loop(n_used, n_blocks, unused, 0)

    def link(k, nxt):
        e = N_EXPERTS - 1 - k
        fb = first_ref[e]

        @pl.when(fb >= 0)
        def _():
            bnext_ref[fb] = nxt
        return jnp.where(fb >= 0, e, nxt)
    lax.fori_loop(0, N_EXPERTS, link, -1)


def _block_tables(counts, n_blocks):
    smem = pl.BlockSpec(memory_space=pltpu.SMEM)
    blk = jax.ShapeDtypeStruct((n_blocks,), jnp.int32)
    return pl.pallas_call(
        functools.partial(_block_table_kernel, n_blocks=n_blocks),
        in_specs=[smem],
        out_specs=[smem] * 6,
        out_shape=[blk] * 5 + [jax.ShapeDtypeStruct((1,), jnp.int32)],
        scratch_shapes=[pltpu.SMEM((N_EXPERTS,), jnp.int32)],
        name="block_tables",
    )(counts)


def _moe_kernel(tok_ref, dst_ref, bexp_ref, bslot_ref, bnext_ref, nused_ref,
                h2p_ref, wg_hbm, wu_hbm, wd_hbm, contrib_hbm,
                xs_ref, ys_ref, wgf_ref, wuf_ref, wdf_ref, wgb_ref, wub_ref, wdb_ref, ssem_ref, wsem_ref,
                *, m_total, half, n_blocks):
    step = pl.program_id(0)
    n_used = nused_ref[0]
    tile = lambda t: pl.ds(pl.multiple_of(t * TILE_ROWS, TILE_ROWS), TILE_ROWS)

    def gather(blk, slt):
        for r in range(MOE_BLOCK):
            xs_ref[slt, pl.ds(r, 1), :] = h2p_ref[pl.ds(tok_ref[blk * MOE_BLOCK + r], 1), :]

    def scatter(blk, slt, r):
        return pltpu.make_async_copy(ys_ref.at[slt, tile(r), :], contrib_hbm.at[tile(dst_ref[blk * MOE_BLOCK + r]), :],
                                     ssem_ref.at[slt])

    def scatter_wait(slt, r):
        pltpu.make_async_copy(ys_ref.at[slt, tile(r), :], contrib_hbm.at[tile(0), :], ssem_ref.at[slt]).wait()

    @pl.when(step == 0)
    def _():
        ys_ref[...] = jnp.zeros_like(ys_ref)
        gap = half - m_total
        fills = [(2 * half + s * MOE_BLOCK, MOE_BLOCK) for s in range(2)]
        fills += [(k * half + m_total, gap) for k in range(TOP_K)] if gap else []
        for start, n in fills:
            fill = pltpu.make_async_copy(ys_ref.at[0, pl.ds(0, n * TILE_ROWS), :],
                                         contrib_hbm.at[pl.ds(start * TILE_ROWS, n * TILE_ROWS), :], ssem_ref.at[0])
            fill.start()
            fill.wait()

    def weight_copies(e, s):
        return [pltpu.make_async_copy(src.at[e], dst.at[s], wsem_ref.at[s])
                for src, dst in ((wg_hbm, wgf_ref), (wu_hbm, wuf_ref), (wd_hbm, wdf_ref))]

    @pl.when(step == 0)
    def _():
        for c in weight_copies(bexp_ref[0], 0):
            c.start()
        gather(0, 0)

    def wait_scatters(slt):
        for r in range(MOE_BLOCK):
            scatter_wait(slt, r)

    def expert_weights(i):
        wslot = bslot_ref[i]

        @pl.when(wslot >= 0)
        def _():
            for c in weight_copies(0, wslot):
                c.wait()

        @pl.when(jnp.logical_and(wslot >= 0, bnext_ref[i] >= 0))
        def _():
            for c in weight_copies(bnext_ref[i], 1 - wslot):
                c.start()

        @pl.when(wslot >= 0)
        def _():
            wgb_ref[...] = wgf_ref[wslot].astype(BF16)
            wub_ref[...] = wuf_ref[wslot].astype(BF16)
            wdb_ref[...] = wdf_ref[wslot].astype(BF16)

    def run_block(i, slt):
        gather(jnp.minimum(i + 1, n_blocks - 1), 1 - slt)
        xb = _unpack_bf16_pairs(xs_ref[slt])
        gate = jnp.dot(xb, wgb_ref[...], preferred_element_type=F32)
        up = jnp.dot(xb, wub_ref[...], preferred_element_type=F32)
        hmid = (jax.nn.silu(gate) * up).astype(BF16)
        _store_token_tiles(ys_ref.at[slt], jnp.dot(hmid, wdb_ref[...], preferred_element_type=F32))
        for r in range(MOE_BLOCK):
            scatter(i, slt, r).start(priority=r % 2)

    for slt in range(2):
        i = 2 * step + slt
        active = i < n_used
        last_active = i == n_used - 1
        pl.when(jnp.logical_and(active, i >= 2))(functools.partial(wait_scatters, slt))
        expert_weights(i)
        pl.when(active)(functools.partial(run_block, i, slt))
        pl.when(last_active)(functools.partial(wait_scatters, slt))
        pl.when(jnp.logical_and(last_active, i >= 1))(functools.partial(wait_scatters, 1 - slt))


def _row_table_kernel(bpos_ref, bcnt_ref, order_ref, tok_ref, dst_ref, *, half, n_blocks):
    rows_per_block = MOE_BLOCK // LANES
    lane = lax.broadcasted_iota(jnp.int32, (rows_per_block, LANES), 1)
    r = lax.broadcasted_iota(jnp.int32, (rows_per_block, LANES), 0) * LANES + lane

    def per_block(b, c):
        p = bpos_ref[b]
        sh = p & (LANES - 1)
        win = order_ref[pl.ds(lax.shift_right_logical(p, 7), rows_per_block + 1), :]
        rolled = pltpu.roll(win, (LANES - sh) & (LANES - 1), axis=1)
        assign = jnp.where(lane < LANES - sh, rolled[:rows_per_block], rolled[1:])
        tok_ref[pl.ds(b * rows_per_block, rows_per_block), :] = jnp.where(assign >= half, assign - half, assign)
        trash = 2 * half + (b & 1) * MOE_BLOCK + r
        dst_ref[pl.ds(b * rows_per_block, rows_per_block), :] = jnp.where(r < bcnt_ref[b], assign, trash)
        return c
    lax.fori_loop(0, n_blocks, per_block, 0)


def _row_tables(order, bpos, bcnt, half):
    n_blocks = bpos.shape[0]
    assert MOE_BLOCK % LANES == 0 and order.shape[0] % LANES == 0
    out = jax.ShapeDtypeStruct((n_blocks * MOE_BLOCK // LANES, LANES), jnp.int32)
    tok, dst = pl.pallas_call(
        functools.partial(_row_table_kernel, half=half, n_blocks=n_blocks),
        grid_spec=pltpu.PrefetchScalarGridSpec(
            num_scalar_prefetch=2, grid=(1,),
            in_specs=[pl.BlockSpec((order.shape[0] // LANES, LANES), lambda i, *_: (0, 0))],
            out_specs=[pl.BlockSpec(out.shape, lambda i, *_: (0, 0))] * 2),
        out_shape=[out, out],
        name="row_tables",
    )(bpos, bcnt, order.reshape(-1, LANES))
    return tok.reshape(-1), dst.reshape(-1)


def _moe_experts(order, tables, h2p, w_gate, w_up, w_down, *, m_total, half):
    bexp, bpos, bcnt, bslot, bnext, n_used = tables
    n_blocks = bexp.shape[0]
    assert 0 <= half - m_total <= MOE_BLOCK and n_blocks % 2 == 0
    tok, dst = _row_tables(order, bpos, bcnt, half)
    tables = (bexp, bslot, bnext, n_used)
    hbm = pl.BlockSpec(memory_space=pl.ANY)
    grid_spec = pltpu.PrefetchScalarGridSpec(
        num_scalar_prefetch=2 + len(tables),
        grid=(n_blocks // 2,),
        in_specs=[_const_spec(h2p.shape), hbm, hbm, hbm],
        out_specs=hbm,
        scratch_shapes=[pltpu.VMEM((2, MOE_BLOCK, D_MODEL // 2), jnp.uint32),
                        pltpu.VMEM((2, MOE_BLOCK * TILE_ROWS, LANES), F32),
                        pltpu.VMEM((2, D_MODEL, D_EXPERT), F32),
                        pltpu.VMEM((2, D_MODEL, D_EXPERT), F32),
                        pltpu.VMEM((2, D_EXPERT, D_MODEL), F32),
                        pltpu.VMEM((D_MODEL, D_EXPERT), BF16),
                        pltpu.VMEM((D_MODEL, D_EXPERT), BF16),
                        pltpu.VMEM((D_EXPERT, D_MODEL), BF16),
                        pltpu.SemaphoreType.DMA((2,)),
                        pltpu.SemaphoreType.DMA((2,))],
    )
    return pl.pallas_call(
        functools.partial(_moe_kernel, m_total=m_total, half=half, n_blocks=n_blocks),
        grid_spec=grid_spec,
        out_shape=jax.ShapeDtypeStruct(((2 * half + 2 * MOE_BLOCK) * TILE_ROWS, LANES), F32),
        compiler_params=pltpu.CompilerParams(dimension_semantics=("arbitrary",),
                                             vmem_limit_bytes=MOE_VMEM_LIMIT_BYTES),
        name="moe_experts",
    )(tok, dst, *tables, h2p, w_gate, w_up, w_down)


def _combine_kernel(c0_ref, c1_ref, x2_ref, route_ref, gf_ref, y_ref):
    tc = x2_ref.shape[0]
    w = lax.bitcast_convert_type(route_ref[:, TOP_K:2 * TOP_K], F32)
    moe = w[:, 0:1] * _load_token_tiles(c0_ref, tc) + w[:, 1:2] * _load_token_tiles(c1_ref, tc)
    y_ref[...] = _rms_norm_f32(x2_ref[...] + moe, gf_ref[...])


def _moe_combine(contrib, x2, route, gf, *, row_off, m, tc, half):
    off = row_off // tc
    assert row_off % tc == 0 and half % tc == 0
    ctile = lambda k: pl.BlockSpec((tc * TILE_ROWS, LANES), lambda i: (i + off + k * (half // tc), 0))
    return pl.pallas_call(
        _combine_kernel,
        grid=(m // tc,),
        in_specs=[ctile(0), ctile(1),
                  pl.BlockSpec((tc, D_MODEL), lambda i: (i + off, 0)),
                  pl.BlockSpec((tc, ROUTER_LANES), lambda i: (i + off, 0)),
                  _const_spec((1, D_MODEL))],
        out_specs=pl.BlockSpec((tc, D_MODEL), lambda i: (i, 0)),
        out_shape=jax.ShapeDtypeStruct((m, D_MODEL), F32),
        compiler_params=pltpu.CompilerParams(dimension_semantics=("arbitrary",),
                                             vmem_limit_bytes=V7X_VMEM_LIMIT_BYTES),
        name="moe_combine_prompt" if row_off == 0 else "moe_combine_sample",
    )(contrib, contrib, x2, route, gf)


def _t5_bucket(dist):
    n = jnp.maximum(dist, 0)
    max_exact = N_BUCKETS // 2
    nf = jnp.maximum(n, 1).astype(F32)
    large = max_exact + (jnp.log(nf / max_exact) / math.log(MAX_DISTANCE / max_exact)
                         * (N_BUCKETS - max_exact)).astype(jnp.int32)
    large = jnp.minimum(large, N_BUCKETS - 1)
    return jnp.where(n < max_exact, n, large)


def _bucket_bias(rel_bias, dist, valid):
    buckets = _t5_bucket(dist).reshape(1, -1)
    onehot = (buckets == jnp.arange(N_BUCKETS, dtype=jnp.int32)[:, None]).astype(F32)
    bias = jnp.dot(rel_bias.astype(F32).T, onehot, precision=lax.Precision.HIGHEST)
    return jnp.where(valid.reshape(1, -1), bias, NEG_BIG).reshape((rel_bias.shape[1],) + dist.shape)


def _prompt_bias_table(rel_bias):
    qi = jnp.arange(ATTN_BLOCK, dtype=jnp.int32)[:, None]
    kj = jnp.arange(2 * ATTN_BLOCK, dtype=jnp.int32)[None, :] - ATTN_BLOCK
    dist = qi - kj
    return _bucket_bias(rel_bias, dist, (dist >= 0) & (dist <= WINDOW))


def _sample_bias_table(rel_bias, w_buf):
    dist = w_buf - jnp.arange(w_buf + 1, dtype=jnp.int32)
    return _bucket_bias(rel_bias, dist, dist <= WINDOW)


def kernel(x_prompt, x_sample, cache_conv, cache_k, cache_v, norm1_g, w_in, conv_w, w_conv_out, w_attn_out, w_o, sinks, rel_bias, norm2_g, w_router_group, b_router_group, w_router_expert, b_router_expert, w_e_gate, w_e_up, w_e_down, norm_f_g):
    assert norm1_g.shape[0] == 1, "single-layer configuration"
    batch, seq, _ = x_prompt.shape
    nseq = x_sample.shape[0]
    w_buf = cache_k.shape[2]
    mp = batch * seq
    m_total = mp + nseq
    assert seq % TM_DENSE == 0 and seq % TM_IN_PROJ == 0 and seq % ATTN_BLOCK == 0 and mp % COMBINE_BLOCK == 0
    assert nseq % SAMPLE_SEQ_PER_STEP == 0 and mp % nseq == 0
    assert TOP_K == 2 and MOE_BLOCK == 1 << MOE_BLOCK_LOG2 and m_total * TOP_K < 1 << ASSIGN_BITS

    g1 = norm1_g[0][None, :]
    g2 = norm2_g[0][None, :]
    gf = norm_f_g[None, :]
    wi = w_in[0].astype(BF16)
    cw = conv_w[0]
    wc = w_conv_out[0].astype(BF16)
    wa = w_attn_out[0].astype(BF16)
    wo = w_o[0].astype(BF16)
    pad_cols = ROUTER_LANES - N_EXPERT_GROUPS - N_EXPERTS
    wr = jnp.concatenate([w_router_group[0], w_router_expert[0],
                          jnp.zeros((D_MODEL, pad_cols), F32)], axis=1).astype(BF16)
    br = jnp.concatenate([b_router_group[0], b_router_expert[0], jnp.zeros((pad_cols,), F32)])[None, :]
    sink = sinks[0].astype(F32)

    xp = x_prompt.reshape(mp, D_MODEL)
    bps = seq // TM_IN_PROJ
    yc_p, q_p, k_p, v_p, sa_p, sb_p, ut_p, kvt_p = _in_proj(
        xp, g1, wi, cw, tm=TM_IN_PROJ, blocks_per_seq=bps, u_tail=8, kv_tail=WINDOW)
    o_p = _attn_prompt(q_p, k_p, v_p, _prompt_bias_table(rel_bias), sink, batch, seq)

    pad_rows = lambda t: jnp.pad(t, ((0, TM_DENSE - nseq), (0, 0)))
    xs = pad_rows(x_sample.reshape(nseq, D_MODEL))
    hist = (pad_rows(cache_conv[0][:, 0, :]), pad_rows(cache_conv[0][:, 1, :]))
    yc_s, q_s, _, _, sa_s, sb_s, ut_s, kvt_s = _in_proj(
        xs, g1, wi, cw, tm=TM_DENSE, blocks_per_seq=1, u_tail=TM_DENSE, kv_tail=TM_DENSE, hist=hist,
        gate_dtype=F32)
    u_s = ut_s[0, :nseq]
    kv_s = kvt_s[0, :nseq]
    head_mask = (jnp.arange(KV_DIM)[None, :] // HEAD_DIM == jnp.arange(N_HEADS)[:, None] // GROUP)
    qbd = (jnp.tile(q_s[:nseq].reshape(nseq, N_HEADS, HEAD_DIM), (1, 1, N_KV_HEADS))
           * head_mask[None].astype(BF16))
    to_keys_minor = lambda c: jnp.transpose(c.reshape(nseq, w_buf, KV_DIM), (0, 2, 1))
    from_keys_minor = lambda c: jnp.transpose(c, (0, 2, 1)).reshape(1, nseq, w_buf, N_KV_HEADS, HEAD_DIM)
    o_s, kwin_s, vwin_s = _attn_sample(qbd, to_keys_minor(cache_k[0]), to_keys_minor(cache_v[0]), kv_s,
                                       _sample_bias_table(rel_bias, w_buf), sink[:, None],
                                       head_mask.astype(F32))
    o_s = pad_rows(o_s.reshape(nseq, Q_DIM))

    half = -(-m_total // COMBINE_BLOCK) * COMBINE_BLOCK
    assert half % nseq == 0 and TOP_K * half < 1 << ASSIGN_BITS
    x2, h2p, route, route_t, cnt = _out_proj((yc_p, o_p, sa_p, sb_p, xp), (yc_s, o_s, sa_s, sb_s, xs),
                                             wc, wa, wo, g2, wr, br, tm=TM_DENSE, valid_rows_b=nseq, half=half)

    n_assign = m_total * TOP_K
    keys = route_t[0:TOP_K, :m_total].reshape(-1)
    counts = cnt[0, N_EXPERT_GROUPS:N_EXPERT_GROUPS + N_EXPERTS].astype(jnp.int32)
    order = jnp.pad(jnp.sort(keys) & ((1 << ASSIGN_BITS) - 1), (0, MOE_BLOCK))
    n_blocks = -(-n_assign // MOE_BLOCK) + N_EXPERTS
    n_blocks += n_blocks % 2
    tables = _block_tables(counts, n_blocks)
    contrib = _moe_experts(order, tables, h2p, w_e_gate[0], w_e_up[0], w_e_down[0],
                           m_total=m_total, half=half)
    y_p = _moe_combine(contrib, x2, route, gf, row_off=0, m=mp, tc=COMBINE_BLOCK, half=half)
    y_s = _moe_combine(contrib, x2, route, gf, row_off=mp, m=nseq, tc=nseq, half=half)

    y_prompt = y_p.reshape(batch, seq, D_MODEL)
    y_sample = y_s.reshape(nseq, 1, D_MODEL)
    conv_state_prompt = ut_p.reshape(batch, bps, 8, D_CONV)[:, -1, 8 - (CONV_WIDTH - 1):, :][None]
    kv_last = kvt_p.reshape(batch, bps, WINDOW, 2 * KV_DIM)[:, -1]
    k_win_prompt = kv_last[:, :, :KV_DIM].reshape(batch, WINDOW, N_KV_HEADS, HEAD_DIM)[None]
    v_win_prompt = kv_last[:, :, KV_DIM:].reshape(batch, WINDOW, N_KV_HEADS, HEAD_DIM)[None]
    conv_state_sample = jnp.concatenate([cache_conv[0][:, 1:, :], u_s[:, None, :]], axis=1)[None]
    k_win_sample = from_keys_minor(kwin_s)
    v_win_sample = from_keys_minor(vwin_s)
    return (y_prompt, y_sample, conv_state_prompt, k_win_prompt, v_win_prompt,
            conv_state_sample, k_win_sample, v_win_sample)
```

```python
import functools
import math

import jax
import jax.numpy as jnp
from jax import lax
from jax.experimental import pallas as pl
from jax.experimental.pallas import tpu as pltpu

D_MODEL = 1024
D_CONV = 1024
CONV_WIDTH = 3
N_HEADS = 16
N_KV_HEADS = 4
HEAD_DIM = 64
GROUP = N_HEADS // N_KV_HEADS
WINDOW = 128
Q_DIM = N_HEADS * HEAD_DIM
KV_DIM = N_KV_HEADS * HEAD_DIM
N_BUCKETS = 32
MAX_DISTANCE = 128
N_EXPERT_GROUPS = 4
EXPERTS_PER_GROUP = 8
N_EXPERTS = N_EXPERT_GROUPS * EXPERTS_PER_GROUP
TOP_K = 2
D_EXPERT = 512
EPS = 1e-6
PAST_LEN = 8192

BF16 = jnp.bfloat16
F32 = jnp.float32
NEG_BIG = -1e30

V7X_VMEM_LIMIT_BYTES = 56 * 1024 * 1024
MOE_VMEM_LIMIT_BYTES = 62 * 1024 * 1024
TILE_ROWS = 8
LANES = 128
ROUTER_LANES = 128
TM_DENSE = 512
TM_IN_PROJ = 1024
OUT_PROJ_PARTS = 2
ATTN_BLOCK = 128
MOE_BLOCK = 256
MOE_BLOCK_LOG2 = 8
ASSIGN_BITS = 16
COMBINE_BLOCK = 256
SAMPLE_SEQ_PER_STEP = 16
HEADS_PER_STORE = LANES // HEAD_DIM


def _const_spec(shape):
    nd = len(shape)
    return pl.BlockSpec(shape, lambda *_: (0,) * nd, pipeline_mode=pl.Buffered(1))


def _rms_norm_f32(xf, g):
    return xf * lax.rsqrt(jnp.mean(xf * xf, axis=-1, keepdims=True) + EPS) * g


def _in_proj_kernel(*refs, tm, sample, blocks_per_seq, u_tail, kv_tail):
    if sample:
        (x_ref, hist0_ref, hist1_ref, g_ref, w_ref,
         cw_ref, yc_ref, q_ref, k_ref, v_ref, sa_ref, sb_ref, ut_ref, kvt_ref) = refs
    else:
        (x_ref, g_ref, w_ref,
         cw_ref, yc_ref, q_ref, k_ref, v_ref, sa_ref, sb_ref, ut_ref, kvt_ref, ubuf_ref) = refs

    h = _rms_norm_f32(x_ref[...], g_ref[...]).astype(BF16)

    widths = (D_CONV, D_CONV, D_CONV, Q_DIM, 2 * KV_DIM, D_MODEL, D_MODEL)
    starts = [sum(widths[:n]) for n in range(len(widths))]
    wcb_ref, wcc_ref, wch_ref, wq_ref, wkv_ref, wga_ref, wgb_ref = [
        w_ref.at[:, pl.ds(a, n)] for a, n in zip(starts, widths)]

    def proj(part_ref):
        return jnp.dot(h, part_ref[...], preferred_element_type=F32)

    u = proj(wcc_ref) * proj(wch_ref)
    w0 = cw_ref[0:1, :]
    w1 = cw_ref[1:2, :]
    w2 = cw_ref[2:3, :]
    if sample:
        conv = w0 * hist0_ref[...] + w1 * hist1_ref[...] + w2 * u
    else:
        @pl.when(pl.program_id(0) % blocks_per_seq == 0)
        def _():
            ubuf_ref[0:8, :] = jnp.zeros((8, D_CONV), F32)

        ubuf_ref[8:8 + tm, :] = u
        conv = w0 * ubuf_ref[6:6 + tm, :] + w1 * ubuf_ref[7:7 + tm, :] + w2 * u
        ubuf_ref[0:8, :] = u[tm - 8:, :]
    yc_ref[...] = (proj(wcb_ref) * conv).astype(BF16)
    ut_ref[0] = u[tm - u_tail:, :]

    q_ref[...] = (proj(wq_ref) * (HEAD_DIM ** -0.5)).astype(BF16)
    kv = proj(wkv_ref)
    k_ref[...] = kv[:, :KV_DIM].astype(BF16)
    v_ref[...] = kv[:, KV_DIM:].astype(BF16)
    kvt_ref[0] = kv[tm - kv_tail:, :]
    sa_ref[...] = jax.nn.sigmoid(proj(wga_ref)).astype(sa_ref.dtype)
    sb_ref[...] = jax.nn.sigmoid(proj(wgb_ref)).astype(sb_ref.dtype)


def _in_proj(x, g1, w_in, conv_w, *, tm, blocks_per_seq, u_tail, kv_tail, hist=None, gate_dtype=BF16):
    m = x.shape[0]
    nblk = m // tm
    sample = hist is not None
    row = lambda width: pl.BlockSpec((tm, width), lambda i: (i, 0))
    in_specs = [row(D_MODEL)]
    args = [x]
    if sample:
        in_specs += [row(D_CONV), row(D_CONV)]
        args += list(hist)
    in_specs += [_const_spec((1, D_MODEL)), _const_spec(w_in.shape), _const_spec(conv_w.shape)]
    args += [g1, w_in, conv_w]
    out_shape = [
        jax.ShapeDtypeStruct((m, D_CONV), BF16),
        jax.ShapeDtypeStruct((m, Q_DIM), BF16),
        jax.ShapeDtypeStruct((m, KV_DIM), BF16),
        jax.ShapeDtypeStruct((m, KV_DIM), BF16),
        jax.ShapeDtypeStruct((m, D_MODEL), gate_dtype),
        jax.ShapeDtypeStruct((m, D_MODEL), gate_dtype),
        jax.ShapeDtypeStruct((nblk, u_tail, D_CONV), F32),
        jax.ShapeDtypeStruct((nblk, kv_tail, 2 * KV_DIM), F32),
    ]
    out_specs = [row(D_CONV), row(Q_DIM), row(KV_DIM), row(KV_DIM), row(D_MODEL), row(D_MODEL),
                 pl.BlockSpec((1, u_tail, D_CONV), lambda i: (i, 0, 0)),
                 pl.BlockSpec((1, kv_tail, 2 * KV_DIM), lambda i: (i, 0, 0))]
    scratch = [] if sample else [pltpu.VMEM((tm + 8, D_CONV), F32)]
    return pl.pallas_call(
        functools.partial(_in_proj_kernel, tm=tm, sample=sample, blocks_per_seq=blocks_per_seq,
                          u_tail=u_tail, kv_tail=kv_tail),
        grid=(nblk,),
        in_specs=in_specs,
        out_specs=out_specs,
        out_shape=out_shape,
        scratch_shapes=scratch,
        compiler_params=pltpu.CompilerParams(dimension_semantics=("arbitrary",),
                                             vmem_limit_bytes=MOE_VMEM_LIMIT_BYTES),
        name="in_proj_sample" if sample else "in_proj_prompt",
    )(*args)


def _attn_prompt_kernel(sink_ref, q_ref, kc_ref, kp_ref, vc_ref, vp_ref, bias_ref, o_ref):
    first = pl.program_id(1) == 0
    col = lax.broadcasted_iota(jnp.int32, (ATTN_BLOCK, 2 * ATTN_BLOCK), 1)
    no_prev = jnp.logical_and(first, col < ATTN_BLOCK)
    for g in range(N_KV_HEADS):
        ks = slice(g * HEAD_DIM, (g + 1) * HEAD_DIM)
        kcat = jnp.concatenate([kp_ref[:, ks], kc_ref[:, ks]], axis=0)
        vcat = jnp.concatenate([vp_ref[:, ks], vc_ref[:, ks]], axis=0)
        for h0 in range(g * GROUP, (g + 1) * GROUP, HEADS_PER_STORE):
            outs = []
            for h in range(h0, h0 + HEADS_PER_STORE):
                hs = slice(h * HEAD_DIM, (h + 1) * HEAD_DIM)
                s = lax.dot_general(q_ref[:, hs], kcat, (((1,), (1,)), ((), ())),
                                    preferred_element_type=F32)
                s = jnp.where(no_prev, NEG_BIG, s + bias_ref[h])
                sink = sink_ref[h]
                m = jnp.maximum(jnp.max(s, axis=-1, keepdims=True), sink)
                p = jnp.exp(s - m)
                denom = jnp.sum(p, axis=-1, keepdims=True) + jnp.exp(sink - m)
                o = jnp.dot(p.astype(BF16), vcat, preferred_element_type=F32)
                outs.append((o / denom).astype(BF16))
            o_ref[:, h0 * HEAD_DIM:(h0 + HEADS_PER_STORE) * HEAD_DIM] = jnp.concatenate(outs, axis=1)


def _attn_prompt(q, k, v, bias, sinks, batch, seq):
    nb = seq // ATTN_BLOCK
    cur = lambda b, i: (b * nb + i, 0)
    prev = lambda b, i: (b * nb + jnp.maximum(i - 1, 0), 0)
    return pl.pallas_call(
        _attn_prompt_kernel,
        grid=(batch, nb),
        in_specs=[pl.BlockSpec(memory_space=pltpu.SMEM),
                  pl.BlockSpec((ATTN_BLOCK, Q_DIM), cur),
                  pl.BlockSpec((ATTN_BLOCK, KV_DIM), cur),
                  pl.BlockSpec((ATTN_BLOCK, KV_DIM), prev),
                  pl.BlockSpec((ATTN_BLOCK, KV_DIM), cur),
                  pl.BlockSpec((ATTN_BLOCK, KV_DIM), prev),
                  _const_spec(bias.shape)],
        out_specs=pl.BlockSpec((ATTN_BLOCK, Q_DIM), cur),
        out_shape=jax.ShapeDtypeStruct((batch * seq, Q_DIM), BF16),
        compiler_params=pltpu.CompilerParams(dimension_semantics=("arbitrary", "arbitrary"),
                                             vmem_limit_bytes=V7X_VMEM_LIMIT_BYTES),
        name="attn_prompt",
    )(sinks, q, k, k, v, v, bias)


def _attn_sample_kernel(qbd_ref, ckt_ref, cvt_ref, kvn_ref, bias_ref, sink_ref, mask_ref,
                        o_ref, kwin_ref, vwin_ref, *, w_buf):
    bf16_round = lambda t: t.astype(BF16).astype(F32)
    seqs = range(SAMPLE_SEQ_PER_STEP)
    sink = sink_ref[...]
    newest = lax.broadcasted_iota(jnp.int32, (KV_DIM, w_buf), 1) == w_buf - 1
    kvn_t = kvn_ref[...].T
    for b in seqs:
        kwin_ref[b] = jnp.where(newest, kvn_t[:KV_DIM, b:b + 1], pltpu.roll(ckt_ref[b], w_buf - 1, axis=1))
        vwin_ref[b] = jnp.where(newest, kvn_t[KV_DIM:, b:b + 1], pltpu.roll(cvt_ref[b], w_buf - 1, axis=1))
    s = [jnp.dot(qbd_ref[b], ckt_ref[b].astype(BF16), preferred_element_type=F32) + bias_ref[:, :w_buf]
         for b in seqs]
    s_new = [jnp.sum(qbd_ref[b].astype(F32) * bf16_round(kvn_ref[b:b + 1, :KV_DIM]), axis=-1, keepdims=True)
             + bias_ref[:, w_buf:w_buf + 1] for b in seqs]
    m = [jnp.maximum(jnp.maximum(jnp.max(s[b], axis=-1, keepdims=True), s_new[b]), sink) for b in seqs]
    p = [jnp.exp(s[b] - m[b]) for b in seqs]
    p_new = [jnp.exp(s_new[b] - m[b]) for b in seqs]
    denom = [jnp.sum(p[b], axis=-1, keepdims=True) + p_new[b] + jnp.exp(sink - m[b]) for b in seqs]
    of = [lax.dot_general((p[b] / denom[b]).astype(BF16), cvt_ref[b].astype(BF16), (((1,), (1,)), ((), ())),
                          preferred_element_type=F32)
          + bf16_round(p_new[b] / denom[b]) * bf16_round(kvn_ref[b:b + 1, KV_DIM:]) for b in seqs]
    for b in seqs:
        ob = of[b] * mask_ref[...]
        o_ref[b] = (ob[:, 0:HEAD_DIM] + ob[:, HEAD_DIM:2 * HEAD_DIM]
                    + ob[:, 2 * HEAD_DIM:3 * HEAD_DIM] + ob[:, 3 * HEAD_DIM:]).astype(BF16)


def _attn_sample(qbd, ckt, cvt, kvn, bias, sink_col, head_mask):
    nseq, w_buf = ckt.shape[0], ckt.shape[2]
    sb = SAMPLE_SEQ_PER_STEP
    seq3 = lambda d1, d2: pl.BlockSpec((sb, d1, d2), lambda i: (i, 0, 0))
    win = jax.ShapeDtypeStruct((nseq, KV_DIM, w_buf), F32)
    return pl.pallas_call(
        functools.partial(_attn_sample_kernel, w_buf=w_buf),
        grid=(nseq // sb,),
        in_specs=[seq3(N_HEADS, KV_DIM), seq3(KV_DIM, w_buf), seq3(KV_DIM, w_buf),
                  pl.BlockSpec((sb, 2 * KV_DIM), lambda i: (i, 0)),
                  _const_spec(bias.shape), _const_spec(sink_col.shape), _const_spec(head_mask.shape)],
        out_specs=[seq3(N_HEADS, HEAD_DIM), seq3(KV_DIM, w_buf), seq3(KV_DIM, w_buf)],
        out_shape=[jax.ShapeDtypeStruct((nseq, N_HEADS, HEAD_DIM), BF16), win, win],
        compiler_params=pltpu.CompilerParams(dimension_semantics=("arbitrary",),
                                             vmem_limit_bytes=V7X_VMEM_LIMIT_BYTES),
        name="attn_sample",
    )(qbd, ckt, cvt, kvn, bias, sink_col, head_mask)


def _route_rows(logits, row0, valid_rows, half):
    tm = logits.shape[0]
    lane = lax.broadcasted_iota(jnp.int32, logits.shape, 1)
    lane_f = lane.astype(F32)
    no_lane = float(ROUTER_LANES)

    def top1(mask):
        best = jnp.max(jnp.where(mask, logits, -jnp.inf), axis=-1, keepdims=True)
        idx = jnp.min(jnp.where(jnp.logical_and(mask, logits == best), lane_f, no_lane), axis=-1, keepdims=True)
        return best, idx

    gmask = lane < N_EXPERT_GROUPS
    gmax, grp = top1(gmask)
    gsum = jnp.sum(jnp.where(gmask, jnp.exp(logits - gmax), 0.0), axis=-1, keepdims=True)
    p_grp = 1.0 / gsum
    lo = N_EXPERT_GROUPS + EXPERTS_PER_GROUP * grp
    emask = jnp.logical_and(lane_f >= lo, lane_f < lo + EXPERTS_PER_GROUP)
    v1, i1 = top1(emask)
    v2, i2 = top1(jnp.logical_and(emask, lane_f != i1))
    e21 = jnp.exp(v2 - v1)
    w1 = p_grp / (1.0 + e21)
    w2 = p_grp * e21 / (1.0 + e21)

    oh1 = lane_f == i1
    oh2 = lane_f == i2
    if valid_rows < tm:
        valid = lax.broadcasted_iota(jnp.int32, logits.shape, 0) < valid_rows
        oh1 = jnp.logical_and(oh1, valid)
        oh2 = jnp.logical_and(oh2, valid)
    oh = oh1.astype(F32) + oh2.astype(F32)
    token = row0 + lax.broadcasted_iota(jnp.int32, (tm, 1), 0)
    key1 = (i1.astype(jnp.int32) - N_EXPERT_GROUPS) * (1 << ASSIGN_BITS) + token
    key2 = (i2.astype(jnp.int32) - N_EXPERT_GROUPS) * (1 << ASSIGN_BITS) + token + half
    w1b = lax.bitcast_convert_type(w1, jnp.int32)
    w2b = lax.bitcast_convert_type(w2, jnp.int32)
    words = jnp.where(lane == 0, key1, jnp.where(lane == 1, key2, jnp.where(lane == 2, w1b,
                      jnp.where(lane == 3, w2b, 0))))
    return words, jnp.sum(oh, axis=0, keepdims=True)


def _store_token_tiles(ref, x):
    n = x.shape[0]
    for c in range(D_MODEL // LANES):
        ref[pl.ds(c, n, stride=TILE_ROWS), :] = x[:, c * LANES:(c + 1) * LANES]


def _load_token_tiles(ref, n):
    return jnp.concatenate([ref[pl.ds(c, n, stride=TILE_ROWS), :] for c in range(D_MODEL // LANES)], axis=1)


def _pack_bf16_pairs(x):
    hw = x.shape[1] // 2
    bits = lambda v: lax.bitcast_convert_type(v.astype(BF16).astype(F32), jnp.uint32)
    return (bits(x[:, hw:]) & jnp.uint32(0xFFFF0000)) | (bits(x[:, :hw]) >> 16)


def _unpack_bf16_pairs(w):
    lo = lax.bitcast_convert_type(w << 16, F32)
    hi = lax.bitcast_convert_type(w & jnp.uint32(0xFFFF0000), F32)
    return jnp.concatenate([lo, hi], axis=1).astype(BF16)


def _out_proj_rows(yc_ref, o_ref, sa_ref, sb_ref, x_ref, wc_ref, wa_ref, wo_ref, g2_ref, wr_ref, br_ref,
                   x2_ref, h2p_ref, route_ref, keys_ref, cnt_ref, *, valid_rows, half):
    tm = x_ref.shape[0]
    parts = [(r0, tm // OUT_PROJ_PARTS) for r0 in range(0, tm, tm // OUT_PROJ_PARTS)]
    for r0, n in parts:
        rows = pl.ds(r0, n)
        y_conv = jnp.dot(yc_ref[rows, :], wc_ref[...], preferred_element_type=F32)
        y_attn = jnp.dot(o_ref[rows, :], wa_ref[...], preferred_element_type=F32)
        mix = (sa_ref[rows, :].astype(F32) * y_conv + sb_ref[rows, :].astype(F32) * y_attn).astype(BF16)
        x2_ref[rows, :] = x_ref[rows, :] + jnp.dot(mix, wo_ref[...], preferred_element_type=F32)
    for r0, n in parts:
        rows = pl.ds(r0, n)
        h2 = _rms_norm_f32(x2_ref[rows, :], g2_ref[...])
        h2p_ref[rows, :] = _pack_bf16_pairs(h2)
        logits = jnp.dot(h2.astype(BF16), wr_ref[...], preferred_element_type=F32) + br_ref[...]
        words, cnt = _route_rows(logits, pl.program_id(0) * tm + r0, min(max(valid_rows - r0, 0), n), half)
        route_ref[rows, :] = words
        keys_ref[:, rows] = words.T[:TILE_ROWS, :]
        cnt_ref[...] += cnt


def _out_proj_kernel(*refs, n_first, valid_rows_second, half):
    first, second, shared = refs[0:5], refs[5:10], refs[10:]
    cnt_ref = shared[-1]
    tm = first[4].shape[0]

    @pl.when(pl.program_id(0) == 0)
    def _():
        cnt_ref[...] = jnp.zeros_like(cnt_ref)

    @pl.when(pl.program_id(0) < n_first)
    def _():
        _out_proj_rows(*first, *shared, valid_rows=tm, half=half)

    @pl.when(pl.program_id(0) >= n_first)
    def _():
        _out_proj_rows(*second, *shared, valid_rows=valid_rows_second, half=half)


def _out_proj(acts_a, acts_b, wc, wa, wo, g2, wr, br, *, tm, valid_rows_b, half):
    na = acts_a[4].shape[0] // tm
    nb = acts_b[4].shape[0] // tm
    assert nb == 1
    m_total = (na + nb) * tm
    spec_a = lambda width: pl.BlockSpec((tm, width), lambda i: (jnp.minimum(i, na - 1), 0))
    spec_b = lambda width: pl.BlockSpec((tm, width), lambda i: (jnp.maximum(i - na, 0), 0))
    widths = (D_CONV, Q_DIM, D_MODEL, D_MODEL, D_MODEL)
    in_specs = [spec_a(w) for w in widths] + [spec_b(w) for w in widths]
    in_specs += [_const_spec(wc.shape), _const_spec(wa.shape), _const_spec(wo.shape),
                 _const_spec(g2.shape), _const_spec(wr.shape), _const_spec(br.shape)]
    orow = lambda width: pl.BlockSpec((tm, width), lambda i: (i, 0))
    return pl.pallas_call(
        functools.partial(_out_proj_kernel, n_first=na, valid_rows_second=valid_rows_b, half=half),
        grid=(na + nb,),
        in_specs=in_specs,
        out_specs=[orow(D_MODEL), orow(D_MODEL // 2), orow(ROUTER_LANES),
                   pl.BlockSpec((TILE_ROWS, tm), lambda i: (0, i)),
                   pl.BlockSpec((1, ROUTER_LANES), lambda i: (0, 0))],
        out_shape=[jax.ShapeDtypeStruct((m_total, D_MODEL), F32),
                   jax.ShapeDtypeStruct((m_total, D_MODEL // 2), jnp.uint32),
                   jax.ShapeDtypeStruct((m_total, ROUTER_LANES), jnp.int32),
                   jax.ShapeDtypeStruct((TILE_ROWS, m_total), jnp.int32),
                   jax.ShapeDtypeStruct((1, ROUTER_LANES), F32)],
        compiler_params=pltpu.CompilerParams(dimension_semantics=("arbitrary",),
                                             vmem_limit_bytes=V7X_VMEM_LIMIT_BYTES),
        name="out_proj",
    )(*acts_a, *acts_b, wc, wa, wo, g2, wr, br)


def _block_table_kernel(counts_ref, bexp_ref, bpos_ref, bcnt_ref, bslot_ref, bnext_ref, nused_ref, first_ref,
                        *, n_blocks):
    def per_expert(e, carry):
        blk0, pos0, ordinal = carry
        cnt = counts_ref[e]
        nblk = lax.shift_right_logical(cnt + (MOE_BLOCK - 1), MOE_BLOCK_LOG2)
        first_ref[e] = jnp.where(nblk > 0, blk0, -1)

        def mark(b, c):
            off = (b - blk0) * MOE_BLOCK
            bexp_ref[b] = e
            bpos_ref[b] = pos0 + off
            bcnt_ref[b] = jnp.minimum(cnt - off, MOE_BLOCK)
            bslot_ref[b] = jnp.where(b == blk0, ordinal & 1, -1)
            bnext_ref[b] = -1
            return c
        lax.fori_loop(blk0, blk0 + nblk, mark, 0)
        return blk0 + nblk, pos0 + cnt, ordinal + jnp.where(nblk > 0, 1, 0)

    n_used, _, _ = lax.fori_loop(0, N_EXPERTS, per_expert, (0, 0, 0))
    nused_ref[0] = n_used

    def unused(b, c):
        bexp_ref[b] = N_EXPERTS - 1
        bpos_ref[b] = 0
        bcnt_ref[b] = 0
        bslot_ref[b] = -1
        bnext_ref[b] = -1
        return c
    lax.fori_loop(n_used, n_blocks, unused, 0)

    def link(k, nxt):
        e = N_EXPERTS - 1 - k
        fb = first_ref[e]

        @pl.when(fb >= 0)
        def _():
            bnext_ref[fb] = nxt
        return jnp.where(fb >= 0, e, nxt)
    lax.fori_loop(0, N_EXPERTS, link, -1)


def _block_tables(counts, n_blocks):
    smem = pl.BlockSpec(memory_space=pltpu.SMEM)
    blk = jax.ShapeDtypeStruct((n_blocks,), jnp.int32)
    return pl.pallas_call(
        functools.partial(_block_table_kernel, n_blocks=n_blocks),
        in_specs=[smem],
        out_specs=[smem] * 6,
        out_shape=[blk] * 5 + [jax.ShapeDtypeStruct((1,), jnp.int32)],
        scratch_shapes=[pltpu.SMEM((N_EXPERTS,), jnp.int32)],
        name="block_tables",
    )(counts)


def _moe_kernel(tok_ref, dst_ref, bexp_ref, bslot_ref, bnext_ref, nused_ref,
                h2p_ref, wg_hbm, wu_hbm, wd_hbm, contrib_hbm,
                xs_ref, ys_ref, wgf_ref, wuf_ref, wdf_ref, wgb_ref, wub_ref, wdb_ref, ssem_ref, wsem_ref,
                *, m_total, half, n_blocks):
    step = pl.program_id(0)
    n_used = nused_ref[0]
    tile = lambda t: pl.ds(pl.multiple_of(t * TILE_ROWS, TILE_ROWS), TILE_ROWS)

    def gather(blk, slt):
        for r in range(MOE_BLOCK):
            xs_ref[slt, pl.ds(r, 1), :] = h2p_ref[pl.ds(tok_ref[blk * MOE_BLOCK + r], 1), :]

    def scatter(blk, slt, r):
        return pltpu.make_async_copy(ys_ref.at[slt, tile(r), :], contrib_hbm.at[tile(dst_ref[blk * MOE_BLOCK + r]), :],
                                     ssem_ref.at[slt])

    def scatter_wait(slt, r):
        pltpu.make_async_copy(ys_ref.at[slt, tile(r), :], contrib_hbm.at[tile(0), :], ssem_ref.at[slt]).wait()

    @pl.when(step == 0)
    def _():
        ys_ref[...] = jnp.zeros_like(ys_ref)
        gap = half - m_total
        fills = [(2 * half + s * MOE_BLOCK, MOE_BLOCK) for s in range(2)]
        fills += [(k * half + m_total, gap) for k in range(TOP_K)] if gap else []
        for start, n in fills:
            fill = pltpu.make_async_copy(ys_ref.at[0, pl.ds(0, n * TILE_ROWS), :],
                                         contrib_hbm.at[pl.ds(start * TILE_ROWS, n * TILE_ROWS), :], ssem_ref.at[0])
            fill.start()
            fill.wait()

    def weight_copies(e, s):
        return [pltpu.make_async_copy(src.at[e], dst.at[s], wsem_ref.at[s])
                for src, dst in ((wg_hbm, wgf_ref), (wu_hbm, wuf_ref), (wd_hbm, wdf_ref))]

    @pl.when(step == 0)
    def _():
        for c in weight_copies(bexp_ref[0], 0):
            c.start()
        gather(0, 0)

    def wait_scatters(slt):
        for r in range(MOE_BLOCK):
            scatter_wait(slt, r)

    def expert_weights(i):
        wslot = bslot_ref[i]

        @pl.when(wslot >= 0)
        def _():
            for c in weight_copies(0, wslot):
                c.wait()

        @pl.when(jnp.logical_and(wslot >= 0, bnext_ref[i] >= 0))
        def _():
            for c in weight_copies(bnext_ref[i], 1 - wslot):
                c.start()

        @pl.when(wslot >= 0)
        def _():
            wgb_ref[...] = wgf_ref[wslot].astype(BF16)
            wub_ref[...] = wuf_ref[wslot].astype(BF16)
            wdb_ref[...] = wdf_ref[wslot].astype(BF16)

    def run_block(i, slt):
        gather(jnp.minimum(i + 1, n_blocks - 1), 1 - slt)
        xb = _unpack_bf16_pairs(xs_ref[slt])
        gate = jnp.dot(xb, wgb_ref[...], preferred_element_type=F32)
        up = jnp.dot(xb, wub_ref[...], preferred_element_type=F32)
        hmid = (jax.nn.silu(gate) * up).astype(BF16)
        _store_token_tiles(ys_ref.at[slt], jnp.dot(hmid, wdb_ref[...], preferred_element_type=F32))
        for r in range(MOE_BLOCK):
            scatter(i, slt, r).start(priority=r % 2)

    for slt in range(2):
        i = 2 * step + slt
        active = i < n_used
        last_active = i == n_used - 1
        pl.when(jnp.logical_and(active, i >= 2))(functools.partial(wait_scatters, slt))
        expert_weights(i)
        pl.when(active)(functools.partial(run_block, i, slt))
        pl.when(last_active)(functools.partial(wait_scatters, slt))
        pl.when(jnp.logical_and(last_active, i >= 1))(functools.partial(wait_scatters, 1 - slt))


def _row_table_kernel(bpos_ref, bcnt_ref, order_ref, tok_ref, dst_ref, *, half, n_blocks):
    rows_per_block = MOE_BLOCK // LANES
    lane = lax.broadcasted_iota(jnp.int32, (rows_per_block, LANES), 1)
    r = lax.broadcasted_iota(jnp.int32, (rows_per_block, LANES), 0) * LANES + lane

    def per_block(b, c):
        p = bpos_ref[b]
        sh = p & (LANES - 1)
        win = order_ref[pl.ds(lax.shift_right_logical(p, 7), rows_per_block + 1), :]
        rolled = pltpu.roll(win, (LANES - sh) & (LANES - 1), axis=1)
        assign = jnp.where(lane < LANES - sh, rolled[:rows_per_block], rolled[1:])
        tok_ref[pl.ds(b * rows_per_block, rows_per_block), :] = jnp.where(assign >= half, assign - half, assign)
        trash = 2 * half + (b & 1) * MOE_BLOCK + r
        dst_ref[pl.ds(b * rows_per_block, rows_per_block), :] = jnp.where(r < bcnt_ref[b], assign, trash)
        return c
    lax.fori_loop(0, n_blocks, per_block, 0, unroll=6)


def _row_tables(order, bpos, bcnt, half):
    n_blocks = bpos.shape[0]
    assert MOE_BLOCK % LANES == 0 and order.shape[0] % LANES == 0
    out = jax.ShapeDtypeStruct((n_blocks * MOE_BLOCK // LANES, LANES), jnp.int32)
    tok, dst = pl.pallas_call(
        functools.partial(_row_table_kernel, half=half, n_blocks=n_blocks),
        grid_spec=pltpu.PrefetchScalarGridSpec(
            num_scalar_prefetch=2, grid=(1,),
            in_specs=[pl.BlockSpec((order.shape[0] // LANES, LANES), lambda i, *_: (0, 0))],
            out_specs=[pl.BlockSpec(out.shape, lambda i, *_: (0, 0))] * 2),
        out_shape=[out, out],
        name="row_tables",
    )(bpos, bcnt, order.reshape(-1, LANES))
    return tok.reshape(-1), dst.reshape(-1)


def _moe_experts(order, tables, h2p, w_gate, w_up, w_down, *, m_total, half):
    bexp, bpos, bcnt, bslot, bnext, n_used = tables
    n_blocks = bexp.shape[0]
    assert 0 <= half - m_total <= MOE_BLOCK and n_blocks % 2 == 0
    tok, dst = _row_tables(order, bpos, bcnt, half)
    tables = (bexp, bslot, bnext, n_used)
    hbm = pl.BlockSpec(memory_space=pl.ANY)
    grid_spec = pltpu.PrefetchScalarGridSpec(
        num_scalar_prefetch=2 + len(tables),
        grid=(n_blocks // 2,),
        in_specs=[_const_spec(h2p.shape), hbm, hbm, hbm],
        out_specs=hbm,
        scratch_shapes=[pltpu.VMEM((2, MOE_BLOCK, D_MODEL // 2), jnp.uint32),
                        pltpu.VMEM((2, MOE_BLOCK * TILE_ROWS, LANES), F32),
                        pltpu.VMEM((2, D_MODEL, D_EXPERT), F32),
                        pltpu.VMEM((2, D_MODEL, D_EXPERT), F32),
                        pltpu.VMEM((2, D_EXPERT, D_MODEL), F32),
                        pltpu.VMEM((D_MODEL, D_EXPERT), BF16),
                        pltpu.VMEM((D_MODEL, D_EXPERT), BF16),
                        pltpu.VMEM((D_EXPERT, D_MODEL), BF16),
                        pltpu.SemaphoreType.DMA((2,)),
                        pltpu.SemaphoreType.DMA((2,))],
    )
    return pl.pallas_call(
        functools.partial(_moe_kernel, m_total=m_total, half=half, n_blocks=n_blocks),
        grid_spec=grid_spec,
        out_shape=jax.ShapeDtypeStruct(((2 * half + 2 * MOE_BLOCK) * TILE_ROWS, LANES), F32),
        compiler_params=pltpu.CompilerParams(dimension_semantics=("arbitrary",),
                                             vmem_limit_bytes=MOE_VMEM_LIMIT_BYTES),
        name="moe_experts",
    )(tok, dst, *tables, h2p, w_gate, w_up, w_down)


def _combine_kernel(c0_ref, c1_ref, x2_ref, route_ref, gf_ref, y_ref):
    tc = x2_ref.shape[0]
    w = lax.bitcast_convert_type(route_ref[:, TOP_K:2 * TOP_K], F32)
    moe = w[:, 0:1] * _load_token_tiles(c0_ref, tc) + w[:, 1:2] * _load_token_tiles(c1_ref, tc)
    y_ref[...] = _rms_norm_f32(x2_ref[...] + moe, gf_ref[...])


def _moe_combine(contrib, x2, route, gf, *, row_off, m, tc, half):
    off = row_off // tc
    assert row_off % tc == 0 and half % tc == 0
    ctile = lambda k: pl.BlockSpec((tc * TILE_ROWS, LANES), lambda i: (i + off + k * (half // tc), 0))
    return pl.pallas_call(
        _combine_kernel,
        grid=(m // tc,),
        in_specs=[ctile(0), ctile(1),
                  pl.BlockSpec((tc, D_MODEL), lambda i: (i + off, 0)),
                  pl.BlockSpec((tc, ROUTER_LANES), lambda i: (i + off, 0)),
                  _const_spec((1, D_MODEL))],
        out_specs=pl.BlockSpec((tc, D_MODEL), lambda i: (i, 0)),
        out_shape=jax.ShapeDtypeStruct((m, D_MODEL), F32),
        compiler_params=pltpu.CompilerParams(dimension_semantics=("arbitrary",),
                                             vmem_limit_bytes=V7X_VMEM_LIMIT_BYTES),
        name="moe_combine_prompt" if row_off == 0 else "moe_combine_sample",
    )(contrib, contrib, x2, route, gf)


def _t5_bucket(dist):
    n = jnp.maximum(dist, 0)
    max_exact = N_BUCKETS // 2
    nf = jnp.maximum(n, 1).astype(F32)
    large = max_exact + (jnp.log(nf / max_exact) / math.log(MAX_DISTANCE / max_exact)
                         * (N_BUCKETS - max_exact)).astype(jnp.int32)
    large = jnp.minimum(large, N_BUCKETS - 1)
    return jnp.where(n < max_exact, n, large)


def _bucket_bias(rel_bias, dist, valid):
    buckets = _t5_bucket(dist).reshape(1, -1)
    onehot = (buckets == jnp.arange(N_BUCKETS, dtype=jnp.int32)[:, None]).astype(F32)
    bias = jnp.dot(rel_bias.astype(F32).T, onehot, precision=lax.Precision.HIGHEST)
    return jnp.where(valid.reshape(1, -1), bias, NEG_BIG).reshape((rel_bias.shape[1],) + dist.shape)


def _prompt_bias_table(rel_bias):
    qi = jnp.arange(ATTN_BLOCK, dtype=jnp.int32)[:, None]
    kj = jnp.arange(2 * ATTN_BLOCK, dtype=jnp.int32)[None, :] - ATTN_BLOCK
    dist = qi - kj
    return _bucket_bias(rel_bias, dist, (dist >= 0) & (dist <= WINDOW))


def _sample_bias_table(rel_bias, w_buf):
    dist = w_buf - jnp.arange(w_buf + 1, dtype=jnp.int32)
    return _bucket_bias(rel_bias, dist, dist <= WINDOW)


def kernel(x_prompt, x_sample, cache_conv, cache_k, cache_v, norm1_g, w_in, conv_w, w_conv_out, w_attn_out, w_o, sinks, rel_bias, norm2_g, w_router_group, b_router_group, w_router_expert, b_router_expert, w_e_gate, w_e_up, w_e_down, norm_f_g):
    assert norm1_g.shape[0] == 1, "single-layer configuration"
    batch, seq, _ = x_prompt.shape
    nseq = x_sample.shape[0]
    w_buf = cache_k.shape[2]
    mp = batch * seq
    m_total = mp + nseq
    assert seq % TM_DENSE == 0 and seq % TM_IN_PROJ == 0 and seq % ATTN_BLOCK == 0 and mp % COMBINE_BLOCK == 0
    assert nseq % SAMPLE_SEQ_PER_STEP == 0 and mp % nseq == 0
    assert TOP_K == 2 and MOE_BLOCK == 1 << MOE_BLOCK_LOG2 and m_total * TOP_K < 1 << ASSIGN_BITS

    g1 = norm1_g[0][None, :]
    g2 = norm2_g[0][None, :]
    gf = norm_f_g[None, :]
    wi = w_in[0].astype(BF16)
    cw = conv_w[0]
    wc = w_conv_out[0].astype(BF16)
    wa = w_attn_out[0].astype(BF16)
    wo = w_o[0].astype(BF16)
    pad_cols = ROUTER_LANES - N_EXPERT_GROUPS - N_EXPERTS
    wr = jnp.concatenate([w_router_group[0], w_router_expert[0],
                          jnp.zeros((D_MODEL, pad_cols), F32)], axis=1).astype(BF16)
    br = jnp.concatenate([b_router_group[0], b_router_expert[0], jnp.zeros((pad_cols,), F32)])[None, :]
    sink = sinks[0].astype(F32)

    xp = x_prompt.reshape(mp, D_MODEL)
    bps = seq // TM_IN_PROJ
    yc_p, q_p, k_p, v_p, sa_p, sb_p, ut_p, kvt_p = _in_proj(
        xp, g1, wi, cw, tm=TM_IN_PROJ, blocks_per_seq=bps, u_tail=8, kv_tail=WINDOW)
    o_p = _attn_prompt(q_p, k_p, v_p, _prompt_bias_table(rel_bias), sink, batch, seq)

    pad_rows = lambda t: jnp.pad(t, ((0, TM_DENSE - nseq), (0, 0)))
    xs = pad_rows(x_sample.reshape(nseq, D_MODEL))
    hist = (pad_rows(cache_conv[0][:, 0, :]), pad_rows(cache_conv[0][:, 1, :]))
    yc_s, q_s, _, _, sa_s, sb_s, ut_s, kvt_s = _in_proj(
        xs, g1, wi, cw, tm=TM_DENSE, blocks_per_seq=1, u_tail=TM_DENSE, kv_tail=TM_DENSE, hist=hist,
        gate_dtype=F32)
    u_s = ut_s[0, :nseq]
    kv_s = kvt_s[0, :nseq]
    head_mask = (jnp.arange(KV_DIM)[None, :] // HEAD_DIM == jnp.arange(N_HEADS)[:, None] // GROUP)
    qbd = (jnp.tile(q_s[:nseq].reshape(nseq, N_HEADS, HEAD_DIM), (1, 1, N_KV_HEADS))
           * head_mask[None].astype(BF16))
    to_keys_minor = lambda c: jnp.transpose(c.reshape(nseq, w_buf, KV_DIM), (0, 2, 1))
    from_keys_minor = lambda c: jnp.transpose(c, (0, 2, 1)).reshape(1, nseq, w_buf, N_KV_HEADS, HEAD_DIM)
    o_s, kwin_s, vwin_s = _attn_sample(qbd, to_keys_minor(cache_k[0]), to_keys_minor(cache_v[0]), kv_s,
                                       _sample_bias_table(rel_bias, w_buf), sink[:, None],
                                       head_mask.astype(F32))
    o_s = pad_rows(o_s.reshape(nseq, Q_DIM))

    half = -(-m_total // COMBINE_BLOCK) * COMBINE_BLOCK
    assert half % nseq == 0 and TOP_K * half < 1 << ASSIGN_BITS
    x2, h2p, route, route_t, cnt = _out_proj((yc_p, o_p, sa_p, sb_p, xp), (yc_s, o_s, sa_s, sb_s, xs),
                                             wc, wa, wo, g2, wr, br, tm=TM_DENSE, valid_rows_b=nseq, half=half)

    n_assign = m_total * TOP_K
    keys = route_t[0:TOP_K, :m_total].reshape(-1)
    counts = cnt[0, N_EXPERT_GROUPS:N_EXPERT_GROUPS + N_EXPERTS].astype(jnp.int32)
    order = jnp.pad(jnp.sort(keys) & ((1 << ASSIGN_BITS) - 1), (0, MOE_BLOCK))
    n_blocks = -(-n_assign // MOE_BLOCK) + N_EXPERTS
    n_blocks += n_blocks % 2
    tables = _block_tables(counts, n_blocks)
    contrib = _moe_experts(order, tables, h2p, w_e_gate[0], w_e_up[0], w_e_down[0],
                           m_total=m_total, half=half)
    y_p = _moe_combine(contrib, x2, route, gf, row_off=0, m=mp, tc=COMBINE_BLOCK, half=half)
    y_s = _moe_combine(contrib, x2, route, gf, row_off=mp, m=nseq, tc=nseq, half=half)

    y_prompt = y_p.reshape(batch, seq, D_MODEL)
    y_sample = y_s.reshape(nseq, 1, D_MODEL)
    conv_state_prompt = ut_p.reshape(batch, bps, 8, D_CONV)[:, -1, 8 - (CONV_WIDTH - 1):, :][None]
    kv_last = kvt_p.reshape(batch, bps, WINDOW, 2 * KV_DIM)[:, -1]
    k_win_prompt = kv_last[:, :, :KV_DIM].reshape(batch, WINDOW, N_KV_HEADS, HEAD_DIM)[None]
    v_win_prompt = kv_last[:, :, KV_DIM:].reshape(batch, WINDOW, N_KV_HEADS, HEAD_DIM)[None]
    conv_state_sample = jnp.concatenate([cache_conv[0][:, 1:, :], u_s[:, None, :]], axis=1)[None]
    k_win_sample = from_keys_minor(kwin_s)
    v_win_sample = from_keys_minor(vwin_s)
    return (y_prompt, y_sample, conv_state_prompt, k_win_prompt, v_win_prompt,
            conv_state_sample, k_win_sample, v_win_sample)
```

```python
import functools
import math

import jax
import jax.numpy as jnp
from jax import lax
from jax.experimental import pallas as pl
from jax.experimental.pallas import tpu as pltpu

D_MODEL = 1024
D_CONV = 1024
CONV_WIDTH = 3
N_HEADS = 16
N_KV_HEADS = 4
HEAD_DIM = 64
GROUP = N_HEADS // N_KV_HEADS
WINDOW = 128
Q_DIM = N_HEADS * HEAD_DIM
KV_DIM = N_KV_HEADS * HEAD_DIM
N_BUCKETS = 32
MAX_DISTANCE = 128
N_EXPERT_GROUPS = 4
EXPERTS_PER_GROUP = 8
N_EXPERTS = N_EXPERT_GROUPS * EXPERTS_PER_GROUP
TOP_K = 2
D_EXPERT = 512
EPS = 1e-6
PAST_LEN = 8192

BF16 = jnp.bfloat16
F32 = jnp.float32
NEG_BIG = -1e30

V7X_VMEM_LIMIT_BYTES = 56 * 1024 * 1024
MOE_VMEM_LIMIT_BYTES = 62 * 1024 * 1024
TILE_ROWS = 8
LANES = 128
ROUTER_LANES = 128
TM_DENSE = 512
TM_IN_PROJ = 1024
OUT_PROJ_PARTS = 2
IN_PROJ_PARTS = 2
ATTN_BLOCK = 128
MOE_BLOCK = 256
MOE_BLOCK_LOG2 = 8
ASSIGN_BITS = 16
COMBINE_BLOCK = 256
SAMPLE_SEQ_PER_STEP = 16
HEADS_PER_STORE = LANES // HEAD_DIM


def _const_spec(shape):
    nd = len(shape)
    return pl.BlockSpec(shape, lambda *_: (0,) * nd, pipeline_mode=pl.Buffered(1))


def _rms_norm_f32(xf, g):
    return xf * lax.rsqrt(jnp.mean(xf * xf, axis=-1, keepdims=True) + EPS) * g


def _in_proj_kernel(*refs, tm, sample, blocks_per_seq, u_tail, kv_tail, parts):
    if sample:
        (x_ref, hist0_ref, hist1_ref, g_ref, w_ref,
         cw_ref, yc_ref, q_ref, k_ref, v_ref, sa_ref, sb_ref, ut_ref, kvt_ref) = refs
    else:
        (x_ref, g_ref, w_ref,
         cw_ref, yc_ref, q_ref, k_ref, v_ref, sa_ref, sb_ref, ut_ref, kvt_ref, ubuf_ref) = refs

    widths = (D_CONV, D_CONV, D_CONV, Q_DIM, 2 * KV_DIM, D_MODEL, D_MODEL)
    starts = [sum(widths[:n]) for n in range(len(widths))]
    wcb_ref, wcc_ref, wch_ref, wq_ref, wkv_ref, wga_ref, wgb_ref = [
        w_ref.at[:, pl.ds(a, n)] for a, n in zip(starts, widths)]
    w0 = cw_ref[0:1, :]
    w1 = cw_ref[1:2, :]
    w2 = cw_ref[2:3, :]

    if not sample:
        @pl.when(pl.program_id(0) % blocks_per_seq == 0)
        def _():
            ubuf_ref[0:8, :] = jnp.zeros((8, D_CONV), F32)

    n = tm // parts
    assert u_tail <= n and kv_tail <= n
    for r0 in range(0, tm, n):
        rows = pl.ds(r0, n)
        last = r0 + n == tm
        h = _rms_norm_f32(x_ref[rows, :], g_ref[...]).astype(BF16)

        def proj(part_ref, h=h):
            return jnp.dot(h, part_ref[...], preferred_element_type=F32)

        u = proj(wcc_ref) * proj(wch_ref)
        if sample:
            conv = w0 * hist0_ref[rows, :] + w1 * hist1_ref[rows, :] + w2 * u
        else:
            ubuf_ref[pl.ds(8 + r0, n), :] = u
            conv = w0 * ubuf_ref[pl.ds(6 + r0, n), :] + w1 * ubuf_ref[pl.ds(7 + r0, n), :] + w2 * u
            if last:
                ubuf_ref[0:8, :] = u[n - 8:, :]
        yc_ref[rows, :] = (proj(wcb_ref) * conv).astype(BF16)
        q_ref[rows, :] = (proj(wq_ref) * (HEAD_DIM ** -0.5)).astype(BF16)
        kv = proj(wkv_ref)
        k_ref[rows, :] = kv[:, :KV_DIM].astype(BF16)
        v_ref[rows, :] = kv[:, KV_DIM:].astype(BF16)
        if last:
            ut_ref[0] = u[n - u_tail:, :]
            kvt_ref[0] = kv[n - kv_tail:, :]
        sa_ref[rows, :] = jax.nn.sigmoid(proj(wga_ref)).astype(sa_ref.dtype)
        sb_ref[rows, :] = jax.nn.sigmoid(proj(wgb_ref)).astype(sb_ref.dtype)


def _in_proj(x, g1, w_in, conv_w, *, tm, blocks_per_seq, u_tail, kv_tail, hist=None, gate_dtype=BF16, parts=1):
    m = x.shape[0]
    nblk = m // tm
    sample = hist is not None
    row = lambda width: pl.BlockSpec((tm, width), lambda i: (i, 0))
    in_specs = [row(D_MODEL)]
    args = [x]
    if sample:
        in_specs += [row(D_CONV), row(D_CONV)]
        args += list(hist)
    in_specs += [_const_spec((1, D_MODEL)), _const_spec(w_in.shape), _const_spec(conv_w.shape)]
    args += [g1, w_in, conv_w]
    out_shape = [
        jax.ShapeDtypeStruct((m, D_CONV), BF16),
        jax.ShapeDtypeStruct((m, Q_DIM), BF16),
        jax.ShapeDtypeStruct((m, KV_DIM), BF16),
        jax.ShapeDtypeStruct((m, KV_DIM), BF16),
        jax.ShapeDtypeStruct((m, D_MODEL), gate_dtype),
        jax.ShapeDtypeStruct((m, D_MODEL), gate_dtype),
        jax.ShapeDtypeStruct((nblk, u_tail, D_CONV), F32),
        jax.ShapeDtypeStruct((nblk, kv_tail, 2 * KV_DIM), F32),
    ]
    out_specs = [row(D_CONV), row(Q_DIM), row(KV_DIM), row(KV_DIM), row(D_MODEL), row(D_MODEL),
                 pl.BlockSpec((1, u_tail, D_CONV), lambda i: (i, 0, 0)),
                 pl.BlockSpec((1, kv_tail, 2 * KV_DIM), lambda i: (i, 0, 0))]
    scratch = [] if sample else [pltpu.VMEM((tm + 8, D_CONV), F32)]
    return pl.pallas_call(
        functools.partial(_in_proj_kernel, tm=tm, sample=sample, blocks_per_seq=blocks_per_seq,
                          u_tail=u_tail, kv_tail=kv_tail, parts=parts),
        grid=(nblk,),
        in_specs=in_specs,
        out_specs=out_specs,
        out_shape=out_shape,
        scratch_shapes=scratch,
        compiler_params=pltpu.CompilerParams(dimension_semantics=("arbitrary",),
                                             vmem_limit_bytes=MOE_VMEM_LIMIT_BYTES),
        name="in_proj_sample" if sample else "in_proj_prompt",
    )(*args)


def _attn_prompt_kernel(sink_ref, q_ref, kc_ref, kp_ref, vc_ref, vp_ref, bias_ref, o_ref):
    first = pl.program_id(1) == 0
    col = lax.broadcasted_iota(jnp.int32, (ATTN_BLOCK, 2 * ATTN_BLOCK), 1)
    no_prev = jnp.logical_and(first, col < ATTN_BLOCK)
    for g in range(N_KV_HEADS):
        ks = slice(g * HEAD_DIM, (g + 1) * HEAD_DIM)
        kcat = jnp.concatenate([kp_ref[:, ks], kc_ref[:, ks]], axis=0)
        vcat = jnp.concatenate([vp_ref[:, ks], vc_ref[:, ks]], axis=0)
        for h0 in range(g * GROUP, (g + 1) * GROUP, HEADS_PER_STORE):
            outs = []
            for h in range(h0, h0 + HEADS_PER_STORE):
                hs = slice(h * HEAD_DIM, (h + 1) * HEAD_DIM)
                s = lax.dot_general(q_ref[:, hs], kcat, (((1,), (1,)), ((), ())),
                                    preferred_element_type=F32)
                s = jnp.where(no_prev, NEG_BIG, s + bias_ref[h])
                sink = sink_ref[h]
                m = jnp.maximum(jnp.max(s, axis=-1, keepdims=True), sink)
                p = jnp.exp(s - m)
                denom = jnp.sum(p, axis=-1, keepdims=True) + jnp.exp(sink - m)
                o = jnp.dot(p.astype(BF16), vcat, preferred_element_type=F32)
                outs.append((o / denom).astype(BF16))
            o_ref[:, h0 * HEAD_DIM:(h0 + HEADS_PER_STORE) * HEAD_DIM] = jnp.concatenate(outs, axis=1)


def _attn_prompt(q, k, v, bias, sinks, batch, seq):
    nb = seq // ATTN_BLOCK
    cur = lambda b, i: (b * nb + i, 0)
    prev = lambda b, i: (b * nb + jnp.maximum(i - 1, 0), 0)
    return pl.pallas_call(
        _attn_prompt_kernel,
        grid=(batch, nb),
        in_specs=[pl.BlockSpec(memory_space=pltpu.SMEM),
                  pl.BlockSpec((ATTN_BLOCK, Q_DIM), cur),
                  pl.BlockSpec((ATTN_BLOCK, KV_DIM), cur),
                  pl.BlockSpec((ATTN_BLOCK, KV_DIM), prev),
                  pl.BlockSpec((ATTN_BLOCK, KV_DIM), cur),
                  pl.BlockSpec((ATTN_BLOCK, KV_DIM), prev),
                  _const_spec(bias.shape)],
        out_specs=pl.BlockSpec((ATTN_BLOCK, Q_DIM), cur),
        out_shape=jax.ShapeDtypeStruct((batch * seq, Q_DIM), BF16),
        compiler_params=pltpu.CompilerParams(dimension_semantics=("arbitrary", "arbitrary"),
                                             vmem_limit_bytes=V7X_VMEM_LIMIT_BYTES),
        name="attn_prompt",
    )(sinks, q, k, k, v, v, bias)


def _attn_sample_kernel(qbd_ref, ckt_ref, cvt_ref, kvn_ref, bias_ref, sink_ref, mask_ref,
                        o_ref, kwin_ref, vwin_ref, *, w_buf):
    bf16_round = lambda t: t.astype(BF16).astype(F32)
    seqs = range(SAMPLE_SEQ_PER_STEP)
    sink = sink_ref[...]
    newest = lax.broadcasted_iota(jnp.int32, (KV_DIM, w_buf), 1) == w_buf - 1
    kvn_t = kvn_ref[...].T
    for b in seqs:
        kwin_ref[b] = jnp.where(newest, kvn_t[:KV_DIM, b:b + 1], pltpu.roll(ckt_ref[b], w_buf - 1, axis=1))
        vwin_ref[b] = jnp.where(newest, kvn_t[KV_DIM:, b:b + 1], pltpu.roll(cvt_ref[b], w_buf - 1, axis=1))
    s = [jnp.dot(qbd_ref[b], ckt_ref[b].astype(BF16), preferred_element_type=F32) + bias_ref[:, :w_buf]
         for b in seqs]
    s_new = [jnp.sum(qbd_ref[b].astype(F32) * bf16_round(kvn_ref[b:b + 1, :KV_DIM]), axis=-1, keepdims=True)
             + bias_ref[:, w_buf:w_buf + 1] for b in seqs]
    m = [jnp.maximum(jnp.maximum(jnp.max(s[b], axis=-1, keepdims=True), s_new[b]), sink) for b in seqs]
    p = [jnp.exp(s[b] - m[b]) for b in seqs]
    p_new = [jnp.exp(s_new[b] - m[b]) for b in seqs]
    denom = [jnp.sum(p[b], axis=-1, keepdims=True) + p_new[b] + jnp.exp(sink - m[b]) for b in seqs]
    of = [lax.dot_general((p[b] / denom[b]).astype(BF16), cvt_ref[b].astype(BF16), (((1,), (1,)), ((), ())),
                          preferred_element_type=F32)
          + bf16_round(p_new[b] / denom[b]) * bf16_round(kvn_ref[b:b + 1, KV_DIM:]) for b in seqs]
    for b in seqs:
        ob = of[b] * mask_ref[...]
        o_ref[b] = (ob[:, 0:HEAD_DIM] + ob[:, HEAD_DIM:2 * HEAD_DIM]
                    + ob[:, 2 * HEAD_DIM:3 * HEAD_DIM] + ob[:, 3 * HEAD_DIM:]).astype(BF16)


def _attn_sample(qbd, ckt, cvt, kvn, bias, sink_col, head_mask):
    nseq, w_buf = ckt.shape[0], ckt.shape[2]
    sb = SAMPLE_SEQ_PER_STEP
    seq3 = lambda d1, d2: pl.BlockSpec((sb, d1, d2), lambda i: (i, 0, 0))
    win = jax.ShapeDtypeStruct((nseq, KV_DIM, w_buf), F32)
    return pl.pallas_call(
        functools.partial(_attn_sample_kernel, w_buf=w_buf),
        grid=(nseq // sb,),
        in_specs=[seq3(N_HEADS, KV_DIM), seq3(KV_DIM, w_buf), seq3(KV_DIM, w_buf),
                  pl.BlockSpec((sb, 2 * KV_DIM), lambda i: (i, 0)),
                  _const_spec(bias.shape), _const_spec(sink_col.shape), _const_spec(head_mask.shape)],
        out_specs=[seq3(N_HEADS, HEAD_DIM), seq3(KV_DIM, w_buf), seq3(KV_DIM, w_buf)],
        out_shape=[jax.ShapeDtypeStruct((nseq, N_HEADS, HEAD_DIM), BF16), win, win],
        compiler_params=pltpu.CompilerParams(dimension_semantics=("arbitrary",),
                                             vmem_limit_bytes=V7X_VMEM_LIMIT_BYTES),
        name="attn_sample",
    )(qbd, ckt, cvt, kvn, bias, sink_col, head_mask)


def _route_rows(logits, row0, valid_rows, half):
    tm = logits.shape[0]
    lane = lax.broadcasted_iota(jnp.int32, logits.shape, 1)
    lane_f = lane.astype(F32)
    no_lane = float(ROUTER_LANES)

    def top1(mask):
        best = jnp.max(jnp.where(mask, logits, -jnp.inf), axis=-1, keepdims=True)
        idx = jnp.min(jnp.where(jnp.logical_and(mask, logits == best), lane_f, no_lane), axis=-1, keepdims=True)
        return best, idx

    gmask = lane < N_EXPERT_GROUPS
    gmax, grp = top1(gmask)
    gsum = jnp.sum(jnp.where(gmask, jnp.exp(logits - gmax), 0.0), axis=-1, keepdims=True)
    p_grp = 1.0 / gsum
    lo = N_EXPERT_GROUPS + EXPERTS_PER_GROUP * grp
    emask = jnp.logical_and(lane_f >= lo, lane_f < lo + EXPERTS_PER_GROUP)
    v1, i1 = top1(emask)
    v2, i2 = top1(jnp.logical_and(emask, lane_f != i1))
    e21 = jnp.exp(v2 - v1)
    w1 = p_grp / (1.0 + e21)
    w2 = p_grp * e21 / (1.0 + e21)

    oh1 = lane_f == i1
    oh2 = lane_f == i2
    if valid_rows < tm:
        valid = lax.broadcasted_iota(jnp.int32, logits.shape, 0) < valid_rows
        oh1 = jnp.logical_and(oh1, valid)
        oh2 = jnp.logical_and(oh2, valid)
    oh = oh1.astype(F32) + oh2.astype(F32)
    token = row0 + lax.broadcasted_iota(jnp.int32, (tm, 1), 0)
    key1 = (i1.astype(jnp.int32) - N_EXPERT_GROUPS) * (1 << ASSIGN_BITS) + token
    key2 = (i2.astype(jnp.int32) - N_EXPERT_GROUPS) * (1 << ASSIGN_BITS) + token + half
    w1b = lax.bitcast_convert_type(w1, jnp.int32)
    w2b = lax.bitcast_convert_type(w2, jnp.int32)
    words = jnp.where(lane == 0, key1, jnp.where(lane == 1, key2, jnp.where(lane == 2, w1b,
                      jnp.where(lane == 3, w2b, 0))))
    return words, jnp.sum(oh, axis=0, keepdims=True)


def _store_token_tiles(ref, x):
    n = x.shape[0]
    for c in range(D_MODEL // LANES):
        ref[pl.ds(c, n, stride=TILE_ROWS), :] = x[:, c * LANES:(c + 1) * LANES]


def _load_token_tiles(ref, n):
    return jnp.concatenate([ref[pl.ds(c, n, stride=TILE_ROWS), :] for c in range(D_MODEL // LANES)], axis=1)


def _pack_bf16_pairs(x):
    hw = x.shape[1] // 2
    bits = lambda v: lax.bitcast_convert_type(v.astype(BF16).astype(F32), jnp.uint32)
    return (bits(x[:, hw:]) & jnp.uint32(0xFFFF0000)) | (bits(x[:, :hw]) >> 16)


def _unpack_bf16_pairs(w):
    lo = lax.bitcast_convert_type(w << 16, F32)
    hi = lax.bitcast_convert_type(w & jnp.uint32(0xFFFF0000), F32)
    return jnp.concatenate([lo, hi], axis=1).astype(BF16)


def _out_proj_rows(yc_ref, o_ref, sa_ref, sb_ref, x_ref, wc_ref, wa_ref, wo_ref, g2_ref, wr_ref, br_ref,
                   x2_ref, h2p_ref, route_ref, keys_ref, cnt_ref, *, valid_rows, half):
    tm = x_ref.shape[0]
    parts = [(r0, tm // OUT_PROJ_PARTS) for r0 in range(0, tm, tm // OUT_PROJ_PARTS)]
    for r0, n in parts:
        rows = pl.ds(r0, n)
        y_conv = jnp.dot(yc_ref[rows, :], wc_ref[...], preferred_element_type=F32)
        y_attn = jnp.dot(o_ref[rows, :], wa_ref[...], preferred_element_type=F32)
        mix = (sa_ref[rows, :].astype(F32) * y_conv + sb_ref[rows, :].astype(F32) * y_attn).astype(BF16)
        x2_ref[rows, :] = x_ref[rows, :] + jnp.dot(mix, wo_ref[...], preferred_element_type=F32)
    for r0, n in parts:
        rows = pl.ds(r0, n)
        h2 = _rms_norm_f32(x2_ref[rows, :], g2_ref[...])
        h2p_ref[rows, :] = _pack_bf16_pairs(h2)
        logits = jnp.dot(h2.astype(BF16), wr_ref[...], preferred_element_type=F32) + br_ref[...]
        words, cnt = _route_rows(logits, pl.program_id(0) * tm + r0, min(max(valid_rows - r0, 0), n), half)
        route_ref[rows, :] = words
        keys_ref[:, rows] = words.T[:TILE_ROWS, :]
        cnt_ref[...] += cnt


def _out_proj_kernel(*refs, n_first, valid_rows_second, half):
    first, second, shared = refs[0:5], refs[5:10], refs[10:]
    cnt_ref = shared[-1]
    tm = first[4].shape[0]

    @pl.when(pl.program_id(0) == 0)
    def _():
        cnt_ref[...] = jnp.zeros_like(cnt_ref)

    @pl.when(pl.program_id(0) < n_first)
    def _():
        _out_proj_rows(*first, *shared, valid_rows=tm, half=half)

    @pl.when(pl.program_id(0) >= n_first)
    def _():
        _out_proj_rows(*second, *shared, valid_rows=valid_rows_second, half=half)


def _out_proj(acts_a, acts_b, wc, wa, wo, g2, wr, br, *, tm, valid_rows_b, half):
    na = acts_a[4].shape[0] // tm
    nb = acts_b[4].shape[0] // tm
    assert nb == 1
    m_total = (na + nb) * tm
    spec_a = lambda width: pl.BlockSpec((tm, width), lambda i: (jnp.minimum(i, na - 1), 0))
    spec_b = lambda width: pl.BlockSpec((tm, width), lambda i: (jnp.maximum(i - na, 0), 0))
    widths = (D_CONV, Q_DIM, D_MODEL, D_MODEL, D_MODEL)
    in_specs = [spec_a(w) for w in widths] + [spec_b(w) for w in widths]
    in_specs += [_const_spec(wc.shape), _const_spec(wa.shape), _const_spec(wo.shape),
                 _const_spec(g2.shape), _const_spec(wr.shape), _const_spec(br.shape)]
    orow = lambda width: pl.BlockSpec((tm, width), lambda i: (i, 0))
    return pl.pallas_call(
        functools.partial(_out_proj_kernel, n_first=na, valid_rows_second=valid_rows_b, half=half),
        grid=(na + nb,),
        in_specs=in_specs,
        out_specs=[orow(D_MODEL), orow(D_MODEL // 2), orow(ROUTER_LANES),
                   pl.BlockSpec((TILE_ROWS, tm), lambda i: (0, i)),
                   pl.BlockSpec((1, ROUTER_LANES), lambda i: (0, 0))],
        out_shape=[jax.ShapeDtypeStruct((m_total, D_MODEL), F32),
                   jax.ShapeDtypeStruct((m_total, D_MODEL // 2), jnp.uint32),
                   jax.ShapeDtypeStruct((m_total, ROUTER_LANES), jnp.int32),
                   jax.ShapeDtypeStruct((TILE_ROWS, m_total), jnp.int32),
                   jax.ShapeDtypeStruct((1, ROUTER_LANES), F32)],
        compiler_params=pltpu.CompilerParams(dimension_semantics=("arbitrary",),
                                             vmem_limit_bytes=V7X_VMEM_LIMIT_BYTES),
        name="out_proj",
    )(*acts_a, *acts_b, wc, wa, wo, g2, wr, br)


def _block_table_kernel(counts_ref, bexp_ref, bpos_ref, bcnt_ref, bslot_ref, bnext_ref, nused_ref, first_ref,
                        *, n_blocks):
    def per_expert(e, carry):
        blk0, pos0, ordinal = carry
        cnt = counts_ref[e]
        nblk = lax.shift_right_logical(cnt + (MOE_BLOCK - 1), MOE_BLOCK_LOG2)
        first_ref[e] = jnp.where(nblk > 0, blk0, -1)

        def mark(b, c):
            off = (b - blk0) * MOE_BLOCK
            bexp_ref[b] = e
            bpos_ref[b] = pos0 + off
            bcnt_ref[b] = jnp.minimum(cnt - off, MOE_BLOCK)
            bslot_ref[b] = jnp.where(b == blk0, ordinal & 1, -1)
            bnext_ref[b] = -1
            return c
        lax.fori_loop(blk0, blk0 + nblk, mark, 0)
        return blk0 + nblk, pos0 + cnt, ordinal + jnp.where(nblk > 0, 1, 0)

    n_used, _, _ = lax.fori_loop(0, N_EXPERTS, per_expert, (0, 0, 0))
    nused_ref[0] = n_used

    def unused(b, c):
        bexp_ref[b] = N_EXPERTS - 1
        bpos_ref[b] = 0
        bcnt_ref[b] = 0
        bslot_ref[b] = -1
        bnext_ref[b] = -1
        return c
    lax.fori_loop(n_used, n_blocks, unused, 0)

    def link(k, nxt):
        e = N_EXPERTS - 1 - k
        fb = first_ref[e]

        @pl.when(fb >= 0)
        def _():
            bnext_ref[fb] = nxt
        return jnp.where(fb >= 0, e, nxt)
    lax.fori_loop(0, N_EXPERTS, link, -1)


def _block_tables(counts, n_blocks):
    smem = pl.BlockSpec(memory_space=pltpu.SMEM)
    blk = jax.ShapeDtypeStruct((n_blocks,), jnp.int32)
    return pl.pallas_call(
        functools.partial(_block_table_kernel, n_blocks=n_blocks),
        in_specs=[smem],
        out_specs=[smem] * 6,
        out_shape=[blk] * 5 + [jax.ShapeDtypeStruct((1,), jnp.int32)],
        scratch_shapes=[pltpu.SMEM((N_EXPERTS,), jnp.int32)],
        name="block_tables",
    )(counts)


def _moe_kernel(tok_ref, dst_ref, bexp_ref, bslot_ref, bnext_ref, nused_ref,
                h2p_ref, wg_hbm, wu_hbm, wd_hbm, contrib_hbm,
                xs_ref, ys_ref, wgf_ref, wuf_ref, wdf_ref, wgb_ref, wub_ref, wdb_ref, ssem_ref, wsem_ref,
                *, m_total, half, n_blocks):
    step = pl.program_id(0)
    n_used = nused_ref[0]
    tile = lambda t: pl.ds(pl.multiple_of(t * TILE_ROWS, TILE_ROWS), TILE_ROWS)

    def gather(blk, slt):
        for r in range(MOE_BLOCK):
            xs_ref[slt, pl.ds(r, 1), :] = h2p_ref[pl.ds(tok_ref[blk * MOE_BLOCK + r], 1), :]

    def scatter(blk, slt, r):
        return pltpu.make_async_copy(ys_ref.at[slt, tile(r), :], contrib_hbm.at[tile(dst_ref[blk * MOE_BLOCK + r]), :],
                                     ssem_ref.at[slt])

    def scatter_wait(slt, r):
        pltpu.make_async_copy(ys_ref.at[slt, tile(r), :], contrib_hbm.at[tile(0), :], ssem_ref.at[slt]).wait()

    @pl.when(step == 0)
    def _():
        ys_ref[...] = jnp.zeros_like(ys_ref)
        gap = half - m_total
        fills = [(2 * half + s * MOE_BLOCK, MOE_BLOCK) for s in range(2)]
        fills += [(k * half + m_total, gap) for k in range(TOP_K)] if gap else []
        for start, n in fills:
            fill = pltpu.make_async_copy(ys_ref.at[0, pl.ds(0, n * TILE_ROWS), :],
                                         contrib_hbm.at[pl.ds(start * TILE_ROWS, n * TILE_ROWS), :], ssem_ref.at[0])
            fill.start()
            fill.wait()

    def weight_copies(e, s):
        return [pltpu.make_async_copy(src.at[e], dst.at[s], wsem_ref.at[s])
                for src, dst in ((wg_hbm, wgf_ref), (wu_hbm, wuf_ref), (wd_hbm, wdf_ref))]

    @pl.when(step == 0)
    def _():
        for c in weight_copies(bexp_ref[0], 0):
            c.start()
        gather(0, 0)

    def wait_scatters(slt):
        for r in range(MOE_BLOCK):
            scatter_wait(slt, r)

    def expert_weights(i):
        wslot = bslot_ref[i]

        @pl.when(wslot >= 0)
        def _():
            for c in weight_copies(0, wslot):
                c.wait()

        @pl.when(jnp.logical_and(wslot >= 0, bnext_ref[i] >= 0))
        def _():
            for c in weight_copies(bnext_ref[i], 1 - wslot):
                c.start()

        @pl.when(wslot >= 0)
        def _():
            wgb_ref[...] = wgf_ref[wslot].astype(BF16)
            wub_ref[...] = wuf_ref[wslot].astype(BF16)
            wdb_ref[...] = wdf_ref[wslot].astype(BF16)

    def run_block(i, slt):
        gather(jnp.minimum(i + 1, n_blocks - 1), 1 - slt)
        xb = _unpack_bf16_pairs(xs_ref[slt])
        gate = jnp.dot(xb, wgb_ref[...], preferred_element_type=F32)
        up = jnp.dot(xb, wub_ref[...], preferred_element_type=F32)
        hmid = (jax.nn.silu(gate) * up).astype(BF16)
        _store_token_tiles(ys_ref.at[slt], jnp.dot(hmid, wdb_ref[...], preferred_element_type=F32))
        for r in range(MOE_BLOCK):
            scatter(i, slt, r).start(priority=r % 2)

    for slt in range(2):
        i = 2 * step + slt
        active = i < n_used
        last_active = i == n_used - 1
        pl.when(jnp.logical_and(active, i >= 2))(functools.partial(wait_scatters, slt))
        expert_weights(i)
        pl.when(active)(functools.partial(run_block, i, slt))
        pl.when(last_active)(functools.partial(wait_scatters, slt))
        pl.when(jnp.logical_and(last_active, i >= 1))(functools.partial(wait_scatters, 1 - slt))


def _row_table_kernel(bpos_ref, bcnt_ref, order_ref, tok_ref, dst_ref, *, half, n_blocks):
    rows_per_block = MOE_BLOCK // LANES
    lane = lax.broadcasted_iota(jnp.int32, (rows_per_block, LANES), 1)
    r = lax.broadcasted_iota(jnp.int32, (rows_per_block, LANES), 0) * LANES + lane

    def per_block(b, c):
        p = bpos_ref[b]
        sh = p & (LANES - 1)
        win = order_ref[pl.ds(lax.shift_right_logical(p, 7), rows_per_block + 1), :]
        rolled = pltpu.roll(win, (LANES - sh) & (LANES - 1), axis=1)
        assign = jnp.where(lane < LANES - sh, rolled[:rows_per_block], rolled[1:])
        tok_ref[pl.ds(b * rows_per_block, rows_per_block), :] = jnp.where(assign >= half, assign - half, assign)
        trash = 2 * half + (b & 1) * MOE_BLOCK + r
        dst_ref[pl.ds(b * rows_per_block, rows_per_block), :] = jnp.where(r < bcnt_ref[b], assign, trash)
        return c
    lax.fori_loop(0, n_blocks, per_block, 0, unroll=6)


def _row_tables(order, bpos, bcnt, half):
    n_blocks = bpos.shape[0]
    assert MOE_BLOCK % LANES == 0 and order.shape[0] % LANES == 0
    out = jax.ShapeDtypeStruct((n_blocks * MOE_BLOCK // LANES, LANES), jnp.int32)
    tok, dst = pl.pallas_call(
        functools.partial(_row_table_kernel, half=half, n_blocks=n_blocks),
        grid_spec=pltpu.PrefetchScalarGridSpec(
            num_scalar_prefetch=2, grid=(1,),
            in_specs=[pl.BlockSpec((order.shape[0] // LANES, LANES), lambda i, *_: (0, 0))],
            out_specs=[pl.BlockSpec(out.shape, lambda i, *_: (0, 0))] * 2),
        out_shape=[out, out],
        name="row_tables",
    )(bpos, bcnt, order.reshape(-1, LANES))
    return tok.reshape(-1), dst.reshape(-1)


def _moe_experts(order, tables, h2p, w_gate, w_up, w_down, *, m_total, half):
    bexp, bpos, bcnt, bslot, bnext, n_used = tables
    n_blocks = bexp.shape[0]
    assert 0 <= half - m_total <= MOE_BLOCK and n_blocks % 2 == 0
    tok, dst = _row_tables(order, bpos, bcnt, half)
    tables = (bexp, bslot, bnext, n_used)
    hbm = pl.BlockSpec(memory_space=pl.ANY)
    grid_spec = pltpu.PrefetchScalarGridSpec(
        num_scalar_prefetch=2 + len(tables),
        grid=(n_blocks // 2,),
        in_specs=[_const_spec(h2p.shape), hbm, hbm, hbm],
        out_specs=hbm,
        scratch_shapes=[pltpu.VMEM((2, MOE_BLOCK, D_MODEL // 2), jnp.uint32),
                        pltpu.VMEM((2, MOE_BLOCK * TILE_ROWS, LANES), F32),
                        pltpu.VMEM((2, D_MODEL, D_EXPERT), F32),
                        pltpu.VMEM((2, D_MODEL, D_EXPERT), F32),
                        pltpu.VMEM((2, D_EXPERT, D_MODEL), F32),
                        pltpu.VMEM((D_MODEL, D_EXPERT), BF16),
                        pltpu.VMEM((D_MODEL, D_EXPERT), BF16),
                        pltpu.VMEM((D_EXPERT, D_MODEL), BF16),
                        pltpu.SemaphoreType.DMA((2,)),
                        pltpu.SemaphoreType.DMA((2,))],
    )
    return pl.pallas_call(
        functools.partial(_moe_kernel, m_total=m_total, half=half, n_blocks=n_blocks),
        grid_spec=grid_spec,
        out_shape=jax.ShapeDtypeStruct(((2 * half + 2 * MOE_BLOCK) * TILE_ROWS, LANES), F32),
        compiler_params=pltpu.CompilerParams(dimension_semantics=("arbitrary",),
                                             vmem_limit_bytes=MOE_VMEM_LIMIT_BYTES),
        name="moe_experts",
    )(tok, dst, *tables, h2p, w_gate, w_up, w_down)


def _combine_kernel(c0_ref, c1_ref, x2_ref, route_ref, gf_ref, y_ref):
    tc = x2_ref.shape[0]
    w = lax.bitcast_convert_type(route_ref[:, TOP_K:2 * TOP_K], F32)
    moe = w[:, 0:1] * _load_token_tiles(c0_ref, tc) + w[:, 1:2] * _load_token_tiles(c1_ref, tc)
    y_ref[...] = _rms_norm_f32(x2_ref[...] + moe, gf_ref[...])


def _moe_combine(contrib, x2, route, gf, *, row_off, m, tc, half):
    off = row_off // tc
    assert row_off % tc == 0 and half % tc == 0
    ctile = lambda k: pl.BlockSpec((tc * TILE_ROWS, LANES), lambda i: (i + off + k * (half // tc), 0))
    return pl.pallas_call(
        _combine_kernel,
        grid=(m // tc,),
        in_specs=[ctile(0), ctile(1),
                  pl.BlockSpec((tc, D_MODEL), lambda i: (i + off, 0)),
                  pl.BlockSpec((tc, ROUTER_LANES), lambda i: (i + off, 0)),
                  _const_spec((1, D_MODEL))],
        out_specs=pl.BlockSpec((tc, D_MODEL), lambda i: (i, 0)),
        out_shape=jax.ShapeDtypeStruct((m, D_MODEL), F32),
        compiler_params=pltpu.CompilerParams(dimension_semantics=("arbitrary",),
                                             vmem_limit_bytes=V7X_VMEM_LIMIT_BYTES),
        name="moe_combine_prompt" if row_off == 0 else "moe_combine_sample",
    )(contrib, contrib, x2, route, gf)


def _t5_bucket(dist):
    n = jnp.maximum(dist, 0)
    max_exact = N_BUCKETS // 2
    nf = jnp.maximum(n, 1).astype(F32)
    large = max_exact + (jnp.log(nf / max_exact) / math.log(MAX_DISTANCE / max_exact)
                         * (N_BUCKETS - max_exact)).astype(jnp.int32)
    large = jnp.minimum(large, N_BUCKETS - 1)
    return jnp.where(n < max_exact, n, large)


def _bucket_bias(rel_bias, dist, valid):
    buckets = _t5_bucket(dist).reshape(1, -1)
    onehot = (buckets == jnp.arange(N_BUCKETS, dtype=jnp.int32)[:, None]).astype(F32)
    bias = jnp.dot(rel_bias.astype(F32).T, onehot, precision=lax.Precision.HIGHEST)
    return jnp.where(valid.reshape(1, -1), bias, NEG_BIG).reshape((rel_bias.shape[1],) + dist.shape)


def _prompt_bias_table(rel_bias):
    qi = jnp.arange(ATTN_BLOCK, dtype=jnp.int32)[:, None]
    kj = jnp.arange(2 * ATTN_BLOCK, dtype=jnp.int32)[None, :] - ATTN_BLOCK
    dist = qi - kj
    return _bucket_bias(rel_bias, dist, (dist >= 0) & (dist <= WINDOW))


def _sample_bias_table(rel_bias, w_buf):
    dist = w_buf - jnp.arange(w_buf + 1, dtype=jnp.int32)
    return _bucket_bias(rel_bias, dist, dist <= WINDOW)


def kernel(x_prompt, x_sample, cache_conv, cache_k, cache_v, norm1_g, w_in, conv_w, w_conv_out, w_attn_out, w_o, sinks, rel_bias, norm2_g, w_router_group, b_router_group, w_router_expert, b_router_expert, w_e_gate, w_e_up, w_e_down, norm_f_g):
    assert norm1_g.shape[0] == 1, "single-layer configuration"
    batch, seq, _ = x_prompt.shape
    nseq = x_sample.shape[0]
    w_buf = cache_k.shape[2]
    mp = batch * seq
    m_total = mp + nseq
    assert seq % TM_DENSE == 0 and seq % TM_IN_PROJ == 0 and seq % ATTN_BLOCK == 0 and mp % COMBINE_BLOCK == 0
    assert nseq % SAMPLE_SEQ_PER_STEP == 0 and mp % nseq == 0
    assert TOP_K == 2 and MOE_BLOCK == 1 << MOE_BLOCK_LOG2 and m_total * TOP_K < 1 << ASSIGN_BITS

    g1 = norm1_g[0][None, :]
    g2 = norm2_g[0][None, :]
    gf = norm_f_g[None, :]
    wi = w_in[0].astype(BF16)
    cw = conv_w[0]
    wc = w_conv_out[0].astype(BF16)
    wa = w_attn_out[0].astype(BF16)
    wo = w_o[0].astype(BF16)
    pad_cols = ROUTER_LANES - N_EXPERT_GROUPS - N_EXPERTS
    wr = jnp.concatenate([w_router_group[0], w_router_expert[0],
                          jnp.zeros((D_MODEL, pad_cols), F32)], axis=1).astype(BF16)
    br = jnp.concatenate([b_router_group[0], b_router_expert[0], jnp.zeros((pad_cols,), F32)])[None, :]
    sink = sinks[0].astype(F32)

    xp = x_prompt.reshape(mp, D_MODEL)
    bps = seq // TM_IN_PROJ
    yc_p, q_p, k_p, v_p, sa_p, sb_p, ut_p, kvt_p = _in_proj(
        xp, g1, wi, cw, tm=TM_IN_PROJ, blocks_per_seq=bps, u_tail=8, kv_tail=WINDOW, parts=IN_PROJ_PARTS)
    o_p = _attn_prompt(q_p, k_p, v_p, _prompt_bias_table(rel_bias), sink, batch, seq)

    pad_rows = lambda t: jnp.pad(t, ((0, TM_DENSE - nseq), (0, 0)))
    xs = pad_rows(x_sample.reshape(nseq, D_MODEL))
    hist = (pad_rows(cache_conv[0][:, 0, :]), pad_rows(cache_conv[0][:, 1, :]))
    yc_s, q_s, _, _, sa_s, sb_s, ut_s, kvt_s = _in_proj(
        xs, g1, wi, cw, tm=TM_DENSE, blocks_per_seq=1, u_tail=TM_DENSE, kv_tail=TM_DENSE, hist=hist,
        gate_dtype=F32)
    u_s = ut_s[0, :nseq]
    kv_s = kvt_s[0, :nseq]
    head_mask = (jnp.arange(KV_DIM)[None, :] // HEAD_DIM == jnp.arange(N_HEADS)[:, None] // GROUP)
    qbd = (jnp.tile(q_s[:nseq].reshape(nseq, N_HEADS, HEAD_DIM), (1, 1, N_KV_HEADS))
           * head_mask[None].astype(BF16))
    to_keys_minor = lambda c: jnp.transpose(c.reshape(nseq, w_buf, KV_DIM), (0, 2, 1))
    from_keys_minor = lambda c: jnp.transpose(c, (0, 2, 1)).reshape(1, nseq, w_buf, N_KV_HEADS, HEAD_DIM)
    o_s, kwin_s, vwin_s = _attn_sample(qbd, to_keys_minor(cache_k[0]), to_keys_minor(cache_v[0]), kv_s,
                                       _sample_bias_table(rel_bias, w_buf), sink[:, None],
                                       head_mask.astype(F32))
    o_s = pad_rows(o_s.reshape(nseq, Q_DIM))

    half = -(-m_total // COMBINE_BLOCK) * COMBINE_BLOCK
    assert half % nseq == 0 and TOP_K * half < 1 << ASSIGN_BITS
    x2, h2p, route, route_t, cnt = _out_proj((yc_p, o_p, sa_p, sb_p, xp), (yc_s, o_s, sa_s, sb_s, xs),
                                             wc, wa, wo, g2, wr, br, tm=TM_DENSE, valid_rows_b=nseq, half=half)

    n_assign = m_total * TOP_K
    keys = route_t[0:TOP_K, :m_total].reshape(-1)
    counts = cnt[0, N_EXPERT_GROUPS:N_EXPERT_GROUPS + N_EXPERTS].astype(jnp.int32)
    order = jnp.pad(jnp.sort(keys) & ((1 << ASSIGN_BITS) - 1), (0, MOE_BLOCK))
    n_blocks = -(-n_assign // MOE_BLOCK) + N_EXPERTS
    n_blocks += n_blocks % 2
    tables = _block_tables(counts, n_blocks)
    contrib = _moe_experts(order, tables, h2p, w_e_gate[0], w_e_up[0], w_e_down[0],
                           m_total=m_total, half=half)
    y_p = _moe_combine(contrib, x2, route, gf, row_off=0, m=mp, tc=COMBINE_BLOCK, half=half)
    y_s = _moe_combine(contrib, x2, route, gf, row_off=mp, m=nseq, tc=nseq, half=half)

    y_prompt = y_p.reshape(batch, seq, D_MODEL)
    y_sample = y_s.reshape(nseq, 1, D_MODEL)
    conv_state_prompt = ut_p.reshape(batch, bps, 8, D_CONV)[:, -1, 8 - (CONV_WIDTH - 1):, :][None]
    kv_last = kvt_p.reshape(batch, bps, WINDOW, 2 * KV_DIM)[:, -1]
    k_win_prompt = kv_last[:, :, :KV_DIM].reshape(batch, WINDOW, N_KV_HEADS, HEAD_DIM)[None]
    v_win_prompt = kv_last[:, :, KV_DIM:].reshape(batch, WINDOW, N_KV_HEADS, HEAD_DIM)[None]
    conv_state_sample = jnp.concatenate([cache_conv[0][:, 1:, :], u_s[:, None, :]], axis=1)[None]
    k_win_sample = from_keys_minor(kwin_s)
    v_win_sample = from_keys_minor(vwin_s)
    return (y_prompt, y_sample, conv_state_prompt, k_win_prompt, v_win_prompt,
            conv_state_sample, k_win_sample, v_win_sample)
```

```python
import functools
import math

import jax
import jax.numpy as jnp
from jax import lax
from jax.experimental import pallas as pl
from jax.experimental.pallas import tpu as pltpu

D_MODEL = 1024
D_CONV = 1024
CONV_WIDTH = 3
N_HEADS = 16
N_KV_HEADS = 4
HEAD_DIM = 64
GROUP = N_HEADS // N_KV_HEADS
WINDOW = 128
Q_DIM = N_HEADS * HEAD_DIM
KV_DIM = N_KV_HEADS * HEAD_DIM
N_BUCKETS = 32
MAX_DISTANCE = 128
N_EXPERT_GROUPS = 4
EXPERTS_PER_GROUP = 8
N_EXPERTS = N_EXPERT_GROUPS * EXPERTS_PER_GROUP
TOP_K = 2
D_EXPERT = 512
EPS = 1e-6
PAST_LEN = 8192

BF16 = jnp.bfloat16
F32 = jnp.float32
NEG_BIG = -1e30

V7X_VMEM_LIMIT_BYTES = 56 * 1024 * 1024
MOE_VMEM_LIMIT_BYTES = 62 * 1024 * 1024
TILE_ROWS = 8
LANES = 128
ROUTER_LANES = 128
TM_DENSE = 512
TM_IN_PROJ = 1024
OUT_PROJ_PARTS = 4
IN_PROJ_PARTS = 8
ATTN_BLOCK = 128
MOE_BLOCK = 256
MOE_BLOCK_LOG2 = 8
ASSIGN_BITS = 16
COMBINE_BLOCK = 256
SAMPLE_SEQ_PER_STEP = 16
HEADS_PER_STORE = LANES // HEAD_DIM


def _const_spec(shape):
    nd = len(shape)
    return pl.BlockSpec(shape, lambda *_: (0,) * nd, pipeline_mode=pl.Buffered(1))


def _rms_norm_f32(xf, g):
    return xf * lax.rsqrt(jnp.mean(xf * xf, axis=-1, keepdims=True) + EPS) * g


def _in_proj_kernel(*refs, tm, sample, blocks_per_seq, u_tail, kv_tail, parts):
    if sample:
        (x_ref, hist0_ref, hist1_ref, g_ref, w_ref,
         cw_ref, yc_ref, q_ref, k_ref, v_ref, sa_ref, sb_ref, ut_ref, kvt_ref) = refs
    else:
        (x_ref, g_ref, w_ref,
         cw_ref, yc_ref, q_ref, k_ref, v_ref, sa_ref, sb_ref, ut_ref, kvt_ref, ubuf_ref) = refs

    widths = (D_CONV, D_CONV, D_CONV, Q_DIM, 2 * KV_DIM, D_MODEL, D_MODEL)
    starts = [sum(widths[:n]) for n in range(len(widths))]
    wcb_ref, wcc_ref, wch_ref, wq_ref, wkv_ref, wga_ref, wgb_ref = [
        w_ref.at[:, pl.ds(a, n)] for a, n in zip(starts, widths)]
    w0 = cw_ref[0:1, :]
    w1 = cw_ref[1:2, :]
    w2 = cw_ref[2:3, :]

    if not sample:
        @pl.when(pl.program_id(0) % blocks_per_seq == 0)
        def _():
            ubuf_ref[0:8, :] = jnp.zeros((8, D_CONV), F32)

    n = tm // parts
    assert u_tail <= n and kv_tail <= n
    for r0 in range(0, tm, n):
        rows = pl.ds(r0, n)
        last = r0 + n == tm
        h = _rms_norm_f32(x_ref[rows, :], g_ref[...]).astype(BF16)

        def proj(part_ref, h=h):
            return jnp.dot(h, part_ref[...], preferred_element_type=F32)

        u = proj(wcc_ref) * proj(wch_ref)
        if sample:
            conv = w0 * hist0_ref[rows, :] + w1 * hist1_ref[rows, :] + w2 * u
        else:
            ubuf_ref[pl.ds(8 + r0, n), :] = u
            conv = w0 * ubuf_ref[pl.ds(6 + r0, n), :] + w1 * ubuf_ref[pl.ds(7 + r0, n), :] + w2 * u
            if last:
                ubuf_ref[0:8, :] = u[n - 8:, :]
        yc_ref[rows, :] = (proj(wcb_ref) * conv).astype(BF16)
        q_ref[rows, :] = (proj(wq_ref) * (HEAD_DIM ** -0.5)).astype(BF16)
        kv = proj(wkv_ref)
        k_ref[rows, :] = kv[:, :KV_DIM].astype(BF16)
        v_ref[rows, :] = kv[:, KV_DIM:].astype(BF16)
        if last:
            ut_ref[0] = u[n - u_tail:, :]
            kvt_ref[0] = kv[n - kv_tail:, :]
        sa_ref[rows, :] = jax.nn.sigmoid(proj(wga_ref)).astype(sa_ref.dtype)
        sb_ref[rows, :] = jax.nn.sigmoid(proj(wgb_ref)).astype(sb_ref.dtype)


def _in_proj(x, g1, w_in, conv_w, *, tm, blocks_per_seq, u_tail, kv_tail, hist=None, gate_dtype=BF16, parts=1):
    m = x.shape[0]
    nblk = m // tm
    sample = hist is not None
    row = lambda width: pl.BlockSpec((tm, width), lambda i: (i, 0))
    in_specs = [row(D_MODEL)]
    args = [x]
    if sample:
        in_specs += [row(D_CONV), row(D_CONV)]
        args += list(hist)
    in_specs += [_const_spec((1, D_MODEL)), _const_spec(w_in.shape), _const_spec(conv_w.shape)]
    args += [g1, w_in, conv_w]
    out_shape = [
        jax.ShapeDtypeStruct((m, D_CONV), BF16),
        jax.ShapeDtypeStruct((m, Q_DIM), BF16),
        jax.ShapeDtypeStruct((m, KV_DIM), BF16),
        jax.ShapeDtypeStruct((m, KV_DIM), BF16),
        jax.ShapeDtypeStruct((m, D_MODEL), gate_dtype),
        jax.ShapeDtypeStruct((m, D_MODEL), gate_dtype),
        jax.ShapeDtypeStruct((nblk, u_tail, D_CONV), F32),
        jax.ShapeDtypeStruct((nblk, kv_tail, 2 * KV_DIM), F32),
    ]
    out_specs = [row(D_CONV), row(Q_DIM), row(KV_DIM), row(KV_DIM), row(D_MODEL), row(D_MODEL),
                 pl.BlockSpec((1, u_tail, D_CONV), lambda i: (i, 0, 0)),
                 pl.BlockSpec((1, kv_tail, 2 * KV_DIM), lambda i: (i, 0, 0))]
    scratch = [] if sample else [pltpu.VMEM((tm + 8, D_CONV), F32)]
    return pl.pallas_call(
        functools.partial(_in_proj_kernel, tm=tm, sample=sample, blocks_per_seq=blocks_per_seq,
                          u_tail=u_tail, kv_tail=kv_tail, parts=parts),
        grid=(nblk,),
        in_specs=in_specs,
        out_specs=out_specs,
        out_shape=out_shape,
        scratch_shapes=scratch,
        compiler_params=pltpu.CompilerParams(dimension_semantics=("arbitrary",),
                                             vmem_limit_bytes=MOE_VMEM_LIMIT_BYTES),
        name="in_proj_sample" if sample else "in_proj_prompt",
    )(*args)


def _attn_prompt_kernel(sink_ref, q_ref, kc_ref, kp_ref, vc_ref, vp_ref, bias_ref, o_ref):
    first = pl.program_id(1) == 0
    col = lax.broadcasted_iota(jnp.int32, (ATTN_BLOCK, 2 * ATTN_BLOCK), 1)
    no_prev = jnp.logical_and(first, col < ATTN_BLOCK)
    for g in range(N_KV_HEADS):
        ks = slice(g * HEAD_DIM, (g + 1) * HEAD_DIM)
        kcat = jnp.concatenate([kp_ref[:, ks], kc_ref[:, ks]], axis=0)
        vcat = jnp.concatenate([vp_ref[:, ks], vc_ref[:, ks]], axis=0)
        for h0 in range(g * GROUP, (g + 1) * GROUP, HEADS_PER_STORE):
            outs = []
            for h in range(h0, h0 + HEADS_PER_STORE):
                hs = slice(h * HEAD_DIM, (h + 1) * HEAD_DIM)
                s = lax.dot_general(q_ref[:, hs], kcat, (((1,), (1,)), ((), ())),
                                    preferred_element_type=F32)
                s = jnp.where(no_prev, NEG_BIG, s + bias_ref[h])
                sink = sink_ref[h]
                m = jnp.maximum(jnp.max(s, axis=-1, keepdims=True), sink)
                p = jnp.exp(s - m)
                denom = jnp.sum(p, axis=-1, keepdims=True) + jnp.exp(sink - m)
                o = jnp.dot(p.astype(BF16), vcat, preferred_element_type=F32)
                outs.append((o / denom).astype(BF16))
            o_ref[:, h0 * HEAD_DIM:(h0 + HEADS_PER_STORE) * HEAD_DIM] = jnp.concatenate(outs, axis=1)


def _attn_prompt(q, k, v, bias, sinks, batch, seq):
    nb = seq // ATTN_BLOCK
    cur = lambda b, i: (b * nb + i, 0)
    prev = lambda b, i: (b * nb + jnp.maximum(i - 1, 0), 0)
    return pl.pallas_call(
        _attn_prompt_kernel,
        grid=(batch, nb),
        in_specs=[pl.BlockSpec(memory_space=pltpu.SMEM),
                  pl.BlockSpec((ATTN_BLOCK, Q_DIM), cur),
                  pl.BlockSpec((ATTN_BLOCK, KV_DIM), cur),
                  pl.BlockSpec((ATTN_BLOCK, KV_DIM), prev),
                  pl.BlockSpec((ATTN_BLOCK, KV_DIM), cur),
                  pl.BlockSpec((ATTN_BLOCK, KV_DIM), prev),
                  _const_spec(bias.shape)],
        out_specs=pl.BlockSpec((ATTN_BLOCK, Q_DIM), cur),
        out_shape=jax.ShapeDtypeStruct((batch * seq, Q_DIM), BF16),
        compiler_params=pltpu.CompilerParams(dimension_semantics=("arbitrary", "arbitrary"),
                                             vmem_limit_bytes=V7X_VMEM_LIMIT_BYTES),
        name="attn_prompt",
    )(sinks, q, k, k, v, v, bias)


def _attn_sample_kernel(qbd_ref, ckt_ref, cvt_ref, kvn_ref, bias_ref, sink_ref, mask_ref,
                        o_ref, kwin_ref, vwin_ref, *, w_buf):
    bf16_round = lambda t: t.astype(BF16).astype(F32)
    seqs = range(SAMPLE_SEQ_PER_STEP)
    sink = sink_ref[...]
    newest = lax.broadcasted_iota(jnp.int32, (KV_DIM, w_buf), 1) == w_buf - 1
    kvn_t = kvn_ref[...].T
    for b in seqs:
        kwin_ref[b] = jnp.where(newest, kvn_t[:KV_DIM, b:b + 1], pltpu.roll(ckt_ref[b], w_buf - 1, axis=1))
        vwin_ref[b] = jnp.where(newest, kvn_t[KV_DIM:, b:b + 1], pltpu.roll(cvt_ref[b], w_buf - 1, axis=1))
    s = [jnp.dot(qbd_ref[b], ckt_ref[b].astype(BF16), preferred_element_type=F32) + bias_ref[:, :w_buf]
         for b in seqs]
    s_new = [jnp.sum(qbd_ref[b].astype(F32) * bf16_round(kvn_ref[b:b + 1, :KV_DIM]), axis=-1, keepdims=True)
             + bias_ref[:, w_buf:w_buf + 1] for b in seqs]
    m = [jnp.maximum(jnp.maximum(jnp.max(s[b], axis=-1, keepdims=True), s_new[b]), sink) for b in seqs]
    p = [jnp.exp(s[b] - m[b]) for b in seqs]
    p_new = [jnp.exp(s_new[b] - m[b]) for b in seqs]
    denom = [jnp.sum(p[b], axis=-1, keepdims=True) + p_new[b] + jnp.exp(sink - m[b]) for b in seqs]
    of = [lax.dot_general((p[b] / denom[b]).astype(BF16), cvt_ref[b].astype(BF16), (((1,), (1,)), ((), ())),
                          preferred_element_type=F32)
          + bf16_round(p_new[b] / denom[b]) * bf16_round(kvn_ref[b:b + 1, KV_DIM:]) for b in seqs]
    for b in seqs:
        ob = of[b] * mask_ref[...]
        o_ref[b] = (ob[:, 0:HEAD_DIM] + ob[:, HEAD_DIM:2 * HEAD_DIM]
                    + ob[:, 2 * HEAD_DIM:3 * HEAD_DIM] + ob[:, 3 * HEAD_DIM:]).astype(BF16)


def _attn_sample(qbd, ckt, cvt, kvn, bias, sink_col, head_mask):
    nseq, w_buf = ckt.shape[0], ckt.shape[2]
    sb = SAMPLE_SEQ_PER_STEP
    seq3 = lambda d1, d2: pl.BlockSpec((sb, d1, d2), lambda i: (i, 0, 0))
    win = jax.ShapeDtypeStruct((nseq, KV_DIM, w_buf), F32)
    return pl.pallas_call(
        functools.partial(_attn_sample_kernel, w_buf=w_buf),
        grid=(nseq // sb,),
        in_specs=[seq3(N_HEADS, KV_DIM), seq3(KV_DIM, w_buf), seq3(KV_DIM, w_buf),
                  pl.BlockSpec((sb, 2 * KV_DIM), lambda i: (i, 0)),
                  _const_spec(bias.shape), _const_spec(sink_col.shape), _const_spec(head_mask.shape)],
        out_specs=[seq3(N_HEADS, HEAD_DIM), seq3(KV_DIM, w_buf), seq3(KV_DIM, w_buf)],
        out_shape=[jax.ShapeDtypeStruct((nseq, N_HEADS, HEAD_DIM), BF16), win, win],
        compiler_params=pltpu.CompilerParams(dimension_semantics=("arbitrary",),
                                             vmem_limit_bytes=V7X_VMEM_LIMIT_BYTES),
        name="attn_sample",
    )(qbd, ckt, cvt, kvn, bias, sink_col, head_mask)


def _route_rows(logits, row0, valid_rows, half):
    tm = logits.shape[0]
    lane = lax.broadcasted_iota(jnp.int32, logits.shape, 1)
    lane_f = lane.astype(F32)
    no_lane = float(ROUTER_LANES)

    def top1(mask):
        best = jnp.max(jnp.where(mask, logits, -jnp.inf), axis=-1, keepdims=True)
        idx = jnp.min(jnp.where(jnp.logical_and(mask, logits == best), lane_f, no_lane), axis=-1, keepdims=True)
        return best, idx

    gmask = lane < N_EXPERT_GROUPS
    gmax, grp = top1(gmask)
    gsum = jnp.sum(jnp.where(gmask, jnp.exp(logits - gmax), 0.0), axis=-1, keepdims=True)
    p_grp = 1.0 / gsum
    lo = N_EXPERT_GROUPS + EXPERTS_PER_GROUP * grp
    emask = jnp.logical_and(lane_f >= lo, lane_f < lo + EXPERTS_PER_GROUP)
    v1, i1 = top1(emask)
    v2, i2 = top1(jnp.logical_and(emask, lane_f != i1))
    e21 = jnp.exp(v2 - v1)
    w1 = p_grp / (1.0 + e21)
    w2 = p_grp * e21 / (1.0 + e21)

    oh1 = lane_f == i1
    oh2 = lane_f == i2
    if valid_rows < tm:
        valid = lax.broadcasted_iota(jnp.int32, logits.shape, 0) < valid_rows
        oh1 = jnp.logical_and(oh1, valid)
        oh2 = jnp.logical_and(oh2, valid)
    oh = oh1.astype(F32) + oh2.astype(F32)
    token = row0 + lax.broadcasted_iota(jnp.int32, (tm, 1), 0)
    key1 = (i1.astype(jnp.int32) - N_EXPERT_GROUPS) * (1 << ASSIGN_BITS) + token
    key2 = (i2.astype(jnp.int32) - N_EXPERT_GROUPS) * (1 << ASSIGN_BITS) + token + half
    w1b = lax.bitcast_convert_type(w1, jnp.int32)
    w2b = lax.bitcast_convert_type(w2, jnp.int32)
    words = jnp.where(lane == 0, key1, jnp.where(lane == 1, key2, jnp.where(lane == 2, w1b,
                      jnp.where(lane == 3, w2b, 0))))
    return words, jnp.sum(oh, axis=0, keepdims=True)


def _store_token_tiles(ref, x):
    n = x.shape[0]
    for c in range(D_MODEL // LANES):
        ref[pl.ds(c, n, stride=TILE_ROWS), :] = x[:, c * LANES:(c + 1) * LANES]


def _load_token_tiles(ref, n):
    return jnp.concatenate([ref[pl.ds(c, n, stride=TILE_ROWS), :] for c in range(D_MODEL // LANES)], axis=1)


def _pack_bf16_pairs(x):
    hw = x.shape[1] // 2
    bits = lambda v: lax.bitcast_convert_type(v.astype(BF16).astype(F32), jnp.uint32)
    return (bits(x[:, hw:]) & jnp.uint32(0xFFFF0000)) | (bits(x[:, :hw]) >> 16)


def _unpack_bf16_pairs(w):
    lo = lax.bitcast_convert_type(w << 16, F32)
    hi = lax.bitcast_convert_type(w & jnp.uint32(0xFFFF0000), F32)
    return jnp.concatenate([lo, hi], axis=1).astype(BF16)


def _out_proj_rows(yc_ref, o_ref, sa_ref, sb_ref, x_ref, wc_ref, wa_ref, wo_ref, g2_ref, wr_ref, br_ref,
                   x2_ref, h2p_ref, route_ref, keys_ref, cnt_ref, *, valid_rows, half):
    tm = x_ref.shape[0]
    parts = [(r0, tm // OUT_PROJ_PARTS) for r0 in range(0, tm, tm // OUT_PROJ_PARTS)]
    for r0, n in parts:
        rows = pl.ds(r0, n)
        y_conv = jnp.dot(yc_ref[rows, :], wc_ref[...], preferred_element_type=F32)
        y_attn = jnp.dot(o_ref[rows, :], wa_ref[...], preferred_element_type=F32)
        mix = (sa_ref[rows, :].astype(F32) * y_conv + sb_ref[rows, :].astype(F32) * y_attn).astype(BF16)
        x2_ref[rows, :] = x_ref[rows, :] + jnp.dot(mix, wo_ref[...], preferred_element_type=F32)
    for r0, n in parts:
        rows = pl.ds(r0, n)
        h2 = _rms_norm_f32(x2_ref[rows, :], g2_ref[...])
        h2p_ref[rows, :] = _pack_bf16_pairs(h2)
        logits = jnp.dot(h2.astype(BF16), wr_ref[...], preferred_element_type=F32) + br_ref[...]
        words, cnt = _route_rows(logits, pl.program_id(0) * tm + r0, min(max(valid_rows - r0, 0), n), half)
        route_ref[rows, :] = words
        keys_ref[:, rows] = words.T[:TILE_ROWS, :]
        cnt_ref[...] += cnt


def _out_proj_kernel(*refs, n_first, valid_rows_second, half):
    first, second, shared = refs[0:5], refs[5:10], refs[10:]
    cnt_ref = shared[-1]
    tm = first[4].shape[0]

    @pl.when(pl.program_id(0) == 0)
    def _():
        cnt_ref[...] = jnp.zeros_like(cnt_ref)

    @pl.when(pl.program_id(0) < n_first)
    def _():
        _out_proj_rows(*first, *shared, valid_rows=tm, half=half)

    @pl.when(pl.program_id(0) >= n_first)
    def _():
        _out_proj_rows(*second, *shared, valid_rows=valid_rows_second, half=half)


def _out_proj(acts_a, acts_b, wc, wa, wo, g2, wr, br, *, tm, valid_rows_b, half):
    na = acts_a[4].shape[0] // tm
    nb = acts_b[4].shape[0] // tm
    assert nb == 1
    m_total = (na + nb) * tm
    spec_a = lambda width: pl.BlockSpec((tm, width), lambda i: (jnp.minimum(i, na - 1), 0))
    spec_b = lambda width: pl.BlockSpec((tm, width), lambda i: (jnp.maximum(i - na, 0), 0))
    widths = (D_CONV, Q_DIM, D_MODEL, D_MODEL, D_MODEL)
    in_specs = [spec_a(w) for w in widths] + [spec_b(w) for w in widths]
    in_specs += [_const_spec(wc.shape), _const_spec(wa.shape), _const_spec(wo.shape),
                 _const_spec(g2.shape), _const_spec(wr.shape), _const_spec(br.shape)]
    orow = lambda width: pl.BlockSpec((tm, width), lambda i: (i, 0))
    return pl.pallas_call(
        functools.partial(_out_proj_kernel, n_first=na, valid_rows_second=valid_rows_b, half=half),
        grid=(na + nb,),
        in_specs=in_specs,
        out_specs=[orow(D_MODEL), orow(D_MODEL // 2), orow(ROUTER_LANES),
                   pl.BlockSpec((TILE_ROWS, tm), lambda i: (0, i)),
                   pl.BlockSpec((1, ROUTER_LANES), lambda i: (0, 0))],
        out_shape=[jax.ShapeDtypeStruct((m_total, D_MODEL), F32),
                   jax.ShapeDtypeStruct((m_total, D_MODEL // 2), jnp.uint32),
                   jax.ShapeDtypeStruct((m_total, ROUTER_LANES), jnp.int32),
                   jax.ShapeDtypeStruct((TILE_ROWS, m_total), jnp.int32),
                   jax.ShapeDtypeStruct((1, ROUTER_LANES), F32)],
        compiler_params=pltpu.CompilerParams(dimension_semantics=("arbitrary",),
                                             vmem_limit_bytes=V7X_VMEM_LIMIT_BYTES),
        name="out_proj",
    )(*acts_a, *acts_b, wc, wa, wo, g2, wr, br)


def _block_table_kernel(counts_ref, bexp_ref, bpos_ref, bcnt_ref, bslot_ref, bnext_ref, nused_ref, first_ref,
                        *, n_blocks):
    def per_expert(e, carry):
        blk0, pos0, ordinal = carry
        cnt = counts_ref[e]
        nblk = lax.shift_right_logical(cnt + (MOE_BLOCK - 1), MOE_BLOCK_LOG2)
        first_ref[e] = jnp.where(nblk > 0, blk0, -1)

        def mark(b, c):
            off = (b - blk0) * MOE_BLOCK
            bexp_ref[b] = e
            bpos_ref[b] = pos0 + off
            bcnt_ref[b] = jnp.minimum(cnt - off, MOE_BLOCK)
            bslot_ref[b] = jnp.where(b == blk0, ordinal & 1, -1)
            bnext_ref[b] = -1
            return c
        lax.fori_loop(blk0, blk0 + nblk, mark, 0)
        return blk0 + nblk, pos0 + cnt, ordinal + jnp.where(nblk > 0, 1, 0)

    n_used, _, _ = lax.fori_loop(0, N_EXPERTS, per_expert, (0, 0, 0))
    nused_ref[0] = n_used

    def unused(b, c):
        bexp_ref[b] = N_EXPERTS - 1
        bpos_ref[b] = 0
        bcnt_ref[b] = 0
        bslot_ref[b] = -1
        bnext_ref[b] = -1
        return c
    lax.fori_loop(n_used, n_blocks, unused, 0)

    def link(k, nxt):
        e = N_EXPERTS - 1 - k
        fb = first_ref[e]

        @pl.when(fb >= 0)
        def _():
            bnext_ref[fb] = nxt
        return jnp.where(fb >= 0, e, nxt)
    lax.fori_loop(0, N_EXPERTS, link, -1)


def _block_tables(counts, n_blocks):
    smem = pl.BlockSpec(memory_space=pltpu.SMEM)
    blk = jax.ShapeDtypeStruct((n_blocks,), jnp.int32)
    return pl.pallas_call(
        functools.partial(_block_table_kernel, n_blocks=n_blocks),
        in_specs=[smem],
        out_specs=[smem] * 6,
        out_shape=[blk] * 5 + [jax.ShapeDtypeStruct((1,), jnp.int32)],
        scratch_shapes=[pltpu.SMEM((N_EXPERTS,), jnp.int32)],
        name="block_tables",
    )(counts)


def _moe_kernel(tok_ref, dst_ref, bexp_ref, bslot_ref, bnext_ref, nused_ref,
                h2p_ref, wg_hbm, wu_hbm, wd_hbm, contrib_hbm,
                xs_ref, ys_ref, wgf_ref, wuf_ref, wdf_ref, wgb_ref, wub_ref, wdb_ref, ssem_ref, wsem_ref,
                *, m_total, half, n_blocks):
    step = pl.program_id(0)
    n_used = nused_ref[0]
    tile = lambda t: pl.ds(pl.multiple_of(t * TILE_ROWS, TILE_ROWS), TILE_ROWS)

    def gather(blk, slt):
        for r in range(MOE_BLOCK):
            xs_ref[slt, pl.ds(r, 1), :] = h2p_ref[pl.ds(tok_ref[blk * MOE_BLOCK + r], 1), :]

    def scatter(blk, slt, r):
        return pltpu.make_async_copy(ys_ref.at[slt, tile(r), :], contrib_hbm.at[tile(dst_ref[blk * MOE_BLOCK + r]), :],
                                     ssem_ref.at[slt])

    def scatter_wait(slt, r):
        pltpu.make_async_copy(ys_ref.at[slt, tile(r), :], contrib_hbm.at[tile(0), :], ssem_ref.at[slt]).wait()

    @pl.when(step == 0)
    def _():
        ys_ref[...] = jnp.zeros_like(ys_ref)
        gap = half - m_total
        fills = [(2 * half + s * MOE_BLOCK, MOE_BLOCK) for s in range(2)]
        fills += [(k * half + m_total, gap) for k in range(TOP_K)] if gap else []
        for start, n in fills:
            fill = pltpu.make_async_copy(ys_ref.at[0, pl.ds(0, n * TILE_ROWS), :],
                                         contrib_hbm.at[pl.ds(start * TILE_ROWS, n * TILE_ROWS), :], ssem_ref.at[0])
            fill.start()
            fill.wait()

    def weight_copies(e, s):
        return [pltpu.make_async_copy(src.at[e], dst.at[s], wsem_ref.at[s])
                for src, dst in ((wg_hbm, wgf_ref), (wu_hbm, wuf_ref), (wd_hbm, wdf_ref))]

    @pl.when(step == 0)
    def _():
        for c in weight_copies(bexp_ref[0], 0):
            c.start()
        gather(0, 0)

    def wait_scatters(slt):
        for r in range(MOE_BLOCK):
            scatter_wait(slt, r)

    def expert_weights(i):
        wslot = bslot_ref[i]

        @pl.when(wslot >= 0)
        def _():
            for c in weight_copies(0, wslot):
                c.wait()

        @pl.when(jnp.logical_and(wslot >= 0, bnext_ref[i] >= 0))
        def _():
            for c in weight_copies(bnext_ref[i], 1 - wslot):
                c.start()

        @pl.when(wslot >= 0)
        def _():
            wgb_ref[...] = wgf_ref[wslot].astype(BF16)
            wub_ref[...] = wuf_ref[wslot].astype(BF16)
            wdb_ref[...] = wdf_ref[wslot].astype(BF16)

    def run_block(i, slt):
        gather(jnp.minimum(i + 1, n_blocks - 1), 1 - slt)
        xb = _unpack_bf16_pairs(xs_ref[slt])
        gate = jnp.dot(xb, wgb_ref[...], preferred_element_type=F32)
        up = jnp.dot(xb, wub_ref[...], preferred_element_type=F32)
        hmid = (jax.nn.silu(gate) * up).astype(BF16)
        _store_token_tiles(ys_ref.at[slt], jnp.dot(hmid, wdb_ref[...], preferred_element_type=F32))
        for r in range(MOE_BLOCK):
            scatter(i, slt, r).start(priority=r % 2)

    for slt in range(2):
        i = 2 * step + slt
        active = i < n_used
        last_active = i == n_used - 1
        pl.when(jnp.logical_and(active, i >= 2))(functools.partial(wait_scatters, slt))
        expert_weights(i)
        pl.when(active)(functools.partial(run_block, i, slt))
        pl.when(last_active)(functools.partial(wait_scatters, slt))
        pl.when(jnp.logical_and(last_active, i >= 1))(functools.partial(wait_scatters, 1 - slt))


def _row_table_kernel(bpos_ref, bcnt_ref, order_ref, tok_ref, dst_ref, *, half, n_blocks):
    rows_per_block = MOE_BLOCK // LANES
    lane = lax.broadcasted_iota(jnp.int32, (rows_per_block, LANES), 1)
    r = lax.broadcasted_iota(jnp.int32, (rows_per_block, LANES), 0) * LANES + lane

    def per_block(b, c):
        p = bpos_ref[b]
        sh = p & (LANES - 1)
        win = order_ref[pl.ds(lax.shift_right_logical(p, 7), rows_per_block + 1), :]
        rolled = pltpu.roll(win, (LANES - sh) & (LANES - 1), axis=1)
        assign = jnp.where(lane < LANES - sh, rolled[:rows_per_block], rolled[1:])
        tok_ref[pl.ds(b * rows_per_block, rows_per_block), :] = jnp.where(assign >= half, assign - half, assign)
        trash = 2 * half + (b & 1) * MOE_BLOCK + r
        dst_ref[pl.ds(b * rows_per_block, rows_per_block), :] = jnp.where(r < bcnt_ref[b], assign, trash)
        return c
    lax.fori_loop(0, n_blocks, per_block, 0, unroll=6)


def _row_tables(order, bpos, bcnt, half):
    n_blocks = bpos.shape[0]
    assert MOE_BLOCK % LANES == 0 and order.shape[0] % LANES == 0
    out = jax.ShapeDtypeStruct((n_blocks * MOE_BLOCK // LANES, LANES), jnp.int32)
    tok, dst = pl.pallas_call(
        functools.partial(_row_table_kernel, half=half, n_blocks=n_blocks),
        grid_spec=pltpu.PrefetchScalarGridSpec(
            num_scalar_prefetch=2, grid=(1,),
            in_specs=[pl.BlockSpec((order.shape[0] // LANES, LANES), lambda i, *_: (0, 0))],
            out_specs=[pl.BlockSpec(out.shape, lambda i, *_: (0, 0))] * 2),
        out_shape=[out, out],
        name="row_tables",
    )(bpos, bcnt, order.reshape(-1, LANES))
    return tok.reshape(-1), dst.reshape(-1)


def _moe_experts(order, tables, h2p, w_gate, w_up, w_down, *, m_total, half):
    bexp, bpos, bcnt, bslot, bnext, n_used = tables
    n_blocks = bexp.shape[0]
    assert 0 <= half - m_total <= MOE_BLOCK and n_blocks % 2 == 0
    tok, dst = _row_tables(order, bpos, bcnt, half)
    tables = (bexp, bslot, bnext, n_used)
    hbm = pl.BlockSpec(memory_space=pl.ANY)
    grid_spec = pltpu.PrefetchScalarGridSpec(
        num_scalar_prefetch=2 + len(tables),
        grid=(n_blocks // 2,),
        in_specs=[_const_spec(h2p.shape), hbm, hbm, hbm],
        out_specs=hbm,
        scratch_shapes=[pltpu.VMEM((2, MOE_BLOCK, D_MODEL // 2), jnp.uint32),
                        pltpu.VMEM((2, MOE_BLOCK * TILE_ROWS, LANES), F32),
                        pltpu.VMEM((2, D_MODEL, D_EXPERT), F32),
                        pltpu.VMEM((2, D_MODEL, D_EXPERT), F32),
                        pltpu.VMEM((2, D_EXPERT, D_MODEL), F32),
                        pltpu.VMEM((D_MODEL, D_EXPERT), BF16),
                        pltpu.VMEM((D_MODEL, D_EXPERT), BF16),
                        pltpu.VMEM((D_EXPERT, D_MODEL), BF16),
                        pltpu.SemaphoreType.DMA((2,)),
                        pltpu.SemaphoreType.DMA((2,))],
    )
    return pl.pallas_call(
        functools.partial(_moe_kernel, m_total=m_total, half=half, n_blocks=n_blocks),
        grid_spec=grid_spec,
        out_shape=jax.ShapeDtypeStruct(((2 * half + 2 * MOE_BLOCK) * TILE_ROWS, LANES), F32),
        compiler_params=pltpu.CompilerParams(dimension_semantics=("arbitrary",),
                                             vmem_limit_bytes=MOE_VMEM_LIMIT_BYTES),
        name="moe_experts",
    )(tok, dst, *tables, h2p, w_gate, w_up, w_down)


def _combine_kernel(c0_ref, c1_ref, x2_ref, route_ref, gf_ref, y_ref):
    tc = x2_ref.shape[0]
    w = lax.bitcast_convert_type(route_ref[:, TOP_K:2 * TOP_K], F32)
    moe = w[:, 0:1] * _load_token_tiles(c0_ref, tc) + w[:, 1:2] * _load_token_tiles(c1_ref, tc)
    y_ref[...] = _rms_norm_f32(x2_ref[...] + moe, gf_ref[...])


def _moe_combine(contrib, x2, route, gf, *, row_off, m, tc, half):
    off = row_off // tc
    assert row_off % tc == 0 and half % tc == 0
    ctile = lambda k: pl.BlockSpec((tc * TILE_ROWS, LANES), lambda i: (i + off + k * (half // tc), 0))
    return pl.pallas_call(
        _combine_kernel,
        grid=(m // tc,),
        in_specs=[ctile(0), ctile(1),
                  pl.BlockSpec((tc, D_MODEL), lambda i: (i + off, 0)),
                  pl.BlockSpec((tc, ROUTER_LANES), lambda i: (i + off, 0)),
                  _const_spec((1, D_MODEL))],
        out_specs=pl.BlockSpec((tc, D_MODEL), lambda i: (i, 0)),
        out_shape=jax.ShapeDtypeStruct((m, D_MODEL), F32),
        compiler_params=pltpu.CompilerParams(dimension_semantics=("arbitrary",),
                                             vmem_limit_bytes=V7X_VMEM_LIMIT_BYTES),
        name="moe_combine_prompt" if row_off == 0 else "moe_combine_sample",
    )(contrib, contrib, x2, route, gf)


def _t5_bucket(dist):
    n = jnp.maximum(dist, 0)
    max_exact = N_BUCKETS // 2
    nf = jnp.maximum(n, 1).astype(F32)
    large = max_exact + (jnp.log(nf / max_exact) / math.log(MAX_DISTANCE / max_exact)
                         * (N_BUCKETS - max_exact)).astype(jnp.int32)
    large = jnp.minimum(large, N_BUCKETS - 1)
    return jnp.where(n < max_exact, n, large)


def _bucket_bias(rel_bias, dist, valid):
    buckets = _t5_bucket(dist).reshape(1, -1)
    onehot = (buckets == jnp.arange(N_BUCKETS, dtype=jnp.int32)[:, None]).astype(F32)
    bias = jnp.dot(rel_bias.astype(F32).T, onehot, precision=lax.Precision.HIGHEST)
    return jnp.where(valid.reshape(1, -1), bias, NEG_BIG).reshape((rel_bias.shape[1],) + dist.shape)


def _prompt_bias_table(rel_bias):
    qi = jnp.arange(ATTN_BLOCK, dtype=jnp.int32)[:, None]
    kj = jnp.arange(2 * ATTN_BLOCK, dtype=jnp.int32)[None, :] - ATTN_BLOCK
    dist = qi - kj
    return _bucket_bias(rel_bias, dist, (dist >= 0) & (dist <= WINDOW))


def _sample_bias_table(rel_bias, w_buf):
    dist = w_buf - jnp.arange(w_buf + 1, dtype=jnp.int32)
    return _bucket_bias(rel_bias, dist, dist <= WINDOW)


def kernel(x_prompt, x_sample, cache_conv, cache_k, cache_v, norm1_g, w_in, conv_w, w_conv_out, w_attn_out, w_o, sinks, rel_bias, norm2_g, w_router_group, b_router_group, w_router_expert, b_router_expert, w_e_gate, w_e_up, w_e_down, norm_f_g):
    assert norm1_g.shape[0] == 1, "single-layer configuration"
    batch, seq, _ = x_prompt.shape
    nseq = x_sample.shape[0]
    w_buf = cache_k.shape[2]
    mp = batch * seq
    m_total = mp + nseq
    assert seq % TM_DENSE == 0 and seq % TM_IN_PROJ == 0 and seq % ATTN_BLOCK == 0 and mp % COMBINE_BLOCK == 0
    assert nseq % SAMPLE_SEQ_PER_STEP == 0 and mp % nseq == 0
    assert TOP_K == 2 and MOE_BLOCK == 1 << MOE_BLOCK_LOG2 and m_total * TOP_K < 1 << ASSIGN_BITS

    g1 = norm1_g[0][None, :]
    g2 = norm2_g[0][None, :]
    gf = norm_f_g[None, :]
    wi = w_in[0].astype(BF16)
    cw = conv_w[0]
    wc = w_conv_out[0].astype(BF16)
    wa = w_attn_out[0].astype(BF16)
    wo = w_o[0].astype(BF16)
    pad_cols = ROUTER_LANES - N_EXPERT_GROUPS - N_EXPERTS
    wr = jnp.concatenate([w_router_group[0], w_router_expert[0],
                          jnp.zeros((D_MODEL, pad_cols), F32)], axis=1).astype(BF16)
    br = jnp.concatenate([b_router_group[0], b_router_expert[0], jnp.zeros((pad_cols,), F32)])[None, :]
    sink = sinks[0].astype(F32)

    xp = x_prompt.reshape(mp, D_MODEL)
    bps = seq // TM_IN_PROJ
    yc_p, q_p, k_p, v_p, sa_p, sb_p, ut_p, kvt_p = _in_proj(
        xp, g1, wi, cw, tm=TM_IN_PROJ, blocks_per_seq=bps, u_tail=8, kv_tail=WINDOW, parts=IN_PROJ_PARTS)
    o_p = _attn_prompt(q_p, k_p, v_p, _prompt_bias_table(rel_bias), sink, batch, seq)

    pad_rows = lambda t: jnp.pad(t, ((0, TM_DENSE - nseq), (0, 0)))
    xs = pad_rows(x_sample.reshape(nseq, D_MODEL))
    hist = (pad_rows(cache_conv[0][:, 0, :]), pad_rows(cache_conv[0][:, 1, :]))
    yc_s, q_s, _, _, sa_s, sb_s, ut_s, kvt_s = _in_proj(
        xs, g1, wi, cw, tm=TM_DENSE, blocks_per_seq=1, u_tail=TM_DENSE, kv_tail=TM_DENSE, hist=hist,
        gate_dtype=F32)
    u_s = ut_s[0, :nseq]
    kv_s = kvt_s[0, :nseq]
    head_mask = (jnp.arange(KV_DIM)[None, :] // HEAD_DIM == jnp.arange(N_HEADS)[:, None] // GROUP)
    qbd = (jnp.tile(q_s[:nseq].reshape(nseq, N_HEADS, HEAD_DIM), (1, 1, N_KV_HEADS))
           * head_mask[None].astype(BF16))
    to_keys_minor = lambda c: jnp.transpose(c.reshape(nseq, w_buf, KV_DIM), (0, 2, 1))
    from_keys_minor = lambda c: jnp.transpose(c, (0, 2, 1)).reshape(1, nseq, w_buf, N_KV_HEADS, HEAD_DIM)
    o_s, kwin_s, vwin_s = _attn_sample(qbd, to_keys_minor(cache_k[0]), to_keys_minor(cache_v[0]), kv_s,
                                       _sample_bias_table(rel_bias, w_buf), sink[:, None],
                                       head_mask.astype(F32))
    o_s = pad_rows(o_s.reshape(nseq, Q_DIM))

    half = -(-m_total // COMBINE_BLOCK) * COMBINE_BLOCK
    assert half % nseq == 0 and TOP_K * half < 1 << ASSIGN_BITS
    x2, h2p, route, route_t, cnt = _out_proj((yc_p, o_p, sa_p, sb_p, xp), (yc_s, o_s, sa_s, sb_s, xs),
                                             wc, wa, wo, g2, wr, br, tm=TM_DENSE, valid_rows_b=nseq, half=half)

    n_assign = m_total * TOP_K
    keys = route_t[0:TOP_K, :m_total].reshape(-1)
    counts = cnt[0, N_EXPERT_GROUPS:N_EXPERT_GROUPS + N_EXPERTS].astype(jnp.int32)
    order = jnp.pad(jnp.sort(keys) & ((1 << ASSIGN_BITS) - 1), (0, MOE_BLOCK))
    n_blocks = -(-n_assign // MOE_BLOCK) + N_EXPERTS
    n_blocks += n_blocks % 2
    tables = _block_tables(counts, n_blocks)
    contrib = _moe_experts(order, tables, h2p, w_e_gate[0], w_e_up[0], w_e_down[0],
                           m_total=m_total, half=half)
    y_p = _moe_combine(contrib, x2, route, gf, row_off=0, m=mp, tc=COMBINE_BLOCK, half=half)
    y_s = _moe_combine(contrib, x2, route, gf, row_off=mp, m=nseq, tc=nseq, half=half)

    y_prompt = y_p.reshape(batch, seq, D_MODEL)
    y_sample = y_s.reshape(nseq, 1, D_MODEL)
    conv_state_prompt = ut_p.reshape(batch, bps, 8, D_CONV)[:, -1, 8 - (CONV_WIDTH - 1):, :][None]
    kv_last = kvt_p.reshape(batch, bps, WINDOW, 2 * KV_DIM)[:, -1]
    k_win_prompt = kv_last[:, :, :KV_DIM].reshape(batch, WINDOW, N_KV_HEADS, HEAD_DIM)[None]
    v_win_prompt = kv_last[:, :, KV_DIM:].reshape(batch, WINDOW, N_KV_HEADS, HEAD_DIM)[None]
    conv_state_sample = jnp.concatenate([cache_conv[0][:, 1:, :], u_s[:, None, :]], axis=1)[None]
    k_win_sample = from_keys_minor(kwin_s)
    v_win_sample = from_keys_minor(vwin_s)
    return (y_prompt, y_sample, conv_state_prompt, k_win_prompt, v_win_prompt,
            conv_state_sample, k_win_sample, v_win_sample)
```

```python
import functools
import math

import jax
import jax.numpy as jnp
from jax import lax
from jax.experimental import pallas as pl
from jax.experimental.pallas import tpu as pltpu

D_MODEL = 1024
D_CONV = 1024
CONV_WIDTH = 3
N_HEADS = 16
N_KV_HEADS = 4
HEAD_DIM = 64
GROUP = N_HEADS // N_KV_HEADS
WINDOW = 128
Q_DIM = N_HEADS * HEAD_DIM
KV_DIM = N_KV_HEADS * HEAD_DIM
N_BUCKETS = 32
MAX_DISTANCE = 128
N_EXPERT_GROUPS = 4
EXPERTS_PER_GROUP = 8
N_EXPERTS = N_EXPERT_GROUPS * EXPERTS_PER_GROUP
TOP_K = 2
D_EXPERT = 512
EPS = 1e-6

BF16 = jnp.bfloat16
F32 = jnp.float32
NEG_BIG = -1e30

V7X_VMEM_LIMIT_BYTES = 56 * 1024 * 1024
V7X_VMEM_LIMIT_LARGE_BYTES = 62 * 1024 * 1024
TILE_ROWS = 8
LANES = 128
LANES_LOG2 = 7
CARRY = TILE_ROWS
ROUTER_LANES = 128
TM_DENSE = 512
TM_IN_PROJ = 1024
OUT_PROJ_PARTS = 2
IN_PROJ_PARTS = 4
ATTN_BLOCK = 128
MOE_BLOCK = 256
MOE_BLOCK_LOG2 = 8
ASSIGN_BITS = 16
COMBINE_BLOCK = 256
SAMPLE_SEQ_PER_STEP = 16
HEADS_PER_STORE = LANES // HEAD_DIM


def _const_spec(shape):
    nd = len(shape)
    return pl.BlockSpec(shape, lambda *_: (0,) * nd, pipeline_mode=pl.Buffered(1))


def _rms_norm_f32(xf, g):
    return xf * lax.rsqrt(jnp.mean(xf * xf, axis=-1, keepdims=True) + EPS) * g


def _in_proj_kernel(*refs, tm, sample, blocks_per_seq, u_tail, kv_tail, parts):
    if sample:
        (x_ref, hist0_ref, hist1_ref, g_ref, w_ref,
         cw_ref, yc_ref, q_ref, k_ref, v_ref, sa_ref, sb_ref, ut_ref, kvt_ref) = refs
    else:
        (x_ref, g_ref, w_ref,
         cw_ref, yc_ref, q_ref, k_ref, v_ref, sa_ref, sb_ref, ut_ref, kvt_ref, ubuf_ref) = refs

    widths = (D_CONV, D_CONV, D_CONV, Q_DIM, 2 * KV_DIM, D_MODEL, D_MODEL)
    starts = [sum(widths[:n]) for n in range(len(widths))]
    wcb_ref, wcc_ref, wch_ref, wq_ref, wkv_ref, wga_ref, wgb_ref = [
        w_ref.at[:, pl.ds(a, n)] for a, n in zip(starts, widths)]
    w0 = cw_ref[0:1, :]
    w1 = cw_ref[1:2, :]
    w2 = cw_ref[2:3, :]

    if not sample:
        @pl.when(pl.program_id(0) % blocks_per_seq == 0)
        def _():
            ubuf_ref[0:CARRY, :] = jnp.zeros((CARRY, D_CONV), F32)

    n = tm // parts
    assert u_tail <= n and kv_tail <= n
    for r0 in range(0, tm, n):
        rows = pl.ds(r0, n)
        last = r0 + n == tm
        h = _rms_norm_f32(x_ref[rows, :], g_ref[...]).astype(BF16)

        def proj(part_ref, h=h):
            return jnp.dot(h, part_ref[...], preferred_element_type=F32)

        u = proj(wcc_ref) * proj(wch_ref)
        if sample:
            conv = w0 * hist0_ref[rows, :] + w1 * hist1_ref[rows, :] + w2 * u
        else:
            ubuf_ref[pl.ds(CARRY + r0, n), :] = u
            conv = (w0 * ubuf_ref[pl.ds(CARRY - 2 + r0, n), :] + w1 * ubuf_ref[pl.ds(CARRY - 1 + r0, n), :]
                    + w2 * u)
            if last:
                ubuf_ref[0:CARRY, :] = u[n - CARRY:, :]
        yc_ref[rows, :] = (proj(wcb_ref) * conv).astype(BF16)
        q_ref[rows, :] = (proj(wq_ref) * (HEAD_DIM ** -0.5)).astype(BF16)
        kv = proj(wkv_ref)
        k_ref[rows, :] = kv[:, :KV_DIM].astype(BF16)
        v_ref[rows, :] = kv[:, KV_DIM:].astype(BF16)
        if last:
            ut_ref[0] = u[n - u_tail:, :]
            kvt_ref[0] = kv[n - kv_tail:, :]
        sa_ref[rows, :] = jax.nn.sigmoid(proj(wga_ref)).astype(sa_ref.dtype)
        sb_ref[rows, :] = jax.nn.sigmoid(proj(wgb_ref)).astype(sb_ref.dtype)


def _in_proj(x, g1, w_in, conv_w, *, tm, blocks_per_seq, u_tail, kv_tail, hist=None, gate_dtype=BF16, parts=1):
    m = x.shape[0]
    nblk = m // tm
    sample = hist is not None
    row = lambda width: pl.BlockSpec((tm, width), lambda i: (i, 0))
    in_specs = [row(D_MODEL)]
    args = [x]
    if sample:
        in_specs += [row(D_CONV), row(D_CONV)]
        args += list(hist)
    in_specs += [_const_spec((1, D_MODEL)), _const_spec(w_in.shape), _const_spec(conv_w.shape)]
    args += [g1, w_in, conv_w]
    out_shape = [
        jax.ShapeDtypeStruct((m, D_CONV), BF16),
        jax.ShapeDtypeStruct((m, Q_DIM), BF16),
        jax.ShapeDtypeStruct((m, KV_DIM), BF16),
        jax.ShapeDtypeStruct((m, KV_DIM), BF16),
        jax.ShapeDtypeStruct((m, D_MODEL), gate_dtype),
        jax.ShapeDtypeStruct((m, D_MODEL), gate_dtype),
        jax.ShapeDtypeStruct((nblk, u_tail, D_CONV), F32),
        jax.ShapeDtypeStruct((nblk, kv_tail, 2 * KV_DIM), F32),
    ]
    out_specs = [row(D_CONV), row(Q_DIM), row(KV_DIM), row(KV_DIM), row(D_MODEL), row(D_MODEL),
                 pl.BlockSpec((1, u_tail, D_CONV), lambda i: (i, 0, 0)),
                 pl.BlockSpec((1, kv_tail, 2 * KV_DIM), lambda i: (i, 0, 0))]
    scratch = [] if sample else [pltpu.VMEM((tm + CARRY, D_CONV), F32)]
    return pl.pallas_call(
        functools.partial(_in_proj_kernel, tm=tm, sample=sample, blocks_per_seq=blocks_per_seq,
                          u_tail=u_tail, kv_tail=kv_tail, parts=parts),
        grid=(nblk,),
        in_specs=in_specs,
        out_specs=out_specs,
        out_shape=out_shape,
        scratch_shapes=scratch,
        compiler_params=pltpu.CompilerParams(dimension_semantics=("arbitrary",),
                                             vmem_limit_bytes=V7X_VMEM_LIMIT_LARGE_BYTES),
        name="in_proj_sample" if sample else "in_proj_prompt",
    )(*args)


def _attn_prompt_kernel(sink_ref, q_ref, kc_ref, kp_ref, vc_ref, vp_ref, bias_ref, o_ref):
    first = pl.program_id(1) == 0
    col = lax.broadcasted_iota(jnp.int32, (ATTN_BLOCK, 2 * ATTN_BLOCK), 1)
    no_prev = jnp.logical_and(first, col < ATTN_BLOCK)
    for g in range(N_KV_HEADS):
        ks = slice(g * HEAD_DIM, (g + 1) * HEAD_DIM)
        kcat = jnp.concatenate([kp_ref[:, ks], kc_ref[:, ks]], axis=0)
        vcat = jnp.concatenate([vp_ref[:, ks], vc_ref[:, ks]], axis=0)
        for h0 in range(g * GROUP, (g + 1) * GROUP, HEADS_PER_STORE):
            outs = []
            for h in range(h0, h0 + HEADS_PER_STORE):
                hs = slice(h * HEAD_DIM, (h + 1) * HEAD_DIM)
                s = lax.dot_general(q_ref[:, hs], kcat, (((1,), (1,)), ((), ())),
                                    preferred_element_type=F32)
                s = jnp.where(no_prev, NEG_BIG, s + bias_ref[h])
                sink = sink_ref[h]
                m = jnp.maximum(jnp.max(s, axis=-1, keepdims=True), sink)
                p = jnp.exp(s - m)
                denom = jnp.sum(p, axis=-1, keepdims=True) + jnp.exp(sink - m)
                o = jnp.dot(p.astype(BF16), vcat, preferred_element_type=F32)
                outs.append((o / denom).astype(BF16))
            o_ref[:, h0 * HEAD_DIM:(h0 + HEADS_PER_STORE) * HEAD_DIM] = jnp.concatenate(outs, axis=1)


def _attn_prompt(q, k, v, bias, sinks, batch, seq):
    nb = seq // ATTN_BLOCK
    cur = lambda b, i: (b * nb + i, 0)
    prev = lambda b, i: (b * nb + jnp.maximum(i - 1, 0), 0)
    return pl.pallas_call(
        _attn_prompt_kernel,
        grid=(batch, nb),
        in_specs=[pl.BlockSpec(memory_space=pltpu.SMEM),
                  pl.BlockSpec((ATTN_BLOCK, Q_DIM), cur),
                  pl.BlockSpec((ATTN_BLOCK, KV_DIM), cur),
                  pl.BlockSpec((ATTN_BLOCK, KV_DIM), prev),
                  pl.BlockSpec((ATTN_BLOCK, KV_DIM), cur),
                  pl.BlockSpec((ATTN_BLOCK, KV_DIM), prev),
                  _const_spec(bias.shape)],
        out_specs=pl.BlockSpec((ATTN_BLOCK, Q_DIM), cur),
        out_shape=jax.ShapeDtypeStruct((batch * seq, Q_DIM), BF16),
        compiler_params=pltpu.CompilerParams(dimension_semantics=("arbitrary", "arbitrary"),
                                             vmem_limit_bytes=V7X_VMEM_LIMIT_BYTES),
        name="attn_prompt",
    )(sinks, q, k, k, v, v, bias)


def _attn_sample_kernel(qbd_ref, ckt_ref, cvt_ref, kvn_ref, bias_ref, sink_ref, mask_ref,
                        o_ref, kwin_ref, vwin_ref, *, w_buf):
    bf16_round = lambda t: t.astype(BF16).astype(F32)
    seqs = range(SAMPLE_SEQ_PER_STEP)
    sink = sink_ref[...]
    newest = lax.broadcasted_iota(jnp.int32, (KV_DIM, w_buf), 1) == w_buf - 1
    kvn_t = kvn_ref[...].T
    for b in seqs:
        kwin_ref[b] = jnp.where(newest, kvn_t[:KV_DIM, b:b + 1], pltpu.roll(ckt_ref[b], w_buf - 1, axis=1))
        vwin_ref[b] = jnp.where(newest, kvn_t[KV_DIM:, b:b + 1], pltpu.roll(cvt_ref[b], w_buf - 1, axis=1))
    s = [jnp.dot(qbd_ref[b], ckt_ref[b].astype(BF16), preferred_element_type=F32) + bias_ref[:, :w_buf]
         for b in seqs]
    s_new = [jnp.sum(qbd_ref[b].astype(F32) * bf16_round(kvn_ref[b:b + 1, :KV_DIM]), axis=-1, keepdims=True)
             + bias_ref[:, w_buf:w_buf + 1] for b in seqs]
    m = [jnp.maximum(jnp.maximum(jnp.max(s[b], axis=-1, keepdims=True), s_new[b]), sink) for b in seqs]
    p = [jnp.exp(s[b] - m[b]) for b in seqs]
    p_new = [jnp.exp(s_new[b] - m[b]) for b in seqs]
    denom = [jnp.sum(p[b], axis=-1, keepdims=True) + p_new[b] + jnp.exp(sink - m[b]) for b in seqs]
    of = [lax.dot_general((p[b] / denom[b]).astype(BF16), cvt_ref[b].astype(BF16), (((1,), (1,)), ((), ())),
                          preferred_element_type=F32)
          + bf16_round(p_new[b] / denom[b]) * bf16_round(kvn_ref[b:b + 1, KV_DIM:]) for b in seqs]
    for b in seqs:
        ob = of[b] * mask_ref[...]
        o_ref[b] = (ob[:, 0:HEAD_DIM] + ob[:, HEAD_DIM:2 * HEAD_DIM]
                    + ob[:, 2 * HEAD_DIM:3 * HEAD_DIM] + ob[:, 3 * HEAD_DIM:]).astype(BF16)


def _attn_sample(qbd, ckt, cvt, kvn, bias, sink_col, head_mask):
    nseq, w_buf = ckt.shape[0], ckt.shape[2]
    sb = SAMPLE_SEQ_PER_STEP
    seq3 = lambda d1, d2: pl.BlockSpec((sb, d1, d2), lambda i: (i, 0, 0))
    win = jax.ShapeDtypeStruct((nseq, KV_DIM, w_buf), F32)
    return pl.pallas_call(
        functools.partial(_attn_sample_kernel, w_buf=w_buf),
        grid=(nseq // sb,),
        in_specs=[seq3(N_HEADS, KV_DIM), seq3(KV_DIM, w_buf), seq3(KV_DIM, w_buf),
                  pl.BlockSpec((sb, 2 * KV_DIM), lambda i: (i, 0)),
                  _const_spec(bias.shape), _const_spec(sink_col.shape), _const_spec(head_mask.shape)],
        out_specs=[seq3(N_HEADS, HEAD_DIM), seq3(KV_DIM, w_buf), seq3(KV_DIM, w_buf)],
        out_shape=[jax.ShapeDtypeStruct((nseq, N_HEADS, HEAD_DIM), BF16), win, win],
        compiler_params=pltpu.CompilerParams(dimension_semantics=("arbitrary",),
                                             vmem_limit_bytes=V7X_VMEM_LIMIT_BYTES),
        name="attn_sample",
    )(qbd, ckt, cvt, kvn, bias, sink_col, head_mask)


def _route_rows(logits, row0, valid_rows, half):
    tm = logits.shape[0]
    lane = lax.broadcasted_iota(jnp.int32, logits.shape, 1)
    lane_f = lane.astype(F32)
    no_lane = float(ROUTER_LANES)

    def top1(mask):
        best = jnp.max(jnp.where(mask, logits, -jnp.inf), axis=-1, keepdims=True)
        idx = jnp.min(jnp.where(jnp.logical_and(mask, logits == best), lane_f, no_lane), axis=-1, keepdims=True)
        return best, idx

    gmask = lane < N_EXPERT_GROUPS
    gmax, grp = top1(gmask)
    gsum = jnp.sum(jnp.where(gmask, jnp.exp(logits - gmax), 0.0), axis=-1, keepdims=True)
    p_grp = 1.0 / gsum
    lo = N_EXPERT_GROUPS + EXPERTS_PER_GROUP * grp
    emask = jnp.logical_and(lane_f >= lo, lane_f < lo + EXPERTS_PER_GROUP)
    v1, i1 = top1(emask)
    v2, i2 = top1(jnp.logical_and(emask, lane_f != i1))
    e21 = jnp.exp(v2 - v1)
    w1 = p_grp / (1.0 + e21)
    w2 = p_grp * e21 / (1.0 + e21)

    oh1 = lane_f == i1
    oh2 = lane_f == i2
    if valid_rows < tm:
        valid = lax.broadcasted_iota(jnp.int32, logits.shape, 0) < valid_rows
        oh1 = jnp.logical_and(oh1, valid)
        oh2 = jnp.logical_and(oh2, valid)
    oh = oh1.astype(F32) + oh2.astype(F32)
    token = row0 + lax.broadcasted_iota(jnp.int32, (tm, 1), 0)
    key1 = (i1.astype(jnp.int32) - N_EXPERT_GROUPS) * (1 << ASSIGN_BITS) + token
    key2 = (i2.astype(jnp.int32) - N_EXPERT_GROUPS) * (1 << ASSIGN_BITS) + token + half
    w1b = lax.bitcast_convert_type(w1, jnp.int32)
    w2b = lax.bitcast_convert_type(w2, jnp.int32)
    words = jnp.where(lane == 0, key1, jnp.where(lane == 1, key2, jnp.where(lane == 2, w1b,
                      jnp.where(lane == 3, w2b, 0))))
    return words, jnp.sum(oh, axis=0, keepdims=True)


def _store_token_tiles(ref, x):
    n = x.shape[0]
    for c in range(D_MODEL // LANES):
        ref[pl.ds(c, n, stride=TILE_ROWS), :] = x[:, c * LANES:(c + 1) * LANES]


def _load_token_tiles(ref, n):
    return jnp.concatenate([ref[pl.ds(c, n, stride=TILE_ROWS), :] for c in range(D_MODEL // LANES)], axis=1)


def _pack_bf16_pairs(x):
    hw = x.shape[1] // 2
    bits = lambda v: lax.bitcast_convert_type(v.astype(BF16).astype(F32), jnp.uint32)
    return (bits(x[:, hw:]) & jnp.uint32(0xFFFF0000)) | (bits(x[:, :hw]) >> 16)


def _unpack_bf16_pairs(w):
    lo = lax.bitcast_convert_type(w << 16, F32)
    hi = lax.bitcast_convert_type(w & jnp.uint32(0xFFFF0000), F32)
    return jnp.concatenate([lo, hi], axis=1).astype(BF16)


def _out_proj_rows(yc_ref, o_ref, sa_ref, sb_ref, x_ref, wc_ref, wa_ref, wo_ref, g2_ref, wr_ref, br_ref,
                   x2_ref, h2p_ref, route_ref, keys_ref, cnt_ref, *, valid_rows, half):
    tm = x_ref.shape[0]
    parts = [(r0, tm // OUT_PROJ_PARTS) for r0 in range(0, tm, tm // OUT_PROJ_PARTS)]
    for r0, n in parts:
        rows = pl.ds(r0, n)
        y_conv = jnp.dot(yc_ref[rows, :], wc_ref[...], preferred_element_type=F32)
        y_attn = jnp.dot(o_ref[rows, :], wa_ref[...], preferred_element_type=F32)
        mix = (sa_ref[rows, :].astype(F32) * y_conv + sb_ref[rows, :].astype(F32) * y_attn).astype(BF16)
        x2_ref[rows, :] = x_ref[rows, :] + jnp.dot(mix, wo_ref[...], preferred_element_type=F32)
    for r0, n in parts:
        rows = pl.ds(r0, n)
        h2 = _rms_norm_f32(x2_ref[rows, :], g2_ref[...])
        h2p_ref[rows, :] = _pack_bf16_pairs(h2)
        logits = jnp.dot(h2.astype(BF16), wr_ref[...], preferred_element_type=F32) + br_ref[...]
        words, cnt = _route_rows(logits, pl.program_id(0) * tm + r0, min(max(valid_rows - r0, 0), n), half)
        route_ref[rows, :] = words
        keys_ref[:, rows] = words.T[:TILE_ROWS, :]
        cnt_ref[...] += cnt


def _out_proj_kernel(*refs, n_first, valid_rows_second, half):
    first, second, shared = refs[0:5], refs[5:10], refs[10:]
    cnt_ref = shared[-1]
    tm = first[4].shape[0]

    @pl.when(pl.program_id(0) == 0)
    def _():
        cnt_ref[...] = jnp.zeros_like(cnt_ref)

    @pl.when(pl.program_id(0) < n_first)
    def _():
        _out_proj_rows(*first, *shared, valid_rows=tm, half=half)

    @pl.when(pl.program_id(0) >= n_first)
    def _():
        _out_proj_rows(*second, *shared, valid_rows=valid_rows_second, half=half)


def _out_proj(acts_a, acts_b, wc, wa, wo, g2, wr, br, *, tm, valid_rows_b, half):
    na = acts_a[4].shape[0] // tm
    nb = acts_b[4].shape[0] // tm
    assert nb == 1
    m_total = (na + nb) * tm
    spec_a = lambda width: pl.BlockSpec((tm, width), lambda i: (jnp.minimum(i, na - 1), 0))
    spec_b = lambda width: pl.BlockSpec((tm, width), lambda i: (jnp.maximum(i - na, 0), 0))
    widths = (D_CONV, Q_DIM, D_MODEL, D_MODEL, D_MODEL)
    in_specs = [spec_a(w) for w in widths] + [spec_b(w) for w in widths]
    in_specs += [_const_spec(wc.shape), _const_spec(wa.shape), _const_spec(wo.shape),
                 _const_spec(g2.shape), _const_spec(wr.shape), _const_spec(br.shape)]
    orow = lambda width: pl.BlockSpec((tm, width), lambda i: (i, 0))
    return pl.pallas_call(
        functools.partial(_out_proj_kernel, n_first=na, valid_rows_second=valid_rows_b, half=half),
        grid=(na + nb,),
        in_specs=in_specs,
        out_specs=[orow(D_MODEL), orow(D_MODEL // 2), orow(ROUTER_LANES),
                   pl.BlockSpec((TILE_ROWS, tm), lambda i: (0, i)),
                   pl.BlockSpec((1, ROUTER_LANES), lambda i: (0, 0))],
        out_shape=[jax.ShapeDtypeStruct((m_total, D_MODEL), F32),
                   jax.ShapeDtypeStruct((m_total, D_MODEL // 2), jnp.uint32),
                   jax.ShapeDtypeStruct((m_total, ROUTER_LANES), jnp.int32),
                   jax.ShapeDtypeStruct((TILE_ROWS, m_total), jnp.int32),
                   jax.ShapeDtypeStruct((1, ROUTER_LANES), F32)],
        compiler_params=pltpu.CompilerParams(dimension_semantics=("arbitrary",),
                                             vmem_limit_bytes=V7X_VMEM_LIMIT_BYTES),
        name="out_proj",
    )(*acts_a, *acts_b, wc, wa, wo, g2, wr, br)


def _block_table_kernel(counts_ref, bexp_ref, bpos_ref, bcnt_ref, bslot_ref, bnext_ref, nused_ref, first_ref,
                        *, n_blocks):
    def per_expert(e, carry):
        blk0, pos0, ordinal = carry
        cnt = counts_ref[e]
        nblk = lax.shift_right_logical(cnt + (MOE_BLOCK - 1), MOE_BLOCK_LOG2)
        first_ref[e] = jnp.where(nblk > 0, blk0, -1)

        def mark(b, c):
            off = (b - blk0) * MOE_BLOCK
            bexp_ref[b] = e
            bpos_ref[b] = pos0 + off
            bcnt_ref[b] = jnp.minimum(cnt - off, MOE_BLOCK)
            bslot_ref[b] = jnp.where(b == blk0, ordinal & 1, -1)
            bnext_ref[b] = -1
            return c
        lax.fori_loop(blk0, blk0 + nblk, mark, 0)
        return blk0 + nblk, pos0 + cnt, ordinal + jnp.where(nblk > 0, 1, 0)

    n_used, _, _ = lax.fori_loop(0, N_EXPERTS, per_expert, (0, 0, 0))
    nused_ref[0] = n_used

    def unused(b, c):
        bexp_ref[b] = N_EXPERTS - 1
        bpos_ref[b] = 0
        bcnt_ref[b] = 0
        bslot_ref[b] = -1
        bnext_ref[b] = -1
        return c
    lax.fori_loop(n_used, n_blocks, unused, 0)

    def link(k, nxt):
        e = N_EXPERTS - 1 - k
        fb = first_ref[e]

        @pl.when(fb >= 0)
        def _():
            bnext_ref[fb] = nxt
        return jnp.where(fb >= 0, e, nxt)
    lax.fori_loop(0, N_EXPERTS, link, -1)


def _block_tables(counts, n_blocks):
    smem = pl.BlockSpec(memory_space=pltpu.SMEM)
    blk = jax.ShapeDtypeStruct((n_blocks,), jnp.int32)
    return pl.pallas_call(
        functools.partial(_block_table_kernel, n_blocks=n_blocks),
        in_specs=[smem],
        out_specs=[smem] * 6,
        out_shape=[blk] * 5 + [jax.ShapeDtypeStruct((1,), jnp.int32)],
        scratch_shapes=[pltpu.SMEM((N_EXPERTS,), jnp.int32)],
        name="block_tables",
    )(counts)


def _moe_kernel(tok_ref, dst_ref, bexp_ref, bslot_ref, bnext_ref, nused_ref,
                h2p_ref, wg_hbm, wu_hbm, wd_hbm, contrib_hbm,
                xs_ref, ys_ref, wgf_ref, wuf_ref, wdf_ref, wgb_ref, wub_ref, wdb_ref, ssem_ref, wsem_ref,
                *, m_total, half, n_blocks):
    step = pl.program_id(0)
    n_used = nused_ref[0]
    tile = lambda t: pl.ds(pl.multiple_of(t * TILE_ROWS, TILE_ROWS), TILE_ROWS)

    def gather(blk, slt):
        for r in range(MOE_BLOCK):
            xs_ref[slt, pl.ds(r, 1), :] = h2p_ref[pl.ds(tok_ref[blk * MOE_BLOCK + r], 1), :]

    def scatter(blk, slt, r):
        return pltpu.make_async_copy(ys_ref.at[slt, tile(r), :], contrib_hbm.at[tile(dst_ref[blk * MOE_BLOCK + r]), :],
                                     ssem_ref.at[slt])

    def scatter_wait(slt, r):
        pltpu.make_async_copy(ys_ref.at[slt, tile(r), :], contrib_hbm.at[tile(0), :], ssem_ref.at[slt]).wait()

    @pl.when(step == 0)
    def _():
        ys_ref[...] = jnp.zeros_like(ys_ref)
        gap = half - m_total
        fills = [(2 * half + s * MOE_BLOCK, MOE_BLOCK) for s in range(2)]
        fills += [(k * half + m_total, gap) for k in range(TOP_K)] if gap else []
        for start, n in fills:
            fill = pltpu.make_async_copy(ys_ref.at[0, pl.ds(0, n * TILE_ROWS), :],
                                         contrib_hbm.at[pl.ds(start * TILE_ROWS, n * TILE_ROWS), :], ssem_ref.at[0])
            fill.start()
            fill.wait()

    def weight_copies(e, s):
        return [pltpu.make_async_copy(src.at[e], dst.at[s], wsem_ref.at[s])
                for src, dst in ((wg_hbm, wgf_ref), (wu_hbm, wuf_ref), (wd_hbm, wdf_ref))]

    @pl.when(step == 0)
    def _():
        for c in weight_copies(bexp_ref[0], 0):
            c.start()
        gather(0, 0)

    def wait_scatters(slt):
        for r in range(MOE_BLOCK):
            scatter_wait(slt, r)

    def expert_weights(i):
        wslot = bslot_ref[i]

        @pl.when(wslot >= 0)
        def _():
            for c in weight_copies(0, wslot):
                c.wait()

        @pl.when(jnp.logical_and(wslot >= 0, bnext_ref[i] >= 0))
        def _():
            for c in weight_copies(bnext_ref[i], 1 - wslot):
                c.start()

        @pl.when(wslot >= 0)
        def _():
            wgb_ref[...] = wgf_ref[wslot].astype(BF16)
            wub_ref[...] = wuf_ref[wslot].astype(BF16)
            wdb_ref[...] = wdf_ref[wslot].astype(BF16)

    def run_block(i, slt):
        gather(jnp.minimum(i + 1, n_blocks - 1), 1 - slt)
        xb = _unpack_bf16_pairs(xs_ref[slt])
        gate = jnp.dot(xb, wgb_ref[...], preferred_element_type=F32)
        up = jnp.dot(xb, wub_ref[...], preferred_element_type=F32)
        hmid = (jax.nn.silu(gate) * up).astype(BF16)
        _store_token_tiles(ys_ref.at[slt], jnp.dot(hmid, wdb_ref[...], preferred_element_type=F32))
        for r in range(MOE_BLOCK):
            scatter(i, slt, r).start(priority=r % 2)

    for slt in range(2):
        i = 2 * step + slt
        active = i < n_used
        last_active = i == n_used - 1
        pl.when(jnp.logical_and(active, i >= 2))(functools.partial(wait_scatters, slt))
        expert_weights(i)
        pl.when(active)(functools.partial(run_block, i, slt))
        pl.when(last_active)(functools.partial(wait_scatters, slt))
        pl.when(jnp.logical_and(last_active, i >= 1))(functools.partial(wait_scatters, 1 - slt))


def _row_table_kernel(bpos_ref, bcnt_ref, order_ref, tok_ref, dst_ref, *, half, n_blocks):
    rows_per_block = MOE_BLOCK // LANES
    lane = lax.broadcasted_iota(jnp.int32, (rows_per_block, LANES), 1)
    r = lax.broadcasted_iota(jnp.int32, (rows_per_block, LANES), 0) * LANES + lane

    def per_block(b, c):
        p = bpos_ref[b]
        sh = p & (LANES - 1)
        win = order_ref[pl.ds(lax.shift_right_logical(p, LANES_LOG2), rows_per_block + 1), :]
        rolled = pltpu.roll(win, (LANES - sh) & (LANES - 1), axis=1)
        assign = jnp.where(lane < LANES - sh, rolled[:rows_per_block], rolled[1:])
        tok_ref[pl.ds(b * rows_per_block, rows_per_block), :] = jnp.where(assign >= half, assign - half, assign)
        trash = 2 * half + (b & 1) * MOE_BLOCK + r
        dst_ref[pl.ds(b * rows_per_block, rows_per_block), :] = jnp.where(r < bcnt_ref[b], assign, trash)
        return c
    lax.fori_loop(0, n_blocks, per_block, 0, unroll=6)


def _row_tables(order, bpos, bcnt, half):
    n_blocks = bpos.shape[0]
    assert MOE_BLOCK % LANES == 0 and order.shape[0] % LANES == 0
    out = jax.ShapeDtypeStruct((n_blocks * MOE_BLOCK // LANES, LANES), jnp.int32)
    tok, dst = pl.pallas_call(
        functools.partial(_row_table_kernel, half=half, n_blocks=n_blocks),
        grid_spec=pltpu.PrefetchScalarGridSpec(
            num_scalar_prefetch=2, grid=(1,),
            in_specs=[pl.BlockSpec((order.shape[0] // LANES, LANES), lambda i, *_: (0, 0))],
            out_specs=[pl.BlockSpec(out.shape, lambda i, *_: (0, 0))] * 2),
        out_shape=[out, out],
        name="row_tables",
    )(bpos, bcnt, order.reshape(-1, LANES))
    return tok.reshape(-1), dst.reshape(-1)


def _moe_experts(order, tables, h2p, w_gate, w_up, w_down, *, m_total, half):
    bexp, bpos, bcnt, bslot, bnext, n_used = tables
    n_blocks = bexp.shape[0]
    assert 0 <= half - m_total <= MOE_BLOCK and n_blocks % 2 == 0
    tok, dst = _row_tables(order, bpos, bcnt, half)
    tables = (bexp, bslot, bnext, n_used)
    hbm = pl.BlockSpec(memory_space=pl.ANY)
    grid_spec = pltpu.PrefetchScalarGridSpec(
        num_scalar_prefetch=2 + len(tables),
        grid=(n_blocks // 2,),
        in_specs=[_const_spec(h2p.shape), hbm, hbm, hbm],
        out_specs=hbm,
        scratch_shapes=[pltpu.VMEM((2, MOE_BLOCK, D_MODEL // 2), jnp.uint32),
                        pltpu.VMEM((2, MOE_BLOCK * TILE_ROWS, LANES), F32),
                        pltpu.VMEM((2, D_MODEL, D_EXPERT), F32),
                        pltpu.VMEM((2, D_MODEL, D_EXPERT), F32),
                        pltpu.VMEM((2, D_EXPERT, D_MODEL), F32),
                        pltpu.VMEM((D_MODEL, D_EXPERT), BF16),
                        pltpu.VMEM((D_MODEL, D_EXPERT), BF16),
                        pltpu.VMEM((D_EXPERT, D_MODEL), BF16),
                        pltpu.SemaphoreType.DMA((2,)),
                        pltpu.SemaphoreType.DMA((2,))],
    )
    return pl.pallas_call(
        functools.partial(_moe_kernel, m_total=m_total, half=half, n_blocks=n_blocks),
        grid_spec=grid_spec,
        out_shape=jax.ShapeDtypeStruct(((2 * half + 2 * MOE_BLOCK) * TILE_ROWS, LANES), F32),
        compiler_params=pltpu.CompilerParams(dimension_semantics=("arbitrary",),
                                             vmem_limit_bytes=V7X_VMEM_LIMIT_LARGE_BYTES),
        name="moe_experts",
    )(tok, dst, *tables, h2p, w_gate, w_up, w_down)


def _combine_kernel(c0_ref, c1_ref, x2_ref, route_ref, gf_ref, y_ref):
    tc = x2_ref.shape[0]
    w = lax.bitcast_convert_type(route_ref[:, TOP_K:2 * TOP_K], F32)
    moe = w[:, 0:1] * _load_token_tiles(c0_ref, tc) + w[:, 1:2] * _load_token_tiles(c1_ref, tc)
    y_ref[...] = _rms_norm_f32(x2_ref[...] + moe, gf_ref[...])


def _moe_combine(contrib, x2, route, gf, *, row_off, m, tc, half):
    off = row_off // tc
    assert row_off % tc == 0 and half % tc == 0
    ctile = lambda k: pl.BlockSpec((tc * TILE_ROWS, LANES), lambda i: (i + off + k * (half // tc), 0))
    return pl.pallas_call(
        _combine_kernel,
        grid=(m // tc,),
        in_specs=[ctile(0), ctile(1),
                  pl.BlockSpec((tc, D_MODEL), lambda i: (i + off, 0)),
                  pl.BlockSpec((tc, ROUTER_LANES), lambda i: (i + off, 0)),
                  _const_spec((1, D_MODEL))],
        out_specs=pl.BlockSpec((tc, D_MODEL), lambda i: (i, 0)),
        out_shape=jax.ShapeDtypeStruct((m, D_MODEL), F32),
        compiler_params=pltpu.CompilerParams(dimension_semantics=("arbitrary",),
                                             vmem_limit_bytes=V7X_VMEM_LIMIT_BYTES),
        name="moe_combine_prompt" if row_off == 0 else "moe_combine_sample",
    )(contrib, contrib, x2, route, gf)


def _t5_bucket(dist):
    n = jnp.maximum(dist, 0)
    max_exact = N_BUCKETS // 2
    nf = jnp.maximum(n, 1).astype(F32)
    large = max_exact + (jnp.log(nf / max_exact) / math.log(MAX_DISTANCE / max_exact)
                         * (N_BUCKETS - max_exact)).astype(jnp.int32)
    large = jnp.minimum(large, N_BUCKETS - 1)
    return jnp.where(n < max_exact, n, large)


def _bucket_bias(rel_bias, dist, valid):
    buckets = _t5_bucket(dist).reshape(1, -1)
    onehot = (buckets == jnp.arange(N_BUCKETS, dtype=jnp.int32)[:, None]).astype(F32)
    bias = jnp.dot(rel_bias.astype(F32).T, onehot, precision=lax.Precision.HIGHEST)
    return jnp.where(valid.reshape(1, -1), bias, NEG_BIG).reshape((rel_bias.shape[1],) + dist.shape)


def _prompt_bias_table(rel_bias):
    qi = jnp.arange(ATTN_BLOCK, dtype=jnp.int32)[:, None]
    kj = jnp.arange(2 * ATTN_BLOCK, dtype=jnp.int32)[None, :] - ATTN_BLOCK
    dist = qi - kj
    return _bucket_bias(rel_bias, dist, (dist >= 0) & (dist <= WINDOW))


def _sample_bias_table(rel_bias, w_buf):
    dist = w_buf - jnp.arange(w_buf + 1, dtype=jnp.int32)
    return _bucket_bias(rel_bias, dist, dist <= WINDOW)


def kernel(x_prompt, x_sample, cache_conv, cache_k, cache_v, norm1_g, w_in, conv_w, w_conv_out, w_attn_out, w_o, sinks, rel_bias, norm2_g, w_router_group, b_router_group, w_router_expert, b_router_expert, w_e_gate, w_e_up, w_e_down, norm_f_g):
    assert norm1_g.shape[0] == 1, "single-layer configuration"
    batch, seq, _ = x_prompt.shape
    nseq = x_sample.shape[0]
    w_buf = cache_k.shape[2]
    mp = batch * seq
    m_total = mp + nseq
    assert seq % TM_DENSE == 0 and seq % TM_IN_PROJ == 0 and seq % ATTN_BLOCK == 0 and mp % COMBINE_BLOCK == 0
    assert nseq % SAMPLE_SEQ_PER_STEP == 0 and mp % nseq == 0
    assert TOP_K == 2 and MOE_BLOCK == 1 << MOE_BLOCK_LOG2 and m_total * TOP_K < 1 << ASSIGN_BITS
    assert CONV_WIDTH == 3 and LANES == 1 << LANES_LOG2 and x_sample.shape[1] == 1 and w_buf == WINDOW

    g1 = norm1_g[0][None, :]
    g2 = norm2_g[0][None, :]
    gf = norm_f_g[None, :]
    wi = w_in[0].astype(BF16)
    cw = conv_w[0]
    wc = w_conv_out[0].astype(BF16)
    wa = w_attn_out[0].astype(BF16)
    wo = w_o[0].astype(BF16)
    pad_cols = ROUTER_LANES - N_EXPERT_GROUPS - N_EXPERTS
    wr = jnp.concatenate([w_router_group[0], w_router_expert[0],
                          jnp.zeros((D_MODEL, pad_cols), F32)], axis=1).astype(BF16)
    br = jnp.concatenate([b_router_group[0], b_router_expert[0], jnp.zeros((pad_cols,), F32)])[None, :]
    sink = sinks[0].astype(F32)

    xp = x_prompt.reshape(mp, D_MODEL)
    bps = seq // TM_IN_PROJ
    yc_p, q_p, k_p, v_p, sa_p, sb_p, ut_p, kvt_p = _in_proj(
        xp, g1, wi, cw, tm=TM_IN_PROJ, blocks_per_seq=bps, u_tail=CARRY, kv_tail=WINDOW, parts=IN_PROJ_PARTS)
    o_p = _attn_prompt(q_p, k_p, v_p, _prompt_bias_table(rel_bias), sink, batch, seq)

    pad_rows = lambda t: jnp.pad(t, ((0, TM_DENSE - nseq), (0, 0)))
    xs = pad_rows(x_sample.reshape(nseq, D_MODEL))
    hist = (pad_rows(cache_conv[0][:, 0, :]), pad_rows(cache_conv[0][:, 1, :]))
    yc_s, q_s, _, _, sa_s, sb_s, ut_s, kvt_s = _in_proj(
        xs, g1, wi, cw, tm=TM_DENSE, blocks_per_seq=1, u_tail=TM_DENSE, kv_tail=TM_DENSE, hist=hist,
        gate_dtype=F32)
    u_s = ut_s[0, :nseq]
    kv_s = kvt_s[0, :nseq]
    head_mask = (jnp.arange(KV_DIM)[None, :] // HEAD_DIM == jnp.arange(N_HEADS)[:, None] // GROUP)
    qbd = (jnp.tile(q_s[:nseq].reshape(nseq, N_HEADS, HEAD_DIM), (1, 1, N_KV_HEADS))
           * head_mask[None].astype(BF16))
    to_keys_minor = lambda c: jnp.transpose(c.reshape(nseq, w_buf, KV_DIM), (0, 2, 1))
    from_keys_minor = lambda c: jnp.transpose(c, (0, 2, 1)).reshape(1, nseq, w_buf, N_KV_HEADS, HEAD_DIM)
    o_s, kwin_s, vwin_s = _attn_sample(qbd, to_keys_minor(cache_k[0]), to_keys_minor(cache_v[0]), kv_s,
                                       _sample_bias_table(rel_bias, w_buf), sink[:, None],
                                       head_mask.astype(F32))
    o_s = pad_rows(o_s.reshape(nseq, Q_DIM))

    half = -(-m_total // COMBINE_BLOCK) * COMBINE_BLOCK
    assert half % nseq == 0 and TOP_K * half < 1 << ASSIGN_BITS
    x2, h2p, route, route_t, cnt = _out_proj((yc_p, o_p, sa_p, sb_p, xp), (yc_s, o_s, sa_s, sb_s, xs),
                                             wc, wa, wo, g2, wr, br, tm=TM_DENSE, valid_rows_b=nseq, half=half)

    n_assign = m_total * TOP_K
    keys = route_t[0:TOP_K, :m_total].reshape(-1)
    counts = cnt[0, N_EXPERT_GROUPS:N_EXPERT_GROUPS + N_EXPERTS].astype(jnp.int32)
    order = jnp.pad(jnp.sort(keys) & ((1 << ASSIGN_BITS) - 1), (0, MOE_BLOCK))
    n_blocks = -(-n_assign // MOE_BLOCK) + N_EXPERTS
    n_blocks += n_blocks % 2
    tables = _block_tables(counts, n_blocks)
    contrib = _moe_experts(order, tables, h2p, w_e_gate[0], w_e_up[0], w_e_down[0],
                           m_total=m_total, half=half)
    y_p = _moe_combine(contrib, x2, route, gf, row_off=0, m=mp, tc=COMBINE_BLOCK, half=half)
    y_s = _moe_combine(contrib, x2, route, gf, row_off=mp, m=nseq, tc=nseq, half=half)

    y_prompt = y_p.reshape(batch, seq, D_MODEL)
    y_sample = y_s.reshape(nseq, 1, D_MODEL)
    conv_state_prompt = ut_p.reshape(batch, bps, CARRY, D_CONV)[:, -1, CARRY - (CONV_WIDTH - 1):, :][None]
    kv_last = kvt_p.reshape(batch, bps, WINDOW, 2 * KV_DIM)[:, -1]
    k_win_prompt = kv_last[:, :, :KV_DIM].reshape(batch, WINDOW, N_KV_HEADS, HEAD_DIM)[None]
    v_win_prompt = kv_last[:, :, KV_DIM:].reshape(batch, WINDOW, N_KV_HEADS, HEAD_DIM)[None]
    conv_state_sample = jnp.concatenate([cache_conv[0][:, 1:, :], u_s[:, None, :]], axis=1)[None]
    k_win_sample = from_keys_minor(kwin_s)
    v_win_sample = from_keys_minor(vwin_s)
    return (y_prompt, y_sample, conv_state_prompt, k_win_prompt, v_win_prompt,
            conv_state_sample, k_win_sample, v_win_sample)
```

```python
import functools
import math

import jax
import jax.numpy as jnp
from jax import lax
from jax.experimental import pallas as pl
from jax.experimental.pallas import tpu as pltpu

D_MODEL = 1024
D_CONV = 1024
CONV_WIDTH = 3
N_HEADS = 16
N_KV_HEADS = 4
HEAD_DIM = 64
GROUP = N_HEADS // N_KV_HEADS
WINDOW = 128
Q_DIM = N_HEADS * HEAD_DIM
KV_DIM = N_KV_HEADS * HEAD_DIM
N_BUCKETS = 32
MAX_DISTANCE = 128
N_EXPERT_GROUPS = 4
EXPERTS_PER_GROUP = 8
N_EXPERTS = N_EXPERT_GROUPS * EXPERTS_PER_GROUP
TOP_K = 2
D_EXPERT = 512
EPS = 1e-6

BF16 = jnp.bfloat16
F32 = jnp.float32
NEG_BIG = -1e30

V7X_VMEM_LIMIT_BYTES = 56 * 1024 * 1024
V7X_VMEM_LIMIT_LARGE_BYTES = 62 * 1024 * 1024
TILE_ROWS = 8
LANES = 128
LANES_LOG2 = 7
CARRY = TILE_ROWS
ROUTER_LANES = 128
TM_DENSE = 512
TM_IN_PROJ = 1024
OUT_PROJ_PARTS = 2
IN_PROJ_PARTS = 4
ATTN_BLOCK = 128
ATTN_Q_BLOCKS = 2
MOE_BLOCK = 256
MOE_BLOCK_LOG2 = 8
ASSIGN_BITS = 16
COMBINE_BLOCK = 256
SAMPLE_SEQ_PER_STEP = 16
HEADS_PER_STORE = LANES // HEAD_DIM


def _const_spec(shape):
    nd = len(shape)
    return pl.BlockSpec(shape, lambda *_: (0,) * nd, pipeline_mode=pl.Buffered(1))


def _rms_norm_f32(xf, g):
    return xf * lax.rsqrt(jnp.mean(xf * xf, axis=-1, keepdims=True) + EPS) * g


def _in_proj_kernel(*refs, tm, sample, blocks_per_seq, u_tail, kv_tail, parts):
    if sample:
        (x_ref, hist0_ref, hist1_ref, g_ref, w_ref,
         cw_ref, yc_ref, q_ref, k_ref, v_ref, sa_ref, sb_ref, ut_ref, kvt_ref) = refs
    else:
        (x_ref, g_ref, w_ref,
         cw_ref, yc_ref, q_ref, k_ref, v_ref, sa_ref, sb_ref, ut_ref, kvt_ref, ubuf_ref) = refs

    widths = (D_CONV, D_CONV, D_CONV, Q_DIM, 2 * KV_DIM, D_MODEL, D_MODEL)
    starts = [sum(widths[:n]) for n in range(len(widths))]
    wcb_ref, wcc_ref, wch_ref, wq_ref, wkv_ref, wga_ref, wgb_ref = [
        w_ref.at[:, pl.ds(a, n)] for a, n in zip(starts, widths)]
    w0 = cw_ref[0:1, :]
    w1 = cw_ref[1:2, :]
    w2 = cw_ref[2:3, :]

    if not sample:
        @pl.when(pl.program_id(0) % blocks_per_seq == 0)
        def _():
            ubuf_ref[0:CARRY, :] = jnp.zeros((CARRY, D_CONV), F32)

    n = tm // parts
    assert u_tail <= n and kv_tail <= n
    for r0 in range(0, tm, n):
        rows = pl.ds(r0, n)
        last = r0 + n == tm
        h = _rms_norm_f32(x_ref[rows, :], g_ref[...]).astype(BF16)

        def proj(part_ref, h=h):
            return jnp.dot(h, part_ref[...], preferred_element_type=F32)

        u = proj(wcc_ref) * proj(wch_ref)
        if sample:
            conv = w0 * hist0_ref[rows, :] + w1 * hist1_ref[rows, :] + w2 * u
        else:
            ubuf_ref[pl.ds(CARRY + r0, n), :] = u
            conv = (w0 * ubuf_ref[pl.ds(CARRY - 2 + r0, n), :] + w1 * ubuf_ref[pl.ds(CARRY - 1 + r0, n), :]
                    + w2 * u)
            if last:
                ubuf_ref[0:CARRY, :] = u[n - CARRY:, :]
        yc_ref[rows, :] = (proj(wcb_ref) * conv).astype(BF16)
        q_ref[rows, :] = (proj(wq_ref) * (HEAD_DIM ** -0.5)).astype(BF16)
        kv = proj(wkv_ref)
        k_ref[rows, :] = kv[:, :KV_DIM].astype(BF16)
        v_ref[rows, :] = kv[:, KV_DIM:].astype(BF16)
        if last:
            ut_ref[0] = u[n - u_tail:, :]
            kvt_ref[0] = kv[n - kv_tail:, :]
        sa_ref[rows, :] = jax.nn.sigmoid(proj(wga_ref)).astype(sa_ref.dtype)
        sb_ref[rows, :] = jax.nn.sigmoid(proj(wgb_ref)).astype(sb_ref.dtype)


def _in_proj(x, g1, w_in, conv_w, *, tm, blocks_per_seq, u_tail, kv_tail, hist=None, gate_dtype=BF16, parts=1):
    m = x.shape[0]
    nblk = m // tm
    sample = hist is not None
    row = lambda width: pl.BlockSpec((tm, width), lambda i: (i, 0))
    in_specs = [row(D_MODEL)]
    args = [x]
    if sample:
        in_specs += [row(D_CONV), row(D_CONV)]
        args += list(hist)
    in_specs += [_const_spec((1, D_MODEL)), _const_spec(w_in.shape), _const_spec(conv_w.shape)]
    args += [g1, w_in, conv_w]
    out_shape = [
        jax.ShapeDtypeStruct((m, D_CONV), BF16),
        jax.ShapeDtypeStruct((m, Q_DIM), BF16),
        jax.ShapeDtypeStruct((m, KV_DIM), BF16),
        jax.ShapeDtypeStruct((m, KV_DIM), BF16),
        jax.ShapeDtypeStruct((m, D_MODEL), gate_dtype),
        jax.ShapeDtypeStruct((m, D_MODEL), gate_dtype),
        jax.ShapeDtypeStruct((nblk, u_tail, D_CONV), F32),
        jax.ShapeDtypeStruct((nblk, kv_tail, 2 * KV_DIM), F32),
    ]
    out_specs = [row(D_CONV), row(Q_DIM), row(KV_DIM), row(KV_DIM), row(D_MODEL), row(D_MODEL),
                 pl.BlockSpec((1, u_tail, D_CONV), lambda i: (i, 0, 0)),
                 pl.BlockSpec((1, kv_tail, 2 * KV_DIM), lambda i: (i, 0, 0))]
    scratch = [] if sample else [pltpu.VMEM((tm + CARRY, D_CONV), F32)]
    return pl.pallas_call(
        functools.partial(_in_proj_kernel, tm=tm, sample=sample, blocks_per_seq=blocks_per_seq,
                          u_tail=u_tail, kv_tail=kv_tail, parts=parts),
        grid=(nblk,),
        in_specs=in_specs,
        out_specs=out_specs,
        out_shape=out_shape,
        scratch_shapes=scratch,
        compiler_params=pltpu.CompilerParams(dimension_semantics=("arbitrary",),
                                             vmem_limit_bytes=V7X_VMEM_LIMIT_LARGE_BYTES),
        name="in_proj_sample" if sample else "in_proj_prompt",
    )(*args)


def _attn_prompt_kernel(sink_ref, q_ref, kc_ref, kp_ref, vc_ref, vp_ref, bias_ref, o_ref):
    col = lax.broadcasted_iota(jnp.int32, (ATTN_BLOCK, 2 * ATTN_BLOCK), 1)
    no_prev = jnp.logical_and(pl.program_id(1) == 0, col < ATTN_BLOCK)
    for j in range(ATTN_Q_BLOCKS):
        rows = pl.ds(j * ATTN_BLOCK, ATTN_BLOCK)
        before = pl.ds((j - 1) * ATTN_BLOCK, ATTN_BLOCK)
        for g in range(N_KV_HEADS):
            ks = slice(g * HEAD_DIM, (g + 1) * HEAD_DIM)
            k_before = kp_ref[:, ks] if j == 0 else kc_ref[before, ks]
            v_before = vp_ref[:, ks] if j == 0 else vc_ref[before, ks]
            kcat = jnp.concatenate([k_before, kc_ref[rows, ks]], axis=0)
            vcat = jnp.concatenate([v_before, vc_ref[rows, ks]], axis=0)
            for h0 in range(g * GROUP, (g + 1) * GROUP, HEADS_PER_STORE):
                outs = []
                for h in range(h0, h0 + HEADS_PER_STORE):
                    hs = slice(h * HEAD_DIM, (h + 1) * HEAD_DIM)
                    s = lax.dot_general(q_ref[rows, hs], kcat, (((1,), (1,)), ((), ())),
                                        preferred_element_type=F32)
                    s = s + bias_ref[h]
                    if j == 0:
                        s = jnp.where(no_prev, NEG_BIG, s)
                    sink = sink_ref[h]
                    m = jnp.maximum(jnp.max(s, axis=-1, keepdims=True), sink)
                    p = jnp.exp(s - m)
                    denom = jnp.sum(p, axis=-1, keepdims=True) + jnp.exp(sink - m)
                    o = jnp.dot(p.astype(BF16), vcat, preferred_element_type=F32)
                    outs.append((o / denom).astype(BF16))
                o_ref[rows, h0 * HEAD_DIM:(h0 + HEADS_PER_STORE) * HEAD_DIM] = jnp.concatenate(outs, axis=1)


def _attn_prompt(q, k, v, bias, sinks, batch, seq):
    tq = ATTN_Q_BLOCKS * ATTN_BLOCK
    nb = seq // tq
    cur = lambda b, i: (b * nb + i, 0)
    prev = lambda b, i: ((b * nb + i) * ATTN_Q_BLOCKS - jnp.minimum(i, 1), 0)
    return pl.pallas_call(
        _attn_prompt_kernel,
        grid=(batch, nb),
        in_specs=[pl.BlockSpec(memory_space=pltpu.SMEM),
                  pl.BlockSpec((tq, Q_DIM), cur),
                  pl.BlockSpec((tq, KV_DIM), cur),
                  pl.BlockSpec((ATTN_BLOCK, KV_DIM), prev),
                  pl.BlockSpec((tq, KV_DIM), cur),
                  pl.BlockSpec((ATTN_BLOCK, KV_DIM), prev),
                  _const_spec(bias.shape)],
        out_specs=pl.BlockSpec((tq, Q_DIM), cur),
        out_shape=jax.ShapeDtypeStruct((batch * seq, Q_DIM), BF16),
        compiler_params=pltpu.CompilerParams(dimension_semantics=("arbitrary", "arbitrary"),
                                             vmem_limit_bytes=V7X_VMEM_LIMIT_BYTES),
        name="attn_prompt",
    )(sinks, q, k, k, v, v, bias)


def _attn_sample_kernel(qbd_ref, ckt_ref, cvt_ref, kvn_ref, bias_ref, sink_ref, mask_ref,
                        o_ref, kwin_ref, vwin_ref, *, w_buf):
    bf16_round = lambda t: t.astype(BF16).astype(F32)
    seqs = range(SAMPLE_SEQ_PER_STEP)
    sink = sink_ref[...]
    newest = lax.broadcasted_iota(jnp.int32, (KV_DIM, w_buf), 1) == w_buf - 1
    kvn_t = kvn_ref[...].T
    for b in seqs:
        kwin_ref[b] = jnp.where(newest, kvn_t[:KV_DIM, b:b + 1], pltpu.roll(ckt_ref[b], w_buf - 1, axis=1))
        vwin_ref[b] = jnp.where(newest, kvn_t[KV_DIM:, b:b + 1], pltpu.roll(cvt_ref[b], w_buf - 1, axis=1))
    s = [jnp.dot(qbd_ref[b], ckt_ref[b].astype(BF16), preferred_element_type=F32) + bias_ref[:, :w_buf]
         for b in seqs]
    s_new = [jnp.sum(qbd_ref[b].astype(F32) * bf16_round(kvn_ref[b:b + 1, :KV_DIM]), axis=-1, keepdims=True)
             + bias_ref[:, w_buf:w_buf + 1] for b in seqs]
    m = [jnp.maximum(jnp.maximum(jnp.max(s[b], axis=-1, keepdims=True), s_new[b]), sink) for b in seqs]
    p = [jnp.exp(s[b] - m[b]) for b in seqs]
    p_new = [jnp.exp(s_new[b] - m[b]) for b in seqs]
    denom = [jnp.sum(p[b], axis=-1, keepdims=True) + p_new[b] + jnp.exp(sink - m[b]) for b in seqs]
    of = [lax.dot_general((p[b] / denom[b]).astype(BF16), cvt_ref[b].astype(BF16), (((1,), (1,)), ((), ())),
                          preferred_element_type=F32)
          + bf16_round(p_new[b] / denom[b]) * bf16_round(kvn_ref[b:b + 1, KV_DIM:]) for b in seqs]
    for b in seqs:
        ob = of[b] * mask_ref[...]
        o_ref[b] = (ob[:, 0:HEAD_DIM] + ob[:, HEAD_DIM:2 * HEAD_DIM]
                    + ob[:, 2 * HEAD_DIM:3 * HEAD_DIM] + ob[:, 3 * HEAD_DIM:]).astype(BF16)


def _attn_sample(qbd, ckt, cvt, kvn, bias, sink_col, head_mask):
    nseq, w_buf = ckt.shape[0], ckt.shape[2]
    sb = SAMPLE_SEQ_PER_STEP
    seq3 = lambda d1, d2: pl.BlockSpec((sb, d1, d2), lambda i: (i, 0, 0))
    win = jax.ShapeDtypeStruct((nseq, KV_DIM, w_buf), F32)
    return pl.pallas_call(
        functools.partial(_attn_sample_kernel, w_buf=w_buf),
        grid=(nseq // sb,),
        in_specs=[seq3(N_HEADS, KV_DIM), seq3(KV_DIM, w_buf), seq3(KV_DIM, w_buf),
                  pl.BlockSpec((sb, 2 * KV_DIM), lambda i: (i, 0)),
                  _const_spec(bias.shape), _const_spec(sink_col.shape), _const_spec(head_mask.shape)],
        out_specs=[seq3(N_HEADS, HEAD_DIM), seq3(KV_DIM, w_buf), seq3(KV_DIM, w_buf)],
        out_shape=[jax.ShapeDtypeStruct((nseq, N_HEADS, HEAD_DIM), BF16), win, win],
        compiler_params=pltpu.CompilerParams(dimension_semantics=("arbitrary",),
                                             vmem_limit_bytes=V7X_VMEM_LIMIT_BYTES),
        name="attn_sample",
    )(qbd, ckt, cvt, kvn, bias, sink_col, head_mask)


def _route_rows(logits, row0, valid_rows, half):
    tm = logits.shape[0]
    lane = lax.broadcasted_iota(jnp.int32, logits.shape, 1)
    lane_f = lane.astype(F32)
    no_lane = float(ROUTER_LANES)

    def top1(mask):
        best = jnp.max(jnp.where(mask, logits, -jnp.inf), axis=-1, keepdims=True)
        idx = jnp.min(jnp.where(jnp.logical_and(mask, logits == best), lane_f, no_lane), axis=-1, keepdims=True)
        return best, idx

    gmask = lane < N_EXPERT_GROUPS
    gmax, grp = top1(gmask)
    gsum = jnp.sum(jnp.where(gmask, jnp.exp(logits - gmax), 0.0), axis=-1, keepdims=True)
    p_grp = 1.0 / gsum
    lo = N_EXPERT_GROUPS + EXPERTS_PER_GROUP * grp
    emask = jnp.logical_and(lane_f >= lo, lane_f < lo + EXPERTS_PER_GROUP)
    v1, i1 = top1(emask)
    v2, i2 = top1(jnp.logical_and(emask, lane_f != i1))
    e21 = jnp.exp(v2 - v1)
    w1 = p_grp / (1.0 + e21)
    w2 = p_grp * e21 / (1.0 + e21)

    oh1 = lane_f == i1
    oh2 = lane_f == i2
    if valid_rows < tm:
        valid = lax.broadcasted_iota(jnp.int32, logits.shape, 0) < valid_rows
        oh1 = jnp.logical_and(oh1, valid)
        oh2 = jnp.logical_and(oh2, valid)
    oh = oh1.astype(F32) + oh2.astype(F32)
    token = row0 + lax.broadcasted_iota(jnp.int32, (tm, 1), 0)
    key1 = (i1.astype(jnp.int32) - N_EXPERT_GROUPS) * (1 << ASSIGN_BITS) + token
    key2 = (i2.astype(jnp.int32) - N_EXPERT_GROUPS) * (1 << ASSIGN_BITS) + token + half
    w1b = lax.bitcast_convert_type(w1, jnp.int32)
    w2b = lax.bitcast_convert_type(w2, jnp.int32)
    words = jnp.where(lane == 0, key1, jnp.where(lane == 1, key2, jnp.where(lane == 2, w1b,
                      jnp.where(lane == 3, w2b, 0))))
    return words, jnp.sum(oh, axis=0, keepdims=True)


def _store_token_tiles(ref, x):
    n = x.shape[0]
    for c in range(D_MODEL // LANES):
        ref[pl.ds(c, n, stride=TILE_ROWS), :] = x[:, c * LANES:(c + 1) * LANES]


def _load_token_tiles(ref, n):
    return jnp.concatenate([ref[pl.ds(c, n, stride=TILE_ROWS), :] for c in range(D_MODEL // LANES)], axis=1)


def _pack_bf16_pairs(x):
    hw = x.shape[1] // 2
    bits = lambda v: lax.bitcast_convert_type(v.astype(BF16).astype(F32), jnp.uint32)
    return (bits(x[:, hw:]) & jnp.uint32(0xFFFF0000)) | (bits(x[:, :hw]) >> 16)


def _unpack_bf16_pairs(w):
    lo = lax.bitcast_convert_type(w << 16, F32)
    hi = lax.bitcast_convert_type(w & jnp.uint32(0xFFFF0000), F32)
    return jnp.concatenate([lo, hi], axis=1).astype(BF16)


def _out_proj_rows(yc_ref, o_ref, sa_ref, sb_ref, x_ref, wc_ref, wa_ref, wo_ref, g2_ref, wr_ref, br_ref,
                   x2_ref, h2p_ref, route_ref, keys_ref, cnt_ref, *, valid_rows, half):
    tm = x_ref.shape[0]
    parts = [(r0, tm // OUT_PROJ_PARTS) for r0 in range(0, tm, tm // OUT_PROJ_PARTS)]
    for r0, n in parts:
        rows = pl.ds(r0, n)
        y_conv = jnp.dot(yc_ref[rows, :], wc_ref[...], preferred_element_type=F32)
        y_attn = jnp.dot(o_ref[rows, :], wa_ref[...], preferred_element_type=F32)
        mix = (sa_ref[rows, :].astype(F32) * y_conv + sb_ref[rows, :].astype(F32) * y_attn).astype(BF16)
        x2_ref[rows, :] = x_ref[rows, :] + jnp.dot(mix, wo_ref[...], preferred_element_type=F32)
    for r0, n in parts:
        rows = pl.ds(r0, n)
        h2 = _rms_norm_f32(x2_ref[rows, :], g2_ref[...])
        h2p_ref[rows, :] = _pack_bf16_pairs(h2)
        logits = jnp.dot(h2.astype(BF16), wr_ref[...], preferred_element_type=F32) + br_ref[...]
        words, cnt = _route_rows(logits, pl.program_id(0) * tm + r0, min(max(valid_rows - r0, 0), n), half)
        route_ref[rows, :] = words
        keys_ref[:, rows] = words.T[:TILE_ROWS, :]
        cnt_ref[...] += cnt


def _out_proj_kernel(*refs, n_first, valid_rows_second, half):
    first, second, shared = refs[0:5], refs[5:10], refs[10:]
    cnt_ref = shared[-1]
    tm = first[4].shape[0]

    @pl.when(pl.program_id(0) == 0)
    def _():
        cnt_ref[...] = jnp.zeros_like(cnt_ref)

    @pl.when(pl.program_id(0) < n_first)
    def _():
        _out_proj_rows(*first, *shared, valid_rows=tm, half=half)

    @pl.when(pl.program_id(0) >= n_first)
    def _():
        _out_proj_rows(*second, *shared, valid_rows=valid_rows_second, half=half)


def _out_proj(acts_a, acts_b, wc, wa, wo, g2, wr, br, *, tm, valid_rows_b, half):
    na = acts_a[4].shape[0] // tm
    nb = acts_b[4].shape[0] // tm
    assert nb == 1
    m_total = (na + nb) * tm
    spec_a = lambda width: pl.BlockSpec((tm, width), lambda i: (jnp.minimum(i, na - 1), 0))
    spec_b = lambda width: pl.BlockSpec((tm, width), lambda i: (jnp.maximum(i - na, 0), 0))
    widths = (D_CONV, Q_DIM, D_MODEL, D_MODEL, D_MODEL)
    in_specs = [spec_a(w) for w in widths] + [spec_b(w) for w in widths]
    in_specs += [_const_spec(wc.shape), _const_spec(wa.shape), _const_spec(wo.shape),
                 _const_spec(g2.shape), _const_spec(wr.shape), _const_spec(br.shape)]
    orow = lambda width: pl.BlockSpec((tm, width), lambda i: (i, 0))
    return pl.pallas_call(
        functools.partial(_out_proj_kernel, n_first=na, valid_rows_second=valid_rows_b, half=half),
        grid=(na + nb,),
        in_specs=in_specs,
        out_specs=[orow(D_MODEL), orow(D_MODEL // 2), orow(ROUTER_LANES),
                   pl.BlockSpec((TILE_ROWS, tm), lambda i: (0, i)),
                   pl.BlockSpec((1, ROUTER_LANES), lambda i: (0, 0))],
        out_shape=[jax.ShapeDtypeStruct((m_total, D_MODEL), F32),
                   jax.ShapeDtypeStruct((m_total, D_MODEL // 2), jnp.uint32),
                   jax.ShapeDtypeStruct((m_total, ROUTER_LANES), jnp.int32),
                   jax.ShapeDtypeStruct((TILE_ROWS, m_total), jnp.int32),
                   jax.ShapeDtypeStruct((1, ROUTER_LANES), F32)],
        compiler_params=pltpu.CompilerParams(dimension_semantics=("arbitrary",),
                                             vmem_limit_bytes=V7X_VMEM_LIMIT_BYTES),
        name="out_proj",
    )(*acts_a, *acts_b, wc, wa, wo, g2, wr, br)


def _block_table_kernel(counts_ref, bexp_ref, bpos_ref, bcnt_ref, bslot_ref, bnext_ref, nused_ref, first_ref,
                        *, n_blocks):
    def per_expert(e, carry):
        blk0, pos0, ordinal = carry
        cnt = counts_ref[e]
        nblk = lax.shift_right_logical(cnt + (MOE_BLOCK - 1), MOE_BLOCK_LOG2)
        first_ref[e] = jnp.where(nblk > 0, blk0, -1)

        def mark(b, c):
            off = (b - blk0) * MOE_BLOCK
            bexp_ref[b] = e
            bpos_ref[b] = pos0 + off
            bcnt_ref[b] = jnp.minimum(cnt - off, MOE_BLOCK)
            bslot_ref[b] = jnp.where(b == blk0, ordinal & 1, -1)
            bnext_ref[b] = -1
            return c
        lax.fori_loop(blk0, blk0 + nblk, mark, 0)
        return blk0 + nblk, pos0 + cnt, ordinal + jnp.where(nblk > 0, 1, 0)

    n_used, _, _ = lax.fori_loop(0, N_EXPERTS, per_expert, (0, 0, 0))
    nused_ref[0] = n_used

    def unused(b, c):
        bexp_ref[b] = N_EXPERTS - 1
        bpos_ref[b] = 0
        bcnt_ref[b] = 0
        bslot_ref[b] = -1
        bnext_ref[b] = -1
        return c
    lax.fori_loop(n_used, n_blocks, unused, 0)

    def link(k, nxt):
        e = N_EXPERTS - 1 - k
        fb = first_ref[e]

        @pl.when(fb >= 0)
        def _():
            bnext_ref[fb] = nxt
        return jnp.where(fb >= 0, e, nxt)
    lax.fori_loop(0, N_EXPERTS, link, -1)


def _block_tables(counts, n_blocks):
    smem = pl.BlockSpec(memory_space=pltpu.SMEM)
    blk = jax.ShapeDtypeStruct((n_blocks,), jnp.int32)
    return pl.pallas_call(
        functools.partial(_block_table_kernel, n_blocks=n_blocks),
        in_specs=[smem],
        out_specs=[smem] * 6,
        out_shape=[blk] * 5 + [jax.ShapeDtypeStruct((1,), jnp.int32)],
        scratch_shapes=[pltpu.SMEM((N_EXPERTS,), jnp.int32)],
        name="block_tables",
    )(counts)


def _moe_kernel(tok_ref, dst_ref, bexp_ref, bslot_ref, bnext_ref, nused_ref,
                h2p_ref, wg_hbm, wu_hbm, wd_hbm, contrib_hbm,
                xs_ref, ys_ref, wgf_ref, wuf_ref, wdf_ref, wgb_ref, wub_ref, wdb_ref, ssem_ref, wsem_ref,
                *, m_total, half, n_blocks):
    step = pl.program_id(0)
    n_used = nused_ref[0]
    tile = lambda t: pl.ds(pl.multiple_of(t * TILE_ROWS, TILE_ROWS), TILE_ROWS)

    def gather(blk, slt):
        for r in range(MOE_BLOCK):
            xs_ref[slt, pl.ds(r, 1), :] = h2p_ref[pl.ds(tok_ref[blk * MOE_BLOCK + r], 1), :]

    def scatter(blk, slt, r):
        return pltpu.make_async_copy(ys_ref.at[slt, tile(r), :], contrib_hbm.at[tile(dst_ref[blk * MOE_BLOCK + r]), :],
                                     ssem_ref.at[slt])

    def scatter_wait(slt, r):
        pltpu.make_async_copy(ys_ref.at[slt, tile(r), :], contrib_hbm.at[tile(0), :], ssem_ref.at[slt]).wait()

    @pl.when(step == 0)
    def _():
        ys_ref[...] = jnp.zeros_like(ys_ref)
        gap = half - m_total
        fills = [(2 * half + s * MOE_BLOCK, MOE_BLOCK) for s in range(2)]
        fills += [(k * half + m_total, gap) for k in range(TOP_K)] if gap else []
        for start, n in fills:
            fill = pltpu.make_async_copy(ys_ref.at[0, pl.ds(0, n * TILE_ROWS), :],
                                         contrib_hbm.at[pl.ds(start * TILE_ROWS, n * TILE_ROWS), :], ssem_ref.at[0])
            fill.start()
            fill.wait()

    def weight_copies(e, s):
        return [pltpu.make_async_copy(src.at[e], dst.at[s], wsem_ref.at[s])
                for src, dst in ((wg_hbm, wgf_ref), (wu_hbm, wuf_ref), (wd_hbm, wdf_ref))]

    @pl.when(step == 0)
    def _():
        for c in weight_copies(bexp_ref[0], 0):
            c.start()
        gather(0, 0)

    def wait_scatters(slt):
        for r in range(MOE_BLOCK):
            scatter_wait(slt, r)

    def expert_weights(i):
        wslot = bslot_ref[i]

        @pl.when(wslot >= 0)
        def _():
            for c in weight_copies(0, wslot):
                c.wait()

        @pl.when(jnp.logical_and(wslot >= 0, bnext_ref[i] >= 0))
        def _():
            for c in weight_copies(bnext_ref[i], 1 - wslot):
                c.start()

        @pl.when(wslot >= 0)
        def _():
            wgb_ref[...] = wgf_ref[wslot].astype(BF16)
            wub_ref[...] = wuf_ref[wslot].astype(BF16)
            wdb_ref[...] = wdf_ref[wslot].astype(BF16)

    def run_block(i, slt):
        gather(jnp.minimum(i + 1, n_blocks - 1), 1 - slt)
        xb = _unpack_bf16_pairs(xs_ref[slt])
        gate = jnp.dot(xb, wgb_ref[...], preferred_element_type=F32)
        up = jnp.dot(xb, wub_ref[...], preferred_element_type=F32)
        hmid = (jax.nn.silu(gate) * up).astype(BF16)
        _store_token_tiles(ys_ref.at[slt], jnp.dot(hmid, wdb_ref[...], preferred_element_type=F32))
        for r in range(MOE_BLOCK):
            scatter(i, slt, r).start(priority=r % 2)

    for slt in range(2):
        i = 2 * step + slt
        active = i < n_used
        last_active = i == n_used - 1
        pl.when(jnp.logical_and(active, i >= 2))(functools.partial(wait_scatters, slt))
        expert_weights(i)
        pl.when(active)(functools.partial(run_block, i, slt))
        pl.when(last_active)(functools.partial(wait_scatters, slt))
        pl.when(jnp.logical_and(last_active, i >= 1))(functools.partial(wait_scatters, 1 - slt))


def _row_table_kernel(bpos_ref, bcnt_ref, order_ref, tok_ref, dst_ref, *, half, n_blocks):
    rows_per_block = MOE_BLOCK // LANES
    lane = lax.broadcasted_iota(jnp.int32, (rows_per_block, LANES), 1)
    r = lax.broadcasted_iota(jnp.int32, (rows_per_block, LANES), 0) * LANES + lane

    def per_block(b, c):
        p = bpos_ref[b]
        sh = p & (LANES - 1)
        win = order_ref[pl.ds(lax.shift_right_logical(p, LANES_LOG2), rows_per_block + 1), :]
        rolled = pltpu.roll(win, (LANES - sh) & (LANES - 1), axis=1)
        assign = jnp.where(lane < LANES - sh, rolled[:rows_per_block], rolled[1:])
        tok_ref[pl.ds(b * rows_per_block, rows_per_block), :] = jnp.where(assign >= half, assign - half, assign)
        trash = 2 * half + (b & 1) * MOE_BLOCK + r
        dst_ref[pl.ds(b * rows_per_block, rows_per_block), :] = jnp.where(r < bcnt_ref[b], assign, trash)
        return c
    lax.fori_loop(0, n_blocks, per_block, 0, unroll=6)


def _row_tables(order, bpos, bcnt, half):
    n_blocks = bpos.shape[0]
    assert MOE_BLOCK % LANES == 0 and order.shape[0] % LANES == 0
    out = jax.ShapeDtypeStruct((n_blocks * MOE_BLOCK // LANES, LANES), jnp.int32)
    tok, dst = pl.pallas_call(
        functools.partial(_row_table_kernel, half=half, n_blocks=n_blocks),
        grid_spec=pltpu.PrefetchScalarGridSpec(
            num_scalar_prefetch=2, grid=(1,),
            in_specs=[pl.BlockSpec((order.shape[0] // LANES, LANES), lambda i, *_: (0, 0))],
            out_specs=[pl.BlockSpec(out.shape, lambda i, *_: (0, 0))] * 2),
        out_shape=[out, out],
        name="row_tables",
    )(bpos, bcnt, order.reshape(-1, LANES))
    return tok.reshape(-1), dst.reshape(-1)


def _moe_experts(order, tables, h2p, w_gate, w_up, w_down, *, m_total, half):
    bexp, bpos, bcnt, bslot, bnext, n_used = tables
    n_blocks = bexp.shape[0]
    assert 0 <= half - m_total <= MOE_BLOCK and n_blocks % 2 == 0
    tok, dst = _row_tables(order, bpos, bcnt, half)
    tables = (bexp, bslot, bnext, n_used)
    hbm = pl.BlockSpec(memory_space=pl.ANY)
    grid_spec = pltpu.PrefetchScalarGridSpec(
        num_scalar_prefetch=2 + len(tables),
        grid=(n_blocks // 2,),
        in_specs=[_const_spec(h2p.shape), hbm, hbm, hbm],
        out_specs=hbm,
        scratch_shapes=[pltpu.VMEM((2, MOE_BLOCK, D_MODEL // 2), jnp.uint32),
                        pltpu.VMEM((2, MOE_BLOCK * TILE_ROWS, LANES), F32),
                        pltpu.VMEM((2, D_MODEL, D_EXPERT), F32),
                        pltpu.VMEM((2, D_MODEL, D_EXPERT), F32),
                        pltpu.VMEM((2, D_EXPERT, D_MODEL), F32),
                        pltpu.VMEM((D_MODEL, D_EXPERT), BF16),
                        pltpu.VMEM((D_MODEL, D_EXPERT), BF16),
                        pltpu.VMEM((D_EXPERT, D_MODEL), BF16),
                        pltpu.SemaphoreType.DMA((2,)),
                        pltpu.SemaphoreType.DMA((2,))],
    )
    return pl.pallas_call(
        functools.partial(_moe_kernel, m_total=m_total, half=half, n_blocks=n_blocks),
        grid_spec=grid_spec,
        out_shape=jax.ShapeDtypeStruct(((2 * half + 2 * MOE_BLOCK) * TILE_ROWS, LANES), F32),
        compiler_params=pltpu.CompilerParams(dimension_semantics=("arbitrary",),
                                             vmem_limit_bytes=V7X_VMEM_LIMIT_LARGE_BYTES),
        name="moe_experts",
    )(tok, dst, *tables, h2p, w_gate, w_up, w_down)


def _combine_kernel(c0_ref, c1_ref, x2_ref, route_ref, gf_ref, y_ref):
    tc = x2_ref.shape[0]
    w = lax.bitcast_convert_type(route_ref[:, TOP_K:2 * TOP_K], F32)
    moe = w[:, 0:1] * _load_token_tiles(c0_ref, tc) + w[:, 1:2] * _load_token_tiles(c1_ref, tc)
    y_ref[...] = _rms_norm_f32(x2_ref[...] + moe, gf_ref[...])


def _moe_combine(contrib, x2, route, gf, *, row_off, m, tc, half):
    off = row_off // tc
    assert row_off % tc == 0 and half % tc == 0
    ctile = lambda k: pl.BlockSpec((tc * TILE_ROWS, LANES), lambda i: (i + off + k * (half // tc), 0))
    return pl.pallas_call(
        _combine_kernel,
        grid=(m // tc,),
        in_specs=[ctile(0), ctile(1),
                  pl.BlockSpec((tc, D_MODEL), lambda i: (i + off, 0)),
                  pl.BlockSpec((tc, ROUTER_LANES), lambda i: (i + off, 0)),
                  _const_spec((1, D_MODEL))],
        out_specs=pl.BlockSpec((tc, D_MODEL), lambda i: (i, 0)),
        out_shape=jax.ShapeDtypeStruct((m, D_MODEL), F32),
        compiler_params=pltpu.CompilerParams(dimension_semantics=("arbitrary",),
                                             vmem_limit_bytes=V7X_VMEM_LIMIT_BYTES),
        name="moe_combine_prompt" if row_off == 0 else "moe_combine_sample",
    )(contrib, contrib, x2, route, gf)


def _t5_bucket(dist):
    n = jnp.maximum(dist, 0)
    max_exact = N_BUCKETS // 2
    nf = jnp.maximum(n, 1).astype(F32)
    large = max_exact + (jnp.log(nf / max_exact) / math.log(MAX_DISTANCE / max_exact)
                         * (N_BUCKETS - max_exact)).astype(jnp.int32)
    large = jnp.minimum(large, N_BUCKETS - 1)
    return jnp.where(n < max_exact, n, large)


def _bucket_bias(rel_bias, dist, valid):
    buckets = _t5_bucket(dist).reshape(1, -1)
    onehot = (buckets == jnp.arange(N_BUCKETS, dtype=jnp.int32)[:, None]).astype(F32)
    bias = jnp.dot(rel_bias.astype(F32).T, onehot, precision=lax.Precision.HIGHEST)
    return jnp.where(valid.reshape(1, -1), bias, NEG_BIG).reshape((rel_bias.shape[1],) + dist.shape)


def _prompt_bias_table(rel_bias):
    qi = jnp.arange(ATTN_BLOCK, dtype=jnp.int32)[:, None]
    kj = jnp.arange(2 * ATTN_BLOCK, dtype=jnp.int32)[None, :] - ATTN_BLOCK
    dist = qi - kj
    return _bucket_bias(rel_bias, dist, (dist >= 0) & (dist <= WINDOW))


def _sample_bias_table(rel_bias, w_buf):
    dist = w_buf - jnp.arange(w_buf + 1, dtype=jnp.int32)
    return _bucket_bias(rel_bias, dist, dist <= WINDOW)


def kernel(x_prompt, x_sample, cache_conv, cache_k, cache_v, norm1_g, w_in, conv_w, w_conv_out, w_attn_out, w_o, sinks, rel_bias, norm2_g, w_router_group, b_router_group, w_router_expert, b_router_expert, w_e_gate, w_e_up, w_e_down, norm_f_g):
    assert norm1_g.shape[0] == 1, "single-layer configuration"
    batch, seq, _ = x_prompt.shape
    nseq = x_sample.shape[0]
    w_buf = cache_k.shape[2]
    mp = batch * seq
    m_total = mp + nseq
    assert seq % TM_DENSE == 0 and seq % TM_IN_PROJ == 0 and seq % ATTN_BLOCK == 0 and mp % COMBINE_BLOCK == 0
    assert nseq % SAMPLE_SEQ_PER_STEP == 0 and mp % nseq == 0
    assert TOP_K == 2 and MOE_BLOCK == 1 << MOE_BLOCK_LOG2 and m_total * TOP_K < 1 << ASSIGN_BITS
    assert CONV_WIDTH == 3 and LANES == 1 << LANES_LOG2 and x_sample.shape[1] == 1 and w_buf == WINDOW
    assert seq % (ATTN_Q_BLOCKS * ATTN_BLOCK) == 0

    g1 = norm1_g[0][None, :]
    g2 = norm2_g[0][None, :]
    gf = norm_f_g[None, :]
    wi = w_in[0].astype(BF16)
    cw = conv_w[0]
    wc = w_conv_out[0].astype(BF16)
    wa = w_attn_out[0].astype(BF16)
    wo = w_o[0].astype(BF16)
    pad_cols = ROUTER_LANES - N_EXPERT_GROUPS - N_EXPERTS
    wr = jnp.concatenate([w_router_group[0], w_router_expert[0],
                          jnp.zeros((D_MODEL, pad_cols), F32)], axis=1).astype(BF16)
    br = jnp.concatenate([b_router_group[0], b_router_expert[0], jnp.zeros((pad_cols,), F32)])[None, :]
    sink = sinks[0].astype(F32)

    xp = x_prompt.reshape(mp, D_MODEL)
    bps = seq // TM_IN_PROJ
    yc_p, q_p, k_p, v_p, sa_p, sb_p, ut_p, kvt_p = _in_proj(
        xp, g1, wi, cw, tm=TM_IN_PROJ, blocks_per_seq=bps, u_tail=CARRY, kv_tail=WINDOW, parts=IN_PROJ_PARTS)
    o_p = _attn_prompt(q_p, k_p, v_p, _prompt_bias_table(rel_bias), sink, batch, seq)

    pad_rows = lambda t: jnp.pad(t, ((0, TM_DENSE - nseq), (0, 0)))
    xs = pad_rows(x_sample.reshape(nseq, D_MODEL))
    hist = (pad_rows(cache_conv[0][:, 0, :]), pad_rows(cache_conv[0][:, 1, :]))
    yc_s, q_s, _, _, sa_s, sb_s, ut_s, kvt_s = _in_proj(
        xs, g1, wi, cw, tm=TM_DENSE, blocks_per_seq=1, u_tail=TM_DENSE, kv_tail=TM_DENSE, hist=hist,
        gate_dtype=F32)
    u_s = ut_s[0, :nseq]
    kv_s = kvt_s[0, :nseq]
    head_mask = (jnp.arange(KV_DIM)[None, :] // HEAD_DIM == jnp.arange(N_HEADS)[:, None] // GROUP)
    qbd = (jnp.tile(q_s[:nseq].reshape(nseq, N_HEADS, HEAD_DIM), (1, 1, N_KV_HEADS))
           * head_mask[None].astype(BF16))
    to_keys_minor = lambda c: jnp.transpose(c.reshape(nseq, w_buf, KV_DIM), (0, 2, 1))
    from_keys_minor = lambda c: jnp.transpose(c, (0, 2, 1)).reshape(1, nseq, w_buf, N_KV_HEADS, HEAD_DIM)
    o_s, kwin_s, vwin_s = _attn_sample(qbd, to_keys_minor(cache_k[0]), to_keys_minor(cache_v[0]), kv_s,
                                       _sample_bias_table(rel_bias, w_buf), sink[:, None],
                                       head_mask.astype(F32))
    o_s = pad_rows(o_s.reshape(nseq, Q_DIM))

    half = -(-m_total // COMBINE_BLOCK) * COMBINE_BLOCK
    assert half % nseq == 0 and TOP_K * half < 1 << ASSIGN_BITS
    x2, h2p, route, route_t, cnt = _out_proj((yc_p, o_p, sa_p, sb_p, xp), (yc_s, o_s, sa_s, sb_s, xs),
                                             wc, wa, wo, g2, wr, br, tm=TM_DENSE, valid_rows_b=nseq, half=half)

    n_assign = m_total * TOP_K
    keys = route_t[0:TOP_K, :m_total].reshape(-1)
    counts = cnt[0, N_EXPERT_GROUPS:N_EXPERT_GROUPS + N_EXPERTS].astype(jnp.int32)
    order = jnp.pad(jnp.sort(keys) & ((1 << ASSIGN_BITS) - 1), (0, MOE_BLOCK))
    n_blocks = -(-n_assign // MOE_BLOCK) + N_EXPERTS
    n_blocks += n_blocks % 2
    tables = _block_tables(counts, n_blocks)
    contrib = _moe_experts(order, tables, h2p, w_e_gate[0], w_e_up[0], w_e_down[0],
                           m_total=m_total, half=half)
    y_p = _moe_combine(contrib, x2, route, gf, row_off=0, m=mp, tc=COMBINE_BLOCK, half=half)
    y_s = _moe_combine(contrib, x2, route, gf, row_off=mp, m=nseq, tc=nseq, half=half)

    y_prompt = y_p.reshape(batch, seq, D_MODEL)
    y_sample = y_s.reshape(nseq, 1, D_MODEL)
    conv_state_prompt = ut_p.reshape(batch, bps, CARRY, D_CONV)[:, -1, CARRY - (CONV_WIDTH - 1):, :][None]
    kv_last = kvt_p.reshape(batch, bps, WINDOW, 2 * KV_DIM)[:, -1]
    k_win_prompt = kv_last[:, :, :KV_DIM].reshape(batch, WINDOW, N_KV_HEADS, HEAD_DIM)[None]
    v_win_prompt = kv_last[:, :, KV_DIM:].reshape(batch, WINDOW, N_KV_HEADS, HEAD_DIM)[None]
    conv_state_sample = jnp.concatenate([cache_conv[0][:, 1:, :], u_s[:, None, :]], axis=1)[None]
    k_win_sample = from_keys_minor(kwin_s)
    v_win_sample = from_keys_minor(vwin_s)
    return (y_prompt, y_sample, conv_state_prompt, k_win_prompt, v_win_prompt,
            conv_state_sample, k_win_sample, v_win_sample)
```

```python
import functools
import math

import jax
import jax.numpy as jnp
from jax import lax
from jax.experimental import pallas as pl
from jax.experimental.pallas import tpu as pltpu

D_MODEL = 1024
D_CONV = 1024
CONV_WIDTH = 3
N_HEADS = 16
N_KV_HEADS = 4
HEAD_DIM = 64
GROUP = N_HEADS // N_KV_HEADS
WINDOW = 128
Q_DIM = N_HEADS * HEAD_DIM
KV_DIM = N_KV_HEADS * HEAD_DIM
N_BUCKETS = 32
MAX_DISTANCE = 128
N_EXPERT_GROUPS = 4
EXPERTS_PER_GROUP = 8
N_EXPERTS = N_EXPERT_GROUPS * EXPERTS_PER_GROUP
TOP_K = 2
D_EXPERT = 512
EPS = 1e-6

BF16 = jnp.bfloat16
F32 = jnp.float32
NEG_BIG = -1e30

V7X_VMEM_LIMIT_BYTES = 56 * 1024 * 1024
V7X_VMEM_LIMIT_LARGE_BYTES = 62 * 1024 * 1024
TILE_ROWS = 8
LANES = 128
LANES_LOG2 = 7
CARRY = TILE_ROWS
ROUTER_LANES = 128
TM_DENSE = 512
TM_IN_PROJ = 1024
OUT_PROJ_PARTS = 2
IN_PROJ_PARTS = 4
ATTN_BLOCK = 128
MOE_BLOCK = 256
MOE_BLOCK_LOG2 = 8
ASSIGN_BITS = 16
COMBINE_BLOCK = 256
SAMPLE_SEQ_PER_STEP = 32
HEADS_PER_STORE = LANES // HEAD_DIM


def _const_spec(shape):
    nd = len(shape)
    return pl.BlockSpec(shape, lambda *_: (0,) * nd, pipeline_mode=pl.Buffered(1))


def _rms_norm_f32(xf, g):
    return xf * lax.rsqrt(jnp.mean(xf * xf, axis=-1, keepdims=True) + EPS) * g


def _in_proj_kernel(*refs, tm, sample, blocks_per_seq, u_tail, kv_tail, parts):
    if sample:
        (x_ref, hist0_ref, hist1_ref, g_ref, w_ref,
         cw_ref, yc_ref, q_ref, k_ref, v_ref, sa_ref, sb_ref, ut_ref, kvt_ref) = refs
    else:
        (x_ref, g_ref, w_ref,
         cw_ref, yc_ref, q_ref, k_ref, v_ref, sa_ref, sb_ref, ut_ref, kvt_ref, ubuf_ref) = refs

    widths = (D_CONV, D_CONV, D_CONV, Q_DIM, 2 * KV_DIM, D_MODEL, D_MODEL)
    starts = [sum(widths[:n]) for n in range(len(widths))]
    wcb_ref, wcc_ref, wch_ref, wq_ref, wkv_ref, wga_ref, wgb_ref = [
        w_ref.at[:, pl.ds(a, n)] for a, n in zip(starts, widths)]
    w0 = cw_ref[0:1, :]
    w1 = cw_ref[1:2, :]
    w2 = cw_ref[2:3, :]

    if not sample:
        @pl.when(pl.program_id(0) % blocks_per_seq == 0)
        def _():
            ubuf_ref[0:CARRY, :] = jnp.zeros((CARRY, D_CONV), F32)

    n = tm // parts
    assert u_tail <= n and kv_tail <= n
    for r0 in range(0, tm, n):
        rows = pl.ds(r0, n)
        last = r0 + n == tm
        h = _rms_norm_f32(x_ref[rows, :], g_ref[...]).astype(BF16)

        def proj(part_ref, h=h):
            return jnp.dot(h, part_ref[...], preferred_element_type=F32)

        u = proj(wcc_ref) * proj(wch_ref)
        if sample:
            conv = w0 * hist0_ref[rows, :] + w1 * hist1_ref[rows, :] + w2 * u
        else:
            ubuf_ref[pl.ds(CARRY + r0, n), :] = u
            conv = (w0 * ubuf_ref[pl.ds(CARRY - 2 + r0, n), :] + w1 * ubuf_ref[pl.ds(CARRY - 1 + r0, n), :]
                    + w2 * u)
            if last:
                ubuf_ref[0:CARRY, :] = u[n - CARRY:, :]
        yc_ref[rows, :] = (proj(wcb_ref) * conv).astype(BF16)
        q_ref[rows, :] = (proj(wq_ref) * (HEAD_DIM ** -0.5)).astype(BF16)
        kv = proj(wkv_ref)
        k_ref[rows, :] = kv[:, :KV_DIM].astype(BF16)
        v_ref[rows, :] = kv[:, KV_DIM:].astype(BF16)
        if last:
            ut_ref[0] = u[n - u_tail:, :]
            kvt_ref[0] = kv[n - kv_tail:, :]
        sa_ref[rows, :] = jax.nn.sigmoid(proj(wga_ref)).astype(sa_ref.dtype)
        sb_ref[rows, :] = jax.nn.sigmoid(proj(wgb_ref)).astype(sb_ref.dtype)


def _in_proj(x, g1, w_in, conv_w, *, tm, blocks_per_seq, u_tail, kv_tail, hist=None, gate_dtype=BF16, parts=1):
    m = x.shape[0]
    nblk = m // tm
    sample = hist is not None
    row = lambda width: pl.BlockSpec((tm, width), lambda i: (i, 0))
    in_specs = [row(D_MODEL)]
    args = [x]
    if sample:
        in_specs += [row(D_CONV), row(D_CONV)]
        args += list(hist)
    in_specs += [_const_spec((1, D_MODEL)), _const_spec(w_in.shape), _const_spec(conv_w.shape)]
    args += [g1, w_in, conv_w]
    out_shape = [
        jax.ShapeDtypeStruct((m, D_CONV), BF16),
        jax.ShapeDtypeStruct((m, Q_DIM), BF16),
        jax.ShapeDtypeStruct((m, KV_DIM), BF16),
        jax.ShapeDtypeStruct((m, KV_DIM), BF16),
        jax.ShapeDtypeStruct((m, D_MODEL), gate_dtype),
        jax.ShapeDtypeStruct((m, D_MODEL), gate_dtype),
        jax.ShapeDtypeStruct((nblk, u_tail, D_CONV), F32),
        jax.ShapeDtypeStruct((nblk, kv_tail, 2 * KV_DIM), F32),
    ]
    out_specs = [row(D_CONV), row(Q_DIM), row(KV_DIM), row(KV_DIM), row(D_MODEL), row(D_MODEL),
                 pl.BlockSpec((1, u_tail, D_CONV), lambda i: (i, 0, 0)),
                 pl.BlockSpec((1, kv_tail, 2 * KV_DIM), lambda i: (i, 0, 0))]
    scratch = [] if sample else [pltpu.VMEM((tm + CARRY, D_CONV), F32)]
    return pl.pallas_call(
        functools.partial(_in_proj_kernel, tm=tm, sample=sample, blocks_per_seq=blocks_per_seq,
                          u_tail=u_tail, kv_tail=kv_tail, parts=parts),
        grid=(nblk,),
        in_specs=in_specs,
        out_specs=out_specs,
        out_shape=out_shape,
        scratch_shapes=scratch,
        compiler_params=pltpu.CompilerParams(dimension_semantics=("arbitrary",),
                                             vmem_limit_bytes=V7X_VMEM_LIMIT_LARGE_BYTES),
        name="in_proj_sample" if sample else "in_proj_prompt",
    )(*args)


def _attn_prompt_kernel(sink_ref, q_ref, kc_ref, kp_ref, vc_ref, vp_ref, bias_ref, o_ref):
    first = pl.program_id(1) == 0
    col = lax.broadcasted_iota(jnp.int32, (ATTN_BLOCK, 2 * ATTN_BLOCK), 1)
    no_prev = jnp.logical_and(first, col < ATTN_BLOCK)
    for g in range(N_KV_HEADS):
        ks = slice(g * HEAD_DIM, (g + 1) * HEAD_DIM)
        kcat = jnp.concatenate([kp_ref[:, ks], kc_ref[:, ks]], axis=0)
        vcat = jnp.concatenate([vp_ref[:, ks], vc_ref[:, ks]], axis=0)
        for h0 in range(g * GROUP, (g + 1) * GROUP, HEADS_PER_STORE):
            outs = []
            for h in range(h0, h0 + HEADS_PER_STORE):
                hs = slice(h * HEAD_DIM, (h + 1) * HEAD_DIM)
                s = lax.dot_general(q_ref[:, hs], kcat, (((1,), (1,)), ((), ())),
                                    preferred_element_type=F32)
                s = jnp.where(no_prev, NEG_BIG, s + bias_ref[h])
                sink = sink_ref[h]
                m = jnp.maximum(jnp.max(s, axis=-1, keepdims=True), sink)
                p = jnp.exp(s - m)
                denom = jnp.sum(p, axis=-1, keepdims=True) + jnp.exp(sink - m)
                o = jnp.dot(p.astype(BF16), vcat, preferred_element_type=F32)
                outs.append((o / denom).astype(BF16))
            o_ref[:, h0 * HEAD_DIM:(h0 + HEADS_PER_STORE) * HEAD_DIM] = jnp.concatenate(outs, axis=1)


def _attn_prompt(q, k, v, bias, sinks, batch, seq):
    nb = seq // ATTN_BLOCK
    cur = lambda b, i: (b * nb + i, 0)
    prev = lambda b, i: (b * nb + jnp.maximum(i - 1, 0), 0)
    return pl.pallas_call(
        _attn_prompt_kernel,
        grid=(batch, nb),
        in_specs=[pl.BlockSpec(memory_space=pltpu.SMEM),
                  pl.BlockSpec((ATTN_BLOCK, Q_DIM), cur),
                  pl.BlockSpec((ATTN_BLOCK, KV_DIM), cur),
                  pl.BlockSpec((ATTN_BLOCK, KV_DIM), prev),
                  pl.BlockSpec((ATTN_BLOCK, KV_DIM), cur),
                  pl.BlockSpec((ATTN_BLOCK, KV_DIM), prev),
                  _const_spec(bias.shape)],
        out_specs=pl.BlockSpec((ATTN_BLOCK, Q_DIM), cur),
        out_shape=jax.ShapeDtypeStruct((batch * seq, Q_DIM), BF16),
        compiler_params=pltpu.CompilerParams(dimension_semantics=("arbitrary", "arbitrary"),
                                             vmem_limit_bytes=V7X_VMEM_LIMIT_BYTES),
        name="attn_prompt",
    )(sinks, q, k, k, v, v, bias)


def _attn_sample_kernel(qbd_ref, ckt_ref, cvt_ref, kvn_ref, bias_ref, sink_ref, mask_ref,
                        o_ref, kwin_ref, vwin_ref, *, w_buf):
    bf16_round = lambda t: t.astype(BF16).astype(F32)
    seqs = range(SAMPLE_SEQ_PER_STEP)
    sink = sink_ref[...]
    newest = lax.broadcasted_iota(jnp.int32, (KV_DIM, w_buf), 1) == w_buf - 1
    kvn_t = kvn_ref[...].T
    for b in seqs:
        kwin_ref[b] = jnp.where(newest, kvn_t[:KV_DIM, b:b + 1], pltpu.roll(ckt_ref[b], w_buf - 1, axis=1))
        vwin_ref[b] = jnp.where(newest, kvn_t[KV_DIM:, b:b + 1], pltpu.roll(cvt_ref[b], w_buf - 1, axis=1))
    s = [jnp.dot(qbd_ref[b], ckt_ref[b].astype(BF16), preferred_element_type=F32) + bias_ref[:, :w_buf]
         for b in seqs]
    s_new = [jnp.sum(qbd_ref[b].astype(F32) * bf16_round(kvn_ref[b:b + 1, :KV_DIM]), axis=-1, keepdims=True)
             + bias_ref[:, w_buf:w_buf + 1] for b in seqs]
    m = [jnp.maximum(jnp.maximum(jnp.max(s[b], axis=-1, keepdims=True), s_new[b]), sink) for b in seqs]
    p = [jnp.exp(s[b] - m[b]) for b in seqs]
    p_new = [jnp.exp(s_new[b] - m[b]) for b in seqs]
    denom = [jnp.sum(p[b], axis=-1, keepdims=True) + p_new[b] + jnp.exp(sink - m[b]) for b in seqs]
    of = [lax.dot_general((p[b] / denom[b]).astype(BF16), cvt_ref[b].astype(BF16), (((1,), (1,)), ((), ())),
                          preferred_element_type=F32)
          + bf16_round(p_new[b] / denom[b]) * bf16_round(kvn_ref[b:b + 1, KV_DIM:]) for b in seqs]
    for b in seqs:
        ob = of[b] * mask_ref[...]
        o_ref[b] = (ob[:, 0:HEAD_DIM] + ob[:, HEAD_DIM:2 * HEAD_DIM]
                    + ob[:, 2 * HEAD_DIM:3 * HEAD_DIM] + ob[:, 3 * HEAD_DIM:]).astype(BF16)


def _attn_sample(qbd, ckt, cvt, kvn, bias, sink_col, head_mask):
    nseq, w_buf = ckt.shape[0], ckt.shape[2]
    sb = SAMPLE_SEQ_PER_STEP
    seq3 = lambda d1, d2: pl.BlockSpec((sb, d1, d2), lambda i: (i, 0, 0))
    win = jax.ShapeDtypeStruct((nseq, KV_DIM, w_buf), F32)
    return pl.pallas_call(
        functools.partial(_attn_sample_kernel, w_buf=w_buf),
        grid=(nseq // sb,),
        in_specs=[seq3(N_HEADS, KV_DIM), seq3(KV_DIM, w_buf), seq3(KV_DIM, w_buf),
                  pl.BlockSpec((sb, 2 * KV_DIM), lambda i: (i, 0)),
                  _const_spec(bias.shape), _const_spec(sink_col.shape), _const_spec(head_mask.shape)],
        out_specs=[seq3(N_HEADS, HEAD_DIM), seq3(KV_DIM, w_buf), seq3(KV_DIM, w_buf)],
        out_shape=[jax.ShapeDtypeStruct((nseq, N_HEADS, HEAD_DIM), BF16), win, win],
        compiler_params=pltpu.CompilerParams(dimension_semantics=("arbitrary",),
                                             vmem_limit_bytes=V7X_VMEM_LIMIT_BYTES),
        name="attn_sample",
    )(qbd, ckt, cvt, kvn, bias, sink_col, head_mask)


def _route_rows(logits, row0, valid_rows, half):
    tm = logits.shape[0]
    lane = lax.broadcasted_iota(jnp.int32, logits.shape, 1)
    lane_f = lane.astype(F32)
    no_lane = float(ROUTER_LANES)

    def top1(mask):
        best = jnp.max(jnp.where(mask, logits, -jnp.inf), axis=-1, keepdims=True)
        idx = jnp.min(jnp.where(jnp.logical_and(mask, logits == best), lane_f, no_lane), axis=-1, keepdims=True)
        return best, idx

    gmask = lane < N_EXPERT_GROUPS
    gmax, grp = top1(gmask)
    gsum = jnp.sum(jnp.where(gmask, jnp.exp(logits - gmax), 0.0), axis=-1, keepdims=True)
    p_grp = 1.0 / gsum
    lo = N_EXPERT_GROUPS + EXPERTS_PER_GROUP * grp
    emask = jnp.logical_and(lane_f >= lo, lane_f < lo + EXPERTS_PER_GROUP)
    v1, i1 = top1(emask)
    v2, i2 = top1(jnp.logical_and(emask, lane_f != i1))
    e21 = jnp.exp(v2 - v1)
    w1 = p_grp / (1.0 + e21)
    w2 = p_grp * e21 / (1.0 + e21)

    oh1 = lane_f == i1
    oh2 = lane_f == i2
    if valid_rows < tm:
        valid = lax.broadcasted_iota(jnp.int32, logits.shape, 0) < valid_rows
        oh1 = jnp.logical_and(oh1, valid)
        oh2 = jnp.logical_and(oh2, valid)
    oh = oh1.astype(F32) + oh2.astype(F32)
    token = row0 + lax.broadcasted_iota(jnp.int32, (tm, 1), 0)
    key1 = (i1.astype(jnp.int32) - N_EXPERT_GROUPS) * (1 << ASSIGN_BITS) + token
    key2 = (i2.astype(jnp.int32) - N_EXPERT_GROUPS) * (1 << ASSIGN_BITS) + token + half
    w1b = lax.bitcast_convert_type(w1, jnp.int32)
    w2b = lax.bitcast_convert_type(w2, jnp.int32)
    words = jnp.where(lane == 0, key1, jnp.where(lane == 1, key2, jnp.where(lane == 2, w1b,
                      jnp.where(lane == 3, w2b, 0))))
    return words, jnp.sum(oh, axis=0, keepdims=True)


def _store_token_tiles(ref, x):
    n = x.shape[0]
    for c in range(D_MODEL // LANES):
        ref[pl.ds(c, n, stride=TILE_ROWS), :] = x[:, c * LANES:(c + 1) * LANES]


def _load_token_tiles(ref, n):
    return jnp.concatenate([ref[pl.ds(c, n, stride=TILE_ROWS), :] for c in range(D_MODEL // LANES)], axis=1)


def _pack_bf16_pairs(x):
    hw = x.shape[1] // 2
    bits = lambda v: lax.bitcast_convert_type(v.astype(BF16).astype(F32), jnp.uint32)
    return (bits(x[:, hw:]) & jnp.uint32(0xFFFF0000)) | (bits(x[:, :hw]) >> 16)


def _unpack_bf16_pairs(w):
    lo = lax.bitcast_convert_type(w << 16, F32)
    hi = lax.bitcast_convert_type(w & jnp.uint32(0xFFFF0000), F32)
    return jnp.concatenate([lo, hi], axis=1).astype(BF16)


def _out_proj_rows(yc_ref, o_ref, sa_ref, sb_ref, x_ref, wc_ref, wa_ref, wo_ref, g2_ref, wr_ref, br_ref,
                   x2_ref, h2p_ref, route_ref, keys_ref, cnt_ref, *, valid_rows, half):
    tm = x_ref.shape[0]
    parts = [(r0, tm // OUT_PROJ_PARTS) for r0 in range(0, tm, tm // OUT_PROJ_PARTS)]
    for r0, n in parts:
        rows = pl.ds(r0, n)
        y_conv = jnp.dot(yc_ref[rows, :], wc_ref[...], preferred_element_type=F32)
        y_attn = jnp.dot(o_ref[rows, :], wa_ref[...], preferred_element_type=F32)
        mix = (sa_ref[rows, :].astype(F32) * y_conv + sb_ref[rows, :].astype(F32) * y_attn).astype(BF16)
        x2_ref[rows, :] = x_ref[rows, :] + jnp.dot(mix, wo_ref[...], preferred_element_type=F32)
    for r0, n in parts:
        rows = pl.ds(r0, n)
        h2 = _rms_norm_f32(x2_ref[rows, :], g2_ref[...])
        h2p_ref[rows, :] = _pack_bf16_pairs(h2)
        logits = jnp.dot(h2.astype(BF16), wr_ref[...], preferred_element_type=F32) + br_ref[...]
        words, cnt = _route_rows(logits, pl.program_id(0) * tm + r0, min(max(valid_rows - r0, 0), n), half)
        route_ref[rows, :] = words
        keys_ref[:, rows] = words.T[:TILE_ROWS, :]
        cnt_ref[...] += cnt


def _out_proj_kernel(*refs, n_first, valid_rows_second, half):
    first, second, shared = refs[0:5], refs[5:10], refs[10:]
    cnt_ref = shared[-1]
    tm = first[4].shape[0]

    @pl.when(pl.program_id(0) == 0)
    def _():
        cnt_ref[...] = jnp.zeros_like(cnt_ref)

    @pl.when(pl.program_id(0) < n_first)
    def _():
        _out_proj_rows(*first, *shared, valid_rows=tm, half=half)

    @pl.when(pl.program_id(0) >= n_first)
    def _():
        _out_proj_rows(*second, *shared, valid_rows=valid_rows_second, half=half)


def _out_proj(acts_a, acts_b, wc, wa, wo, g2, wr, br, *, tm, valid_rows_b, half):
    na = acts_a[4].shape[0] // tm
    nb = acts_b[4].shape[0] // tm
    assert nb == 1
    m_total = (na + nb) * tm
    spec_a = lambda width: pl.BlockSpec((tm, width), lambda i: (jnp.minimum(i, na - 1), 0))
    spec_b = lambda width: pl.BlockSpec((tm, width), lambda i: (jnp.maximum(i - na, 0), 0))
    widths = (D_CONV, Q_DIM, D_MODEL, D_MODEL, D_MODEL)
    in_specs = [spec_a(w) for w in widths] + [spec_b(w) for w in widths]
    in_specs += [_const_spec(wc.shape), _const_spec(wa.shape), _const_spec(wo.shape),
                 _const_spec(g2.shape), _const_spec(wr.shape), _const_spec(br.shape)]
    orow = lambda width: pl.BlockSpec((tm, width), lambda i: (i, 0))
    return pl.pallas_call(
        functools.partial(_out_proj_kernel, n_first=na, valid_rows_second=valid_rows_b, half=half),
        grid=(na + nb,),
        in_specs=in_specs,
        out_specs=[orow(D_MODEL), orow(D_MODEL // 2), orow(ROUTER_LANES),
                   pl.BlockSpec((TILE_ROWS, tm), lambda i: (0, i)),
                   pl.BlockSpec((1, ROUTER_LANES), lambda i: (0, 0))],
        out_shape=[jax.ShapeDtypeStruct((m_total, D_MODEL), F32),
                   jax.ShapeDtypeStruct((m_total, D_MODEL // 2), jnp.uint32),
                   jax.ShapeDtypeStruct((m_total, ROUTER_LANES), jnp.int32),
                   jax.ShapeDtypeStruct((TILE_ROWS, m_total), jnp.int32),
                   jax.ShapeDtypeStruct((1, ROUTER_LANES), F32)],
        compiler_params=pltpu.CompilerParams(dimension_semantics=("arbitrary",),
                                             vmem_limit_bytes=V7X_VMEM_LIMIT_BYTES),
        name="out_proj",
    )(*acts_a, *acts_b, wc, wa, wo, g2, wr, br)


def _block_table_kernel(counts_ref, bexp_ref, bpos_ref, bcnt_ref, bslot_ref, bnext_ref, nused_ref, first_ref,
                        *, n_blocks):
    def per_expert(e, carry):
        blk0, pos0, ordinal = carry
        cnt = counts_ref[e]
        nblk = lax.shift_right_logical(cnt + (MOE_BLOCK - 1), MOE_BLOCK_LOG2)
        first_ref[e] = jnp.where(nblk > 0, blk0, -1)

        def mark(b, c):
            off = (b - blk0) * MOE_BLOCK
            bexp_ref[b] = e
            bpos_ref[b] = pos0 + off
            bcnt_ref[b] = jnp.minimum(cnt - off, MOE_BLOCK)
            bslot_ref[b] = jnp.where(b == blk0, ordinal & 1, -1)
            bnext_ref[b] = -1
            return c
        lax.fori_loop(blk0, blk0 + nblk, mark, 0)
        return blk0 + nblk, pos0 + cnt, ordinal + jnp.where(nblk > 0, 1, 0)

    n_used, _, _ = lax.fori_loop(0, N_EXPERTS, per_expert, (0, 0, 0))
    nused_ref[0] = n_used

    def unused(b, c):
        bexp_ref[b] = N_EXPERTS - 1
        bpos_ref[b] = 0
        bcnt_ref[b] = 0
        bslot_ref[b] = -1
        bnext_ref[b] = -1
        return c
    lax.fori_loop(n_used, n_blocks, unused, 0)

    def link(k, nxt):
        e = N_EXPERTS - 1 - k
        fb = first_ref[e]

        @pl.when(fb >= 0)
        def _():
            bnext_ref[fb] = nxt
        return jnp.where(fb >= 0, e, nxt)
    lax.fori_loop(0, N_EXPERTS, link, -1)


def _block_tables(counts, n_blocks):
    smem = pl.BlockSpec(memory_space=pltpu.SMEM)
    blk = jax.ShapeDtypeStruct((n_blocks,), jnp.int32)
    return pl.pallas_call(
        functools.partial(_block_table_kernel, n_blocks=n_blocks),
        in_specs=[smem],
        out_specs=[smem] * 6,
        out_shape=[blk] * 5 + [jax.ShapeDtypeStruct((1,), jnp.int32)],
        scratch_shapes=[pltpu.SMEM((N_EXPERTS,), jnp.int32)],
        name="block_tables",
    )(counts)


def _moe_kernel(tok_ref, dst_ref, bexp_ref, bslot_ref, bnext_ref, nused_ref,
                h2p_ref, wg_hbm, wu_hbm, wd_hbm, contrib_hbm,
                xs_ref, ys_ref, wgf_ref, wuf_ref, wdf_ref, wgb_ref, wub_ref, wdb_ref, ssem_ref, wsem_ref,
                *, m_total, half, n_blocks):
    step = pl.program_id(0)
    n_used = nused_ref[0]
    tile = lambda t: pl.ds(pl.multiple_of(t * TILE_ROWS, TILE_ROWS), TILE_ROWS)

    def gather(blk, slt):
        for r in range(MOE_BLOCK):
            xs_ref[slt, pl.ds(r, 1), :] = h2p_ref[pl.ds(tok_ref[blk * MOE_BLOCK + r], 1), :]

    def scatter(blk, slt, r):
        return pltpu.make_async_copy(ys_ref.at[slt, tile(r), :], contrib_hbm.at[tile(dst_ref[blk * MOE_BLOCK + r]), :],
                                     ssem_ref.at[slt])

    def scatter_wait(slt, r):
        pltpu.make_async_copy(ys_ref.at[slt, tile(r), :], contrib_hbm.at[tile(0), :], ssem_ref.at[slt]).wait()

    @pl.when(step == 0)
    def _():
        ys_ref[...] = jnp.zeros_like(ys_ref)
        gap = half - m_total
        fills = [(2 * half + s * MOE_BLOCK, MOE_BLOCK) for s in range(2)]
        fills += [(k * half + m_total, gap) for k in range(TOP_K)] if gap else []
        for start, n in fills:
            fill = pltpu.make_async_copy(ys_ref.at[0, pl.ds(0, n * TILE_ROWS), :],
                                         contrib_hbm.at[pl.ds(start * TILE_ROWS, n * TILE_ROWS), :], ssem_ref.at[0])
            fill.start()
            fill.wait()

    def weight_copies(e, s):
        return [pltpu.make_async_copy(src.at[e], dst.at[s], wsem_ref.at[s])
                for src, dst in ((wg_hbm, wgf_ref), (wu_hbm, wuf_ref), (wd_hbm, wdf_ref))]

    @pl.when(step == 0)
    def _():
        for c in weight_copies(bexp_ref[0], 0):
            c.start()
        gather(0, 0)

    def wait_scatters(slt):
        for r in range(MOE_BLOCK):
            scatter_wait(slt, r)

    def expert_weights(i):
        wslot = bslot_ref[i]

        @pl.when(wslot >= 0)
        def _():
            for c in weight_copies(0, wslot):
                c.wait()

        @pl.when(jnp.logical_and(wslot >= 0, bnext_ref[i] >= 0))
        def _():
            for c in weight_copies(bnext_ref[i], 1 - wslot):
                c.start()

        @pl.when(wslot >= 0)
        def _():
            wgb_ref[...] = wgf_ref[wslot].astype(BF16)
            wub_ref[...] = wuf_ref[wslot].astype(BF16)
            wdb_ref[...] = wdf_ref[wslot].astype(BF16)

    def run_block(i, slt):
        gather(jnp.minimum(i + 1, n_blocks - 1), 1 - slt)
        xb = _unpack_bf16_pairs(xs_ref[slt])
        gate = jnp.dot(xb, wgb_ref[...], preferred_element_type=F32)
        up = jnp.dot(xb, wub_ref[...], preferred_element_type=F32)
        hmid = (jax.nn.silu(gate) * up).astype(BF16)
        _store_token_tiles(ys_ref.at[slt], jnp.dot(hmid, wdb_ref[...], preferred_element_type=F32))
        for r in range(MOE_BLOCK):
            scatter(i, slt, r).start(priority=r % 2)

    for slt in range(2):
        i = 2 * step + slt
        active = i < n_used
        last_active = i == n_used - 1
        pl.when(jnp.logical_and(active, i >= 2))(functools.partial(wait_scatters, slt))
        expert_weights(i)
        pl.when(active)(functools.partial(run_block, i, slt))
        pl.when(last_active)(functools.partial(wait_scatters, slt))
        pl.when(jnp.logical_and(last_active, i >= 1))(functools.partial(wait_scatters, 1 - slt))


def _row_table_kernel(bpos_ref, bcnt_ref, order_ref, tok_ref, dst_ref, *, half, n_blocks):
    rows_per_block = MOE_BLOCK // LANES
    lane = lax.broadcasted_iota(jnp.int32, (rows_per_block, LANES), 1)
    r = lax.broadcasted_iota(jnp.int32, (rows_per_block, LANES), 0) * LANES + lane

    def per_block(b, c):
        p = bpos_ref[b]
        sh = p & (LANES - 1)
        win = order_ref[pl.ds(lax.shift_right_logical(p, LANES_LOG2), rows_per_block + 1), :]
        rolled = pltpu.roll(win, (LANES - sh) & (LANES - 1), axis=1)
        assign = jnp.where(lane < LANES - sh, rolled[:rows_per_block], rolled[1:])
        tok_ref[pl.ds(b * rows_per_block, rows_per_block), :] = jnp.where(assign >= half, assign - half, assign)
        trash = 2 * half + (b & 1) * MOE_BLOCK + r
        dst_ref[pl.ds(b * rows_per_block, rows_per_block), :] = jnp.where(r < bcnt_ref[b], assign, trash)
        return c
    lax.fori_loop(0, n_blocks, per_block, 0, unroll=6)


def _row_tables(order, bpos, bcnt, half):
    n_blocks = bpos.shape[0]
    assert MOE_BLOCK % LANES == 0 and order.shape[0] % LANES == 0
    out = jax.ShapeDtypeStruct((n_blocks * MOE_BLOCK // LANES, LANES), jnp.int32)
    tok, dst = pl.pallas_call(
        functools.partial(_row_table_kernel, half=half, n_blocks=n_blocks),
        grid_spec=pltpu.PrefetchScalarGridSpec(
            num_scalar_prefetch=2, grid=(1,),
            in_specs=[pl.BlockSpec((order.shape[0] // LANES, LANES), lambda i, *_: (0, 0))],
            out_specs=[pl.BlockSpec(out.shape, lambda i, *_: (0, 0))] * 2),
        out_shape=[out, out],
        name="row_tables",
    )(bpos, bcnt, order.reshape(-1, LANES))
    return tok.reshape(-1), dst.reshape(-1)


def _moe_experts(order, tables, h2p, w_gate, w_up, w_down, *, m_total, half):
    bexp, bpos, bcnt, bslot, bnext, n_used = tables
    n_blocks = bexp.shape[0]
    assert 0 <= half - m_total <= MOE_BLOCK and n_blocks % 2 == 0
    tok, dst = _row_tables(order, bpos, bcnt, half)
    tables = (bexp, bslot, bnext, n_used)
    hbm = pl.BlockSpec(memory_space=pl.ANY)
    grid_spec = pltpu.PrefetchScalarGridSpec(
        num_scalar_prefetch=2 + len(tables),
        grid=(n_blocks // 2,),
        in_specs=[_const_spec(h2p.shape), hbm, hbm, hbm],
        out_specs=hbm,
        scratch_shapes=[pltpu.VMEM((2, MOE_BLOCK, D_MODEL // 2), jnp.uint32),
                        pltpu.VMEM((2, MOE_BLOCK * TILE_ROWS, LANES), F32),
                        pltpu.VMEM((2, D_MODEL, D_EXPERT), F32),
                        pltpu.VMEM((2, D_MODEL, D_EXPERT), F32),
                        pltpu.VMEM((2, D_EXPERT, D_MODEL), F32),
                        pltpu.VMEM((D_MODEL, D_EXPERT), BF16),
                        pltpu.VMEM((D_MODEL, D_EXPERT), BF16),
                        pltpu.VMEM((D_EXPERT, D_MODEL), BF16),
                        pltpu.SemaphoreType.DMA((2,)),
                        pltpu.SemaphoreType.DMA((2,))],
    )
    return pl.pallas_call(
        functools.partial(_moe_kernel, m_total=m_total, half=half, n_blocks=n_blocks),
        grid_spec=grid_spec,
        out_shape=jax.ShapeDtypeStruct(((2 * half + 2 * MOE_BLOCK) * TILE_ROWS, LANES), F32),
        compiler_params=pltpu.CompilerParams(dimension_semantics=("arbitrary",),
                                             vmem_limit_bytes=V7X_VMEM_LIMIT_LARGE_BYTES),
        name="moe_experts",
    )(tok, dst, *tables, h2p, w_gate, w_up, w_down)


def _combine_kernel(c0_ref, c1_ref, x2_ref, route_ref, gf_ref, y_ref):
    tc = x2_ref.shape[0]
    w = lax.bitcast_convert_type(route_ref[:, TOP_K:2 * TOP_K], F32)
    moe = w[:, 0:1] * _load_token_tiles(c0_ref, tc) + w[:, 1:2] * _load_token_tiles(c1_ref, tc)
    y_ref[...] = _rms_norm_f32(x2_ref[...] + moe, gf_ref[...])


def _moe_combine(contrib, x2, route, gf, *, row_off, m, tc, half):
    off = row_off // tc
    assert row_off % tc == 0 and half % tc == 0
    ctile = lambda k: pl.BlockSpec((tc * TILE_ROWS, LANES), lambda i: (i + off + k * (half // tc), 0))
    return pl.pallas_call(
        _combine_kernel,
        grid=(m // tc,),
        in_specs=[ctile(0), ctile(1),
                  pl.BlockSpec((tc, D_MODEL), lambda i: (i + off, 0)),
                  pl.BlockSpec((tc, ROUTER_LANES), lambda i: (i + off, 0)),
                  _const_spec((1, D_MODEL))],
        out_specs=pl.BlockSpec((tc, D_MODEL), lambda i: (i, 0)),
        out_shape=jax.ShapeDtypeStruct((m, D_MODEL), F32),
        compiler_params=pltpu.CompilerParams(dimension_semantics=("arbitrary",),
                                             vmem_limit_bytes=V7X_VMEM_LIMIT_BYTES),
        name="moe_combine_prompt" if row_off == 0 else "moe_combine_sample",
    )(contrib, contrib, x2, route, gf)


def _t5_bucket(dist):
    n = jnp.maximum(dist, 0)
    max_exact = N_BUCKETS // 2
    nf = jnp.maximum(n, 1).astype(F32)
    large = max_exact + (jnp.log(nf / max_exact) / math.log(MAX_DISTANCE / max_exact)
                         * (N_BUCKETS - max_exact)).astype(jnp.int32)
    large = jnp.minimum(large, N_BUCKETS - 1)
    return jnp.where(n < max_exact, n, large)


def _bucket_bias(rel_bias, dist, valid):
    buckets = _t5_bucket(dist).reshape(1, -1)
    onehot = (buckets == jnp.arange(N_BUCKETS, dtype=jnp.int32)[:, None]).astype(F32)
    bias = jnp.dot(rel_bias.astype(F32).T, onehot, precision=lax.Precision.HIGHEST)
    return jnp.where(valid.reshape(1, -1), bias, NEG_BIG).reshape((rel_bias.shape[1],) + dist.shape)


def _prompt_bias_table(rel_bias):
    qi = jnp.arange(ATTN_BLOCK, dtype=jnp.int32)[:, None]
    kj = jnp.arange(2 * ATTN_BLOCK, dtype=jnp.int32)[None, :] - ATTN_BLOCK
    dist = qi - kj
    return _bucket_bias(rel_bias, dist, (dist >= 0) & (dist <= WINDOW))


def _sample_bias_table(rel_bias, w_buf):
    dist = w_buf - jnp.arange(w_buf + 1, dtype=jnp.int32)
    return _bucket_bias(rel_bias, dist, dist <= WINDOW)


def kernel(x_prompt, x_sample, cache_conv, cache_k, cache_v, norm1_g, w_in, conv_w, w_conv_out, w_attn_out, w_o, sinks, rel_bias, norm2_g, w_router_group, b_router_group, w_router_expert, b_router_expert, w_e_gate, w_e_up, w_e_down, norm_f_g):
    assert norm1_g.shape[0] == 1, "single-layer configuration"
    batch, seq, _ = x_prompt.shape
    nseq = x_sample.shape[0]
    w_buf = cache_k.shape[2]
    mp = batch * seq
    m_total = mp + nseq
    assert seq % TM_DENSE == 0 and seq % TM_IN_PROJ == 0 and seq % ATTN_BLOCK == 0 and mp % COMBINE_BLOCK == 0
    assert nseq % SAMPLE_SEQ_PER_STEP == 0 and mp % nseq == 0
    assert TOP_K == 2 and MOE_BLOCK == 1 << MOE_BLOCK_LOG2 and m_total * TOP_K < 1 << ASSIGN_BITS
    assert CONV_WIDTH == 3 and LANES == 1 << LANES_LOG2 and x_sample.shape[1] == 1 and w_buf == WINDOW

    g1 = norm1_g[0][None, :]
    g2 = norm2_g[0][None, :]
    gf = norm_f_g[None, :]
    wi = w_in[0].astype(BF16)
    cw = conv_w[0]
    wc = w_conv_out[0].astype(BF16)
    wa = w_attn_out[0].astype(BF16)
    wo = w_o[0].astype(BF16)
    pad_cols = ROUTER_LANES - N_EXPERT_GROUPS - N_EXPERTS
    wr = jnp.concatenate([w_router_group[0], w_router_expert[0],
                          jnp.zeros((D_MODEL, pad_cols), F32)], axis=1).astype(BF16)
    br = jnp.concatenate([b_router_group[0], b_router_expert[0], jnp.zeros((pad_cols,), F32)])[None, :]
    sink = sinks[0].astype(F32)

    xp = x_prompt.reshape(mp, D_MODEL)
    bps = seq // TM_IN_PROJ
    yc_p, q_p, k_p, v_p, sa_p, sb_p, ut_p, kvt_p = _in_proj(
        xp, g1, wi, cw, tm=TM_IN_PROJ, blocks_per_seq=bps, u_tail=CARRY, kv_tail=WINDOW, parts=IN_PROJ_PARTS)
    o_p = _attn_prompt(q_p, k_p, v_p, _prompt_bias_table(rel_bias), sink, batch, seq)

    pad_rows = lambda t: jnp.pad(t, ((0, TM_DENSE - nseq), (0, 0)))
    xs = pad_rows(x_sample.reshape(nseq, D_MODEL))
    hist = (pad_rows(cache_conv[0][:, 0, :]), pad_rows(cache_conv[0][:, 1, :]))
    yc_s, q_s, _, _, sa_s, sb_s, ut_s, kvt_s = _in_proj(
        xs, g1, wi, cw, tm=TM_DENSE, blocks_per_seq=1, u_tail=TM_DENSE, kv_tail=TM_DENSE, hist=hist,
        gate_dtype=F32)
    u_s = ut_s[0, :nseq]
    kv_s = kvt_s[0, :nseq]
    head_mask = (jnp.arange(KV_DIM)[None, :] // HEAD_DIM == jnp.arange(N_HEADS)[:, None] // GROUP)
    qbd = (jnp.tile(q_s[:nseq].reshape(nseq, N_HEADS, HEAD_DIM), (1, 1, N_KV_HEADS))
           * head_mask[None].astype(BF16))
    to_keys_minor = lambda c: jnp.transpose(c.reshape(nseq, w_buf, KV_DIM), (0, 2, 1))
    from_keys_minor = lambda c: jnp.transpose(c, (0, 2, 1)).reshape(1, nseq, w_buf, N_KV_HEADS, HEAD_DIM)
    o_s, kwin_s, vwin_s = _attn_sample(qbd, to_keys_minor(cache_k[0]), to_keys_minor(cache_v[0]), kv_s,
                                       _sample_bias_table(rel_bias, w_buf), sink[:, None],
                                       head_mask.astype(F32))
    o_s = pad_rows(o_s.reshape(nseq, Q_DIM))

    half = -(-m_total // COMBINE_BLOCK) * COMBINE_BLOCK
    assert half % nseq == 0 and TOP_K * half < 1 << ASSIGN_BITS
    x2, h2p, route, route_t, cnt = _out_proj((yc_p, o_p, sa_p, sb_p, xp), (yc_s, o_s, sa_s, sb_s, xs),
                                             wc, wa, wo, g2, wr, br, tm=TM_DENSE, valid_rows_b=nseq, half=half)

    n_assign = m_total * TOP_K
    keys = route_t[0:TOP_K, :m_total].reshape(-1)
    counts = cnt[0, N_EXPERT_GROUPS:N_EXPERT_GROUPS + N_EXPERTS].astype(jnp.int32)
    order = jnp.pad(jnp.sort(keys) & ((1 << ASSIGN_BITS) - 1), (0, MOE_BLOCK))
    n_blocks = -(-n_assign // MOE_BLOCK) + N_EXPERTS
    n_blocks += n_blocks % 2
    tables = _block_tables(counts, n_blocks)
    contrib = _moe_experts(order, tables, h2p, w_e_gate[0], w_e_up[0], w_e_down[0],
                           m_total=m_total, half=half)
    y_p = _moe_combine(contrib, x2, route, gf, row_off=0, m=mp, tc=COMBINE_BLOCK, half=half)
    y_s = _moe_combine(contrib, x2, route, gf, row_off=mp, m=nseq, tc=nseq, half=half)

    y_prompt = y_p.reshape(batch, seq, D_MODEL)
    y_sample = y_s.reshape(nseq, 1, D_MODEL)
    conv_state_prompt = ut_p.reshape(batch, bps, CARRY, D_CONV)[:, -1, CARRY - (CONV_WIDTH - 1):, :][None]
    kv_last = kvt_p.reshape(batch, bps, WINDOW, 2 * KV_DIM)[:, -1]
    k_win_prompt = kv_last[:, :, :KV_DIM].reshape(batch, WINDOW, N_KV_HEADS, HEAD_DIM)[None]
    v_win_prompt = kv_last[:, :, KV_DIM:].reshape(batch, WINDOW, N_KV_HEADS, HEAD_DIM)[None]
    conv_state_sample = jnp.concatenate([cache_conv[0][:, 1:, :], u_s[:, None, :]], axis=1)[None]
    k_win_sample = from_keys_minor(kwin_s)
    v_win_sample = from_keys_minor(vwin_s)
    return (y_prompt, y_sample, conv_state_prompt, k_win_prompt, v_win_prompt,
            conv_state_sample, k_win_sample, v_win_sample)
```

```python
import functools
import math

import jax
import jax.numpy as jnp
from jax import lax
from jax.experimental import pallas as pl
from jax.experimental.pallas import tpu as pltpu

D_MODEL = 1024
D_CONV = 1024
CONV_WIDTH = 3
N_HEADS = 16
N_KV_HEADS = 4
HEAD_DIM = 64
GROUP = N_HEADS // N_KV_HEADS
WINDOW = 128
Q_DIM = N_HEADS * HEAD_DIM
KV_DIM = N_KV_HEADS * HEAD_DIM
N_BUCKETS = 32
MAX_DISTANCE = 128
N_EXPERT_GROUPS = 4
EXPERTS_PER_GROUP = 8
N_EXPERTS = N_EXPERT_GROUPS * EXPERTS_PER_GROUP
TOP_K = 2
D_EXPERT = 512
EPS = 1e-6

BF16 = jnp.bfloat16
F32 = jnp.float32
NEG_BIG = -1e30

V7X_VMEM_LIMIT_BYTES = 56 * 1024 * 1024
V7X_VMEM_LIMIT_LARGE_BYTES = 62 * 1024 * 1024
TILE_ROWS = 8
LANES = 128
LANES_LOG2 = 7
CARRY = TILE_ROWS
ROUTER_LANES = 128
TM_DENSE = 512
TM_IN_PROJ = 1024
OUT_PROJ_PARTS = 2
IN_PROJ_PARTS = 4
ATTN_BLOCK = 128
MOE_BLOCK = 256
MOE_BLOCK_LOG2 = 8
ASSIGN_BITS = 16
COMBINE_BLOCK = 128
SAMPLE_SEQ_PER_STEP = 16
HEADS_PER_STORE = LANES // HEAD_DIM


def _const_spec(shape):
    nd = len(shape)
    return pl.BlockSpec(shape, lambda *_: (0,) * nd, pipeline_mode=pl.Buffered(1))


def _rms_norm_f32(xf, g):
    return xf * lax.rsqrt(jnp.mean(xf * xf, axis=-1, keepdims=True) + EPS) * g


def _in_proj_kernel(*refs, tm, sample, blocks_per_seq, u_tail, kv_tail, parts):
    if sample:
        (x_ref, hist0_ref, hist1_ref, g_ref, w_ref,
         cw_ref, yc_ref, q_ref, k_ref, v_ref, sa_ref, sb_ref, ut_ref, kvt_ref) = refs
    else:
        (x_ref, g_ref, w_ref,
         cw_ref, yc_ref, q_ref, k_ref, v_ref, sa_ref, sb_ref, ut_ref, kvt_ref, ubuf_ref) = refs

    widths = (D_CONV, D_CONV, D_CONV, Q_DIM, 2 * KV_DIM, D_MODEL, D_MODEL)
    starts = [sum(widths[:n]) for n in range(len(widths))]
    wcb_ref, wcc_ref, wch_ref, wq_ref, wkv_ref, wga_ref, wgb_ref = [
        w_ref.at[:, pl.ds(a, n)] for a, n in zip(starts, widths)]
    w0 = cw_ref[0:1, :]
    w1 = cw_ref[1:2, :]
    w2 = cw_ref[2:3, :]

    if not sample:
        @pl.when(pl.program_id(0) % blocks_per_seq == 0)
        def _():
            ubuf_ref[0:CARRY, :] = jnp.zeros((CARRY, D_CONV), F32)

    n = tm // parts
    assert u_tail <= n and kv_tail <= n
    for r0 in range(0, tm, n):
        rows = pl.ds(r0, n)
        last = r0 + n == tm
        h = _rms_norm_f32(x_ref[rows, :], g_ref[...]).astype(BF16)

        def proj(part_ref, h=h):
            return jnp.dot(h, part_ref[...], preferred_element_type=F32)

        u = proj(wcc_ref) * proj(wch_ref)
        if sample:
            conv = w0 * hist0_ref[rows, :] + w1 * hist1_ref[rows, :] + w2 * u
        else:
            ubuf_ref[pl.ds(CARRY + r0, n), :] = u
            conv = (w0 * ubuf_ref[pl.ds(CARRY - 2 + r0, n), :] + w1 * ubuf_ref[pl.ds(CARRY - 1 + r0, n), :]
                    + w2 * u)
            if last:
                ubuf_ref[0:CARRY, :] = u[n - CARRY:, :]
        yc_ref[rows, :] = (proj(wcb_ref) * conv).astype(BF16)
        q_ref[rows, :] = (proj(wq_ref) * (HEAD_DIM ** -0.5)).astype(BF16)
        kv = proj(wkv_ref)
        k_ref[rows, :] = kv[:, :KV_DIM].astype(BF16)
        v_ref[rows, :] = kv[:, KV_DIM:].astype(BF16)
        if last:
            ut_ref[0] = u[n - u_tail:, :]
            kvt_ref[0] = kv[n - kv_tail:, :]
        sa_ref[rows, :] = jax.nn.sigmoid(proj(wga_ref)).astype(sa_ref.dtype)
        sb_ref[rows, :] = jax.nn.sigmoid(proj(wgb_ref)).astype(sb_ref.dtype)


def _in_proj(x, g1, w_in, conv_w, *, tm, blocks_per_seq, u_tail, kv_tail, hist=None, gate_dtype=BF16, parts=1):
    m = x.shape[0]
    nblk = m // tm
    sample = hist is not None
    row = lambda width: pl.BlockSpec((tm, width), lambda i: (i, 0))
    in_specs = [row(D_MODEL)]
    args = [x]
    if sample:
        in_specs += [row(D_CONV), row(D_CONV)]
        args += list(hist)
    in_specs += [_const_spec((1, D_MODEL)), _const_spec(w_in.shape), _const_spec(conv_w.shape)]
    args += [g1, w_in, conv_w]
    out_shape = [
        jax.ShapeDtypeStruct((m, D_CONV), BF16),
        jax.ShapeDtypeStruct((m, Q_DIM), BF16),
        jax.ShapeDtypeStruct((m, KV_DIM), BF16),
        jax.ShapeDtypeStruct((m, KV_DIM), BF16),
        jax.ShapeDtypeStruct((m, D_MODEL), gate_dtype),
        jax.ShapeDtypeStruct((m, D_MODEL), gate_dtype),
        jax.ShapeDtypeStruct((nblk, u_tail, D_CONV), F32),
        jax.ShapeDtypeStruct((nblk, kv_tail, 2 * KV_DIM), F32),
    ]
    out_specs = [row(D_CONV), row(Q_DIM), row(KV_DIM), row(KV_DIM), row(D_MODEL), row(D_MODEL),
                 pl.BlockSpec((1, u_tail, D_CONV), lambda i: (i, 0, 0)),
                 pl.BlockSpec((1, kv_tail, 2 * KV_DIM), lambda i: (i, 0, 0))]
    scratch = [] if sample else [pltpu.VMEM((tm + CARRY, D_CONV), F32)]
    return pl.pallas_call(
        functools.partial(_in_proj_kernel, tm=tm, sample=sample, blocks_per_seq=blocks_per_seq,
                          u_tail=u_tail, kv_tail=kv_tail, parts=parts),
        grid=(nblk,),
        in_specs=in_specs,
        out_specs=out_specs,
        out_shape=out_shape,
        scratch_shapes=scratch,
        compiler_params=pltpu.CompilerParams(dimension_semantics=("arbitrary",),
                                             vmem_limit_bytes=V7X_VMEM_LIMIT_LARGE_BYTES),
        name="in_proj_sample" if sample else "in_proj_prompt",
    )(*args)


def _attn_prompt_kernel(sink_ref, q_ref, kc_ref, kp_ref, vc_ref, vp_ref, bias_ref, o_ref):
    first = pl.program_id(1) == 0
    col = lax.broadcasted_iota(jnp.int32, (ATTN_BLOCK, 2 * ATTN_BLOCK), 1)
    no_prev = jnp.logical_and(first, col < ATTN_BLOCK)
    for g in range(N_KV_HEADS):
        ks = slice(g * HEAD_DIM, (g + 1) * HEAD_DIM)
        kcat = jnp.concatenate([kp_ref[:, ks], kc_ref[:, ks]], axis=0)
        vcat = jnp.concatenate([vp_ref[:, ks], vc_ref[:, ks]], axis=0)
        for h0 in range(g * GROUP, (g + 1) * GROUP, HEADS_PER_STORE):
            outs = []
            for h in range(h0, h0 + HEADS_PER_STORE):
                hs = slice(h * HEAD_DIM, (h + 1) * HEAD_DIM)
                s = lax.dot_general(q_ref[:, hs], kcat, (((1,), (1,)), ((), ())),
                                    preferred_element_type=F32)
                s = jnp.where(no_prev, NEG_BIG, s + bias_ref[h])
                sink = sink_ref[h]
                m = jnp.maximum(jnp.max(s, axis=-1, keepdims=True), sink)
                p = jnp.exp(s - m)
                denom = jnp.sum(p, axis=-1, keepdims=True) + jnp.exp(sink - m)
                o = jnp.dot(p.astype(BF16), vcat, preferred_element_type=F32)
                outs.append((o / denom).astype(BF16))
            o_ref[:, h0 * HEAD_DIM:(h0 + HEADS_PER_STORE) * HEAD_DIM] = jnp.concatenate(outs, axis=1)


def _attn_prompt(q, k, v, bias, sinks, batch, seq):
    nb = seq // ATTN_BLOCK
    cur = lambda b, i: (b * nb + i, 0)
    prev = lambda b, i: (b * nb + jnp.maximum(i - 1, 0), 0)
    return pl.pallas_call(
        _attn_prompt_kernel,
        grid=(batch, nb),
        in_specs=[pl.BlockSpec(memory_space=pltpu.SMEM),
                  pl.BlockSpec((ATTN_BLOCK, Q_DIM), cur),
                  pl.BlockSpec((ATTN_BLOCK, KV_DIM), cur),
                  pl.BlockSpec((ATTN_BLOCK, KV_DIM), prev),
                  pl.BlockSpec((ATTN_BLOCK, KV_DIM), cur),
                  pl.BlockSpec((ATTN_BLOCK, KV_DIM), prev),
                  _const_spec(bias.shape)],
        out_specs=pl.BlockSpec((ATTN_BLOCK, Q_DIM), cur),
        out_shape=jax.ShapeDtypeStruct((batch * seq, Q_DIM), BF16),
        compiler_params=pltpu.CompilerParams(dimension_semantics=("arbitrary", "arbitrary"),
                                             vmem_limit_bytes=V7X_VMEM_LIMIT_BYTES),
        name="attn_prompt",
    )(sinks, q, k, k, v, v, bias)


def _attn_sample_kernel(qbd_ref, ckt_ref, cvt_ref, kvn_ref, bias_ref, sink_ref, mask_ref,
                        o_ref, kwin_ref, vwin_ref, *, w_buf):
    bf16_round = lambda t: t.astype(BF16).astype(F32)
    seqs = range(SAMPLE_SEQ_PER_STEP)
    sink = sink_ref[...]
    newest = lax.broadcasted_iota(jnp.int32, (KV_DIM, w_buf), 1) == w_buf - 1
    kvn_t = kvn_ref[...].T
    for b in seqs:
        kwin_ref[b] = jnp.where(newest, kvn_t[:KV_DIM, b:b + 1], pltpu.roll(ckt_ref[b], w_buf - 1, axis=1))
        vwin_ref[b] = jnp.where(newest, kvn_t[KV_DIM:, b:b + 1], pltpu.roll(cvt_ref[b], w_buf - 1, axis=1))
    s = [jnp.dot(qbd_ref[b], ckt_ref[b].astype(BF16), preferred_element_type=F32) + bias_ref[:, :w_buf]
         for b in seqs]
    s_new = [jnp.sum(qbd_ref[b].astype(F32) * bf16_round(kvn_ref[b:b + 1, :KV_DIM]), axis=-1, keepdims=True)
             + bias_ref[:, w_buf:w_buf + 1] for b in seqs]
    m = [jnp.maximum(jnp.maximum(jnp.max(s[b], axis=-1, keepdims=True), s_new[b]), sink) for b in seqs]
    p = [jnp.exp(s[b] - m[b]) for b in seqs]
    p_new = [jnp.exp(s_new[b] - m[b]) for b in seqs]
    denom = [jnp.sum(p[b], axis=-1, keepdims=True) + p_new[b] + jnp.exp(sink - m[b]) for b in seqs]
    of = [lax.dot_general((p[b] / denom[b]).astype(BF16), cvt_ref[b].astype(BF16), (((1,), (1,)), ((), ())),
                          preferred_element_type=F32)
          + bf16_round(p_new[b] / denom[b]) * bf16_round(kvn_ref[b:b + 1, KV_DIM:]) for b in seqs]
    for b in seqs:
        ob = of[b] * mask_ref[...]
        o_ref[b] = (ob[:, 0:HEAD_DIM] + ob[:, HEAD_DIM:2 * HEAD_DIM]
                    + ob[:, 2 * HEAD_DIM:3 * HEAD_DIM] + ob[:, 3 * HEAD_DIM:]).astype(BF16)


def _attn_sample(qbd, ckt, cvt, kvn, bias, sink_col, head_mask):
    nseq, w_buf = ckt.shape[0], ckt.shape[2]
    sb = SAMPLE_SEQ_PER_STEP
    seq3 = lambda d1, d2: pl.BlockSpec((sb, d1, d2), lambda i: (i, 0, 0))
    win = jax.ShapeDtypeStruct((nseq, KV_DIM, w_buf), F32)
    return pl.pallas_call(
        functools.partial(_attn_sample_kernel, w_buf=w_buf),
        grid=(nseq // sb,),
        in_specs=[seq3(N_HEADS, KV_DIM), seq3(KV_DIM, w_buf), seq3(KV_DIM, w_buf),
                  pl.BlockSpec((sb, 2 * KV_DIM), lambda i: (i, 0)),
                  _const_spec(bias.shape), _const_spec(sink_col.shape), _const_spec(head_mask.shape)],
        out_specs=[seq3(N_HEADS, HEAD_DIM), seq3(KV_DIM, w_buf), seq3(KV_DIM, w_buf)],
        out_shape=[jax.ShapeDtypeStruct((nseq, N_HEADS, HEAD_DIM), BF16), win, win],
        compiler_params=pltpu.CompilerParams(dimension_semantics=("arbitrary",),
                                             vmem_limit_bytes=V7X_VMEM_LIMIT_BYTES),
        name="attn_sample",
    )(qbd, ckt, cvt, kvn, bias, sink_col, head_mask)


def _route_rows(logits, row0, valid_rows, half):
    tm = logits.shape[0]
    lane = lax.broadcasted_iota(jnp.int32, logits.shape, 1)
    lane_f = lane.astype(F32)
    no_lane = float(ROUTER_LANES)

    def top1(mask):
        best = jnp.max(jnp.where(mask, logits, -jnp.inf), axis=-1, keepdims=True)
        idx = jnp.min(jnp.where(jnp.logical_and(mask, logits == best), lane_f, no_lane), axis=-1, keepdims=True)
        return best, idx

    gmask = lane < N_EXPERT_GROUPS
    gmax, grp = top1(gmask)
    gsum = jnp.sum(jnp.where(gmask, jnp.exp(logits - gmax), 0.0), axis=-1, keepdims=True)
    p_grp = 1.0 / gsum
    lo = N_EXPERT_GROUPS + EXPERTS_PER_GROUP * grp
    emask = jnp.logical_and(lane_f >= lo, lane_f < lo + EXPERTS_PER_GROUP)
    v1, i1 = top1(emask)
    v2, i2 = top1(jnp.logical_and(emask, lane_f != i1))
    e21 = jnp.exp(v2 - v1)
    w1 = p_grp / (1.0 + e21)
    w2 = p_grp * e21 / (1.0 + e21)

    oh1 = lane_f == i1
    oh2 = lane_f == i2
    if valid_rows < tm:
        valid = lax.broadcasted_iota(jnp.int32, logits.shape, 0) < valid_rows
        oh1 = jnp.logical_and(oh1, valid)
        oh2 = jnp.logical_and(oh2, valid)
    oh = oh1.astype(F32) + oh2.astype(F32)
    token = row0 + lax.broadcasted_iota(jnp.int32, (tm, 1), 0)
    key1 = (i1.astype(jnp.int32) - N_EXPERT_GROUPS) * (1 << ASSIGN_BITS) + token
    key2 = (i2.astype(jnp.int32) - N_EXPERT_GROUPS) * (1 << ASSIGN_BITS) + token + half
    w1b = lax.bitcast_convert_type(w1, jnp.int32)
    w2b = lax.bitcast_convert_type(w2, jnp.int32)
    words = jnp.where(lane == 0, key1, jnp.where(lane == 1, key2, jnp.where(lane == 2, w1b,
                      jnp.where(lane == 3, w2b, 0))))
    return words, jnp.sum(oh, axis=0, keepdims=True)


def _store_token_tiles(ref, x):
    n = x.shape[0]
    for c in range(D_MODEL // LANES):
        ref[pl.ds(c, n, stride=TILE_ROWS), :] = x[:, c * LANES:(c + 1) * LANES]


def _load_token_tiles(ref, n):
    return jnp.concatenate([ref[pl.ds(c, n, stride=TILE_ROWS), :] for c in range(D_MODEL // LANES)], axis=1)


def _pack_bf16_pairs(x):
    hw = x.shape[1] // 2
    bits = lambda v: lax.bitcast_convert_type(v.astype(BF16).astype(F32), jnp.uint32)
    return (bits(x[:, hw:]) & jnp.uint32(0xFFFF0000)) | (bits(x[:, :hw]) >> 16)


def _unpack_bf16_pairs(w):
    lo = lax.bitcast_convert_type(w << 16, F32)
    hi = lax.bitcast_convert_type(w & jnp.uint32(0xFFFF0000), F32)
    return jnp.concatenate([lo, hi], axis=1).astype(BF16)


def _out_proj_rows(yc_ref, o_ref, sa_ref, sb_ref, x_ref, wc_ref, wa_ref, wo_ref, g2_ref, wr_ref, br_ref,
                   x2_ref, h2p_ref, route_ref, keys_ref, cnt_ref, *, valid_rows, half):
    tm = x_ref.shape[0]
    parts = [(r0, tm // OUT_PROJ_PARTS) for r0 in range(0, tm, tm // OUT_PROJ_PARTS)]
    for r0, n in parts:
        rows = pl.ds(r0, n)
        y_conv = jnp.dot(yc_ref[rows, :], wc_ref[...], preferred_element_type=F32)
        y_attn = jnp.dot(o_ref[rows, :], wa_ref[...], preferred_element_type=F32)
        mix = (sa_ref[rows, :].astype(F32) * y_conv + sb_ref[rows, :].astype(F32) * y_attn).astype(BF16)
        x2_ref[rows, :] = x_ref[rows, :] + jnp.dot(mix, wo_ref[...], preferred_element_type=F32)
    for r0, n in parts:
        rows = pl.ds(r0, n)
        h2 = _rms_norm_f32(x2_ref[rows, :], g2_ref[...])
        h2p_ref[rows, :] = _pack_bf16_pairs(h2)
        logits = jnp.dot(h2.astype(BF16), wr_ref[...], preferred_element_type=F32) + br_ref[...]
        words, cnt = _route_rows(logits, pl.program_id(0) * tm + r0, min(max(valid_rows - r0, 0), n), half)
        route_ref[rows, :] = words
        keys_ref[:, rows] = words.T[:TILE_ROWS, :]
        cnt_ref[...] += cnt


def _out_proj_kernel(*refs, n_first, valid_rows_second, half):
    first, second, shared = refs[0:5], refs[5:10], refs[10:]
    cnt_ref = shared[-1]
    tm = first[4].shape[0]

    @pl.when(pl.program_id(0) == 0)
    def _():
        cnt_ref[...] = jnp.zeros_like(cnt_ref)

    @pl.when(pl.program_id(0) < n_first)
    def _():
        _out_proj_rows(*first, *shared, valid_rows=tm, half=half)

    @pl.when(pl.program_id(0) >= n_first)
    def _():
        _out_proj_rows(*second, *shared, valid_rows=valid_rows_second, half=half)


def _out_proj(acts_a, acts_b, wc, wa, wo, g2, wr, br, *, tm, valid_rows_b, half):
    na = acts_a[4].shape[0] // tm
    nb = acts_b[4].shape[0] // tm
    assert nb == 1
    m_total = (na + nb) * tm
    spec_a = lambda width: pl.BlockSpec((tm, width), lambda i: (jnp.minimum(i, na - 1), 0))
    spec_b = lambda width: pl.BlockSpec((tm, width), lambda i: (jnp.maximum(i - na, 0), 0))
    widths = (D_CONV, Q_DIM, D_MODEL, D_MODEL, D_MODEL)
    in_specs = [spec_a(w) for w in widths] + [spec_b(w) for w in widths]
    in_specs += [_const_spec(wc.shape), _const_spec(wa.shape), _const_spec(wo.shape),
                 _const_spec(g2.shape), _const_spec(wr.shape), _const_spec(br.shape)]
    orow = lambda width: pl.BlockSpec((tm, width), lambda i: (i, 0))
    return pl.pallas_call(
        functools.partial(_out_proj_kernel, n_first=na, valid_rows_second=valid_rows_b, half=half),
        grid=(na + nb,),
        in_specs=in_specs,
        out_specs=[orow(D_MODEL), orow(D_MODEL // 2), orow(ROUTER_LANES),
                   pl.BlockSpec((TILE_ROWS, tm), lambda i: (0, i)),
                   pl.BlockSpec((1, ROUTER_LANES), lambda i: (0, 0))],
        out_shape=[jax.ShapeDtypeStruct((m_total, D_MODEL), F32),
                   jax.ShapeDtypeStruct((m_total, D_MODEL // 2), jnp.uint32),
                   jax.ShapeDtypeStruct((m_total, ROUTER_LANES), jnp.int32),
                   jax.ShapeDtypeStruct((TILE_ROWS, m_total), jnp.int32),
                   jax.ShapeDtypeStruct((1, ROUTER_LANES), F32)],
        compiler_params=pltpu.CompilerParams(dimension_semantics=("arbitrary",),
                                             vmem_limit_bytes=V7X_VMEM_LIMIT_BYTES),
        name="out_proj",
    )(*acts_a, *acts_b, wc, wa, wo, g2, wr, br)


def _block_table_kernel(counts_ref, bexp_ref, bpos_ref, bcnt_ref, bslot_ref, bnext_ref, nused_ref, first_ref,
                        *, n_blocks):
    def per_expert(e, carry):
        blk0, pos0, ordinal = carry
        cnt = counts_ref[e]
        nblk = lax.shift_right_logical(cnt + (MOE_BLOCK - 1), MOE_BLOCK_LOG2)
        first_ref[e] = jnp.where(nblk > 0, blk0, -1)

        def mark(b, c):
            off = (b - blk0) * MOE_BLOCK
            bexp_ref[b] = e
            bpos_ref[b] = pos0 + off
            bcnt_ref[b] = jnp.minimum(cnt - off, MOE_BLOCK)
            bslot_ref[b] = jnp.where(b == blk0, ordinal & 1, -1)
            bnext_ref[b] = -1
            return c
        lax.fori_loop(blk0, blk0 + nblk, mark, 0)
        return blk0 + nblk, pos0 + cnt, ordinal + jnp.where(nblk > 0, 1, 0)

    n_used, _, _ = lax.fori_loop(0, N_EXPERTS, per_expert, (0, 0, 0))
    nused_ref[0] = n_used

    def unused(b, c):
        bexp_ref[b] = N_EXPERTS - 1
        bpos_ref[b] = 0
        bcnt_ref[b] = 0
        bslot_ref[b] = -1
        bnext_ref[b] = -1
        return c
    lax.fori_loop(n_used, n_blocks, unused, 0)

    def link(k, nxt):
        e = N_EXPERTS - 1 - k
        fb = first_ref[e]

        @pl.when(fb >= 0)
        def _():
            bnext_ref[fb] = nxt
        return jnp.where(fb >= 0, e, nxt)
    lax.fori_loop(0, N_EXPERTS, link, -1)


def _block_tables(counts, n_blocks):
    smem = pl.BlockSpec(memory_space=pltpu.SMEM)
    blk = jax.ShapeDtypeStruct((n_blocks,), jnp.int32)
    return pl.pallas_call(
        functools.partial(_block_table_kernel, n_blocks=n_blocks),
        in_specs=[smem],
        out_specs=[smem] * 6,
        out_shape=[blk] * 5 + [jax.ShapeDtypeStruct((1,), jnp.int32)],
        scratch_shapes=[pltpu.SMEM((N_EXPERTS,), jnp.int32)],
        name="block_tables",
    )(counts)


def _moe_kernel(tok_ref, dst_ref, bexp_ref, bslot_ref, bnext_ref, nused_ref,
                h2p_ref, wg_hbm, wu_hbm, wd_hbm, contrib_hbm,
                xs_ref, ys_ref, wgf_ref, wuf_ref, wdf_ref, wgb_ref, wub_ref, wdb_ref, ssem_ref, wsem_ref,
                *, m_total, half, n_blocks):
    step = pl.program_id(0)
    n_used = nused_ref[0]
    tile = lambda t: pl.ds(pl.multiple_of(t * TILE_ROWS, TILE_ROWS), TILE_ROWS)

    def gather(blk, slt):
        for r in range(MOE_BLOCK):
            xs_ref[slt, pl.ds(r, 1), :] = h2p_ref[pl.ds(tok_ref[blk * MOE_BLOCK + r], 1), :]

    def scatter(blk, slt, r):
        return pltpu.make_async_copy(ys_ref.at[slt, tile(r), :], contrib_hbm.at[tile(dst_ref[blk * MOE_BLOCK + r]), :],
                                     ssem_ref.at[slt])

    def scatter_wait(slt, r):
        pltpu.make_async_copy(ys_ref.at[slt, tile(r), :], contrib_hbm.at[tile(0), :], ssem_ref.at[slt]).wait()

    @pl.when(step == 0)
    def _():
        ys_ref[...] = jnp.zeros_like(ys_ref)
        gap = half - m_total
        fills = [(2 * half + s * MOE_BLOCK, MOE_BLOCK) for s in range(2)]
        fills += [(k * half + m_total, gap) for k in range(TOP_K)] if gap else []
        for start, n in fills:
            fill = pltpu.make_async_copy(ys_ref.at[0, pl.ds(0, n * TILE_ROWS), :],
                                         contrib_hbm.at[pl.ds(start * TILE_ROWS, n * TILE_ROWS), :], ssem_ref.at[0])
            fill.start()
            fill.wait()

    def weight_copies(e, s):
        return [pltpu.make_async_copy(src.at[e], dst.at[s], wsem_ref.at[s])
                for src, dst in ((wg_hbm, wgf_ref), (wu_hbm, wuf_ref), (wd_hbm, wdf_ref))]

    @pl.when(step == 0)
    def _():
        for c in weight_copies(bexp_ref[0], 0):
            c.start()
        gather(0, 0)

    def wait_scatters(slt):
        for r in range(MOE_BLOCK):
            scatter_wait(slt, r)

    def expert_weights(i):
        wslot = bslot_ref[i]

        @pl.when(wslot >= 0)
        def _():
            for c in weight_copies(0, wslot):
                c.wait()

        @pl.when(jnp.logical_and(wslot >= 0, bnext_ref[i] >= 0))
        def _():
            for c in weight_copies(bnext_ref[i], 1 - wslot):
                c.start()

        @pl.when(wslot >= 0)
        def _():
            wgb_ref[...] = wgf_ref[wslot].astype(BF16)
            wub_ref[...] = wuf_ref[wslot].astype(BF16)
            wdb_ref[...] = wdf_ref[wslot].astype(BF16)

    def run_block(i, slt):
        gather(jnp.minimum(i + 1, n_blocks - 1), 1 - slt)
        xb = _unpack_bf16_pairs(xs_ref[slt])
        gate = jnp.dot(xb, wgb_ref[...], preferred_element_type=F32)
        up = jnp.dot(xb, wub_ref[...], preferred_element_type=F32)
        hmid = (jax.nn.silu(gate) * up).astype(BF16)
        _store_token_tiles(ys_ref.at[slt], jnp.dot(hmid, wdb_ref[...], preferred_element_type=F32))
        for r in range(MOE_BLOCK):
            scatter(i, slt, r).start(priority=r % 2)

    for slt in range(2):
        i = 2 * step + slt
        active = i < n_used
        last_active = i == n_used - 1
        pl.when(jnp.logical_and(active, i >= 2))(functools.partial(wait_scatters, slt))
        expert_weights(i)
        pl.when(active)(functools.partial(run_block, i, slt))
        pl.when(last_active)(functools.partial(wait_scatters, slt))
        pl.when(jnp.logical_and(last_active, i >= 1))(functools.partial(wait_scatters, 1 - slt))


def _row_table_kernel(bpos_ref, bcnt_ref, order_ref, tok_ref, dst_ref, *, half, n_blocks):
    rows_per_block = MOE_BLOCK // LANES
    lane = lax.broadcasted_iota(jnp.int32, (rows_per_block, LANES), 1)
    r = lax.broadcasted_iota(jnp.int32, (rows_per_block, LANES), 0) * LANES + lane

    def per_block(b, c):
        p = bpos_ref[b]
        sh = p & (LANES - 1)
        win = order_ref[pl.ds(lax.shift_right_logical(p, LANES_LOG2), rows_per_block + 1), :]
        rolled = pltpu.roll(win, (LANES - sh) & (LANES - 1), axis=1)
        assign = jnp.where(lane < LANES - sh, rolled[:rows_per_block], rolled[1:])
        tok_ref[pl.ds(b * rows_per_block, rows_per_block), :] = jnp.where(assign >= half, assign - half, assign)
        trash = 2 * half + (b & 1) * MOE_BLOCK + r
        dst_ref[pl.ds(b * rows_per_block, rows_per_block), :] = jnp.where(r < bcnt_ref[b], assign, trash)
        return c
    lax.fori_loop(0, n_blocks, per_block, 0, unroll=6)


def _row_tables(order, bpos, bcnt, half):
    n_blocks = bpos.shape[0]
    assert MOE_BLOCK % LANES == 0 and order.shape[0] % LANES == 0
    out = jax.ShapeDtypeStruct((n_blocks * MOE_BLOCK // LANES, LANES), jnp.int32)
    tok, dst = pl.pallas_call(
        functools.partial(_row_table_kernel, half=half, n_blocks=n_blocks),
        grid_spec=pltpu.PrefetchScalarGridSpec(
            num_scalar_prefetch=2, grid=(1,),
            in_specs=[pl.BlockSpec((order.shape[0] // LANES, LANES), lambda i, *_: (0, 0))],
            out_specs=[pl.BlockSpec(out.shape, lambda i, *_: (0, 0))] * 2),
        out_shape=[out, out],
        name="row_tables",
    )(bpos, bcnt, order.reshape(-1, LANES))
    return tok.reshape(-1), dst.reshape(-1)


def _moe_experts(order, tables, h2p, w_gate, w_up, w_down, *, m_total, half):
    bexp, bpos, bcnt, bslot, bnext, n_used = tables
    n_blocks = bexp.shape[0]
    assert 0 <= half - m_total <= MOE_BLOCK and n_blocks % 2 == 0
    tok, dst = _row_tables(order, bpos, bcnt, half)
    tables = (bexp, bslot, bnext, n_used)
    hbm = pl.BlockSpec(memory_space=pl.ANY)
    grid_spec = pltpu.PrefetchScalarGridSpec(
        num_scalar_prefetch=2 + len(tables),
        grid=(n_blocks // 2,),
        in_specs=[_const_spec(h2p.shape), hbm, hbm, hbm],
        out_specs=hbm,
        scratch_shapes=[pltpu.VMEM((2, MOE_BLOCK, D_MODEL // 2), jnp.uint32),
                        pltpu.VMEM((2, MOE_BLOCK * TILE_ROWS, LANES), F32),
                        pltpu.VMEM((2, D_MODEL, D_EXPERT), F32),
                        pltpu.VMEM((2, D_MODEL, D_EXPERT), F32),
                        pltpu.VMEM((2, D_EXPERT, D_MODEL), F32),
                        pltpu.VMEM((D_MODEL, D_EXPERT), BF16),
                        pltpu.VMEM((D_MODEL, D_EXPERT), BF16),
                        pltpu.VMEM((D_EXPERT, D_MODEL), BF16),
                        pltpu.SemaphoreType.DMA((2,)),
                        pltpu.SemaphoreType.DMA((2,))],
    )
    return pl.pallas_call(
        functools.partial(_moe_kernel, m_total=m_total, half=half, n_blocks=n_blocks),
        grid_spec=grid_spec,
        out_shape=jax.ShapeDtypeStruct(((2 * half + 2 * MOE_BLOCK) * TILE_ROWS, LANES), F32),
        compiler_params=pltpu.CompilerParams(dimension_semantics=("arbitrary",),
                                             vmem_limit_bytes=V7X_VMEM_LIMIT_LARGE_BYTES),
        name="moe_experts",
    )(tok, dst, *tables, h2p, w_gate, w_up, w_down)


def _combine_kernel(c0_ref, c1_ref, x2_ref, route_ref, gf_ref, y_ref):
    tc = x2_ref.shape[0]
    w = lax.bitcast_convert_type(route_ref[:, TOP_K:2 * TOP_K], F32)
    moe = w[:, 0:1] * _load_token_tiles(c0_ref, tc) + w[:, 1:2] * _load_token_tiles(c1_ref, tc)
    y_ref[...] = _rms_norm_f32(x2_ref[...] + moe, gf_ref[...])


def _moe_combine(contrib, x2, route, gf, *, row_off, m, tc, half):
    off = row_off // tc
    assert row_off % tc == 0 and half % tc == 0
    ctile = lambda k: pl.BlockSpec((tc * TILE_ROWS, LANES), lambda i: (i + off + k * (half // tc), 0))
    return pl.pallas_call(
        _combine_kernel,
        grid=(m // tc,),
        in_specs=[ctile(0), ctile(1),
                  pl.BlockSpec((tc, D_MODEL), lambda i: (i + off, 0)),
                  pl.BlockSpec((tc, ROUTER_LANES), lambda i: (i + off, 0)),
                  _const_spec((1, D_MODEL))],
        out_specs=pl.BlockSpec((tc, D_MODEL), lambda i: (i, 0)),
        out_shape=jax.ShapeDtypeStruct((m, D_MODEL), F32),
        compiler_params=pltpu.CompilerParams(dimension_semantics=("arbitrary",),
                                             vmem_limit_bytes=V7X_VMEM_LIMIT_BYTES),
        name="moe_combine_prompt" if row_off == 0 else "moe_combine_sample",
    )(contrib, contrib, x2, route, gf)


def _t5_bucket(dist):
    n = jnp.maximum(dist, 0)
    max_exact = N_BUCKETS // 2
    nf = jnp.maximum(n, 1).astype(F32)
    large = max_exact + (jnp.log(nf / max_exact) / math.log(MAX_DISTANCE / max_exact)
                         * (N_BUCKETS - max_exact)).astype(jnp.int32)
    large = jnp.minimum(large, N_BUCKETS - 1)
    return jnp.where(n < max_exact, n, large)


def _bucket_bias(rel_bias, dist, valid):
    buckets = _t5_bucket(dist).reshape(1, -1)
    onehot = (buckets == jnp.arange(N_BUCKETS, dtype=jnp.int32)[:, None]).astype(F32)
    bias = jnp.dot(rel_bias.astype(F32).T, onehot, precision=lax.Precision.HIGHEST)
    return jnp.where(valid.reshape(1, -1), bias, NEG_BIG).reshape((rel_bias.shape[1],) + dist.shape)


def _prompt_bias_table(rel_bias):
    qi = jnp.arange(ATTN_BLOCK, dtype=jnp.int32)[:, None]
    kj = jnp.arange(2 * ATTN_BLOCK, dtype=jnp.int32)[None, :] - ATTN_BLOCK
    dist = qi - kj
    return _bucket_bias(rel_bias, dist, (dist >= 0) & (dist <= WINDOW))


def _sample_bias_table(rel_bias, w_buf):
    dist = w_buf - jnp.arange(w_buf + 1, dtype=jnp.int32)
    return _bucket_bias(rel_bias, dist, dist <= WINDOW)


def kernel(x_prompt, x_sample, cache_conv, cache_k, cache_v, norm1_g, w_in, conv_w, w_conv_out, w_attn_out, w_o, sinks, rel_bias, norm2_g, w_router_group, b_router_group, w_router_expert, b_router_expert, w_e_gate, w_e_up, w_e_down, norm_f_g):
    assert norm1_g.shape[0] == 1, "single-layer configuration"
    batch, seq, _ = x_prompt.shape
    nseq = x_sample.shape[0]
    w_buf = cache_k.shape[2]
    mp = batch * seq
    m_total = mp + nseq
    assert seq % TM_DENSE == 0 and seq % TM_IN_PROJ == 0 and seq % ATTN_BLOCK == 0 and mp % COMBINE_BLOCK == 0
    assert nseq % SAMPLE_SEQ_PER_STEP == 0 and mp % nseq == 0
    assert TOP_K == 2 and MOE_BLOCK == 1 << MOE_BLOCK_LOG2 and m_total * TOP_K < 1 << ASSIGN_BITS
    assert CONV_WIDTH == 3 and LANES == 1 << LANES_LOG2 and x_sample.shape[1] == 1 and w_buf == WINDOW

    g1 = norm1_g[0][None, :]
    g2 = norm2_g[0][None, :]
    gf = norm_f_g[None, :]
    wi = w_in[0].astype(BF16)
    cw = conv_w[0]
    wc = w_conv_out[0].astype(BF16)
    wa = w_attn_out[0].astype(BF16)
    wo = w_o[0].astype(BF16)
    pad_cols = ROUTER_LANES - N_EXPERT_GROUPS - N_EXPERTS
    wr = jnp.concatenate([w_router_group[0], w_router_expert[0],
                          jnp.zeros((D_MODEL, pad_cols), F32)], axis=1).astype(BF16)
    br = jnp.concatenate([b_router_group[0], b_router_expert[0], jnp.zeros((pad_cols,), F32)])[None, :]
    sink = sinks[0].astype(F32)

    xp = x_prompt.reshape(mp, D_MODEL)
    bps = seq // TM_IN_PROJ
    yc_p, q_p, k_p, v_p, sa_p, sb_p, ut_p, kvt_p = _in_proj(
        xp, g1, wi, cw, tm=TM_IN_PROJ, blocks_per_seq=bps, u_tail=CARRY, kv_tail=WINDOW, parts=IN_PROJ_PARTS)
    o_p = _attn_prompt(q_p, k_p, v_p, _prompt_bias_table(rel_bias), sink, batch, seq)

    pad_rows = lambda t: jnp.pad(t, ((0, TM_DENSE - nseq), (0, 0)))
    xs = pad_rows(x_sample.reshape(nseq, D_MODEL))
    hist = (pad_rows(cache_conv[0][:, 0, :]), pad_rows(cache_conv[0][:, 1, :]))
    yc_s, q_s, _, _, sa_s, sb_s, ut_s, kvt_s = _in_proj(
        xs, g1, wi, cw, tm=TM_DENSE, blocks_per_seq=1, u_tail=TM_DENSE, kv_tail=TM_DENSE, hist=hist,
        gate_dtype=F32)
    u_s = ut_s[0, :nseq]
    kv_s = kvt_s[0, :nseq]
    head_mask = (jnp.arange(KV_DIM)[None, :] // HEAD_DIM == jnp.arange(N_HEADS)[:, None] // GROUP)
    qbd = (jnp.tile(q_s[:nseq].reshape(nseq, N_HEADS, HEAD_DIM), (1, 1, N_KV_HEADS))
           * head_mask[None].astype(BF16))
    to_keys_minor = lambda c: jnp.transpose(c.reshape(nseq, w_buf, KV_DIM), (0, 2, 1))
    from_keys_minor = lambda c: jnp.transpose(c, (0, 2, 1)).reshape(1, nseq, w_buf, N_KV_HEADS, HEAD_DIM)
    o_s, kwin_s, vwin_s = _attn_sample(qbd, to_keys_minor(cache_k[0]), to_keys_minor(cache_v[0]), kv_s,
                                       _sample_bias_table(rel_bias, w_buf), sink[:, None],
                                       head_mask.astype(F32))
    o_s = pad_rows(o_s.reshape(nseq, Q_DIM))

    half = -(-m_total // COMBINE_BLOCK) * COMBINE_BLOCK
    assert half % nseq == 0 and TOP_K * half < 1 << ASSIGN_BITS
    x2, h2p, route, route_t, cnt = _out_proj((yc_p, o_p, sa_p, sb_p, xp), (yc_s, o_s, sa_s, sb_s, xs),
                                             wc, wa, wo, g2, wr, br, tm=TM_DENSE, valid_rows_b=nseq, half=half)

    n_assign = m_total * TOP_K
    keys = route_t[0:TOP_K, :m_total].reshape(-1)
    counts = cnt[0, N_EXPERT_GROUPS:N_EXPERT_GROUPS + N_EXPERTS].astype(jnp.int32)
    order = jnp.pad(jnp.sort(keys) & ((1 << ASSIGN_BITS) - 1), (0, MOE_BLOCK))
    n_blocks = -(-n_assign // MOE_BLOCK) + N_EXPERTS
    n_blocks += n_blocks % 2
    tables = _block_tables(counts, n_blocks)
    contrib = _moe_experts(order, tables, h2p, w_e_gate[0], w_e_up[0], w_e_down[0],
                           m_total=m_total, half=half)
    y_p = _moe_combine(contrib, x2, route, gf, row_off=0, m=mp, tc=COMBINE_BLOCK, half=half)
    y_s = _moe_combine(contrib, x2, route, gf, row_off=mp, m=nseq, tc=nseq, half=half)

    y_prompt = y_p.reshape(batch, seq, D_MODEL)
    y_sample = y_s.reshape(nseq, 1, D_MODEL)
    conv_state_prompt = ut_p.reshape(batch, bps, CARRY, D_CONV)[:, -1, CARRY - (CONV_WIDTH - 1):, :][None]
    kv_last = kvt_p.reshape(batch, bps, WINDOW, 2 * KV_DIM)[:, -1]
    k_win_prompt = kv_last[:, :, :KV_DIM].reshape(batch, WINDOW, N_KV_HEADS, HEAD_DIM)[None]
    v_win_prompt = kv_last[:, :, KV_DIM:].reshape(batch, WINDOW, N_KV_HEADS, HEAD_DIM)[None]
    conv_state_sample = jnp.concatenate([cache_conv[0][:, 1:, :], u_s[:, None, :]], axis=1)[None]
    k_win_sample = from_keys_minor(kwin_s)
    v_win_sample = from_keys_minor(vwin_s)
    return (y_prompt, y_sample, conv_state_prompt, k_win_prompt, v_win_prompt,
            conv_state_sample, k_win_sample, v_win_sample)
```
